```python
import jax, jax.numpy as jnp
from jax import lax
import numpy as np

D_MODEL = 2048
BATCH = 8
SEQ = 4096
DEPTH = 1

N_META = 16
POOL_WIDTH = D_MODEL
POOL_WINDOWS = (2, 4, 8, 16)
POOL_GROUPS = len(POOL_WINDOWS)
POOL_GROUP_DIM = POOL_WIDTH // POOL_GROUPS
LRU_WIDTH = D_MODEL
LRU_HEAD_DIM = 256
LRU_HEADS = LRU_WIDTH // LRU_HEAD_DIM
CONV_WIDTH = 4
LRU_C = 8.0
D_FF = 4 * D_MODEL
NORM_EPS = 1e-6
IN_SPLITS = (POOL_WIDTH,
             POOL_WIDTH + LRU_WIDTH,
             POOL_WIDTH + 2 * LRU_WIDTH,
             POOL_WIDTH + 2 * LRU_WIDTH + D_MODEL)
IN_COLS = POOL_WIDTH + 2 * LRU_WIDTH + 2 * D_MODEL

kernel_name = "hybrid_pool_rglru_gated_block"


def rmsnorm(x, g):
    xf = x.astype(jnp.float32)
    y = xf * lax.rsqrt(jnp.mean(xf * xf, axis=-1, keepdims=True) + NORM_EPS)
    return (y * g.astype(jnp.float32)).astype(x.dtype)


def causal_window_mean(v, w):
    T = v.shape[1]
    c = lax.cumsum(v, axis=1)
    c_shift = jnp.pad(c, ((0, 0), (w, 0), (0, 0)))[:, :T]
    cnt = jnp.minimum(jnp.arange(1, T + 1), w).astype(jnp.float32)
    return (c - c_shift) / cnt[None, :, None]


def pool_mixer(v, pool_w, pool_scale):
    B, T, _ = v.shape
    vf = v.astype(jnp.float32)
    diffs = []
    for g, w in enumerate(POOL_WINDOWS):
        vg = vf[..., g * POOL_GROUP_DIM:(g + 1) * POOL_GROUP_DIM]
        diffs.append(causal_window_mean(vg, w) - vg)
    d = jnp.stack(diffs, axis=2)
    y = jnp.einsum('btgc,gcd->btgd', d, pool_w.astype(jnp.float32))
    y = y.reshape(B, T, POOL_WIDTH) * pool_scale.astype(jnp.float32)
    return y.astype(v.dtype)


def causal_depthwise_conv(x, w, b):
    y = lax.conv_general_dilated(
        x, w[:, None, :].astype(x.dtype), window_strides=(1,),
        padding=((CONV_WIDTH - 1, 0),),
        dimension_numbers=('NWC', 'WIO', 'NWC'),
        feature_group_count=x.shape[-1])
    return y + b.astype(x.dtype)


def rg_lru(xc, gate_a_w, gate_a_b, gate_x_w, gate_x_b, lam):
    B, T, W = xc.shape
    xf = xc.astype(jnp.float32)
    xh = xf.reshape(B, T, LRU_HEADS, LRU_HEAD_DIM)
    r = jax.nn.sigmoid(jnp.einsum('bthi,hij->bthj', xh, gate_a_w.astype(jnp.float32))
                       + gate_a_b.astype(jnp.float32)).reshape(B, T, W)
    i = jax.nn.sigmoid(jnp.einsum('bthi,hij->bthj', xh, gate_x_w.astype(jnp.float32))
                       + gate_x_b.astype(jnp.float32)).reshape(B, T, W)
    log_a = -LRU_C * r * jax.nn.softplus(-lam.astype(jnp.float32))
    a = jnp.exp(log_a)
    mult = jnp.sqrt(-jnp.expm1(2.0 * log_a))
    bt = mult * (i * xf)

    def combine(left, right):
        a1, b1 = left
        a2, b2 = right
        return a1 * a2, a2 * b1 + b2

    _, h = lax.associative_scan(combine, (a, bt), axis=1)
    return h.astype(xc.dtype)


def _fwd_setup_inputs(seed: int = 0) -> dict:
    key = jax.random.key(seed)
    ks = jax.random.split(key, 20)
    f32 = jnp.float32
    x = jax.random.normal(ks[0], (BATCH, SEQ, D_MODEL), f32)
    meta_tokens = jax.random.normal(ks[1], (N_META, D_MODEL), f32)
    norm1_g = 1.0 + 0.02 * jax.random.normal(ks[2], (DEPTH, D_MODEL), f32)
    w_in = jax.random.normal(ks[3], (DEPTH, D_MODEL, IN_COLS), f32) * D_MODEL ** -0.5
    pool_w = jax.random.normal(ks[4], (DEPTH, POOL_GROUPS, POOL_GROUP_DIM, POOL_GROUP_DIM), f32) * POOL_GROUP_DIM ** -0.5
    pool_scale = 1.0 + 0.02 * jax.random.normal(ks[5], (DEPTH, POOL_WIDTH), f32)
    conv_w = jax.random.normal(ks[6], (DEPTH, CONV_WIDTH, LRU_WIDTH), f32) * CONV_WIDTH ** -0.5
    conv_b = 0.01 * jax.random.normal(ks[7], (DEPTH, LRU_WIDTH), f32)
    gate_a_w = jax.random.normal(ks[8], (DEPTH, LRU_HEADS, LRU_HEAD_DIM, LRU_HEAD_DIM), f32) * LRU_HEAD_DIM ** -0.5
    gate_a_b = 0.01 * jax.random.normal(ks[9], (DEPTH, LRU_HEADS, LRU_HEAD_DIM), f32)
    gate_x_w = jax.random.normal(ks[10], (DEPTH, LRU_HEADS, LRU_HEAD_DIM, LRU_HEAD_DIM), f32) * LRU_HEAD_DIM ** -0.5
    gate_x_b = 0.01 * jax.random.normal(ks[11], (DEPTH, LRU_HEADS, LRU_HEAD_DIM), f32)
    u = jax.random.uniform(ks[12], (DEPTH, LRU_WIDTH), f32, minval=0.9, maxval=0.999)
    s = u ** (1.0 / LRU_C)
    lru_lambda = jnp.log(s) - jnp.log1p(-s)
    w_out = jax.random.normal(ks[13], (DEPTH, D_MODEL, D_MODEL), f32) * D_MODEL ** -0.5
    norm2_g = 1.0 + 0.02 * jax.random.normal(ks[14], (DEPTH, D_MODEL), f32)
    mlp_w1 = jax.random.normal(ks[15], (DEPTH, D_MODEL, D_FF), f32) * D_MODEL ** -0.5
    mlp_w2 = jax.random.normal(ks[16], (DEPTH, D_FF, D_MODEL), f32) * D_FF ** -0.5
    final_g = 1.0 + 0.02 * jax.random.normal(ks[17], (D_MODEL,), f32)
    return {"x": x, "meta_tokens": meta_tokens, "norm1_g": norm1_g, "w_in": w_in,
            "pool_w": pool_w, "pool_scale": pool_scale, "conv_w": conv_w, "conv_b": conv_b,
            "gate_a_w": gate_a_w, "gate_a_b": gate_a_b, "gate_x_w": gate_x_w, "gate_x_b": gate_x_b,
            "lru_lambda": lru_lambda, "w_out": w_out, "norm2_g": norm2_g,
            "mlp_w1": mlp_w1, "mlp_w2": mlp_w2, "final_g": final_g}


def _fwd_reference(x, meta_tokens, norm1_g, w_in, pool_w, pool_scale, conv_w, conv_b,
              gate_a_w, gate_a_b, gate_x_w, gate_x_b, lru_lambda, w_out, norm2_g,
              mlp_w1, mlp_w2, final_g):
    B = x.shape[0]
    meta = jnp.broadcast_to(meta_tokens[None].astype(x.dtype), (B, N_META, x.shape[-1]))
    h = jnp.concatenate([meta, x], axis=1)
    for l in range(DEPTH):
        u = rmsnorm(h, norm1_g[l])
        proj = u @ w_in[l]
        v_pool, v_lru, v_gelu, g_pool, g_lru = jnp.split(proj, IN_SPLITS, axis=-1)
        pool_out = pool_mixer(v_pool, pool_w[l], pool_scale[l])
        xc = causal_depthwise_conv(v_lru, conv_w[l], conv_b[l])
        lru_out = rg_lru(xc, gate_a_w[l], gate_a_b[l], gate_x_w[l], gate_x_b[l],
                         lru_lambda[l]) * jax.nn.gelu(v_gelu)
        merged = jax.nn.sigmoid(g_pool) * pool_out + jax.nn.sigmoid(g_lru) * lru_out
        h = h + merged @ w_out[l]
        u2 = rmsnorm(h, norm2_g[l])
        h = h + jnp.square(jax.nn.relu(u2 @ mlp_w1[l])) @ mlp_w2[l]
    out = rmsnorm(h, final_g)
    return out[:, N_META:]


import jax as _jax
import jax.numpy as _jnp

TWIN_FORMAT = 'train_step'
FWD_PARAMS = ['x', 'meta_tokens', 'norm1_g', 'w_in', 'pool_w', 'pool_scale', 'conv_w', 'conv_b', 'gate_a_w', 'gate_a_b', 'gate_x_w', 'gate_x_b', 'lru_lambda', 'w_out', 'norm2_g', 'mlp_w1', 'mlp_w2', 'final_g']
TWIN_WEIGHTS = ['meta_tokens', 'norm1_g', 'w_in', 'pool_w', 'pool_scale', 'conv_w', 'conv_b', 'gate_a_w', 'gate_a_b', 'gate_x_w', 'gate_x_b', 'lru_lambda', 'w_out', 'norm2_g', 'mlp_w1', 'mlp_w2', 'final_g']
TWIN_DIFF_INPUT = 'x'
TWIN_INPUTS = ['x', 'meta_tokens', 'norm1_g', 'w_in', 'pool_w', 'pool_scale', 'conv_w', 'conv_b', 'gate_a_w', 'gate_a_b', 'gate_x_w', 'gate_x_b', 'lru_lambda', 'w_out', 'norm2_g', 'mlp_w1', 'mlp_w2', 'final_g', 'loss_target', 'm_meta_tokens', 'm_norm1_g', 'm_w_in', 'm_pool_w', 'm_pool_scale', 'm_conv_w', 'm_conv_b', 'm_gate_a_w', 'm_gate_a_b', 'm_gate_x_w', 'm_gate_x_b', 'm_lru_lambda', 'm_w_out', 'm_norm2_g', 'm_mlp_w1', 'm_mlp_w2', 'm_final_g', 'v_meta_tokens', 'v_norm1_g', 'v_w_in', 'v_pool_w', 'v_pool_scale', 'v_conv_w', 'v_conv_b', 'v_gate_a_w', 'v_gate_a_b', 'v_gate_x_w', 'v_gate_x_b', 'v_lru_lambda', 'v_w_out', 'v_norm2_g', 'v_mlp_w1', 'v_mlp_w2', 'v_final_g']
TWIN_OUTPUTS = ['loss', 'grad_x', 'grad_meta_tokens', 'grad_norm1_g', 'grad_w_in', 'grad_pool_w', 'grad_pool_scale', 'grad_conv_w', 'grad_conv_b', 'grad_gate_a_w', 'grad_gate_a_b', 'grad_gate_x_w', 'grad_gate_x_b', 'grad_lru_lambda', 'grad_w_out', 'grad_norm2_g', 'grad_mlp_w1', 'grad_mlp_w2', 'grad_final_g', 'delta_meta_tokens', 'delta_norm1_g', 'delta_w_in', 'delta_pool_w', 'delta_pool_scale', 'delta_conv_w', 'delta_conv_b', 'delta_gate_a_w', 'delta_gate_a_b', 'delta_gate_x_w', 'delta_gate_x_b', 'delta_lru_lambda', 'delta_w_out', 'delta_norm2_g', 'delta_mlp_w1', 'delta_mlp_w2', 'delta_final_g', 'new_m_meta_tokens', 'new_m_norm1_g', 'new_m_w_in', 'new_m_pool_w', 'new_m_pool_scale', 'new_m_conv_w', 'new_m_conv_b', 'new_m_gate_a_w', 'new_m_gate_a_b', 'new_m_gate_x_w', 'new_m_gate_x_b', 'new_m_lru_lambda', 'new_m_w_out', 'new_m_norm2_g', 'new_m_mlp_w1', 'new_m_mlp_w2', 'new_m_final_g', 'new_v_meta_tokens', 'new_v_norm1_g', 'new_v_w_in', 'new_v_pool_w', 'new_v_pool_scale', 'new_v_conv_w', 'new_v_conv_b', 'new_v_gate_a_w', 'new_v_gate_a_b', 'new_v_gate_x_w', 'new_v_gate_x_b', 'new_v_lru_lambda', 'new_v_w_out', 'new_v_norm2_g', 'new_v_mlp_w1', 'new_v_mlp_w2', 'new_v_final_g']
TWIN_LEAF_KINDS = {'loss': 'loss', 'grad_x': 'grad_x', 'grad_meta_tokens': 'grad_w', 'grad_norm1_g': 'grad_w', 'grad_w_in': 'grad_w', 'grad_pool_w': 'grad_w', 'grad_pool_scale': 'grad_w', 'grad_conv_w': 'grad_w', 'grad_conv_b': 'grad_w', 'grad_gate_a_w': 'grad_w', 'grad_gate_a_b': 'grad_w', 'grad_gate_x_w': 'grad_w', 'grad_gate_x_b': 'grad_w', 'grad_lru_lambda': 'grad_w', 'grad_w_out': 'grad_w', 'grad_norm2_g': 'grad_w', 'grad_mlp_w1': 'grad_w', 'grad_mlp_w2': 'grad_w', 'grad_final_g': 'grad_w', 'delta_meta_tokens': 'delta_w', 'delta_norm1_g': 'delta_w', 'delta_w_in': 'delta_w', 'delta_pool_w': 'delta_w', 'delta_pool_scale': 'delta_w', 'delta_conv_w': 'delta_w', 'delta_conv_b': 'delta_w', 'delta_gate_a_w': 'delta_w', 'delta_gate_a_b': 'delta_w', 'delta_gate_x_w': 'delta_w', 'delta_gate_x_b': 'delta_w', 'delta_lru_lambda': 'delta_w', 'delta_w_out': 'delta_w', 'delta_norm2_g': 'delta_w', 'delta_mlp_w1': 'delta_w', 'delta_mlp_w2': 'delta_w', 'delta_final_g': 'delta_w', 'new_m_meta_tokens': 'new_m', 'new_m_norm1_g': 'new_m', 'new_m_w_in': 'new_m', 'new_m_pool_w': 'new_m', 'new_m_pool_scale': 'new_m', 'new_m_conv_w': 'new_m', 'new_m_conv_b': 'new_m', 'new_m_gate_a_w': 'new_m', 'new_m_gate_a_b': 'new_m', 'new_m_gate_x_w': 'new_m', 'new_m_gate_x_b': 'new_m', 'new_m_lru_lambda': 'new_m', 'new_m_w_out': 'new_m', 'new_m_norm2_g': 'new_m', 'new_m_mlp_w1': 'new_m', 'new_m_mlp_w2': 'new_m', 'new_m_final_g': 'new_m', 'new_v_meta_tokens': 'new_v', 'new_v_norm1_g': 'new_v', 'new_v_w_in': 'new_v', 'new_v_pool_w': 'new_v', 'new_v_pool_scale': 'new_v', 'new_v_conv_w': 'new_v', 'new_v_conv_b': 'new_v', 'new_v_gate_a_w': 'new_v', 'new_v_gate_a_b': 'new_v', 'new_v_gate_x_w': 'new_v', 'new_v_gate_x_b': 'new_v', 'new_v_lru_lambda': 'new_v', 'new_v_w_out': 'new_v', 'new_v_norm2_g': 'new_v', 'new_v_mlp_w1': 'new_v', 'new_v_mlp_w2': 'new_v', 'new_v_final_g': 'new_v'}


def _forward(args):
    return _fwd_reference(*[args[k] for k in FWD_PARAMS])


def _output_shape():
    def fwd():
        inp = _fwd_setup_inputs(0)
        return _fwd_reference(*[inp[k] for k in FWD_PARAMS])
    out = _jax.eval_shape(fwd)
    return out.shape, out.dtype

N_MICROBATCH = 1
ADAM_LR = 0.001
ADAM_B1 = 0.9
ADAM_B2 = 0.999
ADAM_EPS = 1e-08
ADAM_WD = 0.01
ADAM_STEP = 10
PER_EXAMPLE_BATCH_AXIS = {'x': 0, 'loss_target': 0}
SHARED_INPUTS = []
_WEIGHT_DTYPES = {'meta_tokens': _jnp.float32, 'norm1_g': _jnp.float32, 'w_in': _jnp.float32, 'pool_w': _jnp.float32, 'pool_scale': _jnp.float32, 'conv_w': _jnp.float32, 'conv_b': _jnp.float32, 'gate_a_w': _jnp.float32, 'gate_a_b': _jnp.float32, 'gate_x_w': _jnp.float32, 'gate_x_b': _jnp.float32, 'lru_lambda': _jnp.float32, 'w_out': _jnp.float32, 'norm2_g': _jnp.float32, 'mlp_w1': _jnp.float32, 'mlp_w2': _jnp.float32, 'final_g': _jnp.float32}
MOMENT_SCALE = {'meta_tokens': 1.074353e-03, 'norm1_g': 5.270506e-02, 'w_in': 2.301957e-02, 'pool_w': 4.066961e-02, 'pool_scale': 4.236961e-02, 'conv_w': 1.958468e-02, 'conv_b': 2.131179e-01, 'gate_a_w': 5.130902e-03, 'gate_a_b': 4.430772e-03, 'gate_x_w': 9.047688e-03, 'gate_x_b': 7.427935e-03, 'lru_lambda': 9.460449e-03, 'w_out': 4.462735e-02, 'norm2_g': 7.697156e-02, 'mlp_w1': 3.770710e-02, 'mlp_w2': 7.705371e-02, 'final_g': 1.611941e+01}


def _to_microbatches(a, axis):
    t = _jnp.moveaxis(a, axis, 0)
    t = t.reshape((N_MICROBATCH, t.shape[0] // N_MICROBATCH) + t.shape[1:])
    return _jnp.moveaxis(t, 1, axis + 1)


def setup_inputs(seed: int = 0) -> dict:
    inp = _fwd_setup_inputs(seed)
    key = _jax.random.fold_in(_jax.random.key(seed), 7919)
    shape, _ = _output_shape()
    out = dict(inp)
    out["loss_target"] = _jax.random.normal(_jax.random.fold_in(key, 0), shape, _jnp.float32)
    for i, name in enumerate(TWIN_WEIGHTS):
        w = inp[name].astype(_jnp.float32)
        if MOMENT_SCALE is None:
            s = _jnp.sqrt(_jnp.mean(_jnp.square(w)) + 1e-30)
        else:
            s = MOMENT_SCALE[name]
        km, kv = _jax.random.split(_jax.random.fold_in(key, i + 1))
        out[name] = w
        out["m_" + name] = s * _jax.random.normal(km, w.shape, _jnp.float32)
        out["v_" + name] = (s * s) * _jax.random.uniform(kv, w.shape, _jnp.float32, 0.5, 1.5)
    if N_MICROBATCH > 1:
        for name, axis in PER_EXAMPLE_BATCH_AXIS.items():
            out[name] = _to_microbatches(out[name], axis)
    return {'x': out['x'], 'meta_tokens': out['meta_tokens'], 'norm1_g': out['norm1_g'], 'w_in': out['w_in'], 'pool_w': out['pool_w'], 'pool_scale': out['pool_scale'], 'conv_w': out['conv_w'], 'conv_b': out['conv_b'], 'gate_a_w': out['gate_a_w'], 'gate_a_b': out['gate_a_b'], 'gate_x_w': out['gate_x_w'], 'gate_x_b': out['gate_x_b'], 'lru_lambda': out['lru_lambda'], 'w_out': out['w_out'], 'norm2_g': out['norm2_g'], 'mlp_w1': out['mlp_w1'], 'mlp_w2': out['mlp_w2'], 'final_g': out['final_g'], 'loss_target': out['loss_target'], 'm_meta_tokens': out['m_meta_tokens'], 'm_norm1_g': out['m_norm1_g'], 'm_w_in': out['m_w_in'], 'm_pool_w': out['m_pool_w'], 'm_pool_scale': out['m_pool_scale'], 'm_conv_w': out['m_conv_w'], 'm_conv_b': out['m_conv_b'], 'm_gate_a_w': out['m_gate_a_w'], 'm_gate_a_b': out['m_gate_a_b'], 'm_gate_x_w': out['m_gate_x_w'], 'm_gate_x_b': out['m_gate_x_b'], 'm_lru_lambda': out['m_lru_lambda'], 'm_w_out': out['m_w_out'], 'm_norm2_g': out['m_norm2_g'], 'm_mlp_w1': out['m_mlp_w1'], 'm_mlp_w2': out['m_mlp_w2'], 'm_final_g': out['m_final_g'], 'v_meta_tokens': out['v_meta_tokens'], 'v_norm1_g': out['v_norm1_g'], 'v_w_in': out['v_w_in'], 'v_pool_w': out['v_pool_w'], 'v_pool_scale': out['v_pool_scale'], 'v_conv_w': out['v_conv_w'], 'v_conv_b': out['v_conv_b'], 'v_gate_a_w': out['v_gate_a_w'], 'v_gate_a_b': out['v_gate_a_b'], 'v_gate_x_w': out['v_gate_x_w'], 'v_gate_x_b': out['v_gate_x_b'], 'v_lru_lambda': out['v_lru_lambda'], 'v_w_out': out['v_w_out'], 'v_norm2_g': out['v_norm2_g'], 'v_mlp_w1': out['v_mlp_w1'], 'v_mlp_w2': out['v_mlp_w2'], 'v_final_g': out['v_final_g']}


def _loss(weights, diff, rest, loss_target):
    with _jax.named_scope("forward"):
        args = {**rest, TWIN_DIFF_INPUT: diff, **{k: w.astype(_WEIGHT_DTYPES[k]) for k, w in weights.items()}}
        y = _forward(args)
    with _jax.named_scope("loss_head"):
        err = _jnp.square(y.astype(_jnp.float32) - loss_target)
        return 0.5 * _jnp.sum(_jnp.mean(err, axis=-1)) if err.ndim else 0.5 * err


def _adamw(w, g, m, v):
    m = ADAM_B1 * m + (1.0 - ADAM_B1) * g
    v = ADAM_B2 * v + (1.0 - ADAM_B2) * _jnp.square(g)
    m_hat = m / (1.0 - ADAM_B1 ** ADAM_STEP)
    v_hat = v / (1.0 - ADAM_B2 ** ADAM_STEP)
    delta = -ADAM_LR * (m_hat / (_jnp.sqrt(v_hat) + ADAM_EPS) + ADAM_WD * w)
    return delta, m, v


def reference(x, meta_tokens, norm1_g, w_in, pool_w, pool_scale, conv_w, conv_b, gate_a_w, gate_a_b, gate_x_w, gate_x_b, lru_lambda, w_out, norm2_g, mlp_w1, mlp_w2, final_g, loss_target, m_meta_tokens, m_norm1_g, m_w_in, m_pool_w, m_pool_scale, m_conv_w, m_conv_b, m_gate_a_w, m_gate_a_b, m_gate_x_w, m_gate_x_b, m_lru_lambda, m_w_out, m_norm2_g, m_mlp_w1, m_mlp_w2, m_final_g, v_meta_tokens, v_norm1_g, v_w_in, v_pool_w, v_pool_scale, v_conv_w, v_conv_b, v_gate_a_w, v_gate_a_b, v_gate_x_w, v_gate_x_b, v_lru_lambda, v_w_out, v_norm2_g, v_mlp_w1, v_mlp_w2, v_final_g):
    given = dict(x=x, meta_tokens=meta_tokens, norm1_g=norm1_g, w_in=w_in, pool_w=pool_w, pool_scale=pool_scale, conv_w=conv_w, conv_b=conv_b, gate_a_w=gate_a_w, gate_a_b=gate_a_b, gate_x_w=gate_x_w, gate_x_b=gate_x_b, lru_lambda=lru_lambda, w_out=w_out, norm2_g=norm2_g, mlp_w1=mlp_w1, mlp_w2=mlp_w2, final_g=final_g, loss_target=loss_target, m_meta_tokens=m_meta_tokens, m_norm1_g=m_norm1_g, m_w_in=m_w_in, m_pool_w=m_pool_w, m_pool_scale=m_pool_scale, m_conv_w=m_conv_w, m_conv_b=m_conv_b, m_gate_a_w=m_gate_a_w, m_gate_a_b=m_gate_a_b, m_gate_x_w=m_gate_x_w, m_gate_x_b=m_gate_x_b, m_lru_lambda=m_lru_lambda, m_w_out=m_w_out, m_norm2_g=m_norm2_g, m_mlp_w1=m_mlp_w1, m_mlp_w2=m_mlp_w2, m_final_g=m_final_g, v_meta_tokens=v_meta_tokens, v_norm1_g=v_norm1_g, v_w_in=v_w_in, v_pool_w=v_pool_w, v_pool_scale=v_pool_scale, v_conv_w=v_conv_w, v_conv_b=v_conv_b, v_gate_a_w=v_gate_a_w, v_gate_a_b=v_gate_a_b, v_gate_x_w=v_gate_x_w, v_gate_x_b=v_gate_x_b, v_lru_lambda=v_lru_lambda, v_w_out=v_w_out, v_norm2_g=v_norm2_g, v_mlp_w1=v_mlp_w1, v_mlp_w2=v_mlp_w2, v_final_g=v_final_g)
    weights = {n: given[n] for n in TWIN_WEIGHTS}
    shared = {n: given[n] for n in SHARED_INPUTS}
    per_example = {n: given[n] for n in ['x']}
    grad_fn = _jax.value_and_grad(_loss, argnums=(0, 1))

    def one_microbatch(ex, loss_target):
        ex = dict(ex)
        diff = ex.pop(TWIN_DIFF_INPUT)
        return grad_fn(weights, diff, {**shared, **ex}, loss_target)

    if N_MICROBATCH == 1:
        loss, (grad_w, grad_x) = one_microbatch(per_example, given["loss_target"])
    else:
        def body(carry, xs):
            loss_sum, grad_sum = carry
            l_k, (gw_k, gx_k) = one_microbatch(xs[0], xs[1])
            with _jax.named_scope("update"):
                return (loss_sum + l_k, _jax.tree.map(_jnp.add, grad_sum, gw_k)), gx_k

        init = (_jnp.zeros((), _jnp.float32), _jax.tree.map(_jnp.zeros_like, weights))
        (loss, grad_w), grad_x = _jax.lax.scan(body, init, (per_example, given["loss_target"]))
    with _jax.named_scope("update"):
        delta_w, new_m, new_v = {}, {}, {}
        for n in TWIN_WEIGHTS:
            delta_w[n], new_m[n], new_v[n] = _adamw(weights[n], grad_w[n], given["m_" + n], given["v_" + n])
    return (loss, grad_x, *[grad_w[n] for n in TWIN_WEIGHTS], *[delta_w[n] for n in TWIN_WEIGHTS],
            *[new_m[n] for n in TWIN_WEIGHTS], *[new_v[n] for n in TWIN_WEIGHTS])
```

```python
import functools

import jax
import jax.numpy as jnp
from jax import lax
from jax.experimental import pallas as pl
from jax.experimental.pallas import tpu as pltpu

F32 = jnp.float32
BF16 = jnp.bfloat16
MESH = pl.DeviceIdType.MESH

NORM_EPS = 1e-6
N_META = 16
HEAD_DIM = 256
POOL_WINDOWS = (2, 4, 8, 16)
LRU_C = 8.0
ROW_TILE = 128
HIST = 16
VMEM_LIMIT_BYTES = 56 * 1024 * 1024
ADAM_LR, ADAM_B1, ADAM_B2, ADAM_EPS, ADAM_WD, ADAM_STEP = 0.001, 0.9, 0.999, 1e-08, 0.01, 10


def _pick(n, prefs):
    for p in prefs:
        if n % p == 0:
            return p
    return n


def _params(*sem):
    return pltpu.CompilerParams(dimension_semantics=sem, vmem_limit_bytes=VMEM_LIMIT_BYTES)


def _sigmoid(x):
    return 1.0 / (1.0 + jnp.exp(-x))


def _gelu_tanh(x):
    t = jnp.tanh(0.7978845608028654 * (x + 0.044715 * (x * x * x)))
    return 0.5 * x * (1.0 + t), t


def _gelu_tanh_grad(x, t):
    return 0.5 * (1.0 + t) + 0.5 * x * (1.0 - t * t) * (0.7978845608028654 * (1.0 + 3.0 * 0.044715 * x * x))


def _neg_expm1(x):
    series = -x * (1.0 + x * 0.5 * (1.0 + x * (1.0 / 3.0) * (1.0 + x * 0.25 * (1.0 + x * 0.2 * (1.0 + x * (1.0 / 6.0) * (1.0 + x * (1.0 / 7.0)))))))
    return jnp.where(x > -0.3, series, 1.0 - jnp.exp(x))


def _softplus_neg(lam):
    z = jnp.exp(-jnp.abs(lam))
    log1p_z = jnp.where(z < 0.01, z * (1.0 - z * (0.5 - z * (1.0 / 3.0))), jnp.log(1.0 + z))
    return jnp.maximum(-lam, 0.0) + log1p_z


def _tile_masks(is_meta, rows):
    row = lax.broadcasted_iota(jnp.int32, (rows, 1), 0)
    valid = jnp.logical_or(jnp.logical_not(is_meta), row >= rows - N_META)
    t_log = jnp.where(is_meta, row - (rows - N_META), 1 << 20)
    return row, valid, t_log


def _window_count(t_log, w):
    return jnp.clip(t_log + 1, 1, w).astype(F32)


def _dot_nt(a, b):
    return lax.dot_general(a, b, (((1,), (1,)), ((), ())), preferred_element_type=F32)


def _dot_tn(a, b):
    return lax.dot_general(a, b, (((0,), (0,)), ((), ())), preferred_element_type=F32)


def _matmul(kind, a, b, *, grid, a_spec, b_spec, out_spec, out_shape, acc_shape, name,
            a_pro=None, epilogue=None, extras=(), extra_specs=()):
    nk = grid[2]
    n_extra = len(extras)

    def body(a_ref, b_ref, *rest):
        extra_refs = rest[:n_extra]
        o_ref = rest[n_extra]
        av = a_ref[...]
        if a_pro is not None:
            av = a_pro(av)
        av = av.astype(BF16)
        bv = b_ref[...].astype(BF16)
        if kind == "nn":
            p = jnp.dot(av, bv, preferred_element_type=F32)
        elif kind == "nt":
            p = _dot_nt(av, bv)
        else:
            p = _dot_tn(av, bv)

        def finish(r):
            if epilogue is not None:
                r = epilogue(r, *[e[...] for e in extra_refs])
            o_ref[...] = r.astype(o_ref.dtype)

        if nk == 1:
            finish(p)
        else:
            acc_ref = rest[n_extra + 1]
            k = pl.program_id(2)

            @pl.when(k == 0)
            def _():
                acc_ref[...] = p

            @pl.when(k > 0)
            def _():
                acc_ref[...] += p

            @pl.when(k == nk - 1)
            def _():
                finish(acc_ref[...])

    scratch = [] if nk == 1 else [pltpu.VMEM(acc_shape, F32)]
    return pl.pallas_call(
        body, grid=grid, in_specs=[a_spec, b_spec, *extra_specs], out_specs=out_spec, out_shape=out_shape,
        scratch_shapes=scratch, name=name, compiler_params=_params("parallel", "parallel", "arbitrary"),
    )(a, b, *extras)


def mm_nn(a, b, *, out_dtype, name, a_pro=None, epilogue=None, extras=()):
    M, K = a.shape
    sharded = b.ndim == 3
    ns = b.shape[2] if sharded else b.shape[1]
    N = ns * b.shape[0] if sharded else ns
    tm = _pick(M, (1408, 512, 256, 128))
    tn = _pick(ns, (512, 256, 128))
    tk = _pick(K, (2048, 1024, 512, 256, 128))
    per = ns // tn
    if sharded:
        b_spec = pl.BlockSpec((None, tk, tn), lambda i, j, k: (j // per, k, j % per))
    else:
        b_spec = pl.BlockSpec((tk, tn), lambda i, j, k: (k, j))
    mn = pl.BlockSpec((tm, tn), lambda i, j, k: (i, j))
    return _matmul("nn", a, b, grid=(M // tm, N // tn, K // tk), a_spec=pl.BlockSpec((tm, tk), lambda i, j, k: (i, k)),
                   b_spec=b_spec, out_spec=mn, out_shape=jax.ShapeDtypeStruct((M, N), out_dtype), acc_shape=(tm, tn),
                   name=name, a_pro=a_pro, epilogue=epilogue, extras=extras, extra_specs=[mn] * len(extras))


def mm_nt(a, w, *, out_dtype, name, epilogue=None, extras=()):
    M, N = a.shape
    sharded = w.ndim == 3
    kw = w.shape[1] if sharded else w.shape[0]
    ns = w.shape[2] if sharded else w.shape[1]
    tm = _pick(M, (1408, 512, 256, 128))
    tkw = _pick(kw, (1024, 512, 256, 128))
    tk = _pick(ns, (2048, 1280, 1024, 512, 256, 128))
    per = ns // tk
    if sharded:
        w_spec = pl.BlockSpec((None, tkw, tk), lambda i, j, k: (k // per, j, k % per))
    else:
        w_spec = pl.BlockSpec((tkw, tk), lambda i, j, k: (j, k))
    mo = pl.BlockSpec((tm, tkw), lambda i, j, k: (i, j))
    return _matmul("nt", a, w, grid=(M // tm, kw // tkw, N // tk), a_spec=pl.BlockSpec((tm, tk), lambda i, j, k: (i, k)),
                   b_spec=w_spec, out_spec=mo, out_shape=jax.ShapeDtypeStruct((M, kw), out_dtype), acc_shape=(tm, tkw),
                   name=name, epilogue=epilogue, extras=extras, extra_specs=[mo] * len(extras))


def mm_tn(a, g, *, shards, name, a_pro=None):
    T, kw = a.shape
    N = g.shape[1]
    ns = N // shards
    tt = _pick(T, (1408, 512, 256, 128))
    tkw = _pick(kw, (1024, 512, 256, 128))
    tn = _pick(ns, (1280, 1024, 512, 256, 128))
    per = ns // tn
    if shards > 1:
        out_spec = pl.BlockSpec((None, tkw, tn), lambda i, j, k: (j // per, i, j % per))
        out_shape = jax.ShapeDtypeStruct((shards, kw, ns), F32)
    else:
        out_spec = pl.BlockSpec((tkw, tn), lambda i, j, k: (i, j))
        out_shape = jax.ShapeDtypeStruct((kw, N), F32)
    return _matmul("tn", a, g, grid=(kw // tkw, N // tn, T // tt), a_spec=pl.BlockSpec((tt, tkw), lambda i, j, k: (k, i)),
                   b_spec=pl.BlockSpec((tt, tn), lambda i, j, k: (k, j)), out_spec=out_spec, out_shape=out_shape,
                   acc_shape=(tkw, tn), name=name, a_pro=a_pro)


def _relu_sq(a):
    r = jnp.maximum(a.astype(F32), 0.0)
    return r * r


def rmsnorm_fwd(h, g, *, name):
    Tp, D = h.shape
    tr = _pick(Tp, (384, 256, 128))

    def body(h_ref, g_ref, u_ref):
        x = h_ref[...]
        r = lax.rsqrt(jnp.mean(x * x, axis=-1, keepdims=True) + NORM_EPS)
        u_ref[...] = ((x * r) * g_ref[...]).astype(BF16)

    row = pl.BlockSpec((tr, D), lambda i: (i, 0))
    return pl.pallas_call(body, grid=(Tp // tr,), in_specs=[row, pl.BlockSpec((1, D), lambda i: (0, 0))], out_specs=row,
                          out_shape=jax.ShapeDtypeStruct((Tp, D), BF16), name=name, compiler_params=_params("parallel"))(h, g)


def _rms_bwd_math(x, g, dy):
    r = lax.rsqrt(jnp.mean(x * x, axis=-1, keepdims=True) + NORM_EPS)
    xh = x * r
    dyg = dy * g
    dx = r * (dyg - xh * jnp.mean(dyg * xh, axis=-1, keepdims=True))
    return dx, xh


def final_loss(h2, target, gf, *, name):
    Tp, D = h2.shape
    nt = Tp // ROW_TILE

    def body(h_ref, t_ref, g_ref, dh_ref, dhb_ref, st_ref):
        i = pl.program_id(0)

        @pl.when(i == 0)
        def _():
            st_ref[...] = jnp.zeros_like(st_ref)

        x = h_ref[...]
        g = g_ref[...]
        r = lax.rsqrt(jnp.mean(x * x, axis=-1, keepdims=True) + NORM_EPS)
        xh = x * r
        err = jnp.where(i == nt - 1, 0.0, xh * g - t_ref[...])
        dout = err * (1.0 / D)
        dyg = dout * g
        dx = r * (dyg - xh * jnp.mean(dyg * xh, axis=-1, keepdims=True))
        dh_ref[...] = dx
        dhb_ref[...] = dx.astype(BF16)
        st_ref[0:1, :] += jnp.sum(dout * xh, axis=0, keepdims=True)
        st_ref[1:2, :] += jnp.sum(err * err, axis=0, keepdims=True) * (0.5 / D)

    row = pl.BlockSpec((ROW_TILE, D), lambda i: (i, 0))
    return pl.pallas_call(
        body, grid=(nt,),
        in_specs=[row, pl.BlockSpec((ROW_TILE, D), lambda i: (jnp.minimum(i, nt - 2), 0)), pl.BlockSpec((1, D), lambda i: (0, 0))],
        out_specs=[row, row, pl.BlockSpec((8, D), lambda i: (0, 0))],
        out_shape=[jax.ShapeDtypeStruct((Tp, D), F32), jax.ShapeDtypeStruct((Tp, D), BF16), jax.ShapeDtypeStruct((8, D), F32)],
        name=name, compiler_params=_params("arbitrary"))(h2, target, gf)


def rms_bwd(h, g, du, dres, *, name):
    Tp, D = h.shape
    tr = _pick(Tp, (384, 256, 128))

    def body(h_ref, g_ref, du_ref, dr_ref, dh_ref, dhb_ref, st_ref):
        @pl.when(pl.program_id(0) == 0)
        def _():
            st_ref[...] = jnp.zeros_like(st_ref)

        du_v = du_ref[...].astype(F32)
        dx, xh = _rms_bwd_math(h_ref[...], g_ref[...], du_v)
        dh = dr_ref[...] + dx
        dh_ref[...] = dh
        dhb_ref[...] = dh.astype(BF16)
        st_ref[0:1, :] += jnp.sum(du_v * xh, axis=0, keepdims=True)

    row = pl.BlockSpec((tr, D), lambda i: (i, 0))
    return pl.pallas_call(
        body, grid=(Tp // tr,), in_specs=[row, pl.BlockSpec((1, D), lambda i: (0, 0)), row, row],
        out_specs=[row, row, pl.BlockSpec((8, D), lambda i: (0, 0))],
        out_shape=[jax.ShapeDtypeStruct((Tp, D), F32), jax.ShapeDtypeStruct((Tp, D), BF16), jax.ShapeDtypeStruct((8, D), F32)],
        name=name, compiler_params=_params("arbitrary"))(h, g, du, dres)


def rms_bwd_input(h, g, du, dres, *, name):
    Tp, D = h.shape
    nt = Tp // ROW_TILE

    def body(h_ref, g_ref, du_ref, dr_ref, gx_ref, gm_ref, st_ref):
        i = pl.program_id(0)

        @pl.when(i == 0)
        def _():
            st_ref[...] = jnp.zeros_like(st_ref)

        du_v = du_ref[...].astype(F32)
        dx, xh = _rms_bwd_math(h_ref[...], g_ref[...], du_v)
        dh = dr_ref[...] + dx
        st_ref[0:1, :] += jnp.sum(du_v * xh, axis=0, keepdims=True)

        @pl.when(i < nt - 1)
        def _():
            gx_ref[...] = dh

        @pl.when(i == nt - 1)
        def _():
            gm_ref[...] = dh[ROW_TILE - N_META:, :]

    row = pl.BlockSpec((ROW_TILE, D), lambda i: (i, 0))
    return pl.pallas_call(
        body, grid=(nt,), in_specs=[row, pl.BlockSpec((1, D), lambda i: (0, 0)), row, row],
        out_specs=[pl.BlockSpec((ROW_TILE, D), lambda i: (jnp.minimum(i, nt - 2), 0)), pl.BlockSpec((N_META, D), lambda i: (0, 0)),
                   pl.BlockSpec((8, D), lambda i: (0, 0))],
        out_shape=[jax.ShapeDtypeStruct((Tp - ROW_TILE, D), F32), jax.ShapeDtypeStruct((N_META, D), F32),
                   jax.ShapeDtypeStruct((8, D), F32)],
        name=name, compiler_params=_params("arbitrary"))(h, g, du, dres)


def _conv_taps(ext, cur, vec_ref, cs, rows):
    n = rows + HIST
    x1 = pltpu.roll(ext, 1, 0)[HIST:, :]
    x2 = pltpu.roll(ext, 2, 0)[HIST:, :]
    x3 = pltpu.roll(ext, 3, 0)[HIST:, :]
    del n
    xc = (vec_ref[1:2, cs] + vec_ref[8:9, cs] * cur + vec_ref[7:8, cs] * x1 + vec_ref[6:7, cs] * x2 + vec_ref[5:6, cs] * x3)
    return xc, x1, x2, x3


def _window_sum_back(ext, w):
    s, sh = ext, 1
    while sh < w:
        s = s + pltpu.roll(s, sh, 0)
        sh *= 2
    return s[HIST:, :]


def _lru_gates(xc, wa, wx, vec_ref, cs, sp):
    xcb = xc.astype(BF16)
    r = _sigmoid(jnp.dot(xcb, wa, preferred_element_type=F32) + vec_ref[2:3, cs])
    ig = _sigmoid(jnp.dot(xcb, wx, preferred_element_type=F32) + vec_ref[3:4, cs])
    log_a = (-LRU_C) * r * sp
    a = jnp.exp(log_a)
    mult = jnp.sqrt(_neg_expm1(2.0 * log_a))
    return xcb, r, ig, a, mult


def mix_fwd(proj, pool_w, gate_a, gate_x, vecs, *, name):
    Tp = proj.shape[0]
    D = proj.shape[1] // 5
    R = ROW_TILE
    nt = Tp // R
    H = D // HEAD_DIM
    PG = D // len(POOL_WINDOWS)

    def body(p_ref, pw_ref, wa_ref, wx_ref, vec_ref, m_ref, hs_ref, hist_p, hist_l, hcar, mtmp):
        i = pl.program_id(0)
        is_meta = i == 0

        @pl.when(is_meta)
        def _():
            hist_p[...] = jnp.zeros_like(hist_p)
            hist_l[...] = jnp.zeros_like(hist_l)
            hcar[...] = jnp.zeros_like(hcar)

        row, valid, t_log = _tile_masks(is_meta, R)

        for g, w in enumerate(POOL_WINDOWS):
            cs = slice(g * PG, (g + 1) * PG)
            v = p_ref[:, g * PG:(g + 1) * PG].astype(F32)
            ws = _window_sum_back(jnp.concatenate([hist_p[:, cs], v], axis=0), w)
            d = ws / _window_count(t_log, w) - v
            y = jnp.dot(d.astype(BF16), pw_ref[g], preferred_element_type=F32)
            gp = p_ref[:, 3 * D + g * PG:3 * D + (g + 1) * PG].astype(F32)
            mtmp[:, cs] = _sigmoid(gp) * (y * vec_ref[0:1, cs])
            hist_p[:, cs] = v[R - HIST:, :]

        for h in range(H):
            cs = slice(h * HEAD_DIM, (h + 1) * HEAD_DIM)
            vl = p_ref[:, D + h * HEAD_DIM:D + (h + 1) * HEAD_DIM].astype(F32)
            xc, _, _, _ = _conv_taps(jnp.concatenate([hist_l[:, cs], vl], axis=0), vl, vec_ref, cs, R)
            sp = _softplus_neg(vec_ref[4:5, cs])
            _, r, ig, a, mult = _lru_gates(xc, wa_ref[h], wx_ref[h], vec_ref, cs, sp)
            b = jnp.where(valid, mult * (ig * xc), 0.0)
            ca, cb = a, b
            sh = 1
            while sh < R:
                a_sh = jnp.where(row >= sh, pltpu.roll(ca, sh, 0), 1.0)
                b_sh = jnp.where(row >= sh, pltpu.roll(cb, sh, 0), 0.0)
                cb = cb + ca * b_sh
                ca = ca * a_sh
                sh *= 2
            hs = cb + ca * hcar[7:8, cs]
            hs_ref[:, cs] = hs
            hcar[:, cs] = hs[R - 8:, :]
            hist_l[:, cs] = vl[R - HIST:, :]
            ge, _ = _gelu_tanh(p_ref[:, 2 * D + h * HEAD_DIM:2 * D + (h + 1) * HEAD_DIM].astype(F32))
            gl = p_ref[:, 4 * D + h * HEAD_DIM:4 * D + (h + 1) * HEAD_DIM].astype(F32)
            m_ref[:, cs] = (mtmp[:, cs] + _sigmoid(gl) * (hs * ge)).astype(BF16)

    def tile(i):
        return (i + nt - 1) % nt

    full = lambda shape: pl.BlockSpec(shape, lambda i: (0,) * len(shape))
    return pl.pallas_call(
        body, grid=(nt,),
        in_specs=[pl.BlockSpec((R, 5 * D), lambda i: (tile(i), 0)), full(pool_w.shape), full(gate_a.shape), full(gate_x.shape),
                  full(vecs.shape)],
        out_specs=[pl.BlockSpec((R, D), lambda i: (tile(i), 0)), pl.BlockSpec((R, D), lambda i: (tile(i), 0))],
        out_shape=[jax.ShapeDtypeStruct((Tp, D), BF16), jax.ShapeDtypeStruct((Tp, D), F32)],
        scratch_shapes=[pltpu.VMEM((HIST, D), F32), pltpu.VMEM((HIST, D), F32), pltpu.VMEM((8, D), F32), pltpu.VMEM((R, D), F32)],
        name=name, compiler_params=_params("arbitrary"))(proj, pool_w, gate_a, gate_x, vecs)


def mix_bwd(proj, hs, dmerged, pool_w, gate_a, gate_x, vecs, *, name):
    Tp = proj.shape[0]
    D = proj.shape[1] // 5
    R = ROW_TILE
    nt = Tp // R
    H = D // HEAD_DIM
    PG = D // len(POOL_WINDOWS)

    def body(p_ref, pprev_ref, hs_ref, hprev_ref, dm_ref, pw_ref, wa_ref, wx_ref, vec_ref,
             dp_ref, dpw_ref, dwa_ref, dwx_ref, dvec_ref, car_g, fut_dxc, fut_q):
        i = pl.program_id(0)
        is_meta = i == nt - 1

        @pl.when(i == 0)
        def _():
            dpw_ref[...] = jnp.zeros_like(dpw_ref)
            dwa_ref[...] = jnp.zeros_like(dwa_ref)
            dwx_ref[...] = jnp.zeros_like(dwx_ref)
            dvec_ref[...] = jnp.zeros_like(dvec_ref)
            car_g[...] = jnp.zeros_like(car_g)
            fut_dxc[...] = jnp.zeros_like(fut_dxc)
            fut_q[...] = jnp.zeros_like(fut_q)

        row, valid, t_log = _tile_masks(is_meta, R)
        keep_prev = jnp.logical_not(is_meta)

        def colsum(x):
            return jnp.sum(x, axis=0, keepdims=True)

        for g, w in enumerate(POOL_WINDOWS):
            cs = slice(g * PG, (g + 1) * PG)
            v = p_ref[:, g * PG:(g + 1) * PG].astype(F32)
            vprev = jnp.where(keep_prev, pprev_ref[:, g * PG:(g + 1) * PG].astype(F32), 0.0)
            cnt = _window_count(t_log, w)
            d = _window_sum_back(jnp.concatenate([vprev, v], axis=0), w) / cnt - v
            d_bf = d.astype(BF16)
            y = jnp.dot(d_bf, pw_ref[g], preferred_element_type=F32)
            scale = vec_ref[0:1, cs]
            sg = _sigmoid(p_ref[:, 3 * D + g * PG:3 * D + (g + 1) * PG].astype(F32))
            dm = dm_ref[:, cs].astype(F32)
            dpo = dm * sg
            dp_ref[:, 3 * D + g * PG:3 * D + (g + 1) * PG] = (dm * (y * scale) * sg * (1.0 - sg)).astype(BF16)
            dvec_ref[0:1, cs] += colsum(dpo * y)
            dy = (dpo * scale).astype(BF16)
            dd = _dot_nt(dy, pw_ref[g])
            dpw_ref[g] += _dot_tn(d_bf, dy)
            q = dd / cnt
            s, sh = jnp.concatenate([q, fut_q[:, cs]], axis=0), 1
            while sh < w:
                s = s + pltpu.roll(s, R + HIST - sh, 0)
                sh *= 2
            dp_ref[:, g * PG:(g + 1) * PG] = (s[:R, :] - dd).astype(BF16)
            fut_q[:, cs] = q[:HIST, :]

        for h in range(H):
            cs = slice(h * HEAD_DIM, (h + 1) * HEAD_DIM)
            pc = lambda blk: slice(blk * D + h * HEAD_DIM, blk * D + (h + 1) * HEAD_DIM)
            vl = p_ref[:, pc(1)].astype(F32)
            vlprev = jnp.where(keep_prev, pprev_ref[:, pc(1)].astype(F32), 0.0)
            xc, x1, x2, x3 = _conv_taps(jnp.concatenate([vlprev, vl], axis=0), vl, vec_ref, cs, R)
            lam = vec_ref[4:5, cs]
            sp = _softplus_neg(lam)
            xcb, r, ig, a, mult = _lru_gates(xc, wa_ref[h], wx_ref[h], vec_ref, cs, sp)
            hsv = hs_ref[:, cs]
            hprev = jnp.where(row >= 1, pltpu.roll(hsv, 1, 0), hprev_ref[HIST - 1:HIST, cs])
            vg = p_ref[:, pc(2)].astype(F32)
            ge, th = _gelu_tanh(vg)
            sgl = _sigmoid(p_ref[:, pc(4)].astype(F32))
            dm = dm_ref[:, cs].astype(F32)
            dlo = dm * sgl
            dp_ref[:, pc(4)] = (dm * (hsv * ge) * sgl * (1.0 - sgl)).astype(BF16)
            dp_ref[:, pc(2)] = (dlo * hsv * _gelu_tanh_grad(vg, th)).astype(BF16)
            ca = jnp.where(row < R - 1, pltpu.roll(a, R - 1, 0), 1.0)
            cb = dlo * ge
            sh = 1
            while sh < R:
                a_sh = jnp.where(row < R - sh, pltpu.roll(ca, R - sh, 0), 1.0)
                b_sh = jnp.where(row < R - sh, pltpu.roll(cb, R - sh, 0), 0.0)
                cb = cb + ca * b_sh
                ca = ca * a_sh
                sh *= 2
            G = cb + ca * car_g[0:1, cs]
            car_g[:, cs] = (a * G)[0:8, :]
            da = jnp.where(valid, G * hprev, 0.0)
            db = jnp.where(valid, G, 0.0)
            dmult = db * (ig * xc)
            dig = db * (mult * xc)
            dxc = db * (mult * ig)
            dlog_a = da * a - dmult * ((a * a) / mult)
            dvec_ref[4:5, cs] += colsum(dlog_a * r) * (-LRU_C)
            dr = dlog_a * ((-LRU_C) * sp)
            dpa = dr * r * (1.0 - r)
            dpx = dig * ig * (1.0 - ig)
            dpa_bf = dpa.astype(BF16)
            dpx_bf = dpx.astype(BF16)
            dwa_ref[h] += _dot_tn(xcb, dpa_bf)
            dwx_ref[h] += _dot_tn(xcb, dpx_bf)
            dvec_ref[2:3, cs] += colsum(dpa)
            dvec_ref[3:4, cs] += colsum(dpx)
            dxc = dxc + _dot_nt(dpa_bf, wa_ref[h]) + _dot_nt(dpx_bf, wx_ref[h])
            ext = jnp.concatenate([dxc, fut_dxc[:, cs]], axis=0)
            n = R + HIST
            dvl = (vec_ref[8:9, cs] * dxc + vec_ref[7:8, cs] * pltpu.roll(ext, n - 1, 0)[:R, :]
                   + vec_ref[6:7, cs] * pltpu.roll(ext, n - 2, 0)[:R, :] + vec_ref[5:6, cs] * pltpu.roll(ext, n - 3, 0)[:R, :])
            dp_ref[:, pc(1)] = dvl.astype(BF16)
            dvec_ref[1:2, cs] += colsum(dxc)
            dvec_ref[8:9, cs] += colsum(dxc * vl)
            dvec_ref[7:8, cs] += colsum(dxc * x1)
            dvec_ref[6:7, cs] += colsum(dxc * x2)
            dvec_ref[5:6, cs] += colsum(dxc * x3)
            fut_dxc[:, cs] = dxc[:HIST, :]

            @pl.when(is_meta)
            def _():
                dvec_ref[4:5, cs] = dvec_ref[4:5, cs] * (-_sigmoid(-lam))

    def tile(i):
        return (2 * nt - 2 - i) % nt

    def prev_blk(i):
        per = R // HIST
        return jnp.where(i == nt - 1, 0, jnp.where(i == nt - 2, Tp // HIST - 1, (nt - 2 - i) * per - 1))

    full = lambda shape: pl.BlockSpec(shape, lambda i: (0,) * len(shape))
    G_ = len(POOL_WINDOWS)
    return pl.pallas_call(
        body, grid=(nt,),
        in_specs=[pl.BlockSpec((R, 5 * D), lambda i: (tile(i), 0)), pl.BlockSpec((HIST, 5 * D), lambda i: (prev_blk(i), 0)),
                  pl.BlockSpec((R, D), lambda i: (tile(i), 0)), pl.BlockSpec((HIST, D), lambda i: (prev_blk(i), 0)),
                  pl.BlockSpec((R, D), lambda i: (tile(i), 0)),
                  full(pool_w.shape), full(gate_a.shape), full(gate_x.shape), full(vecs.shape)],
        out_specs=[pl.BlockSpec((R, 5 * D), lambda i: (tile(i), 0)), full((G_, PG, PG)), full((H, HEAD_DIM, HEAD_DIM)),
                   full((H, HEAD_DIM, HEAD_DIM)), full((16, D))],
        out_shape=[jax.ShapeDtypeStruct((Tp, 5 * D), BF16), jax.ShapeDtypeStruct((G_, PG, PG), F32),
                   jax.ShapeDtypeStruct((H, HEAD_DIM, HEAD_DIM), F32), jax.ShapeDtypeStruct((H, HEAD_DIM, HEAD_DIM), F32),
                   jax.ShapeDtypeStruct((16, D), F32)],
        scratch_shapes=[pltpu.VMEM((8, D), F32), pltpu.VMEM((HIST, D), F32), pltpu.VMEM((HIST, D), F32)],
        name=name, compiler_params=_params("arbitrary"))(proj, proj, hs, hs, dmerged, pool_w, gate_a, gate_x, vecs)


def adamw(w, g, m, v, *, name):
    rows, cols = w.shape
    tr = _pick(rows, (256, 128, 64, 32, 16, 8))

    def body(w_ref, g_ref, m_ref, v_ref, go_ref, d_ref, mo_ref, vo_ref):
        gv = g_ref[...]
        mn = ADAM_B1 * m_ref[...] + (1.0 - ADAM_B1) * gv
        vn = ADAM_B2 * v_ref[...] + (1.0 - ADAM_B2) * (gv * gv)
        m_hat = mn / (1.0 - ADAM_B1 ** ADAM_STEP)
        v_hat = vn / (1.0 - ADAM_B2 ** ADAM_STEP)
        go_ref[...] = gv
        d_ref[...] = -ADAM_LR * (m_hat / (jnp.sqrt(v_hat) + ADAM_EPS) + ADAM_WD * w_ref[...])
        mo_ref[...] = mn
        vo_ref[...] = vn

    blk = pl.BlockSpec((tr, cols), lambda i: (i, 0))
    sds = jax.ShapeDtypeStruct((rows, cols), F32)
    return pl.pallas_call(body, grid=(rows // tr,), in_specs=[blk] * 4, out_specs=[blk] * 4, out_shape=[sds] * 4, name=name,
                          compiler_params=_params("parallel"))(w, g, m, v)


def _place():
    x, y, c = lax.axis_index("x"), lax.axis_index("y"), lax.axis_index("c")
    chips = [(1 - x, y), (x, 1 - y), (1 - x, 1 - y)]
    return x, y, c, chips


def allgather8(block, *, name, reduce_sum=False):
    rows, cols = block.shape

    def body(x_ref, out_ref, *scratch):
        if reduce_sum:
            buf, send_sems, recv_sems, local_sem = scratch
        else:
            buf = out_ref
            send_sems, recv_sems, local_sem = scratch
        x, y, c, chips = _place()
        me, sibling = (x, y, c), (x, y, 1 - c)

        def slot(px, py, pc):
            return buf.at[4 * px + 2 * py + pc]

        def copy(k, blk, to, src=None):
            return pltpu.make_async_remote_copy(src_ref=slot(*blk) if src is None else src, dst_ref=slot(*blk),
                                                send_sem=send_sems.at[k], recv_sem=recv_sems.at[k], device_id=to, device_id_type=MESH)

        mine = pltpu.make_async_copy(x_ref, slot(*me), local_sem)
        mine.start()
        first = [copy(0, me, sibling, src=x_ref)]
        first += [copy(1 + j, me, (*chip, c), src=x_ref) for j, chip in enumerate(chips)]
        for cp in first:
            cp.start()
        passed = [copy(4 + j, (*chip, c), sibling) for j, chip in enumerate(chips)]
        for j, chip in enumerate(chips):
            copy(1 + j, (*chip, c), me).wait_recv()
            passed[j].start()
        copy(0, sibling, me).wait_recv()
        for j, chip in enumerate(chips):
            copy(4 + j, (*chip, 1 - c), me).wait_recv()
        for cp in first + passed:
            cp.wait_send()
        mine.wait()
        if reduce_sum:
            acc = buf[0]
            for d in range(1, 8):
                acc = acc + buf[d]
            out_ref[...] = acc

    sems = [pltpu.SemaphoreType.DMA((7,)), pltpu.SemaphoreType.DMA((7,)), pltpu.SemaphoreType.DMA]
    if reduce_sum:
        out_shape = jax.ShapeDtypeStruct((rows, cols), block.dtype)
        scratch = [pltpu.VMEM((8, rows, cols), block.dtype)] + sems
    else:
        out_shape = jax.ShapeDtypeStruct((8, rows, cols), block.dtype)
        scratch = sems
    return pl.pallas_call(body, out_shape=out_shape, in_specs=[pl.BlockSpec(memory_space=pltpu.VMEM)],
                          out_specs=pl.BlockSpec(memory_space=pltpu.VMEM), scratch_shapes=scratch, name=name,
                          compiler_params=pltpu.CompilerParams(vmem_limit_bytes=VMEM_LIMIT_BYTES))(block)


def _any_specs(n):
    return [pl.BlockSpec(memory_space=pl.ANY)] * n


def gather_weights(shards, *, name):
    n = len(shards)

    def body(*refs):
        ins, outs = refs[:n], refs[n:2 * n]
        send_sems, recv_sems, local_sems = refs[2 * n:]
        x, y, c, chips = _place()
        sibling = (x, y, 1 - c)
        me_chip = 2 * x + y
        local = [pltpu.make_async_copy(ins[t], outs[t].at[me_chip], local_sems.at[t]) for t in range(n)]
        for cp in local:
            cp.start()

        def copy(t, k, chip_idx, half, to, src=None):
            dst = outs[t].at[chip_idx, half]
            return pltpu.make_async_remote_copy(src_ref=dst if src is None else src, dst_ref=dst, send_sem=send_sems.at[t, k],
                                                recv_sem=recv_sems.at[t, k], device_id=to, device_id_type=MESH)

        first = [copy(t, j, me_chip, c, (*chip, c), src=ins[t].at[c]) for t in range(n) for j, chip in enumerate(chips)]
        for cp in first:
            cp.start()
        passed = []
        for t in range(n):
            for j, (px, py) in enumerate(chips):
                copy(t, j, 2 * px + py, c, (x, y, c)).wait_recv()
                fwd = copy(t, 3 + j, 2 * px + py, c, sibling)
                fwd.start()
                passed.append(fwd)
        for t in range(n):
            for j, (px, py) in enumerate(chips):
                copy(t, 3 + j, 2 * px + py, 1 - c, (x, y, c)).wait_recv()
        for cp in first + passed:
            cp.wait_send()
        for cp in local:
            cp.wait()

    out_shape = [jax.ShapeDtypeStruct((4, *s.shape), s.dtype) for s in shards]
    return pl.pallas_call(
        body, out_shape=out_shape, in_specs=_any_specs(n), out_specs=_any_specs(n),
        scratch_shapes=[pltpu.SemaphoreType.DMA((n, 6)), pltpu.SemaphoreType.DMA((n, 6)), pltpu.SemaphoreType.DMA((n,))],
        name=name)(*shards)


def exchange_halves_in(grads, *, name):
    n = len(grads)

    def body(*refs):
        ins, outs = refs[:n], refs[n:2 * n]
        send_sems, recv_sems = refs[2 * n:]
        x, y, c, _ = _place()
        copies = [pltpu.make_async_remote_copy(src_ref=ins[t].at[k, 1 - c], dst_ref=outs[t].at[k], send_sem=send_sems.at[t, k],
                                               recv_sem=recv_sems.at[t, k], device_id=(x, y, 1 - c), device_id_type=MESH)
                  for t in range(n) for k in range(4)]
        for cp in copies:
            cp.start()
        for cp in copies:
            cp.wait()

    out_shape = [jax.ShapeDtypeStruct((4, *g.shape[2:]), g.dtype) for g in grads]
    return pl.pallas_call(body, out_shape=out_shape, in_specs=_any_specs(n), out_specs=_any_specs(n),
                          scratch_shapes=[pltpu.SemaphoreType.DMA((n, 4)), pltpu.SemaphoreType.DMA((n, 4))], name=name)(*grads)


def chip_presum(grad, recv, c_arr, *, name):
    _, _, r, cols = grad.shape
    tr = _pick(r, (256, 128, 64, 32, 16))

    def body(c_ref, g_ref, r_ref, o_ref):
        del c_ref
        o_ref[...] = (g_ref[...] + r_ref[...]).astype(BF16)

    grid_spec = pltpu.PrefetchScalarGridSpec(
        num_scalar_prefetch=1, grid=(4, r // tr),
        in_specs=[pl.BlockSpec((None, None, tr, cols), lambda k, i, c_ref: (k, c_ref[0], i, 0)),
                  pl.BlockSpec((None, tr, cols), lambda k, i, c_ref: (k, i, 0))],
        out_specs=pl.BlockSpec((None, tr, cols), lambda k, i, c_ref: (k, i, 0)))
    return pl.pallas_call(body, grid_spec=grid_spec, out_shape=jax.ShapeDtypeStruct((4, r, cols), BF16), name=name,
                          compiler_params=_params("parallel", "parallel"))(c_arr, grad, recv)


def scatter_presums(presums, *, name):
    n = len(presums)

    def body(*refs):
        ins, outs = refs[:n], refs[n:2 * n]
        send_sems, recv_sems = refs[2 * n:]
        x, y, c, chips = _place()
        copies = [pltpu.make_async_remote_copy(src_ref=ins[t].at[2 * px + py], dst_ref=outs[t].at[j], send_sem=send_sems.at[t, j],
                                               recv_sem=recv_sems.at[t, j], device_id=(px, py, c), device_id_type=MESH)
                  for t in range(n) for j, (px, py) in enumerate(chips)]
        for cp in copies:
            cp.start()
        for cp in copies:
            cp.wait()

    out_shape = [jax.ShapeDtypeStruct((3, *p.shape[1:]), p.dtype) for p in presums]
    return pl.pallas_call(body, out_shape=out_shape, in_specs=_any_specs(n), out_specs=_any_specs(n),
                          scratch_shapes=[pltpu.SemaphoreType.DMA((n, 3)), pltpu.SemaphoreType.DMA((n, 3))], name=name)(*presums)


def final_half(grad, recv, got, mc_arr, *, name):
    _, _, r, cols = grad.shape
    tr = _pick(r, (256, 128, 64, 32, 16))

    def body(mc_ref, g_ref, r_ref, q_ref, o_ref):
        del mc_ref
        acc = g_ref[...] + r_ref[...]
        for j in range(3):
            acc = acc + q_ref[j].astype(F32)
        o_ref[...] = acc

    grid_spec = pltpu.PrefetchScalarGridSpec(
        num_scalar_prefetch=1, grid=(r // tr,),
        in_specs=[pl.BlockSpec((None, None, tr, cols), lambda i, mc: (mc[0], mc[1], i, 0)),
                  pl.BlockSpec((None, tr, cols), lambda i, mc: (mc[0], i, 0)),
                  pl.BlockSpec((3, tr, cols), lambda i, mc: (0, i, 0))],
        out_specs=pl.BlockSpec((tr, cols), lambda i, mc: (i, 0)))
    return pl.pallas_call(body, grid_spec=grid_spec, out_shape=jax.ShapeDtypeStruct((r, cols), F32), name=name,
                          compiler_params=_params("parallel"))(mc_arr, grad, recv, got)


def exchange_halves_out(halves, *, name):
    n = len(halves)

    def body(*refs):
        ins, outs = refs[:n], refs[n:2 * n]
        send_sems, recv_sems, local_sems = refs[2 * n:]
        x, y, c, _ = _place()
        local = [pltpu.make_async_copy(ins[t], outs[t].at[c], local_sems.at[t]) for t in range(n)]
        copies = [pltpu.make_async_remote_copy(src_ref=ins[t], dst_ref=outs[t].at[c], send_sem=send_sems.at[t],
                                               recv_sem=recv_sems.at[t], device_id=(x, y, 1 - c), device_id_type=MESH) for t in range(n)]
        for cp in local + copies:
            cp.start()
        for t in range(n):
            pltpu.make_async_remote_copy(src_ref=ins[t], dst_ref=outs[t].at[1 - c], send_sem=send_sems.at[t], recv_sem=recv_sems.at[t],
                                         device_id=(x, y, 1 - c), device_id_type=MESH).wait()
        for cp in local:
            cp.wait()

    out_shape = [jax.ShapeDtypeStruct((2, *h.shape), h.dtype) for h in halves]
    return pl.pallas_call(body, out_shape=out_shape, in_specs=_any_specs(n), out_specs=_any_specs(n),
                          scratch_shapes=[pltpu.SemaphoreType.DMA((n,)), pltpu.SemaphoreType.DMA((n,)), pltpu.SemaphoreType.DMA((n,))],
                          name=name)(*halves)


def _pad_rows(a, rows):
    return jnp.pad(a, ((0, rows - a.shape[0]), (0, 0)))


def local_step(x, target, meta, g1, g2, gf, w_in, w_out, w1, w2, pool_w, gate_a, gate_x, vecs):
    S, D = x.shape
    h0 = jnp.concatenate([x, jnp.zeros((ROW_TILE - N_META, D), F32), meta], axis=0)
    u = rmsnorm_fwd(h0, g1, name="norm1")
    proj = mm_nn(u, w_in, out_dtype=BF16, name="proj")
    merged, hs = mix_fwd(proj, pool_w, gate_a, gate_x, vecs, name="mix_fwd")
    h1 = mm_nn(merged, w_out, out_dtype=F32, name="out_proj", epilogue=lambda r, res: r + res, extras=(h0,))
    u2 = rmsnorm_fwd(h1, g2, name="norm2")
    a1 = mm_nn(u2, w1, out_dtype=BF16, name="mlp_up")
    h2 = mm_nn(a1, w2, out_dtype=F32, name="mlp_down", a_pro=_relu_sq, epilogue=lambda r, res: r + res, extras=(h1,))
    dh2, dh2_bf, st_f = final_loss(h2, target, gf, name="final_loss")
    da1 = mm_nt(dh2_bf, w2, out_dtype=BF16, name="mlp_down_dx",
                epilogue=lambda r, a: r * (2.0 * jnp.maximum(a.astype(F32), 0.0)), extras=(a1,))
    d_w2 = mm_tn(a1, dh2_bf, shards=1, name="mlp_down_dw", a_pro=_relu_sq)
    d_w1 = mm_tn(u2, da1, shards=4, name="mlp_up_dw")
    du2 = mm_nt(da1, w1, out_dtype=F32, name="mlp_up_dx")
    dh1, dh1_bf, st_2 = rms_bwd(h1, g2, du2, dh2, name="norm2_bwd")
    dmerged = mm_nt(dh1_bf, w_out, out_dtype=BF16, name="out_proj_dx")
    d_wout = mm_tn(merged, dh1_bf, shards=1, name="out_proj_dw")
    dproj, d_pool, d_ga, d_gx, d_vecs = mix_bwd(proj, hs, dmerged, pool_w, gate_a, gate_x, vecs, name="mix_bwd")
    d_win = mm_tn(u, dproj, shards=4, name="proj_dw")
    du = mm_nt(dproj, w_in, out_dtype=F32, name="proj_dx")
    grad_x, d_meta, st_1 = rms_bwd_input(h0, g1, du, dh1, name="norm1_bwd")
    small = jnp.concatenate([d_meta, d_vecs, st_1, st_2, st_f], axis=0)
    return grad_x, small, d_win, d_wout, d_w1, d_w2, d_pool, d_ga, d_gx


def kernel(x, meta_tokens, norm1_g, w_in, pool_w, pool_scale, conv_w, conv_b, gate_a_w, gate_a_b, gate_x_w, gate_x_b, lru_lambda, w_out, norm2_g, mlp_w1, mlp_w2, final_g, loss_target, m_meta_tokens, m_norm1_g, m_w_in, m_pool_w, m_pool_scale, m_conv_w, m_conv_b, m_gate_a_w, m_gate_a_b, m_gate_x_w, m_gate_x_b, m_lru_lambda, m_w_out, m_norm2_g, m_mlp_w1, m_mlp_w2, m_final_g, v_meta_tokens, v_norm1_g, v_w_in, v_pool_w, v_pool_scale, v_conv_w, v_conv_b, v_gate_a_w, v_gate_a_b, v_gate_x_w, v_gate_x_b, v_lru_lambda, v_w_out, v_norm2_g, v_mlp_w1, v_mlp_w2, v_final_g):
    D = x.shape[-1]
    H = D // HEAD_DIM
    G = len(POOL_WINDOWS)
    PG = D // G
    ax, ay, ac = lax.axis_index("x"), lax.axis_index("y"), lax.axis_index("c")
    chip = 2 * ax + ay
    dshard = D // 4

    def halves(w):
        w2d = w.reshape(-1, w.shape[-1])
        return w2d.astype(BF16).reshape(2, w2d.shape[0] // 2, w2d.shape[1])

    big = [w_in, pool_w, gate_a_w, gate_x_w, w_out, mlp_w1, mlp_w2]
    gw_in, gpool, gga, ggx, gw_out, gw1, gw2 = gather_weights([halves(w) for w in big], name="gather_weights")
    kin = w_in.shape[1]
    w_in_f = gw_in.reshape(4, kin, w_in.shape[2])
    w1_f = gw1.reshape(4, mlp_w1.shape[1], mlp_w1.shape[2])
    w_out_f = gw_out.reshape(4 * w_out.shape[1], w_out.shape[2])
    w2_f = gw2.reshape(4 * mlp_w2.shape[1], mlp_w2.shape[2])
    pool_f = gpool.reshape(4, G, PG // 4, PG).transpose(1, 0, 2, 3).reshape(G, PG, PG)
    ga_f = gga.reshape(4, H, HEAD_DIM // 4, HEAD_DIM).transpose(1, 0, 2, 3).reshape(H, HEAD_DIM, HEAD_DIM)
    gx_f = ggx.reshape(4, H, HEAD_DIM // 4, HEAD_DIM).transpose(1, 0, 2, 3).reshape(H, HEAD_DIM, HEAD_DIM)

    small_in = jnp.concatenate([meta_tokens, _pad_rows(conv_w[0], 8), _pad_rows(gate_a_b.reshape(1, dshard), 8),
                                _pad_rows(gate_x_b.reshape(1, dshard), 8)], axis=0)
    sm = allgather8(small_in, name="gather_small")[0::2]
    meta_f = sm[:, 0:16].transpose(1, 0, 2).reshape(N_META, D)
    conv_w_f = sm[:, 16:20].transpose(1, 0, 2).reshape(4, D)
    hd4 = HEAD_DIM // 4
    ba_f = sm[:, 24].reshape(4, H, hd4).transpose(1, 0, 2).reshape(1, D)
    bx_f = sm[:, 32].reshape(4, H, hd4).transpose(1, 0, 2).reshape(1, D)
    vecs = jnp.concatenate([pool_scale, conv_b, ba_f, bx_f, lru_lambda, conv_w_f, jnp.zeros((7, D), F32)], axis=0)

    gfin = final_g.reshape(1, D)
    grad_x, small, d_win, d_wout, d_w1, d_w2, d_pool, d_ga, d_gx = local_step(
        x[0], loss_target[0], meta_f, norm1_g, norm2_g, gfin, w_in_f, w_out_f, w1_f, w2_f, pool_f, ga_f, gx_f, vecs)

    tot = allgather8(small, name="sum_small", reduce_sum=True)
    loss = jnp.sum(tot[49])
    g_meta = lax.dynamic_slice_in_dim(tot[0:16], chip * dshard, dshard, axis=1)
    g_pool_scale, g_conv_b, g_lam = tot[16:17], tot[17:18], tot[20:21]
    g_ba = lax.dynamic_slice_in_dim(tot[18].reshape(H, HEAD_DIM), chip * hd4, hd4, axis=1)[None]
    g_bx = lax.dynamic_slice_in_dim(tot[19].reshape(H, HEAD_DIM), chip * hd4, hd4, axis=1)[None]
    g_conv_w = lax.dynamic_slice_in_dim(tot[21:25], chip * dshard, dshard, axis=1)[None]
    g_n1, g_n2, g_fin = tot[32:33], tot[40:41], tot[48]

    def chipwise(g, n_blocks, rows):
        return g.reshape(n_blocks, 4, rows, g.shape[-1]).transpose(1, 0, 2, 3).reshape(4, n_blocks * rows, g.shape[-1])

    def split2(g):
        return g.reshape(4, 2, g.shape[1] // 2, g.shape[2])

    full = [split2(d_win), split2(chipwise(d_pool, G, PG // 4)), split2(chipwise(d_ga, H, hd4)), split2(chipwise(d_gx, H, hd4)),
            split2(d_wout.reshape(4, -1, d_wout.shape[-1])), split2(d_w1), split2(d_w2.reshape(4, -1, d_w2.shape[-1]))]
    names = ["w_in", "pool_w", "gate_a_w", "gate_x_w", "w_out", "mlp_w1", "mlp_w2"]
    c_arr = jnp.reshape(ac, (1,)).astype(jnp.int32)
    mc_arr = jnp.stack([chip, ac]).astype(jnp.int32)
    recv = exchange_halves_in(full, name="grads_to_sibling")
    presums = [chip_presum(g, r, c_arr, name="presum_" + nm) for g, r, nm in zip(full, recv, names)]
    got = scatter_presums(presums, name="grads_to_chips")
    mine = [final_half(g, r, q, mc_arr, name="sum_" + nm) for g, r, q, nm in zip(full, recv, got, names)]
    shard_grads = exchange_halves_out(mine, name="grads_from_sibling")

    def step(w, g, m, v, nm):
        cols = w.shape[-1]
        outs = adamw(w.reshape(-1, cols), g.reshape(-1, cols), m.reshape(-1, cols), v.reshape(-1, cols), name="adamw_" + nm)
        return [o.reshape(w.shape) for o in outs]

    weights = dict(meta_tokens=(meta_tokens, g_meta, m_meta_tokens, v_meta_tokens), norm1_g=(norm1_g, g_n1, m_norm1_g, v_norm1_g),
                   w_in=(w_in, shard_grads[0], m_w_in, v_w_in), pool_w=(pool_w, shard_grads[1], m_pool_w, v_pool_w),
                   pool_scale=(pool_scale, g_pool_scale, m_pool_scale, v_pool_scale), conv_w=(conv_w, g_conv_w, m_conv_w, v_conv_w),
                   conv_b=(conv_b, g_conv_b, m_conv_b, v_conv_b), gate_a_w=(gate_a_w, shard_grads[2], m_gate_a_w, v_gate_a_w),
                   gate_a_b=(gate_a_b, g_ba, m_gate_a_b, v_gate_a_b), gate_x_w=(gate_x_w, shard_grads[3], m_gate_x_w, v_gate_x_w),
                   gate_x_b=(gate_x_b, g_bx, m_gate_x_b, v_gate_x_b), lru_lambda=(lru_lambda, g_lam, m_lru_lambda, v_lru_lambda),
                   w_out=(w_out, shard_grads[4], m_w_out, v_w_out), norm2_g=(norm2_g, g_n2, m_norm2_g, v_norm2_g),
                   mlp_w1=(mlp_w1, shard_grads[5], m_mlp_w1, v_mlp_w1), mlp_w2=(mlp_w2, shard_grads[6], m_mlp_w2, v_mlp_w2),
                   final_g=(final_g, g_fin, m_final_g, v_final_g))
    res = {nm: step(*args, nm) for nm, args in weights.items()}
    order = list(weights)
    return (loss, grad_x[None], *[res[n][0] for n in order], *[res[n][1] for n in order], *[res[n][2] for n in order],
            *[res[n][3] for n in order])
```

```python
import functools

import jax
import jax.numpy as jnp
from jax import lax
from jax.experimental import pallas as pl
from jax.experimental.pallas import tpu as pltpu

F32 = jnp.float32
BF16 = jnp.bfloat16
MESH = pl.DeviceIdType.MESH

NORM_EPS = 1e-6
N_META = 16
HEAD_DIM = 256
POOL_WINDOWS = (2, 4, 8, 16)
LRU_C = 8.0
ROW_TILE = 128
HIST = 16
VMEM_LIMIT_BYTES = 56 * 1024 * 1024
ADAM_LR, ADAM_B1, ADAM_B2, ADAM_EPS, ADAM_WD, ADAM_STEP = 0.001, 0.9, 0.999, 1e-08, 0.01, 10


def _pick(n, prefs):
    for p in prefs:
        if n % p == 0:
            return p
    return n


def _params(*sem):
    return pltpu.CompilerParams(dimension_semantics=sem, vmem_limit_bytes=VMEM_LIMIT_BYTES)


def _sigmoid(x):
    return 1.0 / (1.0 + jnp.exp(-x))


def _gelu_tanh(x):
    t = jnp.tanh(0.7978845608028654 * (x + 0.044715 * (x * x * x)))
    return 0.5 * x * (1.0 + t), t


def _gelu_tanh_grad(x, t):
    return 0.5 * (1.0 + t) + 0.5 * x * (1.0 - t * t) * (0.7978845608028654 * (1.0 + 3.0 * 0.044715 * x * x))


def _neg_expm1(x):
    series = -x * (1.0 + x * 0.5 * (1.0 + x * (1.0 / 3.0) * (1.0 + x * 0.25 * (1.0 + x * 0.2 * (1.0 + x * (1.0 / 6.0) * (1.0 + x * (1.0 / 7.0)))))))
    return jnp.where(x > -0.3, series, 1.0 - jnp.exp(x))


def _softplus_neg(lam):
    z = jnp.exp(-jnp.abs(lam))
    log1p_z = jnp.where(z < 0.01, z * (1.0 - z * (0.5 - z * (1.0 / 3.0))), jnp.log(1.0 + z))
    return jnp.maximum(-lam, 0.0) + log1p_z


def _tile_masks(is_meta, rows):
    row = lax.broadcasted_iota(jnp.int32, (rows, 1), 0)
    valid = jnp.logical_or(jnp.logical_not(is_meta), row >= rows - N_META)
    t_log = jnp.where(is_meta, row - (rows - N_META), 1 << 20)
    return row, valid, t_log


def _window_count(t_log, w):
    return jnp.clip(t_log + 1, 1, w).astype(F32)


def _dot_nt(a, b):
    return lax.dot_general(a, b, (((1,), (1,)), ((), ())), preferred_element_type=F32)


def _dot_tn(a, b):
    return lax.dot_general(a, b, (((0,), (0,)), ((), ())), preferred_element_type=F32)


def _place():
    x, y, c = lax.axis_index("x"), lax.axis_index("y"), lax.axis_index("c")
    chips = [(1 - x, y), (x, 1 - y), (1 - x, 1 - y)]
    return x, y, c, chips


def _remote(src, dst, send_sem, recv_sem, to):
    return pltpu.make_async_remote_copy(src_ref=src, dst_ref=dst, send_sem=send_sem, recv_sem=recv_sem, device_id=to,
                                        device_id_type=MESH)


class Stage:
    def __init__(self, arrays, out_shapes, aliases, n_copies, copies):
        self.arrays, self.out_shapes, self.aliases, self.n_copies, self.copies = list(arrays), list(out_shapes), aliases, n_copies, copies


def _sds(a):
    return jax.ShapeDtypeStruct(a.shape, a.dtype)


def stage_gather_ici(bufs):
    n = len(bufs)

    def copies(ins, outs, send, recv):
        x, y, c, chips = _place()
        me = 2 * x + y
        sends, recvs = [], []
        for t in range(n):
            for j, (px, py) in enumerate(chips):
                k = 3 * t + j
                mine, theirs = outs[t].at[me, c], outs[t].at[2 * px + py, c]
                sends.append(_remote(mine, mine, send.at[k], recv.at[k], (px, py, c)))
                recvs.append(_remote(theirs, theirs, send.at[k], recv.at[k], (px, py, c)))
        return sends, recvs

    return Stage(bufs, [_sds(b) for b in bufs], {t: t for t in range(n)}, 3 * n, copies)


def stage_gather_d2d(bufs):
    n = len(bufs)

    def copies(ins, outs, send, recv):
        x, y, c, chips = _place()
        sends, recvs = [], []
        for t in range(n):
            for j, (px, py) in enumerate(chips):
                k = 3 * t + j
                got, sib = outs[t].at[2 * px + py, c], outs[t].at[2 * px + py, 1 - c]
                sends.append(_remote(got, got, send.at[k], recv.at[k], (x, y, 1 - c)))
                recvs.append(_remote(sib, sib, send.at[k], recv.at[k], (x, y, 1 - c)))
        return sends, recvs

    return Stage(bufs, [_sds(b) for b in bufs], {t: t for t in range(n)}, 3 * n, copies)


def stage_to_sibling(grads):
    n = len(grads)

    def copies(ins, outs, send, recv):
        x, y, c, _ = _place()
        sends, recvs = [], []
        for t in range(n):
            for k4 in range(4):
                k = 4 * t + k4
                sends.append(_remote(ins[t].at[k4, 1 - c], outs[t].at[k4], send.at[k], recv.at[k], (x, y, 1 - c)))
                recvs.append(_remote(outs[t].at[k4], outs[t].at[k4], send.at[k], recv.at[k], (x, y, 1 - c)))
        return sends, recvs

    return Stage(grads, [jax.ShapeDtypeStruct((4, *g.shape[2:]), g.dtype) for g in grads], {}, 4 * n, copies)


def stage_to_chips(presums):
    n = len(presums)

    def copies(ins, outs, send, recv):
        x, y, c, chips = _place()
        sends, recvs = [], []
        for t in range(n):
            for j, (px, py) in enumerate(chips):
                k = 3 * t + j
                sends.append(_remote(ins[t].at[2 * px + py], outs[t].at[j], send.at[k], recv.at[k], (px, py, c)))
                recvs.append(_remote(outs[t].at[j], outs[t].at[j], send.at[k], recv.at[k], (px, py, c)))
        return sends, recvs

    return Stage(presums, [jax.ShapeDtypeStruct((3, *p.shape[1:]), p.dtype) for p in presums], {}, 3 * n, copies)


def stage_from_sibling(halves):
    n = len(halves)

    def copies(ins, outs, send, recv):
        x, y, c, _ = _place()
        sends = [_remote(ins[t], outs[t], send.at[t], recv.at[t], (x, y, 1 - c)) for t in range(n)]
        recvs = [_remote(outs[t], outs[t], send.at[t], recv.at[t], (x, y, 1 - c)) for t in range(n)]
        return sends, recvs

    return Stage(halves, [_sds(h) for h in halves], {}, n, copies)


def _any_specs(n):
    return [pl.BlockSpec(memory_space=pl.ANY)] * n


def _staged_call(body, *, grid, in_specs, out_specs, out_shape, scratch_shapes, name, semantics, inputs, stages=()):
    n_in, n_out, n_scr = len(in_specs), len(out_specs), len(scratch_shapes)
    st_arrays = [a for s in stages for a in s.arrays]
    st_outs = [o for s in stages for o in s.out_shapes]
    st_sems = [pltpu.SemaphoreType.DMA((s.n_copies,)) for s in stages for _ in range(2)]
    aliases = {}
    at_in, at_out = n_in, n_out
    for s in stages:
        for a, o in s.aliases.items():
            aliases[at_in + a] = at_out + o
        at_in += len(s.arrays)
        at_out += len(s.out_shapes)

    def full_body(*refs):
        pos = 0

        def take(count):
            nonlocal pos
            part = refs[pos:pos + count]
            pos += count
            return part

        ins, s_ins, outs, s_outs, scr = take(n_in), take(len(st_arrays)), take(n_out), take(len(st_outs)), take(n_scr)
        s_sems = refs[pos:]

        def each_stage(action):
            at_i = at_o = 0
            for idx, s in enumerate(stages):
                sends, recvs = s.copies(s_ins[at_i:at_i + len(s.arrays)], s_outs[at_o:at_o + len(s.out_shapes)],
                                        s_sems[2 * idx], s_sems[2 * idx + 1])
                action(sends, recvs)
                at_i += len(s.arrays)
                at_o += len(s.out_shapes)

        if stages:
            ids = [pl.program_id(a) for a in range(len(grid))]
            first = functools.reduce(jnp.logical_and, [i == 0 for i in ids])
            last = functools.reduce(jnp.logical_and, [i == g - 1 for i, g in zip(ids, grid)])

            def start(sends, recvs):
                for cp in sends:
                    cp.start()

            def finish(sends, recvs):
                for cp in recvs:
                    cp.wait_recv()
                for cp in sends:
                    cp.wait_send()

            @pl.when(first)
            def _():
                each_stage(start)

        body(*ins, *outs, *scr)

        if stages:
            @pl.when(last)
            def _():
                each_stage(finish)

    sem = tuple("arbitrary" for _ in grid) if stages else tuple(semantics)
    res = pl.pallas_call(
        full_body, grid=grid, in_specs=[*in_specs, *_any_specs(len(st_arrays))], out_specs=[*out_specs, *_any_specs(len(st_outs))],
        out_shape=[*out_shape, *st_outs], scratch_shapes=[*scratch_shapes, *st_sems], input_output_aliases=aliases, name=name,
        compiler_params=_params(*sem))(*inputs, *st_arrays)
    outs, rest = list(res[:n_out]), list(res[n_out:])
    per_stage = []
    for s in stages:
        per_stage.append(rest[:len(s.out_shapes)])
        rest = rest[len(s.out_shapes):]
    return outs, per_stage


def comm_call(stages, *, name):
    return _staged_call(lambda: None, grid=(1,), in_specs=[], out_specs=[], out_shape=[], scratch_shapes=[], name=name,
                        semantics=("arbitrary",), inputs=[], stages=stages)[1]


def _matmul(kind, a, b, *, grid, a_spec, b_spec, out_spec, out_shape, acc_shape, name,
            a_pro=None, epilogue=None, extras=(), extra_specs=(), stages=()):
    nk = grid[2]
    n_extra = len(extras)

    def body(a_ref, b_ref, *rest):
        extra_refs = rest[:n_extra]
        o_ref = rest[n_extra]
        av = a_ref[...]
        if a_pro is not None:
            av = a_pro(av)
        av = av.astype(BF16)
        bv = b_ref[...].astype(BF16)
        if kind == "nn":
            p = jnp.dot(av, bv, preferred_element_type=F32)
        elif kind == "nt":
            p = _dot_nt(av, bv)
        else:
            p = _dot_tn(av, bv)

        def finish(r):
            if epilogue is not None:
                r = epilogue(r, *[e[...] for e in extra_refs])
            o_ref[...] = r.astype(o_ref.dtype)

        if nk == 1:
            finish(p)
        else:
            acc_ref = rest[n_extra + 1]
            k = pl.program_id(2)

            @pl.when(k == 0)
            def _():
                acc_ref[...] = p

            @pl.when(k > 0)
            def _():
                acc_ref[...] += p

            @pl.when(k == nk - 1)
            def _():
                finish(acc_ref[...])

    scratch = [] if nk == 1 else [pltpu.VMEM(acc_shape, F32)]
    outs, staged = _staged_call(body, grid=grid, in_specs=[a_spec, b_spec, *extra_specs], out_specs=[out_spec], out_shape=[out_shape],
                                scratch_shapes=scratch, name=name, semantics=("parallel", "parallel", "arbitrary"),
                                inputs=[a, b, *extras], stages=stages)
    return (outs[0], staged) if stages else outs[0]


def mm_nn(a, b, *, out_dtype, name, a_pro=None, epilogue=None, extras=(), stages=()):
    M, K = a.shape
    sharded = b.ndim == 3
    ns = b.shape[2] if sharded else b.shape[1]
    N = ns * b.shape[0] if sharded else ns
    tm = _pick(M, (1408, 512, 256, 128))
    tn = _pick(ns, (512, 256, 128))
    tk = _pick(K, (2048, 1024, 512, 256, 128))
    per = ns // tn
    if sharded:
        b_spec = pl.BlockSpec((None, tk, tn), lambda i, j, k: (j // per, k, j % per))
    else:
        b_spec = pl.BlockSpec((tk, tn), lambda i, j, k: (k, j))
    mn = pl.BlockSpec((tm, tn), lambda i, j, k: (i, j))
    return _matmul("nn", a, b, grid=(M // tm, N // tn, K // tk), a_spec=pl.BlockSpec((tm, tk), lambda i, j, k: (i, k)),
                   b_spec=b_spec, out_spec=mn, out_shape=jax.ShapeDtypeStruct((M, N), out_dtype), acc_shape=(tm, tn),
                   name=name, a_pro=a_pro, epilogue=epilogue, extras=extras, extra_specs=[mn] * len(extras), stages=stages)


def mm_nt(a, w, *, out_dtype, name, epilogue=None, extras=(), stages=()):
    M, N = a.shape
    sharded = w.ndim == 3
    kw = w.shape[1] if sharded else w.shape[0]
    ns = w.shape[2] if sharded else w.shape[1]
    tm = _pick(M, (1408, 512, 256, 128))
    tkw = _pick(kw, (1024, 512, 256, 128))
    tk = _pick(ns, (2048, 1280, 1024, 512, 256, 128))
    per = ns // tk
    if sharded:
        w_spec = pl.BlockSpec((None, tkw, tk), lambda i, j, k: (k // per, j, k % per))
    else:
        w_spec = pl.BlockSpec((tkw, tk), lambda i, j, k: (j, k))
    mo = pl.BlockSpec((tm, tkw), lambda i, j, k: (i, j))
    return _matmul("nt", a, w, grid=(M // tm, kw // tkw, N // tk), a_spec=pl.BlockSpec((tm, tk), lambda i, j, k: (i, k)),
                   b_spec=w_spec, out_spec=mo, out_shape=jax.ShapeDtypeStruct((M, kw), out_dtype), acc_shape=(tm, tkw),
                   name=name, epilogue=epilogue, extras=extras, extra_specs=[mo] * len(extras), stages=stages)


def mm_tn(a, g, *, shards, name, a_pro=None, stages=()):
    T, kw = a.shape
    N = g.shape[1]
    ns = N // shards
    tt = _pick(T, (1408, 512, 256, 128))
    tkw = _pick(kw, (1024, 512, 256, 128))
    tn = _pick(ns, (1280, 1024, 512, 256, 128))
    per = ns // tn
    if shards > 1:
        out_spec = pl.BlockSpec((None, tkw, tn), lambda i, j, k: (j // per, i, j % per))
        out_shape = jax.ShapeDtypeStruct((shards, kw, ns), F32)
    else:
        out_spec = pl.BlockSpec((tkw, tn), lambda i, j, k: (i, j))
        out_shape = jax.ShapeDtypeStruct((kw, N), F32)
    return _matmul("tn", a, g, grid=(kw // tkw, N // tn, T // tt), a_spec=pl.BlockSpec((tt, tkw), lambda i, j, k: (k, i)),
                   b_spec=pl.BlockSpec((tt, tn), lambda i, j, k: (k, j)), out_spec=out_spec, out_shape=out_shape,
                   acc_shape=(tkw, tn), name=name, a_pro=a_pro, stages=stages)


def _relu_sq(a):
    r = jnp.maximum(a.astype(F32), 0.0)
    return r * r


def rmsnorm_fwd(h, g, *, name):
    Tp, D = h.shape
    tr = _pick(Tp, (384, 256, 128))

    def body(h_ref, g_ref, u_ref):
        x = h_ref[...]
        r = lax.rsqrt(jnp.mean(x * x, axis=-1, keepdims=True) + NORM_EPS)
        u_ref[...] = ((x * r) * g_ref[...]).astype(BF16)

    row = pl.BlockSpec((tr, D), lambda i: (i, 0))
    return pl.pallas_call(body, grid=(Tp // tr,), in_specs=[row, pl.BlockSpec((1, D), lambda i: (0, 0))], out_specs=row,
                          out_shape=jax.ShapeDtypeStruct((Tp, D), BF16), name=name, compiler_params=_params("parallel"))(h, g)


def _rms_bwd_math(x, g, dy):
    r = lax.rsqrt(jnp.mean(x * x, axis=-1, keepdims=True) + NORM_EPS)
    xh = x * r
    dyg = dy * g
    dx = r * (dyg - xh * jnp.mean(dyg * xh, axis=-1, keepdims=True))
    return dx, xh


def final_loss(h2, target, gf, *, name):
    Tp, D = h2.shape
    nt = Tp // ROW_TILE

    def body(h_ref, t_ref, g_ref, dh_ref, dhb_ref, st_ref):
        i = pl.program_id(0)

        @pl.when(i == 0)
        def _():
            st_ref[...] = jnp.zeros_like(st_ref)

        x = h_ref[...]
        g = g_ref[...]
        r = lax.rsqrt(jnp.mean(x * x, axis=-1, keepdims=True) + NORM_EPS)
        xh = x * r
        err = jnp.where(i == nt - 1, 0.0, xh * g - t_ref[...])
        dout = err * (1.0 / D)
        dyg = dout * g
        dx = r * (dyg - xh * jnp.mean(dyg * xh, axis=-1, keepdims=True))
        dh_ref[...] = dx
        dhb_ref[...] = dx.astype(BF16)
        st_ref[0:1, :] += jnp.sum(dout * xh, axis=0, keepdims=True)
        st_ref[1:2, :] += jnp.sum(err * err, axis=0, keepdims=True) * (0.5 / D)

    row = pl.BlockSpec((ROW_TILE, D), lambda i: (i, 0))
    return pl.pallas_call(
        body, grid=(nt,),
        in_specs=[row, pl.BlockSpec((ROW_TILE, D), lambda i: (jnp.minimum(i, nt - 2), 0)), pl.BlockSpec((1, D), lambda i: (0, 0))],
        out_specs=[row, row, pl.BlockSpec((8, D), lambda i: (0, 0))],
        out_shape=[jax.ShapeDtypeStruct((Tp, D), F32), jax.ShapeDtypeStruct((Tp, D), BF16), jax.ShapeDtypeStruct((8, D), F32)],
        name=name, compiler_params=_params("arbitrary"))(h2, target, gf)


def rms_bwd(h, g, du, dres, *, name):
    Tp, D = h.shape
    tr = _pick(Tp, (384, 256, 128))

    def body(h_ref, g_ref, du_ref, dr_ref, dh_ref, dhb_ref, st_ref):
        @pl.when(pl.program_id(0) == 0)
        def _():
            st_ref[...] = jnp.zeros_like(st_ref)

        du_v = du_ref[...].astype(F32)
        dx, xh = _rms_bwd_math(h_ref[...], g_ref[...], du_v)
        dh = dr_ref[...] + dx
        dh_ref[...] = dh
        dhb_ref[...] = dh.astype(BF16)
        st_ref[0:1, :] += jnp.sum(du_v * xh, axis=0, keepdims=True)

    row = pl.BlockSpec((tr, D), lambda i: (i, 0))
    return pl.pallas_call(
        body, grid=(Tp // tr,), in_specs=[row, pl.BlockSpec((1, D), lambda i: (0, 0)), row, row],
        out_specs=[row, row, pl.BlockSpec((8, D), lambda i: (0, 0))],
        out_shape=[jax.ShapeDtypeStruct((Tp, D), F32), jax.ShapeDtypeStruct((Tp, D), BF16), jax.ShapeDtypeStruct((8, D), F32)],
        name=name, compiler_params=_params("arbitrary"))(h, g, du, dres)


def rms_bwd_input(h, g, du, dres, *, name, stages=()):
    Tp, D = h.shape
    nt = Tp // ROW_TILE

    def body(h_ref, g_ref, du_ref, dr_ref, gx_ref, gm_ref, st_ref):
        i = pl.program_id(0)

        @pl.when(i == 0)
        def _():
            st_ref[...] = jnp.zeros_like(st_ref)

        du_v = du_ref[...].astype(F32)
        dx, xh = _rms_bwd_math(h_ref[...], g_ref[...], du_v)
        dh = dr_ref[...] + dx
        st_ref[0:1, :] += jnp.sum(du_v * xh, axis=0, keepdims=True)

        @pl.when(i < nt - 1)
        def _():
            gx_ref[...] = dh

        @pl.when(i == nt - 1)
        def _():
            gm_ref[...] = dh[ROW_TILE - N_META:, :]

    row = pl.BlockSpec((ROW_TILE, D), lambda i: (i, 0))
    outs, staged = _staged_call(
        body, grid=(nt,), in_specs=[row, pl.BlockSpec((1, D), lambda i: (0, 0)), row, row],
        out_specs=[pl.BlockSpec((ROW_TILE, D), lambda i: (jnp.minimum(i, nt - 2), 0)), pl.BlockSpec((N_META, D), lambda i: (0, 0)),
                   pl.BlockSpec((8, D), lambda i: (0, 0))],
        out_shape=[jax.ShapeDtypeStruct((Tp - ROW_TILE, D), F32), jax.ShapeDtypeStruct((N_META, D), F32),
                   jax.ShapeDtypeStruct((8, D), F32)],
        scratch_shapes=[], name=name, semantics=("arbitrary",), inputs=[h, g, du, dres], stages=stages)
    return (outs, staged) if stages else outs


def _conv_taps(ext, cur, vec_ref, cs, rows):
    del rows
    x1 = pltpu.roll(ext, 1, 0)[HIST:, :]
    x2 = pltpu.roll(ext, 2, 0)[HIST:, :]
    x3 = pltpu.roll(ext, 3, 0)[HIST:, :]
    xc = (vec_ref[1:2, cs] + vec_ref[8:9, cs] * cur + vec_ref[7:8, cs] * x1 + vec_ref[6:7, cs] * x2 + vec_ref[5:6, cs] * x3)
    return xc, x1, x2, x3


def _window_sum_back(ext, w):
    s, sh = ext, 1
    while sh < w:
        s = s + pltpu.roll(s, sh, 0)
        sh *= 2
    return s[HIST:, :]


def _lru_gates(xc, wa, wx, vec_ref, cs, sp):
    xcb = xc.astype(BF16)
    r = _sigmoid(jnp.dot(xcb, wa, preferred_element_type=F32) + vec_ref[2:3, cs])
    ig = _sigmoid(jnp.dot(xcb, wx, preferred_element_type=F32) + vec_ref[3:4, cs])
    log_a = (-LRU_C) * r * sp
    a = jnp.exp(log_a)
    mult = jnp.sqrt(_neg_expm1(2.0 * log_a))
    return xcb, r, ig, a, mult


def mix_fwd(proj, pool_w, gate_a, gate_x, vecs, *, name, stages=()):
    Tp = proj.shape[0]
    D = proj.shape[1] // 5
    R = ROW_TILE
    nt = Tp // R
    H = D // HEAD_DIM
    PG = D // len(POOL_WINDOWS)

    def body(p_ref, pw_ref, wa_ref, wx_ref, vec_ref, m_ref, hs_ref, hist_p, hist_l, hcar, mtmp):
        i = pl.program_id(0)
        is_meta = i == 0

        @pl.when(is_meta)
        def _():
            hist_p[...] = jnp.zeros_like(hist_p)
            hist_l[...] = jnp.zeros_like(hist_l)
            hcar[...] = jnp.zeros_like(hcar)

        row, valid, t_log = _tile_masks(is_meta, R)

        for g, w in enumerate(POOL_WINDOWS):
            cs = slice(g * PG, (g + 1) * PG)
            v = p_ref[:, g * PG:(g + 1) * PG].astype(F32)
            ws = _window_sum_back(jnp.concatenate([hist_p[:, cs], v], axis=0), w)
            d = ws / _window_count(t_log, w) - v
            y = jnp.dot(d.astype(BF16), pw_ref[g], preferred_element_type=F32)
            gp = p_ref[:, 3 * D + g * PG:3 * D + (g + 1) * PG].astype(F32)
            mtmp[:, cs] = _sigmoid(gp) * (y * vec_ref[0:1, cs])
            hist_p[:, cs] = v[R - HIST:, :]

        for h in range(H):
            cs = slice(h * HEAD_DIM, (h + 1) * HEAD_DIM)
            vl = p_ref[:, D + h * HEAD_DIM:D + (h + 1) * HEAD_DIM].astype(F32)
            xc, _, _, _ = _conv_taps(jnp.concatenate([hist_l[:, cs], vl], axis=0), vl, vec_ref, cs, R)
            sp = _softplus_neg(vec_ref[4:5, cs])
            _, r, ig, a, mult = _lru_gates(xc, wa_ref[h], wx_ref[h], vec_ref, cs, sp)
            b = jnp.where(valid, mult * (ig * xc), 0.0)
            ca, cb = a, b
            sh = 1
            while sh < R:
                a_sh = jnp.where(row >= sh, pltpu.roll(ca, sh, 0), 1.0)
                b_sh = jnp.where(row >= sh, pltpu.roll(cb, sh, 0), 0.0)
                cb = cb + ca * b_sh
                ca = ca * a_sh
                sh *= 2
            hs = cb + ca * hcar[7:8, cs]
            hs_ref[:, cs] = hs
            hcar[:, cs] = hs[R - 8:, :]
            hist_l[:, cs] = vl[R - HIST:, :]
            ge, _ = _gelu_tanh(p_ref[:, 2 * D + h * HEAD_DIM:2 * D + (h + 1) * HEAD_DIM].astype(F32))
            gl = p_ref[:, 4 * D + h * HEAD_DIM:4 * D + (h + 1) * HEAD_DIM].astype(F32)
            m_ref[:, cs] = (mtmp[:, cs] + _sigmoid(gl) * (hs * ge)).astype(BF16)

    def tile(i):
        return (i + nt - 1) % nt

    full = lambda shape: pl.BlockSpec(shape, lambda i: (0,) * len(shape))
    outs, staged = _staged_call(
        body, grid=(nt,),
        in_specs=[pl.BlockSpec((R, 5 * D), lambda i: (tile(i), 0)), full(pool_w.shape), full(gate_a.shape), full(gate_x.shape),
                  full(vecs.shape)],
        out_specs=[pl.BlockSpec((R, D), lambda i: (tile(i), 0)), pl.BlockSpec((R, D), lambda i: (tile(i), 0))],
        out_shape=[jax.ShapeDtypeStruct((Tp, D), BF16), jax.ShapeDtypeStruct((Tp, D), F32)],
        scratch_shapes=[pltpu.VMEM((HIST, D), F32), pltpu.VMEM((HIST, D), F32), pltpu.VMEM((8, D), F32), pltpu.VMEM((R, D), F32)],
        name=name, semantics=("arbitrary",), inputs=[proj, pool_w, gate_a, gate_x, vecs], stages=stages)
    return (outs, staged) if stages else outs


def mix_bwd(proj, hs, dmerged, pool_w, gate_a, gate_x, vecs, *, name, stages=()):
    Tp = proj.shape[0]
    D = proj.shape[1] // 5
    R = ROW_TILE
    nt = Tp // R
    H = D // HEAD_DIM
    PG = D // len(POOL_WINDOWS)

    def body(p_ref, pprev_ref, hs_ref, hprev_ref, dm_ref, pw_ref, wa_ref, wx_ref, vec_ref,
             dp_ref, dpw_ref, dwa_ref, dwx_ref, dvec_ref, car_g, fut_dxc, fut_q):
        i = pl.program_id(0)
        is_meta = i == nt - 1

        @pl.when(i == 0)
        def _():
            dpw_ref[...] = jnp.zeros_like(dpw_ref)
            dwa_ref[...] = jnp.zeros_like(dwa_ref)
            dwx_ref[...] = jnp.zeros_like(dwx_ref)
            dvec_ref[...] = jnp.zeros_like(dvec_ref)
            car_g[...] = jnp.zeros_like(car_g)
            fut_dxc[...] = jnp.zeros_like(fut_dxc)
            fut_q[...] = jnp.zeros_like(fut_q)

        row, valid, t_log = _tile_masks(is_meta, R)
        keep_prev = jnp.logical_not(is_meta)

        def colsum(x):
            return jnp.sum(x, axis=0, keepdims=True)

        for g, w in enumerate(POOL_WINDOWS):
            cs = slice(g * PG, (g + 1) * PG)
            v = p_ref[:, g * PG:(g + 1) * PG].astype(F32)
            vprev = jnp.where(keep_prev, pprev_ref[:, g * PG:(g + 1) * PG].astype(F32), 0.0)
            cnt = _window_count(t_log, w)
            d = _window_sum_back(jnp.concatenate([vprev, v], axis=0), w) / cnt - v
            d_bf = d.astype(BF16)
            y = jnp.dot(d_bf, pw_ref[g], preferred_element_type=F32)
            scale = vec_ref[0:1, cs]
            sg = _sigmoid(p_ref[:, 3 * D + g * PG:3 * D + (g + 1) * PG].astype(F32))
            dm = dm_ref[:, cs].astype(F32)
            dpo = dm * sg
            dp_ref[:, 3 * D + g * PG:3 * D + (g + 1) * PG] = (dm * (y * scale) * sg * (1.0 - sg)).astype(BF16)
            dvec_ref[0:1, cs] += colsum(dpo * y)
            dy = (dpo * scale).astype(BF16)
            dd = _dot_nt(dy, pw_ref[g])
            dpw_ref[g] += _dot_tn(d_bf, dy)
            q = dd / cnt
            s, sh = jnp.concatenate([q, fut_q[:, cs]], axis=0), 1
            while sh < w:
                s = s + pltpu.roll(s, R + HIST - sh, 0)
                sh *= 2
            dp_ref[:, g * PG:(g + 1) * PG] = (s[:R, :] - dd).astype(BF16)
            fut_q[:, cs] = q[:HIST, :]

        for h in range(H):
            cs = slice(h * HEAD_DIM, (h + 1) * HEAD_DIM)
            pc = lambda blk: slice(blk * D + h * HEAD_DIM, blk * D + (h + 1) * HEAD_DIM)
            vl = p_ref[:, pc(1)].astype(F32)
            vlprev = jnp.where(keep_prev, pprev_ref[:, pc(1)].astype(F32), 0.0)
            xc, x1, x2, x3 = _conv_taps(jnp.concatenate([vlprev, vl], axis=0), vl, vec_ref, cs, R)
            lam = vec_ref[4:5, cs]
            sp = _softplus_neg(lam)
            xcb, r, ig, a, mult = _lru_gates(xc, wa_ref[h], wx_ref[h], vec_ref, cs, sp)
            hsv = hs_ref[:, cs]
            hprev = jnp.where(row >= 1, pltpu.roll(hsv, 1, 0), hprev_ref[HIST - 1:HIST, cs])
            vg = p_ref[:, pc(2)].astype(F32)
            ge, th = _gelu_tanh(vg)
            sgl = _sigmoid(p_ref[:, pc(4)].astype(F32))
            dm = dm_ref[:, cs].astype(F32)
            dlo = dm * sgl
            dp_ref[:, pc(4)] = (dm * (hsv * ge) * sgl * (1.0 - sgl)).astype(BF16)
            dp_ref[:, pc(2)] = (dlo * hsv * _gelu_tanh_grad(vg, th)).astype(BF16)
            ca = jnp.where(row < R - 1, pltpu.roll(a, R - 1, 0), 1.0)
            cb = dlo * ge
            sh = 1
            while sh < R:
                a_sh = jnp.where(row < R - sh, pltpu.roll(ca, R - sh, 0), 1.0)
                b_sh = jnp.where(row < R - sh, pltpu.roll(cb, R - sh, 0), 0.0)
                cb = cb + ca * b_sh
                ca = ca * a_sh
                sh *= 2
            G = cb + ca * car_g[0:1, cs]
            car_g[:, cs] = (a * G)[0:8, :]
            da = jnp.where(valid, G * hprev, 0.0)
            db = jnp.where(valid, G, 0.0)
            dmult = db * (ig * xc)
            dig = db * (mult * xc)
            dxc = db * (mult * ig)
            dlog_a = da * a - dmult * ((a * a) / mult)
            dvec_ref[4:5, cs] += colsum(dlog_a * r) * (-LRU_C)
            dr = dlog_a * ((-LRU_C) * sp)
            dpa = dr * r * (1.0 - r)
            dpx = dig * ig * (1.0 - ig)
            dpa_bf = dpa.astype(BF16)
            dpx_bf = dpx.astype(BF16)
            dwa_ref[h] += _dot_tn(xcb, dpa_bf)
            dwx_ref[h] += _dot_tn(xcb, dpx_bf)
            dvec_ref[2:3, cs] += colsum(dpa)
            dvec_ref[3:4, cs] += colsum(dpx)
            dxc = dxc + _dot_nt(dpa_bf, wa_ref[h]) + _dot_nt(dpx_bf, wx_ref[h])
            ext = jnp.concatenate([dxc, fut_dxc[:, cs]], axis=0)
            n = R + HIST
            dvl = (vec_ref[8:9, cs] * dxc + vec_ref[7:8, cs] * pltpu.roll(ext, n - 1, 0)[:R, :]
                   + vec_ref[6:7, cs] * pltpu.roll(ext, n - 2, 0)[:R, :] + vec_ref[5:6, cs] * pltpu.roll(ext, n - 3, 0)[:R, :])
            dp_ref[:, pc(1)] = dvl.astype(BF16)
            dvec_ref[1:2, cs] += colsum(dxc)
            dvec_ref[8:9, cs] += colsum(dxc * vl)
            dvec_ref[7:8, cs] += colsum(dxc * x1)
            dvec_ref[6:7, cs] += colsum(dxc * x2)
            dvec_ref[5:6, cs] += colsum(dxc * x3)
            fut_dxc[:, cs] = dxc[:HIST, :]

            @pl.when(is_meta)
            def _():
                dvec_ref[4:5, cs] = dvec_ref[4:5, cs] * (-_sigmoid(-lam))

    def tile(i):
        return (2 * nt - 2 - i) % nt

    def prev_blk(i):
        per = R // HIST
        return jnp.where(i == nt - 1, 0, jnp.where(i == nt - 2, Tp // HIST - 1, (nt - 2 - i) * per - 1))

    full = lambda shape: pl.BlockSpec(shape, lambda i: (0,) * len(shape))
    G_ = len(POOL_WINDOWS)
    outs, staged = _staged_call(
        body, grid=(nt,),
        in_specs=[pl.BlockSpec((R, 5 * D), lambda i: (tile(i), 0)), pl.BlockSpec((HIST, 5 * D), lambda i: (prev_blk(i), 0)),
                  pl.BlockSpec((R, D), lambda i: (tile(i), 0)), pl.BlockSpec((HIST, D), lambda i: (prev_blk(i), 0)),
                  pl.BlockSpec((R, D), lambda i: (tile(i), 0)),
                  full(pool_w.shape), full(gate_a.shape), full(gate_x.shape), full(vecs.shape)],
        out_specs=[pl.BlockSpec((R, 5 * D), lambda i: (tile(i), 0)), full((G_, PG, PG)), full((H, HEAD_DIM, HEAD_DIM)),
                   full((H, HEAD_DIM, HEAD_DIM)), full((16, D))],
        out_shape=[jax.ShapeDtypeStruct((Tp, 5 * D), BF16), jax.ShapeDtypeStruct((G_, PG, PG), F32),
                   jax.ShapeDtypeStruct((H, HEAD_DIM, HEAD_DIM), F32), jax.ShapeDtypeStruct((H, HEAD_DIM, HEAD_DIM), F32),
                   jax.ShapeDtypeStruct((16, D), F32)],
        scratch_shapes=[pltpu.VMEM((8, D), F32), pltpu.VMEM((HIST, D), F32), pltpu.VMEM((HIST, D), F32)],
        name=name, semantics=("arbitrary",), inputs=[proj, proj, hs, hs, dmerged, pool_w, gate_a, gate_x, vecs], stages=stages)
    return (outs, staged) if stages else outs


def _adamw_math(w, g, m, v):
    mn = ADAM_B1 * m + (1.0 - ADAM_B1) * g
    vn = ADAM_B2 * v + (1.0 - ADAM_B2) * (g * g)
    m_hat = mn / (1.0 - ADAM_B1 ** ADAM_STEP)
    v_hat = vn / (1.0 - ADAM_B2 ** ADAM_STEP)
    return -ADAM_LR * (m_hat / (jnp.sqrt(v_hat) + ADAM_EPS) + ADAM_WD * w), mn, vn


def adamw(w, g, m, v, *, name):
    rows, cols = w.shape
    tr = _pick(rows, (256, 128, 64, 32, 16, 8))

    def body(w_ref, g_ref, m_ref, v_ref, go_ref, d_ref, mo_ref, vo_ref):
        gv = g_ref[...]
        go_ref[...] = gv
        d_ref[...], mo_ref[...], vo_ref[...] = _adamw_math(w_ref[...], gv, m_ref[...], v_ref[...])

    blk = pl.BlockSpec((tr, cols), lambda i: (i, 0))
    sds = jax.ShapeDtypeStruct((rows, cols), F32)
    return pl.pallas_call(body, grid=(rows // tr,), in_specs=[blk] * 4, out_specs=[blk] * 4, out_shape=[sds] * 4, name=name,
                          compiler_params=_params("parallel"))(w, g, m, v)


def allgather8(block, *, name, reduce_sum=False):
    rows, cols = block.shape

    def body(x_ref, out_ref, *scratch):
        if reduce_sum:
            buf, send_sems, recv_sems, local_sem = scratch
        else:
            buf = out_ref
            send_sems, recv_sems, local_sem = scratch
        x, y, c, chips = _place()
        me, sibling = (x, y, c), (x, y, 1 - c)

        def slot(px, py, pc):
            return buf.at[4 * px + 2 * py + pc]

        def copy(k, blk, to, src=None):
            return pltpu.make_async_remote_copy(src_ref=slot(*blk) if src is None else src, dst_ref=slot(*blk),
                                                send_sem=send_sems.at[k], recv_sem=recv_sems.at[k], device_id=to, device_id_type=MESH)

        mine = pltpu.make_async_copy(x_ref, slot(*me), local_sem)
        mine.start()
        first = [copy(0, me, sibling, src=x_ref)]
        first += [copy(1 + j, me, (*chip, c), src=x_ref) for j, chip in enumerate(chips)]
        for cp in first:
            cp.start()
        passed = [copy(4 + j, (*chip, c), sibling) for j, chip in enumerate(chips)]
        for j, chip in enumerate(chips):
            copy(1 + j, (*chip, c), me).wait_recv()
            passed[j].start()
        copy(0, sibling, me).wait_recv()
        for j, chip in enumerate(chips):
            copy(4 + j, (*chip, 1 - c), me).wait_recv()
        for cp in first + passed:
            cp.wait_send()
        mine.wait()
        if reduce_sum:
            acc = buf[0]
            for d in range(1, 8):
                acc = acc + buf[d]
            out_ref[...] = acc

    sems = [pltpu.SemaphoreType.DMA((7,)), pltpu.SemaphoreType.DMA((7,)), pltpu.SemaphoreType.DMA]
    if reduce_sum:
        out_shape = jax.ShapeDtypeStruct((rows, cols), block.dtype)
        scratch = [pltpu.VMEM((8, rows, cols), block.dtype)] + sems
    else:
        out_shape = jax.ShapeDtypeStruct((8, rows, cols), block.dtype)
        scratch = sems
    return pl.pallas_call(body, out_shape=out_shape, in_specs=[pl.BlockSpec(memory_space=pltpu.VMEM)],
                          out_specs=pl.BlockSpec(memory_space=pltpu.VMEM), scratch_shapes=scratch, name=name,
                          compiler_params=pltpu.CompilerParams(vmem_limit_bytes=VMEM_LIMIT_BYTES))(block)


def cast_into_slot(w, chip_arr, *, name):
    _, r, cols = w.shape
    tr = _pick(r, (256, 128, 64, 32, 16))

    def body(chip_ref, w_ref, o_ref):
        del chip_ref
        o_ref[...] = w_ref[...].astype(BF16)

    grid_spec = pltpu.PrefetchScalarGridSpec(
        num_scalar_prefetch=1, grid=(2, r // tr),
        in_specs=[pl.BlockSpec((None, tr, cols), lambda h, i, chip: (h, i, 0))],
        out_specs=pl.BlockSpec((None, None, tr, cols), lambda h, i, chip: (chip[0], h, i, 0)))
    return pl.pallas_call(body, grid_spec=grid_spec, out_shape=jax.ShapeDtypeStruct((4, 2, r, cols), BF16), name=name,
                          compiler_params=_params("parallel", "parallel"))(chip_arr, w)


def gather_now(bufs, *, name):
    n = len(bufs)

    def body(*refs):
        outs = refs[n:2 * n]
        send_sems, recv_sems = refs[2 * n:]
        x, y, c, chips = _place()
        me = 2 * x + y
        first = [_remote(outs[t].at[me, c], outs[t].at[me, c], send_sems.at[t, j], recv_sems.at[t, j], (px, py, c))
                 for t in range(n) for j, (px, py) in enumerate(chips)]
        for cp in first:
            cp.start()
        passed = []
        for t in range(n):
            for j, (px, py) in enumerate(chips):
                got = outs[t].at[2 * px + py, c]
                _remote(got, got, send_sems.at[t, j], recv_sems.at[t, j], (px, py, c)).wait_recv()
                passed.append(_remote(got, got, send_sems.at[t, 3 + j], recv_sems.at[t, 3 + j], (x, y, 1 - c)))
                passed[-1].start()
        for t in range(n):
            for j, (px, py) in enumerate(chips):
                sib = outs[t].at[2 * px + py, 1 - c]
                _remote(sib, sib, send_sems.at[t, 3 + j], recv_sems.at[t, 3 + j], (x, y, 1 - c)).wait_recv()
        for cp in first + passed:
            cp.wait_send()

    return pl.pallas_call(
        body, out_shape=[_sds(b) for b in bufs], in_specs=_any_specs(n), out_specs=_any_specs(n),
        scratch_shapes=[pltpu.SemaphoreType.DMA((n, 6)), pltpu.SemaphoreType.DMA((n, 6))],
        input_output_aliases={t: t for t in range(n)}, name=name)(*bufs)


def chip_presum(grad, recv, c_arr, *, name):
    _, _, r, cols = grad.shape
    tr = _pick(r, (256, 128, 64, 32, 16))

    def body(c_ref, g_ref, r_ref, o_ref):
        del c_ref
        o_ref[...] = (g_ref[...] + r_ref[...]).astype(BF16)

    grid_spec = pltpu.PrefetchScalarGridSpec(
        num_scalar_prefetch=1, grid=(4, r // tr),
        in_specs=[pl.BlockSpec((None, None, tr, cols), lambda k, i, c_ref: (k, c_ref[0], i, 0)),
                  pl.BlockSpec((None, tr, cols), lambda k, i, c_ref: (k, i, 0))],
        out_specs=pl.BlockSpec((None, tr, cols), lambda k, i, c_ref: (k, i, 0)))
    return pl.pallas_call(body, grid_spec=grid_spec, out_shape=jax.ShapeDtypeStruct((4, r, cols), BF16), name=name,
                          compiler_params=_params("parallel", "parallel"))(c_arr, grad, recv)


def final_half(grad, recv, got, mc_arr, *, name):
    _, _, r, cols = grad.shape
    tr = _pick(r, (256, 128, 64, 32, 16))

    def body(mc_ref, g_ref, r_ref, q_ref, o_ref):
        del mc_ref
        acc = g_ref[...] + r_ref[...]
        for j in range(3):
            acc = acc + q_ref[j].astype(F32)
        o_ref[...] = acc

    grid_spec = pltpu.PrefetchScalarGridSpec(
        num_scalar_prefetch=1, grid=(r // tr,),
        in_specs=[pl.BlockSpec((None, None, tr, cols), lambda i, mc: (mc[0], mc[1], i, 0)),
                  pl.BlockSpec((None, tr, cols), lambda i, mc: (mc[0], i, 0)),
                  pl.BlockSpec((3, tr, cols), lambda i, mc: (0, i, 0))],
        out_specs=pl.BlockSpec((tr, cols), lambda i, mc: (i, 0)))
    return pl.pallas_call(body, grid_spec=grid_spec, out_shape=jax.ShapeDtypeStruct((r, cols), F32), name=name,
                          compiler_params=_params("parallel"))(mc_arr, grad, recv, got)


def adamw_halves(w, mine, theirs, m, v, c_arr, *, name):
    _, r, cols = w.shape
    tr = _pick(r, (256, 128, 64, 32, 16, 8))

    def body(c_ref, w_ref, mine_ref, theirs_ref, m_ref, v_ref, go_ref, d_ref, mo_ref, vo_ref):
        gv = jnp.where(pl.program_id(0) == c_ref[0], mine_ref[...], theirs_ref[...])
        go_ref[...] = gv
        d_ref[...], mo_ref[...], vo_ref[...] = _adamw_math(w_ref[...], gv, m_ref[...], v_ref[...])

    blk = pl.BlockSpec((None, tr, cols), lambda h, i, c_ref: (h, i, 0))
    grid_spec = pltpu.PrefetchScalarGridSpec(
        num_scalar_prefetch=1, grid=(2, r // tr),
        in_specs=[blk, pl.BlockSpec((tr, cols), lambda h, i, c_ref: (jnp.where(h == c_ref[0], i, 0), 0)),
                  pl.BlockSpec((tr, cols), lambda h, i, c_ref: (jnp.where(h == c_ref[0], 0, i), 0)), blk, blk],
        out_specs=[blk] * 4)
    sds = jax.ShapeDtypeStruct((2, r, cols), F32)
    return pl.pallas_call(body, grid_spec=grid_spec, out_shape=[sds] * 4, name=name,
                          compiler_params=_params("parallel", "parallel"))(c_arr, w, mine, theirs, m, v)


def _pad_rows(a, rows):
    return jnp.pad(a, ((0, rows - a.shape[0]), (0, 0)))


def kernel(x, meta_tokens, norm1_g, w_in, pool_w, pool_scale, conv_w, conv_b, gate_a_w, gate_a_b, gate_x_w, gate_x_b, lru_lambda, w_out, norm2_g, mlp_w1, mlp_w2, final_g, loss_target, m_meta_tokens, m_norm1_g, m_w_in, m_pool_w, m_pool_scale, m_conv_w, m_conv_b, m_gate_a_w, m_gate_a_b, m_gate_x_w, m_gate_x_b, m_lru_lambda, m_w_out, m_norm2_g, m_mlp_w1, m_mlp_w2, m_final_g, v_meta_tokens, v_norm1_g, v_w_in, v_pool_w, v_pool_scale, v_conv_w, v_conv_b, v_gate_a_w, v_gate_a_b, v_gate_x_w, v_gate_x_b, v_lru_lambda, v_w_out, v_norm2_g, v_mlp_w1, v_mlp_w2, v_final_g):
    D = x.shape[-1]
    H = D // HEAD_DIM
    G = len(POOL_WINDOWS)
    PG = D // G
    ax, ay, ac = lax.axis_index("x"), lax.axis_index("y"), lax.axis_index("c")
    chip = 2 * ax + ay
    dshard = D // 4

    c_arr = jnp.reshape(ac, (1,)).astype(jnp.int32)
    chip_arr = jnp.reshape(chip, (1,)).astype(jnp.int32)
    mc_arr = jnp.stack([chip, ac]).astype(jnp.int32)
    names = ["w_in", "pool_w", "gate_a_w", "gate_x_w", "w_out", "mlp_w1", "mlp_w2"]
    big = [w_in, pool_w, gate_a_w, gate_x_w, w_out, mlp_w1, mlp_w2]

    def halves(w):
        w2d = w.reshape(-1, w.shape[-1])
        return w2d.reshape(2, w2d.shape[0] // 2, w2d.shape[1])

    bufs = [cast_into_slot(halves(w), chip_arr, name="cast_" + nm) for w, nm in zip(big, names)]
    b_win, b_pool, b_ga, b_gx = gather_now(bufs[0:4], name="gather_first")
    w_in_f = b_win.reshape(4, w_in.shape[1], w_in.shape[2])
    pool_f = b_pool.reshape(4, G, PG // 4, PG).transpose(1, 0, 2, 3).reshape(G, PG, PG)
    ga_f = b_ga.reshape(4, H, HEAD_DIM // 4, HEAD_DIM).transpose(1, 0, 2, 3).reshape(H, HEAD_DIM, HEAD_DIM)
    gx_f = b_gx.reshape(4, H, HEAD_DIM // 4, HEAD_DIM).transpose(1, 0, 2, 3).reshape(H, HEAD_DIM, HEAD_DIM)

    small_in = jnp.concatenate([meta_tokens, _pad_rows(conv_w[0], 8), _pad_rows(gate_a_b.reshape(1, dshard), 8),
                                _pad_rows(gate_x_b.reshape(1, dshard), 8)], axis=0)
    sm = allgather8(small_in, name="gather_small")[0::2]
    meta_f = sm[:, 0:16].transpose(1, 0, 2).reshape(N_META, D)
    conv_w_f = sm[:, 16:20].transpose(1, 0, 2).reshape(4, D)
    hd4 = HEAD_DIM // 4
    ba_f = sm[:, 24].reshape(4, H, hd4).transpose(1, 0, 2).reshape(1, D)
    bx_f = sm[:, 32].reshape(4, H, hd4).transpose(1, 0, 2).reshape(1, D)
    vecs = jnp.zeros((16, D), F32)
    for r0, part in ((0, pool_scale), (1, conv_b), (2, ba_f), (3, bx_f), (4, lru_lambda), (5, conv_w_f)):
        vecs = lax.dynamic_update_slice(vecs, part, (r0, 0))

    def chipwise(g, n_blocks, rows):
        return g.reshape(n_blocks, 4, rows, g.shape[-1]).transpose(1, 0, 2, 3).reshape(4, n_blocks * rows, g.shape[-1])

    def split2(g):
        return g.reshape(4, 2, g.shape[1] // 2, g.shape[2])

    def presum(t, g, r):
        return chip_presum(g, r, c_arr, name="presum_" + names[t])

    def total(t, g, r, q):
        return final_half(g, r, q, mc_arr, name="sum_" + names[t])

    xs, target, gfin = x[0], loss_target[0], final_g.reshape(1, D)
    h0 = jnp.concatenate([xs, jnp.zeros((ROW_TILE - N_META, D), F32), meta_f], axis=0)
    u = rmsnorm_fwd(h0, norm1_g, name="norm1")
    proj, [[b_wout, b_w1]] = mm_nn(u, w_in_f, out_dtype=BF16, name="proj", stages=[stage_gather_ici([bufs[4], bufs[5]])])
    (merged, hs), [[b_wout, b_w1], [b_w2]] = mix_fwd(proj, pool_f, ga_f, gx_f, vecs, name="mix_fwd",
                                                     stages=[stage_gather_d2d([b_wout, b_w1]), stage_gather_ici([bufs[6]])])
    w_out_f = b_wout.reshape(4 * w_out.shape[1], w_out.shape[2])
    w1_f = b_w1.reshape(4, mlp_w1.shape[1], mlp_w1.shape[2])
    h1, [[b_w2]] = mm_nn(merged, w_out_f, out_dtype=F32, name="out_proj", epilogue=lambda r, res: r + res, extras=(h0,),
                         stages=[stage_gather_d2d([b_w2])])
    w2_f = b_w2.reshape(4 * mlp_w2.shape[1], mlp_w2.shape[2])
    u2 = rmsnorm_fwd(h1, norm2_g, name="norm2")
    a1 = mm_nn(u2, w1_f, out_dtype=BF16, name="mlp_up")
    h2 = mm_nn(a1, w2_f, out_dtype=F32, name="mlp_down", a_pro=_relu_sq, epilogue=lambda r, res: r + res, extras=(h1,))
    dh2, dh2_bf, st_f = final_loss(h2, target, gfin, name="final_loss")

    da1 = mm_nt(dh2_bf, w2_f, out_dtype=BF16, name="mlp_down_dx",
                epilogue=lambda r, a: r * (2.0 * jnp.maximum(a.astype(F32), 0.0)), extras=(a1,))
    d_w2 = mm_tn(a1, dh2_bf, shards=1, name="mlp_down_dw", a_pro=_relu_sq)
    g6 = split2(d_w2.reshape(4, -1, d_w2.shape[-1]))
    d_w1, [[r6]] = mm_tn(u2, da1, shards=4, name="mlp_up_dw", stages=[stage_to_sibling([g6])])
    g5 = split2(d_w1)
    p6 = presum(6, g6, r6)
    du2, [[q6], [r5]] = mm_nt(da1, w1_f, out_dtype=F32, name="mlp_up_dx", stages=[stage_to_chips([p6]), stage_to_sibling([g5])])
    p5 = presum(5, g5, r5)
    f6 = total(6, g6, r6, q6)
    dh1, dh1_bf, st_2 = rms_bwd(h1, norm2_g, du2, dh2, name="norm2_bwd")
    dmerged, [[o6]] = mm_nt(dh1_bf, w_out_f, out_dtype=BF16, name="out_proj_dx", stages=[stage_from_sibling([f6])])
    d_wout = mm_tn(merged, dh1_bf, shards=1, name="out_proj_dw")
    g4 = split2(d_wout.reshape(4, -1, d_wout.shape[-1]))
    (dproj, d_pool, d_ga, d_gx, d_vecs), [[q5], [r4]] = mix_bwd(proj, hs, dmerged, pool_f, ga_f, gx_f, vecs, name="mix_bwd",
                                                               stages=[stage_to_chips([p5]), stage_to_sibling([g4])])
    p4 = presum(4, g4, r4)
    f5 = total(5, g5, r5, q5)
    g1, g2, g3 = split2(chipwise(d_pool, G, PG // 4)), split2(chipwise(d_ga, H, hd4)), split2(chipwise(d_gx, H, hd4))
    d_win, [[r1, r2, r3], [o5], [q4]] = mm_tn(u, dproj, shards=4, name="proj_dw",
                                             stages=[stage_to_sibling([g1, g2, g3]), stage_from_sibling([f5]), stage_to_chips([p4])])
    g0 = split2(d_win)
    [[r0]] = comm_call([stage_to_sibling([g0])], name="w_in_grad_to_sibling")
    p0, p1, p2, p3 = presum(0, g0, r0), presum(1, g1, r1), presum(2, g2, r2), presum(3, g3, r3)
    f4 = total(4, g4, r4, q4)
    du, [[q1, q2, q3, q0], [o4]] = mm_nt(dproj, w_in_f, out_dtype=F32, name="proj_dx",
                                        stages=[stage_to_chips([p1, p2, p3, p0]), stage_from_sibling([f4])])
    f0, f1, f2, f3 = total(0, g0, r0, q0), total(1, g1, r1, q1), total(2, g2, r2, q2), total(3, g3, r3, q3)
    (grad_x, d_meta, st_1), [[o0, o1, o2, o3]] = rms_bwd_input(h0, norm1_g, du, dh1, name="norm1_bwd",
                                                              stages=[stage_from_sibling([f0, f1, f2, f3])])
    mine = [f0, f1, f2, f3, f4, f5, f6]
    theirs = [o0, o1, o2, o3, o4, o5, o6]

    small = jnp.concatenate([d_meta, d_vecs, st_1, st_2, st_f], axis=0)
    tot = allgather8(small, name="sum_small", reduce_sum=True)
    loss = jnp.sum(tot[49])
    g_meta = lax.dynamic_slice_in_dim(tot[0:16], chip * dshard, dshard, axis=1)
    g_pool_scale, g_conv_b, g_lam = tot[16:17], tot[17:18], tot[20:21]
    g_ba = lax.dynamic_slice_in_dim(tot[18].reshape(H, HEAD_DIM), chip * hd4, hd4, axis=1)[None]
    g_bx = lax.dynamic_slice_in_dim(tot[19].reshape(H, HEAD_DIM), chip * hd4, hd4, axis=1)[None]
    g_conv_w = lax.dynamic_slice_in_dim(tot[21:25], chip * dshard, dshard, axis=1)[None]
    g_n1, g_n2, g_fin = tot[32:33], tot[40:41], tot[48]

    def step(w, g, m, v, nm):
        cols = w.shape[-1]
        outs = adamw(w.reshape(-1, cols), g.reshape(-1, cols), m.reshape(-1, cols), v.reshape(-1, cols), name="adamw_" + nm)
        return [o.reshape(w.shape) for o in outs]

    def step_big(t, w, m, v):
        outs = adamw_halves(halves(w), mine[t], theirs[t], halves(m), halves(v), c_arr, name="adamw_" + names[t])
        return [o.reshape(w.shape) for o in outs]

    res = dict(meta_tokens=step(meta_tokens, g_meta, m_meta_tokens, v_meta_tokens, "meta_tokens"),
               norm1_g=step(norm1_g, g_n1, m_norm1_g, v_norm1_g, "norm1_g"),
               w_in=step_big(0, w_in, m_w_in, v_w_in), pool_w=step_big(1, pool_w, m_pool_w, v_pool_w),
               pool_scale=step(pool_scale, g_pool_scale, m_pool_scale, v_pool_scale, "pool_scale"),
               conv_w=step(conv_w, g_conv_w, m_conv_w, v_conv_w, "conv_w"), conv_b=step(conv_b, g_conv_b, m_conv_b, v_conv_b, "conv_b"),
               gate_a_w=step_big(2, gate_a_w, m_gate_a_w, v_gate_a_w), gate_a_b=step(gate_a_b, g_ba, m_gate_a_b, v_gate_a_b, "gate_a_b"),
               gate_x_w=step_big(3, gate_x_w, m_gate_x_w, v_gate_x_w), gate_x_b=step(gate_x_b, g_bx, m_gate_x_b, v_gate_x_b, "gate_x_b"),
               lru_lambda=step(lru_lambda, g_lam, m_lru_lambda, v_lru_lambda, "lru_lambda"), w_out=step_big(4, w_out, m_w_out, v_w_out),
               norm2_g=step(norm2_g, g_n2, m_norm2_g, v_norm2_g, "norm2_g"), mlp_w1=step_big(5, mlp_w1, m_mlp_w1, v_mlp_w1),
               mlp_w2=step_big(6, mlp_w2, m_mlp_w2, v_mlp_w2), final_g=step(final_g, g_fin, m_final_g, v_final_g, "final_g"))
    order = list(res)
    return (loss, grad_x[None], *[res[n][0] for n in order], *[res[n][1] for n in order], *[res[n][2] for n in order],
            *[res[n][3] for n in order])
```

```python
import functools

import jax
import jax.numpy as jnp
from jax import lax
from jax.experimental import pallas as pl
from jax.experimental.pallas import tpu as pltpu

F32 = jnp.float32
BF16 = jnp.bfloat16
MESH = pl.DeviceIdType.MESH

NORM_EPS = 1e-6
N_META = 16
HEAD_DIM = 256
POOL_WINDOWS = (2, 4, 8, 16)
LRU_C = 8.0
ROW_TILE = 128
HIST = 16
VMEM_LIMIT_BYTES = 56 * 1024 * 1024
ADAM_LR, ADAM_B1, ADAM_B2, ADAM_EPS, ADAM_WD, ADAM_STEP = 0.001, 0.9, 0.999, 1e-08, 0.01, 10


def _pick(n, prefs):
    for p in prefs:
        if n % p == 0:
            return p
    return n


def _params(*sem):
    return pltpu.CompilerParams(dimension_semantics=sem, vmem_limit_bytes=VMEM_LIMIT_BYTES)


def _sigmoid(x):
    return 1.0 / (1.0 + jnp.exp(-x))


def _gelu_tanh(x):
    t = jnp.tanh(0.7978845608028654 * (x + 0.044715 * (x * x * x)))
    return 0.5 * x * (1.0 + t), t


def _gelu_tanh_grad(x, t):
    return 0.5 * (1.0 + t) + 0.5 * x * (1.0 - t * t) * (0.7978845608028654 * (1.0 + 3.0 * 0.044715 * x * x))


def _neg_expm1(x, exp_x):
    series = x * (-1.0 + x * (-0.5 + x * ((-1.0 / 6.0) + x * ((-1.0 / 24.0) + x * (-1.0 / 120.0)))))
    return jnp.where(x > -0.125, series, 1.0 - exp_x)


def _softplus_neg(lam):
    z = jnp.exp(-jnp.abs(lam))
    log1p_z = jnp.where(z < 0.01, z * (1.0 - z * (0.5 - z * (1.0 / 3.0))), jnp.log(1.0 + z))
    return jnp.maximum(-lam, 0.0) + log1p_z


def _tile_masks(is_meta, rows):
    row = lax.broadcasted_iota(jnp.int32, (rows, 1), 0)
    valid = jnp.logical_or(jnp.logical_not(is_meta), row >= rows - N_META)
    t_log = jnp.where(is_meta, row - (rows - N_META), 1 << 20)
    return row, valid, t_log


def _window_count_inv(t_log, w):
    return 1.0 / jnp.clip(t_log + 1, 1, w).astype(F32)


def _dot_nt(a, b):
    return lax.dot_general(a, b, (((1,), (1,)), ((), ())), preferred_element_type=F32)


def _dot_tn(a, b):
    return lax.dot_general(a, b, (((0,), (0,)), ((), ())), preferred_element_type=F32)


def _place():
    x, y, c = lax.axis_index("x"), lax.axis_index("y"), lax.axis_index("c")
    chips = [(1 - x, y), (x, 1 - y), (1 - x, 1 - y)]
    return x, y, c, chips


def _remote(src, dst, send_sem, recv_sem, to):
    return pltpu.make_async_remote_copy(src_ref=src, dst_ref=dst, send_sem=send_sem, recv_sem=recv_sem, device_id=to,
                                        device_id_type=MESH)


class Stage:
    def __init__(self, arrays, out_shapes, aliases, n_copies, copies):
        self.arrays, self.out_shapes, self.aliases, self.n_copies, self.copies = list(arrays), list(out_shapes), aliases, n_copies, copies


def _sds(a):
    return jax.ShapeDtypeStruct(a.shape, a.dtype)


def stage_gather_ici(bufs):
    n = len(bufs)

    def copies(ins, outs, send, recv):
        x, y, c, chips = _place()
        me = 2 * x + y
        sends, recvs = [], []
        for t in range(n):
            for j, (px, py) in enumerate(chips):
                k = 3 * t + j
                mine, theirs = outs[t].at[me, c], outs[t].at[2 * px + py, c]
                sends.append(_remote(mine, mine, send.at[k], recv.at[k], (px, py, c)))
                recvs.append(_remote(theirs, theirs, send.at[k], recv.at[k], (px, py, c)))
        return sends, recvs

    return Stage(bufs, [_sds(b) for b in bufs], {t: t for t in range(n)}, 3 * n, copies)


def stage_gather_d2d(bufs):
    n = len(bufs)

    def copies(ins, outs, send, recv):
        x, y, c, chips = _place()
        sends, recvs = [], []
        for t in range(n):
            for j, (px, py) in enumerate(chips):
                k = 3 * t + j
                got, sib = outs[t].at[2 * px + py, c], outs[t].at[2 * px + py, 1 - c]
                sends.append(_remote(got, got, send.at[k], recv.at[k], (x, y, 1 - c)))
                recvs.append(_remote(sib, sib, send.at[k], recv.at[k], (x, y, 1 - c)))
        return sends, recvs

    return Stage(bufs, [_sds(b) for b in bufs], {t: t for t in range(n)}, 3 * n, copies)


def stage_to_sibling(grads):
    n = len(grads)

    def copies(ins, outs, send, recv):
        x, y, c, _ = _place()
        sends, recvs = [], []
        for t in range(n):
            for k4 in range(4):
                k = 4 * t + k4
                sends.append(_remote(ins[t].at[k4, 1 - c], outs[t].at[k4], send.at[k], recv.at[k], (x, y, 1 - c)))
                recvs.append(_remote(outs[t].at[k4], outs[t].at[k4], send.at[k], recv.at[k], (x, y, 1 - c)))
        return sends, recvs

    return Stage(grads, [jax.ShapeDtypeStruct((4, *g.shape[2:]), g.dtype) for g in grads], {}, 4 * n, copies)


def stage_to_chips(presums):
    n = len(presums)

    def copies(ins, outs, send, recv):
        x, y, c, chips = _place()
        sends, recvs = [], []
        for t in range(n):
            for j, (px, py) in enumerate(chips):
                k = 3 * t + j
                sends.append(_remote(ins[t].at[2 * px + py], outs[t].at[j], send.at[k], recv.at[k], (px, py, c)))
                recvs.append(_remote(outs[t].at[j], outs[t].at[j], send.at[k], recv.at[k], (px, py, c)))
        return sends, recvs

    return Stage(presums, [jax.ShapeDtypeStruct((3, *p.shape[1:]), p.dtype) for p in presums], {}, 3 * n, copies)


def stage_from_sibling(halves):
    n = len(halves)

    def copies(ins, outs, send, recv):
        x, y, c, _ = _place()
        sends = [_remote(ins[t], outs[t], send.at[t], recv.at[t], (x, y, 1 - c)) for t in range(n)]
        recvs = [_remote(outs[t], outs[t], send.at[t], recv.at[t], (x, y, 1 - c)) for t in range(n)]
        return sends, recvs

    return Stage(halves, [_sds(h) for h in halves], {}, n, copies)


def _any_specs(n):
    return [pl.BlockSpec(memory_space=pl.ANY)] * n


def _staged_call(body, *, grid, in_specs, out_specs, out_shape, scratch_shapes, name, semantics, inputs, stages=()):
    n_in, n_out, n_scr = len(in_specs), len(out_specs), len(scratch_shapes)
    st_arrays = [a for s in stages for a in s.arrays]
    st_outs = [o for s in stages for o in s.out_shapes]
    st_sems = [pltpu.SemaphoreType.DMA((s.n_copies,)) for s in stages for _ in range(2)]
    aliases = {}
    at_in, at_out = n_in, n_out
    for s in stages:
        for a, o in s.aliases.items():
            aliases[at_in + a] = at_out + o
        at_in += len(s.arrays)
        at_out += len(s.out_shapes)

    def full_body(*refs):
        pos = 0

        def take(count):
            nonlocal pos
            part = refs[pos:pos + count]
            pos += count
            return part

        ins, s_ins, outs, s_outs, scr = take(n_in), take(len(st_arrays)), take(n_out), take(len(st_outs)), take(n_scr)
        s_sems = refs[pos:]

        def each_stage(action):
            at_i = at_o = 0
            for idx, s in enumerate(stages):
                sends, recvs = s.copies(s_ins[at_i:at_i + len(s.arrays)], s_outs[at_o:at_o + len(s.out_shapes)],
                                        s_sems[2 * idx], s_sems[2 * idx + 1])
                action(sends, recvs)
                at_i += len(s.arrays)
                at_o += len(s.out_shapes)

        if stages:
            ids = [pl.program_id(a) for a in range(len(grid))]
            first = functools.reduce(jnp.logical_and, [i == 0 for i in ids])
            last = functools.reduce(jnp.logical_and, [i == g - 1 for i, g in zip(ids, grid)])

            def start(sends, recvs):
                for cp in sends:
                    cp.start()

            def finish(sends, recvs):
                for cp in recvs:
                    cp.wait_recv()
                for cp in sends:
                    cp.wait_send()

            @pl.when(first)
            def _():
                each_stage(start)

        body(*ins, *outs, *scr)

        if stages:
            @pl.when(last)
            def _():
                each_stage(finish)

    sem = tuple("arbitrary" for _ in grid) if stages else tuple(semantics)
    res = pl.pallas_call(
        full_body, grid=grid, in_specs=[*in_specs, *_any_specs(len(st_arrays))], out_specs=[*out_specs, *_any_specs(len(st_outs))],
        out_shape=[*out_shape, *st_outs], scratch_shapes=[*scratch_shapes, *st_sems], input_output_aliases=aliases, name=name,
        compiler_params=_params(*sem))(*inputs, *st_arrays)
    outs, rest = list(res[:n_out]), list(res[n_out:])
    per_stage = []
    for s in stages:
        per_stage.append(rest[:len(s.out_shapes)])
        rest = rest[len(s.out_shapes):]
    return outs, per_stage


def comm_call(stages, *, name):
    return _staged_call(lambda: None, grid=(1,), in_specs=[], out_specs=[], out_shape=[], scratch_shapes=[], name=name,
                        semantics=("arbitrary",), inputs=[], stages=stages)[1]


def _matmul(kind, a, b, *, grid, a_spec, b_spec, out_spec, out_shape, acc_shape, name,
            a_pro=None, epilogue=None, extras=(), extra_specs=(), stages=()):
    nk = grid[2]
    n_extra = len(extras)

    def body(a_ref, b_ref, *rest):
        extra_refs = rest[:n_extra]
        o_ref = rest[n_extra]
        av = a_ref[...]
        if a_pro is not None:
            av = a_pro(av)
        av = av.astype(BF16)
        bv = b_ref[...].astype(BF16)
        if kind == "nn":
            p = jnp.dot(av, bv, preferred_element_type=F32)
        elif kind == "nt":
            p = _dot_nt(av, bv)
        else:
            p = _dot_tn(av, bv)

        def finish(r):
            if epilogue is not None:
                r = epilogue(r, *[e[...] for e in extra_refs])
            o_ref[...] = r.astype(o_ref.dtype)

        if nk == 1:
            finish(p)
        else:
            acc_ref = rest[n_extra + 1]
            k = pl.program_id(2)

            @pl.when(k == 0)
            def _():
                acc_ref[...] = p

            @pl.when(k > 0)
            def _():
                acc_ref[...] += p

            @pl.when(k == nk - 1)
            def _():
                finish(acc_ref[...])

    scratch = [] if nk == 1 else [pltpu.VMEM(acc_shape, F32)]
    outs, staged = _staged_call(body, grid=grid, in_specs=[a_spec, b_spec, *extra_specs], out_specs=[out_spec], out_shape=[out_shape],
                                scratch_shapes=scratch, name=name, semantics=("parallel", "parallel", "arbitrary"),
                                inputs=[a, b, *extras], stages=stages)
    return (outs[0], staged) if stages else outs[0]


def mm_nn(a, b, *, out_dtype, name, a_pro=None, epilogue=None, extras=(), stages=()):
    M, K = a.shape
    sharded = b.ndim == 3
    ns = b.shape[2] if sharded else b.shape[1]
    N = ns * b.shape[0] if sharded else ns
    tm = _pick(M, (1408, 512, 256, 128))
    tn = _pick(ns, (512, 256, 128))
    tk = _pick(K, (2048, 1024, 512, 256, 128))
    per = ns // tn
    if sharded:
        b_spec = pl.BlockSpec((None, tk, tn), lambda i, j, k: (j // per, k, j % per))
    else:
        b_spec = pl.BlockSpec((tk, tn), lambda i, j, k: (k, j))
    mn = pl.BlockSpec((tm, tn), lambda i, j, k: (i, j))
    return _matmul("nn", a, b, grid=(M // tm, N // tn, K // tk), a_spec=pl.BlockSpec((tm, tk), lambda i, j, k: (i, k)),
                   b_spec=b_spec, out_spec=mn, out_shape=jax.ShapeDtypeStruct((M, N), out_dtype), acc_shape=(tm, tn),
                   name=name, a_pro=a_pro, epilogue=epilogue, extras=extras, extra_specs=[mn] * len(extras), stages=stages)


def mm_nt(a, w, *, out_dtype, name, epilogue=None, extras=(), stages=()):
    M, N = a.shape
    sharded = w.ndim == 3
    kw = w.shape[1] if sharded else w.shape[0]
    ns = w.shape[2] if sharded else w.shape[1]
    tm = _pick(M, (1408, 512, 256, 128))
    tkw = _pick(kw, (1024, 512, 256, 128))
    tk = _pick(ns, (2048, 1280, 1024, 512, 256, 128))
    per = ns // tk
    if sharded:
        w_spec = pl.BlockSpec((None, tkw, tk), lambda i, j, k: (k // per, j, k % per))
    else:
        w_spec = pl.BlockSpec((tkw, tk), lambda i, j, k: (j, k))
    mo = pl.BlockSpec((tm, tkw), lambda i, j, k: (i, j))
    return _matmul("nt", a, w, grid=(M // tm, kw // tkw, N // tk), a_spec=pl.BlockSpec((tm, tk), lambda i, j, k: (i, k)),
                   b_spec=w_spec, out_spec=mo, out_shape=jax.ShapeDtypeStruct((M, kw), out_dtype), acc_shape=(tm, tkw),
                   name=name, epilogue=epilogue, extras=extras, extra_specs=[mo] * len(extras), stages=stages)


def mm_tn(a, g, *, shards, name, a_pro=None, stages=()):
    T, kw = a.shape
    N = g.shape[1]
    ns = N // shards
    tt = _pick(T, (1408, 512, 256, 128))
    tkw = _pick(kw, (1024, 512, 256, 128))
    tn = _pick(ns, (1280, 1024, 512, 256, 128))
    per = ns // tn
    if shards > 1:
        out_spec = pl.BlockSpec((None, tkw, tn), lambda i, j, k: (j // per, i, j % per))
        out_shape = jax.ShapeDtypeStruct((shards, kw, ns), F32)
    else:
        out_spec = pl.BlockSpec((tkw, tn), lambda i, j, k: (i, j))
        out_shape = jax.ShapeDtypeStruct((kw, N), F32)
    return _matmul("tn", a, g, grid=(kw // tkw, N // tn, T // tt), a_spec=pl.BlockSpec((tt, tkw), lambda i, j, k: (k, i)),
                   b_spec=pl.BlockSpec((tt, tn), lambda i, j, k: (k, j)), out_spec=out_spec, out_shape=out_shape,
                   acc_shape=(tkw, tn), name=name, a_pro=a_pro, stages=stages)


def _relu_sq(a):
    r = jnp.maximum(a.astype(F32), 0.0)
    return r * r


def rmsnorm_fwd(h, g, *, name):
    Tp, D = h.shape
    tr = _pick(Tp, (384, 256, 128))

    def body(h_ref, g_ref, u_ref):
        x = h_ref[...]
        r = lax.rsqrt(jnp.mean(x * x, axis=-1, keepdims=True) + NORM_EPS)
        u_ref[...] = ((x * r) * g_ref[...]).astype(BF16)

    row = pl.BlockSpec((tr, D), lambda i: (i, 0))
    return pl.pallas_call(body, grid=(Tp // tr,), in_specs=[row, pl.BlockSpec((1, D), lambda i: (0, 0))], out_specs=row,
                          out_shape=jax.ShapeDtypeStruct((Tp, D), BF16), name=name, compiler_params=_params("parallel"))(h, g)


def _rms_bwd_math(x, g, dy):
    r = lax.rsqrt(jnp.mean(x * x, axis=-1, keepdims=True) + NORM_EPS)
    xh = x * r
    dyg = dy * g
    dx = r * (dyg - xh * jnp.mean(dyg * xh, axis=-1, keepdims=True))
    return dx, xh


def final_loss(h2, target, gf, *, name):
    Tp, D = h2.shape
    nt = Tp // ROW_TILE

    def body(h_ref, t_ref, g_ref, dh_ref, dhb_ref, st_ref):
        i = pl.program_id(0)

        @pl.when(i == 0)
        def _():
            st_ref[...] = jnp.zeros_like(st_ref)

        x = h_ref[...]
        g = g_ref[...]
        r = lax.rsqrt(jnp.mean(x * x, axis=-1, keepdims=True) + NORM_EPS)
        xh = x * r
        err = jnp.where(i == nt - 1, 0.0, xh * g - t_ref[...])
        dout = err * (1.0 / D)
        dyg = dout * g
        dx = r * (dyg - xh * jnp.mean(dyg * xh, axis=-1, keepdims=True))
        dh_ref[...] = dx
        dhb_ref[...] = dx.astype(BF16)
        st_ref[0:1, :] += jnp.sum(dout * xh, axis=0, keepdims=True)
        st_ref[1:2, :] += jnp.sum(err * err, axis=0, keepdims=True) * (0.5 / D)

    row = pl.BlockSpec((ROW_TILE, D), lambda i: (i, 0))
    return pl.pallas_call(
        body, grid=(nt,),
        in_specs=[row, pl.BlockSpec((ROW_TILE, D), lambda i: (jnp.minimum(i, nt - 2), 0)), pl.BlockSpec((1, D), lambda i: (0, 0))],
        out_specs=[row, row, pl.BlockSpec((8, D), lambda i: (0, 0))],
        out_shape=[jax.ShapeDtypeStruct((Tp, D), F32), jax.ShapeDtypeStruct((Tp, D), BF16), jax.ShapeDtypeStruct((8, D), F32)],
        name=name, compiler_params=_params("arbitrary"))(h2, target, gf)


def rms_bwd(h, g, du, dres, *, name):
    Tp, D = h.shape
    tr = _pick(Tp, (384, 256, 128))

    def body(h_ref, g_ref, du_ref, dr_ref, dh_ref, dhb_ref, st_ref):
        @pl.when(pl.program_id(0) == 0)
        def _():
            st_ref[...] = jnp.zeros_like(st_ref)

        du_v = du_ref[...].astype(F32)
        dx, xh = _rms_bwd_math(h_ref[...], g_ref[...], du_v)
        dh = dr_ref[...] + dx
        dh_ref[...] = dh
        dhb_ref[...] = dh.astype(BF16)
        st_ref[0:1, :] += jnp.sum(du_v * xh, axis=0, keepdims=True)

    row = pl.BlockSpec((tr, D), lambda i: (i, 0))
    return pl.pallas_call(
        body, grid=(Tp // tr,), in_specs=[row, pl.BlockSpec((1, D), lambda i: (0, 0)), row, row],
        out_specs=[row, row, pl.BlockSpec((8, D), lambda i: (0, 0))],
        out_shape=[jax.ShapeDtypeStruct((Tp, D), F32), jax.ShapeDtypeStruct((Tp, D), BF16), jax.ShapeDtypeStruct((8, D), F32)],
        name=name, compiler_params=_params("arbitrary"))(h, g, du, dres)


def rms_bwd_input(h, g, du, dres, *, name, stages=()):
    Tp, D = h.shape
    nt = Tp // ROW_TILE

    def body(h_ref, g_ref, du_ref, dr_ref, gx_ref, gm_ref, st_ref):
        i = pl.program_id(0)

        @pl.when(i == 0)
        def _():
            st_ref[...] = jnp.zeros_like(st_ref)

        du_v = du_ref[...].astype(F32)
        dx, xh = _rms_bwd_math(h_ref[...], g_ref[...], du_v)
        dh = dr_ref[...] + dx
        st_ref[0:1, :] += jnp.sum(du_v * xh, axis=0, keepdims=True)

        @pl.when(i < nt - 1)
        def _():
            gx_ref[...] = dh

        @pl.when(i == nt - 1)
        def _():
            gm_ref[...] = dh[ROW_TILE - N_META:, :]

    row = pl.BlockSpec((ROW_TILE, D), lambda i: (i, 0))
    outs, staged = _staged_call(
        body, grid=(nt,), in_specs=[row, pl.BlockSpec((1, D), lambda i: (0, 0)), row, row],
        out_specs=[pl.BlockSpec((ROW_TILE, D), lambda i: (jnp.minimum(i, nt - 2), 0)), pl.BlockSpec((N_META, D), lambda i: (0, 0)),
                   pl.BlockSpec((8, D), lambda i: (0, 0))],
        out_shape=[jax.ShapeDtypeStruct((Tp - ROW_TILE, D), F32), jax.ShapeDtypeStruct((N_META, D), F32),
                   jax.ShapeDtypeStruct((8, D), F32)],
        scratch_shapes=[], name=name, semantics=("arbitrary",), inputs=[h, g, du, dres], stages=stages)
    return (outs, staged) if stages else outs


def _conv_taps(ext, cur, vec_ref, cs, rows):
    del rows
    x1 = pltpu.roll(ext, 1, 0)[HIST:, :]
    x2 = pltpu.roll(ext, 2, 0)[HIST:, :]
    x3 = pltpu.roll(ext, 3, 0)[HIST:, :]
    xc = (vec_ref[1:2, cs] + vec_ref[8:9, cs] * cur + vec_ref[7:8, cs] * x1 + vec_ref[6:7, cs] * x2 + vec_ref[5:6, cs] * x3)
    return xc, x1, x2, x3


def _window_sum_back(ext, w):
    s, sh = ext, 1
    while sh < w:
        s = s + pltpu.roll(s, sh, 0)
        sh *= 2
    return s[HIST:, :]


def _scan_rows(a, b, carry, *, reverse):
    rows = a.shape[0]
    rin = jnp.bitwise_and(lax.broadcasted_iota(jnp.int32, (rows, 1), 0), 7)
    sh = 1
    while sh < 8:
        keep = (rin < 8 - sh) if reverse else (rin >= sh)
        amount = rows - sh if reverse else sh
        a_sh = jnp.where(keep, pltpu.roll(a, amount, 0), 1.0)
        b_sh = jnp.where(keep, pltpu.roll(b, amount, 0), 0.0)
        b = b + a * b_sh
        a = a * a_sh
        sh *= 2
    out = [None] * (rows // 8)
    for g in (reversed(range(rows // 8)) if reverse else range(rows // 8)):
        hg = b[8 * g:8 * g + 8, :] + a[8 * g:8 * g + 8, :] * carry
        carry = hg[0:1, :] if reverse else hg[7:8, :]
        out[g] = hg
    return jnp.concatenate(out, axis=0)


def _lru_gates(xc, wa, wx, vec_ref, cs, sp):
    xcb = xc.astype(BF16)
    r = _sigmoid(jnp.dot(xcb, wa, preferred_element_type=F32) + vec_ref[2:3, cs])
    ig = _sigmoid(jnp.dot(xcb, wx, preferred_element_type=F32) + vec_ref[3:4, cs])
    log_a = (-LRU_C) * r * sp
    a = jnp.exp(log_a)
    a2 = a * a
    return xcb, r, ig, a, a2, _neg_expm1(2.0 * log_a, a2)


def mix_fwd(proj, pool_w, gate_a, gate_x, vecs, *, name, stages=()):
    Tp = proj.shape[0]
    D = proj.shape[1] // 5
    R = ROW_TILE
    nt = Tp // R
    H = D // HEAD_DIM
    PG = D // len(POOL_WINDOWS)

    def body(p_ref, pw_ref, wa_ref, wx_ref, vec_ref, m_ref, hs_ref, hist_p, hist_l, hcar, mtmp):
        i = pl.program_id(0)
        is_meta = i == 0

        @pl.when(is_meta)
        def _():
            hist_p[...] = jnp.zeros_like(hist_p)
            hist_l[...] = jnp.zeros_like(hist_l)
            hcar[...] = jnp.zeros_like(hcar)

        row, valid, t_log = _tile_masks(is_meta, R)

        for g, w in enumerate(POOL_WINDOWS):
            cs = slice(g * PG, (g + 1) * PG)
            v = p_ref[:, g * PG:(g + 1) * PG].astype(F32)
            ws = _window_sum_back(jnp.concatenate([hist_p[:, cs], v], axis=0), w)
            d = ws * _window_count_inv(t_log, w) - v
            y = jnp.dot(d.astype(BF16), pw_ref[g], preferred_element_type=F32)
            gp = p_ref[:, 3 * D + g * PG:3 * D + (g + 1) * PG].astype(F32)
            mtmp[:, cs] = _sigmoid(gp) * (y * vec_ref[0:1, cs])
            hist_p[:, cs] = v[R - HIST:, :]

        for h in range(H):
            cs = slice(h * HEAD_DIM, (h + 1) * HEAD_DIM)
            vl = p_ref[:, D + h * HEAD_DIM:D + (h + 1) * HEAD_DIM].astype(F32)
            xc, _, _, _ = _conv_taps(jnp.concatenate([hist_l[:, cs], vl], axis=0), vl, vec_ref, cs, R)
            sp = _softplus_neg(vec_ref[4:5, cs])
            _, r, ig, a, _, em = _lru_gates(xc, wa_ref[h], wx_ref[h], vec_ref, cs, sp)
            b = jnp.where(valid, jnp.sqrt(em) * (ig * xc), 0.0)
            hs = _scan_rows(a, b, hcar[7:8, cs], reverse=False)
            hs_ref[:, cs] = hs
            hcar[:, cs] = hs[R - 8:, :]
            hist_l[:, cs] = vl[R - HIST:, :]
            ge, _ = _gelu_tanh(p_ref[:, 2 * D + h * HEAD_DIM:2 * D + (h + 1) * HEAD_DIM].astype(F32))
            gl = p_ref[:, 4 * D + h * HEAD_DIM:4 * D + (h + 1) * HEAD_DIM].astype(F32)
            m_ref[:, cs] = (mtmp[:, cs] + _sigmoid(gl) * (hs * ge)).astype(BF16)

    def tile(i):
        return (i + nt - 1) % nt

    full = lambda shape: pl.BlockSpec(shape, lambda i: (0,) * len(shape))
    outs, staged = _staged_call(
        body, grid=(nt,),
        in_specs=[pl.BlockSpec((R, 5 * D), lambda i: (tile(i), 0)), full(pool_w.shape), full(gate_a.shape), full(gate_x.shape),
                  full(vecs.shape)],
        out_specs=[pl.BlockSpec((R, D), lambda i: (tile(i), 0)), pl.BlockSpec((R, D), lambda i: (tile(i), 0))],
        out_shape=[jax.ShapeDtypeStruct((Tp, D), BF16), jax.ShapeDtypeStruct((Tp, D), F32)],
        scratch_shapes=[pltpu.VMEM((HIST, D), F32), pltpu.VMEM((HIST, D), F32), pltpu.VMEM((8, D), F32), pltpu.VMEM((R, D), F32)],
        name=name, semantics=("arbitrary",), inputs=[proj, pool_w, gate_a, gate_x, vecs], stages=stages)
    return (outs, staged) if stages else outs


def mix_bwd(proj, hs, dmerged, pool_w, gate_a, gate_x, vecs, *, name, stages=()):
    Tp = proj.shape[0]
    D = proj.shape[1] // 5
    R = ROW_TILE
    nt = Tp // R
    H = D // HEAD_DIM
    PG = D // len(POOL_WINDOWS)

    def body(p_ref, pprev_ref, hs_ref, hprev_ref, dm_ref, pw_ref, wa_ref, wx_ref, vec_ref,
             dp_ref, dpw_ref, dwa_ref, dwx_ref, dvec_ref, car_g, fut_dxc, fut_q):
        i = pl.program_id(0)
        is_meta = i == nt - 1

        @pl.when(i == 0)
        def _():
            dpw_ref[...] = jnp.zeros_like(dpw_ref)
            dwa_ref[...] = jnp.zeros_like(dwa_ref)
            dwx_ref[...] = jnp.zeros_like(dwx_ref)
            dvec_ref[...] = jnp.zeros_like(dvec_ref)
            car_g[...] = jnp.zeros_like(car_g)
            fut_dxc[...] = jnp.zeros_like(fut_dxc)
            fut_q[...] = jnp.zeros_like(fut_q)

        row, valid, t_log = _tile_masks(is_meta, R)
        keep_prev = jnp.logical_not(is_meta)

        def colsum(x):
            return jnp.sum(x, axis=0, keepdims=True)

        for g, w in enumerate(POOL_WINDOWS):
            cs = slice(g * PG, (g + 1) * PG)
            v = p_ref[:, g * PG:(g + 1) * PG].astype(F32)
            vprev = jnp.where(keep_prev, pprev_ref[:, g * PG:(g + 1) * PG].astype(F32), 0.0)
            inv_cnt = _window_count_inv(t_log, w)
            d = _window_sum_back(jnp.concatenate([vprev, v], axis=0), w) * inv_cnt - v
            d_bf = d.astype(BF16)
            y = jnp.dot(d_bf, pw_ref[g], preferred_element_type=F32)
            scale = vec_ref[0:1, cs]
            sg = _sigmoid(p_ref[:, 3 * D + g * PG:3 * D + (g + 1) * PG].astype(F32))
            dm = dm_ref[:, cs].astype(F32)
            dpo = dm * sg
            dp_ref[:, 3 * D + g * PG:3 * D + (g + 1) * PG] = (dm * (y * scale) * sg * (1.0 - sg)).astype(BF16)
            dvec_ref[0:1, cs] += colsum(dpo * y)
            dy = (dpo * scale).astype(BF16)
            dd = _dot_nt(dy, pw_ref[g])
            dpw_ref[g] += _dot_tn(d_bf, dy)
            q = dd * inv_cnt
            s, sh = jnp.concatenate([q, fut_q[:, cs]], axis=0), 1
            while sh < w:
                s = s + pltpu.roll(s, R + HIST - sh, 0)
                sh *= 2
            dp_ref[:, g * PG:(g + 1) * PG] = (s[:R, :] - dd).astype(BF16)
            fut_q[:, cs] = q[:HIST, :]

        for h in range(H):
            cs = slice(h * HEAD_DIM, (h + 1) * HEAD_DIM)
            pc = lambda blk: slice(blk * D + h * HEAD_DIM, blk * D + (h + 1) * HEAD_DIM)
            vl = p_ref[:, pc(1)].astype(F32)
            vlprev = jnp.where(keep_prev, pprev_ref[:, pc(1)].astype(F32), 0.0)
            xc, x1, x2, x3 = _conv_taps(jnp.concatenate([vlprev, vl], axis=0), vl, vec_ref, cs, R)
            lam = vec_ref[4:5, cs]
            sp = _softplus_neg(lam)
            xcb, r, ig, a, a2, em = _lru_gates(xc, wa_ref[h], wx_ref[h], vec_ref, cs, sp)
            inv_mult = lax.rsqrt(em)
            mult = em * inv_mult
            hsv = hs_ref[:, cs]
            hprev = jnp.where(row >= 1, pltpu.roll(hsv, 1, 0), hprev_ref[HIST - 1:HIST, cs])
            vg = p_ref[:, pc(2)].astype(F32)
            ge, th = _gelu_tanh(vg)
            sgl = _sigmoid(p_ref[:, pc(4)].astype(F32))
            dm = dm_ref[:, cs].astype(F32)
            dlo = dm * sgl
            dp_ref[:, pc(4)] = (dm * (hsv * ge) * sgl * (1.0 - sgl)).astype(BF16)
            dp_ref[:, pc(2)] = (dlo * hsv * _gelu_tanh_grad(vg, th)).astype(BF16)
            a_next = jnp.where(row < R - 1, pltpu.roll(a, R - 1, 0), 1.0)
            G = _scan_rows(a_next, dlo * ge, car_g[0:1, cs], reverse=True)
            car_g[:, cs] = (a * G)[0:8, :]
            da = jnp.where(valid, G * hprev, 0.0)
            db = jnp.where(valid, G, 0.0)
            dmult = db * (ig * xc)
            dig = db * (mult * xc)
            dxc = db * (mult * ig)
            dlog_a = da * a - dmult * (a2 * inv_mult)
            dvec_ref[4:5, cs] += colsum(dlog_a * r) * (-LRU_C)
            dr = dlog_a * ((-LRU_C) * sp)
            dpa = dr * r * (1.0 - r)
            dpx = dig * ig * (1.0 - ig)
            dpa_bf = dpa.astype(BF16)
            dpx_bf = dpx.astype(BF16)
            dwa_ref[h] += _dot_tn(xcb, dpa_bf)
            dwx_ref[h] += _dot_tn(xcb, dpx_bf)
            dvec_ref[2:3, cs] += colsum(dpa)
            dvec_ref[3:4, cs] += colsum(dpx)
            dxc = dxc + _dot_nt(dpa_bf, wa_ref[h]) + _dot_nt(dpx_bf, wx_ref[h])
            ext = jnp.concatenate([dxc, fut_dxc[:, cs]], axis=0)
            n = R + HIST
            dvl = (vec_ref[8:9, cs] * dxc + vec_ref[7:8, cs] * pltpu.roll(ext, n - 1, 0)[:R, :]
                   + vec_ref[6:7, cs] * pltpu.roll(ext, n - 2, 0)[:R, :] + vec_ref[5:6, cs] * pltpu.roll(ext, n - 3, 0)[:R, :])
            dp_ref[:, pc(1)] = dvl.astype(BF16)
            dvec_ref[1:2, cs] += colsum(dxc)
            dvec_ref[8:9, cs] += colsum(dxc * vl)
            dvec_ref[7:8, cs] += colsum(dxc * x1)
            dvec_ref[6:7, cs] += colsum(dxc * x2)
            dvec_ref[5:6, cs] += colsum(dxc * x3)
            fut_dxc[:, cs] = dxc[:HIST, :]

            @pl.when(is_meta)
            def _():
                dvec_ref[4:5, cs] = dvec_ref[4:5, cs] * (-_sigmoid(-lam))

    def tile(i):
        return (2 * nt - 2 - i) % nt

    def prev_blk(i):
        per = R // HIST
        return jnp.where(i == nt - 1, 0, jnp.where(i == nt - 2, Tp // HIST - 1, (nt - 2 - i) * per - 1))

    full = lambda shape: pl.BlockSpec(shape, lambda i: (0,) * len(shape))
    G_ = len(POOL_WINDOWS)
    outs, staged = _staged_call(
        body, grid=(nt,),
        in_specs=[pl.BlockSpec((R, 5 * D), lambda i: (tile(i), 0)), pl.BlockSpec((HIST, 5 * D), lambda i: (prev_blk(i), 0)),
                  pl.BlockSpec((R, D), lambda i: (tile(i), 0)), pl.BlockSpec((HIST, D), lambda i: (prev_blk(i), 0)),
                  pl.BlockSpec((R, D), lambda i: (tile(i), 0)),
                  full(pool_w.shape), full(gate_a.shape), full(gate_x.shape), full(vecs.shape)],
        out_specs=[pl.BlockSpec((R, 5 * D), lambda i: (tile(i), 0)), full((G_, PG, PG)), full((H, HEAD_DIM, HEAD_DIM)),
                   full((H, HEAD_DIM, HEAD_DIM)), full((16, D))],
        out_shape=[jax.ShapeDtypeStruct((Tp, 5 * D), BF16), jax.ShapeDtypeStruct((G_, PG, PG), F32),
                   jax.ShapeDtypeStruct((H, HEAD_DIM, HEAD_DIM), F32), jax.ShapeDtypeStruct((H, HEAD_DIM, HEAD_DIM), F32),
                   jax.ShapeDtypeStruct((16, D), F32)],
        scratch_shapes=[pltpu.VMEM((8, D), F32), pltpu.VMEM((HIST, D), F32), pltpu.VMEM((HIST, D), F32)],
        name=name, semantics=("arbitrary",), inputs=[proj, proj, hs, hs, dmerged, pool_w, gate_a, gate_x, vecs], stages=stages)
    return (outs, staged) if stages else outs


def _adamw_math(w, g, m, v):
    mn = ADAM_B1 * m + (1.0 - ADAM_B1) * g
    vn = ADAM_B2 * v + (1.0 - ADAM_B2) * (g * g)
    m_hat = mn / (1.0 - ADAM_B1 ** ADAM_STEP)
    v_hat = vn / (1.0 - ADAM_B2 ** ADAM_STEP)
    return -ADAM_LR * (m_hat / (jnp.sqrt(v_hat) + ADAM_EPS) + ADAM_WD * w), mn, vn


def adamw(w, g, m, v, *, name):
    rows, cols = w.shape
    tr = _pick(rows, (256, 128, 64, 32, 16, 8))

    def body(w_ref, g_ref, m_ref, v_ref, go_ref, d_ref, mo_ref, vo_ref):
        gv = g_ref[...]
        go_ref[...] = gv
        d_ref[...], mo_ref[...], vo_ref[...] = _adamw_math(w_ref[...], gv, m_ref[...], v_ref[...])

    blk = pl.BlockSpec((tr, cols), lambda i: (i, 0))
    sds = jax.ShapeDtypeStruct((rows, cols), F32)
    return pl.pallas_call(body, grid=(rows // tr,), in_specs=[blk] * 4, out_specs=[blk] * 4, out_shape=[sds] * 4, name=name,
                          compiler_params=_params("parallel"))(w, g, m, v)


def allgather8(block, *, name, reduce_sum=False):
    rows, cols = block.shape

    def body(x_ref, out_ref, *scratch):
        if reduce_sum:
            buf, send_sems, recv_sems, local_sem = scratch
        else:
            buf = out_ref
            send_sems, recv_sems, local_sem = scratch
        x, y, c, chips = _place()
        me, sibling = (x, y, c), (x, y, 1 - c)

        def slot(px, py, pc):
            return buf.at[4 * px + 2 * py + pc]

        def copy(k, blk, to, src=None):
            return pltpu.make_async_remote_copy(src_ref=slot(*blk) if src is None else src, dst_ref=slot(*blk),
                                                send_sem=send_sems.at[k], recv_sem=recv_sems.at[k], device_id=to, device_id_type=MESH)

        mine = pltpu.make_async_copy(x_ref, slot(*me), local_sem)
        mine.start()
        first = [copy(0, me, sibling, src=x_ref)]
        first += [copy(1 + j, me, (*chip, c), src=x_ref) for j, chip in enumerate(chips)]
        for cp in first:
            cp.start()
        passed = [copy(4 + j, (*chip, c), sibling) for j, chip in enumerate(chips)]
        for j, chip in enumerate(chips):
            copy(1 + j, (*chip, c), me).wait_recv()
            passed[j].start()
        copy(0, sibling, me).wait_recv()
        for j, chip in enumerate(chips):
            copy(4 + j, (*chip, 1 - c), me).wait_recv()
        for cp in first + passed:
            cp.wait_send()
        mine.wait()
        if reduce_sum:
            acc = buf[0]
            for d in range(1, 8):
                acc = acc + buf[d]
            out_ref[...] = acc

    sems = [pltpu.SemaphoreType.DMA((7,)), pltpu.SemaphoreType.DMA((7,)), pltpu.SemaphoreType.DMA]
    if reduce_sum:
        out_shape = jax.ShapeDtypeStruct((rows, cols), block.dtype)
        scratch = [pltpu.VMEM((8, rows, cols), block.dtype)] + sems
    else:
        out_shape = jax.ShapeDtypeStruct((8, rows, cols), block.dtype)
        scratch = sems
    return pl.pallas_call(body, out_shape=out_shape, in_specs=[pl.BlockSpec(memory_space=pltpu.VMEM)],
                          out_specs=pl.BlockSpec(memory_space=pltpu.VMEM), scratch_shapes=scratch, name=name,
                          compiler_params=pltpu.CompilerParams(vmem_limit_bytes=VMEM_LIMIT_BYTES))(block)


def cast_into_slot(w, chip_arr, *, name):
    _, r, cols = w.shape
    tr = _pick(r, (256, 128, 64, 32, 16))

    def body(chip_ref, w_ref, o_ref):
        del chip_ref
        o_ref[...] = w_ref[...].astype(BF16)

    grid_spec = pltpu.PrefetchScalarGridSpec(
        num_scalar_prefetch=1, grid=(2, r // tr),
        in_specs=[pl.BlockSpec((None, tr, cols), lambda h, i, chip: (h, i, 0))],
        out_specs=pl.BlockSpec((None, None, tr, cols), lambda h, i, chip: (chip[0], h, i, 0)))
    return pl.pallas_call(body, grid_spec=grid_spec, out_shape=jax.ShapeDtypeStruct((4, 2, r, cols), BF16), name=name,
                          compiler_params=_params("parallel", "parallel"))(chip_arr, w)


def proj_with_gather(u, bufs, order_arr, *, name):
    Tp, K = u.shape
    n = len(bufs)
    Ns = bufs[0].shape[3]
    tm = _pick(Tp, (1408, 512, 256, 128))
    tc = _pick(Ns, (512, 256, 128))
    n_rows = Tp // tm

    def body(order_ref, u_ref, *refs):
        del order_ref
        o_ref, outs = refs[n], refs[n + 1:2 * n + 1]
        wbuf, ici_send, ici_recv, d2d_send, d2d_recv, load_sems = refs[2 * n + 1:]
        s, i = pl.program_id(0), pl.program_id(1)
        x, y, c, chips = _place()
        me = 2 * x + y

        def ici(t, j, landing):
            px, py = chips[j]
            slot = outs[t].at[2 * px + py, c] if landing else outs[t].at[me, c]
            return _remote(slot, slot, ici_send.at[t, j], ici_recv.at[t, j], (px, py, c))

        def d2d(t, j, landing):
            px, py = chips[j]
            slot = outs[t].at[2 * px + py, 1 - c] if landing else outs[t].at[2 * px + py, c]
            return _remote(slot, slot, d2d_send.at[t, j], d2d_recv.at[t, j], (x, y, 1 - c))

        def load(chip_idx):
            parts = [pltpu.make_async_copy(outs[0].at[chip_idx, hh], wbuf.at[pl.ds(hh * (K // 2), K // 2), :], load_sems.at[hh])
                     for hh in range(2)]
            for cp in parts:
                cp.start()
            for cp in parts:
                cp.wait()

        @pl.when(jnp.logical_and(s == 0, i == 0))
        def _():
            for j in range(3):
                for t in range(n):
                    ici(t, j, False).start()
            load(me)

        for j in range(3):
            @pl.when(jnp.logical_and(s == j + 1, i == 0))
            def _(j=j):
                for t in range(n):
                    ici(t, j, True).wait_recv()
                    d2d(t, j, False).start()
                for t in range(n):
                    d2d(t, j, True).wait_recv()
                px, py = chips[j]
                load(2 * px + py)

        uv = u_ref[...]
        for cc in range(Ns // tc):
            o_ref[:, cc * tc:(cc + 1) * tc] = jnp.dot(uv, wbuf[:, cc * tc:(cc + 1) * tc], preferred_element_type=F32).astype(BF16)

        @pl.when(jnp.logical_and(s == 3, i == n_rows - 1))
        def _():
            for j in range(3):
                for t in range(n):
                    ici(t, j, False).wait_send()
                    d2d(t, j, False).wait_send()

    grid_spec = pltpu.PrefetchScalarGridSpec(
        num_scalar_prefetch=1, grid=(4, n_rows),
        in_specs=[pl.BlockSpec((tm, K), lambda s, i, order: (i, 0)), *_any_specs(n)],
        out_specs=[pl.BlockSpec((tm, Ns), lambda s, i, order: (i, order[s])), *_any_specs(n)],
        scratch_shapes=[pltpu.VMEM((K, Ns), BF16), pltpu.SemaphoreType.DMA((n, 3)), pltpu.SemaphoreType.DMA((n, 3)),
                        pltpu.SemaphoreType.DMA((n, 3)), pltpu.SemaphoreType.DMA((n, 3)), pltpu.SemaphoreType.DMA((2,))])
    res = pl.pallas_call(body, grid_spec=grid_spec, out_shape=[jax.ShapeDtypeStruct((Tp, 4 * Ns), BF16), *[_sds(b) for b in bufs]],
                         input_output_aliases={2 + t: 1 + t for t in range(n)}, name=name,
                         compiler_params=_params("arbitrary", "arbitrary"))(order_arr, u, *bufs)
    return res[0], list(res[1:])


def chip_presum(grad, recv, c_arr, *, name):
    _, _, r, cols = grad.shape
    tr = _pick(r, (256, 128, 64, 32, 16))

    def body(c_ref, g_ref, r_ref, o_ref):
        del c_ref
        o_ref[...] = (g_ref[...] + r_ref[...]).astype(BF16)

    grid_spec = pltpu.PrefetchScalarGridSpec(
        num_scalar_prefetch=1, grid=(4, r // tr),
        in_specs=[pl.BlockSpec((None, None, tr, cols), lambda k, i, c_ref: (k, c_ref[0], i, 0)),
                  pl.BlockSpec((None, tr, cols), lambda k, i, c_ref: (k, i, 0))],
        out_specs=pl.BlockSpec((None, tr, cols), lambda k, i, c_ref: (k, i, 0)))
    return pl.pallas_call(body, grid_spec=grid_spec, out_shape=jax.ShapeDtypeStruct((4, r, cols), BF16), name=name,
                          compiler_params=_params("parallel", "parallel"))(c_arr, grad, recv)


def final_half(grad, recv, got, mc_arr, *, name):
    _, _, r, cols = grad.shape
    tr = _pick(r, (256, 128, 64, 32, 16))

    def body(mc_ref, g_ref, r_ref, q_ref, o_ref):
        del mc_ref
        acc = g_ref[...] + r_ref[...]
        for j in range(3):
            acc = acc + q_ref[j].astype(F32)
        o_ref[...] = acc

    grid_spec = pltpu.PrefetchScalarGridSpec(
        num_scalar_prefetch=1, grid=(r // tr,),
        in_specs=[pl.BlockSpec((None, None, tr, cols), lambda i, mc: (mc[0], mc[1], i, 0)),
                  pl.BlockSpec((None, tr, cols), lambda i, mc: (mc[0], i, 0)),
                  pl.BlockSpec((3, tr, cols), lambda i, mc: (0, i, 0))],
        out_specs=pl.BlockSpec((tr, cols), lambda i, mc: (i, 0)))
    return pl.pallas_call(body, grid_spec=grid_spec, out_shape=jax.ShapeDtypeStruct((r, cols), F32), name=name,
                          compiler_params=_params("parallel"))(mc_arr, grad, recv, got)


def adamw_halves(w, mine, theirs, m, v, c_arr, *, name):
    _, r, cols = w.shape
    tr = _pick(r, (256, 128, 64, 32, 16, 8))

    def body(c_ref, w_ref, mine_ref, theirs_ref, m_ref, v_ref, go_ref, d_ref, mo_ref, vo_ref):
        gv = jnp.where(pl.program_id(0) == c_ref[0], mine_ref[...], theirs_ref[...])
        go_ref[...] = gv
        d_ref[...], mo_ref[...], vo_ref[...] = _adamw_math(w_ref[...], gv, m_ref[...], v_ref[...])

    blk = pl.BlockSpec((None, tr, cols), lambda h, i, c_ref: (h, i, 0))
    grid_spec = pltpu.PrefetchScalarGridSpec(
        num_scalar_prefetch=1, grid=(2, r // tr),
        in_specs=[blk, pl.BlockSpec((tr, cols), lambda h, i, c_ref: (jnp.where(h == c_ref[0], i, 0), 0)),
                  pl.BlockSpec((tr, cols), lambda h, i, c_ref: (jnp.where(h == c_ref[0], 0, i), 0)), blk, blk],
        out_specs=[blk] * 4)
    sds = jax.ShapeDtypeStruct((2, r, cols), F32)
    return pl.pallas_call(body, grid_spec=grid_spec, out_shape=[sds] * 4, name=name,
                          compiler_params=_params("parallel", "parallel"))(c_arr, w, mine, theirs, m, v)


def _pad_rows(a, rows):
    return jnp.pad(a, ((0, rows - a.shape[0]), (0, 0)))


def kernel(x, meta_tokens, norm1_g, w_in, pool_w, pool_scale, conv_w, conv_b, gate_a_w, gate_a_b, gate_x_w, gate_x_b, lru_lambda, w_out, norm2_g, mlp_w1, mlp_w2, final_g, loss_target, m_meta_tokens, m_norm1_g, m_w_in, m_pool_w, m_pool_scale, m_conv_w, m_conv_b, m_gate_a_w, m_gate_a_b, m_gate_x_w, m_gate_x_b, m_lru_lambda, m_w_out, m_norm2_g, m_mlp_w1, m_mlp_w2, m_final_g, v_meta_tokens, v_norm1_g, v_w_in, v_pool_w, v_pool_scale, v_conv_w, v_conv_b, v_gate_a_w, v_gate_a_b, v_gate_x_w, v_gate_x_b, v_lru_lambda, v_w_out, v_norm2_g, v_mlp_w1, v_mlp_w2, v_final_g):
    D = x.shape[-1]
    H = D // HEAD_DIM
    G = len(POOL_WINDOWS)
    PG = D // G
    ax, ay, ac = lax.axis_index("x"), lax.axis_index("y"), lax.axis_index("c")
    chip = 2 * ax + ay
    dshard = D // 4

    c_arr = jnp.reshape(ac, (1,)).astype(jnp.int32)
    chip_arr = jnp.reshape(chip, (1,)).astype(jnp.int32)
    mc_arr = jnp.stack([chip, ac]).astype(jnp.int32)
    names = ["w_in", "pool_w", "gate_a_w", "gate_x_w", "w_out", "mlp_w1", "mlp_w2"]
    big = [w_in, pool_w, gate_a_w, gate_x_w, w_out, mlp_w1, mlp_w2]

    def halves(w):
        w2d = w.reshape(-1, w.shape[-1])
        return w2d.reshape(2, w2d.shape[0] // 2, w2d.shape[1])

    bufs = [cast_into_slot(halves(w), chip_arr, name="cast_" + nm) for w, nm in zip(big, names)]
    order_arr = jnp.stack([chip, 2 * (1 - ax) + ay, 2 * ax + (1 - ay), 2 * (1 - ax) + (1 - ay)]).astype(jnp.int32)

    small_in = jnp.concatenate([meta_tokens, _pad_rows(conv_w[0], 8), _pad_rows(gate_a_b.reshape(1, dshard), 8),
                                _pad_rows(gate_x_b.reshape(1, dshard), 8)], axis=0)
    sm = allgather8(small_in, name="gather_small")[0::2]
    meta_f = sm[:, 0:16].transpose(1, 0, 2).reshape(N_META, D)
    conv_w_f = sm[:, 16:20].transpose(1, 0, 2).reshape(4, D)
    hd4 = HEAD_DIM // 4
    ba_f = sm[:, 24].reshape(4, H, hd4).transpose(1, 0, 2).reshape(1, D)
    bx_f = sm[:, 32].reshape(4, H, hd4).transpose(1, 0, 2).reshape(1, D)
    vecs = jnp.zeros((16, D), F32)
    for r0, part in ((0, pool_scale), (1, conv_b), (2, ba_f), (3, bx_f), (4, lru_lambda), (5, conv_w_f)):
        vecs = lax.dynamic_update_slice(vecs, part, (r0, 0))

    def chipwise(g, n_blocks, rows):
        return g.reshape(n_blocks, 4, rows, g.shape[-1]).transpose(1, 0, 2, 3).reshape(4, n_blocks * rows, g.shape[-1])

    def split2(g):
        return g.reshape(4, 2, g.shape[1] // 2, g.shape[2])

    def presum(t, g, r):
        return chip_presum(g, r, c_arr, name="presum_" + names[t])

    def total(t, g, r, q):
        return final_half(g, r, q, mc_arr, name="sum_" + names[t])

    xs, target, gfin = x[0], loss_target[0], final_g.reshape(1, D)
    h0 = jnp.concatenate([xs, jnp.zeros((ROW_TILE - N_META, D), F32), meta_f], axis=0)
    u = rmsnorm_fwd(h0, norm1_g, name="norm1")
    proj, (b_win, b_pool, b_ga, b_gx, b_wout) = proj_with_gather(u, bufs[0:5], order_arr, name="proj")
    w_in_f = b_win.reshape(4, w_in.shape[1], w_in.shape[2])
    pool_f = b_pool.reshape(4, G, PG // 4, PG).transpose(1, 0, 2, 3).reshape(G, PG, PG)
    ga_f = b_ga.reshape(4, H, HEAD_DIM // 4, HEAD_DIM).transpose(1, 0, 2, 3).reshape(H, HEAD_DIM, HEAD_DIM)
    gx_f = b_gx.reshape(4, H, HEAD_DIM // 4, HEAD_DIM).transpose(1, 0, 2, 3).reshape(H, HEAD_DIM, HEAD_DIM)
    w_out_f = b_wout.reshape(4 * w_out.shape[1], w_out.shape[2])
    (merged, hs), [[b_w1]] = mix_fwd(proj, pool_f, ga_f, gx_f, vecs, name="mix_fwd", stages=[stage_gather_ici([bufs[5]])])
    h1, [[b_w1]] = mm_nn(merged, w_out_f, out_dtype=F32, name="out_proj", epilogue=lambda r, res: r + res, extras=(h0,),
                         stages=[stage_gather_d2d([b_w1])])
    w1_f = b_w1.reshape(4, mlp_w1.shape[1], mlp_w1.shape[2])
    u2 = rmsnorm_fwd(h1, norm2_g, name="norm2")
    a1, [[b_w2]] = mm_nn(u2, w1_f, out_dtype=BF16, name="mlp_up", stages=[stage_gather_ici([bufs[6]])])
    [[b_w2]] = comm_call([stage_gather_d2d([b_w2])], name="w2_to_sibling")
    w2_f = b_w2.reshape(4 * mlp_w2.shape[1], mlp_w2.shape[2])
    h2 = mm_nn(a1, w2_f, out_dtype=F32, name="mlp_down", a_pro=_relu_sq, epilogue=lambda r, res: r + res, extras=(h1,))
    dh2, dh2_bf, st_f = final_loss(h2, target, gfin, name="final_loss")

    da1 = mm_nt(dh2_bf, w2_f, out_dtype=BF16, name="mlp_down_dx",
                epilogue=lambda r, a: r * (2.0 * jnp.maximum(a.astype(F32), 0.0)), extras=(a1,))
    d_w2 = mm_tn(a1, dh2_bf, shards=1, name="mlp_down_dw", a_pro=_relu_sq)
    g6 = split2(d_w2.reshape(4, -1, d_w2.shape[-1]))
    d_w1, [[r6]] = mm_tn(u2, da1, shards=4, name="mlp_up_dw", stages=[stage_to_sibling([g6])])
    g5 = split2(d_w1)
    p6 = presum(6, g6, r6)
    du2, [[q6], [r5]] = mm_nt(da1, w1_f, out_dtype=F32, name="mlp_up_dx", stages=[stage_to_chips([p6]), stage_to_sibling([g5])])
    p5 = presum(5, g5, r5)
    f6 = total(6, g6, r6, q6)
    dh1, dh1_bf, st_2 = rms_bwd(h1, norm2_g, du2, dh2, name="norm2_bwd")
    dmerged, [[o6]] = mm_nt(dh1_bf, w_out_f, out_dtype=BF16, name="out_proj_dx", stages=[stage_from_sibling([f6])])
    d_wout = mm_tn(merged, dh1_bf, shards=1, name="out_proj_dw")
    g4 = split2(d_wout.reshape(4, -1, d_wout.shape[-1]))
    (dproj, d_pool, d_ga, d_gx, d_vecs), [[q5], [r4]] = mix_bwd(proj, hs, dmerged, pool_f, ga_f, gx_f, vecs, name="mix_bwd",
                                                               stages=[stage_to_chips([p5]), stage_to_sibling([g4])])
    p4 = presum(4, g4, r4)
    f5 = total(5, g5, r5, q5)
    g1, g2, g3 = split2(chipwise(d_pool, G, PG // 4)), split2(chipwise(d_ga, H, hd4)), split2(chipwise(d_gx, H, hd4))
    d_win, [[r1, r2, r3], [o5], [q4]] = mm_tn(u, dproj, shards=4, name="proj_dw",
                                             stages=[stage_to_sibling([g1, g2, g3]), stage_from_sibling([f5]), stage_to_chips([p4])])
    g0 = split2(d_win)
    [[r0]] = comm_call([stage_to_sibling([g0])], name="w_in_grad_to_sibling")
    p0, p1, p2, p3 = presum(0, g0, r0), presum(1, g1, r1), presum(2, g2, r2), presum(3, g3, r3)
    f4 = total(4, g4, r4, q4)
    du, [[q1, q2, q3, q0], [o4]] = mm_nt(dproj, w_in_f, out_dtype=F32, name="proj_dx",
                                        stages=[stage_to_chips([p1, p2, p3, p0]), stage_from_sibling([f4])])
    f0, f1, f2, f3 = total(0, g0, r0, q0), total(1, g1, r1, q1), total(2, g2, r2, q2), total(3, g3, r3, q3)
    (grad_x, d_meta, st_1), [[o0, o1, o2, o3]] = rms_bwd_input(h0, norm1_g, du, dh1, name="norm1_bwd",
                                                              stages=[stage_from_sibling([f0, f1, f2, f3])])
    mine = [f0, f1, f2, f3, f4, f5, f6]
    theirs = [o0, o1, o2, o3, o4, o5, o6]

    small = jnp.concatenate([d_meta, d_vecs, st_1, st_2, st_f], axis=0)
    tot = allgather8(small, name="sum_small", reduce_sum=True)
    loss = jnp.sum(tot[49])
    g_meta = lax.dynamic_slice_in_dim(tot[0:16], chip * dshard, dshard, axis=1)
    g_pool_scale, g_conv_b, g_lam = tot[16:17], tot[17:18], tot[20:21]
    g_ba = lax.dynamic_slice_in_dim(tot[18].reshape(H, HEAD_DIM), chip * hd4, hd4, axis=1)[None]
    g_bx = lax.dynamic_slice_in_dim(tot[19].reshape(H, HEAD_DIM), chip * hd4, hd4, axis=1)[None]
    g_conv_w = lax.dynamic_slice_in_dim(tot[21:25], chip * dshard, dshard, axis=1)[None]
    g_n1, g_n2, g_fin = tot[32:33], tot[40:41], tot[48]

    def step(w, g, m, v, nm):
        cols = w.shape[-1]
        outs = adamw(w.reshape(-1, cols), g.reshape(-1, cols), m.reshape(-1, cols), v.reshape(-1, cols), name="adamw_" + nm)
        return [o.reshape(w.shape) for o in outs]

    def step_big(t, w, m, v):
        outs = adamw_halves(halves(w), mine[t], theirs[t], halves(m), halves(v), c_arr, name="adamw_" + names[t])
        return [o.reshape(w.shape) for o in outs]

    res = dict(meta_tokens=step(meta_tokens, g_meta, m_meta_tokens, v_meta_tokens, "meta_tokens"),
               norm1_g=step(norm1_g, g_n1, m_norm1_g, v_norm1_g, "norm1_g"),
               w_in=step_big(0, w_in, m_w_in, v_w_in), pool_w=step_big(1, pool_w, m_pool_w, v_pool_w),
               pool_scale=step(pool_scale, g_pool_scale, m_pool_scale, v_pool_scale, "pool_scale"),
               conv_w=step(conv_w, g_conv_w, m_conv_w, v_conv_w, "conv_w"), conv_b=step(conv_b, g_conv_b, m_conv_b, v_conv_b, "conv_b"),
               gate_a_w=step_big(2, gate_a_w, m_gate_a_w, v_gate_a_w), gate_a_b=step(gate_a_b, g_ba, m_gate_a_b, v_gate_a_b, "gate_a_b"),
               gate_x_w=step_big(3, gate_x_w, m_gate_x_w, v_gate_x_w), gate_x_b=step(gate_x_b, g_bx, m_gate_x_b, v_gate_x_b, "gate_x_b"),
               lru_lambda=step(lru_lambda, g_lam, m_lru_lambda, v_lru_lambda, "lru_lambda"), w_out=step_big(4, w_out, m_w_out, v_w_out),
               norm2_g=step(norm2_g, g_n2, m_norm2_g, v_norm2_g, "norm2_g"), mlp_w1=step_big(5, mlp_w1, m_mlp_w1, v_mlp_w1),
               mlp_w2=step_big(6, mlp_w2, m_mlp_w2, v_mlp_w2), final_g=step(final_g, g_fin, m_final_g, v_final_g, "final_g"))
    order = list(res)
    return (loss, grad_x[None], *[res[n][0] for n in order], *[res[n][1] for n in order], *[res[n][2] for n in order],
            *[res[n][3] for n in order])
```

```python
import functools

import jax
import jax.numpy as jnp
from jax import lax
from jax.experimental import pallas as pl
from jax.experimental.pallas import tpu as pltpu

F32 = jnp.float32
BF16 = jnp.bfloat16
MESH = pl.DeviceIdType.MESH

NORM_EPS = 1e-6
N_META = 16
HEAD_DIM = 256
POOL_WINDOWS = (2, 4, 8, 16)
LRU_C = 8.0
ROW_TILE = 128
HIST = 16
VMEM_LIMIT_BYTES = 56 * 1024 * 1024
ADAM_LR, ADAM_B1, ADAM_B2, ADAM_EPS, ADAM_WD, ADAM_STEP = 0.001, 0.9, 0.999, 1e-08, 0.01, 10


def _pick(n, prefs):
    for p in prefs:
        if n % p == 0:
            return p
    return n


def _params(*sem):
    return pltpu.CompilerParams(dimension_semantics=sem, vmem_limit_bytes=VMEM_LIMIT_BYTES)


def _sigmoid(x):
    return 1.0 / (1.0 + jnp.exp(-x))


def _gelu_tanh(x):
    t = jnp.tanh(0.7978845608028654 * (x + 0.044715 * (x * x * x)))
    return 0.5 * x * (1.0 + t), t


def _gelu_tanh_grad(x, t):
    return 0.5 * (1.0 + t) + 0.5 * x * (1.0 - t * t) * (0.7978845608028654 * (1.0 + 3.0 * 0.044715 * x * x))


def _neg_expm1(x, exp_x):
    series = x * (-1.0 + x * (-0.5 + x * ((-1.0 / 6.0) + x * ((-1.0 / 24.0) + x * (-1.0 / 120.0)))))
    return jnp.where(x > -0.125, series, 1.0 - exp_x)


def _softplus_neg(lam):
    z = jnp.exp(-jnp.abs(lam))
    log1p_z = jnp.where(z < 0.01, z * (1.0 - z * (0.5 - z * (1.0 / 3.0))), jnp.log(1.0 + z))
    return jnp.maximum(-lam, 0.0) + log1p_z


def _tile_masks(is_meta, rows):
    row = lax.broadcasted_iota(jnp.int32, (rows, 1), 0)
    valid = jnp.logical_or(jnp.logical_not(is_meta), row >= rows - N_META)
    t_log = jnp.where(is_meta, row - (rows - N_META), 1 << 20)
    return row, valid, t_log


def _window_count_inv(t_log, w):
    return 1.0 / jnp.clip(t_log + 1, 1, w).astype(F32)


def _dot_nt(a, b):
    return lax.dot_general(a, b, (((1,), (1,)), ((), ())), preferred_element_type=F32)


def _dot_tn(a, b):
    return lax.dot_general(a, b, (((0,), (0,)), ((), ())), preferred_element_type=F32)


def _place():
    x, y, c = lax.axis_index("x"), lax.axis_index("y"), lax.axis_index("c")
    chips = [(1 - x, y), (x, 1 - y), (1 - x, 1 - y)]
    return x, y, c, chips


def _remote(src, dst, send_sem, recv_sem, to):
    return pltpu.make_async_remote_copy(src_ref=src, dst_ref=dst, send_sem=send_sem, recv_sem=recv_sem, device_id=to,
                                        device_id_type=MESH)


class Stage:
    def __init__(self, arrays, out_shapes, aliases, n_copies, copies):
        self.arrays, self.out_shapes, self.aliases, self.n_copies, self.copies = list(arrays), list(out_shapes), aliases, n_copies, copies


def _sds(a):
    return jax.ShapeDtypeStruct(a.shape, a.dtype)


def stage_gather_ici(bufs):
    n = len(bufs)

    def copies(ins, outs, send, recv):
        x, y, c, chips = _place()
        me = 2 * x + y
        sends, recvs = [], []
        for t in range(n):
            for j, (px, py) in enumerate(chips):
                k = 3 * t + j
                mine, theirs = outs[t].at[me, c], outs[t].at[2 * px + py, c]
                sends.append(_remote(mine, mine, send.at[k], recv.at[k], (px, py, c)))
                recvs.append(_remote(theirs, theirs, send.at[k], recv.at[k], (px, py, c)))
        return sends, recvs

    return Stage(bufs, [_sds(b) for b in bufs], {t: t for t in range(n)}, 3 * n, copies)


def stage_gather_d2d(bufs):
    n = len(bufs)

    def copies(ins, outs, send, recv):
        x, y, c, chips = _place()
        sends, recvs = [], []
        for t in range(n):
            for j, (px, py) in enumerate(chips):
                k = 3 * t + j
                got, sib = outs[t].at[2 * px + py, c], outs[t].at[2 * px + py, 1 - c]
                sends.append(_remote(got, got, send.at[k], recv.at[k], (x, y, 1 - c)))
                recvs.append(_remote(sib, sib, send.at[k], recv.at[k], (x, y, 1 - c)))
        return sends, recvs

    return Stage(bufs, [_sds(b) for b in bufs], {t: t for t in range(n)}, 3 * n, copies)


def stage_to_sibling(grads):
    n = len(grads)

    def copies(ins, outs, send, recv):
        x, y, c, _ = _place()
        sends, recvs = [], []
        for t in range(n):
            for k4 in range(4):
                k = 4 * t + k4
                sends.append(_remote(ins[t].at[k4, 1 - c], outs[t].at[k4], send.at[k], recv.at[k], (x, y, 1 - c)))
                recvs.append(_remote(outs[t].at[k4], outs[t].at[k4], send.at[k], recv.at[k], (x, y, 1 - c)))
        return sends, recvs

    return Stage(grads, [jax.ShapeDtypeStruct((4, *g.shape[2:]), g.dtype) for g in grads], {}, 4 * n, copies)


def stage_to_chips(presums):
    n = len(presums)

    def copies(ins, outs, send, recv):
        x, y, c, chips = _place()
        sends, recvs = [], []
        for t in range(n):
            for j, (px, py) in enumerate(chips):
                k = 3 * t + j
                sends.append(_remote(ins[t].at[2 * px + py], outs[t].at[j], send.at[k], recv.at[k], (px, py, c)))
                recvs.append(_remote(outs[t].at[j], outs[t].at[j], send.at[k], recv.at[k], (px, py, c)))
        return sends, recvs

    return Stage(presums, [jax.ShapeDtypeStruct((3, *p.shape[1:]), p.dtype) for p in presums], {}, 3 * n, copies)


def stage_from_sibling(halves):
    n = len(halves)

    def copies(ins, outs, send, recv):
        x, y, c, _ = _place()
        sends = [_remote(ins[t], outs[t], send.at[t], recv.at[t], (x, y, 1 - c)) for t in range(n)]
        recvs = [_remote(outs[t], outs[t], send.at[t], recv.at[t], (x, y, 1 - c)) for t in range(n)]
        return sends, recvs

    return Stage(halves, [_sds(h) for h in halves], {}, n, copies)


def _any_specs(n):
    return [pl.BlockSpec(memory_space=pl.ANY)] * n


def _staged_call(body, *, grid, in_specs, out_specs, out_shape, scratch_shapes, name, semantics, inputs, stages=()):
    n_in, n_out, n_scr = len(in_specs), len(out_specs), len(scratch_shapes)
    st_arrays = [a for s in stages for a in s.arrays]
    st_outs = [o for s in stages for o in s.out_shapes]
    st_sems = [pltpu.SemaphoreType.DMA((s.n_copies,)) for s in stages for _ in range(2)]
    aliases = {}
    at_in, at_out = n_in, n_out
    for s in stages:
        for a, o in s.aliases.items():
            aliases[at_in + a] = at_out + o
        at_in += len(s.arrays)
        at_out += len(s.out_shapes)

    def full_body(*refs):
        pos = 0

        def take(count):
            nonlocal pos
            part = refs[pos:pos + count]
            pos += count
            return part

        ins, s_ins, outs, s_outs, scr = take(n_in), take(len(st_arrays)), take(n_out), take(len(st_outs)), take(n_scr)
        s_sems = refs[pos:]

        def each_stage(action):
            at_i = at_o = 0
            for idx, s in enumerate(stages):
                sends, recvs = s.copies(s_ins[at_i:at_i + len(s.arrays)], s_outs[at_o:at_o + len(s.out_shapes)],
                                        s_sems[2 * idx], s_sems[2 * idx + 1])
                action(sends, recvs)
                at_i += len(s.arrays)
                at_o += len(s.out_shapes)

        if stages:
            ids = [pl.program_id(a) for a in range(len(grid))]
            first = functools.reduce(jnp.logical_and, [i == 0 for i in ids])
            last = functools.reduce(jnp.logical_and, [i == g - 1 for i, g in zip(ids, grid)])

            def start(sends, recvs):
                for cp in sends:
                    cp.start()

            def finish(sends, recvs):
                for cp in recvs:
                    cp.wait_recv()
                for cp in sends:
                    cp.wait_send()

            @pl.when(first)
            def _():
                each_stage(start)

        body(*ins, *outs, *scr)

        if stages:
            @pl.when(last)
            def _():
                each_stage(finish)

    sem = tuple("arbitrary" for _ in grid) if stages else tuple(semantics)
    res = pl.pallas_call(
        full_body, grid=grid, in_specs=[*in_specs, *_any_specs(len(st_arrays))], out_specs=[*out_specs, *_any_specs(len(st_outs))],
        out_shape=[*out_shape, *st_outs], scratch_shapes=[*scratch_shapes, *st_sems], input_output_aliases=aliases, name=name,
        compiler_params=_params(*sem))(*inputs, *st_arrays)
    outs, rest = list(res[:n_out]), list(res[n_out:])
    per_stage = []
    for s in stages:
        per_stage.append(rest[:len(s.out_shapes)])
        rest = rest[len(s.out_shapes):]
    return outs, per_stage


def comm_call(stages, *, name):
    return _staged_call(lambda: None, grid=(1,), in_specs=[], out_specs=[], out_shape=[], scratch_shapes=[], name=name,
                        semantics=("arbitrary",), inputs=[], stages=stages)[1]


def _matmul(kind, a, b, *, grid, a_spec, b_spec, out_spec, out_shape, acc_shape, name,
            a_pro=None, epilogue=None, extras=(), extra_specs=(), stages=()):
    nk = grid[2]
    n_extra = len(extras)

    def body(a_ref, b_ref, *rest):
        extra_refs = rest[:n_extra]
        o_ref = rest[n_extra]
        av = a_ref[...]
        if a_pro is not None:
            av = a_pro(av)
        av = av.astype(BF16)
        bv = b_ref[...].astype(BF16)
        if kind == "nn":
            p = jnp.dot(av, bv, preferred_element_type=F32)
        elif kind == "nt":
            p = _dot_nt(av, bv)
        else:
            p = _dot_tn(av, bv)

        def finish(r):
            if epilogue is not None:
                r = epilogue(r, *[e[...] for e in extra_refs])
            o_ref[...] = r.astype(o_ref.dtype)

        if nk == 1:
            finish(p)
        else:
            acc_ref = rest[n_extra + 1]
            k = pl.program_id(2)

            @pl.when(k == 0)
            def _():
                acc_ref[...] = p

            @pl.when(k > 0)
            def _():
                acc_ref[...] += p

            @pl.when(k == nk - 1)
            def _():
                finish(acc_ref[...])

    scratch = [] if nk == 1 else [pltpu.VMEM(acc_shape, F32)]
    outs, staged = _staged_call(body, grid=grid, in_specs=[a_spec, b_spec, *extra_specs], out_specs=[out_spec], out_shape=[out_shape],
                                scratch_shapes=scratch, name=name, semantics=("parallel", "parallel", "arbitrary"),
                                inputs=[a, b, *extras], stages=stages)
    return (outs[0], staged) if stages else outs[0]


def mm_nn(a, b, *, out_dtype, name, a_pro=None, epilogue=None, extras=(), stages=()):
    M, K = a.shape
    sharded = b.ndim == 3
    ns = b.shape[2] if sharded else b.shape[1]
    N = ns * b.shape[0] if sharded else ns
    tm = _pick(M, (1408, 512, 256, 128))
    tn = _pick(ns, (512, 256, 128))
    tk = _pick(K, (2048, 1024, 512, 256, 128))
    per = ns // tn
    if sharded:
        b_spec = pl.BlockSpec((None, tk, tn), lambda i, j, k: (j // per, k, j % per))
    else:
        b_spec = pl.BlockSpec((tk, tn), lambda i, j, k: (k, j))
    mn = pl.BlockSpec((tm, tn), lambda i, j, k: (i, j))
    return _matmul("nn", a, b, grid=(M // tm, N // tn, K // tk), a_spec=pl.BlockSpec((tm, tk), lambda i, j, k: (i, k)),
                   b_spec=b_spec, out_spec=mn, out_shape=jax.ShapeDtypeStruct((M, N), out_dtype), acc_shape=(tm, tn),
                   name=name, a_pro=a_pro, epilogue=epilogue, extras=extras, extra_specs=[mn] * len(extras), stages=stages)


def mm_nt(a, w, *, out_dtype, name, epilogue=None, extras=(), stages=()):
    M, N = a.shape
    sharded = w.ndim == 3
    kw = w.shape[1] if sharded else w.shape[0]
    ns = w.shape[2] if sharded else w.shape[1]
    tm = _pick(M, (1408, 512, 256, 128))
    tkw = _pick(kw, (1024, 512, 256, 128))
    tk = _pick(ns, (2048, 1280, 1024, 512, 256, 128))
    per = ns // tk
    if sharded:
        w_spec = pl.BlockSpec((None, tkw, tk), lambda i, j, k: (k // per, j, k % per))
    else:
        w_spec = pl.BlockSpec((tkw, tk), lambda i, j, k: (j, k))
    mo = pl.BlockSpec((tm, tkw), lambda i, j, k: (i, j))
    return _matmul("nt", a, w, grid=(M // tm, kw // tkw, N // tk), a_spec=pl.BlockSpec((tm, tk), lambda i, j, k: (i, k)),
                   b_spec=w_spec, out_spec=mo, out_shape=jax.ShapeDtypeStruct((M, kw), out_dtype), acc_shape=(tm, tkw),
                   name=name, epilogue=epilogue, extras=extras, extra_specs=[mo] * len(extras), stages=stages)


def mm_tn(a, g, *, shards, name, a_pro=None, stages=()):
    T, kw = a.shape
    N = g.shape[1]
    ns = N // shards
    tt = _pick(T, (1408, 512, 256, 128))
    tkw = _pick(kw, (1024, 512, 256, 128))
    tn = _pick(ns, (1280, 1024, 512, 256, 128))
    per = ns // tn
    if shards > 1:
        out_spec = pl.BlockSpec((None, tkw, tn), lambda i, j, k: (j // per, i, j % per))
        out_shape = jax.ShapeDtypeStruct((shards, kw, ns), F32)
    else:
        out_spec = pl.BlockSpec((tkw, tn), lambda i, j, k: (i, j))
        out_shape = jax.ShapeDtypeStruct((kw, N), F32)
    return _matmul("tn", a, g, grid=(kw // tkw, N // tn, T // tt), a_spec=pl.BlockSpec((tt, tkw), lambda i, j, k: (k, i)),
                   b_spec=pl.BlockSpec((tt, tn), lambda i, j, k: (k, j)), out_spec=out_spec, out_shape=out_shape,
                   acc_shape=(tkw, tn), name=name, a_pro=a_pro, stages=stages)


def _relu_sq(a):
    r = jnp.maximum(a.astype(F32), 0.0)
    return r * r


def rmsnorm_fwd(h, g, *, name):
    Tp, D = h.shape
    tr = _pick(Tp, (384, 256, 128))

    def body(h_ref, g_ref, u_ref):
        x = h_ref[...]
        r = lax.rsqrt(jnp.mean(x * x, axis=-1, keepdims=True) + NORM_EPS)
        u_ref[...] = ((x * r) * g_ref[...]).astype(BF16)

    row = pl.BlockSpec((tr, D), lambda i: (i, 0))
    return pl.pallas_call(body, grid=(Tp // tr,), in_specs=[row, pl.BlockSpec((1, D), lambda i: (0, 0))], out_specs=row,
                          out_shape=jax.ShapeDtypeStruct((Tp, D), BF16), name=name, compiler_params=_params("parallel"))(h, g)


def _rms_bwd_math(x, g, dy):
    r = lax.rsqrt(jnp.mean(x * x, axis=-1, keepdims=True) + NORM_EPS)
    xh = x * r
    dyg = dy * g
    dx = r * (dyg - xh * jnp.mean(dyg * xh, axis=-1, keepdims=True))
    return dx, xh


def final_loss(h2, target, gf, *, name):
    Tp, D = h2.shape
    nt = Tp // ROW_TILE

    def body(h_ref, t_ref, g_ref, dh_ref, dhb_ref, st_ref):
        i = pl.program_id(0)

        @pl.when(i == 0)
        def _():
            st_ref[...] = jnp.zeros_like(st_ref)

        x = h_ref[...]
        g = g_ref[...]
        r = lax.rsqrt(jnp.mean(x * x, axis=-1, keepdims=True) + NORM_EPS)
        xh = x * r
        err = jnp.where(i == nt - 1, 0.0, xh * g - t_ref[...])
        dout = err * (1.0 / D)
        dyg = dout * g
        dx = r * (dyg - xh * jnp.mean(dyg * xh, axis=-1, keepdims=True))
        dh_ref[...] = dx
        dhb_ref[...] = dx.astype(BF16)
        st_ref[0:1, :] += jnp.sum(dout * xh, axis=0, keepdims=True)
        st_ref[1:2, :] += jnp.sum(err * err, axis=0, keepdims=True) * (0.5 / D)

    row = pl.BlockSpec((ROW_TILE, D), lambda i: (i, 0))
    return pl.pallas_call(
        body, grid=(nt,),
        in_specs=[row, pl.BlockSpec((ROW_TILE, D), lambda i: (jnp.minimum(i, nt - 2), 0)), pl.BlockSpec((1, D), lambda i: (0, 0))],
        out_specs=[row, row, pl.BlockSpec((8, D), lambda i: (0, 0))],
        out_shape=[jax.ShapeDtypeStruct((Tp, D), F32), jax.ShapeDtypeStruct((Tp, D), BF16), jax.ShapeDtypeStruct((8, D), F32)],
        name=name, compiler_params=_params("arbitrary"))(h2, target, gf)


def rms_bwd(h, g, du, dres, *, name):
    Tp, D = h.shape
    tr = _pick(Tp, (384, 256, 128))

    def body(h_ref, g_ref, du_ref, dr_ref, dh_ref, dhb_ref, st_ref):
        @pl.when(pl.program_id(0) == 0)
        def _():
            st_ref[...] = jnp.zeros_like(st_ref)

        du_v = du_ref[...].astype(F32)
        dx, xh = _rms_bwd_math(h_ref[...], g_ref[...], du_v)
        dh = dr_ref[...] + dx
        dh_ref[...] = dh
        dhb_ref[...] = dh.astype(BF16)
        st_ref[0:1, :] += jnp.sum(du_v * xh, axis=0, keepdims=True)

    row = pl.BlockSpec((tr, D), lambda i: (i, 0))
    return pl.pallas_call(
        body, grid=(Tp // tr,), in_specs=[row, pl.BlockSpec((1, D), lambda i: (0, 0)), row, row],
        out_specs=[row, row, pl.BlockSpec((8, D), lambda i: (0, 0))],
        out_shape=[jax.ShapeDtypeStruct((Tp, D), F32), jax.ShapeDtypeStruct((Tp, D), BF16), jax.ShapeDtypeStruct((8, D), F32)],
        name=name, compiler_params=_params("arbitrary"))(h, g, du, dres)


def rms_bwd_input(h, g, du, dres, *, name, stages=()):
    Tp, D = h.shape
    nt = Tp // ROW_TILE

    def body(h_ref, g_ref, du_ref, dr_ref, gx_ref, gm_ref, st_ref):
        i = pl.program_id(0)

        @pl.when(i == 0)
        def _():
            st_ref[...] = jnp.zeros_like(st_ref)

        du_v = du_ref[...].astype(F32)
        dx, xh = _rms_bwd_math(h_ref[...], g_ref[...], du_v)
        dh = dr_ref[...] + dx
        st_ref[0:1, :] += jnp.sum(du_v * xh, axis=0, keepdims=True)

        @pl.when(i < nt - 1)
        def _():
            gx_ref[...] = dh

        @pl.when(i == nt - 1)
        def _():
            gm_ref[...] = dh[ROW_TILE - N_META:, :]

    row = pl.BlockSpec((ROW_TILE, D), lambda i: (i, 0))
    outs, staged = _staged_call(
        body, grid=(nt,), in_specs=[row, pl.BlockSpec((1, D), lambda i: (0, 0)), row, row],
        out_specs=[pl.BlockSpec((ROW_TILE, D), lambda i: (jnp.minimum(i, nt - 2), 0)), pl.BlockSpec((N_META, D), lambda i: (0, 0)),
                   pl.BlockSpec((8, D), lambda i: (0, 0))],
        out_shape=[jax.ShapeDtypeStruct((Tp - ROW_TILE, D), F32), jax.ShapeDtypeStruct((N_META, D), F32),
                   jax.ShapeDtypeStruct((8, D), F32)],
        scratch_shapes=[], name=name, semantics=("arbitrary",), inputs=[h, g, du, dres], stages=stages)
    return (outs, staged) if stages else outs


def _conv_taps(ext, cur, vec_ref, cs, rows):
    del rows
    x1 = pltpu.roll(ext, 1, 0)[HIST:, :]
    x2 = pltpu.roll(ext, 2, 0)[HIST:, :]
    x3 = pltpu.roll(ext, 3, 0)[HIST:, :]
    xc = (vec_ref[1:2, cs] + vec_ref[8:9, cs] * cur + vec_ref[7:8, cs] * x1 + vec_ref[6:7, cs] * x2 + vec_ref[5:6, cs] * x3)
    return xc, x1, x2, x3


def _window_sum_back(ext, w):
    s, sh = ext, 1
    while sh < w:
        s = s + pltpu.roll(s, sh, 0)
        sh *= 2
    return s[HIST:, :]


def _scan_rows(a, b, carry, *, reverse):
    rows = a.shape[0]
    rin = jnp.bitwise_and(lax.broadcasted_iota(jnp.int32, (rows, 1), 0), 7)
    sh = 1
    while sh < 8:
        keep = (rin < 8 - sh) if reverse else (rin >= sh)
        amount = rows - sh if reverse else sh
        a_sh = jnp.where(keep, pltpu.roll(a, amount, 0), 1.0)
        b_sh = jnp.where(keep, pltpu.roll(b, amount, 0), 0.0)
        b = b + a * b_sh
        a = a * a_sh
        sh *= 2
    out = [None] * (rows // 8)
    for g in (reversed(range(rows // 8)) if reverse else range(rows // 8)):
        hg = b[8 * g:8 * g + 8, :] + a[8 * g:8 * g + 8, :] * carry
        carry = hg[0:1, :] if reverse else hg[7:8, :]
        out[g] = hg
    return jnp.concatenate(out, axis=0)


def _lru_gates(xc, wa, wx, vec_ref, cs, sp):
    xcb = xc.astype(BF16)
    r = _sigmoid(jnp.dot(xcb, wa, preferred_element_type=F32) + vec_ref[2:3, cs])
    ig = _sigmoid(jnp.dot(xcb, wx, preferred_element_type=F32) + vec_ref[3:4, cs])
    log_a = (-LRU_C) * r * sp
    a = jnp.exp(log_a)
    a2 = a * a
    return xcb, r, ig, a, a2, _neg_expm1(2.0 * log_a, a2)


def mix_fwd(proj, pool_w, gate_a, gate_x, vecs, *, name, stages=()):
    Tp = proj.shape[0]
    D = proj.shape[1] // 5
    R = ROW_TILE
    nt = Tp // R
    H = D // HEAD_DIM
    PG = D // len(POOL_WINDOWS)

    def body(p_ref, pw_ref, wa_ref, wx_ref, vec_ref, m_ref, hs_ref, hist_p, hist_l, hcar, mtmp):
        i = pl.program_id(0)
        is_meta = i == 0

        @pl.when(is_meta)
        def _():
            hist_p[...] = jnp.zeros_like(hist_p)
            hist_l[...] = jnp.zeros_like(hist_l)
            hcar[...] = jnp.zeros_like(hcar)

        row, valid, t_log = _tile_masks(is_meta, R)

        for g, w in enumerate(POOL_WINDOWS):
            cs = slice(g * PG, (g + 1) * PG)
            v = p_ref[:, g * PG:(g + 1) * PG].astype(F32)
            ws = _window_sum_back(jnp.concatenate([hist_p[:, cs], v], axis=0), w)
            d = ws * _window_count_inv(t_log, w) - v
            y = jnp.dot(d.astype(BF16), pw_ref[g], preferred_element_type=F32)
            gp = p_ref[:, 3 * D + g * PG:3 * D + (g + 1) * PG].astype(F32)
            mtmp[:, cs] = _sigmoid(gp) * (y * vec_ref[0:1, cs])
            hist_p[:, cs] = v[R - HIST:, :]

        for h in range(H):
            cs = slice(h * HEAD_DIM, (h + 1) * HEAD_DIM)
            vl = p_ref[:, D + h * HEAD_DIM:D + (h + 1) * HEAD_DIM].astype(F32)
            xc, _, _, _ = _conv_taps(jnp.concatenate([hist_l[:, cs], vl], axis=0), vl, vec_ref, cs, R)
            sp = _softplus_neg(vec_ref[4:5, cs])
            _, r, ig, a, _, em = _lru_gates(xc, wa_ref[h], wx_ref[h], vec_ref, cs, sp)
            b = jnp.where(valid, jnp.sqrt(em) * (ig * xc), 0.0)
            hs = _scan_rows(a, b, hcar[7:8, cs], reverse=False)
            hs_ref[:, cs] = hs
            hcar[:, cs] = hs[R - 8:, :]
            hist_l[:, cs] = vl[R - HIST:, :]
            ge, _ = _gelu_tanh(p_ref[:, 2 * D + h * HEAD_DIM:2 * D + (h + 1) * HEAD_DIM].astype(F32))
            gl = p_ref[:, 4 * D + h * HEAD_DIM:4 * D + (h + 1) * HEAD_DIM].astype(F32)
            m_ref[:, cs] = (mtmp[:, cs] + _sigmoid(gl) * (hs * ge)).astype(BF16)

    def tile(i):
        return (i + nt - 1) % nt

    full = lambda shape: pl.BlockSpec(shape, lambda i: (0,) * len(shape))
    outs, staged = _staged_call(
        body, grid=(nt,),
        in_specs=[pl.BlockSpec((R, 5 * D), lambda i: (tile(i), 0)), full(pool_w.shape), full(gate_a.shape), full(gate_x.shape),
                  full(vecs.shape)],
        out_specs=[pl.BlockSpec((R, D), lambda i: (tile(i), 0)), pl.BlockSpec((R, D), lambda i: (tile(i), 0))],
        out_shape=[jax.ShapeDtypeStruct((Tp, D), BF16), jax.ShapeDtypeStruct((Tp, D), F32)],
        scratch_shapes=[pltpu.VMEM((HIST, D), F32), pltpu.VMEM((HIST, D), F32), pltpu.VMEM((8, D), F32), pltpu.VMEM((R, D), F32)],
        name=name, semantics=("arbitrary",), inputs=[proj, pool_w, gate_a, gate_x, vecs], stages=stages)
    return (outs, staged) if stages else outs


def mix_bwd(proj, hs, dmerged, pool_w, gate_a, gate_x, vecs, *, name, stages=()):
    Tp = proj.shape[0]
    D = proj.shape[1] // 5
    R = ROW_TILE
    nt = Tp // R
    H = D // HEAD_DIM
    PG = D // len(POOL_WINDOWS)

    def body(p_ref, pprev_ref, hs_ref, hprev_ref, dm_ref, pw_ref, wa_ref, wx_ref, vec_ref,
             dp_ref, dpw_ref, dwa_ref, dwx_ref, dvec_ref, car_g, fut_dxc, fut_q):
        i = pl.program_id(0)
        is_meta = i == nt - 1

        @pl.when(i == 0)
        def _():
            dpw_ref[...] = jnp.zeros_like(dpw_ref)
            dwa_ref[...] = jnp.zeros_like(dwa_ref)
            dwx_ref[...] = jnp.zeros_like(dwx_ref)
            dvec_ref[...] = jnp.zeros_like(dvec_ref)
            car_g[...] = jnp.zeros_like(car_g)
            fut_dxc[...] = jnp.zeros_like(fut_dxc)
            fut_q[...] = jnp.zeros_like(fut_q)

        row, valid, t_log = _tile_masks(is_meta, R)
        keep_prev = jnp.logical_not(is_meta)

        def colsum(x):
            return jnp.sum(x, axis=0, keepdims=True)

        for g, w in enumerate(POOL_WINDOWS):
            cs = slice(g * PG, (g + 1) * PG)
            v = p_ref[:, g * PG:(g + 1) * PG].astype(F32)
            vprev = jnp.where(keep_prev, pprev_ref[:, g * PG:(g + 1) * PG].astype(F32), 0.0)
            inv_cnt = _window_count_inv(t_log, w)
            d = _window_sum_back(jnp.concatenate([vprev, v], axis=0), w) * inv_cnt - v
            d_bf = d.astype(BF16)
            y = jnp.dot(d_bf, pw_ref[g], preferred_element_type=F32)
            scale = vec_ref[0:1, cs]
            sg = _sigmoid(p_ref[:, 3 * D + g * PG:3 * D + (g + 1) * PG].astype(F32))
            dm = dm_ref[:, cs].astype(F32)
            dpo = dm * sg
            dp_ref[:, 3 * D + g * PG:3 * D + (g + 1) * PG] = (dm * (y * scale) * sg * (1.0 - sg)).astype(BF16)
            dvec_ref[0:1, cs] += colsum(dpo * y)
            dy = (dpo * scale).astype(BF16)
            dd = _dot_nt(dy, pw_ref[g])
            dpw_ref[g] += _dot_tn(d_bf, dy)
            q = dd * inv_cnt
            s, sh = jnp.concatenate([q, fut_q[:, cs]], axis=0), 1
            while sh < w:
                s = s + pltpu.roll(s, R + HIST - sh, 0)
                sh *= 2
            dp_ref[:, g * PG:(g + 1) * PG] = (s[:R, :] - dd).astype(BF16)
            fut_q[:, cs] = q[:HIST, :]

        for h in range(H):
            cs = slice(h * HEAD_DIM, (h + 1) * HEAD_DIM)
            pc = lambda blk: slice(blk * D + h * HEAD_DIM, blk * D + (h + 1) * HEAD_DIM)
            vl = p_ref[:, pc(1)].astype(F32)
            vlprev = jnp.where(keep_prev, pprev_ref[:, pc(1)].astype(F32), 0.0)
            xc, x1, x2, x3 = _conv_taps(jnp.concatenate([vlprev, vl], axis=0), vl, vec_ref, cs, R)
            lam = vec_ref[4:5, cs]
            sp = _softplus_neg(lam)
            xcb, r, ig, a, a2, em = _lru_gates(xc, wa_ref[h], wx_ref[h], vec_ref, cs, sp)
            inv_mult = lax.rsqrt(em)
            mult = em * inv_mult
            hsv = hs_ref[:, cs]
            hprev = jnp.where(row >= 1, pltpu.roll(hsv, 1, 0), hprev_ref[HIST - 1:HIST, cs])
            vg = p_ref[:, pc(2)].astype(F32)
            ge, th = _gelu_tanh(vg)
            sgl = _sigmoid(p_ref[:, pc(4)].astype(F32))
            dm = dm_ref[:, cs].astype(F32)
            dlo = dm * sgl
            dp_ref[:, pc(4)] = (dm * (hsv * ge) * sgl * (1.0 - sgl)).astype(BF16)
            dp_ref[:, pc(2)] = (dlo * hsv * _gelu_tanh_grad(vg, th)).astype(BF16)
            a_next = jnp.where(row < R - 1, pltpu.roll(a, R - 1, 0), 1.0)
            G = _scan_rows(a_next, dlo * ge, car_g[0:1, cs], reverse=True)
            car_g[:, cs] = (a * G)[0:8, :]
            da = jnp.where(valid, G * hprev, 0.0)
            db = jnp.where(valid, G, 0.0)
            dmult = db * (ig * xc)
            dig = db * (mult * xc)
            dxc = db * (mult * ig)
            dlog_a = da * a - dmult * (a2 * inv_mult)
            dvec_ref[4:5, cs] += colsum(dlog_a * r) * (-LRU_C)
            dr = dlog_a * ((-LRU_C) * sp)
            dpa = dr * r * (1.0 - r)
            dpx = dig * ig * (1.0 - ig)
            dpa_bf = dpa.astype(BF16)
            dpx_bf = dpx.astype(BF16)
            dwa_ref[h] += _dot_tn(xcb, dpa_bf)
            dwx_ref[h] += _dot_tn(xcb, dpx_bf)
            dvec_ref[2:3, cs] += colsum(dpa)
            dvec_ref[3:4, cs] += colsum(dpx)
            dxc = dxc + _dot_nt(dpa_bf, wa_ref[h]) + _dot_nt(dpx_bf, wx_ref[h])
            ext = jnp.concatenate([dxc, fut_dxc[:, cs]], axis=0)
            n = R + HIST
            dvl = (vec_ref[8:9, cs] * dxc + vec_ref[7:8, cs] * pltpu.roll(ext, n - 1, 0)[:R, :]
                   + vec_ref[6:7, cs] * pltpu.roll(ext, n - 2, 0)[:R, :] + vec_ref[5:6, cs] * pltpu.roll(ext, n - 3, 0)[:R, :])
            dp_ref[:, pc(1)] = dvl.astype(BF16)
            dvec_ref[1:2, cs] += colsum(dxc)
            dvec_ref[8:9, cs] += colsum(dxc * vl)
            dvec_ref[7:8, cs] += colsum(dxc * x1)
            dvec_ref[6:7, cs] += colsum(dxc * x2)
            dvec_ref[5:6, cs] += colsum(dxc * x3)
            fut_dxc[:, cs] = dxc[:HIST, :]

            @pl.when(is_meta)
            def _():
                dvec_ref[4:5, cs] = dvec_ref[4:5, cs] * (-_sigmoid(-lam))

    def tile(i):
        return (2 * nt - 2 - i) % nt

    def prev_blk(i):
        per = R // HIST
        return jnp.where(i == nt - 1, 0, jnp.where(i == nt - 2, Tp // HIST - 1, (nt - 2 - i) * per - 1))

    full = lambda shape: pl.BlockSpec(shape, lambda i: (0,) * len(shape))
    G_ = len(POOL_WINDOWS)
    outs, staged = _staged_call(
        body, grid=(nt,),
        in_specs=[pl.BlockSpec((R, 5 * D), lambda i: (tile(i), 0)), pl.BlockSpec((HIST, 5 * D), lambda i: (prev_blk(i), 0)),
                  pl.BlockSpec((R, D), lambda i: (tile(i), 0)), pl.BlockSpec((HIST, D), lambda i: (prev_blk(i), 0)),
                  pl.BlockSpec((R, D), lambda i: (tile(i), 0)),
                  full(pool_w.shape), full(gate_a.shape), full(gate_x.shape), full(vecs.shape)],
        out_specs=[pl.BlockSpec((R, 5 * D), lambda i: (tile(i), 0)), full((G_, PG, PG)), full((H, HEAD_DIM, HEAD_DIM)),
                   full((H, HEAD_DIM, HEAD_DIM)), full((16, D))],
        out_shape=[jax.ShapeDtypeStruct((Tp, 5 * D), BF16), jax.ShapeDtypeStruct((G_, PG, PG), F32),
                   jax.ShapeDtypeStruct((H, HEAD_DIM, HEAD_DIM), F32), jax.ShapeDtypeStruct((H, HEAD_DIM, HEAD_DIM), F32),
                   jax.ShapeDtypeStruct((16, D), F32)],
        scratch_shapes=[pltpu.VMEM((8, D), F32), pltpu.VMEM((HIST, D), F32), pltpu.VMEM((HIST, D), F32)],
        name=name, semantics=("arbitrary",), inputs=[proj, proj, hs, hs, dmerged, pool_w, gate_a, gate_x, vecs], stages=stages)
    return (outs, staged) if stages else outs


def _adamw_math(w, g, m, v):
    mn = ADAM_B1 * m + (1.0 - ADAM_B1) * g
    vn = ADAM_B2 * v + (1.0 - ADAM_B2) * (g * g)
    m_hat = mn / (1.0 - ADAM_B1 ** ADAM_STEP)
    v_hat = vn / (1.0 - ADAM_B2 ** ADAM_STEP)
    return -ADAM_LR * (m_hat / (jnp.sqrt(v_hat) + ADAM_EPS) + ADAM_WD * w), mn, vn


def adamw(w, g, m, v, *, name):
    rows, cols = w.shape
    tr = _pick(rows, (256, 128, 64, 32, 16, 8))

    def body(w_ref, g_ref, m_ref, v_ref, go_ref, d_ref, mo_ref, vo_ref):
        gv = g_ref[...]
        go_ref[...] = gv
        d_ref[...], mo_ref[...], vo_ref[...] = _adamw_math(w_ref[...], gv, m_ref[...], v_ref[...])

    blk = pl.BlockSpec((tr, cols), lambda i: (i, 0))
    sds = jax.ShapeDtypeStruct((rows, cols), F32)
    return pl.pallas_call(body, grid=(rows // tr,), in_specs=[blk] * 4, out_specs=[blk] * 4, out_shape=[sds] * 4, name=name,
                          compiler_params=_params("parallel"))(w, g, m, v)


def allgather8(block, *, name, reduce_sum=False):
    rows, cols = block.shape

    def body(x_ref, out_ref, *scratch):
        if reduce_sum:
            buf, send_sems, recv_sems, local_sem = scratch
        else:
            buf = out_ref
            send_sems, recv_sems, local_sem = scratch
        x, y, c, chips = _place()
        me, sibling = (x, y, c), (x, y, 1 - c)

        def slot(px, py, pc):
            return buf.at[4 * px + 2 * py + pc]

        def copy(k, blk, to, src=None):
            return pltpu.make_async_remote_copy(src_ref=slot(*blk) if src is None else src, dst_ref=slot(*blk),
                                                send_sem=send_sems.at[k], recv_sem=recv_sems.at[k], device_id=to, device_id_type=MESH)

        mine = pltpu.make_async_copy(x_ref, slot(*me), local_sem)
        mine.start()
        first = [copy(0, me, sibling, src=x_ref)]
        first += [copy(1 + j, me, (*chip, c), src=x_ref) for j, chip in enumerate(chips)]
        for cp in first:
            cp.start()
        passed = [copy(4 + j, (*chip, c), sibling) for j, chip in enumerate(chips)]
        for j, chip in enumerate(chips):
            copy(1 + j, (*chip, c), me).wait_recv()
            passed[j].start()
        copy(0, sibling, me).wait_recv()
        for j, chip in enumerate(chips):
            copy(4 + j, (*chip, 1 - c), me).wait_recv()
        for cp in first + passed:
            cp.wait_send()
        mine.wait()
        if reduce_sum:
            acc = buf[0]
            for d in range(1, 8):
                acc = acc + buf[d]
            out_ref[...] = acc

    sems = [pltpu.SemaphoreType.DMA((7,)), pltpu.SemaphoreType.DMA((7,)), pltpu.SemaphoreType.DMA]
    if reduce_sum:
        out_shape = jax.ShapeDtypeStruct((rows, cols), block.dtype)
        scratch = [pltpu.VMEM((8, rows, cols), block.dtype)] + sems
    else:
        out_shape = jax.ShapeDtypeStruct((8, rows, cols), block.dtype)
        scratch = sems
    return pl.pallas_call(body, out_shape=out_shape, in_specs=[pl.BlockSpec(memory_space=pltpu.VMEM)],
                          out_specs=pl.BlockSpec(memory_space=pltpu.VMEM), scratch_shapes=scratch, name=name,
                          compiler_params=pltpu.CompilerParams(vmem_limit_bytes=VMEM_LIMIT_BYTES))(block)


def cast_into_slot(w, chip_arr, *, name):
    _, r, cols = w.shape
    tr = _pick(r, (256, 128, 64, 32, 16))

    def body(chip_ref, w_ref, o_ref):
        del chip_ref
        o_ref[...] = w_ref[...].astype(BF16)

    grid_spec = pltpu.PrefetchScalarGridSpec(
        num_scalar_prefetch=1, grid=(2, r // tr),
        in_specs=[pl.BlockSpec((None, tr, cols), lambda h, i, chip: (h, i, 0))],
        out_specs=pl.BlockSpec((None, None, tr, cols), lambda h, i, chip: (chip[0], h, i, 0)))
    return pl.pallas_call(body, grid_spec=grid_spec, out_shape=jax.ShapeDtypeStruct((4, 2, r, cols), BF16), name=name,
                          compiler_params=_params("parallel", "parallel"))(chip_arr, w)


def proj_with_gather(u, bufs, order_arr, *, name):
    Tp, K = u.shape
    n = len(bufs)
    Ns = bufs[0].shape[3]
    tm = _pick(Tp, (1408, 512, 256, 128))
    tc = _pick(Ns, (512, 256, 128))
    n_rows = Tp // tm

    def body(order_ref, u_ref, *refs):
        del order_ref
        o_ref, outs = refs[n], refs[n + 1:2 * n + 1]
        wbuf, ici_send, ici_recv, d2d_send, d2d_recv, load_sems = refs[2 * n + 1:]
        s, i = pl.program_id(0), pl.program_id(1)
        x, y, c, chips = _place()
        me = 2 * x + y

        def ici(t, j, landing):
            px, py = chips[j]
            slot = outs[t].at[2 * px + py, c] if landing else outs[t].at[me, c]
            return _remote(slot, slot, ici_send.at[t, j], ici_recv.at[t, j], (px, py, c))

        def d2d(t, j, landing):
            px, py = chips[j]
            slot = outs[t].at[2 * px + py, 1 - c] if landing else outs[t].at[2 * px + py, c]
            return _remote(slot, slot, d2d_send.at[t, j], d2d_recv.at[t, j], (x, y, 1 - c))

        def load(chip_idx):
            parts = [pltpu.make_async_copy(outs[0].at[chip_idx, hh], wbuf.at[pl.ds(hh * (K // 2), K // 2), :], load_sems.at[hh])
                     for hh in range(2)]
            for cp in parts:
                cp.start()
            for cp in parts:
                cp.wait()

        @pl.when(jnp.logical_and(s == 0, i == 0))
        def _():
            for t in range(n):
                for j in range(3):
                    ici(t, j, False).start()
            load(me)

        for j in range(3):
            @pl.when(jnp.logical_and(s == j + 1, i == 0))
            def _(j=j):
                ici(0, j, True).wait_recv()
                d2d(0, j, False).start()
                d2d(0, j, True).wait_recv()
                px, py = chips[j]
                load(2 * px + py)

        uv = u_ref[...]
        for cc in range(Ns // tc):
            o_ref[:, cc * tc:(cc + 1) * tc] = jnp.dot(uv, wbuf[:, cc * tc:(cc + 1) * tc], preferred_element_type=F32).astype(BF16)

        @pl.when(jnp.logical_and(s == 3, i == n_rows - 1))
        def _():
            for t in range(1, n):
                for j in range(3):
                    ici(t, j, True).wait_recv()
                    d2d(t, j, False).start()
            for t in range(1, n):
                for j in range(3):
                    d2d(t, j, True).wait_recv()
            for t in range(n):
                for j in range(3):
                    ici(t, j, False).wait_send()
                    d2d(t, j, False).wait_send()

    grid_spec = pltpu.PrefetchScalarGridSpec(
        num_scalar_prefetch=1, grid=(4, n_rows),
        in_specs=[pl.BlockSpec((tm, K), lambda s, i, order: (i, 0)), *_any_specs(n)],
        out_specs=[pl.BlockSpec((tm, Ns), lambda s, i, order: (i, order[s])), *_any_specs(n)],
        scratch_shapes=[pltpu.VMEM((K, Ns), BF16), pltpu.SemaphoreType.DMA((n, 3)), pltpu.SemaphoreType.DMA((n, 3)),
                        pltpu.SemaphoreType.DMA((n, 3)), pltpu.SemaphoreType.DMA((n, 3)), pltpu.SemaphoreType.DMA((2,))])
    res = pl.pallas_call(body, grid_spec=grid_spec, out_shape=[jax.ShapeDtypeStruct((Tp, 4 * Ns), BF16), *[_sds(b) for b in bufs]],
                         input_output_aliases={2 + t: 1 + t for t in range(n)}, name=name,
                         compiler_params=_params("arbitrary", "arbitrary"))(order_arr, u, *bufs)
    return res[0], list(res[1:])


def chip_presum(grad, recv, c_arr, *, name):
    _, _, r, cols = grad.shape
    tr = _pick(r, (256, 128, 64, 32, 16))

    def body(c_ref, g_ref, r_ref, o_ref):
        del c_ref
        o_ref[...] = (g_ref[...] + r_ref[...]).astype(BF16)

    grid_spec = pltpu.PrefetchScalarGridSpec(
        num_scalar_prefetch=1, grid=(4, r // tr),
        in_specs=[pl.BlockSpec((None, None, tr, cols), lambda k, i, c_ref: (k, c_ref[0], i, 0)),
                  pl.BlockSpec((None, tr, cols), lambda k, i, c_ref: (k, i, 0))],
        out_specs=pl.BlockSpec((None, tr, cols), lambda k, i, c_ref: (k, i, 0)))
    return pl.pallas_call(body, grid_spec=grid_spec, out_shape=jax.ShapeDtypeStruct((4, r, cols), BF16), name=name,
                          compiler_params=_params("parallel", "parallel"))(c_arr, grad, recv)


def final_half(grad, recv, got, mc_arr, *, name):
    _, _, r, cols = grad.shape
    tr = _pick(r, (256, 128, 64, 32, 16))

    def body(mc_ref, g_ref, r_ref, q_ref, o_ref):
        del mc_ref
        acc = g_ref[...] + r_ref[...]
        for j in range(3):
            acc = acc + q_ref[j].astype(F32)
        o_ref[...] = acc

    grid_spec = pltpu.PrefetchScalarGridSpec(
        num_scalar_prefetch=1, grid=(r // tr,),
        in_specs=[pl.BlockSpec((None, None, tr, cols), lambda i, mc: (mc[0], mc[1], i, 0)),
                  pl.BlockSpec((None, tr, cols), lambda i, mc: (mc[0], i, 0)),
                  pl.BlockSpec((3, tr, cols), lambda i, mc: (0, i, 0))],
        out_specs=pl.BlockSpec((tr, cols), lambda i, mc: (i, 0)))
    return pl.pallas_call(body, grid_spec=grid_spec, out_shape=jax.ShapeDtypeStruct((r, cols), F32), name=name,
                          compiler_params=_params("parallel"))(mc_arr, grad, recv, got)


def adamw_halves(w, mine, theirs, m, v, c_arr, *, name):
    _, r, cols = w.shape
    tr = _pick(r, (256, 128, 64, 32, 16, 8))

    def body(c_ref, w_ref, mine_ref, theirs_ref, m_ref, v_ref, go_ref, d_ref, mo_ref, vo_ref):
        gv = jnp.where(pl.program_id(0) == c_ref[0], mine_ref[...], theirs_ref[...])
        go_ref[...] = gv
        d_ref[...], mo_ref[...], vo_ref[...] = _adamw_math(w_ref[...], gv, m_ref[...], v_ref[...])

    blk = pl.BlockSpec((None, tr, cols), lambda h, i, c_ref: (h, i, 0))
    grid_spec = pltpu.PrefetchScalarGridSpec(
        num_scalar_prefetch=1, grid=(2, r // tr),
        in_specs=[blk, pl.BlockSpec((tr, cols), lambda h, i, c_ref: (jnp.where(h == c_ref[0], i, 0), 0)),
                  pl.BlockSpec((tr, cols), lambda h, i, c_ref: (jnp.where(h == c_ref[0], 0, i), 0)), blk, blk],
        out_specs=[blk] * 4)
    sds = jax.ShapeDtypeStruct((2, r, cols), F32)
    return pl.pallas_call(body, grid_spec=grid_spec, out_shape=[sds] * 4, name=name,
                          compiler_params=_params("parallel", "parallel"))(c_arr, w, mine, theirs, m, v)


def _pad_rows(a, rows):
    return jnp.pad(a, ((0, rows - a.shape[0]), (0, 0)))


def kernel(x, meta_tokens, norm1_g, w_in, pool_w, pool_scale, conv_w, conv_b, gate_a_w, gate_a_b, gate_x_w, gate_x_b, lru_lambda, w_out, norm2_g, mlp_w1, mlp_w2, final_g, loss_target, m_meta_tokens, m_norm1_g, m_w_in, m_pool_w, m_pool_scale, m_conv_w, m_conv_b, m_gate_a_w, m_gate_a_b, m_gate_x_w, m_gate_x_b, m_lru_lambda, m_w_out, m_norm2_g, m_mlp_w1, m_mlp_w2, m_final_g, v_meta_tokens, v_norm1_g, v_w_in, v_pool_w, v_pool_scale, v_conv_w, v_conv_b, v_gate_a_w, v_gate_a_b, v_gate_x_w, v_gate_x_b, v_lru_lambda, v_w_out, v_norm2_g, v_mlp_w1, v_mlp_w2, v_final_g):
    D = x.shape[-1]
    H = D // HEAD_DIM
    G = len(POOL_WINDOWS)
    PG = D // G
    ax, ay, ac = lax.axis_index("x"), lax.axis_index("y"), lax.axis_index("c")
    chip = 2 * ax + ay
    dshard = D // 4

    c_arr = jnp.reshape(ac, (1,)).astype(jnp.int32)
    chip_arr = jnp.reshape(chip, (1,)).astype(jnp.int32)
    mc_arr = jnp.stack([chip, ac]).astype(jnp.int32)
    names = ["w_in", "pool_w", "gate_a_w", "gate_x_w", "w_out", "mlp_w1", "mlp_w2"]
    big = [w_in, pool_w, gate_a_w, gate_x_w, w_out, mlp_w1, mlp_w2]

    def halves(w):
        w2d = w.reshape(-1, w.shape[-1])
        return w2d.reshape(2, w2d.shape[0] // 2, w2d.shape[1])

    bufs = [cast_into_slot(halves(w), chip_arr, name="cast_" + nm) for w, nm in zip(big, names)]
    order_arr = jnp.stack([chip, 2 * (1 - ax) + ay, 2 * ax + (1 - ay), 2 * (1 - ax) + (1 - ay)]).astype(jnp.int32)

    small_in = jnp.concatenate([meta_tokens, _pad_rows(conv_w[0], 8), _pad_rows(gate_a_b.reshape(1, dshard), 8),
                                _pad_rows(gate_x_b.reshape(1, dshard), 8)], axis=0)
    sm = allgather8(small_in, name="gather_small")[0::2]
    meta_f = sm[:, 0:16].transpose(1, 0, 2).reshape(N_META, D)
    conv_w_f = sm[:, 16:20].transpose(1, 0, 2).reshape(4, D)
    hd4 = HEAD_DIM // 4
    ba_f = sm[:, 24].reshape(4, H, hd4).transpose(1, 0, 2).reshape(1, D)
    bx_f = sm[:, 32].reshape(4, H, hd4).transpose(1, 0, 2).reshape(1, D)
    vecs = jnp.zeros((16, D), F32)
    for r0, part in ((0, pool_scale), (1, conv_b), (2, ba_f), (3, bx_f), (4, lru_lambda), (5, conv_w_f)):
        vecs = lax.dynamic_update_slice(vecs, part, (r0, 0))

    def chipwise(g, n_blocks, rows):
        return g.reshape(n_blocks, 4, rows, g.shape[-1]).transpose(1, 0, 2, 3).reshape(4, n_blocks * rows, g.shape[-1])

    def split2(g):
        return g.reshape(4, 2, g.shape[1] // 2, g.shape[2])

    def presum(t, g, r):
        return chip_presum(g, r, c_arr, name="presum_" + names[t])

    def total(t, g, r, q):
        return final_half(g, r, q, mc_arr, name="sum_" + names[t])

    xs, target, gfin = x[0], loss_target[0], final_g.reshape(1, D)
    h0 = jnp.concatenate([xs, jnp.zeros((ROW_TILE - N_META, D), F32), meta_f], axis=0)
    u = rmsnorm_fwd(h0, norm1_g, name="norm1")
    proj, (b_win, b_pool, b_ga, b_gx, b_wout) = proj_with_gather(u, bufs[0:5], order_arr, name="proj")
    w_in_f = b_win.reshape(4, w_in.shape[1], w_in.shape[2])
    pool_f = b_pool.reshape(4, G, PG // 4, PG).transpose(1, 0, 2, 3).reshape(G, PG, PG)
    ga_f = b_ga.reshape(4, H, HEAD_DIM // 4, HEAD_DIM).transpose(1, 0, 2, 3).reshape(H, HEAD_DIM, HEAD_DIM)
    gx_f = b_gx.reshape(4, H, HEAD_DIM // 4, HEAD_DIM).transpose(1, 0, 2, 3).reshape(H, HEAD_DIM, HEAD_DIM)
    w_out_f = b_wout.reshape(4 * w_out.shape[1], w_out.shape[2])
    (merged, hs), [[b_w1]] = mix_fwd(proj, pool_f, ga_f, gx_f, vecs, name="mix_fwd", stages=[stage_gather_ici([bufs[5]])])
    h1, [[b_w1]] = mm_nn(merged, w_out_f, out_dtype=F32, name="out_proj", epilogue=lambda r, res: r + res, extras=(h0,),
                         stages=[stage_gather_d2d([b_w1])])
    w1_f = b_w1.reshape(4, mlp_w1.shape[1], mlp_w1.shape[2])
    u2 = rmsnorm_fwd(h1, norm2_g, name="norm2")
    a1, [[b_w2]] = mm_nn(u2, w1_f, out_dtype=BF16, name="mlp_up", stages=[stage_gather_ici([bufs[6]])])
    [[b_w2]] = comm_call([stage_gather_d2d([b_w2])], name="w2_to_sibling")
    w2_f = b_w2.reshape(4 * mlp_w2.shape[1], mlp_w2.shape[2])
    h2 = mm_nn(a1, w2_f, out_dtype=F32, name="mlp_down", a_pro=_relu_sq, epilogue=lambda r, res: r + res, extras=(h1,))
    dh2, dh2_bf, st_f = final_loss(h2, target, gfin, name="final_loss")

    da1 = mm_nt(dh2_bf, w2_f, out_dtype=BF16, name="mlp_down_dx",
                epilogue=lambda r, a: r * (2.0 * jnp.maximum(a.astype(F32), 0.0)), extras=(a1,))
    d_w2 = mm_tn(a1, dh2_bf, shards=1, name="mlp_down_dw", a_pro=_relu_sq)
    g6 = split2(d_w2.reshape(4, -1, d_w2.shape[-1]))
    d_w1, [[r6]] = mm_tn(u2, da1, shards=4, name="mlp_up_dw", stages=[stage_to_sibling([g6])])
    g5 = split2(d_w1)
    p6 = presum(6, g6, r6)
    du2, [[q6], [r5]] = mm_nt(da1, w1_f, out_dtype=F32, name="mlp_up_dx", stages=[stage_to_chips([p6]), stage_to_sibling([g5])])
    p5 = presum(5, g5, r5)
    f6 = total(6, g6, r6, q6)
    dh1, dh1_bf, st_2 = rms_bwd(h1, norm2_g, du2, dh2, name="norm2_bwd")
    dmerged, [[o6]] = mm_nt(dh1_bf, w_out_f, out_dtype=BF16, name="out_proj_dx", stages=[stage_from_sibling([f6])])
    d_wout = mm_tn(merged, dh1_bf, shards=1, name="out_proj_dw")
    g4 = split2(d_wout.reshape(4, -1, d_wout.shape[-1]))
    (dproj, d_pool, d_ga, d_gx, d_vecs), [[q5], [r4]] = mix_bwd(proj, hs, dmerged, pool_f, ga_f, gx_f, vecs, name="mix_bwd",
                                                               stages=[stage_to_chips([p5]), stage_to_sibling([g4])])
    p4 = presum(4, g4, r4)
    f5 = total(5, g5, r5, q5)
    g1, g2, g3 = split2(chipwise(d_pool, G, PG // 4)), split2(chipwise(d_ga, H, hd4)), split2(chipwise(d_gx, H, hd4))
    half_k = u.shape[1] // 2
    u_theirs = lax.dynamic_slice_in_dim(u, (1 - ac) * half_k, half_k, axis=1)
    u_mine = lax.dynamic_slice_in_dim(u, ac * half_k, half_k, axis=1)
    g_theirs, [[r1, r2, r3], [o5], [q4]] = mm_tn(u_theirs, dproj, shards=4, name="proj_dw_sibling_rows",
                                                stages=[stage_to_sibling([g1, g2, g3]), stage_from_sibling([f5]), stage_to_chips([p4])])
    g_mine, [[r0]] = mm_tn(u_mine, dproj, shards=4, name="proj_dw_own_rows", stages=[stage_from_sibling([g_theirs])])
    g0 = g_mine[:, None]
    p0 = chip_presum(g0, r0, jnp.zeros((1,), jnp.int32), name="presum_w_in")
    p1, p2, p3 = presum(1, g1, r1), presum(2, g2, r2), presum(3, g3, r3)
    f4 = total(4, g4, r4, q4)
    du, [[q1, q2, q3, q0], [o4]] = mm_nt(dproj, w_in_f, out_dtype=F32, name="proj_dx",
                                        stages=[stage_to_chips([p1, p2, p3, p0]), stage_from_sibling([f4])])
    f0 = final_half(g0, r0, q0, jnp.stack([chip, 0]).astype(jnp.int32), name="sum_w_in")
    f1, f2, f3 = total(1, g1, r1, q1), total(2, g2, r2, q2), total(3, g3, r3, q3)
    (grad_x, d_meta, st_1), [[o0, o1, o2, o3]] = rms_bwd_input(h0, norm1_g, du, dh1, name="norm1_bwd",
                                                              stages=[stage_from_sibling([f0, f1, f2, f3])])
    mine = [f0, f1, f2, f3, f4, f5, f6]
    theirs = [o0, o1, o2, o3, o4, o5, o6]

    small = jnp.concatenate([d_meta, d_vecs, st_1, st_2, st_f], axis=0)
    tot = allgather8(small, name="sum_small", reduce_sum=True)
    loss = jnp.sum(tot[49])
    g_meta = lax.dynamic_slice_in_dim(tot[0:16], chip * dshard, dshard, axis=1)
    g_pool_scale, g_conv_b, g_lam = tot[16:17], tot[17:18], tot[20:21]
    g_ba = lax.dynamic_slice_in_dim(tot[18].reshape(H, HEAD_DIM), chip * hd4, hd4, axis=1)[None]
    g_bx = lax.dynamic_slice_in_dim(tot[19].reshape(H, HEAD_DIM), chip * hd4, hd4, axis=1)[None]
    g_conv_w = lax.dynamic_slice_in_dim(tot[21:25], chip * dshard, dshard, axis=1)[None]
    g_n1, g_n2, g_fin = tot[32:33], tot[40:41], tot[48]

    def step(w, g, m, v, nm):
        cols = w.shape[-1]
        outs = adamw(w.reshape(-1, cols), g.reshape(-1, cols), m.reshape(-1, cols), v.reshape(-1, cols), name="adamw_" + nm)
        return [o.reshape(w.shape) for o in outs]

    def step_big(t, w, m, v):
        outs = adamw_halves(halves(w), mine[t], theirs[t], halves(m), halves(v), c_arr, name="adamw_" + names[t])
        return [o.reshape(w.shape) for o in outs]

    res = dict(meta_tokens=step(meta_tokens, g_meta, m_meta_tokens, v_meta_tokens, "meta_tokens"),
               norm1_g=step(norm1_g, g_n1, m_norm1_g, v_norm1_g, "norm1_g"),
               w_in=step_big(0, w_in, m_w_in, v_w_in), pool_w=step_big(1, pool_w, m_pool_w, v_pool_w),
               pool_scale=step(pool_scale, g_pool_scale, m_pool_scale, v_pool_scale, "pool_scale"),
               conv_w=step(conv_w, g_conv_w, m_conv_w, v_conv_w, "conv_w"), conv_b=step(conv_b, g_conv_b, m_conv_b, v_conv_b, "conv_b"),
               gate_a_w=step_big(2, gate_a_w, m_gate_a_w, v_gate_a_w), gate_a_b=step(gate_a_b, g_ba, m_gate_a_b, v_gate_a_b, "gate_a_b"),
               gate_x_w=step_big(3, gate_x_w, m_gate_x_w, v_gate_x_w), gate_x_b=step(gate_x_b, g_bx, m_gate_x_b, v_gate_x_b, "gate_x_b"),
               lru_lambda=step(lru_lambda, g_lam, m_lru_lambda, v_lru_lambda, "lru_lambda"), w_out=step_big(4, w_out, m_w_out, v_w_out),
               norm2_g=step(norm2_g, g_n2, m_norm2_g, v_norm2_g, "norm2_g"), mlp_w1=step_big(5, mlp_w1, m_mlp_w1, v_mlp_w1),
               mlp_w2=step_big(6, mlp_w2, m_mlp_w2, v_mlp_w2), final_g=step(final_g, g_fin, m_final_g, v_final_g, "final_g"))
    order = list(res)
    return (loss, grad_x[None], *[res[n][0] for n in order], *[res[n][1] for n in order], *[res[n][2] for n in order],
            *[res[n][3] for n in order])
```

```python
import functools

import jax
import jax.numpy as jnp
from jax import lax
from jax.experimental import pallas as pl
from jax.experimental.pallas import tpu as pltpu

F32 = jnp.float32
BF16 = jnp.bfloat16
MESH = pl.DeviceIdType.MESH

NORM_EPS = 1e-6
N_META = 16
HEAD_DIM = 256
POOL_WINDOWS = (2, 4, 8, 16)
LRU_C = 8.0
ROW_TILE = 128
HIST = 16
VMEM_LIMIT_BYTES = 56 * 1024 * 1024
ADAM_LR, ADAM_B1, ADAM_B2, ADAM_EPS, ADAM_WD, ADAM_STEP = 0.001, 0.9, 0.999, 1e-08, 0.01, 10


def _pick(n, prefs):
    for p in prefs:
        if n % p == 0:
            return p
    return n


def _params(*sem):
    return pltpu.CompilerParams(dimension_semantics=sem, vmem_limit_bytes=VMEM_LIMIT_BYTES)


def _sigmoid(x):
    return 1.0 / (1.0 + jnp.exp(-x))


def _gelu_tanh(x):
    t = jnp.tanh(0.7978845608028654 * (x + 0.044715 * (x * x * x)))
    return 0.5 * x * (1.0 + t), t


def _gelu_tanh_grad(x, t):
    return 0.5 * (1.0 + t) + 0.5 * x * (1.0 - t * t) * (0.7978845608028654 * (1.0 + 3.0 * 0.044715 * x * x))


def _neg_expm1(x, exp_x):
    series = x * (-1.0 + x * (-0.5 + x * ((-1.0 / 6.0) + x * ((-1.0 / 24.0) + x * (-1.0 / 120.0)))))
    return jnp.where(x > -0.125, series, 1.0 - exp_x)


def _softplus_neg(lam):
    z = jnp.exp(-jnp.abs(lam))
    log1p_z = jnp.where(z < 0.01, z * (1.0 - z * (0.5 - z * (1.0 / 3.0))), jnp.log(1.0 + z))
    return jnp.maximum(-lam, 0.0) + log1p_z


def _tile_masks(is_meta, rows):
    row = lax.broadcasted_iota(jnp.int32, (rows, 1), 0)
    valid = jnp.logical_or(jnp.logical_not(is_meta), row >= rows - N_META)
    t_log = jnp.where(is_meta, row - (rows - N_META), 1 << 20)
    return row, valid, t_log


def _window_count_inv(t_log, w):
    return 1.0 / jnp.clip(t_log + 1, 1, w).astype(F32)


def _dot_nt(a, b):
    return lax.dot_general(a, b, (((1,), (1,)), ((), ())), preferred_element_type=F32)


def _dot_tn(a, b):
    return lax.dot_general(a, b, (((0,), (0,)), ((), ())), preferred_element_type=F32)


def _place():
    x, y, c = lax.axis_index("x"), lax.axis_index("y"), lax.axis_index("c")
    chips = [(1 - x, y), (x, 1 - y), (1 - x, 1 - y)]
    return x, y, c, chips


def _remote(src, dst, send_sem, recv_sem, to):
    return pltpu.make_async_remote_copy(src_ref=src, dst_ref=dst, send_sem=send_sem, recv_sem=recv_sem, device_id=to,
                                        device_id_type=MESH)


class Stage:
    def __init__(self, arrays, out_shapes, aliases, n_copies, copies):
        self.arrays, self.out_shapes, self.aliases, self.n_copies, self.copies = list(arrays), list(out_shapes), aliases, n_copies, copies


def _sds(a):
    return jax.ShapeDtypeStruct(a.shape, a.dtype)


def stage_gather_ici(bufs):
    n = len(bufs)

    def copies(ins, outs, send, recv):
        x, y, c, chips = _place()
        me = 2 * x + y
        sends, recvs = [], []
        for t in range(n):
            for j, (px, py) in enumerate(chips):
                k = 3 * t + j
                mine, theirs = outs[t].at[me, c], outs[t].at[2 * px + py, c]
                sends.append(_remote(mine, mine, send.at[k], recv.at[k], (px, py, c)))
                recvs.append(_remote(theirs, theirs, send.at[k], recv.at[k], (px, py, c)))
        return sends, recvs

    return Stage(bufs, [_sds(b) for b in bufs], {t: t for t in range(n)}, 3 * n, copies)


def stage_gather_d2d(bufs):
    n = len(bufs)

    def copies(ins, outs, send, recv):
        x, y, c, chips = _place()
        sends, recvs = [], []
        for t in range(n):
            for j, (px, py) in enumerate(chips):
                k = 3 * t + j
                got, sib = outs[t].at[2 * px + py, c], outs[t].at[2 * px + py, 1 - c]
                sends.append(_remote(got, got, send.at[k], recv.at[k], (x, y, 1 - c)))
                recvs.append(_remote(sib, sib, send.at[k], recv.at[k], (x, y, 1 - c)))
        return sends, recvs

    return Stage(bufs, [_sds(b) for b in bufs], {t: t for t in range(n)}, 3 * n, copies)


def stage_to_sibling(grads):
    n = len(grads)

    def copies(ins, outs, send, recv):
        x, y, c, _ = _place()
        sends, recvs = [], []
        for t in range(n):
            for k4 in range(4):
                k = 4 * t + k4
                sends.append(_remote(ins[t].at[k4, 1 - c], outs[t].at[k4], send.at[k], recv.at[k], (x, y, 1 - c)))
                recvs.append(_remote(outs[t].at[k4], outs[t].at[k4], send.at[k], recv.at[k], (x, y, 1 - c)))
        return sends, recvs

    return Stage(grads, [jax.ShapeDtypeStruct((4, *g.shape[2:]), g.dtype) for g in grads], {}, 4 * n, copies)


def stage_to_chips(presums):
    n = len(presums)

    def copies(ins, outs, send, recv):
        x, y, c, chips = _place()
        sends, recvs = [], []
        for t in range(n):
            for j, (px, py) in enumerate(chips):
                k = 3 * t + j
                sends.append(_remote(ins[t].at[2 * px + py], outs[t].at[j], send.at[k], recv.at[k], (px, py, c)))
                recvs.append(_remote(outs[t].at[j], outs[t].at[j], send.at[k], recv.at[k], (px, py, c)))
        return sends, recvs

    return Stage(presums, [jax.ShapeDtypeStruct((3, *p.shape[1:]), p.dtype) for p in presums], {}, 3 * n, copies)


def stage_from_sibling(halves):
    n = len(halves)

    def copies(ins, outs, send, recv):
        x, y, c, _ = _place()
        sends = [_remote(ins[t], outs[t], send.at[t], recv.at[t], (x, y, 1 - c)) for t in range(n)]
        recvs = [_remote(outs[t], outs[t], send.at[t], recv.at[t], (x, y, 1 - c)) for t in range(n)]
        return sends, recvs

    return Stage(halves, [_sds(h) for h in halves], {}, n, copies)


def _any_specs(n):
    return [pl.BlockSpec(memory_space=pl.ANY)] * n


def _staged_call(body, *, grid, in_specs, out_specs, out_shape, scratch_shapes, name, semantics, inputs, stages=()):
    n_in, n_out, n_scr = len(in_specs), len(out_specs), len(scratch_shapes)
    st_arrays = [a for s in stages for a in s.arrays]
    st_outs = [o for s in stages for o in s.out_shapes]
    st_sems = [pltpu.SemaphoreType.DMA((s.n_copies,)) for s in stages for _ in range(2)]
    aliases = {}
    at_in, at_out = n_in, n_out
    for s in stages:
        for a, o in s.aliases.items():
            aliases[at_in + a] = at_out + o
        at_in += len(s.arrays)
        at_out += len(s.out_shapes)

    def full_body(*refs):
        pos = 0

        def take(count):
            nonlocal pos
            part = refs[pos:pos + count]
            pos += count
            return part

        ins, s_ins, outs, s_outs, scr = take(n_in), take(len(st_arrays)), take(n_out), take(len(st_outs)), take(n_scr)
        s_sems = refs[pos:]

        def each_stage(action):
            at_i = at_o = 0
            for idx, s in enumerate(stages):
                sends, recvs = s.copies(s_ins[at_i:at_i + len(s.arrays)], s_outs[at_o:at_o + len(s.out_shapes)],
                                        s_sems[2 * idx], s_sems[2 * idx + 1])
                action(sends, recvs)
                at_i += len(s.arrays)
                at_o += len(s.out_shapes)

        if stages:
            ids = [pl.program_id(a) for a in range(len(grid))]
            first = functools.reduce(jnp.logical_and, [i == 0 for i in ids])
            last = functools.reduce(jnp.logical_and, [i == g - 1 for i, g in zip(ids, grid)])

            def start(sends, recvs):
                for cp in sends:
                    cp.start()

            def finish(sends, recvs):
                for cp in recvs:
                    cp.wait_recv()
                for cp in sends:
                    cp.wait_send()

            @pl.when(first)
            def _():
                each_stage(start)

        body(*ins, *outs, *scr)

        if stages:
            @pl.when(last)
            def _():
                each_stage(finish)

    sem = tuple("arbitrary" for _ in grid) if stages else tuple(semantics)
    res = pl.pallas_call(
        full_body, grid=grid, in_specs=[*in_specs, *_any_specs(len(st_arrays))], out_specs=[*out_specs, *_any_specs(len(st_outs))],
        out_shape=[*out_shape, *st_outs], scratch_shapes=[*scratch_shapes, *st_sems], input_output_aliases=aliases, name=name,
        compiler_params=_params(*sem))(*inputs, *st_arrays)
    outs, rest = list(res[:n_out]), list(res[n_out:])
    per_stage = []
    for s in stages:
        per_stage.append(rest[:len(s.out_shapes)])
        rest = rest[len(s.out_shapes):]
    return outs, per_stage


def comm_call(stages, *, name):
    return _staged_call(lambda: None, grid=(1,), in_specs=[], out_specs=[], out_shape=[], scratch_shapes=[], name=name,
                        semantics=("arbitrary",), inputs=[], stages=stages)[1]


def _matmul(kind, a, b, *, grid, a_spec, b_spec, out_spec, out_shape, acc_shape, name,
            a_pro=None, epilogue=None, extras=(), extra_specs=(), stages=()):
    nk = grid[2]
    n_extra = len(extras)

    def body(a_ref, b_ref, *rest):
        extra_refs = rest[:n_extra]
        o_ref = rest[n_extra]
        av = a_ref[...]
        if a_pro is not None:
            av = a_pro(av)
        av = av.astype(BF16)
        bv = b_ref[...].astype(BF16)
        if kind == "nn":
            p = jnp.dot(av, bv, preferred_element_type=F32)
        elif kind == "nt":
            p = _dot_nt(av, bv)
        else:
            p = _dot_tn(av, bv)

        def finish(r):
            if epilogue is not None:
                r = epilogue(r, *[e[...] for e in extra_refs])
            o_ref[...] = r.astype(o_ref.dtype)

        if nk == 1:
            finish(p)
        else:
            acc_ref = rest[n_extra + 1]
            k = pl.program_id(2)

            @pl.when(k == 0)
            def _():
                acc_ref[...] = p

            @pl.when(k > 0)
            def _():
                acc_ref[...] += p

            @pl.when(k == nk - 1)
            def _():
                finish(acc_ref[...])

    scratch = [] if nk == 1 else [pltpu.VMEM(acc_shape, F32)]
    outs, staged = _staged_call(body, grid=grid, in_specs=[a_spec, b_spec, *extra_specs], out_specs=[out_spec], out_shape=[out_shape],
                                scratch_shapes=scratch, name=name, semantics=("parallel", "parallel", "arbitrary"),
                                inputs=[a, b, *extras], stages=stages)
    return (outs[0], staged) if stages else outs[0]


def mm_nn(a, b, *, out_dtype, name, a_pro=None, epilogue=None, extras=(), stages=(), out_shards=1, tn_prefs=(512, 256, 128)):
    M, K = a.shape
    sharded = b.ndim == 3
    ns = b.shape[2] if sharded else b.shape[1]
    N = ns * b.shape[0] if sharded else ns
    if out_shards > 1:
        ns = N // out_shards
    tm = _pick(M, (1408, 1024, 512, 256, 128))
    tn = _pick(ns, tn_prefs)
    tk = _pick(K, (2048, 1408, 1024, 512, 256, 128))
    per = ns // tn
    if sharded:
        b_spec = pl.BlockSpec((None, tk, tn), lambda i, j, k: (j // per, k, j % per))
    else:
        b_spec = pl.BlockSpec((tk, tn), lambda i, j, k: (k, j))
    mn = pl.BlockSpec((tm, tn), lambda i, j, k: (i, j))
    if out_shards > 1:
        out_spec = pl.BlockSpec((None, tm, tn), lambda i, j, k: (j // per, i, j % per))
        out_shape = jax.ShapeDtypeStruct((out_shards, M, ns), out_dtype)
    else:
        out_spec, out_shape = mn, jax.ShapeDtypeStruct((M, N), out_dtype)
    return _matmul("nn", a, b, grid=(M // tm, N // tn, K // tk), a_spec=pl.BlockSpec((tm, tk), lambda i, j, k: (i, k)),
                   b_spec=b_spec, out_spec=out_spec, out_shape=out_shape, acc_shape=(tm, tn),
                   name=name, a_pro=a_pro, epilogue=epilogue, extras=extras, extra_specs=[mn] * len(extras), stages=stages)


def mm_nt(a, w, *, out_dtype, name, epilogue=None, extras=(), stages=()):
    M, N = a.shape
    sharded = w.ndim == 3
    kw = w.shape[1] if sharded else w.shape[0]
    ns = w.shape[2] if sharded else w.shape[1]
    tm = _pick(M, (1408, 512, 256, 128))
    tkw = _pick(kw, (1024, 512, 256, 128))
    tk = _pick(ns, (2048, 1280, 1024, 512, 256, 128))
    per = ns // tk
    if sharded:
        w_spec = pl.BlockSpec((None, tkw, tk), lambda i, j, k: (k // per, j, k % per))
    else:
        w_spec = pl.BlockSpec((tkw, tk), lambda i, j, k: (j, k))
    mo = pl.BlockSpec((tm, tkw), lambda i, j, k: (i, j))
    return _matmul("nt", a, w, grid=(M // tm, kw // tkw, N // tk), a_spec=pl.BlockSpec((tm, tk), lambda i, j, k: (i, k)),
                   b_spec=w_spec, out_spec=mo, out_shape=jax.ShapeDtypeStruct((M, kw), out_dtype), acc_shape=(tm, tkw),
                   name=name, epilogue=epilogue, extras=extras, extra_specs=[mo] * len(extras), stages=stages)


def mm_tn(a, g, *, shards, name, a_pro=None, stages=()):
    T, kw = a.shape
    N = g.shape[1]
    ns = N // shards
    tt = _pick(T, (1408, 512, 256, 128))
    tkw = _pick(kw, (1024, 512, 256, 128))
    tn = _pick(ns, (1280, 1024, 512, 256, 128))
    per = ns // tn
    if shards > 1:
        out_spec = pl.BlockSpec((None, tkw, tn), lambda i, j, k: (j // per, i, j % per))
        out_shape = jax.ShapeDtypeStruct((shards, kw, ns), F32)
    else:
        out_spec = pl.BlockSpec((tkw, tn), lambda i, j, k: (i, j))
        out_shape = jax.ShapeDtypeStruct((kw, N), F32)
    return _matmul("tn", a, g, grid=(kw // tkw, N // tn, T // tt), a_spec=pl.BlockSpec((tt, tkw), lambda i, j, k: (k, i)),
                   b_spec=pl.BlockSpec((tt, tn), lambda i, j, k: (k, j)), out_spec=out_spec, out_shape=out_shape,
                   acc_shape=(tkw, tn), name=name, a_pro=a_pro, stages=stages)


def _relu_sq(a):
    r = jnp.maximum(a.astype(F32), 0.0)
    return r * r


def rmsnorm_fwd(h, g, *, name):
    Tp, D = h.shape
    tr = _pick(Tp, (384, 256, 128))

    def body(h_ref, g_ref, u_ref, ut_ref):
        x = h_ref[...]
        r = lax.rsqrt(jnp.mean(x * x, axis=-1, keepdims=True) + NORM_EPS)
        u = (x * r) * g_ref[...]
        u_ref[...] = u.astype(BF16)
        ut_ref[...] = u.T.astype(BF16)

    row = pl.BlockSpec((tr, D), lambda i: (i, 0))
    return pl.pallas_call(body, grid=(Tp // tr,), in_specs=[row, pl.BlockSpec((1, D), lambda i: (0, 0))],
                          out_specs=[row, pl.BlockSpec((D, tr), lambda i: (0, i))],
                          out_shape=[jax.ShapeDtypeStruct((Tp, D), BF16), jax.ShapeDtypeStruct((D, Tp), BF16)], name=name,
                          compiler_params=_params("parallel"))(h, g)


def _rms_bwd_math(x, g, dy):
    r = lax.rsqrt(jnp.mean(x * x, axis=-1, keepdims=True) + NORM_EPS)
    xh = x * r
    dyg = dy * g
    dx = r * (dyg - xh * jnp.mean(dyg * xh, axis=-1, keepdims=True))
    return dx, xh


def final_loss(h2, target, gf, *, name):
    Tp, D = h2.shape
    nt = Tp // ROW_TILE

    def body(h_ref, t_ref, g_ref, dh_ref, dhb_ref, st_ref):
        i = pl.program_id(0)

        @pl.when(i == 0)
        def _():
            st_ref[...] = jnp.zeros_like(st_ref)

        x = h_ref[...]
        g = g_ref[...]
        r = lax.rsqrt(jnp.mean(x * x, axis=-1, keepdims=True) + NORM_EPS)
        xh = x * r
        err = jnp.where(i == nt - 1, 0.0, xh * g - t_ref[...])
        dout = err * (1.0 / D)
        dyg = dout * g
        dx = r * (dyg - xh * jnp.mean(dyg * xh, axis=-1, keepdims=True))
        dh_ref[...] = dx
        dhb_ref[...] = dx.astype(BF16)
        st_ref[0:1, :] += jnp.sum(dout * xh, axis=0, keepdims=True)
        st_ref[1:2, :] += jnp.sum(err * err, axis=0, keepdims=True) * (0.5 / D)

    row = pl.BlockSpec((ROW_TILE, D), lambda i: (i, 0))
    return pl.pallas_call(
        body, grid=(nt,),
        in_specs=[row, pl.BlockSpec((ROW_TILE, D), lambda i: (jnp.minimum(i, nt - 2), 0)), pl.BlockSpec((1, D), lambda i: (0, 0))],
        out_specs=[row, row, pl.BlockSpec((8, D), lambda i: (0, 0))],
        out_shape=[jax.ShapeDtypeStruct((Tp, D), F32), jax.ShapeDtypeStruct((Tp, D), BF16), jax.ShapeDtypeStruct((8, D), F32)],
        name=name, compiler_params=_params("arbitrary"))(h2, target, gf)


def rms_bwd(h, g, du, dres, *, name):
    Tp, D = h.shape
    tr = _pick(Tp, (384, 256, 128))

    def body(h_ref, g_ref, du_ref, dr_ref, dh_ref, dhb_ref, st_ref):
        @pl.when(pl.program_id(0) == 0)
        def _():
            st_ref[...] = jnp.zeros_like(st_ref)

        du_v = du_ref[...].astype(F32)
        dx, xh = _rms_bwd_math(h_ref[...], g_ref[...], du_v)
        dh = dr_ref[...] + dx
        dh_ref[...] = dh
        dhb_ref[...] = dh.astype(BF16)
        st_ref[0:1, :] += jnp.sum(du_v * xh, axis=0, keepdims=True)

    row = pl.BlockSpec((tr, D), lambda i: (i, 0))
    return pl.pallas_call(
        body, grid=(Tp // tr,), in_specs=[row, pl.BlockSpec((1, D), lambda i: (0, 0)), row, row],
        out_specs=[row, row, pl.BlockSpec((8, D), lambda i: (0, 0))],
        out_shape=[jax.ShapeDtypeStruct((Tp, D), F32), jax.ShapeDtypeStruct((Tp, D), BF16), jax.ShapeDtypeStruct((8, D), F32)],
        name=name, compiler_params=_params("arbitrary"))(h, g, du, dres)


def rms_bwd_input(h, g, du, dres, *, name, stages=()):
    Tp, D = h.shape
    nt = Tp // ROW_TILE

    def body(h_ref, g_ref, du_ref, dr_ref, gx_ref, gm_ref, st_ref):
        i = pl.program_id(0)

        @pl.when(i == 0)
        def _():
            st_ref[...] = jnp.zeros_like(st_ref)

        du_v = du_ref[...].astype(F32)
        dx, xh = _rms_bwd_math(h_ref[...], g_ref[...], du_v)
        dh = dr_ref[...] + dx
        st_ref[0:1, :] += jnp.sum(du_v * xh, axis=0, keepdims=True)

        @pl.when(i < nt - 1)
        def _():
            gx_ref[...] = dh

        @pl.when(i == nt - 1)
        def _():
            gm_ref[...] = dh[ROW_TILE - N_META:, :]

    row = pl.BlockSpec((ROW_TILE, D), lambda i: (i, 0))
    outs, staged = _staged_call(
        body, grid=(nt,), in_specs=[row, pl.BlockSpec((1, D), lambda i: (0, 0)), row, row],
        out_specs=[pl.BlockSpec((ROW_TILE, D), lambda i: (jnp.minimum(i, nt - 2), 0)), pl.BlockSpec((N_META, D), lambda i: (0, 0)),
                   pl.BlockSpec((8, D), lambda i: (0, 0))],
        out_shape=[jax.ShapeDtypeStruct((Tp - ROW_TILE, D), F32), jax.ShapeDtypeStruct((N_META, D), F32),
                   jax.ShapeDtypeStruct((8, D), F32)],
        scratch_shapes=[], name=name, semantics=("arbitrary",), inputs=[h, g, du, dres], stages=stages)
    return (outs, staged) if stages else outs


def _conv_shifts(ext):
    return tuple(pltpu.roll(ext, k, 0)[HIST:, :] for k in (1, 2, 3))


def _conv_taps(ext, cur, vec_ref, cs, rows):
    del rows
    x1, x2, x3 = _conv_shifts(ext)
    xc = (vec_ref[1:2, cs] + vec_ref[8:9, cs] * cur + vec_ref[7:8, cs] * x1 + vec_ref[6:7, cs] * x2 + vec_ref[5:6, cs] * x3)
    return xc, x1, x2, x3


def _window_sum_back(ext, w):
    s, sh = ext, 1
    while sh < w:
        s = s + pltpu.roll(s, sh, 0)
        sh *= 2
    return s[HIST:, :]


def _scan_rows(a, b, carry, *, reverse):
    rows = a.shape[0]
    rin = jnp.bitwise_and(lax.broadcasted_iota(jnp.int32, (rows, 1), 0), 7)
    sh = 1
    while sh < 8:
        keep = (rin < 8 - sh) if reverse else (rin >= sh)
        amount = rows - sh if reverse else sh
        a_sh = jnp.where(keep, pltpu.roll(a, amount, 0), 1.0)
        b_sh = jnp.where(keep, pltpu.roll(b, amount, 0), 0.0)
        b = b + a * b_sh
        a = a * a_sh
        sh *= 2
    out = [None] * (rows // 8)
    for g in (reversed(range(rows // 8)) if reverse else range(rows // 8)):
        hg = b[8 * g:8 * g + 8, :] + a[8 * g:8 * g + 8, :] * carry
        carry = hg[0:1, :] if reverse else hg[7:8, :]
        out[g] = hg
    return jnp.concatenate(out, axis=0)


def _lru_gates(xc, wa, wx, vec_ref, cs, sp):
    xcb = xc.astype(BF16)
    r = _sigmoid(jnp.dot(xcb, wa, preferred_element_type=F32) + vec_ref[2:3, cs])
    ig = _sigmoid(jnp.dot(xcb, wx, preferred_element_type=F32) + vec_ref[3:4, cs])
    log_a = (-LRU_C) * r * sp
    a = jnp.exp(log_a)
    a2 = a * a
    return xcb, r, ig, a, a2, _neg_expm1(2.0 * log_a, a2)


def mix_fwd(proj, pool_w, gate_a, gate_x, vecs, *, name, stages=()):
    Tp = proj.shape[0]
    D = proj.shape[1] // 5
    R = ROW_TILE
    nt = Tp // R
    H = D // HEAD_DIM
    PG = D // len(POOL_WINDOWS)

    def body(p_ref, pw_ref, wa_ref, wx_ref, vec_ref, m_ref, hs_ref, xc_ref, r_ref, ig_ref, a_ref, mu_ref, hist_p, hist_l, hcar, mtmp):
        i = pl.program_id(0)
        is_meta = i == 0

        @pl.when(is_meta)
        def _():
            hist_p[...] = jnp.zeros_like(hist_p)
            hist_l[...] = jnp.zeros_like(hist_l)
            hcar[...] = jnp.zeros_like(hcar)

        row, valid, t_log = _tile_masks(is_meta, R)

        for g, w in enumerate(POOL_WINDOWS):
            cs = slice(g * PG, (g + 1) * PG)
            v = p_ref[:, g * PG:(g + 1) * PG].astype(F32)
            ws = _window_sum_back(jnp.concatenate([hist_p[:, cs], v], axis=0), w)
            d = ws * _window_count_inv(t_log, w) - v
            y = jnp.dot(d.astype(BF16), pw_ref[g], preferred_element_type=F32)
            gp = p_ref[:, 3 * D + g * PG:3 * D + (g + 1) * PG].astype(F32)
            mtmp[:, cs] = _sigmoid(gp) * (y * vec_ref[0:1, cs])
            hist_p[:, cs] = v[R - HIST:, :]

        for h in range(H):
            cs = slice(h * HEAD_DIM, (h + 1) * HEAD_DIM)
            vl = p_ref[:, D + h * HEAD_DIM:D + (h + 1) * HEAD_DIM].astype(F32)
            xc, _, _, _ = _conv_taps(jnp.concatenate([hist_l[:, cs], vl], axis=0), vl, vec_ref, cs, R)
            sp = _softplus_neg(vec_ref[4:5, cs])
            xcb, r, ig, a, _, em = _lru_gates(xc, wa_ref[h], wx_ref[h], vec_ref, cs, sp)
            mult = jnp.sqrt(em)
            b = jnp.where(valid, mult * (ig * xc), 0.0)
            hs = _scan_rows(a, b, hcar[7:8, cs], reverse=False)
            hs_ref[:, cs] = hs
            xc_ref[:, cs], r_ref[:, cs], ig_ref[:, cs] = xcb, r.astype(BF16), ig.astype(BF16)
            a_ref[:, cs], mu_ref[:, cs] = a, mult
            hcar[:, cs] = hs[R - 8:, :]
            hist_l[:, cs] = vl[R - HIST:, :]
            ge, _ = _gelu_tanh(p_ref[:, 2 * D + h * HEAD_DIM:2 * D + (h + 1) * HEAD_DIM].astype(F32))
            gl = p_ref[:, 4 * D + h * HEAD_DIM:4 * D + (h + 1) * HEAD_DIM].astype(F32)
            m_ref[:, cs] = (mtmp[:, cs] + _sigmoid(gl) * (hs * ge)).astype(BF16)

    def tile(i):
        return (i + nt - 1) % nt

    full = lambda shape: pl.BlockSpec(shape, lambda i: (0,) * len(shape))
    outs, staged = _staged_call(
        body, grid=(nt,),
        in_specs=[pl.BlockSpec((R, 5 * D), lambda i: (tile(i), 0)), full(pool_w.shape), full(gate_a.shape), full(gate_x.shape),
                  full(vecs.shape)],
        out_specs=[pl.BlockSpec((R, D), lambda i: (tile(i), 0))] * 7,
        out_shape=[jax.ShapeDtypeStruct((Tp, D), dt) for dt in (BF16, F32, BF16, BF16, BF16, F32, F32)],
        scratch_shapes=[pltpu.VMEM((HIST, D), F32), pltpu.VMEM((HIST, D), F32), pltpu.VMEM((8, D), F32), pltpu.VMEM((R, D), F32)],
        name=name, semantics=("arbitrary",), inputs=[proj, pool_w, gate_a, gate_x, vecs], stages=stages)
    outs = [outs[0], outs[1], outs[2:]]
    return (outs, staged) if stages else outs


def mix_bwd(proj, hs, saved, dmerged, pool_w, gate_a, gate_x, vecs, *, name, stages=()):
    Tp = proj.shape[0]
    D = proj.shape[1] // 5
    R = ROW_TILE
    nt = Tp // R
    H = D // HEAD_DIM
    PG = D // len(POOL_WINDOWS)

    def body(p_ref, pprev_ref, hs_ref, hprev_ref, dm_ref, xc_ref, r_ref, ig_ref, a_ref, mu_ref, pw_ref, wa_ref, wx_ref, vec_ref,
             dp_ref, dpw_ref, dwa_ref, dwx_ref, dvec_ref, car_g, fut_dxc, fut_q):
        i = pl.program_id(0)
        is_meta = i == nt - 1

        @pl.when(i == 0)
        def _():
            dpw_ref[...] = jnp.zeros_like(dpw_ref)
            dwa_ref[...] = jnp.zeros_like(dwa_ref)
            dwx_ref[...] = jnp.zeros_like(dwx_ref)
            dvec_ref[...] = jnp.zeros_like(dvec_ref)
            car_g[...] = jnp.zeros_like(car_g)
            fut_dxc[...] = jnp.zeros_like(fut_dxc)
            fut_q[...] = jnp.zeros_like(fut_q)

        row, valid, t_log = _tile_masks(is_meta, R)
        keep_prev = jnp.logical_not(is_meta)

        def colsum(x):
            return jnp.sum(x, axis=0, keepdims=True)

        for g, w in enumerate(POOL_WINDOWS):
            cs = slice(g * PG, (g + 1) * PG)
            v = p_ref[:, g * PG:(g + 1) * PG].astype(F32)
            vprev = jnp.where(keep_prev, pprev_ref[:, g * PG:(g + 1) * PG].astype(F32), 0.0)
            inv_cnt = _window_count_inv(t_log, w)
            d = _window_sum_back(jnp.concatenate([vprev, v], axis=0), w) * inv_cnt - v
            d_bf = d.astype(BF16)
            y = jnp.dot(d_bf, pw_ref[g], preferred_element_type=F32)
            scale = vec_ref[0:1, cs]
            sg = _sigmoid(p_ref[:, 3 * D + g * PG:3 * D + (g + 1) * PG].astype(F32))
            dm = dm_ref[:, cs].astype(F32)
            dpo = dm * sg
            dp_ref[:, 3 * D + g * PG:3 * D + (g + 1) * PG] = (dm * (y * scale) * sg * (1.0 - sg)).astype(BF16)
            dvec_ref[0:1, cs] += colsum(dpo * y)
            dy = (dpo * scale).astype(BF16)
            dd = _dot_nt(dy, pw_ref[g])
            dpw_ref[g] += _dot_tn(d_bf, dy)
            q = dd * inv_cnt
            s, sh = jnp.concatenate([q, fut_q[:, cs]], axis=0), 1
            while sh < w:
                s = s + pltpu.roll(s, R + HIST - sh, 0)
                sh *= 2
            dp_ref[:, g * PG:(g + 1) * PG] = (s[:R, :] - dd).astype(BF16)
            fut_q[:, cs] = q[:HIST, :]

        for h in range(H):
            cs = slice(h * HEAD_DIM, (h + 1) * HEAD_DIM)
            pc = lambda blk: slice(blk * D + h * HEAD_DIM, blk * D + (h + 1) * HEAD_DIM)
            vl = p_ref[:, pc(1)].astype(F32)
            vlprev = jnp.where(keep_prev, pprev_ref[:, pc(1)].astype(F32), 0.0)
            x1, x2, x3 = _conv_shifts(jnp.concatenate([vlprev, vl], axis=0))
            lam = vec_ref[4:5, cs]
            sp = _softplus_neg(lam)
            xcb = xc_ref[:, cs]
            xc, r, ig = xcb.astype(F32), r_ref[:, cs].astype(F32), ig_ref[:, cs].astype(F32)
            a, mult = a_ref[:, cs], mu_ref[:, cs]
            a2 = a * a
            inv_mult = 1.0 / mult
            hsv = hs_ref[:, cs]
            hprev = jnp.where(row >= 1, pltpu.roll(hsv, 1, 0), hprev_ref[HIST - 1:HIST, cs])
            vg = p_ref[:, pc(2)].astype(F32)
            ge, th = _gelu_tanh(vg)
            sgl = _sigmoid(p_ref[:, pc(4)].astype(F32))
            dm = dm_ref[:, cs].astype(F32)
            dlo = dm * sgl
            dp_ref[:, pc(4)] = (dm * (hsv * ge) * sgl * (1.0 - sgl)).astype(BF16)
            dp_ref[:, pc(2)] = (dlo * hsv * _gelu_tanh_grad(vg, th)).astype(BF16)
            a_next = jnp.where(row < R - 1, pltpu.roll(a, R - 1, 0), 1.0)
            G = _scan_rows(a_next, dlo * ge, car_g[0:1, cs], reverse=True)
            car_g[:, cs] = (a * G)[0:8, :]
            da = jnp.where(valid, G * hprev, 0.0)
            db = jnp.where(valid, G, 0.0)
            dmult = db * (ig * xc)
            dig = db * (mult * xc)
            dxc = db * (mult * ig)
            dlog_a = da * a - dmult * (a2 * inv_mult)
            dvec_ref[4:5, cs] += colsum(dlog_a * r) * (-LRU_C)
            dr = dlog_a * ((-LRU_C) * sp)
            dpa = dr * r * (1.0 - r)
            dpx = dig * ig * (1.0 - ig)
            dpa_bf = dpa.astype(BF16)
            dpx_bf = dpx.astype(BF16)
            dwa_ref[h] += _dot_tn(xcb, dpa_bf)
            dwx_ref[h] += _dot_tn(xcb, dpx_bf)
            dvec_ref[2:3, cs] += colsum(dpa)
            dvec_ref[3:4, cs] += colsum(dpx)
            dxc = dxc + _dot_nt(dpa_bf, wa_ref[h]) + _dot_nt(dpx_bf, wx_ref[h])
            ext = jnp.concatenate([dxc, fut_dxc[:, cs]], axis=0)
            n = R + HIST
            dvl = (vec_ref[8:9, cs] * dxc + vec_ref[7:8, cs] * pltpu.roll(ext, n - 1, 0)[:R, :]
                   + vec_ref[6:7, cs] * pltpu.roll(ext, n - 2, 0)[:R, :] + vec_ref[5:6, cs] * pltpu.roll(ext, n - 3, 0)[:R, :])
            dp_ref[:, pc(1)] = dvl.astype(BF16)
            dvec_ref[1:2, cs] += colsum(dxc)
            dvec_ref[8:9, cs] += colsum(dxc * vl)
            dvec_ref[7:8, cs] += colsum(dxc * x1)
            dvec_ref[6:7, cs] += colsum(dxc * x2)
            dvec_ref[5:6, cs] += colsum(dxc * x3)
            fut_dxc[:, cs] = dxc[:HIST, :]

            @pl.when(is_meta)
            def _():
                dvec_ref[4:5, cs] = dvec_ref[4:5, cs] * (-_sigmoid(-lam))

    def tile(i):
        return (2 * nt - 2 - i) % nt

    def prev_blk(i):
        per = R // HIST
        return jnp.where(i == nt - 1, 0, jnp.where(i == nt - 2, Tp // HIST - 1, (nt - 2 - i) * per - 1))

    full = lambda shape: pl.BlockSpec(shape, lambda i: (0,) * len(shape))
    G_ = len(POOL_WINDOWS)
    outs, staged = _staged_call(
        body, grid=(nt,),
        in_specs=[pl.BlockSpec((R, 5 * D), lambda i: (tile(i), 0)), pl.BlockSpec((HIST, 5 * D), lambda i: (prev_blk(i), 0)),
                  pl.BlockSpec((R, D), lambda i: (tile(i), 0)), pl.BlockSpec((HIST, D), lambda i: (prev_blk(i), 0)),
                  *[pl.BlockSpec((R, D), lambda i: (tile(i), 0))] * 6,
                  full(pool_w.shape), full(gate_a.shape), full(gate_x.shape), full(vecs.shape)],
        out_specs=[pl.BlockSpec((R, 5 * D), lambda i: (tile(i), 0)), full((G_, PG, PG)), full((H, HEAD_DIM, HEAD_DIM)),
                   full((H, HEAD_DIM, HEAD_DIM)), full((16, D))],
        out_shape=[jax.ShapeDtypeStruct((Tp, 5 * D), BF16), jax.ShapeDtypeStruct((G_, PG, PG), F32),
                   jax.ShapeDtypeStruct((H, HEAD_DIM, HEAD_DIM), F32), jax.ShapeDtypeStruct((H, HEAD_DIM, HEAD_DIM), F32),
                   jax.ShapeDtypeStruct((16, D), F32)],
        scratch_shapes=[pltpu.VMEM((8, D), F32), pltpu.VMEM((HIST, D), F32), pltpu.VMEM((HIST, D), F32)],
        name=name, semantics=("arbitrary",), inputs=[proj, proj, hs, hs, dmerged, *saved, pool_w, gate_a, gate_x, vecs], stages=stages)
    return (outs, staged) if stages else outs


def _adamw_math(w, g, m, v):
    mn = ADAM_B1 * m + (1.0 - ADAM_B1) * g
    vn = ADAM_B2 * v + (1.0 - ADAM_B2) * (g * g)
    m_hat = mn / (1.0 - ADAM_B1 ** ADAM_STEP)
    v_hat = vn / (1.0 - ADAM_B2 ** ADAM_STEP)
    return -ADAM_LR * (m_hat / (jnp.sqrt(v_hat) + ADAM_EPS) + ADAM_WD * w), mn, vn


def adamw(w, g, m, v, *, name):
    rows, cols = w.shape
    tr = _pick(rows, (256, 128, 64, 32, 16, 8))

    def body(w_ref, g_ref, m_ref, v_ref, go_ref, d_ref, mo_ref, vo_ref):
        gv = g_ref[...]
        go_ref[...] = gv
        d_ref[...], mo_ref[...], vo_ref[...] = _adamw_math(w_ref[...], gv, m_ref[...], v_ref[...])

    blk = pl.BlockSpec((tr, cols), lambda i: (i, 0))
    sds = jax.ShapeDtypeStruct((rows, cols), F32)
    return pl.pallas_call(body, grid=(rows // tr,), in_specs=[blk] * 4, out_specs=[blk] * 4, out_shape=[sds] * 4, name=name,
                          compiler_params=_params("parallel"))(w, g, m, v)


def allgather8(block, *, name, reduce_sum=False, stages=()):
    rows, cols = block.shape

    def body(x_ref, out_ref, *scratch):
        if reduce_sum:
            buf, send_sems, recv_sems, local_sem = scratch
        else:
            buf = out_ref
            send_sems, recv_sems, local_sem = scratch
        x, y, c, chips = _place()
        me, sibling = (x, y, c), (x, y, 1 - c)

        def slot(px, py, pc):
            return buf.at[4 * px + 2 * py + pc]

        def copy(k, blk, to, src=None):
            return pltpu.make_async_remote_copy(src_ref=slot(*blk) if src is None else src, dst_ref=slot(*blk),
                                                send_sem=send_sems.at[k], recv_sem=recv_sems.at[k], device_id=to, device_id_type=MESH)

        mine = pltpu.make_async_copy(x_ref, slot(*me), local_sem)
        mine.start()
        first = [copy(0, me, sibling, src=x_ref)]
        first += [copy(1 + j, me, (*chip, c), src=x_ref) for j, chip in enumerate(chips)]
        for cp in first:
            cp.start()
        passed = [copy(4 + j, (*chip, c), sibling) for j, chip in enumerate(chips)]
        for j, chip in enumerate(chips):
            copy(1 + j, (*chip, c), me).wait_recv()
            passed[j].start()
        copy(0, sibling, me).wait_recv()
        for j, chip in enumerate(chips):
            copy(4 + j, (*chip, 1 - c), me).wait_recv()
        for cp in first + passed:
            cp.wait_send()
        mine.wait()
        if reduce_sum:
            acc = buf[0]
            for d in range(1, 8):
                acc = acc + buf[d]
            out_ref[...] = acc

    sems = [pltpu.SemaphoreType.DMA((7,)), pltpu.SemaphoreType.DMA((7,)), pltpu.SemaphoreType.DMA]
    if reduce_sum:
        out_shape = jax.ShapeDtypeStruct((rows, cols), block.dtype)
        scratch = [pltpu.VMEM((8, rows, cols), block.dtype)] + sems
    else:
        out_shape = jax.ShapeDtypeStruct((8, rows, cols), block.dtype)
        scratch = sems
    vmem = pl.BlockSpec(memory_space=pltpu.VMEM)
    outs, staged = _staged_call(body, grid=(1,), in_specs=[vmem], out_specs=[vmem], out_shape=[out_shape], scratch_shapes=scratch,
                                name=name, semantics=("arbitrary",), inputs=[block], stages=stages)
    return (outs[0], staged) if stages else outs[0]


def cast_into_slot(w, chip_arr, *, name):
    _, r, cols = w.shape
    tr = _pick(r, (256, 128, 64, 32, 16))

    def body(chip_ref, w_ref, o_ref):
        del chip_ref
        o_ref[...] = w_ref[...].astype(BF16)

    grid_spec = pltpu.PrefetchScalarGridSpec(
        num_scalar_prefetch=1, grid=(2, r // tr),
        in_specs=[pl.BlockSpec((None, tr, cols), lambda h, i, chip: (h, i, 0))],
        out_specs=pl.BlockSpec((None, None, tr, cols), lambda h, i, chip: (chip[0], h, i, 0)))
    return pl.pallas_call(body, grid_spec=grid_spec, out_shape=jax.ShapeDtypeStruct((4, 2, r, cols), BF16), name=name,
                          compiler_params=_params("parallel", "parallel"))(chip_arr, w)


def proj_with_gather(u, bufs, order_arr, *, name):
    Tp, K = u.shape
    n = len(bufs)
    Ns = bufs[0].shape[3]
    tm = _pick(Tp, (1408, 512, 256, 128))
    tc = _pick(Ns, (512, 256, 128))
    n_rows = Tp // tm

    def body(order_ref, u_ref, *refs):
        del order_ref
        o_ref, outs = refs[n], refs[n + 1:2 * n + 1]
        wbuf, ici_send, ici_recv, d2d_send, d2d_recv, load_sems = refs[2 * n + 1:]
        s, i = pl.program_id(0), pl.program_id(1)
        x, y, c, chips = _place()
        me = 2 * x + y

        def ici(t, j, landing):
            px, py = chips[j]
            slot = outs[t].at[2 * px + py, c] if landing else outs[t].at[me, c]
            return _remote(slot, slot, ici_send.at[t, j], ici_recv.at[t, j], (px, py, c))

        def d2d(t, j, landing):
            px, py = chips[j]
            slot = outs[t].at[2 * px + py, 1 - c] if landing else outs[t].at[2 * px + py, c]
            return _remote(slot, slot, d2d_send.at[t, j], d2d_recv.at[t, j], (x, y, 1 - c))

        def load(chip_idx):
            parts = [pltpu.make_async_copy(outs[0].at[chip_idx, hh], wbuf.at[pl.ds(hh * (K // 2), K // 2), :], load_sems.at[hh])
                     for hh in range(2)]
            for cp in parts:
                cp.start()
            for cp in parts:
                cp.wait()

        @pl.when(jnp.logical_and(s == 0, i == 0))
        def _():
            for t in range(n):
                for j in range(3):
                    ici(t, j, False).start()
            load(me)

        for j in range(3):
            @pl.when(jnp.logical_and(s == j + 1, i == 0))
            def _(j=j):
                ici(0, j, True).wait_recv()
                d2d(0, j, False).start()
                d2d(0, j, True).wait_recv()
                px, py = chips[j]
                load(2 * px + py)

        uv = u_ref[...]
        for cc in range(Ns // tc):
            o_ref[:, cc * tc:(cc + 1) * tc] = jnp.dot(uv, wbuf[:, cc * tc:(cc + 1) * tc], preferred_element_type=F32).astype(BF16)

        @pl.when(jnp.logical_and(s == 3, i == n_rows - 1))
        def _():
            for t in range(1, n):
                for j in range(3):
                    ici(t, j, True).wait_recv()
                    d2d(t, j, False).start()
            for t in range(1, n):
                for j in range(3):
                    d2d(t, j, True).wait_recv()
            for t in range(n):
                for j in range(3):
                    ici(t, j, False).wait_send()
                    d2d(t, j, False).wait_send()

    grid_spec = pltpu.PrefetchScalarGridSpec(
        num_scalar_prefetch=1, grid=(4, n_rows),
        in_specs=[pl.BlockSpec((tm, K), lambda s, i, order: (i, 0)), *_any_specs(n)],
        out_specs=[pl.BlockSpec((tm, Ns), lambda s, i, order: (i, order[s])), *_any_specs(n)],
        scratch_shapes=[pltpu.VMEM((K, Ns), BF16), pltpu.SemaphoreType.DMA((n, 3)), pltpu.SemaphoreType.DMA((n, 3)),
                        pltpu.SemaphoreType.DMA((n, 3)), pltpu.SemaphoreType.DMA((n, 3)), pltpu.SemaphoreType.DMA((2,))])
    res = pl.pallas_call(body, grid_spec=grid_spec, out_shape=[jax.ShapeDtypeStruct((Tp, 4 * Ns), BF16), *[_sds(b) for b in bufs]],
                         input_output_aliases={2 + t: 1 + t for t in range(n)}, name=name,
                         compiler_params=_params("arbitrary", "arbitrary"))(order_arr, u, *bufs)
    return res[0], list(res[1:])


def chip_presum(grad, recv, c_arr, *, name):
    _, _, r, cols = grad.shape
    tr = _pick(r, (256, 128, 64, 32, 16))

    def body(c_ref, g_ref, r_ref, o_ref):
        del c_ref
        o_ref[...] = (g_ref[...] + r_ref[...]).astype(BF16)

    grid_spec = pltpu.PrefetchScalarGridSpec(
        num_scalar_prefetch=1, grid=(4, r // tr),
        in_specs=[pl.BlockSpec((None, None, tr, cols), lambda k, i, c_ref: (k, c_ref[0], i, 0)),
                  pl.BlockSpec((None, tr, cols), lambda k, i, c_ref: (k, i, 0))],
        out_specs=pl.BlockSpec((None, tr, cols), lambda k, i, c_ref: (k, i, 0)))
    return pl.pallas_call(body, grid_spec=grid_spec, out_shape=jax.ShapeDtypeStruct((4, r, cols), BF16), name=name,
                          compiler_params=_params("parallel", "parallel"))(c_arr, grad, recv)


def final_half(grad, recv, got, mc_arr, *, name):
    _, _, r, cols = grad.shape
    tr = _pick(r, (256, 128, 64, 32, 16))

    def body(mc_ref, g_ref, r_ref, q_ref, o_ref):
        del mc_ref
        acc = g_ref[...] + r_ref[...]
        for j in range(3):
            acc = acc + q_ref[j].astype(F32)
        o_ref[...] = acc

    grid_spec = pltpu.PrefetchScalarGridSpec(
        num_scalar_prefetch=1, grid=(r // tr,),
        in_specs=[pl.BlockSpec((None, None, tr, cols), lambda i, mc: (mc[0], mc[1], i, 0)),
                  pl.BlockSpec((None, tr, cols), lambda i, mc: (mc[0], i, 0)),
                  pl.BlockSpec((3, tr, cols), lambda i, mc: (0, i, 0))],
        out_specs=pl.BlockSpec((tr, cols), lambda i, mc: (i, 0)))
    return pl.pallas_call(body, grid_spec=grid_spec, out_shape=jax.ShapeDtypeStruct((r, cols), F32), name=name,
                          compiler_params=_params("parallel"))(mc_arr, grad, recv, got)


def adamw_halves(w, mine, theirs, m, v, c_arr, *, name):
    _, r, cols = w.shape
    tr = _pick(r, (256, 128, 64, 32, 16, 8))

    def body(c_ref, w_ref, mine_ref, theirs_ref, m_ref, v_ref, go_ref, d_ref, mo_ref, vo_ref):
        gv = jnp.where(pl.program_id(0) == c_ref[0], mine_ref[...], theirs_ref[...])
        go_ref[...] = gv
        d_ref[...], mo_ref[...], vo_ref[...] = _adamw_math(w_ref[...], gv, m_ref[...], v_ref[...])

    blk = pl.BlockSpec((None, tr, cols), lambda h, i, c_ref: (h, i, 0))
    grid_spec = pltpu.PrefetchScalarGridSpec(
        num_scalar_prefetch=1, grid=(2, r // tr),
        in_specs=[blk, pl.BlockSpec((tr, cols), lambda h, i, c_ref: (jnp.where(h == c_ref[0], i, 0), 0)),
                  pl.BlockSpec((tr, cols), lambda h, i, c_ref: (jnp.where(h == c_ref[0], 0, i), 0)), blk, blk],
        out_specs=[blk] * 4)
    sds = jax.ShapeDtypeStruct((2, r, cols), F32)
    return pl.pallas_call(body, grid_spec=grid_spec, out_shape=[sds] * 4, name=name,
                          compiler_params=_params("parallel", "parallel"))(c_arr, w, mine, theirs, m, v)


def _pad_rows(a, rows):
    return jnp.pad(a, ((0, rows - a.shape[0]), (0, 0)))


def kernel(x, meta_tokens, norm1_g, w_in, pool_w, pool_scale, conv_w, conv_b, gate_a_w, gate_a_b, gate_x_w, gate_x_b, lru_lambda, w_out, norm2_g, mlp_w1, mlp_w2, final_g, loss_target, m_meta_tokens, m_norm1_g, m_w_in, m_pool_w, m_pool_scale, m_conv_w, m_conv_b, m_gate_a_w, m_gate_a_b, m_gate_x_w, m_gate_x_b, m_lru_lambda, m_w_out, m_norm2_g, m_mlp_w1, m_mlp_w2, m_final_g, v_meta_tokens, v_norm1_g, v_w_in, v_pool_w, v_pool_scale, v_conv_w, v_conv_b, v_gate_a_w, v_gate_a_b, v_gate_x_w, v_gate_x_b, v_lru_lambda, v_w_out, v_norm2_g, v_mlp_w1, v_mlp_w2, v_final_g):
    D = x.shape[-1]
    H = D // HEAD_DIM
    G = len(POOL_WINDOWS)
    PG = D // G
    ax, ay, ac = lax.axis_index("x"), lax.axis_index("y"), lax.axis_index("c")
    chip = 2 * ax + ay
    dshard = D // 4

    c_arr = jnp.reshape(ac, (1,)).astype(jnp.int32)
    chip_arr = jnp.reshape(chip, (1,)).astype(jnp.int32)
    mc_arr = jnp.stack([chip, ac]).astype(jnp.int32)
    names = ["w_in", "pool_w", "gate_a_w", "gate_x_w", "w_out", "mlp_w1", "mlp_w2"]
    big = [w_in, pool_w, gate_a_w, gate_x_w, w_out, mlp_w1, mlp_w2]

    def halves(w):
        w2d = w.reshape(-1, w.shape[-1])
        return w2d.reshape(2, w2d.shape[0] // 2, w2d.shape[1])

    bufs = [cast_into_slot(halves(w), chip_arr, name="cast_" + nm) for w, nm in zip(big, names)]
    order_arr = jnp.stack([chip, 2 * (1 - ax) + ay, 2 * ax + (1 - ay), 2 * (1 - ax) + (1 - ay)]).astype(jnp.int32)

    small_in = jnp.concatenate([meta_tokens, _pad_rows(conv_w[0], 8), _pad_rows(gate_a_b.reshape(1, dshard), 8),
                                _pad_rows(gate_x_b.reshape(1, dshard), 8)], axis=0)
    sm = allgather8(small_in, name="gather_small")[0::2]
    meta_f = sm[:, 0:16].transpose(1, 0, 2).reshape(N_META, D)
    conv_w_f = sm[:, 16:20].transpose(1, 0, 2).reshape(4, D)
    hd4 = HEAD_DIM // 4
    ba_f = sm[:, 24].reshape(4, H, hd4).transpose(1, 0, 2).reshape(1, D)
    bx_f = sm[:, 32].reshape(4, H, hd4).transpose(1, 0, 2).reshape(1, D)
    vecs = jnp.zeros((16, D), F32)
    for r0, part in ((0, pool_scale), (1, conv_b), (2, ba_f), (3, bx_f), (4, lru_lambda), (5, conv_w_f)):
        vecs = lax.dynamic_update_slice(vecs, part, (r0, 0))

    def chipwise(g, n_blocks, rows):
        return g.reshape(n_blocks, 4, rows, g.shape[-1]).transpose(1, 0, 2, 3).reshape(4, n_blocks * rows, g.shape[-1])

    def split2(g):
        return g.reshape(4, 2, g.shape[1] // 2, g.shape[2])

    def presum(t, g, r):
        return chip_presum(g, r, c_arr, name="presum_" + names[t])

    def total(t, g, r, q):
        return final_half(g, r, q, mc_arr, name="sum_" + names[t])

    xs, target, gfin = x[0], loss_target[0], final_g.reshape(1, D)
    h0 = jnp.concatenate([xs, jnp.zeros((ROW_TILE - N_META, D), F32), meta_f], axis=0)
    u, u_t = rmsnorm_fwd(h0, norm1_g, name="norm1")
    proj, (b_win, b_pool, b_ga, b_gx, b_wout) = proj_with_gather(u, bufs[0:5], order_arr, name="proj")
    w_in_f = b_win.reshape(4, w_in.shape[1], w_in.shape[2])
    pool_f = b_pool.reshape(4, G, PG // 4, PG).transpose(1, 0, 2, 3).reshape(G, PG, PG)
    ga_f = b_ga.reshape(4, H, HEAD_DIM // 4, HEAD_DIM).transpose(1, 0, 2, 3).reshape(H, HEAD_DIM, HEAD_DIM)
    gx_f = b_gx.reshape(4, H, HEAD_DIM // 4, HEAD_DIM).transpose(1, 0, 2, 3).reshape(H, HEAD_DIM, HEAD_DIM)
    w_out_f = b_wout.reshape(4 * w_out.shape[1], w_out.shape[2])
    (merged, hs, saved), [[b_w1]] = mix_fwd(proj, pool_f, ga_f, gx_f, vecs, name="mix_fwd", stages=[stage_gather_ici([bufs[5]])])
    h1, [[b_w1]] = mm_nn(merged, w_out_f, out_dtype=F32, name="out_proj", epilogue=lambda r, res: r + res, extras=(h0,),
                         stages=[stage_gather_d2d([b_w1])])
    w1_f = b_w1.reshape(4, mlp_w1.shape[1], mlp_w1.shape[2])
    u2, u2_t = rmsnorm_fwd(h1, norm2_g, name="norm2")
    a1, [[b_w2]] = mm_nn(u2, w1_f, out_dtype=BF16, name="mlp_up", stages=[stage_gather_ici([bufs[6]])])
    [[b_w2]] = comm_call([stage_gather_d2d([b_w2])], name="w2_to_sibling")
    w2_f = b_w2.reshape(4 * mlp_w2.shape[1], mlp_w2.shape[2])
    h2 = mm_nn(a1, w2_f, out_dtype=F32, name="mlp_down", a_pro=_relu_sq, epilogue=lambda r, res: r + res, extras=(h1,))
    dh2, dh2_bf, st_f = final_loss(h2, target, gfin, name="final_loss")

    da1 = mm_nt(dh2_bf, w2_f, out_dtype=BF16, name="mlp_down_dx",
                epilogue=lambda r, a: r * (2.0 * jnp.maximum(a.astype(F32), 0.0)), extras=(a1,))
    d_w2 = mm_tn(a1, dh2_bf, shards=1, name="mlp_down_dw", a_pro=_relu_sq)
    g6 = split2(d_w2.reshape(4, -1, d_w2.shape[-1]))
    d_w1, [[r6]] = mm_nn(u2_t, da1, out_dtype=F32, name="mlp_up_dw", out_shards=4, tn_prefs=(1024, 512, 256, 128),
                         stages=[stage_to_sibling([g6])])
    g5 = split2(d_w1)
    p6 = presum(6, g6, r6)
    du2, [[q6], [r5]] = mm_nt(da1, w1_f, out_dtype=F32, name="mlp_up_dx", stages=[stage_to_chips([p6]), stage_to_sibling([g5])])
    p5 = presum(5, g5, r5)
    f6 = total(6, g6, r6, q6)
    dh1, dh1_bf, st_2 = rms_bwd(h1, norm2_g, du2, dh2, name="norm2_bwd")
    dmerged, [[o6]] = mm_nt(dh1_bf, w_out_f, out_dtype=BF16, name="out_proj_dx", stages=[stage_from_sibling([f6])])
    d_wout = mm_tn(merged, dh1_bf, shards=1, name="out_proj_dw")
    g4 = split2(d_wout.reshape(4, -1, d_wout.shape[-1]))
    (dproj, d_pool, d_ga, d_gx, d_vecs), [[q5], [r4]] = mix_bwd(proj, hs, saved, dmerged, pool_f, ga_f, gx_f, vecs, name="mix_bwd",
                                                               stages=[stage_to_chips([p5]), stage_to_sibling([g4])])
    p4 = presum(4, g4, r4)
    f5 = total(5, g5, r5, q5)
    g1, g2, g3 = split2(chipwise(d_pool, G, PG // 4)), split2(chipwise(d_ga, H, hd4)), split2(chipwise(d_gx, H, hd4))
    half_k = u_t.shape[0] // 2
    wide = (1280, 1024, 512, 256, 128)
    u_theirs = lax.dynamic_slice_in_dim(u_t, (1 - ac) * half_k, half_k, axis=0)
    u_mine = lax.dynamic_slice_in_dim(u_t, ac * half_k, half_k, axis=0)
    g_theirs, [[r1, r2, r3], [o5], [q4]] = mm_nn(u_theirs, dproj, out_dtype=F32, out_shards=4, tn_prefs=wide, name="proj_dw_sibling_rows",
                                                stages=[stage_to_sibling([g1, g2, g3]), stage_from_sibling([f5]), stage_to_chips([p4])])
    g_mine, [[r0]] = mm_nn(u_mine, dproj, out_dtype=F32, out_shards=4, tn_prefs=wide, name="proj_dw_own_rows",
                           stages=[stage_from_sibling([g_theirs])])
    g0 = g_mine[:, None]
    p0 = chip_presum(g0, r0, jnp.zeros((1,), jnp.int32), name="presum_w_in")
    p1, p2, p3 = presum(1, g1, r1), presum(2, g2, r2), presum(3, g3, r3)
    f4 = total(4, g4, r4, q4)
    du, [[q1, q2, q3, q0], [o4]] = mm_nt(dproj, w_in_f, out_dtype=F32, name="proj_dx",
                                        stages=[stage_to_chips([p1, p2, p3, p0]), stage_from_sibling([f4])])
    f0 = final_half(g0, r0, q0, jnp.stack([chip, 0]).astype(jnp.int32), name="sum_w_in")
    f1, f2, f3 = total(1, g1, r1, q1), total(2, g2, r2, q2), total(3, g3, r3, q3)
    grad_x, d_meta, st_1 = rms_bwd_input(h0, norm1_g, du, dh1, name="norm1_bwd")

    small = jnp.concatenate([d_meta, d_vecs, st_1, st_2, st_f], axis=0)
    tot, [[o0, o1, o2, o3]] = allgather8(small, name="sum_small", reduce_sum=True, stages=[stage_from_sibling([f0, f1, f2, f3])])
    mine = [f0, f1, f2, f3, f4, f5, f6]
    theirs = [o0, o1, o2, o3, o4, o5, o6]
    loss = jnp.sum(tot[49])
    g_meta = lax.dynamic_slice_in_dim(tot[0:16], chip * dshard, dshard, axis=1)
    g_pool_scale, g_conv_b, g_lam = tot[16:17], tot[17:18], tot[20:21]
    g_ba = lax.dynamic_slice_in_dim(tot[18].reshape(H, HEAD_DIM), chip * hd4, hd4, axis=1)[None]
    g_bx = lax.dynamic_slice_in_dim(tot[19].reshape(H, HEAD_DIM), chip * hd4, hd4, axis=1)[None]
    g_conv_w = lax.dynamic_slice_in_dim(tot[21:25], chip * dshard, dshard, axis=1)[None]
    g_n1, g_n2, g_fin = tot[32:33], tot[40:41], tot[48]

    def step(w, g, m, v, nm):
        cols = w.shape[-1]
        outs = adamw(w.reshape(-1, cols), g.reshape(-1, cols), m.reshape(-1, cols), v.reshape(-1, cols), name="adamw_" + nm)
        return [o.reshape(w.shape) for o in outs]

    def step_big(t, w, m, v):
        outs = adamw_halves(halves(w), mine[t], theirs[t], halves(m), halves(v), c_arr, name="adamw_" + names[t])
        return [o.reshape(w.shape) for o in outs]

    res = dict(meta_tokens=step(meta_tokens, g_meta, m_meta_tokens, v_meta_tokens, "meta_tokens"),
               norm1_g=step(norm1_g, g_n1, m_norm1_g, v_norm1_g, "norm1_g"),
               w_in=step_big(0, w_in, m_w_in, v_w_in), pool_w=step_big(1, pool_w, m_pool_w, v_pool_w),
               pool_scale=step(pool_scale, g_pool_scale, m_pool_scale, v_pool_scale, "pool_scale"),
               conv_w=step(conv_w, g_conv_w, m_conv_w, v_conv_w, "conv_w"), conv_b=step(conv_b, g_conv_b, m_conv_b, v_conv_b, "conv_b"),
               gate_a_w=step_big(2, gate_a_w, m_gate_a_w, v_gate_a_w), gate_a_b=step(gate_a_b, g_ba, m_gate_a_b, v_gate_a_b, "gate_a_b"),
               gate_x_w=step_big(3, gate_x_w, m_gate_x_w, v_gate_x_w), gate_x_b=step(gate_x_b, g_bx, m_gate_x_b, v_gate_x_b, "gate_x_b"),
               lru_lambda=step(lru_lambda, g_lam, m_lru_lambda, v_lru_lambda, "lru_lambda"), w_out=step_big(4, w_out, m_w_out, v_w_out),
               norm2_g=step(norm2_g, g_n2, m_norm2_g, v_norm2_g, "norm2_g"), mlp_w1=step_big(5, mlp_w1, m_mlp_w1, v_mlp_w1),
               mlp_w2=step_big(6, mlp_w2, m_mlp_w2, v_mlp_w2), final_g=step(final_g, g_fin, m_final_g, v_final_g, "final_g"))
    order = list(res)
    return (loss, grad_x[None], *[res[n][0] for n in order], *[res[n][1] for n in order], *[res[n][2] for n in order],
            *[res[n][3] for n in order])
```

```python
import functools

import jax
import jax.numpy as jnp
from jax import lax
from jax.experimental import pallas as pl
from jax.experimental.pallas import tpu as pltpu

F32 = jnp.float32
BF16 = jnp.bfloat16
MESH = pl.DeviceIdType.MESH

NORM_EPS = 1e-6
N_META = 16
HEAD_DIM = 256
POOL_WINDOWS = (2, 4, 8, 16)
LRU_C = 8.0
ROW_TILE = 128
HIST = 16
VMEM_LIMIT_BYTES = 56 * 1024 * 1024
ADAM_LR, ADAM_B1, ADAM_B2, ADAM_EPS, ADAM_WD, ADAM_STEP = 0.001, 0.9, 0.999, 1e-08, 0.01, 10


def _pick(n, prefs):
    for p in prefs:
        if n % p == 0:
            return p
    return n


def _params(*sem):
    return pltpu.CompilerParams(dimension_semantics=sem, vmem_limit_bytes=VMEM_LIMIT_BYTES)


def _sigmoid(x):
    return 1.0 / (1.0 + jnp.exp(-x))


def _gelu_tanh(x):
    t = jnp.tanh(0.7978845608028654 * (x + 0.044715 * (x * x * x)))
    return 0.5 * x * (1.0 + t), t


def _gelu_tanh_grad(x, t):
    return 0.5 * (1.0 + t) + 0.5 * x * (1.0 - t * t) * (0.7978845608028654 * (1.0 + 3.0 * 0.044715 * x * x))


def _neg_expm1(x, exp_x):
    series = x * (-1.0 + x * (-0.5 + x * ((-1.0 / 6.0) + x * ((-1.0 / 24.0) + x * (-1.0 / 120.0)))))
    return jnp.where(x > -0.125, series, 1.0 - exp_x)


def _softplus_neg(lam):
    z = jnp.exp(-jnp.abs(lam))
    log1p_z = jnp.where(z < 0.01, z * (1.0 - z * (0.5 - z * (1.0 / 3.0))), jnp.log(1.0 + z))
    return jnp.maximum(-lam, 0.0) + log1p_z


def _tile_masks(is_meta, rows):
    row = lax.broadcasted_iota(jnp.int32, (rows, 1), 0)
    valid = jnp.logical_or(jnp.logical_not(is_meta), row >= rows - N_META)
    t_log = jnp.where(is_meta, row - (rows - N_META), 1 << 20)
    return row, valid, t_log


def _window_count_inv(t_log, w):
    return 1.0 / jnp.clip(t_log + 1, 1, w).astype(F32)


def _dot_nt(a, b):
    return lax.dot_general(a, b, (((1,), (1,)), ((), ())), preferred_element_type=F32)


def _dot_tn(a, b):
    return lax.dot_general(a, b, (((0,), (0,)), ((), ())), preferred_element_type=F32)


def _place():
    x, y, c = lax.axis_index("x"), lax.axis_index("y"), lax.axis_index("c")
    chips = [(1 - x, y), (x, 1 - y), (1 - x, 1 - y)]
    return x, y, c, chips


def _remote(src, dst, send_sem, recv_sem, to):
    return pltpu.make_async_remote_copy(src_ref=src, dst_ref=dst, send_sem=send_sem, recv_sem=recv_sem, device_id=to,
                                        device_id_type=MESH)


class Stage:
    def __init__(self, arrays, out_shapes, aliases, n_copies, copies):
        self.arrays, self.out_shapes, self.aliases, self.n_copies, self.copies = list(arrays), list(out_shapes), aliases, n_copies, copies


def _sds(a):
    return jax.ShapeDtypeStruct(a.shape, a.dtype)


def stage_gather_ici(bufs):
    n = len(bufs)

    def copies(ins, outs, send, recv):
        x, y, c, chips = _place()
        me = 2 * x + y
        sends, recvs = [], []
        for t in range(n):
            for j, (px, py) in enumerate(chips):
                k = 3 * t + j
                mine, theirs = outs[t].at[me, c], outs[t].at[2 * px + py, c]
                sends.append(_remote(mine, mine, send.at[k], recv.at[k], (px, py, c)))
                recvs.append(_remote(theirs, theirs, send.at[k], recv.at[k], (px, py, c)))
        return sends, recvs

    return Stage(bufs, [_sds(b) for b in bufs], {t: t for t in range(n)}, 3 * n, copies)


def stage_gather_d2d(bufs):
    n = len(bufs)

    def copies(ins, outs, send, recv):
        x, y, c, chips = _place()
        sends, recvs = [], []
        for t in range(n):
            for j, (px, py) in enumerate(chips):
                k = 3 * t + j
                got, sib = outs[t].at[2 * px + py, c], outs[t].at[2 * px + py, 1 - c]
                sends.append(_remote(got, got, send.at[k], recv.at[k], (x, y, 1 - c)))
                recvs.append(_remote(sib, sib, send.at[k], recv.at[k], (x, y, 1 - c)))
        return sends, recvs

    return Stage(bufs, [_sds(b) for b in bufs], {t: t for t in range(n)}, 3 * n, copies)


def stage_to_sibling(grads):
    n = len(grads)

    def copies(ins, outs, send, recv):
        x, y, c, _ = _place()
        sends, recvs = [], []
        for t in range(n):
            for k4 in range(4):
                k = 4 * t + k4
                sends.append(_remote(ins[t].at[k4, 1 - c], outs[t].at[k4], send.at[k], recv.at[k], (x, y, 1 - c)))
                recvs.append(_remote(outs[t].at[k4], outs[t].at[k4], send.at[k], recv.at[k], (x, y, 1 - c)))
        return sends, recvs

    return Stage(grads, [jax.ShapeDtypeStruct((4, *g.shape[2:]), g.dtype) for g in grads], {}, 4 * n, copies)


def stage_to_chips(presums):
    n = len(presums)

    def copies(ins, outs, send, recv):
        x, y, c, chips = _place()
        sends, recvs = [], []
        for t in range(n):
            for j, (px, py) in enumerate(chips):
                k = 3 * t + j
                sends.append(_remote(ins[t].at[2 * px + py], outs[t].at[j], send.at[k], recv.at[k], (px, py, c)))
                recvs.append(_remote(outs[t].at[j], outs[t].at[j], send.at[k], recv.at[k], (px, py, c)))
        return sends, recvs

    return Stage(presums, [jax.ShapeDtypeStruct((3, *p.shape[1:]), p.dtype) for p in presums], {}, 3 * n, copies)


def stage_from_sibling(halves):
    n = len(halves)

    def copies(ins, outs, send, recv):
        x, y, c, _ = _place()
        sends = [_remote(ins[t], outs[t], send.at[t], recv.at[t], (x, y, 1 - c)) for t in range(n)]
        recvs = [_remote(outs[t], outs[t], send.at[t], recv.at[t], (x, y, 1 - c)) for t in range(n)]
        return sends, recvs

    return Stage(halves, [_sds(h) for h in halves], {}, n, copies)


def _any_specs(n):
    return [pl.BlockSpec(memory_space=pl.ANY)] * n


def _staged_call(body, *, grid, in_specs, out_specs, out_shape, scratch_shapes, name, semantics, inputs, stages=()):
    n_in, n_out, n_scr = len(in_specs), len(out_specs), len(scratch_shapes)
    st_arrays = [a for s in stages for a in s.arrays]
    st_outs = [o for s in stages for o in s.out_shapes]
    st_sems = [pltpu.SemaphoreType.DMA((s.n_copies,)) for s in stages for _ in range(2)]
    aliases = {}
    at_in, at_out = n_in, n_out
    for s in stages:
        for a, o in s.aliases.items():
            aliases[at_in + a] = at_out + o
        at_in += len(s.arrays)
        at_out += len(s.out_shapes)

    def full_body(*refs):
        pos = 0

        def take(count):
            nonlocal pos
            part = refs[pos:pos + count]
            pos += count
            return part

        ins, s_ins, outs, s_outs, scr = take(n_in), take(len(st_arrays)), take(n_out), take(len(st_outs)), take(n_scr)
        s_sems = refs[pos:]

        def each_stage(action):
            at_i = at_o = 0
            for idx, s in enumerate(stages):
                sends, recvs = s.copies(s_ins[at_i:at_i + len(s.arrays)], s_outs[at_o:at_o + len(s.out_shapes)],
                                        s_sems[2 * idx], s_sems[2 * idx + 1])
                action(sends, recvs)
                at_i += len(s.arrays)
                at_o += len(s.out_shapes)

        if stages:
            ids = [pl.program_id(a) for a in range(len(grid))]
            first = functools.reduce(jnp.logical_and, [i == 0 for i in ids])
            last = functools.reduce(jnp.logical_and, [i == g - 1 for i, g in zip(ids, grid)])

            def start(sends, recvs):
                for cp in sends:
                    cp.start()

            def finish(sends, recvs):
                for cp in recvs:
                    cp.wait_recv()
                for cp in sends:
                    cp.wait_send()

            @pl.when(first)
            def _():
                each_stage(start)

        body(*ins, *outs, *scr)

        if stages:
            @pl.when(last)
            def _():
                each_stage(finish)

    sem = tuple("arbitrary" for _ in grid) if stages else tuple(semantics)
    res = pl.pallas_call(
        full_body, grid=grid, in_specs=[*in_specs, *_any_specs(len(st_arrays))], out_specs=[*out_specs, *_any_specs(len(st_outs))],
        out_shape=[*out_shape, *st_outs], scratch_shapes=[*scratch_shapes, *st_sems], input_output_aliases=aliases, name=name,
        compiler_params=_params(*sem))(*inputs, *st_arrays)
    outs, rest = list(res[:n_out]), list(res[n_out:])
    per_stage = []
    for s in stages:
        per_stage.append(rest[:len(s.out_shapes)])
        rest = rest[len(s.out_shapes):]
    return outs, per_stage


def comm_call(stages, *, name):
    return _staged_call(lambda: None, grid=(1,), in_specs=[], out_specs=[], out_shape=[], scratch_shapes=[], name=name,
                        semantics=("arbitrary",), inputs=[], stages=stages)[1]


def _matmul(kind, a, b, *, grid, a_spec, b_spec, out_spec, out_shape, acc_shape, name,
            a_pro=None, epilogue=None, extras=(), extra_specs=(), stages=()):
    nk = grid[2]
    n_extra = len(extras)

    def body(a_ref, b_ref, *rest):
        extra_refs = rest[:n_extra]
        o_ref = rest[n_extra]
        av = a_ref[...]
        if a_pro is not None:
            av = a_pro(av)
        av = av.astype(BF16)
        bv = b_ref[...].astype(BF16)
        if kind == "nn":
            p = jnp.dot(av, bv, preferred_element_type=F32)
        elif kind == "nt":
            p = _dot_nt(av, bv)
        else:
            p = _dot_tn(av, bv)

        def finish(r):
            if epilogue is not None:
                r = epilogue(r, *[e[...] for e in extra_refs])
            o_ref[...] = r.astype(o_ref.dtype)

        if nk == 1:
            finish(p)
        else:
            acc_ref = rest[n_extra + 1]
            k = pl.program_id(2)

            @pl.when(k == 0)
            def _():
                acc_ref[...] = p

            @pl.when(k > 0)
            def _():
                acc_ref[...] += p

            @pl.when(k == nk - 1)
            def _():
                finish(acc_ref[...])

    scratch = [] if nk == 1 else [pltpu.VMEM(acc_shape, F32)]
    outs, staged = _staged_call(body, grid=grid, in_specs=[a_spec, b_spec, *extra_specs], out_specs=[out_spec], out_shape=[out_shape],
                                scratch_shapes=scratch, name=name, semantics=("parallel", "parallel", "arbitrary"),
                                inputs=[a, b, *extras], stages=stages)
    return (outs[0], staged) if stages else outs[0]


def mm_nn(a, b, *, out_dtype, name, a_pro=None, epilogue=None, extras=(), stages=(), out_shards=1, tiles=(None, None, None)):
    M, K = a.shape
    sharded = b.ndim == 3
    ns = b.shape[2] if sharded else b.shape[1]
    N = ns * b.shape[0] if sharded else ns
    if out_shards > 1:
        ns = N // out_shards
    tm = tiles[0] or _pick(M, (1408, 1024, 512, 256, 128))
    tn = tiles[1] or _pick(ns, (512, 256, 128))
    tk = tiles[2] or _pick(K, (2048, 1408, 1024, 512, 256, 128))
    per = ns // tn
    if sharded:
        b_spec = pl.BlockSpec((None, tk, tn), lambda i, j, k: (j // per, k, j % per))
    else:
        b_spec = pl.BlockSpec((tk, tn), lambda i, j, k: (k, j))
    mn = pl.BlockSpec((tm, tn), lambda i, j, k: (i, j))
    if out_shards > 1:
        out_spec = pl.BlockSpec((None, tm, tn), lambda i, j, k: (j // per, i, j % per))
        out_shape = jax.ShapeDtypeStruct((out_shards, M, ns), out_dtype)
    else:
        out_spec, out_shape = mn, jax.ShapeDtypeStruct((M, N), out_dtype)
    return _matmul("nn", a, b, grid=(M // tm, N // tn, K // tk), a_spec=pl.BlockSpec((tm, tk), lambda i, j, k: (i, k)),
                   b_spec=b_spec, out_spec=out_spec, out_shape=out_shape, acc_shape=(tm, tn),
                   name=name, a_pro=a_pro, epilogue=epilogue, extras=extras, extra_specs=[mn] * len(extras), stages=stages)


def mm_nt(a, w, *, out_dtype, name, epilogue=None, extras=(), stages=(), tiles=(None, None, None)):
    M, N = a.shape
    sharded = w.ndim == 3
    kw = w.shape[1] if sharded else w.shape[0]
    ns = w.shape[2] if sharded else w.shape[1]
    tm = tiles[0] or _pick(M, (1408, 512, 256, 128))
    tkw = tiles[1] or _pick(kw, (1024, 512, 256, 128))
    tk = tiles[2] or _pick(ns, (2048, 1280, 1024, 512, 256, 128))
    per = ns // tk
    if sharded:
        w_spec = pl.BlockSpec((None, tkw, tk), lambda i, j, k: (k // per, j, k % per))
    else:
        w_spec = pl.BlockSpec((tkw, tk), lambda i, j, k: (j, k))
    mo = pl.BlockSpec((tm, tkw), lambda i, j, k: (i, j))
    return _matmul("nt", a, w, grid=(M // tm, kw // tkw, N // tk), a_spec=pl.BlockSpec((tm, tk), lambda i, j, k: (i, k)),
                   b_spec=w_spec, out_spec=mo, out_shape=jax.ShapeDtypeStruct((M, kw), out_dtype), acc_shape=(tm, tkw),
                   name=name, epilogue=epilogue, extras=extras, extra_specs=[mo] * len(extras), stages=stages)


def mm_tn(a, g, *, shards, name, out_dtype, a_pro=None, stages=(), tiles=(None, None, None)):
    T, kw = a.shape
    N = g.shape[1]
    ns = N // shards
    tkw = tiles[0] or _pick(kw, (1024, 512, 256, 128))
    tn = tiles[1] or _pick(ns, (1280, 1024, 512, 256, 128))
    tt = tiles[2] or _pick(T, (1408, 512, 256, 128))
    per = ns // tn
    if shards > 1:
        out_spec = pl.BlockSpec((None, tkw, tn), lambda i, j, k: (j // per, i, j % per))
        out_shape = jax.ShapeDtypeStruct((shards, kw, ns), out_dtype)
    else:
        out_spec = pl.BlockSpec((tkw, tn), lambda i, j, k: (i, j))
        out_shape = jax.ShapeDtypeStruct((kw, N), out_dtype)
    return _matmul("tn", a, g, grid=(kw // tkw, N // tn, T // tt), a_spec=pl.BlockSpec((tt, tkw), lambda i, j, k: (k, i)),
                   b_spec=pl.BlockSpec((tt, tn), lambda i, j, k: (k, j)), out_spec=out_spec, out_shape=out_shape,
                   acc_shape=(tkw, tn), name=name, a_pro=a_pro, stages=stages)


def _relu_sq(a):
    r = jnp.maximum(a, 0.0)
    return r * r


def rmsnorm_fwd(h, g, *, name):
    Tp, D = h.shape
    tr = _pick(Tp, (384, 256, 128))

    def body(h_ref, g_ref, u_ref, ut_ref):
        x = h_ref[...]
        r = lax.rsqrt(jnp.mean(x * x, axis=-1, keepdims=True) + NORM_EPS)
        u = (x * r) * g_ref[...]
        u_ref[...] = u.astype(BF16)
        ut_ref[...] = u.T.astype(BF16)

    row = pl.BlockSpec((tr, D), lambda i: (i, 0))
    return pl.pallas_call(body, grid=(Tp // tr,), in_specs=[row, pl.BlockSpec((1, D), lambda i: (0, 0))],
                          out_specs=[row, pl.BlockSpec((D, tr), lambda i: (0, i))],
                          out_shape=[jax.ShapeDtypeStruct((Tp, D), BF16), jax.ShapeDtypeStruct((D, Tp), BF16)], name=name,
                          compiler_params=_params("parallel"))(h, g)


def _rms_bwd_math(x, g, dy):
    r = lax.rsqrt(jnp.mean(x * x, axis=-1, keepdims=True) + NORM_EPS)
    xh = x * r
    dyg = dy * g
    dx = r * (dyg - xh * jnp.mean(dyg * xh, axis=-1, keepdims=True))
    return dx, xh


def final_loss(h2, target, gf, *, name):
    Tp, D = h2.shape
    nt = Tp // ROW_TILE

    def body(h_ref, t_ref, g_ref, dh_ref, dhb_ref, st_ref):
        i = pl.program_id(0)

        @pl.when(i == 0)
        def _():
            st_ref[...] = jnp.zeros_like(st_ref)

        x = h_ref[...]
        g = g_ref[...]
        r = lax.rsqrt(jnp.mean(x * x, axis=-1, keepdims=True) + NORM_EPS)
        xh = x * r
        err = jnp.where(i == nt - 1, 0.0, xh * g - t_ref[...])
        dout = err * (1.0 / D)
        dyg = dout * g
        dx = r * (dyg - xh * jnp.mean(dyg * xh, axis=-1, keepdims=True))
        dh_ref[...] = dx
        dhb_ref[...] = dx.astype(BF16)
        st_ref[0:1, :] += jnp.sum(dout * xh, axis=0, keepdims=True)
        st_ref[1:2, :] += jnp.sum(err * err, axis=0, keepdims=True) * (0.5 / D)

    row = pl.BlockSpec((ROW_TILE, D), lambda i: (i, 0))
    return pl.pallas_call(
        body, grid=(nt,),
        in_specs=[row, pl.BlockSpec((ROW_TILE, D), lambda i: (jnp.minimum(i, nt - 2), 0)), pl.BlockSpec((1, D), lambda i: (0, 0))],
        out_specs=[row, row, pl.BlockSpec((8, D), lambda i: (0, 0))],
        out_shape=[jax.ShapeDtypeStruct((Tp, D), F32), jax.ShapeDtypeStruct((Tp, D), BF16), jax.ShapeDtypeStruct((8, D), F32)],
        name=name, compiler_params=_params("arbitrary"))(h2, target, gf)


def rms_bwd(h, g, du, dres, *, name):
    Tp, D = h.shape
    tr = _pick(Tp, (384, 256, 128))

    def body(h_ref, g_ref, du_ref, dr_ref, dh_ref, dhb_ref, st_ref):
        @pl.when(pl.program_id(0) == 0)
        def _():
            st_ref[...] = jnp.zeros_like(st_ref)

        du_v = du_ref[...].astype(F32)
        dx, xh = _rms_bwd_math(h_ref[...], g_ref[...], du_v)
        dh = dr_ref[...] + dx
        dh_ref[...] = dh
        dhb_ref[...] = dh.astype(BF16)
        st_ref[0:1, :] += jnp.sum(du_v * xh, axis=0, keepdims=True)

    row = pl.BlockSpec((tr, D), lambda i: (i, 0))
    return pl.pallas_call(
        body, grid=(Tp // tr,), in_specs=[row, pl.BlockSpec((1, D), lambda i: (0, 0)), row, row],
        out_specs=[row, row, pl.BlockSpec((8, D), lambda i: (0, 0))],
        out_shape=[jax.ShapeDtypeStruct((Tp, D), F32), jax.ShapeDtypeStruct((Tp, D), BF16), jax.ShapeDtypeStruct((8, D), F32)],
        name=name, compiler_params=_params("arbitrary"))(h, g, du, dres)


def rms_bwd_input(h, g, du, dres, *, name, stages=()):
    Tp, D = h.shape
    nt = Tp // ROW_TILE

    def body(h_ref, g_ref, du_ref, dr_ref, gx_ref, gm_ref, st_ref):
        i = pl.program_id(0)

        @pl.when(i == 0)
        def _():
            st_ref[...] = jnp.zeros_like(st_ref)

        du_v = du_ref[...].astype(F32)
        dx, xh = _rms_bwd_math(h_ref[...], g_ref[...], du_v)
        dh = dr_ref[...] + dx
        st_ref[0:1, :] += jnp.sum(du_v * xh, axis=0, keepdims=True)

        @pl.when(i < nt - 1)
        def _():
            gx_ref[...] = dh

        @pl.when(i == nt - 1)
        def _():
            gm_ref[...] = dh[ROW_TILE - N_META:, :]

    row = pl.BlockSpec((ROW_TILE, D), lambda i: (i, 0))
    outs, staged = _staged_call(
        body, grid=(nt,), in_specs=[row, pl.BlockSpec((1, D), lambda i: (0, 0)), row, row],
        out_specs=[pl.BlockSpec((ROW_TILE, D), lambda i: (jnp.minimum(i, nt - 2), 0)), pl.BlockSpec((N_META, D), lambda i: (0, 0)),
                   pl.BlockSpec((8, D), lambda i: (0, 0))],
        out_shape=[jax.ShapeDtypeStruct((Tp - ROW_TILE, D), F32), jax.ShapeDtypeStruct((N_META, D), F32),
                   jax.ShapeDtypeStruct((8, D), F32)],
        scratch_shapes=[], name=name, semantics=("arbitrary",), inputs=[h, g, du, dres], stages=stages)
    return (outs, staged) if stages else outs


def _conv_shifts(ext):
    return tuple(pltpu.roll(ext, k, 0)[HIST:, :] for k in (1, 2, 3))


def _conv_taps(ext, cur, vec_ref, cs, rows):
    del rows
    x1, x2, x3 = _conv_shifts(ext)
    xc = (vec_ref[1:2, cs] + vec_ref[8:9, cs] * cur + vec_ref[7:8, cs] * x1 + vec_ref[6:7, cs] * x2 + vec_ref[5:6, cs] * x3)
    return xc, x1, x2, x3


def _window_sum_back(ext, w):
    s, sh = ext, 1
    while sh < w:
        s = s + pltpu.roll(s, sh, 0)
        sh *= 2
    return s[HIST:, :]


def _scan_rows(a, b, carry, *, reverse):
    rows = a.shape[0]
    rin = jnp.bitwise_and(lax.broadcasted_iota(jnp.int32, (rows, 1), 0), 7)
    sh = 1
    while sh < 8:
        keep = (rin < 8 - sh) if reverse else (rin >= sh)
        amount = rows - sh if reverse else sh
        a_sh = jnp.where(keep, pltpu.roll(a, amount, 0), 1.0)
        b_sh = jnp.where(keep, pltpu.roll(b, amount, 0), 0.0)
        b = b + a * b_sh
        a = a * a_sh
        sh *= 2
    out = [None] * (rows // 8)
    for g in (reversed(range(rows // 8)) if reverse else range(rows // 8)):
        hg = b[8 * g:8 * g + 8, :] + a[8 * g:8 * g + 8, :] * carry
        carry = hg[0:1, :] if reverse else hg[7:8, :]
        out[g] = hg
    return jnp.concatenate(out, axis=0)


def _lru_gates(xc, wa, wx, vec_ref, cs, sp):
    xcb = xc.astype(BF16)
    r = _sigmoid(jnp.dot(xcb, wa, preferred_element_type=F32) + vec_ref[2:3, cs])
    ig = _sigmoid(jnp.dot(xcb, wx, preferred_element_type=F32) + vec_ref[3:4, cs])
    log_a = (-LRU_C) * r * sp
    a = jnp.exp(log_a)
    a2 = a * a
    return xcb, r, ig, a, a2, _neg_expm1(2.0 * log_a, a2)


def mix_fwd(proj, pool_w, gate_a, gate_x, vecs, *, name, stages=()):
    Tp = proj.shape[0]
    D = proj.shape[1] // 5
    R = ROW_TILE
    nt = Tp // R
    H = D // HEAD_DIM
    PG = D // len(POOL_WINDOWS)

    def body(p_ref, pw_ref, wa_ref, wx_ref, vec_ref, m_ref, hs_ref, xc_ref, r_ref, ig_ref, a_ref, mu_ref, hist_p, hist_l, hcar, mtmp):
        i = pl.program_id(0)
        is_meta = i == 0

        @pl.when(is_meta)
        def _():
            hist_p[...] = jnp.zeros_like(hist_p)
            hist_l[...] = jnp.zeros_like(hist_l)
            hcar[...] = jnp.zeros_like(hcar)

        row, valid, t_log = _tile_masks(is_meta, R)

        for g, w in enumerate(POOL_WINDOWS):
            cs = slice(g * PG, (g + 1) * PG)
            v = p_ref[:, g * PG:(g + 1) * PG].astype(F32)
            ws = _window_sum_back(jnp.concatenate([hist_p[:, cs], v], axis=0), w)
            d = ws * _window_count_inv(t_log, w) - v
            y = jnp.dot(d.astype(BF16), pw_ref[g], preferred_element_type=F32)
            gp = p_ref[:, 3 * D + g * PG:3 * D + (g + 1) * PG].astype(F32)
            mtmp[:, cs] = _sigmoid(gp) * (y * vec_ref[0:1, cs])
            hist_p[:, cs] = v[R - HIST:, :]

        for h in range(H):
            cs = slice(h * HEAD_DIM, (h + 1) * HEAD_DIM)
            vl = p_ref[:, D + h * HEAD_DIM:D + (h + 1) * HEAD_DIM].astype(F32)
            xc, _, _, _ = _conv_taps(jnp.concatenate([hist_l[:, cs], vl], axis=0), vl, vec_ref, cs, R)
            sp = _softplus_neg(vec_ref[4:5, cs])
            xcb, r, ig, a, _, em = _lru_gates(xc, wa_ref[h], wx_ref[h], vec_ref, cs, sp)
            mult = jnp.sqrt(em)
            b = jnp.where(valid, mult * (ig * xc), 0.0)
            hs = _scan_rows(a, b, hcar[7:8, cs], reverse=False)
            hs_ref[:, cs] = hs
            xc_ref[:, cs], r_ref[:, cs], ig_ref[:, cs] = xcb, r.astype(BF16), ig.astype(BF16)
            a_ref[:, cs], mu_ref[:, cs] = a, mult
            hcar[:, cs] = hs[R - 8:, :]
            hist_l[:, cs] = vl[R - HIST:, :]
            ge, _ = _gelu_tanh(p_ref[:, 2 * D + h * HEAD_DIM:2 * D + (h + 1) * HEAD_DIM].astype(F32))
            gl = p_ref[:, 4 * D + h * HEAD_DIM:4 * D + (h + 1) * HEAD_DIM].astype(F32)
            m_ref[:, cs] = (mtmp[:, cs] + _sigmoid(gl) * (hs * ge)).astype(BF16)

    def tile(i):
        return (i + nt - 1) % nt

    full = lambda shape: pl.BlockSpec(shape, lambda i: (0,) * len(shape))
    outs, staged = _staged_call(
        body, grid=(nt,),
        in_specs=[pl.BlockSpec((R, 5 * D), lambda i: (tile(i), 0)), full(pool_w.shape), full(gate_a.shape), full(gate_x.shape),
                  full(vecs.shape)],
        out_specs=[pl.BlockSpec((R, D), lambda i: (tile(i), 0))] * 7,
        out_shape=[jax.ShapeDtypeStruct((Tp, D), dt) for dt in (BF16, F32, BF16, BF16, BF16, F32, F32)],
        scratch_shapes=[pltpu.VMEM((HIST, D), F32), pltpu.VMEM((HIST, D), F32), pltpu.VMEM((8, D), F32), pltpu.VMEM((R, D), F32)],
        name=name, semantics=("arbitrary",), inputs=[proj, pool_w, gate_a, gate_x, vecs], stages=stages)
    outs = [outs[0], outs[1], outs[2:]]
    return (outs, staged) if stages else outs


def mix_bwd(proj, hs, saved, dmerged, pool_w, gate_a, gate_x, vecs, *, name, stages=()):
    Tp = proj.shape[0]
    D = proj.shape[1] // 5
    R = ROW_TILE
    nt = Tp // R
    H = D // HEAD_DIM
    PG = D // len(POOL_WINDOWS)

    def body(p_ref, pprev_ref, hs_ref, hprev_ref, dm_ref, xc_ref, r_ref, ig_ref, a_ref, mu_ref, pw_ref, wa_ref, wx_ref, vec_ref,
             dp_ref, dpw_ref, dwa_ref, dwx_ref, dvec_ref, car_g, fut_dxc, fut_q):
        i = pl.program_id(0)
        is_meta = i == nt - 1

        @pl.when(i == 0)
        def _():
            dpw_ref[...] = jnp.zeros_like(dpw_ref)
            dwa_ref[...] = jnp.zeros_like(dwa_ref)
            dwx_ref[...] = jnp.zeros_like(dwx_ref)
            dvec_ref[...] = jnp.zeros_like(dvec_ref)
            car_g[...] = jnp.zeros_like(car_g)
            fut_dxc[...] = jnp.zeros_like(fut_dxc)
            fut_q[...] = jnp.zeros_like(fut_q)

        row, valid, t_log = _tile_masks(is_meta, R)
        keep_prev = jnp.logical_not(is_meta)

        def colsum(x):
            return jnp.sum(x, axis=0, keepdims=True)

        for g, w in enumerate(POOL_WINDOWS):
            cs = slice(g * PG, (g + 1) * PG)
            v = p_ref[:, g * PG:(g + 1) * PG].astype(F32)
            vprev = jnp.where(keep_prev, pprev_ref[:, g * PG:(g + 1) * PG].astype(F32), 0.0)
            inv_cnt = _window_count_inv(t_log, w)
            d = _window_sum_back(jnp.concatenate([vprev, v], axis=0), w) * inv_cnt - v
            d_bf = d.astype(BF16)
            y = jnp.dot(d_bf, pw_ref[g], preferred_element_type=F32)
            scale = vec_ref[0:1, cs]
            sg = _sigmoid(p_ref[:, 3 * D + g * PG:3 * D + (g + 1) * PG].astype(F32))
            dm = dm_ref[:, cs].astype(F32)
            dpo = dm * sg
            dp_ref[:, 3 * D + g * PG:3 * D + (g + 1) * PG] = (dm * (y * scale) * sg * (1.0 - sg)).astype(BF16)
            dvec_ref[0:1, cs] += colsum(dpo * y)
            dy = (dpo * scale).astype(BF16)
            dd = _dot_nt(dy, pw_ref[g])
            dpw_ref[g] += _dot_tn(d_bf, dy)
            q = dd * inv_cnt
            s, sh = jnp.concatenate([q, fut_q[:, cs]], axis=0), 1
            while sh < w:
                s = s + pltpu.roll(s, R + HIST - sh, 0)
                sh *= 2
            dp_ref[:, g * PG:(g + 1) * PG] = (s[:R, :] - dd).astype(BF16)
            fut_q[:, cs] = q[:HIST, :]

        for h in range(H):
            cs = slice(h * HEAD_DIM, (h + 1) * HEAD_DIM)
            pc = lambda blk: slice(blk * D + h * HEAD_DIM, blk * D + (h + 1) * HEAD_DIM)
            vl = p_ref[:, pc(1)].astype(F32)
            vlprev = jnp.where(keep_prev, pprev_ref[:, pc(1)].astype(F32), 0.0)
            x1, x2, x3 = _conv_shifts(jnp.concatenate([vlprev, vl], axis=0))
            lam = vec_ref[4:5, cs]
            sp = _softplus_neg(lam)
            xcb = xc_ref[:, cs]
            xc, r, ig = xcb.astype(F32), r_ref[:, cs].astype(F32), ig_ref[:, cs].astype(F32)
            a, mult = a_ref[:, cs], mu_ref[:, cs]
            a2 = a * a
            inv_mult = 1.0 / mult
            hsv = hs_ref[:, cs]
            hprev = jnp.where(row >= 1, pltpu.roll(hsv, 1, 0), hprev_ref[HIST - 1:HIST, cs])
            vg = p_ref[:, pc(2)].astype(F32)
            ge, th = _gelu_tanh(vg)
            sgl = _sigmoid(p_ref[:, pc(4)].astype(F32))
            dm = dm_ref[:, cs].astype(F32)
            dlo = dm * sgl
            dp_ref[:, pc(4)] = (dm * (hsv * ge) * sgl * (1.0 - sgl)).astype(BF16)
            dp_ref[:, pc(2)] = (dlo * hsv * _gelu_tanh_grad(vg, th)).astype(BF16)
            a_next = jnp.where(row < R - 1, pltpu.roll(a, R - 1, 0), 1.0)
            G = _scan_rows(a_next, dlo * ge, car_g[0:1, cs], reverse=True)
            car_g[:, cs] = (a * G)[0:8, :]
            da = jnp.where(valid, G * hprev, 0.0)
            db = jnp.where(valid, G, 0.0)
            dmult = db * (ig * xc)
            dig = db * (mult * xc)
            dxc = db * (mult * ig)
            dlog_a = da * a - dmult * (a2 * inv_mult)
            dvec_ref[4:5, cs] += colsum(dlog_a * r) * (-LRU_C)
            dr = dlog_a * ((-LRU_C) * sp)
            dpa = dr * r * (1.0 - r)
            dpx = dig * ig * (1.0 - ig)
            dpa_bf = dpa.astype(BF16)
            dpx_bf = dpx.astype(BF16)
            dwa_ref[h] += _dot_tn(xcb, dpa_bf)
            dwx_ref[h] += _dot_tn(xcb, dpx_bf)
            dvec_ref[2:3, cs] += colsum(dpa)
            dvec_ref[3:4, cs] += colsum(dpx)
            dxc = dxc + _dot_nt(dpa_bf, wa_ref[h]) + _dot_nt(dpx_bf, wx_ref[h])
            ext = jnp.concatenate([dxc, fut_dxc[:, cs]], axis=0)
            n = R + HIST
            dvl = (vec_ref[8:9, cs] * dxc + vec_ref[7:8, cs] * pltpu.roll(ext, n - 1, 0)[:R, :]
                   + vec_ref[6:7, cs] * pltpu.roll(ext, n - 2, 0)[:R, :] + vec_ref[5:6, cs] * pltpu.roll(ext, n - 3, 0)[:R, :])
            dp_ref[:, pc(1)] = dvl.astype(BF16)
            dvec_ref[1:2, cs] += colsum(dxc)
            dvec_ref[8:9, cs] += colsum(dxc * vl)
            dvec_ref[7:8, cs] += colsum(dxc * x1)
            dvec_ref[6:7, cs] += colsum(dxc * x2)
            dvec_ref[5:6, cs] += colsum(dxc * x3)
            fut_dxc[:, cs] = dxc[:HIST, :]

            @pl.when(is_meta)
            def _():
                dvec_ref[4:5, cs] = dvec_ref[4:5, cs] * (-_sigmoid(-lam))

    def tile(i):
        return (2 * nt - 2 - i) % nt

    def prev_blk(i):
        per = R // HIST
        return jnp.where(i == nt - 1, 0, jnp.where(i == nt - 2, Tp // HIST - 1, (nt - 2 - i) * per - 1))

    full = lambda shape: pl.BlockSpec(shape, lambda i: (0,) * len(shape))
    G_ = len(POOL_WINDOWS)
    outs, staged = _staged_call(
        body, grid=(nt,),
        in_specs=[pl.BlockSpec((R, 5 * D), lambda i: (tile(i), 0)), pl.BlockSpec((HIST, 5 * D), lambda i: (prev_blk(i), 0)),
                  pl.BlockSpec((R, D), lambda i: (tile(i), 0)), pl.BlockSpec((HIST, D), lambda i: (prev_blk(i), 0)),
                  *[pl.BlockSpec((R, D), lambda i: (tile(i), 0))] * 6,
                  full(pool_w.shape), full(gate_a.shape), full(gate_x.shape), full(vecs.shape)],
        out_specs=[pl.BlockSpec((R, 5 * D), lambda i: (tile(i), 0)), full((G_, PG, PG)), full((H, HEAD_DIM, HEAD_DIM)),
                   full((H, HEAD_DIM, HEAD_DIM)), full((16, D))],
        out_shape=[jax.ShapeDtypeStruct((Tp, 5 * D), BF16), jax.ShapeDtypeStruct((G_, PG, PG), F32),
                   jax.ShapeDtypeStruct((H, HEAD_DIM, HEAD_DIM), F32), jax.ShapeDtypeStruct((H, HEAD_DIM, HEAD_DIM), F32),
                   jax.ShapeDtypeStruct((16, D), F32)],
        scratch_shapes=[pltpu.VMEM((8, D), F32), pltpu.VMEM((HIST, D), F32), pltpu.VMEM((HIST, D), F32)],
        name=name, semantics=("arbitrary",), inputs=[proj, proj, hs, hs, dmerged, *saved, pool_w, gate_a, gate_x, vecs], stages=stages)
    return (outs, staged) if stages else outs


def _adamw_math(w, g, m, v):
    mn = ADAM_B1 * m + (1.0 - ADAM_B1) * g
    vn = ADAM_B2 * v + (1.0 - ADAM_B2) * (g * g)
    m_hat = mn / (1.0 - ADAM_B1 ** ADAM_STEP)
    v_hat = vn / (1.0 - ADAM_B2 ** ADAM_STEP)
    return -ADAM_LR * (m_hat / (jnp.sqrt(v_hat) + ADAM_EPS) + ADAM_WD * w), mn, vn


def adamw(w, g, m, v, *, name):
    rows, cols = w.shape
    tr = _pick(rows, (256, 128, 64, 32, 16, 8))

    def body(w_ref, g_ref, m_ref, v_ref, go_ref, d_ref, mo_ref, vo_ref):
        gv = g_ref[...]
        go_ref[...] = gv
        d_ref[...], mo_ref[...], vo_ref[...] = _adamw_math(w_ref[...], gv, m_ref[...], v_ref[...])

    blk = pl.BlockSpec((tr, cols), lambda i: (i, 0))
    sds = jax.ShapeDtypeStruct((rows, cols), F32)
    return pl.pallas_call(body, grid=(rows // tr,), in_specs=[blk] * 4, out_specs=[blk] * 4, out_shape=[sds] * 4, name=name,
                          compiler_params=_params("parallel"))(w, g, m, v)


def allgather8(block, *, name, reduce_sum=False, stages=()):
    rows, cols = block.shape

    def body(x_ref, out_ref, *scratch):
        if reduce_sum:
            buf, send_sems, recv_sems, local_sem = scratch
        else:
            buf = out_ref
            send_sems, recv_sems, local_sem = scratch
        x, y, c, chips = _place()
        me, sibling = (x, y, c), (x, y, 1 - c)

        def slot(px, py, pc):
            return buf.at[4 * px + 2 * py + pc]

        def copy(k, blk, to, src=None):
            return pltpu.make_async_remote_copy(src_ref=slot(*blk) if src is None else src, dst_ref=slot(*blk),
                                                send_sem=send_sems.at[k], recv_sem=recv_sems.at[k], device_id=to, device_id_type=MESH)

        mine = pltpu.make_async_copy(x_ref, slot(*me), local_sem)
        mine.start()
        first = [copy(0, me, sibling, src=x_ref)]
        first += [copy(1 + j, me, (*chip, c), src=x_ref) for j, chip in enumerate(chips)]
        for cp in first:
            cp.start()
        passed = [copy(4 + j, (*chip, c), sibling) for j, chip in enumerate(chips)]
        for j, chip in enumerate(chips):
            copy(1 + j, (*chip, c), me).wait_recv()
            passed[j].start()
        copy(0, sibling, me).wait_recv()
        for j, chip in enumerate(chips):
            copy(4 + j, (*chip, 1 - c), me).wait_recv()
        for cp in first + passed:
            cp.wait_send()
        mine.wait()
        if reduce_sum:
            acc = buf[0]
            for d in range(1, 8):
                acc = acc + buf[d]
            out_ref[...] = acc

    sems = [pltpu.SemaphoreType.DMA((7,)), pltpu.SemaphoreType.DMA((7,)), pltpu.SemaphoreType.DMA]
    if reduce_sum:
        out_shape = jax.ShapeDtypeStruct((rows, cols), block.dtype)
        scratch = [pltpu.VMEM((8, rows, cols), block.dtype)] + sems
    else:
        out_shape = jax.ShapeDtypeStruct((8, rows, cols), block.dtype)
        scratch = sems
    vmem = pl.BlockSpec(memory_space=pltpu.VMEM)
    outs, staged = _staged_call(body, grid=(1,), in_specs=[vmem], out_specs=[vmem], out_shape=[out_shape], scratch_shapes=scratch,
                                name=name, semantics=("arbitrary",), inputs=[block], stages=stages)
    return (outs[0], staged) if stages else outs[0]


def cast_into_slot(w, chip_arr, *, name):
    _, r, cols = w.shape
    tr = _pick(r, (256, 128, 64, 32, 16))

    def body(chip_ref, w_ref, o_ref):
        del chip_ref
        o_ref[...] = w_ref[...].astype(BF16)

    grid_spec = pltpu.PrefetchScalarGridSpec(
        num_scalar_prefetch=1, grid=(2, r // tr),
        in_specs=[pl.BlockSpec((None, tr, cols), lambda h, i, chip: (h, i, 0))],
        out_specs=pl.BlockSpec((None, None, tr, cols), lambda h, i, chip: (chip[0], h, i, 0)))
    return pl.pallas_call(body, grid_spec=grid_spec, out_shape=jax.ShapeDtypeStruct((4, 2, r, cols), BF16), name=name,
                          compiler_params=_params("parallel", "parallel"))(chip_arr, w)


def proj_with_gather(u, bufs, order_arr, *, name):
    Tp, K = u.shape
    n = len(bufs)
    Ns = bufs[0].shape[3]
    tm = _pick(Tp, (1408, 512, 256, 128))
    tc = _pick(Ns, (512, 256, 128))
    n_rows = Tp // tm

    def body(order_ref, u_ref, *refs):
        del order_ref
        o_ref, outs = refs[n], refs[n + 1:2 * n + 1]
        wbuf, ici_send, ici_recv, d2d_send, d2d_recv, load_sems = refs[2 * n + 1:]
        s, i = pl.program_id(0), pl.program_id(1)
        x, y, c, chips = _place()
        me = 2 * x + y

        def ici(t, j, landing):
            px, py = chips[j]
            slot = outs[t].at[2 * px + py, c] if landing else outs[t].at[me, c]
            return _remote(slot, slot, ici_send.at[t, j], ici_recv.at[t, j], (px, py, c))

        def d2d(t, j, landing):
            px, py = chips[j]
            slot = outs[t].at[2 * px + py, 1 - c] if landing else outs[t].at[2 * px + py, c]
            return _remote(slot, slot, d2d_send.at[t, j], d2d_recv.at[t, j], (x, y, 1 - c))

        def load(chip_idx):
            parts = [pltpu.make_async_copy(outs[0].at[chip_idx, hh], wbuf.at[pl.ds(hh * (K // 2), K // 2), :], load_sems.at[hh])
                     for hh in range(2)]
            for cp in parts:
                cp.start()
            for cp in parts:
                cp.wait()

        @pl.when(jnp.logical_and(s == 0, i == 0))
        def _():
            for t in range(n):
                for j in range(3):
                    ici(t, j, False).start()
            load(me)

        for j in range(3):
            @pl.when(jnp.logical_and(s == j + 1, i == 0))
            def _(j=j):
                ici(0, j, True).wait_recv()
                d2d(0, j, False).start()
                d2d(0, j, True).wait_recv()
                px, py = chips[j]
                load(2 * px + py)

        uv = u_ref[...]
        for cc in range(Ns // tc):
            o_ref[:, cc * tc:(cc + 1) * tc] = jnp.dot(uv, wbuf[:, cc * tc:(cc + 1) * tc], preferred_element_type=F32).astype(BF16)

        @pl.when(jnp.logical_and(s == 3, i == n_rows - 1))
        def _():
            for t in range(1, n):
                for j in range(3):
                    ici(t, j, True).wait_recv()
                    d2d(t, j, False).start()
            for t in range(1, n):
                for j in range(3):
                    d2d(t, j, True).wait_recv()
            for t in range(n):
                for j in range(3):
                    ici(t, j, False).wait_send()
                    d2d(t, j, False).wait_send()

    grid_spec = pltpu.PrefetchScalarGridSpec(
        num_scalar_prefetch=1, grid=(4, n_rows),
        in_specs=[pl.BlockSpec((tm, K), lambda s, i, order: (i, 0)), *_any_specs(n)],
        out_specs=[pl.BlockSpec((tm, Ns), lambda s, i, order: (i, order[s])), *_any_specs(n)],
        scratch_shapes=[pltpu.VMEM((K, Ns), BF16), pltpu.SemaphoreType.DMA((n, 3)), pltpu.SemaphoreType.DMA((n, 3)),
                        pltpu.SemaphoreType.DMA((n, 3)), pltpu.SemaphoreType.DMA((n, 3)), pltpu.SemaphoreType.DMA((2,))])
    res = pl.pallas_call(body, grid_spec=grid_spec, out_shape=[jax.ShapeDtypeStruct((Tp, 4 * Ns), BF16), *[_sds(b) for b in bufs]],
                         input_output_aliases={2 + t: 1 + t for t in range(n)}, name=name,
                         compiler_params=_params("arbitrary", "arbitrary"))(order_arr, u, *bufs)
    return res[0], list(res[1:])


def chip_presum(grad, recv, c_arr, *, name):
    _, _, r, cols = grad.shape
    tr = _pick(r, (256, 128, 64, 32, 16))

    def body(c_ref, g_ref, r_ref, o_ref):
        del c_ref
        o_ref[...] = (g_ref[...].astype(F32) + r_ref[...].astype(F32)).astype(BF16)

    grid_spec = pltpu.PrefetchScalarGridSpec(
        num_scalar_prefetch=1, grid=(4, r // tr),
        in_specs=[pl.BlockSpec((None, None, tr, cols), lambda k, i, c_ref: (k, c_ref[0], i, 0)),
                  pl.BlockSpec((None, tr, cols), lambda k, i, c_ref: (k, i, 0))],
        out_specs=pl.BlockSpec((None, tr, cols), lambda k, i, c_ref: (k, i, 0)))
    return pl.pallas_call(body, grid_spec=grid_spec, out_shape=jax.ShapeDtypeStruct((4, r, cols), BF16), name=name,
                          compiler_params=_params("parallel", "parallel"))(c_arr, grad, recv)


def final_half(grad, recv, got, mc_arr, *, name):
    _, _, r, cols = grad.shape
    tr = _pick(r, (256, 128, 64, 32, 16))

    def body(mc_ref, g_ref, r_ref, q_ref, o_ref):
        del mc_ref
        acc = g_ref[...].astype(F32) + r_ref[...].astype(F32)
        for j in range(3):
            acc = acc + q_ref[j].astype(F32)
        o_ref[...] = acc

    grid_spec = pltpu.PrefetchScalarGridSpec(
        num_scalar_prefetch=1, grid=(r // tr,),
        in_specs=[pl.BlockSpec((None, None, tr, cols), lambda i, mc: (mc[0], mc[1], i, 0)),
                  pl.BlockSpec((None, tr, cols), lambda i, mc: (mc[0], i, 0)),
                  pl.BlockSpec((3, tr, cols), lambda i, mc: (0, i, 0))],
        out_specs=pl.BlockSpec((tr, cols), lambda i, mc: (i, 0)))
    return pl.pallas_call(body, grid_spec=grid_spec, out_shape=jax.ShapeDtypeStruct((r, cols), F32), name=name,
                          compiler_params=_params("parallel"))(mc_arr, grad, recv, got)


def adamw_halves(w, mine, theirs, m, v, c_arr, *, name):
    _, r, cols = w.shape
    tr = _pick(r, (256, 128, 64, 32, 16, 8))

    def body(c_ref, w_ref, mine_ref, theirs_ref, m_ref, v_ref, go_ref, d_ref, mo_ref, vo_ref):
        gv = jnp.where(pl.program_id(0) == c_ref[0], mine_ref[...], theirs_ref[...])
        go_ref[...] = gv
        d_ref[...], mo_ref[...], vo_ref[...] = _adamw_math(w_ref[...], gv, m_ref[...], v_ref[...])

    blk = pl.BlockSpec((None, tr, cols), lambda h, i, c_ref: (h, i, 0))
    grid_spec = pltpu.PrefetchScalarGridSpec(
        num_scalar_prefetch=1, grid=(2, r // tr),
        in_specs=[blk, pl.BlockSpec((tr, cols), lambda h, i, c_ref: (jnp.where(h == c_ref[0], i, 0), 0)),
                  pl.BlockSpec((tr, cols), lambda h, i, c_ref: (jnp.where(h == c_ref[0], 0, i), 0)), blk, blk],
        out_specs=[blk] * 4)
    sds = jax.ShapeDtypeStruct((2, r, cols), F32)
    return pl.pallas_call(body, grid_spec=grid_spec, out_shape=[sds] * 4, name=name,
                          compiler_params=_params("parallel", "parallel"))(c_arr, w, mine, theirs, m, v)


def _pad_rows(a, rows):
    return jnp.pad(a, ((0, rows - a.shape[0]), (0, 0)))


def kernel(x, meta_tokens, norm1_g, w_in, pool_w, pool_scale, conv_w, conv_b, gate_a_w, gate_a_b, gate_x_w, gate_x_b, lru_lambda, w_out, norm2_g, mlp_w1, mlp_w2, final_g, loss_target, m_meta_tokens, m_norm1_g, m_w_in, m_pool_w, m_pool_scale, m_conv_w, m_conv_b, m_gate_a_w, m_gate_a_b, m_gate_x_w, m_gate_x_b, m_lru_lambda, m_w_out, m_norm2_g, m_mlp_w1, m_mlp_w2, m_final_g, v_meta_tokens, v_norm1_g, v_w_in, v_pool_w, v_pool_scale, v_conv_w, v_conv_b, v_gate_a_w, v_gate_a_b, v_gate_x_w, v_gate_x_b, v_lru_lambda, v_w_out, v_norm2_g, v_mlp_w1, v_mlp_w2, v_final_g):
    D = x.shape[-1]
    H = D // HEAD_DIM
    G = len(POOL_WINDOWS)
    PG = D // G
    ax, ay, ac = lax.axis_index("x"), lax.axis_index("y"), lax.axis_index("c")
    chip = 2 * ax + ay
    dshard = D // 4

    c_arr = jnp.reshape(ac, (1,)).astype(jnp.int32)
    chip_arr = jnp.reshape(chip, (1,)).astype(jnp.int32)
    mc_arr = jnp.stack([chip, ac]).astype(jnp.int32)
    names = ["w_in", "pool_w", "gate_a_w", "gate_x_w", "w_out", "mlp_w1", "mlp_w2"]
    big = [w_in, pool_w, gate_a_w, gate_x_w, w_out, mlp_w1, mlp_w2]

    def halves(w):
        w2d = w.reshape(-1, w.shape[-1])
        return w2d.reshape(2, w2d.shape[0] // 2, w2d.shape[1])

    bufs = [cast_into_slot(halves(w), chip_arr, name="cast_" + nm) for w, nm in zip(big, names)]
    order_arr = jnp.stack([chip, 2 * (1 - ax) + ay, 2 * ax + (1 - ay), 2 * (1 - ax) + (1 - ay)]).astype(jnp.int32)

    small_in = jnp.concatenate([meta_tokens, _pad_rows(conv_w[0], 8), _pad_rows(gate_a_b.reshape(1, dshard), 8),
                                _pad_rows(gate_x_b.reshape(1, dshard), 8)], axis=0)
    sm = allgather8(small_in, name="gather_small")[0::2]
    meta_f = sm[:, 0:16].transpose(1, 0, 2).reshape(N_META, D)
    conv_w_f = sm[:, 16:20].transpose(1, 0, 2).reshape(4, D)
    hd4 = HEAD_DIM // 4
    ba_f = sm[:, 24].reshape(4, H, hd4).transpose(1, 0, 2).reshape(1, D)
    bx_f = sm[:, 32].reshape(4, H, hd4).transpose(1, 0, 2).reshape(1, D)
    vecs = jnp.zeros((16, D), F32)
    for r0, part in ((0, pool_scale), (1, conv_b), (2, ba_f), (3, bx_f), (4, lru_lambda), (5, conv_w_f)):
        vecs = lax.dynamic_update_slice(vecs, part, (r0, 0))

    def chipwise(g, n_blocks, rows):
        return g.reshape(n_blocks, 4, rows, g.shape[-1]).transpose(1, 0, 2, 3).reshape(4, n_blocks * rows, g.shape[-1])

    def split2(g):
        return g.reshape(4, 2, g.shape[1] // 2, g.shape[2])

    def presum(t, g, r):
        return chip_presum(g, r, c_arr, name="presum_" + names[t])

    def total(t, g, r, q):
        return final_half(g, r, q, mc_arr, name="sum_" + names[t])

    xs, target, gfin = x[0], loss_target[0], final_g.reshape(1, D)
    h0 = jnp.concatenate([xs, jnp.zeros((ROW_TILE - N_META, D), F32), meta_f], axis=0)
    u, u_t = rmsnorm_fwd(h0, norm1_g, name="norm1")
    proj, (b_win, b_pool, b_ga, b_gx, b_wout) = proj_with_gather(u, bufs[0:5], order_arr, name="proj")
    w_in_f = b_win.reshape(4, w_in.shape[1], w_in.shape[2])
    pool_f = b_pool.reshape(4, G, PG // 4, PG).transpose(1, 0, 2, 3).reshape(G, PG, PG)
    ga_f = b_ga.reshape(4, H, HEAD_DIM // 4, HEAD_DIM).transpose(1, 0, 2, 3).reshape(H, HEAD_DIM, HEAD_DIM)
    gx_f = b_gx.reshape(4, H, HEAD_DIM // 4, HEAD_DIM).transpose(1, 0, 2, 3).reshape(H, HEAD_DIM, HEAD_DIM)
    w_out_f = b_wout.reshape(4 * w_out.shape[1], w_out.shape[2])
    (merged, hs, saved), [[b_w1]] = mix_fwd(proj, pool_f, ga_f, gx_f, vecs, name="mix_fwd", stages=[stage_gather_ici([bufs[5]])])
    h1, [[b_w1]] = mm_nn(merged, w_out_f, out_dtype=F32, name="out_proj", epilogue=lambda r, res: r + res, extras=(h0,),
                         stages=[stage_gather_d2d([b_w1])])
    w1_f = b_w1.reshape(4, mlp_w1.shape[1], mlp_w1.shape[2])
    u2, u2_t = rmsnorm_fwd(h1, norm2_g, name="norm2")
    a1, [[b_w2]] = mm_nn(u2, w1_f, out_dtype=BF16, name="mlp_up", stages=[stage_gather_ici([bufs[6]])])
    [[b_w2]] = comm_call([stage_gather_d2d([b_w2])], name="w2_to_sibling")
    w2_f = b_w2.reshape(4 * mlp_w2.shape[1], mlp_w2.shape[2])
    Tp = h0.shape[0]
    h2 = mm_nn(a1, w2_f, out_dtype=F32, name="mlp_down", a_pro=_relu_sq, epilogue=lambda r, res: r + res, extras=(h1,))
    dh2, dh2_bf, st_f = final_loss(h2, target, gfin, name="final_loss")

    da1 = mm_nt(dh2_bf, w2_f, out_dtype=BF16, name="mlp_down_dx",
                epilogue=lambda r, a: r * (2.0 * jnp.maximum(a.astype(F32), 0.0)), extras=(a1,))
    d_w2 = mm_tn(a1, dh2_bf, shards=1, out_dtype=BF16, name="mlp_down_dw", a_pro=_relu_sq, tiles=(512, min(1024, D), Tp))
    g6 = split2(d_w2.reshape(4, -1, d_w2.shape[-1]))
    d_w1, [[r6]] = mm_nn(u2_t, da1, out_dtype=BF16, name="mlp_up_dw", out_shards=4, tiles=(min(1024, D), None, Tp),
                         stages=[stage_to_sibling([g6])])
    g5 = split2(d_w1)
    p6 = presum(6, g6, r6)
    du2, [[q6], [r5]] = mm_nt(da1, w1_f, out_dtype=F32, name="mlp_up_dx", stages=[stage_to_chips([p6]), stage_to_sibling([g5])])
    p5 = presum(5, g5, r5)
    f6 = total(6, g6, r6, q6)
    dh1, dh1_bf, st_2 = rms_bwd(h1, norm2_g, du2, dh2, name="norm2_bwd")
    dmerged, [[o6]] = mm_nt(dh1_bf, w_out_f, out_dtype=BF16, name="out_proj_dx", stages=[stage_from_sibling([f6])])
    d_wout = mm_tn(merged, dh1_bf, shards=1, out_dtype=BF16, name="out_proj_dw", tiles=(512, min(1024, D), Tp))
    g4 = split2(d_wout.reshape(4, -1, d_wout.shape[-1]))
    (dproj, d_pool, d_ga, d_gx, d_vecs), [[q5], [r4]] = mix_bwd(proj, hs, saved, dmerged, pool_f, ga_f, gx_f, vecs, name="mix_bwd",
                                                               stages=[stage_to_chips([p5]), stage_to_sibling([g4])])
    p4 = presum(4, g4, r4)
    f5 = total(5, g5, r5, q5)
    g1, g2, g3 = split2(chipwise(d_pool, G, PG // 4)), split2(chipwise(d_ga, H, hd4)), split2(chipwise(d_gx, H, hd4))
    half_k = u_t.shape[0] // 2
    whole_k = (half_k, None, Tp)
    u_theirs = lax.dynamic_slice_in_dim(u_t, (1 - ac) * half_k, half_k, axis=0)
    u_mine = lax.dynamic_slice_in_dim(u_t, ac * half_k, half_k, axis=0)
    g_theirs, [[r1, r2, r3], [o5], [q4]] = mm_nn(u_theirs, dproj, out_dtype=BF16, out_shards=4, tiles=whole_k, name="proj_dw_sibling_rows",
                                                stages=[stage_to_sibling([g1, g2, g3]), stage_from_sibling([f5]), stage_to_chips([p4])])
    g_mine, [[r0]] = mm_nn(u_mine, dproj, out_dtype=BF16, out_shards=4, tiles=whole_k, name="proj_dw_own_rows",
                           stages=[stage_from_sibling([g_theirs])])
    g0 = g_mine[:, None]
    p0 = chip_presum(g0, r0, jnp.zeros((1,), jnp.int32), name="presum_w_in")
    p1, p2, p3 = presum(1, g1, r1), presum(2, g2, r2), presum(3, g3, r3)
    f4 = total(4, g4, r4, q4)
    du, [[q1, q2, q3, q0], [o4]] = mm_nt(dproj, w_in_f, out_dtype=F32, name="proj_dx", tiles=(None, None, w_in_f.shape[2]),
                                        stages=[stage_to_chips([p1, p2, p3, p0]), stage_from_sibling([f4])])
    f0 = final_half(g0, r0, q0, jnp.stack([chip, 0]).astype(jnp.int32), name="sum_w_in")
    f1, f2, f3 = total(1, g1, r1, q1), total(2, g2, r2, q2), total(3, g3, r3, q3)
    grad_x, d_meta, st_1 = rms_bwd_input(h0, norm1_g, du, dh1, name="norm1_bwd")

    small = jnp.concatenate([d_meta, d_vecs, st_1, st_2, st_f], axis=0)
    tot, [[o0, o1, o2, o3]] = allgather8(small, name="sum_small", reduce_sum=True, stages=[stage_from_sibling([f0, f1, f2, f3])])
    mine = [f0, f1, f2, f3, f4, f5, f6]
    theirs = [o0, o1, o2, o3, o4, o5, o6]
    loss = jnp.sum(tot[49])
    g_meta = lax.dynamic_slice_in_dim(tot[0:16], chip * dshard, dshard, axis=1)
    g_pool_scale, g_conv_b, g_lam = tot[16:17], tot[17:18], tot[20:21]
    g_ba = lax.dynamic_slice_in_dim(tot[18].reshape(H, HEAD_DIM), chip * hd4, hd4, axis=1)[None]
    g_bx = lax.dynamic_slice_in_dim(tot[19].reshape(H, HEAD_DIM), chip * hd4, hd4, axis=1)[None]
    g_conv_w = lax.dynamic_slice_in_dim(tot[21:25], chip * dshard, dshard, axis=1)[None]
    g_n1, g_n2, g_fin = tot[32:33], tot[40:41], tot[48]

    def step(w, g, m, v, nm):
        cols = w.shape[-1]
        outs = adamw(w.reshape(-1, cols), g.reshape(-1, cols), m.reshape(-1, cols), v.reshape(-1, cols), name="adamw_" + nm)
        return [o.reshape(w.shape) for o in outs]

    def step_big(t, w, m, v):
        outs = adamw_halves(halves(w), mine[t], theirs[t], halves(m), halves(v), c_arr, name="adamw_" + names[t])
        return [o.reshape(w.shape) for o in outs]

    res = dict(meta_tokens=step(meta_tokens, g_meta, m_meta_tokens, v_meta_tokens, "meta_tokens"),
               norm1_g=step(norm1_g, g_n1, m_norm1_g, v_norm1_g, "norm1_g"),
               w_in=step_big(0, w_in, m_w_in, v_w_in), pool_w=step_big(1, pool_w, m_pool_w, v_pool_w),
               pool_scale=step(pool_scale, g_pool_scale, m_pool_scale, v_pool_scale, "pool_scale"),
               conv_w=step(conv_w, g_conv_w, m_conv_w, v_conv_w, "conv_w"), conv_b=step(conv_b, g_conv_b, m_conv_b, v_conv_b, "conv_b"),
               gate_a_w=step_big(2, gate_a_w, m_gate_a_w, v_gate_a_w), gate_a_b=step(gate_a_b, g_ba, m_gate_a_b, v_gate_a_b, "gate_a_b"),
               gate_x_w=step_big(3, gate_x_w, m_gate_x_w, v_gate_x_w), gate_x_b=step(gate_x_b, g_bx, m_gate_x_b, v_gate_x_b, "gate_x_b"),
               lru_lambda=step(lru_lambda, g_lam, m_lru_lambda, v_lru_lambda, "lru_lambda"), w_out=step_big(4, w_out, m_w_out, v_w_out),
               norm2_g=step(norm2_g, g_n2, m_norm2_g, v_norm2_g, "norm2_g"), mlp_w1=step_big(5, mlp_w1, m_mlp_w1, v_mlp_w1),
               mlp_w2=step_big(6, mlp_w2, m_mlp_w2, v_mlp_w2), final_g=step(final_g, g_fin, m_final_g, v_final_g, "final_g"))
    order = list(res)
    return (loss, grad_x[None], *[res[n][0] for n in order], *[res[n][1] for n in order], *[res[n][2] for n in order],
            *[res[n][3] for n in order])
```

```python
import functools

import jax
import jax.numpy as jnp
from jax import lax
from jax.experimental import pallas as pl
from jax.experimental.pallas import tpu as pltpu

F32 = jnp.float32
BF16 = jnp.bfloat16
MESH = pl.DeviceIdType.MESH

NORM_EPS = 1e-6
N_META = 16
HEAD_DIM = 256
POOL_WINDOWS = (2, 4, 8, 16)
LRU_C = 8.0
ROW_TILE = 128
HIST = 16
VMEM_LIMIT_BYTES = 56 * 1024 * 1024
ADAM_LR, ADAM_B1, ADAM_B2, ADAM_EPS, ADAM_WD, ADAM_STEP = 0.001, 0.9, 0.999, 1e-08, 0.01, 10


def _pick(n, prefs):
    for p in prefs:
        if n % p == 0:
            return p
    return n


def _params(*sem):
    return pltpu.CompilerParams(dimension_semantics=sem, vmem_limit_bytes=VMEM_LIMIT_BYTES)


def _sigmoid(x):
    return 1.0 / (1.0 + jnp.exp(-x))


def _gelu_tanh(x):
    t = jnp.tanh(0.7978845608028654 * (x + 0.044715 * (x * x * x)))
    return 0.5 * x * (1.0 + t), t


def _gelu_tanh_grad(x, t):
    return 0.5 * (1.0 + t) + 0.5 * x * (1.0 - t * t) * (0.7978845608028654 * (1.0 + 3.0 * 0.044715 * x * x))


def _neg_expm1(x, exp_x):
    series = x * (-1.0 + x * (-0.5 + x * ((-1.0 / 6.0) + x * ((-1.0 / 24.0) + x * (-1.0 / 120.0)))))
    return jnp.where(x > -0.125, series, 1.0 - exp_x)


def _softplus_neg(lam):
    z = jnp.exp(-jnp.abs(lam))
    log1p_z = jnp.where(z < 0.01, z * (1.0 - z * (0.5 - z * (1.0 / 3.0))), jnp.log(1.0 + z))
    return jnp.maximum(-lam, 0.0) + log1p_z


def _tile_masks(is_meta, rows):
    row = lax.broadcasted_iota(jnp.int32, (rows, 1), 0)
    valid = jnp.logical_or(jnp.logical_not(is_meta), row >= rows - N_META)
    t_log = jnp.where(is_meta, row - (rows - N_META), 1 << 20)
    return row, valid, t_log


def _window_count_inv(t_log, w):
    return 1.0 / jnp.clip(t_log + 1, 1, w).astype(F32)


def _dot_nt(a, b):
    return lax.dot_general(a, b, (((1,), (1,)), ((), ())), preferred_element_type=F32)


def _dot_tn(a, b):
    return lax.dot_general(a, b, (((0,), (0,)), ((), ())), preferred_element_type=F32)


def _place():
    x, y, c = lax.axis_index("x"), lax.axis_index("y"), lax.axis_index("c")
    chips = [(1 - x, y), (x, 1 - y), (1 - x, 1 - y)]
    return x, y, c, chips


def _remote(src, dst, send_sem, recv_sem, to):
    return pltpu.make_async_remote_copy(src_ref=src, dst_ref=dst, send_sem=send_sem, recv_sem=recv_sem, device_id=to,
                                        device_id_type=MESH)


class Stage:
    def __init__(self, arrays, out_shapes, aliases, n_copies, copies):
        self.arrays, self.out_shapes, self.aliases, self.n_copies, self.copies = list(arrays), list(out_shapes), aliases, n_copies, copies


def _sds(a):
    return jax.ShapeDtypeStruct(a.shape, a.dtype)


def stage_gather_ici(bufs):
    n = len(bufs)

    def copies(ins, outs, send, recv):
        x, y, c, chips = _place()
        me = 2 * x + y
        sends, recvs = [], []
        for t in range(n):
            for j, (px, py) in enumerate(chips):
                k = 3 * t + j
                mine, theirs = outs[t].at[me, c], outs[t].at[2 * px + py, c]
                sends.append(_remote(mine, mine, send.at[k], recv.at[k], (px, py, c)))
                recvs.append(_remote(theirs, theirs, send.at[k], recv.at[k], (px, py, c)))
        return sends, recvs

    return Stage(bufs, [_sds(b) for b in bufs], {t: t for t in range(n)}, 3 * n, copies)


def stage_gather_d2d(bufs):
    n = len(bufs)

    def copies(ins, outs, send, recv):
        x, y, c, chips = _place()
        sends, recvs = [], []
        for t in range(n):
            for j, (px, py) in enumerate(chips):
                k = 3 * t + j
                got, sib = outs[t].at[2 * px + py, c], outs[t].at[2 * px + py, 1 - c]
                sends.append(_remote(got, got, send.at[k], recv.at[k], (x, y, 1 - c)))
                recvs.append(_remote(sib, sib, send.at[k], recv.at[k], (x, y, 1 - c)))
        return sends, recvs

    return Stage(bufs, [_sds(b) for b in bufs], {t: t for t in range(n)}, 3 * n, copies)


def stage_to_sibling(grads):
    n = len(grads)

    def copies(ins, outs, send, recv):
        x, y, c, _ = _place()
        sends, recvs = [], []
        for t in range(n):
            for k4 in range(4):
                k = 4 * t + k4
                sends.append(_remote(ins[t].at[k4, 1 - c], outs[t].at[k4], send.at[k], recv.at[k], (x, y, 1 - c)))
                recvs.append(_remote(outs[t].at[k4], outs[t].at[k4], send.at[k], recv.at[k], (x, y, 1 - c)))
        return sends, recvs

    return Stage(grads, [jax.ShapeDtypeStruct((4, *g.shape[2:]), g.dtype) for g in grads], {}, 4 * n, copies)


def stage_to_chips(presums, peers=(0, 1, 2), into=None):
    n = len(presums)

    def copies(ins, outs, send, recv):
        x, y, c, chips = _place()
        sends, recvs = [], []
        for t in range(n):
            for slot, j in enumerate(peers):
                px, py = chips[j]
                k = len(peers) * t + slot
                sends.append(_remote(ins[t].at[2 * px + py], outs[t].at[j], send.at[k], recv.at[k], (px, py, c)))
                recvs.append(_remote(outs[t].at[j], outs[t].at[j], send.at[k], recv.at[k], (px, py, c)))
        return sends, recvs

    out_shapes = [jax.ShapeDtypeStruct((3, *p.shape[1:]), p.dtype) for p in presums]
    if into is None:
        return Stage(presums, out_shapes, {}, len(peers) * n, copies)
    return Stage([*presums, *into], out_shapes, {n + t: t for t in range(n)}, len(peers) * n, copies)


def stage_from_sibling(halves):
    n = len(halves)

    def copies(ins, outs, send, recv):
        x, y, c, _ = _place()
        sends = [_remote(ins[t], outs[t], send.at[t], recv.at[t], (x, y, 1 - c)) for t in range(n)]
        recvs = [_remote(outs[t], outs[t], send.at[t], recv.at[t], (x, y, 1 - c)) for t in range(n)]
        return sends, recvs

    return Stage(halves, [_sds(h) for h in halves], {}, n, copies)


def _any_specs(n):
    return [pl.BlockSpec(memory_space=pl.ANY)] * n


def _staged_call(body, *, grid, in_specs, out_specs, out_shape, scratch_shapes, name, semantics, inputs, stages=()):
    n_in, n_out, n_scr = len(in_specs), len(out_specs), len(scratch_shapes)
    st_arrays = [a for s in stages for a in s.arrays]
    st_outs = [o for s in stages for o in s.out_shapes]
    st_sems = [pltpu.SemaphoreType.DMA((s.n_copies,)) for s in stages for _ in range(2)]
    aliases = {}
    at_in, at_out = n_in, n_out
    for s in stages:
        for a, o in s.aliases.items():
            aliases[at_in + a] = at_out + o
        at_in += len(s.arrays)
        at_out += len(s.out_shapes)

    def full_body(*refs):
        pos = 0

        def take(count):
            nonlocal pos
            part = refs[pos:pos + count]
            pos += count
            return part

        ins, s_ins, outs, s_outs, scr = take(n_in), take(len(st_arrays)), take(n_out), take(len(st_outs)), take(n_scr)
        s_sems = refs[pos:]

        def each_stage(action):
            at_i = at_o = 0
            for idx, s in enumerate(stages):
                sends, recvs = s.copies(s_ins[at_i:at_i + len(s.arrays)], s_outs[at_o:at_o + len(s.out_shapes)],
                                        s_sems[2 * idx], s_sems[2 * idx + 1])
                action(sends, recvs)
                at_i += len(s.arrays)
                at_o += len(s.out_shapes)

        if stages:
            ids = [pl.program_id(a) for a in range(len(grid))]
            first = functools.reduce(jnp.logical_and, [i == 0 for i in ids])
            last = functools.reduce(jnp.logical_and, [i == g - 1 for i, g in zip(ids, grid)])

            def start(sends, recvs):
                for cp in sends:
                    cp.start()

            def finish(sends, recvs):
                for cp in recvs:
                    cp.wait_recv()
                for cp in sends:
                    cp.wait_send()

            @pl.when(first)
            def _():
                each_stage(start)

        body(*ins, *outs, *scr)

        if stages:
            @pl.when(last)
            def _():
                each_stage(finish)

    sem = tuple("arbitrary" for _ in grid) if stages else tuple(semantics)
    res = pl.pallas_call(
        full_body, grid=grid, in_specs=[*in_specs, *_any_specs(len(st_arrays))], out_specs=[*out_specs, *_any_specs(len(st_outs))],
        out_shape=[*out_shape, *st_outs], scratch_shapes=[*scratch_shapes, *st_sems], input_output_aliases=aliases, name=name,
        compiler_params=_params(*sem))(*inputs, *st_arrays)
    outs, rest = list(res[:n_out]), list(res[n_out:])
    per_stage = []
    for s in stages:
        per_stage.append(rest[:len(s.out_shapes)])
        rest = rest[len(s.out_shapes):]
    return outs, per_stage


def comm_call(stages, *, name):
    return _staged_call(lambda: None, grid=(1,), in_specs=[], out_specs=[], out_shape=[], scratch_shapes=[], name=name,
                        semantics=("arbitrary",), inputs=[], stages=stages)[1]


def _matmul(kind, a, b, *, grid, a_spec, b_spec, out_spec, out_shape, acc_shape, name,
            a_pro=None, epilogue=None, extras=(), extra_specs=(), stages=()):
    nk = grid[2]
    n_extra = len(extras)

    def body(a_ref, b_ref, *rest):
        extra_refs = rest[:n_extra]
        o_ref = rest[n_extra]
        av = a_ref[...]
        if a_pro is not None:
            av = a_pro(av)
        av = av.astype(BF16)
        bv = b_ref[...].astype(BF16)
        if kind == "nn":
            p = jnp.dot(av, bv, preferred_element_type=F32)
        elif kind == "nt":
            p = _dot_nt(av, bv)
        else:
            p = _dot_tn(av, bv)

        def finish(r):
            if epilogue is not None:
                r = epilogue(r, *[e[...] for e in extra_refs])
            o_ref[...] = r.astype(o_ref.dtype)

        if nk == 1:
            finish(p)
        else:
            acc_ref = rest[n_extra + 1]
            k = pl.program_id(2)

            @pl.when(k == 0)
            def _():
                acc_ref[...] = p

            @pl.when(k > 0)
            def _():
                acc_ref[...] += p

            @pl.when(k == nk - 1)
            def _():
                finish(acc_ref[...])

    scratch = [] if nk == 1 else [pltpu.VMEM(acc_shape, F32)]
    outs, staged = _staged_call(body, grid=grid, in_specs=[a_spec, b_spec, *extra_specs], out_specs=[out_spec], out_shape=[out_shape],
                                scratch_shapes=scratch, name=name, semantics=("parallel", "parallel", "arbitrary"),
                                inputs=[a, b, *extras], stages=stages)
    return (outs[0], staged) if stages else outs[0]


def mm_nn(a, b, *, out_dtype, name, a_pro=None, epilogue=None, extras=(), stages=(), out_shards=1, tiles=(None, None, None)):
    M, K = a.shape
    sharded = b.ndim == 3
    ns = b.shape[2] if sharded else b.shape[1]
    N = ns * b.shape[0] if sharded else ns
    if out_shards > 1:
        ns = N // out_shards
    tm = tiles[0] or _pick(M, (1408, 1024, 512, 256, 128))
    tn = tiles[1] or _pick(ns, (512, 256, 128))
    tk = tiles[2] or _pick(K, (2048, 1408, 1024, 512, 256, 128))
    per = ns // tn
    if sharded:
        b_spec = pl.BlockSpec((None, tk, tn), lambda i, j, k: (j // per, k, j % per))
    else:
        b_spec = pl.BlockSpec((tk, tn), lambda i, j, k: (k, j))
    mn = pl.BlockSpec((tm, tn), lambda i, j, k: (i, j))
    if out_shards > 1:
        out_spec = pl.BlockSpec((None, tm, tn), lambda i, j, k: (j // per, i, j % per))
        out_shape = jax.ShapeDtypeStruct((out_shards, M, ns), out_dtype)
    else:
        out_spec, out_shape = mn, jax.ShapeDtypeStruct((M, N), out_dtype)
    return _matmul("nn", a, b, grid=(M // tm, N // tn, K // tk), a_spec=pl.BlockSpec((tm, tk), lambda i, j, k: (i, k)),
                   b_spec=b_spec, out_spec=out_spec, out_shape=out_shape, acc_shape=(tm, tn),
                   name=name, a_pro=a_pro, epilogue=epilogue, extras=extras, extra_specs=[mn] * len(extras), stages=stages)


def mm_nt(a, w, *, out_dtype, name, epilogue=None, extras=(), stages=(), tiles=(None, None, None)):
    M, N = a.shape
    sharded = w.ndim == 3
    kw = w.shape[1] if sharded else w.shape[0]
    ns = w.shape[2] if sharded else w.shape[1]
    tm = tiles[0] or _pick(M, (1408, 512, 256, 128))
    tkw = tiles[1] or _pick(kw, (1024, 512, 256, 128))
    tk = tiles[2] or _pick(ns, (2048, 1280, 1024, 512, 256, 128))
    per = ns // tk
    if sharded:
        w_spec = pl.BlockSpec((None, tkw, tk), lambda i, j, k: (k // per, j, k % per))
    else:
        w_spec = pl.BlockSpec((tkw, tk), lambda i, j, k: (j, k))
    mo = pl.BlockSpec((tm, tkw), lambda i, j, k: (i, j))
    return _matmul("nt", a, w, grid=(M // tm, kw // tkw, N // tk), a_spec=pl.BlockSpec((tm, tk), lambda i, j, k: (i, k)),
                   b_spec=w_spec, out_spec=mo, out_shape=jax.ShapeDtypeStruct((M, kw), out_dtype), acc_shape=(tm, tkw),
                   name=name, epilogue=epilogue, extras=extras, extra_specs=[mo] * len(extras), stages=stages)


def mm_tn(a, g, *, shards, name, out_dtype, a_pro=None, stages=(), tiles=(None, None, None)):
    T, kw = a.shape
    N = g.shape[1]
    ns = N // shards
    tkw = tiles[0] or _pick(kw, (1024, 512, 256, 128))
    tn = tiles[1] or _pick(ns, (1280, 1024, 512, 256, 128))
    tt = tiles[2] or _pick(T, (1408, 512, 256, 128))
    per = ns // tn
    if shards > 1:
        out_spec = pl.BlockSpec((None, tkw, tn), lambda i, j, k: (j // per, i, j % per))
        out_shape = jax.ShapeDtypeStruct((shards, kw, ns), out_dtype)
    else:
        out_spec = pl.BlockSpec((tkw, tn), lambda i, j, k: (i, j))
        out_shape = jax.ShapeDtypeStruct((kw, N), out_dtype)
    return _matmul("tn", a, g, grid=(kw // tkw, N // tn, T // tt), a_spec=pl.BlockSpec((tt, tkw), lambda i, j, k: (k, i)),
                   b_spec=pl.BlockSpec((tt, tn), lambda i, j, k: (k, j)), out_spec=out_spec, out_shape=out_shape,
                   acc_shape=(tkw, tn), name=name, a_pro=a_pro, stages=stages)


def _relu_sq(a):
    r = jnp.maximum(a, 0.0)
    return r * r


def rmsnorm_fwd(h, g, *, name):
    Tp, D = h.shape
    tr = _pick(Tp, (384, 256, 128))

    def body(h_ref, g_ref, u_ref, ut_ref):
        x = h_ref[...]
        r = lax.rsqrt(jnp.mean(x * x, axis=-1, keepdims=True) + NORM_EPS)
        u = (x * r) * g_ref[...]
        u_ref[...] = u.astype(BF16)
        ut_ref[...] = u.T.astype(BF16)

    row = pl.BlockSpec((tr, D), lambda i: (i, 0))
    return pl.pallas_call(body, grid=(Tp // tr,), in_specs=[row, pl.BlockSpec((1, D), lambda i: (0, 0))],
                          out_specs=[row, pl.BlockSpec((D, tr), lambda i: (0, i))],
                          out_shape=[jax.ShapeDtypeStruct((Tp, D), BF16), jax.ShapeDtypeStruct((D, Tp), BF16)], name=name,
                          compiler_params=_params("parallel"))(h, g)


def rmsnorm_fwd_input(x, meta_tile, g, *, name):
    S, D = x.shape
    nt = S // ROW_TILE + 1

    def body(x_ref, mt_ref, g_ref, h_ref, u_ref, ut_ref):
        h = jnp.where(pl.program_id(0) == nt - 1, mt_ref[...], x_ref[...])
        r = lax.rsqrt(jnp.mean(h * h, axis=-1, keepdims=True) + NORM_EPS)
        u = (h * r) * g_ref[...]
        h_ref[...] = h
        u_ref[...] = u.astype(BF16)
        ut_ref[...] = u.T.astype(BF16)

    row = pl.BlockSpec((ROW_TILE, D), lambda i: (i, 0))
    return pl.pallas_call(
        body, grid=(nt,),
        in_specs=[pl.BlockSpec((ROW_TILE, D), lambda i: (jnp.minimum(i, nt - 2), 0)), pl.BlockSpec((ROW_TILE, D), lambda i: (0, 0)),
                  pl.BlockSpec((1, D), lambda i: (0, 0))],
        out_specs=[row, row, pl.BlockSpec((D, ROW_TILE), lambda i: (0, i))],
        out_shape=[jax.ShapeDtypeStruct((S + ROW_TILE, D), F32), jax.ShapeDtypeStruct((S + ROW_TILE, D), BF16),
                   jax.ShapeDtypeStruct((D, S + ROW_TILE), BF16)],
        name=name, compiler_params=_params("parallel"))(x, meta_tile, g)


def _rms_bwd_math(x, g, dy):
    r = lax.rsqrt(jnp.mean(x * x, axis=-1, keepdims=True) + NORM_EPS)
    xh = x * r
    dyg = dy * g
    dx = r * (dyg - xh * jnp.mean(dyg * xh, axis=-1, keepdims=True))
    return dx, xh


def final_loss(h2, target, gf, *, name):
    Tp, D = h2.shape
    nt = Tp // ROW_TILE

    def body(h_ref, t_ref, g_ref, dh_ref, dhb_ref, st_ref):
        i = pl.program_id(0)

        @pl.when(i == 0)
        def _():
            st_ref[...] = jnp.zeros_like(st_ref)

        x = h_ref[...]
        g = g_ref[...]
        r = lax.rsqrt(jnp.mean(x * x, axis=-1, keepdims=True) + NORM_EPS)
        xh = x * r
        err = jnp.where(i == nt - 1, 0.0, xh * g - t_ref[...])
        dout = err * (1.0 / D)
        dyg = dout * g
        dx = r * (dyg - xh * jnp.mean(dyg * xh, axis=-1, keepdims=True))
        dh_ref[...] = dx
        dhb_ref[...] = dx.astype(BF16)
        st_ref[0:1, :] += jnp.sum(dout * xh, axis=0, keepdims=True)
        st_ref[1:2, :] += jnp.sum(err * err, axis=0, keepdims=True) * (0.5 / D)

    row = pl.BlockSpec((ROW_TILE, D), lambda i: (i, 0))
    return pl.pallas_call(
        body, grid=(nt,),
        in_specs=[row, pl.BlockSpec((ROW_TILE, D), lambda i: (jnp.minimum(i, nt - 2), 0)), pl.BlockSpec((1, D), lambda i: (0, 0))],
        out_specs=[row, row, pl.BlockSpec((8, D), lambda i: (0, 0))],
        out_shape=[jax.ShapeDtypeStruct((Tp, D), F32), jax.ShapeDtypeStruct((Tp, D), BF16), jax.ShapeDtypeStruct((8, D), F32)],
        name=name, compiler_params=_params("arbitrary"))(h2, target, gf)


def rms_bwd(h, g, du, dres, *, name):
    Tp, D = h.shape
    tr = _pick(Tp, (384, 256, 128))

    def body(h_ref, g_ref, du_ref, dr_ref, dh_ref, dhb_ref, st_ref):
        @pl.when(pl.program_id(0) == 0)
        def _():
            st_ref[...] = jnp.zeros_like(st_ref)

        du_v = du_ref[...].astype(F32)
        dx, xh = _rms_bwd_math(h_ref[...], g_ref[...], du_v)
        dh = dr_ref[...] + dx
        dh_ref[...] = dh
        dhb_ref[...] = dh.astype(BF16)
        st_ref[0:1, :] += jnp.sum(du_v * xh, axis=0, keepdims=True)

    row = pl.BlockSpec((tr, D), lambda i: (i, 0))
    return pl.pallas_call(
        body, grid=(Tp // tr,), in_specs=[row, pl.BlockSpec((1, D), lambda i: (0, 0)), row, row],
        out_specs=[row, row, pl.BlockSpec((8, D), lambda i: (0, 0))],
        out_shape=[jax.ShapeDtypeStruct((Tp, D), F32), jax.ShapeDtypeStruct((Tp, D), BF16), jax.ShapeDtypeStruct((8, D), F32)],
        name=name, compiler_params=_params("arbitrary"))(h, g, du, dres)


def rms_bwd_input(h, g, du, dres, *, name, stages=()):
    Tp, D = h.shape
    nt = Tp // ROW_TILE

    def body(h_ref, g_ref, du_ref, dr_ref, gx_ref, gm_ref, st_ref):
        i = pl.program_id(0)

        @pl.when(i == 0)
        def _():
            st_ref[...] = jnp.zeros_like(st_ref)

        du_v = du_ref[...].astype(F32)
        dx, xh = _rms_bwd_math(h_ref[...], g_ref[...], du_v)
        dh = dr_ref[...] + dx
        st_ref[0:1, :] += jnp.sum(du_v * xh, axis=0, keepdims=True)

        @pl.when(i < nt - 1)
        def _():
            gx_ref[...] = dh

        @pl.when(i == nt - 1)
        def _():
            gm_ref[...] = dh[ROW_TILE - N_META:, :]

    row = pl.BlockSpec((ROW_TILE, D), lambda i: (i, 0))
    outs, staged = _staged_call(
        body, grid=(nt,), in_specs=[row, pl.BlockSpec((1, D), lambda i: (0, 0)), row, row],
        out_specs=[pl.BlockSpec((ROW_TILE, D), lambda i: (jnp.minimum(i, nt - 2), 0)), pl.BlockSpec((N_META, D), lambda i: (0, 0)),
                   pl.BlockSpec((8, D), lambda i: (0, 0))],
        out_shape=[jax.ShapeDtypeStruct((Tp - ROW_TILE, D), F32), jax.ShapeDtypeStruct((N_META, D), F32),
                   jax.ShapeDtypeStruct((8, D), F32)],
        scratch_shapes=[], name=name, semantics=("arbitrary",), inputs=[h, g, du, dres], stages=stages)
    return (outs, staged) if stages else outs


def _conv_shifts(ext):
    return tuple(pltpu.roll(ext, k, 0)[HIST:, :] for k in (1, 2, 3))


def _conv_taps(ext, cur, vec_ref, cs, rows):
    del rows
    x1, x2, x3 = _conv_shifts(ext)
    xc = (vec_ref[1:2, cs] + vec_ref[8:9, cs] * cur + vec_ref[7:8, cs] * x1 + vec_ref[6:7, cs] * x2 + vec_ref[5:6, cs] * x3)
    return xc, x1, x2, x3


def _window_sum_back(ext, w):
    s, sh = ext, 1
    while sh < w:
        s = s + pltpu.roll(s, sh, 0)
        sh *= 2
    return s[HIST:, :]


def _scan_rows(a, b, carry, *, reverse):
    rows = a.shape[0]
    rin = jnp.bitwise_and(lax.broadcasted_iota(jnp.int32, (rows, 1), 0), 7)
    sh = 1
    while sh < 8:
        keep = (rin < 8 - sh) if reverse else (rin >= sh)
        amount = rows - sh if reverse else sh
        a_sh = jnp.where(keep, pltpu.roll(a, amount, 0), 1.0)
        b_sh = jnp.where(keep, pltpu.roll(b, amount, 0), 0.0)
        b = b + a * b_sh
        a = a * a_sh
        sh *= 2
    out = [None] * (rows // 8)
    for g in (reversed(range(rows // 8)) if reverse else range(rows // 8)):
        hg = b[8 * g:8 * g + 8, :] + a[8 * g:8 * g + 8, :] * carry
        carry = hg[0:1, :] if reverse else hg[7:8, :]
        out[g] = hg
    return jnp.concatenate(out, axis=0)


def _lru_gates(xc, wa, wx, vec_ref, cs, sp):
    xcb = xc.astype(BF16)
    r = _sigmoid(jnp.dot(xcb, wa, preferred_element_type=F32) + vec_ref[2:3, cs])
    ig = _sigmoid(jnp.dot(xcb, wx, preferred_element_type=F32) + vec_ref[3:4, cs])
    log_a = (-LRU_C) * r * sp
    a = jnp.exp(log_a)
    a2 = a * a
    return xcb, r, ig, a, a2, _neg_expm1(2.0 * log_a, a2)


def mix_fwd(proj, pool_w, gate_a, gate_x, vecs, *, name, stages=()):
    Tp = proj.shape[0]
    D = proj.shape[1] // 5
    R = ROW_TILE
    nt = Tp // R
    H = D // HEAD_DIM
    PG = D // len(POOL_WINDOWS)

    def body(p_ref, pw_ref, wa_ref, wx_ref, vec_ref, m_ref, hs_ref, xc_ref, r_ref, ig_ref, a_ref, mu_ref, hist_p, hist_l, hcar, mtmp):
        i = pl.program_id(0)
        is_meta = i == 0

        @pl.when(is_meta)
        def _():
            hist_p[...] = jnp.zeros_like(hist_p)
            hist_l[...] = jnp.zeros_like(hist_l)
            hcar[...] = jnp.zeros_like(hcar)

        row, valid, t_log = _tile_masks(is_meta, R)

        for g, w in enumerate(POOL_WINDOWS):
            cs = slice(g * PG, (g + 1) * PG)
            v = p_ref[:, g * PG:(g + 1) * PG].astype(F32)
            ws = _window_sum_back(jnp.concatenate([hist_p[:, cs], v], axis=0), w)
            d = ws * _window_count_inv(t_log, w) - v
            y = jnp.dot(d.astype(BF16), pw_ref[g], preferred_element_type=F32)
            gp = p_ref[:, 3 * D + g * PG:3 * D + (g + 1) * PG].astype(F32)
            mtmp[:, cs] = _sigmoid(gp) * (y * vec_ref[0:1, cs])
            hist_p[:, cs] = v[R - HIST:, :]

        for h in range(H):
            cs = slice(h * HEAD_DIM, (h + 1) * HEAD_DIM)
            vl = p_ref[:, D + h * HEAD_DIM:D + (h + 1) * HEAD_DIM].astype(F32)
            xc, _, _, _ = _conv_taps(jnp.concatenate([hist_l[:, cs], vl], axis=0), vl, vec_ref, cs, R)
            sp = _softplus_neg(vec_ref[4:5, cs])
            xcb, r, ig, a, _, em = _lru_gates(xc, wa_ref[h], wx_ref[h], vec_ref, cs, sp)
            mult = jnp.sqrt(em)
            b = jnp.where(valid, mult * (ig * xc), 0.0)
            hs = _scan_rows(a, b, hcar[7:8, cs], reverse=False)
            hs_ref[:, cs] = hs
            xc_ref[:, cs], r_ref[:, cs], ig_ref[:, cs] = xcb, r.astype(BF16), ig.astype(BF16)
            a_ref[:, cs], mu_ref[:, cs] = a, mult
            hcar[:, cs] = hs[R - 8:, :]
            hist_l[:, cs] = vl[R - HIST:, :]
            ge, _ = _gelu_tanh(p_ref[:, 2 * D + h * HEAD_DIM:2 * D + (h + 1) * HEAD_DIM].astype(F32))
            gl = p_ref[:, 4 * D + h * HEAD_DIM:4 * D + (h + 1) * HEAD_DIM].astype(F32)
            m_ref[:, cs] = (mtmp[:, cs] + _sigmoid(gl) * (hs * ge)).astype(BF16)

    def tile(i):
        return (i + nt - 1) % nt

    full = lambda shape: pl.BlockSpec(shape, lambda i: (0,) * len(shape))
    outs, staged = _staged_call(
        body, grid=(nt,),
        in_specs=[pl.BlockSpec((R, 5 * D), lambda i: (tile(i), 0)), full(pool_w.shape), full(gate_a.shape), full(gate_x.shape),
                  full(vecs.shape)],
        out_specs=[pl.BlockSpec((R, D), lambda i: (tile(i), 0))] * 7,
        out_shape=[jax.ShapeDtypeStruct((Tp, D), dt) for dt in (BF16, F32, BF16, BF16, BF16, F32, F32)],
        scratch_shapes=[pltpu.VMEM((HIST, D), F32), pltpu.VMEM((HIST, D), F32), pltpu.VMEM((8, D), F32), pltpu.VMEM((R, D), F32)],
        name=name, semantics=("arbitrary",), inputs=[proj, pool_w, gate_a, gate_x, vecs], stages=stages)
    outs = [outs[0], outs[1], outs[2:]]
    return (outs, staged) if stages else outs


def mix_bwd(proj, hs, saved, dmerged, pool_w, gate_a, gate_x, vecs, *, name, stages=()):
    Tp = proj.shape[0]
    D = proj.shape[1] // 5
    R = ROW_TILE
    nt = Tp // R
    H = D // HEAD_DIM
    PG = D // len(POOL_WINDOWS)

    def body(p_ref, pprev_ref, hs_ref, hprev_ref, dm_ref, xc_ref, r_ref, ig_ref, a_ref, mu_ref, pw_ref, wa_ref, wx_ref, vec_ref,
             dp_ref, dpw_ref, dwa_ref, dwx_ref, dvec_ref, car_g, fut_dxc, fut_q):
        i = pl.program_id(0)
        is_meta = i == nt - 1

        @pl.when(i == 0)
        def _():
            dpw_ref[...] = jnp.zeros_like(dpw_ref)
            dwa_ref[...] = jnp.zeros_like(dwa_ref)
            dwx_ref[...] = jnp.zeros_like(dwx_ref)
            dvec_ref[...] = jnp.zeros_like(dvec_ref)
            car_g[...] = jnp.zeros_like(car_g)
            fut_dxc[...] = jnp.zeros_like(fut_dxc)
            fut_q[...] = jnp.zeros_like(fut_q)

        row, valid, t_log = _tile_masks(is_meta, R)
        keep_prev = jnp.logical_not(is_meta)

        def colsum(x):
            return jnp.sum(x, axis=0, keepdims=True)

        for g, w in enumerate(POOL_WINDOWS):
            cs = slice(g * PG, (g + 1) * PG)
            v = p_ref[:, g * PG:(g + 1) * PG].astype(F32)
            vprev = jnp.where(keep_prev, pprev_ref[:, g * PG:(g + 1) * PG].astype(F32), 0.0)
            inv_cnt = _window_count_inv(t_log, w)
            d = _window_sum_back(jnp.concatenate([vprev, v], axis=0), w) * inv_cnt - v
            d_bf = d.astype(BF16)
            y = jnp.dot(d_bf, pw_ref[g], preferred_element_type=F32)
            scale = vec_ref[0:1, cs]
            sg = _sigmoid(p_ref[:, 3 * D + g * PG:3 * D + (g + 1) * PG].astype(F32))
            dm = dm_ref[:, cs].astype(F32)
            dpo = dm * sg
            dp_ref[:, 3 * D + g * PG:3 * D + (g + 1) * PG] = (dm * (y * scale) * sg * (1.0 - sg)).astype(BF16)
            dvec_ref[0:1, cs] += colsum(dpo * y)
            dy = (dpo * scale).astype(BF16)
            dd = _dot_nt(dy, pw_ref[g])
            dpw_ref[g] += _dot_tn(d_bf, dy)
            q = dd * inv_cnt
            s, sh = jnp.concatenate([q, fut_q[:, cs]], axis=0), 1
            while sh < w:
                s = s + pltpu.roll(s, R + HIST - sh, 0)
                sh *= 2
            dp_ref[:, g * PG:(g + 1) * PG] = (s[:R, :] - dd).astype(BF16)
            fut_q[:, cs] = q[:HIST, :]

        for h in range(H):
            cs = slice(h * HEAD_DIM, (h + 1) * HEAD_DIM)
            pc = lambda blk: slice(blk * D + h * HEAD_DIM, blk * D + (h + 1) * HEAD_DIM)
            vl = p_ref[:, pc(1)].astype(F32)
            vlprev = jnp.where(keep_prev, pprev_ref[:, pc(1)].astype(F32), 0.0)
            x1, x2, x3 = _conv_shifts(jnp.concatenate([vlprev, vl], axis=0))
            lam = vec_ref[4:5, cs]
            sp = _softplus_neg(lam)
            xcb = xc_ref[:, cs]
            xc, r, ig = xcb.astype(F32), r_ref[:, cs].astype(F32), ig_ref[:, cs].astype(F32)
            a, mult = a_ref[:, cs], mu_ref[:, cs]
            a2 = a * a
            inv_mult = 1.0 / mult
            hsv = hs_ref[:, cs]
            hprev = jnp.where(row >= 1, pltpu.roll(hsv, 1, 0), hprev_ref[HIST - 1:HIST, cs])
            vg = p_ref[:, pc(2)].astype(F32)
            ge, th = _gelu_tanh(vg)
            sgl = _sigmoid(p_ref[:, pc(4)].astype(F32))
            dm = dm_ref[:, cs].astype(F32)
            dlo = dm * sgl
            dp_ref[:, pc(4)] = (dm * (hsv * ge) * sgl * (1.0 - sgl)).astype(BF16)
            dp_ref[:, pc(2)] = (dlo * hsv * _gelu_tanh_grad(vg, th)).astype(BF16)
            a_next = jnp.where(row < R - 1, pltpu.roll(a, R - 1, 0), 1.0)
            G = _scan_rows(a_next, dlo * ge, car_g[0:1, cs], reverse=True)
            car_g[:, cs] = (a * G)[0:8, :]
            da = jnp.where(valid, G * hprev, 0.0)
            db = jnp.where(valid, G, 0.0)
            dmult = db * (ig * xc)
            dig = db * (mult * xc)
            dxc = db * (mult * ig)
            dlog_a = da * a - dmult * (a2 * inv_mult)
            dvec_ref[4:5, cs] += colsum(dlog_a * r) * (-LRU_C)
            dr = dlog_a * ((-LRU_C) * sp)
            dpa = dr * r * (1.0 - r)
            dpx = dig * ig * (1.0 - ig)
            dpa_bf = dpa.astype(BF16)
            dpx_bf = dpx.astype(BF16)
            dwa_ref[h] += _dot_tn(xcb, dpa_bf)
            dwx_ref[h] += _dot_tn(xcb, dpx_bf)
            dvec_ref[2:3, cs] += colsum(dpa)
            dvec_ref[3:4, cs] += colsum(dpx)
            dxc = dxc + _dot_nt(dpa_bf, wa_ref[h]) + _dot_nt(dpx_bf, wx_ref[h])
            ext = jnp.concatenate([dxc, fut_dxc[:, cs]], axis=0)
            n = R + HIST
            dvl = (vec_ref[8:9, cs] * dxc + vec_ref[7:8, cs] * pltpu.roll(ext, n - 1, 0)[:R, :]
                   + vec_ref[6:7, cs] * pltpu.roll(ext, n - 2, 0)[:R, :] + vec_ref[5:6, cs] * pltpu.roll(ext, n - 3, 0)[:R, :])
            dp_ref[:, pc(1)] = dvl.astype(BF16)
            dvec_ref[1:2, cs] += colsum(dxc)
            dvec_ref[8:9, cs] += colsum(dxc * vl)
            dvec_ref[7:8, cs] += colsum(dxc * x1)
            dvec_ref[6:7, cs] += colsum(dxc * x2)
            dvec_ref[5:6, cs] += colsum(dxc * x3)
            fut_dxc[:, cs] = dxc[:HIST, :]

            @pl.when(is_meta)
            def _():
                dvec_ref[4:5, cs] = dvec_ref[4:5, cs] * (-_sigmoid(-lam))

    def tile(i):
        return (2 * nt - 2 - i) % nt

    def prev_blk(i):
        per = R // HIST
        return jnp.where(i == nt - 1, 0, jnp.where(i == nt - 2, Tp // HIST - 1, (nt - 2 - i) * per - 1))

    full = lambda shape: pl.BlockSpec(shape, lambda i: (0,) * len(shape))
    G_ = len(POOL_WINDOWS)
    outs, staged = _staged_call(
        body, grid=(nt,),
        in_specs=[pl.BlockSpec((R, 5 * D), lambda i: (tile(i), 0)), pl.BlockSpec((HIST, 5 * D), lambda i: (prev_blk(i), 0)),
                  pl.BlockSpec((R, D), lambda i: (tile(i), 0)), pl.BlockSpec((HIST, D), lambda i: (prev_blk(i), 0)),
                  *[pl.BlockSpec((R, D), lambda i: (tile(i), 0))] * 6,
                  full(pool_w.shape), full(gate_a.shape), full(gate_x.shape), full(vecs.shape)],
        out_specs=[pl.BlockSpec((R, 5 * D), lambda i: (tile(i), 0)), full((G_, PG, PG)), full((H, HEAD_DIM, HEAD_DIM)),
                   full((H, HEAD_DIM, HEAD_DIM)), full((16, D))],
        out_shape=[jax.ShapeDtypeStruct((Tp, 5 * D), BF16), jax.ShapeDtypeStruct((G_, PG, PG), F32),
                   jax.ShapeDtypeStruct((H, HEAD_DIM, HEAD_DIM), F32), jax.ShapeDtypeStruct((H, HEAD_DIM, HEAD_DIM), F32),
                   jax.ShapeDtypeStruct((16, D), F32)],
        scratch_shapes=[pltpu.VMEM((8, D), F32), pltpu.VMEM((HIST, D), F32), pltpu.VMEM((HIST, D), F32)],
        name=name, semantics=("arbitrary",), inputs=[proj, proj, hs, hs, dmerged, *saved, pool_w, gate_a, gate_x, vecs], stages=stages)
    return (outs, staged) if stages else outs


def _adamw_math(w, g, m, v):
    mn = ADAM_B1 * m + (1.0 - ADAM_B1) * g
    vn = ADAM_B2 * v + (1.0 - ADAM_B2) * (g * g)
    m_hat = mn / (1.0 - ADAM_B1 ** ADAM_STEP)
    v_hat = vn / (1.0 - ADAM_B2 ** ADAM_STEP)
    return -ADAM_LR * (m_hat / (jnp.sqrt(v_hat) + ADAM_EPS) + ADAM_WD * w), mn, vn


def adamw(w, g, m, v, *, name):
    rows, cols = w.shape
    tr = _pick(rows, (256, 128, 64, 32, 16, 8))

    def body(w_ref, g_ref, m_ref, v_ref, go_ref, d_ref, mo_ref, vo_ref):
        gv = g_ref[...]
        go_ref[...] = gv
        d_ref[...], mo_ref[...], vo_ref[...] = _adamw_math(w_ref[...], gv, m_ref[...], v_ref[...])

    blk = pl.BlockSpec((tr, cols), lambda i: (i, 0))
    sds = jax.ShapeDtypeStruct((rows, cols), F32)
    return pl.pallas_call(body, grid=(rows // tr,), in_specs=[blk] * 4, out_specs=[blk] * 4, out_shape=[sds] * 4, name=name,
                          compiler_params=_params("parallel"))(w, g, m, v)


def allgather8(block, *, name, reduce_sum=False, stages=()):
    rows, cols = block.shape

    def body(x_ref, out_ref, *scratch):
        if reduce_sum:
            buf, send_sems, recv_sems, local_sem = scratch
        else:
            buf = out_ref
            send_sems, recv_sems, local_sem = scratch
        x, y, c, chips = _place()
        me, sibling = (x, y, c), (x, y, 1 - c)

        def slot(px, py, pc):
            return buf.at[4 * px + 2 * py + pc]

        def copy(k, blk, to, src=None):
            return pltpu.make_async_remote_copy(src_ref=slot(*blk) if src is None else src, dst_ref=slot(*blk),
                                                send_sem=send_sems.at[k], recv_sem=recv_sems.at[k], device_id=to, device_id_type=MESH)

        mine = pltpu.make_async_copy(x_ref, slot(*me), local_sem)
        mine.start()
        first = [copy(0, me, sibling, src=x_ref)]
        first += [copy(1 + j, me, (*chip, c), src=x_ref) for j, chip in enumerate(chips)]
        for cp in first:
            cp.start()
        passed = [copy(4 + j, (*chip, c), sibling) for j, chip in enumerate(chips)]
        for j, chip in enumerate(chips):
            copy(1 + j, (*chip, c), me).wait_recv()
            passed[j].start()
        copy(0, sibling, me).wait_recv()
        for j, chip in enumerate(chips):
            copy(4 + j, (*chip, 1 - c), me).wait_recv()
        for cp in first + passed:
            cp.wait_send()
        mine.wait()
        if reduce_sum:
            acc = buf[0]
            for d in range(1, 8):
                acc = acc + buf[d]
            out_ref[...] = acc

    sems = [pltpu.SemaphoreType.DMA((7,)), pltpu.SemaphoreType.DMA((7,)), pltpu.SemaphoreType.DMA]
    if reduce_sum:
        out_shape = jax.ShapeDtypeStruct((rows, cols), block.dtype)
        scratch = [pltpu.VMEM((8, rows, cols), block.dtype)] + sems
    else:
        out_shape = jax.ShapeDtypeStruct((8, rows, cols), block.dtype)
        scratch = sems
    vmem = pl.BlockSpec(memory_space=pltpu.VMEM)
    outs, staged = _staged_call(body, grid=(1,), in_specs=[vmem], out_specs=[vmem], out_shape=[out_shape], scratch_shapes=scratch,
                                name=name, semantics=("arbitrary",), inputs=[block], stages=stages)
    return (outs[0], staged) if stages else outs[0]


def cast_into_slot(w, chip_arr, *, name):
    _, r, cols = w.shape
    tr = _pick(r, (256, 128, 64, 32, 16))

    def body(chip_ref, w_ref, o_ref):
        del chip_ref
        o_ref[...] = w_ref[...].astype(BF16)

    grid_spec = pltpu.PrefetchScalarGridSpec(
        num_scalar_prefetch=1, grid=(2, r // tr),
        in_specs=[pl.BlockSpec((None, tr, cols), lambda h, i, chip: (h, i, 0))],
        out_specs=pl.BlockSpec((None, None, tr, cols), lambda h, i, chip: (chip[0], h, i, 0)))
    return pl.pallas_call(body, grid_spec=grid_spec, out_shape=jax.ShapeDtypeStruct((4, 2, r, cols), BF16), name=name,
                          compiler_params=_params("parallel", "parallel"))(chip_arr, w)


def proj_with_gather(u, bufs, order_arr, *, name):
    Tp, K = u.shape
    n = len(bufs)
    Ns = bufs[0].shape[3]
    tm = _pick(Tp, (1408, 512, 256, 128))
    tc = _pick(Ns, (512, 256, 128))
    n_rows = Tp // tm

    def body(order_ref, u_ref, *refs):
        del order_ref
        o_ref, outs = refs[n], refs[n + 1:2 * n + 1]
        wbuf, ici_send, ici_recv, d2d_send, d2d_recv, load_sems = refs[2 * n + 1:]
        s, i = pl.program_id(0), pl.program_id(1)
        x, y, c, chips = _place()
        me = 2 * x + y

        def ici(t, j, landing):
            px, py = chips[j]
            slot = outs[t].at[2 * px + py, c] if landing else outs[t].at[me, c]
            return _remote(slot, slot, ici_send.at[t, j], ici_recv.at[t, j], (px, py, c))

        def d2d(t, j, landing):
            px, py = chips[j]
            slot = outs[t].at[2 * px + py, 1 - c] if landing else outs[t].at[2 * px + py, c]
            return _remote(slot, slot, d2d_send.at[t, j], d2d_recv.at[t, j], (x, y, 1 - c))

        def load(chip_idx):
            parts = [pltpu.make_async_copy(outs[0].at[chip_idx, hh], wbuf.at[pl.ds(hh * (K // 2), K // 2), :], load_sems.at[hh])
                     for hh in range(2)]
            for cp in parts:
                cp.start()
            for cp in parts:
                cp.wait()

        @pl.when(jnp.logical_and(s == 0, i == 0))
        def _():
            for t in range(n):
                for j in range(3):
                    ici(t, j, False).start()
            load(me)

        for j in range(3):
            @pl.when(jnp.logical_and(s == j + 1, i == 0))
            def _(j=j):
                ici(0, j, True).wait_recv()
                d2d(0, j, False).start()
                d2d(0, j, True).wait_recv()
                px, py = chips[j]
                load(2 * px + py)

        uv = u_ref[...]
        for cc in range(Ns // tc):
            o_ref[:, cc * tc:(cc + 1) * tc] = jnp.dot(uv, wbuf[:, cc * tc:(cc + 1) * tc], preferred_element_type=F32).astype(BF16)

        @pl.when(jnp.logical_and(s == 3, i == n_rows - 1))
        def _():
            for t in range(1, n):
                for j in range(3):
                    ici(t, j, True).wait_recv()
                    d2d(t, j, False).start()
            for t in range(1, n):
                for j in range(3):
                    d2d(t, j, True).wait_recv()
            for t in range(n):
                for j in range(3):
                    ici(t, j, False).wait_send()
                    d2d(t, j, False).wait_send()

    grid_spec = pltpu.PrefetchScalarGridSpec(
        num_scalar_prefetch=1, grid=(4, n_rows),
        in_specs=[pl.BlockSpec((tm, K), lambda s, i, order: (i, 0)), *_any_specs(n)],
        out_specs=[pl.BlockSpec((tm, Ns), lambda s, i, order: (i, order[s])), *_any_specs(n)],
        scratch_shapes=[pltpu.VMEM((K, Ns), BF16), pltpu.SemaphoreType.DMA((n, 3)), pltpu.SemaphoreType.DMA((n, 3)),
                        pltpu.SemaphoreType.DMA((n, 3)), pltpu.SemaphoreType.DMA((n, 3)), pltpu.SemaphoreType.DMA((2,))])
    res = pl.pallas_call(body, grid_spec=grid_spec, out_shape=[jax.ShapeDtypeStruct((Tp, 4 * Ns), BF16), *[_sds(b) for b in bufs]],
                         input_output_aliases={2 + t: 1 + t for t in range(n)}, name=name,
                         compiler_params=_params("arbitrary", "arbitrary"))(order_arr, u, *bufs)
    return res[0], list(res[1:])


def chip_presum(grad, recv, c_arr, *, name):
    _, _, r, cols = grad.shape
    tr = _pick(r, (256, 128, 64, 32, 16))

    def body(c_ref, g_ref, r_ref, o_ref):
        del c_ref
        o_ref[...] = (g_ref[...].astype(F32) + r_ref[...].astype(F32)).astype(BF16)

    grid_spec = pltpu.PrefetchScalarGridSpec(
        num_scalar_prefetch=1, grid=(4, r // tr),
        in_specs=[pl.BlockSpec((None, None, tr, cols), lambda k, i, c_ref: (k, c_ref[0], i, 0)),
                  pl.BlockSpec((None, tr, cols), lambda k, i, c_ref: (k, i, 0))],
        out_specs=pl.BlockSpec((None, tr, cols), lambda k, i, c_ref: (k, i, 0)))
    return pl.pallas_call(body, grid_spec=grid_spec, out_shape=jax.ShapeDtypeStruct((4, r, cols), BF16), name=name,
                          compiler_params=_params("parallel", "parallel"))(c_arr, grad, recv)


def final_half(grad, recv, got, mc_arr, *, name):
    _, _, r, cols = grad.shape
    tr = _pick(r, (256, 128, 64, 32, 16))

    def body(mc_ref, g_ref, r_ref, q_ref, o_ref):
        del mc_ref
        acc = g_ref[...].astype(F32) + r_ref[...].astype(F32)
        for j in range(3):
            acc = acc + q_ref[j].astype(F32)
        o_ref[...] = acc

    grid_spec = pltpu.PrefetchScalarGridSpec(
        num_scalar_prefetch=1, grid=(r // tr,),
        in_specs=[pl.BlockSpec((None, None, tr, cols), lambda i, mc: (mc[0], mc[1], i, 0)),
                  pl.BlockSpec((None, tr, cols), lambda i, mc: (mc[0], i, 0)),
                  pl.BlockSpec((3, tr, cols), lambda i, mc: (0, i, 0))],
        out_specs=pl.BlockSpec((tr, cols), lambda i, mc: (i, 0)))
    return pl.pallas_call(body, grid_spec=grid_spec, out_shape=jax.ShapeDtypeStruct((r, cols), F32), name=name,
                          compiler_params=_params("parallel"))(mc_arr, grad, recv, got)


def adamw_halves(w, mine, theirs, m, v, c_arr, *, name):
    _, r, cols = w.shape
    tr = _pick(r, (256, 128, 64, 32, 16, 8))

    def body(c_ref, w_ref, mine_ref, theirs_ref, m_ref, v_ref, go_ref, d_ref, mo_ref, vo_ref):
        gv = jnp.where(pl.program_id(0) == c_ref[0], mine_ref[...], theirs_ref[...])
        go_ref[...] = gv
        d_ref[...], mo_ref[...], vo_ref[...] = _adamw_math(w_ref[...], gv, m_ref[...], v_ref[...])

    blk = pl.BlockSpec((None, tr, cols), lambda h, i, c_ref: (h, i, 0))
    grid_spec = pltpu.PrefetchScalarGridSpec(
        num_scalar_prefetch=1, grid=(2, r // tr),
        in_specs=[blk, pl.BlockSpec((tr, cols), lambda h, i, c_ref: (jnp.where(h == c_ref[0], i, 0), 0)),
                  pl.BlockSpec((tr, cols), lambda h, i, c_ref: (jnp.where(h == c_ref[0], 0, i), 0)), blk, blk],
        out_specs=[blk] * 4)
    sds = jax.ShapeDtypeStruct((2, r, cols), F32)
    return pl.pallas_call(body, grid_spec=grid_spec, out_shape=[sds] * 4, name=name,
                          compiler_params=_params("parallel", "parallel"))(c_arr, w, mine, theirs, m, v)


def _pad_rows(a, rows):
    return jnp.pad(a, ((0, rows - a.shape[0]), (0, 0)))


def kernel(x, meta_tokens, norm1_g, w_in, pool_w, pool_scale, conv_w, conv_b, gate_a_w, gate_a_b, gate_x_w, gate_x_b, lru_lambda, w_out, norm2_g, mlp_w1, mlp_w2, final_g, loss_target, m_meta_tokens, m_norm1_g, m_w_in, m_pool_w, m_pool_scale, m_conv_w, m_conv_b, m_gate_a_w, m_gate_a_b, m_gate_x_w, m_gate_x_b, m_lru_lambda, m_w_out, m_norm2_g, m_mlp_w1, m_mlp_w2, m_final_g, v_meta_tokens, v_norm1_g, v_w_in, v_pool_w, v_pool_scale, v_conv_w, v_conv_b, v_gate_a_w, v_gate_a_b, v_gate_x_w, v_gate_x_b, v_lru_lambda, v_w_out, v_norm2_g, v_mlp_w1, v_mlp_w2, v_final_g):
    D = x.shape[-1]
    H = D // HEAD_DIM
    G = len(POOL_WINDOWS)
    PG = D // G
    ax, ay, ac = lax.axis_index("x"), lax.axis_index("y"), lax.axis_index("c")
    chip = 2 * ax + ay
    dshard = D // 4

    c_arr = jnp.reshape(ac, (1,)).astype(jnp.int32)
    chip_arr = jnp.reshape(chip, (1,)).astype(jnp.int32)
    mc_arr = jnp.stack([chip, ac]).astype(jnp.int32)
    names = ["w_in", "pool_w", "gate_a_w", "gate_x_w", "w_out", "mlp_w1", "mlp_w2"]
    big = [w_in, pool_w, gate_a_w, gate_x_w, w_out, mlp_w1, mlp_w2]

    def halves(w):
        w2d = w.reshape(-1, w.shape[-1])
        return w2d.reshape(2, w2d.shape[0] // 2, w2d.shape[1])

    bufs = [cast_into_slot(halves(w), chip_arr, name="cast_" + nm) for w, nm in zip(big, names)]
    order_arr = jnp.stack([chip, 2 * (1 - ax) + ay, 2 * ax + (1 - ay), 2 * (1 - ax) + (1 - ay)]).astype(jnp.int32)

    small_in = jnp.concatenate([meta_tokens, _pad_rows(conv_w[0], 8), _pad_rows(gate_a_b.reshape(1, dshard), 8),
                                _pad_rows(gate_x_b.reshape(1, dshard), 8)], axis=0)
    sm = allgather8(small_in, name="gather_small")[0::2]
    meta_f = sm[:, 0:16].transpose(1, 0, 2).reshape(N_META, D)
    conv_w_f = sm[:, 16:20].transpose(1, 0, 2).reshape(4, D)
    hd4 = HEAD_DIM // 4
    ba_f = sm[:, 24].reshape(4, H, hd4).transpose(1, 0, 2).reshape(1, D)
    bx_f = sm[:, 32].reshape(4, H, hd4).transpose(1, 0, 2).reshape(1, D)
    vecs = jnp.zeros((16, D), F32)
    for r0, part in ((0, pool_scale), (1, conv_b), (2, ba_f), (3, bx_f), (4, lru_lambda), (5, conv_w_f)):
        vecs = lax.dynamic_update_slice(vecs, part, (r0, 0))

    def chipwise(g, n_blocks, rows):
        return g.reshape(n_blocks, 4, rows, g.shape[-1]).transpose(1, 0, 2, 3).reshape(4, n_blocks * rows, g.shape[-1])

    def split2(g):
        return g.reshape(4, 2, g.shape[1] // 2, g.shape[2])

    def presum(t, g, r):
        return chip_presum(g, r, c_arr, name="presum_" + names[t])

    def total(t, g, r, q):
        return final_half(g, r, q, mc_arr, name="sum_" + names[t])

    xs, target, gfin = x[0], loss_target[0], final_g.reshape(1, D)
    meta_tile = jnp.concatenate([jnp.zeros((ROW_TILE - N_META, D), F32), meta_f], axis=0)
    h0, u, u_t = rmsnorm_fwd_input(xs, meta_tile, norm1_g, name="norm1")
    proj, (b_win, b_pool, b_ga, b_gx, b_wout) = proj_with_gather(u, bufs[0:5], order_arr, name="proj")
    w_in_f = b_win.reshape(4, w_in.shape[1], w_in.shape[2])
    pool_f = b_pool.reshape(4, G, PG // 4, PG).transpose(1, 0, 2, 3).reshape(G, PG, PG)
    ga_f = b_ga.reshape(4, H, HEAD_DIM // 4, HEAD_DIM).transpose(1, 0, 2, 3).reshape(H, HEAD_DIM, HEAD_DIM)
    gx_f = b_gx.reshape(4, H, HEAD_DIM // 4, HEAD_DIM).transpose(1, 0, 2, 3).reshape(H, HEAD_DIM, HEAD_DIM)
    w_out_f = b_wout.reshape(4 * w_out.shape[1], w_out.shape[2])
    (merged, hs, saved), [[b_w1]] = mix_fwd(proj, pool_f, ga_f, gx_f, vecs, name="mix_fwd", stages=[stage_gather_ici([bufs[5]])])
    h1, [[b_w1]] = mm_nn(merged, w_out_f, out_dtype=F32, name="out_proj", epilogue=lambda r, res: r + res, extras=(h0,),
                         stages=[stage_gather_d2d([b_w1])])
    w1_f = b_w1.reshape(4, mlp_w1.shape[1], mlp_w1.shape[2])
    u2, u2_t = rmsnorm_fwd(h1, norm2_g, name="norm2")
    a1, [[b_w2]] = mm_nn(u2, w1_f, out_dtype=BF16, name="mlp_up", stages=[stage_gather_ici([bufs[6]])])
    [[b_w2]] = comm_call([stage_gather_d2d([b_w2])], name="w2_to_sibling")
    w2_f = b_w2.reshape(4 * mlp_w2.shape[1], mlp_w2.shape[2])
    Tp = h0.shape[0]
    h2 = mm_nn(a1, w2_f, out_dtype=F32, name="mlp_down", a_pro=_relu_sq, epilogue=lambda r, res: r + res, extras=(h1,))
    dh2, dh2_bf, st_f = final_loss(h2, target, gfin, name="final_loss")

    da1 = mm_nt(dh2_bf, w2_f, out_dtype=BF16, name="mlp_down_dx",
                epilogue=lambda r, a: r * (2.0 * jnp.maximum(a.astype(F32), 0.0)), extras=(a1,))
    d_w2 = mm_tn(a1, dh2_bf, shards=1, out_dtype=BF16, name="mlp_down_dw", a_pro=_relu_sq, tiles=(512, min(1024, D), Tp))
    g6 = split2(d_w2.reshape(4, -1, d_w2.shape[-1]))
    d_w1, [[r6]] = mm_nn(u2_t, da1, out_dtype=BF16, name="mlp_up_dw", out_shards=4, tiles=(min(1024, D), None, Tp),
                         stages=[stage_to_sibling([g6])])
    g5 = split2(d_w1)
    p6 = presum(6, g6, r6)
    du2, [[q6_near], [r5]] = mm_nt(da1, w1_f, out_dtype=F32, name="mlp_up_dx",
                                  stages=[stage_to_chips([p6], peers=(0, 1)), stage_to_sibling([g5])])
    p5 = presum(5, g5, r5)
    dh1, dh1_bf, st_2 = rms_bwd(h1, norm2_g, du2, dh2, name="norm2_bwd")
    dmerged = mm_nt(dh1_bf, w_out_f, out_dtype=BF16, name="out_proj_dx")
    d_wout = mm_tn(merged, dh1_bf, shards=1, out_dtype=BF16, name="out_proj_dw", tiles=(512, min(1024, D), Tp))
    g4 = split2(d_wout.reshape(4, -1, d_wout.shape[-1]))
    (dproj, d_pool, d_ga, d_gx, d_vecs), [[q6], [q5], [r4]] = mix_bwd(
        proj, hs, saved, dmerged, pool_f, ga_f, gx_f, vecs, name="mix_bwd",
        stages=[stage_to_chips([p6], peers=(2,), into=[q6_near]), stage_to_chips([p5]), stage_to_sibling([g4])])
    p4 = presum(4, g4, r4)
    f5, f6 = total(5, g5, r5, q5), total(6, g6, r6, q6)
    g1, g2, g3 = split2(chipwise(d_pool, G, PG // 4)), split2(chipwise(d_ga, H, hd4)), split2(chipwise(d_gx, H, hd4))
    half_k = u_t.shape[0] // 2
    whole_k = (half_k, None, Tp)
    u_theirs = lax.dynamic_slice_in_dim(u_t, (1 - ac) * half_k, half_k, axis=0)
    u_mine = lax.dynamic_slice_in_dim(u_t, ac * half_k, half_k, axis=0)
    g_theirs, [[r1, r2, r3], [o5, o6], [q4]] = mm_nn(u_theirs, dproj, out_dtype=BF16, out_shards=4, tiles=whole_k, name="proj_dw_sibling_rows",
                                                    stages=[stage_to_sibling([g1, g2, g3]), stage_from_sibling([f5, f6]), stage_to_chips([p4])])
    p1, p2, p3 = presum(1, g1, r1), presum(2, g2, r2), presum(3, g3, r3)
    g_mine, [[r0], [q1, q2, q3]] = mm_nn(u_mine, dproj, out_dtype=BF16, out_shards=4, tiles=whole_k, name="proj_dw_own_rows",
                                         stages=[stage_from_sibling([g_theirs]), stage_to_chips([p1, p2, p3])])
    g0 = g_mine[:, None]
    p0 = chip_presum(g0, r0, jnp.zeros((1,), jnp.int32), name="presum_w_in")
    f4 = total(4, g4, r4, q4)
    du, [[q0], [o4]] = mm_nt(dproj, w_in_f, out_dtype=F32, name="proj_dx", tiles=(None, None, w_in_f.shape[2]),
                             stages=[stage_to_chips([p0]), stage_from_sibling([f4])])
    f0 = final_half(g0, r0, q0, jnp.stack([chip, 0]).astype(jnp.int32), name="sum_w_in")
    f1, f2, f3 = total(1, g1, r1, q1), total(2, g2, r2, q2), total(3, g3, r3, q3)
    grad_x, d_meta, st_1 = rms_bwd_input(h0, norm1_g, du, dh1, name="norm1_bwd")

    small = jnp.concatenate([d_meta, d_vecs, st_1, st_2, st_f], axis=0)
    tot, [[o0, o1, o2, o3]] = allgather8(small, name="sum_small", reduce_sum=True, stages=[stage_from_sibling([f0, f1, f2, f3])])
    mine = [f0, f1, f2, f3, f4, f5, f6]
    theirs = [o0, o1, o2, o3, o4, o5, o6]
    loss = jnp.sum(tot[49])
    g_meta = lax.dynamic_slice_in_dim(tot[0:16], chip * dshard, dshard, axis=1)
    g_pool_scale, g_conv_b, g_lam = tot[16:17], tot[17:18], tot[20:21]
    g_ba = lax.dynamic_slice_in_dim(tot[18].reshape(H, HEAD_DIM), chip * hd4, hd4, axis=1)[None]
    g_bx = lax.dynamic_slice_in_dim(tot[19].reshape(H, HEAD_DIM), chip * hd4, hd4, axis=1)[None]
    g_conv_w = lax.dynamic_slice_in_dim(tot[21:25], chip * dshard, dshard, axis=1)[None]
    g_n1, g_n2, g_fin = tot[32:33], tot[40:41], tot[48]

    def step(w, g, m, v, nm):
        cols = w.shape[-1]
        outs = adamw(w.reshape(-1, cols), g.reshape(-1, cols), m.reshape(-1, cols), v.reshape(-1, cols), name="adamw_" + nm)
        return [o.reshape(w.shape) for o in outs]

    def step_big(t, w, m, v):
        outs = adamw_halves(halves(w), mine[t], theirs[t], halves(m), halves(v), c_arr, name="adamw_" + names[t])
        return [o.reshape(w.shape) for o in outs]

    res = dict(meta_tokens=step(meta_tokens, g_meta, m_meta_tokens, v_meta_tokens, "meta_tokens"),
               norm1_g=step(norm1_g, g_n1, m_norm1_g, v_norm1_g, "norm1_g"),
               w_in=step_big(0, w_in, m_w_in, v_w_in), pool_w=step_big(1, pool_w, m_pool_w, v_pool_w),
               pool_scale=step(pool_scale, g_pool_scale, m_pool_scale, v_pool_scale, "pool_scale"),
               conv_w=step(conv_w, g_conv_w, m_conv_w, v_conv_w, "conv_w"), conv_b=step(conv_b, g_conv_b, m_conv_b, v_conv_b, "conv_b"),
               gate_a_w=step_big(2, gate_a_w, m_gate_a_w, v_gate_a_w), gate_a_b=step(gate_a_b, g_ba, m_gate_a_b, v_gate_a_b, "gate_a_b"),
               gate_x_w=step_big(3, gate_x_w, m_gate_x_w, v_gate_x_w), gate_x_b=step(gate_x_b, g_bx, m_gate_x_b, v_gate_x_b, "gate_x_b"),
               lru_lambda=step(lru_lambda, g_lam, m_lru_lambda, v_lru_lambda, "lru_lambda"), w_out=step_big(4, w_out, m_w_out, v_w_out),
               norm2_g=step(norm2_g, g_n2, m_norm2_g, v_norm2_g, "norm2_g"), mlp_w1=step_big(5, mlp_w1, m_mlp_w1, v_mlp_w1),
               mlp_w2=step_big(6, mlp_w2, m_mlp_w2, v_mlp_w2), final_g=step(final_g, g_fin, m_final_g, v_final_g, "final_g"))
    order = list(res)
    return (loss, grad_x[None], *[res[n][0] for n in order], *[res[n][1] for n in order], *[res[n][2] for n in order],
            *[res[n][3] for n in order])
```

```python
import functools

import jax
import jax.numpy as jnp
from jax import lax
from jax.experimental import pallas as pl
from jax.experimental.pallas import tpu as pltpu

F32 = jnp.float32
BF16 = jnp.bfloat16
MESH = pl.DeviceIdType.MESH

NORM_EPS = 1e-6
N_META = 16
HEAD_DIM = 256
POOL_WINDOWS = (2, 4, 8, 16)
LRU_C = 8.0
ROW_TILE = 128
HIST = 16
VMEM_LIMIT_BYTES = 56 * 1024 * 1024
ADAM_LR, ADAM_B1, ADAM_B2, ADAM_EPS, ADAM_WD, ADAM_STEP = 0.001, 0.9, 0.999, 1e-08, 0.01, 10


def _pick(n, prefs):
    for p in prefs:
        if n % p == 0:
            return p
    return n


def _params(*sem):
    return pltpu.CompilerParams(dimension_semantics=sem, vmem_limit_bytes=VMEM_LIMIT_BYTES)


def _sigmoid(x):
    return 1.0 / (1.0 + jnp.exp(-x))


def _gelu_tanh(x):
    t = jnp.tanh(0.7978845608028654 * (x + 0.044715 * (x * x * x)))
    return 0.5 * x * (1.0 + t), t


def _gelu_tanh_grad(x, t):
    return 0.5 * (1.0 + t) + 0.5 * x * (1.0 - t * t) * (0.7978845608028654 * (1.0 + 3.0 * 0.044715 * x * x))


def _neg_expm1(x, exp_x):
    series = x * (-1.0 + x * (-0.5 + x * ((-1.0 / 6.0) + x * ((-1.0 / 24.0) + x * (-1.0 / 120.0)))))
    return jnp.where(x > -0.125, series, 1.0 - exp_x)


def _softplus_neg(lam):
    z = jnp.exp(-jnp.abs(lam))
    log1p_z = jnp.where(z < 0.01, z * (1.0 - z * (0.5 - z * (1.0 / 3.0))), jnp.log(1.0 + z))
    return jnp.maximum(-lam, 0.0) + log1p_z


def _tile_masks(is_meta, rows):
    row = lax.broadcasted_iota(jnp.int32, (rows, 1), 0)
    valid = jnp.logical_or(jnp.logical_not(is_meta), row >= rows - N_META)
    t_log = jnp.where(is_meta, row - (rows - N_META), 1 << 20)
    return row, valid, t_log


def _window_count_inv(t_log, w):
    return 1.0 / jnp.clip(t_log + 1, 1, w).astype(F32)


def _dot_nt(a, b):
    return lax.dot_general(a, b, (((1,), (1,)), ((), ())), preferred_element_type=F32)


def _dot_tn(a, b):
    return lax.dot_general(a, b, (((0,), (0,)), ((), ())), preferred_element_type=F32)


def _place():
    x, y, c = lax.axis_index("x"), lax.axis_index("y"), lax.axis_index("c")
    chips = [(1 - x, y), (x, 1 - y), (1 - x, 1 - y)]
    return x, y, c, chips


def _remote(src, dst, send_sem, recv_sem, to):
    return pltpu.make_async_remote_copy(src_ref=src, dst_ref=dst, send_sem=send_sem, recv_sem=recv_sem, device_id=to,
                                        device_id_type=MESH)


class Stage:
    def __init__(self, arrays, out_shapes, aliases, n_copies, copies):
        self.arrays, self.out_shapes, self.aliases, self.n_copies, self.copies = list(arrays), list(out_shapes), aliases, n_copies, copies


def _sds(a):
    return jax.ShapeDtypeStruct(a.shape, a.dtype)


def stage_gather_ici(bufs):
    n = len(bufs)

    def copies(ins, outs, send, recv):
        x, y, c, chips = _place()
        me = 2 * x + y
        sends, recvs = [], []
        for t in range(n):
            for j, (px, py) in enumerate(chips):
                k = 3 * t + j
                mine, theirs = outs[t].at[me, c], outs[t].at[2 * px + py, c]
                sends.append(_remote(mine, mine, send.at[k], recv.at[k], (px, py, c)))
                recvs.append(_remote(theirs, theirs, send.at[k], recv.at[k], (px, py, c)))
        return sends, recvs

    return Stage(bufs, [_sds(b) for b in bufs], {t: t for t in range(n)}, 3 * n, copies)


def stage_gather_d2d(bufs):
    n = len(bufs)

    def copies(ins, outs, send, recv):
        x, y, c, chips = _place()
        sends, recvs = [], []
        for t in range(n):
            for j, (px, py) in enumerate(chips):
                k = 3 * t + j
                got, sib = outs[t].at[2 * px + py, c], outs[t].at[2 * px + py, 1 - c]
                sends.append(_remote(got, got, send.at[k], recv.at[k], (x, y, 1 - c)))
                recvs.append(_remote(sib, sib, send.at[k], recv.at[k], (x, y, 1 - c)))
        return sends, recvs

    return Stage(bufs, [_sds(b) for b in bufs], {t: t for t in range(n)}, 3 * n, copies)


def stage_to_sibling(grads):
    n = len(grads)

    def copies(ins, outs, send, recv):
        x, y, c, _ = _place()
        sends, recvs = [], []
        for t in range(n):
            for k4 in range(4):
                k = 4 * t + k4
                sends.append(_remote(ins[t].at[k4, 1 - c], outs[t].at[k4], send.at[k], recv.at[k], (x, y, 1 - c)))
                recvs.append(_remote(outs[t].at[k4], outs[t].at[k4], send.at[k], recv.at[k], (x, y, 1 - c)))
        return sends, recvs

    return Stage(grads, [jax.ShapeDtypeStruct((4, *g.shape[2:]), g.dtype) for g in grads], {}, 4 * n, copies)


def stage_to_chips(presums, peers=(0, 1, 2), into=None):
    n = len(presums)

    def copies(ins, outs, send, recv):
        x, y, c, chips = _place()
        sends, recvs = [], []
        for t in range(n):
            for slot, j in enumerate(peers):
                px, py = chips[j]
                k = len(peers) * t + slot
                sends.append(_remote(ins[t].at[2 * px + py], outs[t].at[j], send.at[k], recv.at[k], (px, py, c)))
                recvs.append(_remote(outs[t].at[j], outs[t].at[j], send.at[k], recv.at[k], (px, py, c)))
        return sends, recvs

    out_shapes = [jax.ShapeDtypeStruct((3, *p.shape[1:]), p.dtype) for p in presums]
    if into is None:
        return Stage(presums, out_shapes, {}, len(peers) * n, copies)
    return Stage([*presums, *into], out_shapes, {n + t: t for t in range(n)}, len(peers) * n, copies)


def stage_from_sibling(halves):
    n = len(halves)

    def copies(ins, outs, send, recv):
        x, y, c, _ = _place()
        sends = [_remote(ins[t], outs[t], send.at[t], recv.at[t], (x, y, 1 - c)) for t in range(n)]
        recvs = [_remote(outs[t], outs[t], send.at[t], recv.at[t], (x, y, 1 - c)) for t in range(n)]
        return sends, recvs

    return Stage(halves, [_sds(h) for h in halves], {}, n, copies)


def _any_specs(n):
    return [pl.BlockSpec(memory_space=pl.ANY)] * n


def _staged_call(body, *, grid, in_specs, out_specs, out_shape, scratch_shapes, name, semantics, inputs, stages=()):
    n_in, n_out, n_scr = len(in_specs), len(out_specs), len(scratch_shapes)
    st_arrays = [a for s in stages for a in s.arrays]
    st_outs = [o for s in stages for o in s.out_shapes]
    st_sems = [pltpu.SemaphoreType.DMA((s.n_copies,)) for s in stages for _ in range(2)]
    aliases = {}
    at_in, at_out = n_in, n_out
    for s in stages:
        for a, o in s.aliases.items():
            aliases[at_in + a] = at_out + o
        at_in += len(s.arrays)
        at_out += len(s.out_shapes)

    def full_body(*refs):
        pos = 0

        def take(count):
            nonlocal pos
            part = refs[pos:pos + count]
            pos += count
            return part

        ins, s_ins, outs, s_outs, scr = take(n_in), take(len(st_arrays)), take(n_out), take(len(st_outs)), take(n_scr)
        s_sems = refs[pos:]

        def each_stage(action):
            at_i = at_o = 0
            for idx, s in enumerate(stages):
                sends, recvs = s.copies(s_ins[at_i:at_i + len(s.arrays)], s_outs[at_o:at_o + len(s.out_shapes)],
                                        s_sems[2 * idx], s_sems[2 * idx + 1])
                action(sends, recvs)
                at_i += len(s.arrays)
                at_o += len(s.out_shapes)

        if stages:
            ids = [pl.program_id(a) for a in range(len(grid))]
            first = functools.reduce(jnp.logical_and, [i == 0 for i in ids])
            last = functools.reduce(jnp.logical_and, [i == g - 1 for i, g in zip(ids, grid)])

            def start(sends, recvs):
                for cp in sends:
                    cp.start()

            def finish(sends, recvs):
                for cp in recvs:
                    cp.wait_recv()
                for cp in sends:
                    cp.wait_send()

            @pl.when(first)
            def _():
                each_stage(start)

        body(*ins, *outs, *scr)

        if stages:
            @pl.when(last)
            def _():
                each_stage(finish)

    sem = tuple("arbitrary" for _ in grid) if stages else tuple(semantics)
    res = pl.pallas_call(
        full_body, grid=grid, in_specs=[*in_specs, *_any_specs(len(st_arrays))], out_specs=[*out_specs, *_any_specs(len(st_outs))],
        out_shape=[*out_shape, *st_outs], scratch_shapes=[*scratch_shapes, *st_sems], input_output_aliases=aliases, name=name,
        compiler_params=_params(*sem))(*inputs, *st_arrays)
    outs, rest = list(res[:n_out]), list(res[n_out:])
    per_stage = []
    for s in stages:
        per_stage.append(rest[:len(s.out_shapes)])
        rest = rest[len(s.out_shapes):]
    return outs, per_stage


def comm_call(stages, *, name):
    return _staged_call(lambda: None, grid=(1,), in_specs=[], out_specs=[], out_shape=[], scratch_shapes=[], name=name,
                        semantics=("arbitrary",), inputs=[], stages=stages)[1]


def _matmul(kind, a, b, *, grid, a_spec, b_spec, out_spec, out_shape, acc_shape, name,
            a_pro=None, epilogue=None, extras=(), extra_specs=(), stages=()):
    nk = grid[2]
    n_extra = len(extras)

    def body(a_ref, b_ref, *rest):
        extra_refs = rest[:n_extra]
        o_ref = rest[n_extra]
        av = a_ref[...]
        if a_pro is not None:
            av = a_pro(av)
        av = av.astype(BF16)
        bv = b_ref[...].astype(BF16)
        if kind == "nn":
            p = jnp.dot(av, bv, preferred_element_type=F32)
        elif kind == "nt":
            p = _dot_nt(av, bv)
        else:
            p = _dot_tn(av, bv)

        def finish(r):
            if epilogue is not None:
                r = epilogue(r, *[e[...] for e in extra_refs])
            o_ref[...] = r.astype(o_ref.dtype)

        if nk == 1:
            finish(p)
        else:
            acc_ref = rest[n_extra + 1]
            k = pl.program_id(2)

            @pl.when(k == 0)
            def _():
                acc_ref[...] = p

            @pl.when(k > 0)
            def _():
                acc_ref[...] += p

            @pl.when(k == nk - 1)
            def _():
                finish(acc_ref[...])

    scratch = [] if nk == 1 else [pltpu.VMEM(acc_shape, F32)]
    outs, staged = _staged_call(body, grid=grid, in_specs=[a_spec, b_spec, *extra_specs], out_specs=[out_spec], out_shape=[out_shape],
                                scratch_shapes=scratch, name=name, semantics=("parallel", "parallel", "arbitrary"),
                                inputs=[a, b, *extras], stages=stages)
    return (outs[0], staged) if stages else outs[0]


def mm_nn(a, b, *, out_dtype, name, a_pro=None, epilogue=None, extras=(), stages=(), out_shards=1, tiles=(None, None, None)):
    M, K = a.shape
    sharded = b.ndim == 3
    ns = b.shape[2] if sharded else b.shape[1]
    N = ns * b.shape[0] if sharded else ns
    if out_shards > 1:
        ns = N // out_shards
    tm = tiles[0] or _pick(M, (1408, 1024, 512, 256, 128))
    tn = tiles[1] or _pick(ns, (512, 256, 128))
    tk = tiles[2] or _pick(K, (2048, 1408, 1024, 512, 256, 128))
    per = ns // tn
    if sharded:
        b_spec = pl.BlockSpec((None, tk, tn), lambda i, j, k: (j // per, k, j % per))
    else:
        b_spec = pl.BlockSpec((tk, tn), lambda i, j, k: (k, j))
    mn = pl.BlockSpec((tm, tn), lambda i, j, k: (i, j))
    if out_shards > 1:
        out_spec = pl.BlockSpec((None, tm, tn), lambda i, j, k: (j // per, i, j % per))
        out_shape = jax.ShapeDtypeStruct((out_shards, M, ns), out_dtype)
    else:
        out_spec, out_shape = mn, jax.ShapeDtypeStruct((M, N), out_dtype)
    return _matmul("nn", a, b, grid=(M // tm, N // tn, K // tk), a_spec=pl.BlockSpec((tm, tk), lambda i, j, k: (i, k)),
                   b_spec=b_spec, out_spec=out_spec, out_shape=out_shape, acc_shape=(tm, tn),
                   name=name, a_pro=a_pro, epilogue=epilogue, extras=extras, extra_specs=[mn] * len(extras), stages=stages)


def mm_nt(a, w, *, out_dtype, name, epilogue=None, extras=(), stages=(), tiles=(None, None, None)):
    M, N = a.shape
    sharded = w.ndim == 3
    kw = w.shape[1] if sharded else w.shape[0]
    ns = w.shape[2] if sharded else w.shape[1]
    tm = tiles[0] or _pick(M, (1408, 512, 256, 128))
    tkw = tiles[1] or _pick(kw, (1024, 512, 256, 128))
    tk = tiles[2] or _pick(ns, (2048, 1280, 1024, 512, 256, 128))
    per = ns // tk
    if sharded:
        w_spec = pl.BlockSpec((None, tkw, tk), lambda i, j, k: (k // per, j, k % per))
    else:
        w_spec = pl.BlockSpec((tkw, tk), lambda i, j, k: (j, k))
    mo = pl.BlockSpec((tm, tkw), lambda i, j, k: (i, j))
    return _matmul("nt", a, w, grid=(M // tm, kw // tkw, N // tk), a_spec=pl.BlockSpec((tm, tk), lambda i, j, k: (i, k)),
                   b_spec=w_spec, out_spec=mo, out_shape=jax.ShapeDtypeStruct((M, kw), out_dtype), acc_shape=(tm, tkw),
                   name=name, epilogue=epilogue, extras=extras, extra_specs=[mo] * len(extras), stages=stages)


def mm_tn(a, g, *, shards, name, out_dtype, a_pro=None, stages=(), tiles=(None, None, None)):
    T, kw = a.shape
    N = g.shape[1]
    ns = N // shards
    tkw = tiles[0] or _pick(kw, (1024, 512, 256, 128))
    tn = tiles[1] or _pick(ns, (1280, 1024, 512, 256, 128))
    tt = tiles[2] or _pick(T, (1408, 512, 256, 128))
    per = ns // tn
    if shards > 1:
        out_spec = pl.BlockSpec((None, tkw, tn), lambda i, j, k: (j // per, i, j % per))
        out_shape = jax.ShapeDtypeStruct((shards, kw, ns), out_dtype)
    else:
        out_spec = pl.BlockSpec((tkw, tn), lambda i, j, k: (i, j))
        out_shape = jax.ShapeDtypeStruct((kw, N), out_dtype)
    return _matmul("tn", a, g, grid=(kw // tkw, N // tn, T // tt), a_spec=pl.BlockSpec((tt, tkw), lambda i, j, k: (k, i)),
                   b_spec=pl.BlockSpec((tt, tn), lambda i, j, k: (k, j)), out_spec=out_spec, out_shape=out_shape,
                   acc_shape=(tkw, tn), name=name, a_pro=a_pro, stages=stages)


def _relu_sq(a):
    r = jnp.maximum(a, 0.0)
    return r * r


def rmsnorm_fwd(h, g, *, name):
    Tp, D = h.shape
    tr = _pick(Tp, (384, 256, 128))

    def body(h_ref, g_ref, u_ref, ut_ref):
        x = h_ref[...]
        r = lax.rsqrt(jnp.mean(x * x, axis=-1, keepdims=True) + NORM_EPS)
        u = (x * r) * g_ref[...]
        u_ref[...] = u.astype(BF16)
        ut_ref[...] = u.T.astype(BF16)

    row = pl.BlockSpec((tr, D), lambda i: (i, 0))
    return pl.pallas_call(body, grid=(Tp // tr,), in_specs=[row, pl.BlockSpec((1, D), lambda i: (0, 0))],
                          out_specs=[row, pl.BlockSpec((D, tr), lambda i: (0, i))],
                          out_shape=[jax.ShapeDtypeStruct((Tp, D), BF16), jax.ShapeDtypeStruct((D, Tp), BF16)], name=name,
                          compiler_params=_params("parallel"))(h, g)


def rmsnorm_fwd_input(x, meta_tile, g, *, name):
    S, D = x.shape
    nt = S // ROW_TILE + 1

    def body(x_ref, mt_ref, g_ref, h_ref, u_ref, ut_ref):
        h = jnp.where(pl.program_id(0) == nt - 1, mt_ref[...], x_ref[...])
        r = lax.rsqrt(jnp.mean(h * h, axis=-1, keepdims=True) + NORM_EPS)
        u = (h * r) * g_ref[...]
        h_ref[...] = h
        u_ref[...] = u.astype(BF16)
        ut_ref[...] = u.T.astype(BF16)

    row = pl.BlockSpec((ROW_TILE, D), lambda i: (i, 0))
    return pl.pallas_call(
        body, grid=(nt,),
        in_specs=[pl.BlockSpec((ROW_TILE, D), lambda i: (jnp.minimum(i, nt - 2), 0)), pl.BlockSpec((ROW_TILE, D), lambda i: (0, 0)),
                  pl.BlockSpec((1, D), lambda i: (0, 0))],
        out_specs=[row, row, pl.BlockSpec((D, ROW_TILE), lambda i: (0, i))],
        out_shape=[jax.ShapeDtypeStruct((S + ROW_TILE, D), F32), jax.ShapeDtypeStruct((S + ROW_TILE, D), BF16),
                   jax.ShapeDtypeStruct((D, S + ROW_TILE), BF16)],
        name=name, compiler_params=_params("parallel"))(x, meta_tile, g)


def _rms_bwd_math(x, g, dy):
    r = lax.rsqrt(jnp.mean(x * x, axis=-1, keepdims=True) + NORM_EPS)
    xh = x * r
    dyg = dy * g
    dx = r * (dyg - xh * jnp.mean(dyg * xh, axis=-1, keepdims=True))
    return dx, xh


def final_loss(h2, target, gf, *, name):
    Tp, D = h2.shape
    nt = Tp // ROW_TILE

    def body(h_ref, t_ref, g_ref, dh_ref, dhb_ref, st_ref):
        i = pl.program_id(0)

        @pl.when(i == 0)
        def _():
            st_ref[...] = jnp.zeros_like(st_ref)

        x = h_ref[...]
        g = g_ref[...]
        r = lax.rsqrt(jnp.mean(x * x, axis=-1, keepdims=True) + NORM_EPS)
        xh = x * r
        err = jnp.where(i == nt - 1, 0.0, xh * g - t_ref[...])
        dout = err * (1.0 / D)
        dyg = dout * g
        dx = r * (dyg - xh * jnp.mean(dyg * xh, axis=-1, keepdims=True))
        dh_ref[...] = dx
        dhb_ref[...] = dx.astype(BF16)
        st_ref[0:1, :] += jnp.sum(dout * xh, axis=0, keepdims=True)
        st_ref[1:2, :] += jnp.sum(err * err, axis=0, keepdims=True) * (0.5 / D)

    row = pl.BlockSpec((ROW_TILE, D), lambda i: (i, 0))
    return pl.pallas_call(
        body, grid=(nt,),
        in_specs=[row, pl.BlockSpec((ROW_TILE, D), lambda i: (jnp.minimum(i, nt - 2), 0)), pl.BlockSpec((1, D), lambda i: (0, 0))],
        out_specs=[row, row, pl.BlockSpec((8, D), lambda i: (0, 0))],
        out_shape=[jax.ShapeDtypeStruct((Tp, D), F32), jax.ShapeDtypeStruct((Tp, D), BF16), jax.ShapeDtypeStruct((8, D), F32)],
        name=name, compiler_params=_params("arbitrary"))(h2, target, gf)


def rms_bwd(h, g, du, dres, *, name):
    Tp, D = h.shape
    tr = _pick(Tp, (384, 256, 128))

    def body(h_ref, g_ref, du_ref, dr_ref, dh_ref, dhb_ref, st_ref):
        @pl.when(pl.program_id(0) == 0)
        def _():
            st_ref[...] = jnp.zeros_like(st_ref)

        du_v = du_ref[...].astype(F32)
        dx, xh = _rms_bwd_math(h_ref[...], g_ref[...], du_v)
        dh = dr_ref[...] + dx
        dh_ref[...] = dh
        dhb_ref[...] = dh.astype(BF16)
        st_ref[0:1, :] += jnp.sum(du_v * xh, axis=0, keepdims=True)

    row = pl.BlockSpec((tr, D), lambda i: (i, 0))
    return pl.pallas_call(
        body, grid=(Tp // tr,), in_specs=[row, pl.BlockSpec((1, D), lambda i: (0, 0)), row, row],
        out_specs=[row, row, pl.BlockSpec((8, D), lambda i: (0, 0))],
        out_shape=[jax.ShapeDtypeStruct((Tp, D), F32), jax.ShapeDtypeStruct((Tp, D), BF16), jax.ShapeDtypeStruct((8, D), F32)],
        name=name, compiler_params=_params("arbitrary"))(h, g, du, dres)


def rms_bwd_input(h, g, du, dres, *, name, stages=()):
    Tp, D = h.shape
    nt = Tp // ROW_TILE

    def body(h_ref, g_ref, du_ref, dr_ref, gx_ref, gm_ref, st_ref):
        i = pl.program_id(0)

        @pl.when(i == 0)
        def _():
            st_ref[...] = jnp.zeros_like(st_ref)

        du_v = du_ref[...].astype(F32)
        dx, xh = _rms_bwd_math(h_ref[...], g_ref[...], du_v)
        dh = dr_ref[...] + dx
        st_ref[0:1, :] += jnp.sum(du_v * xh, axis=0, keepdims=True)

        @pl.when(i < nt - 1)
        def _():
            gx_ref[...] = dh

        @pl.when(i == nt - 1)
        def _():
            gm_ref[...] = dh[ROW_TILE - N_META:, :]

    row = pl.BlockSpec((ROW_TILE, D), lambda i: (i, 0))
    outs, staged = _staged_call(
        body, grid=(nt,), in_specs=[row, pl.BlockSpec((1, D), lambda i: (0, 0)), row, row],
        out_specs=[pl.BlockSpec((ROW_TILE, D), lambda i: (jnp.minimum(i, nt - 2), 0)), pl.BlockSpec((N_META, D), lambda i: (0, 0)),
                   pl.BlockSpec((8, D), lambda i: (0, 0))],
        out_shape=[jax.ShapeDtypeStruct((Tp - ROW_TILE, D), F32), jax.ShapeDtypeStruct((N_META, D), F32),
                   jax.ShapeDtypeStruct((8, D), F32)],
        scratch_shapes=[], name=name, semantics=("arbitrary",), inputs=[h, g, du, dres], stages=stages)
    return (outs, staged) if stages else outs


def _conv_shifts(ext):
    return tuple(pltpu.roll(ext, k, 0)[HIST:, :] for k in (1, 2, 3))


def _conv_taps(ext, cur, vec_ref, cs, rows):
    del rows
    x1, x2, x3 = _conv_shifts(ext)
    xc = (vec_ref[1:2, cs] + vec_ref[8:9, cs] * cur + vec_ref[7:8, cs] * x1 + vec_ref[6:7, cs] * x2 + vec_ref[5:6, cs] * x3)
    return xc, x1, x2, x3


def _window_sum_back(ext, w):
    s, sh = ext, 1
    while sh < w:
        s = s + pltpu.roll(s, sh, 0)
        sh *= 2
    return s[HIST:, :]


def _scan_rows(a, b, carry, *, reverse):
    rows = a.shape[0]
    rin = jnp.bitwise_and(lax.broadcasted_iota(jnp.int32, (rows, 1), 0), 7)
    sh = 1
    while sh < 8:
        keep = (rin < 8 - sh) if reverse else (rin >= sh)
        amount = rows - sh if reverse else sh
        a_sh = jnp.where(keep, pltpu.roll(a, amount, 0), 1.0)
        b_sh = jnp.where(keep, pltpu.roll(b, amount, 0), 0.0)
        b = b + a * b_sh
        a = a * a_sh
        sh *= 2
    out = [None] * (rows // 8)
    for g in (reversed(range(rows // 8)) if reverse else range(rows // 8)):
        hg = b[8 * g:8 * g + 8, :] + a[8 * g:8 * g + 8, :] * carry
        carry = hg[0:1, :] if reverse else hg[7:8, :]
        out[g] = hg
    return jnp.concatenate(out, axis=0)


def _lru_gates(xc, wa, wx, vec_ref, cs, sp):
    xcb = xc.astype(BF16)
    r = _sigmoid(jnp.dot(xcb, wa, preferred_element_type=F32) + vec_ref[2:3, cs])
    ig = _sigmoid(jnp.dot(xcb, wx, preferred_element_type=F32) + vec_ref[3:4, cs])
    log_a = (-LRU_C) * r * sp
    a = jnp.exp(log_a)
    a2 = a * a
    return xcb, r, ig, a, a2, _neg_expm1(2.0 * log_a, a2)


def mix_fwd(proj, pool_w, gate_a, gate_x, vecs, *, name, stages=()):
    Tp = proj.shape[0]
    D = proj.shape[1] // 5
    R = ROW_TILE
    nt = Tp // R
    H = D // HEAD_DIM
    PG = D // len(POOL_WINDOWS)

    def body(p_ref, pw_ref, wa_ref, wx_ref, vec_ref, m_ref, hs_ref, xc_ref, r_ref, ig_ref, a_ref, mu_ref, ge_ref, dge_ref,
             hist_p, hist_l, hcar, mtmp):
        i = pl.program_id(0)
        is_meta = i == 0

        @pl.when(is_meta)
        def _():
            hist_p[...] = jnp.zeros_like(hist_p)
            hist_l[...] = jnp.zeros_like(hist_l)
            hcar[...] = jnp.zeros_like(hcar)

        row, valid, t_log = _tile_masks(is_meta, R)

        for g, w in enumerate(POOL_WINDOWS):
            cs = slice(g * PG, (g + 1) * PG)
            v = p_ref[:, g * PG:(g + 1) * PG].astype(F32)
            ws = _window_sum_back(jnp.concatenate([hist_p[:, cs], v], axis=0), w)
            d = ws * _window_count_inv(t_log, w) - v
            y = jnp.dot(d.astype(BF16), pw_ref[g], preferred_element_type=F32)
            gp = p_ref[:, 3 * D + g * PG:3 * D + (g + 1) * PG].astype(F32)
            mtmp[:, cs] = _sigmoid(gp) * (y * vec_ref[0:1, cs])
            hist_p[:, cs] = v[R - HIST:, :]

        for h in range(H):
            cs = slice(h * HEAD_DIM, (h + 1) * HEAD_DIM)
            vl = p_ref[:, D + h * HEAD_DIM:D + (h + 1) * HEAD_DIM].astype(F32)
            xc, _, _, _ = _conv_taps(jnp.concatenate([hist_l[:, cs], vl], axis=0), vl, vec_ref, cs, R)
            sp = _softplus_neg(vec_ref[4:5, cs])
            xcb, r, ig, a, _, em = _lru_gates(xc, wa_ref[h], wx_ref[h], vec_ref, cs, sp)
            mult = jnp.sqrt(em)
            b = jnp.where(valid, mult * (ig * xc), 0.0)
            hs = _scan_rows(a, b, hcar[7:8, cs], reverse=False)
            hs_ref[:, cs] = hs
            xc_ref[:, cs], r_ref[:, cs], ig_ref[:, cs] = xcb, r.astype(BF16), ig.astype(BF16)
            a_ref[:, cs], mu_ref[:, cs] = a, mult
            hcar[:, cs] = hs[R - 8:, :]
            hist_l[:, cs] = vl[R - HIST:, :]
            vg = p_ref[:, 2 * D + h * HEAD_DIM:2 * D + (h + 1) * HEAD_DIM].astype(F32)
            ge, th = _gelu_tanh(vg)
            ge_ref[:, cs], dge_ref[:, cs] = ge.astype(BF16), _gelu_tanh_grad(vg, th).astype(BF16)
            gl = p_ref[:, 4 * D + h * HEAD_DIM:4 * D + (h + 1) * HEAD_DIM].astype(F32)
            m_ref[:, cs] = (mtmp[:, cs] + _sigmoid(gl) * (hs * ge)).astype(BF16)

    def tile(i):
        return (i + nt - 1) % nt

    full = lambda shape: pl.BlockSpec(shape, lambda i: (0,) * len(shape))
    outs, staged = _staged_call(
        body, grid=(nt,),
        in_specs=[pl.BlockSpec((R, 5 * D), lambda i: (tile(i), 0)), full(pool_w.shape), full(gate_a.shape), full(gate_x.shape),
                  full(vecs.shape)],
        out_specs=[pl.BlockSpec((R, D), lambda i: (tile(i), 0))] * 9,
        out_shape=[jax.ShapeDtypeStruct((Tp, D), dt) for dt in (BF16, F32, BF16, BF16, BF16, F32, F32, BF16, BF16)],
        scratch_shapes=[pltpu.VMEM((HIST, D), F32), pltpu.VMEM((HIST, D), F32), pltpu.VMEM((8, D), F32), pltpu.VMEM((R, D), F32)],
        name=name, semantics=("arbitrary",), inputs=[proj, pool_w, gate_a, gate_x, vecs], stages=stages)
    outs = [outs[0], outs[1], outs[2:]]
    return (outs, staged) if stages else outs


def mix_bwd(proj, hs, saved, dmerged, pool_w, gate_a, gate_x, vecs, *, name, stages=()):
    Tp = proj.shape[0]
    D = proj.shape[1] // 5
    R = ROW_TILE
    nt = Tp // R
    H = D // HEAD_DIM
    PG = D // len(POOL_WINDOWS)

    def body(p_ref, pprev_ref, hs_ref, hprev_ref, dm_ref, xc_ref, r_ref, ig_ref, a_ref, mu_ref, ge_ref, dge_ref, pw_ref, wa_ref, wx_ref, vec_ref,
             dp_ref, dpw_ref, dwa_ref, dwx_ref, dvec_ref, car_g, fut_dxc, fut_q):
        i = pl.program_id(0)
        is_meta = i == nt - 1

        @pl.when(i == 0)
        def _():
            dpw_ref[...] = jnp.zeros_like(dpw_ref)
            dwa_ref[...] = jnp.zeros_like(dwa_ref)
            dwx_ref[...] = jnp.zeros_like(dwx_ref)
            dvec_ref[...] = jnp.zeros_like(dvec_ref)
            car_g[...] = jnp.zeros_like(car_g)
            fut_dxc[...] = jnp.zeros_like(fut_dxc)
            fut_q[...] = jnp.zeros_like(fut_q)

        row, valid, t_log = _tile_masks(is_meta, R)
        keep_prev = jnp.logical_not(is_meta)

        def colsum(x):
            return jnp.sum(x, axis=0, keepdims=True)

        for g, w in enumerate(POOL_WINDOWS):
            cs = slice(g * PG, (g + 1) * PG)
            v = p_ref[:, g * PG:(g + 1) * PG].astype(F32)
            vprev = jnp.where(keep_prev, pprev_ref[:, g * PG:(g + 1) * PG].astype(F32), 0.0)
            inv_cnt = _window_count_inv(t_log, w)
            d = _window_sum_back(jnp.concatenate([vprev, v], axis=0), w) * inv_cnt - v
            d_bf = d.astype(BF16)
            y = jnp.dot(d_bf, pw_ref[g], preferred_element_type=F32)
            scale = vec_ref[0:1, cs]
            sg = _sigmoid(p_ref[:, 3 * D + g * PG:3 * D + (g + 1) * PG].astype(F32))
            dm = dm_ref[:, cs].astype(F32)
            dpo = dm * sg
            dp_ref[:, 3 * D + g * PG:3 * D + (g + 1) * PG] = (dm * (y * scale) * sg * (1.0 - sg)).astype(BF16)
            dvec_ref[0:1, cs] += colsum(dpo * y)
            dy = (dpo * scale).astype(BF16)
            dd = _dot_nt(dy, pw_ref[g])
            dpw_ref[g] += _dot_tn(d_bf, dy)
            q = dd * inv_cnt
            s, sh = jnp.concatenate([q, fut_q[:, cs]], axis=0), 1
            while sh < w:
                s = s + pltpu.roll(s, R + HIST - sh, 0)
                sh *= 2
            dp_ref[:, g * PG:(g + 1) * PG] = (s[:R, :] - dd).astype(BF16)
            fut_q[:, cs] = q[:HIST, :]

        for h in range(H):
            cs = slice(h * HEAD_DIM, (h + 1) * HEAD_DIM)
            pc = lambda blk: slice(blk * D + h * HEAD_DIM, blk * D + (h + 1) * HEAD_DIM)
            vl = p_ref[:, pc(1)].astype(F32)
            vlprev = jnp.where(keep_prev, pprev_ref[:, pc(1)].astype(F32), 0.0)
            x1, x2, x3 = _conv_shifts(jnp.concatenate([vlprev, vl], axis=0))
            lam = vec_ref[4:5, cs]
            sp = _softplus_neg(lam)
            xcb = xc_ref[:, cs]
            xc, r, ig = xcb.astype(F32), r_ref[:, cs].astype(F32), ig_ref[:, cs].astype(F32)
            a, mult = a_ref[:, cs], mu_ref[:, cs]
            a2 = a * a
            inv_mult = 1.0 / mult
            hsv = hs_ref[:, cs]
            hprev = jnp.where(row >= 1, pltpu.roll(hsv, 1, 0), hprev_ref[HIST - 1:HIST, cs])
            ge = ge_ref[:, cs].astype(F32)
            sgl = _sigmoid(p_ref[:, pc(4)].astype(F32))
            dm = dm_ref[:, cs].astype(F32)
            dlo = dm * sgl
            dp_ref[:, pc(4)] = (dm * (hsv * ge) * sgl * (1.0 - sgl)).astype(BF16)
            dp_ref[:, pc(2)] = (dlo * hsv * dge_ref[:, cs].astype(F32)).astype(BF16)
            a_next = jnp.where(row < R - 1, pltpu.roll(a, R - 1, 0), 1.0)
            G = _scan_rows(a_next, dlo * ge, car_g[0:1, cs], reverse=True)
            car_g[:, cs] = (a * G)[0:8, :]
            da = jnp.where(valid, G * hprev, 0.0)
            db = jnp.where(valid, G, 0.0)
            dmult = db * (ig * xc)
            dig = db * (mult * xc)
            dxc = db * (mult * ig)
            dlog_a = da * a - dmult * (a2 * inv_mult)
            dvec_ref[4:5, cs] += colsum(dlog_a * r) * (-LRU_C)
            dr = dlog_a * ((-LRU_C) * sp)
            dpa = dr * r * (1.0 - r)
            dpx = dig * ig * (1.0 - ig)
            dpa_bf = dpa.astype(BF16)
            dpx_bf = dpx.astype(BF16)
            dwa_ref[h] += _dot_tn(xcb, dpa_bf)
            dwx_ref[h] += _dot_tn(xcb, dpx_bf)
            dvec_ref[2:3, cs] += colsum(dpa)
            dvec_ref[3:4, cs] += colsum(dpx)
            dxc = dxc + _dot_nt(dpa_bf, wa_ref[h]) + _dot_nt(dpx_bf, wx_ref[h])
            ext = jnp.concatenate([dxc, fut_dxc[:, cs]], axis=0)
            n = R + HIST
            dvl = (vec_ref[8:9, cs] * dxc + vec_ref[7:8, cs] * pltpu.roll(ext, n - 1, 0)[:R, :]
                   + vec_ref[6:7, cs] * pltpu.roll(ext, n - 2, 0)[:R, :] + vec_ref[5:6, cs] * pltpu.roll(ext, n - 3, 0)[:R, :])
            dp_ref[:, pc(1)] = dvl.astype(BF16)
            dvec_ref[1:2, cs] += colsum(dxc)
            dvec_ref[8:9, cs] += colsum(dxc * vl)
            dvec_ref[7:8, cs] += colsum(dxc * x1)
            dvec_ref[6:7, cs] += colsum(dxc * x2)
            dvec_ref[5:6, cs] += colsum(dxc * x3)
            fut_dxc[:, cs] = dxc[:HIST, :]

            @pl.when(is_meta)
            def _():
                dvec_ref[4:5, cs] = dvec_ref[4:5, cs] * (-_sigmoid(-lam))

    def tile(i):
        return (2 * nt - 2 - i) % nt

    def prev_blk(i):
        per = R // HIST
        return jnp.where(i == nt - 1, 0, jnp.where(i == nt - 2, Tp // HIST - 1, (nt - 2 - i) * per - 1))

    full = lambda shape: pl.BlockSpec(shape, lambda i: (0,) * len(shape))
    G_ = len(POOL_WINDOWS)
    outs, staged = _staged_call(
        body, grid=(nt,),
        in_specs=[pl.BlockSpec((R, 5 * D), lambda i: (tile(i), 0)), pl.BlockSpec((HIST, 5 * D), lambda i: (prev_blk(i), 0)),
                  pl.BlockSpec((R, D), lambda i: (tile(i), 0)), pl.BlockSpec((HIST, D), lambda i: (prev_blk(i), 0)),
                  *[pl.BlockSpec((R, D), lambda i: (tile(i), 0))] * 8,
                  full(pool_w.shape), full(gate_a.shape), full(gate_x.shape), full(vecs.shape)],
        out_specs=[pl.BlockSpec((R, 5 * D), lambda i: (tile(i), 0)), full((G_, PG, PG)), full((H, HEAD_DIM, HEAD_DIM)),
                   full((H, HEAD_DIM, HEAD_DIM)), full((16, D))],
        out_shape=[jax.ShapeDtypeStruct((Tp, 5 * D), BF16), jax.ShapeDtypeStruct((G_, PG, PG), F32),
                   jax.ShapeDtypeStruct((H, HEAD_DIM, HEAD_DIM), F32), jax.ShapeDtypeStruct((H, HEAD_DIM, HEAD_DIM), F32),
                   jax.ShapeDtypeStruct((16, D), F32)],
        scratch_shapes=[pltpu.VMEM((8, D), F32), pltpu.VMEM((HIST, D), F32), pltpu.VMEM((HIST, D), F32)],
        name=name, semantics=("arbitrary",), inputs=[proj, proj, hs, hs, dmerged, *saved, pool_w, gate_a, gate_x, vecs], stages=stages)
    return (outs, staged) if stages else outs


def _adamw_math(w, g, m, v):
    mn = ADAM_B1 * m + (1.0 - ADAM_B1) * g
    vn = ADAM_B2 * v + (1.0 - ADAM_B2) * (g * g)
    m_hat = mn / (1.0 - ADAM_B1 ** ADAM_STEP)
    v_hat = vn / (1.0 - ADAM_B2 ** ADAM_STEP)
    return -ADAM_LR * (m_hat / (jnp.sqrt(v_hat) + ADAM_EPS) + ADAM_WD * w), mn, vn


def adamw(w, g, m, v, *, name):
    rows, cols = w.shape
    tr = _pick(rows, (256, 128, 64, 32, 16, 8))

    def body(w_ref, g_ref, m_ref, v_ref, go_ref, d_ref, mo_ref, vo_ref):
        gv = g_ref[...]
        go_ref[...] = gv
        d_ref[...], mo_ref[...], vo_ref[...] = _adamw_math(w_ref[...], gv, m_ref[...], v_ref[...])

    blk = pl.BlockSpec((tr, cols), lambda i: (i, 0))
    sds = jax.ShapeDtypeStruct((rows, cols), F32)
    return pl.pallas_call(body, grid=(rows // tr,), in_specs=[blk] * 4, out_specs=[blk] * 4, out_shape=[sds] * 4, name=name,
                          compiler_params=_params("parallel"))(w, g, m, v)


def allgather8(block, *, name, reduce_sum=False, stages=()):
    rows, cols = block.shape

    def body(x_ref, out_ref, *scratch):
        if reduce_sum:
            buf, send_sems, recv_sems, local_sem = scratch
        else:
            buf = out_ref
            send_sems, recv_sems, local_sem = scratch
        x, y, c, chips = _place()
        me, sibling = (x, y, c), (x, y, 1 - c)

        def slot(px, py, pc):
            return buf.at[4 * px + 2 * py + pc]

        def copy(k, blk, to, src=None):
            return pltpu.make_async_remote_copy(src_ref=slot(*blk) if src is None else src, dst_ref=slot(*blk),
                                                send_sem=send_sems.at[k], recv_sem=recv_sems.at[k], device_id=to, device_id_type=MESH)

        mine = pltpu.make_async_copy(x_ref, slot(*me), local_sem)
        mine.start()
        first = [copy(0, me, sibling, src=x_ref)]
        first += [copy(1 + j, me, (*chip, c), src=x_ref) for j, chip in enumerate(chips)]
        for cp in first:
            cp.start()
        passed = [copy(4 + j, (*chip, c), sibling) for j, chip in enumerate(chips)]
        for j, chip in enumerate(chips):
            copy(1 + j, (*chip, c), me).wait_recv()
            passed[j].start()
        copy(0, sibling, me).wait_recv()
        for j, chip in enumerate(chips):
            copy(4 + j, (*chip, 1 - c), me).wait_recv()
        for cp in first + passed:
            cp.wait_send()
        mine.wait()
        if reduce_sum:
            acc = buf[0]
            for d in range(1, 8):
                acc = acc + buf[d]
            out_ref[...] = acc

    sems = [pltpu.SemaphoreType.DMA((7,)), pltpu.SemaphoreType.DMA((7,)), pltpu.SemaphoreType.DMA]
    if reduce_sum:
        out_shape = jax.ShapeDtypeStruct((rows, cols), block.dtype)
        scratch = [pltpu.VMEM((8, rows, cols), block.dtype)] + sems
    else:
        out_shape = jax.ShapeDtypeStruct((8, rows, cols), block.dtype)
        scratch = sems
    vmem = pl.BlockSpec(memory_space=pltpu.VMEM)
    outs, staged = _staged_call(body, grid=(1,), in_specs=[vmem], out_specs=[vmem], out_shape=[out_shape], scratch_shapes=scratch,
                                name=name, semantics=("arbitrary",), inputs=[block], stages=stages)
    return (outs[0], staged) if stages else outs[0]


def cast_into_slot(w, chip_arr, *, name):
    _, r, cols = w.shape
    tr = _pick(r, (256, 128, 64, 32, 16))

    def body(chip_ref, w_ref, o_ref):
        del chip_ref
        o_ref[...] = w_ref[...].astype(BF16)

    grid_spec = pltpu.PrefetchScalarGridSpec(
        num_scalar_prefetch=1, grid=(2, r // tr),
        in_specs=[pl.BlockSpec((None, tr, cols), lambda h, i, chip: (h, i, 0))],
        out_specs=pl.BlockSpec((None, None, tr, cols), lambda h, i, chip: (chip[0], h, i, 0)))
    return pl.pallas_call(body, grid_spec=grid_spec, out_shape=jax.ShapeDtypeStruct((4, 2, r, cols), BF16), name=name,
                          compiler_params=_params("parallel", "parallel"))(chip_arr, w)


def proj_with_gather(u, bufs, order_arr, *, name):
    Tp, K = u.shape
    n = len(bufs)
    Ns = bufs[0].shape[3]
    tm = _pick(Tp, (1408, 512, 256, 128))
    tc = _pick(Ns, (512, 256, 128))
    n_rows = Tp // tm

    def body(order_ref, u_ref, *refs):
        del order_ref
        o_ref, outs = refs[n], refs[n + 1:2 * n + 1]
        wbuf, ici_send, ici_recv, d2d_send, d2d_recv, load_sems = refs[2 * n + 1:]
        s, i = pl.program_id(0), pl.program_id(1)
        x, y, c, chips = _place()
        me = 2 * x + y

        def ici(t, j, landing):
            px, py = chips[j]
            slot = outs[t].at[2 * px + py, c] if landing else outs[t].at[me, c]
            return _remote(slot, slot, ici_send.at[t, j], ici_recv.at[t, j], (px, py, c))

        def d2d(t, j, landing):
            px, py = chips[j]
            slot = outs[t].at[2 * px + py, 1 - c] if landing else outs[t].at[2 * px + py, c]
            return _remote(slot, slot, d2d_send.at[t, j], d2d_recv.at[t, j], (x, y, 1 - c))

        def load(chip_idx):
            parts = [pltpu.make_async_copy(outs[0].at[chip_idx, hh], wbuf.at[pl.ds(hh * (K // 2), K // 2), :], load_sems.at[hh])
                     for hh in range(2)]
            for cp in parts:
                cp.start()
            for cp in parts:
                cp.wait()

        @pl.when(jnp.logical_and(s == 0, i == 0))
        def _():
            for t in range(n):
                for j in range(3):
                    ici(t, j, False).start()
            load(me)

        for j in range(3):
            @pl.when(jnp.logical_and(s == j + 1, i == 0))
            def _(j=j):
                ici(0, j, True).wait_recv()
                d2d(0, j, False).start()
                d2d(0, j, True).wait_recv()
                px, py = chips[j]
                load(2 * px + py)

        uv = u_ref[...]
        for cc in range(Ns // tc):
            o_ref[:, cc * tc:(cc + 1) * tc] = jnp.dot(uv, wbuf[:, cc * tc:(cc + 1) * tc], preferred_element_type=F32).astype(BF16)

        @pl.when(jnp.logical_and(s == 3, i == n_rows - 1))
        def _():
            for t in range(1, n):
                for j in range(3):
                    ici(t, j, True).wait_recv()
                    d2d(t, j, False).start()
            for t in range(1, n):
                for j in range(3):
                    d2d(t, j, True).wait_recv()
            for t in range(n):
                for j in range(3):
                    ici(t, j, False).wait_send()
                    d2d(t, j, False).wait_send()

    grid_spec = pltpu.PrefetchScalarGridSpec(
        num_scalar_prefetch=1, grid=(4, n_rows),
        in_specs=[pl.BlockSpec((tm, K), lambda s, i, order: (i, 0)), *_any_specs(n)],
        out_specs=[pl.BlockSpec((tm, Ns), lambda s, i, order: (i, order[s])), *_any_specs(n)],
        scratch_shapes=[pltpu.VMEM((K, Ns), BF16), pltpu.SemaphoreType.DMA((n, 3)), pltpu.SemaphoreType.DMA((n, 3)),
                        pltpu.SemaphoreType.DMA((n, 3)), pltpu.SemaphoreType.DMA((n, 3)), pltpu.SemaphoreType.DMA((2,))])
    res = pl.pallas_call(body, grid_spec=grid_spec, out_shape=[jax.ShapeDtypeStruct((Tp, 4 * Ns), BF16), *[_sds(b) for b in bufs]],
                         input_output_aliases={2 + t: 1 + t for t in range(n)}, name=name,
                         compiler_params=_params("arbitrary", "arbitrary"))(order_arr, u, *bufs)
    return res[0], list(res[1:])


def chip_presum(grad, recv, c_arr, *, name):
    _, _, r, cols = grad.shape
    tr = _pick(r, (256, 128, 64, 32, 16))

    def body(c_ref, g_ref, r_ref, o_ref):
        del c_ref
        o_ref[...] = (g_ref[...].astype(F32) + r_ref[...].astype(F32)).astype(BF16)

    grid_spec = pltpu.PrefetchScalarGridSpec(
        num_scalar_prefetch=1, grid=(4, r // tr),
        in_specs=[pl.BlockSpec((None, None, tr, cols), lambda k, i, c_ref: (k, c_ref[0], i, 0)),
                  pl.BlockSpec((None, tr, cols), lambda k, i, c_ref: (k, i, 0))],
        out_specs=pl.BlockSpec((None, tr, cols), lambda k, i, c_ref: (k, i, 0)))
    return pl.pallas_call(body, grid_spec=grid_spec, out_shape=jax.ShapeDtypeStruct((4, r, cols), BF16), name=name,
                          compiler_params=_params("parallel", "parallel"))(c_arr, grad, recv)


def final_half(grad, recv, got, mc_arr, *, name):
    _, _, r, cols = grad.shape
    tr = _pick(r, (256, 128, 64, 32, 16))

    def body(mc_ref, g_ref, r_ref, q_ref, o_ref):
        del mc_ref
        acc = g_ref[...].astype(F32) + r_ref[...].astype(F32)
        for j in range(3):
            acc = acc + q_ref[j].astype(F32)
        o_ref[...] = acc

    grid_spec = pltpu.PrefetchScalarGridSpec(
        num_scalar_prefetch=1, grid=(r // tr,),
        in_specs=[pl.BlockSpec((None, None, tr, cols), lambda i, mc: (mc[0], mc[1], i, 0)),
                  pl.BlockSpec((None, tr, cols), lambda i, mc: (mc[0], i, 0)),
                  pl.BlockSpec((3, tr, cols), lambda i, mc: (0, i, 0))],
        out_specs=pl.BlockSpec((tr, cols), lambda i, mc: (i, 0)))
    return pl.pallas_call(body, grid_spec=grid_spec, out_shape=jax.ShapeDtypeStruct((r, cols), F32), name=name,
                          compiler_params=_params("parallel"))(mc_arr, grad, recv, got)


def adamw_halves(w, mine, theirs, m, v, c_arr, *, name):
    _, r, cols = w.shape
    tr = _pick(r, (256, 128, 64, 32, 16, 8))

    def body(c_ref, w_ref, mine_ref, theirs_ref, m_ref, v_ref, go_ref, d_ref, mo_ref, vo_ref):
        gv = jnp.where(pl.program_id(0) == c_ref[0], mine_ref[...], theirs_ref[...])
        go_ref[...] = gv
        d_ref[...], mo_ref[...], vo_ref[...] = _adamw_math(w_ref[...], gv, m_ref[...], v_ref[...])

    blk = pl.BlockSpec((None, tr, cols), lambda h, i, c_ref: (h, i, 0))
    grid_spec = pltpu.PrefetchScalarGridSpec(
        num_scalar_prefetch=1, grid=(2, r // tr),
        in_specs=[blk, pl.BlockSpec((tr, cols), lambda h, i, c_ref: (jnp.where(h == c_ref[0], i, 0), 0)),
                  pl.BlockSpec((tr, cols), lambda h, i, c_ref: (jnp.where(h == c_ref[0], 0, i), 0)), blk, blk],
        out_specs=[blk] * 4)
    sds = jax.ShapeDtypeStruct((2, r, cols), F32)
    return pl.pallas_call(body, grid_spec=grid_spec, out_shape=[sds] * 4, name=name,
                          compiler_params=_params("parallel", "parallel"))(c_arr, w, mine, theirs, m, v)


def _pad_rows(a, rows):
    return jnp.pad(a, ((0, rows - a.shape[0]), (0, 0)))


def kernel(x, meta_tokens, norm1_g, w_in, pool_w, pool_scale, conv_w, conv_b, gate_a_w, gate_a_b, gate_x_w, gate_x_b, lru_lambda, w_out, norm2_g, mlp_w1, mlp_w2, final_g, loss_target, m_meta_tokens, m_norm1_g, m_w_in, m_pool_w, m_pool_scale, m_conv_w, m_conv_b, m_gate_a_w, m_gate_a_b, m_gate_x_w, m_gate_x_b, m_lru_lambda, m_w_out, m_norm2_g, m_mlp_w1, m_mlp_w2, m_final_g, v_meta_tokens, v_norm1_g, v_w_in, v_pool_w, v_pool_scale, v_conv_w, v_conv_b, v_gate_a_w, v_gate_a_b, v_gate_x_w, v_gate_x_b, v_lru_lambda, v_w_out, v_norm2_g, v_mlp_w1, v_mlp_w2, v_final_g):
    D = x.shape[-1]
    H = D // HEAD_DIM
    G = len(POOL_WINDOWS)
    PG = D // G
    ax, ay, ac = lax.axis_index("x"), lax.axis_index("y"), lax.axis_index("c")
    chip = 2 * ax + ay
    dshard = D // 4

    c_arr = jnp.reshape(ac, (1,)).astype(jnp.int32)
    chip_arr = jnp.reshape(chip, (1,)).astype(jnp.int32)
    mc_arr = jnp.stack([chip, ac]).astype(jnp.int32)
    names = ["w_in", "pool_w", "gate_a_w", "gate_x_w", "w_out", "mlp_w1", "mlp_w2"]
    big = [w_in, pool_w, gate_a_w, gate_x_w, w_out, mlp_w1, mlp_w2]

    def halves(w):
        w2d = w.reshape(-1, w.shape[-1])
        return w2d.reshape(2, w2d.shape[0] // 2, w2d.shape[1])

    bufs = [cast_into_slot(halves(w), chip_arr, name="cast_" + nm) for w, nm in zip(big, names)]
    order_arr = jnp.stack([chip, 2 * (1 - ax) + ay, 2 * ax + (1 - ay), 2 * (1 - ax) + (1 - ay)]).astype(jnp.int32)

    small_in = jnp.concatenate([meta_tokens, _pad_rows(conv_w[0], 8), _pad_rows(gate_a_b.reshape(1, dshard), 8),
                                _pad_rows(gate_x_b.reshape(1, dshard), 8)], axis=0)
    sm = allgather8(small_in, name="gather_small")[0::2]
    meta_f = sm[:, 0:16].transpose(1, 0, 2).reshape(N_META, D)
    conv_w_f = sm[:, 16:20].transpose(1, 0, 2).reshape(4, D)
    hd4 = HEAD_DIM // 4
    ba_f = sm[:, 24].reshape(4, H, hd4).transpose(1, 0, 2).reshape(1, D)
    bx_f = sm[:, 32].reshape(4, H, hd4).transpose(1, 0, 2).reshape(1, D)
    vecs = jnp.zeros((16, D), F32)
    for r0, part in ((0, pool_scale), (1, conv_b), (2, ba_f), (3, bx_f), (4, lru_lambda), (5, conv_w_f)):
        vecs = lax.dynamic_update_slice(vecs, part, (r0, 0))

    def chipwise(g, n_blocks, rows):
        return g.reshape(n_blocks, 4, rows, g.shape[-1]).transpose(1, 0, 2, 3).reshape(4, n_blocks * rows, g.shape[-1])

    def split2(g):
        return g.reshape(4, 2, g.shape[1] // 2, g.shape[2])

    def presum(t, g, r):
        return chip_presum(g, r, c_arr, name="presum_" + names[t])

    def total(t, g, r, q):
        return final_half(g, r, q, mc_arr, name="sum_" + names[t])

    xs, target, gfin = x[0], loss_target[0], final_g.reshape(1, D)
    meta_tile = jnp.concatenate([jnp.zeros((ROW_TILE - N_META, D), F32), meta_f], axis=0)
    h0, u, u_t = rmsnorm_fwd_input(xs, meta_tile, norm1_g, name="norm1")
    proj, (b_win, b_pool, b_ga, b_gx, b_wout) = proj_with_gather(u, bufs[0:5], order_arr, name="proj")
    w_in_f = b_win.reshape(4, w_in.shape[1], w_in.shape[2])
    pool_f = b_pool.reshape(4, G, PG // 4, PG).transpose(1, 0, 2, 3).reshape(G, PG, PG)
    ga_f = b_ga.reshape(4, H, HEAD_DIM // 4, HEAD_DIM).transpose(1, 0, 2, 3).reshape(H, HEAD_DIM, HEAD_DIM)
    gx_f = b_gx.reshape(4, H, HEAD_DIM // 4, HEAD_DIM).transpose(1, 0, 2, 3).reshape(H, HEAD_DIM, HEAD_DIM)
    w_out_f = b_wout.reshape(4 * w_out.shape[1], w_out.shape[2])
    (merged, hs, saved), [[b_w1]] = mix_fwd(proj, pool_f, ga_f, gx_f, vecs, name="mix_fwd", stages=[stage_gather_ici([bufs[5]])])
    h1, [[b_w1]] = mm_nn(merged, w_out_f, out_dtype=F32, name="out_proj", epilogue=lambda r, res: r + res, extras=(h0,),
                         stages=[stage_gather_d2d([b_w1])])
    w1_f = b_w1.reshape(4, mlp_w1.shape[1], mlp_w1.shape[2])
    u2, u2_t = rmsnorm_fwd(h1, norm2_g, name="norm2")
    a1, [[b_w2]] = mm_nn(u2, w1_f, out_dtype=BF16, name="mlp_up", stages=[stage_gather_ici([bufs[6]])])
    [[b_w2]] = comm_call([stage_gather_d2d([b_w2])], name="w2_to_sibling")
    w2_f = b_w2.reshape(4 * mlp_w2.shape[1], mlp_w2.shape[2])
    Tp = h0.shape[0]
    h2 = mm_nn(a1, w2_f, out_dtype=F32, name="mlp_down", a_pro=_relu_sq, epilogue=lambda r, res: r + res, extras=(h1,),
               tiles=(_pick(Tp, (704, 512, 256, 128)), min(256, D), a1.shape[1]))
    dh2, dh2_bf, st_f = final_loss(h2, target, gfin, name="final_loss")

    da1 = mm_nt(dh2_bf, w2_f, out_dtype=BF16, name="mlp_down_dx",
                epilogue=lambda r, a: r * (2.0 * jnp.maximum(a.astype(F32), 0.0)), extras=(a1,))
    d_w2 = mm_tn(a1, dh2_bf, shards=1, out_dtype=BF16, name="mlp_down_dw", a_pro=_relu_sq, tiles=(512, min(1024, D), Tp))
    g6 = split2(d_w2.reshape(4, -1, d_w2.shape[-1]))
    d_w1, [[r6]] = mm_nn(u2_t, da1, out_dtype=BF16, name="mlp_up_dw", out_shards=4, tiles=(min(1024, D), None, Tp),
                         stages=[stage_to_sibling([g6])])
    g5 = split2(d_w1)
    p6 = presum(6, g6, r6)
    du2, [[q6_near], [r5]] = mm_nt(da1, w1_f, out_dtype=F32, name="mlp_up_dx",
                                  stages=[stage_to_chips([p6], peers=(0, 1)), stage_to_sibling([g5])])
    p5 = presum(5, g5, r5)
    dh1, dh1_bf, st_2 = rms_bwd(h1, norm2_g, du2, dh2, name="norm2_bwd")
    dmerged = mm_nt(dh1_bf, w_out_f, out_dtype=BF16, name="out_proj_dx")
    d_wout = mm_tn(merged, dh1_bf, shards=1, out_dtype=BF16, name="out_proj_dw", tiles=(512, min(1024, D), Tp))
    g4 = split2(d_wout.reshape(4, -1, d_wout.shape[-1]))
    (dproj, d_pool, d_ga, d_gx, d_vecs), [[q6], [q5], [r4]] = mix_bwd(
        proj, hs, saved, dmerged, pool_f, ga_f, gx_f, vecs, name="mix_bwd",
        stages=[stage_to_chips([p6], peers=(2,), into=[q6_near]), stage_to_chips([p5]), stage_to_sibling([g4])])
    p4 = presum(4, g4, r4)
    f5, f6 = total(5, g5, r5, q5), total(6, g6, r6, q6)
    g1, g2, g3 = split2(chipwise(d_pool, G, PG // 4)), split2(chipwise(d_ga, H, hd4)), split2(chipwise(d_gx, H, hd4))
    half_k = u_t.shape[0] // 2
    whole_k = (half_k, None, Tp)
    u_theirs = lax.dynamic_slice_in_dim(u_t, (1 - ac) * half_k, half_k, axis=0)
    u_mine = lax.dynamic_slice_in_dim(u_t, ac * half_k, half_k, axis=0)
    g_theirs, [[r1, r2, r3], [o5, o6], [q4]] = mm_nn(u_theirs, dproj, out_dtype=BF16, out_shards=4, tiles=whole_k, name="proj_dw_sibling_rows",
                                                    stages=[stage_to_sibling([g1, g2, g3]), stage_from_sibling([f5, f6]), stage_to_chips([p4])])
    p1, p2, p3 = presum(1, g1, r1), presum(2, g2, r2), presum(3, g3, r3)
    g_mine, [[r0], [q1, q2, q3]] = mm_nn(u_mine, dproj, out_dtype=BF16, out_shards=4, tiles=whole_k, name="proj_dw_own_rows",
                                         stages=[stage_from_sibling([g_theirs]), stage_to_chips([p1, p2, p3])])
    g0 = g_mine[:, None]
    p0 = chip_presum(g0, r0, jnp.zeros((1,), jnp.int32), name="presum_w_in")
    f4 = total(4, g4, r4, q4)
    du, [[q0], [o4]] = mm_nt(dproj, w_in_f, out_dtype=F32, name="proj_dx", tiles=(None, None, w_in_f.shape[2]),
                             stages=[stage_to_chips([p0]), stage_from_sibling([f4])])
    f0 = final_half(g0, r0, q0, jnp.stack([chip, 0]).astype(jnp.int32), name="sum_w_in")
    f1, f2, f3 = total(1, g1, r1, q1), total(2, g2, r2, q2), total(3, g3, r3, q3)
    grad_x, d_meta, st_1 = rms_bwd_input(h0, norm1_g, du, dh1, name="norm1_bwd")

    small = jnp.concatenate([d_meta, d_vecs, st_1, st_2, st_f], axis=0)
    tot, [[o0, o1, o2, o3]] = allgather8(small, name="sum_small", reduce_sum=True, stages=[stage_from_sibling([f0, f1, f2, f3])])
    mine = [f0, f1, f2, f3, f4, f5, f6]
    theirs = [o0, o1, o2, o3, o4, o5, o6]
    loss = jnp.sum(tot[49])
    g_meta = lax.dynamic_slice_in_dim(tot[0:16], chip * dshard, dshard, axis=1)
    g_pool_scale, g_conv_b, g_lam = tot[16:17], tot[17:18], tot[20:21]
    g_ba = lax.dynamic_slice_in_dim(tot[18].reshape(H, HEAD_DIM), chip * hd4, hd4, axis=1)[None]
    g_bx = lax.dynamic_slice_in_dim(tot[19].reshape(H, HEAD_DIM), chip * hd4, hd4, axis=1)[None]
    g_conv_w = lax.dynamic_slice_in_dim(tot[21:25], chip * dshard, dshard, axis=1)[None]
    g_n1, g_n2, g_fin = tot[32:33], tot[40:41], tot[48]

    def step(w, g, m, v, nm):
        cols = w.shape[-1]
        outs = adamw(w.reshape(-1, cols), g.reshape(-1, cols), m.reshape(-1, cols), v.reshape(-1, cols), name="adamw_" + nm)
        return [o.reshape(w.shape) for o in outs]

    def step_big(t, w, m, v):
        outs = adamw_halves(halves(w), mine[t], theirs[t], halves(m), halves(v), c_arr, name="adamw_" + names[t])
        return [o.reshape(w.shape) for o in outs]

    res = dict(meta_tokens=step(meta_tokens, g_meta, m_meta_tokens, v_meta_tokens, "meta_tokens"),
               norm1_g=step(norm1_g, g_n1, m_norm1_g, v_norm1_g, "norm1_g"),
               w_in=step_big(0, w_in, m_w_in, v_w_in), pool_w=step_big(1, pool_w, m_pool_w, v_pool_w),
               pool_scale=step(pool_scale, g_pool_scale, m_pool_scale, v_pool_scale, "pool_scale"),
               conv_w=step(conv_w, g_conv_w, m_conv_w, v_conv_w, "conv_w"), conv_b=step(conv_b, g_conv_b, m_conv_b, v_conv_b, "conv_b"),
               gate_a_w=step_big(2, gate_a_w, m_gate_a_w, v_gate_a_w), gate_a_b=step(gate_a_b, g_ba, m_gate_a_b, v_gate_a_b, "gate_a_b"),
               gate_x_w=step_big(3, gate_x_w, m_gate_x_w, v_gate_x_w), gate_x_b=step(gate_x_b, g_bx, m_gate_x_b, v_gate_x_b, "gate_x_b"),
               lru_lambda=step(lru_lambda, g_lam, m_lru_lambda, v_lru_lambda, "lru_lambda"), w_out=step_big(4, w_out, m_w_out, v_w_out),
               norm2_g=step(norm2_g, g_n2, m_norm2_g, v_norm2_g, "norm2_g"), mlp_w1=step_big(5, mlp_w1, m_mlp_w1, v_mlp_w1),
               mlp_w2=step_big(6, mlp_w2, m_mlp_w2, v_mlp_w2), final_g=step(final_g, g_fin, m_final_g, v_final_g, "final_g"))
    order = list(res)
    return (loss, grad_x[None], *[res[n][0] for n in order], *[res[n][1] for n in order], *[res[n][2] for n in order],
            *[res[n][3] for n in order])
```

```python
import functools

import jax
import jax.numpy as jnp
from jax import lax
from jax.experimental import pallas as pl
from jax.experimental.pallas import tpu as pltpu

F32 = jnp.float32
BF16 = jnp.bfloat16
MESH = pl.DeviceIdType.MESH

NORM_EPS = 1e-6
N_META = 16
HEAD_DIM = 256
POOL_WINDOWS = (2, 4, 8, 16)
LRU_C = 8.0
ROW_TILE = 128
HIST = 16
VMEM_LIMIT_BYTES = 56 * 1024 * 1024
ADAM_LR, ADAM_B1, ADAM_B2, ADAM_EPS, ADAM_WD, ADAM_STEP = 0.001, 0.9, 0.999, 1e-08, 0.01, 10


def _pick(n, prefs):
    for p in prefs:
        if n % p == 0:
            return p
    return n


def _params(*sem):
    return pltpu.CompilerParams(dimension_semantics=sem, vmem_limit_bytes=VMEM_LIMIT_BYTES)


def _sigmoid(x):
    return 1.0 / (1.0 + jnp.exp(-x))


def _gelu_tanh(x):
    t = jnp.tanh(0.7978845608028654 * (x + 0.044715 * (x * x * x)))
    return 0.5 * x * (1.0 + t), t


def _gelu_tanh_grad(x, t):
    return 0.5 * (1.0 + t) + 0.5 * x * (1.0 - t * t) * (0.7978845608028654 * (1.0 + 3.0 * 0.044715 * x * x))


def _neg_expm1(x, exp_x):
    series = x * (-1.0 + x * (-0.5 + x * ((-1.0 / 6.0) + x * ((-1.0 / 24.0) + x * (-1.0 / 120.0)))))
    return jnp.where(x > -0.125, series, 1.0 - exp_x)


def _softplus_neg(lam):
    z = jnp.exp(-jnp.abs(lam))
    log1p_z = jnp.where(z < 0.01, z * (1.0 - z * (0.5 - z * (1.0 / 3.0))), jnp.log(1.0 + z))
    return jnp.maximum(-lam, 0.0) + log1p_z


def _tile_masks(is_meta, rows):
    row = lax.broadcasted_iota(jnp.int32, (rows, 1), 0)
    valid = jnp.logical_or(jnp.logical_not(is_meta), row >= rows - N_META)
    t_log = jnp.where(is_meta, row - (rows - N_META), 1 << 20)
    return row, valid, t_log


def _window_count_inv(t_log, w):
    return 1.0 / jnp.clip(t_log + 1, 1, w).astype(F32)


def _dot_nt(a, b):
    return lax.dot_general(a, b, (((1,), (1,)), ((), ())), preferred_element_type=F32)


def _dot_tn(a, b):
    return lax.dot_general(a, b, (((0,), (0,)), ((), ())), preferred_element_type=F32)


def _place():
    x, y, c = lax.axis_index("x"), lax.axis_index("y"), lax.axis_index("c")
    chips = [(1 - x, y), (x, 1 - y), (1 - x, 1 - y)]
    return x, y, c, chips


def _remote(src, dst, send_sem, recv_sem, to):
    return pltpu.make_async_remote_copy(src_ref=src, dst_ref=dst, send_sem=send_sem, recv_sem=recv_sem, device_id=to,
                                        device_id_type=MESH)


class Stage:
    def __init__(self, arrays, out_shapes, aliases, n_copies, copies):
        self.arrays, self.out_shapes, self.aliases, self.n_copies, self.copies = list(arrays), list(out_shapes), aliases, n_copies, copies


def _sds(a):
    return jax.ShapeDtypeStruct(a.shape, a.dtype)


def _relay_rows(buf_ref, chip_idx, c, quarter):
    rows = buf_ref.shape[2] // 2
    return buf_ref.at[chip_idx, c, pl.ds(quarter * rows, rows)]


def stage_gather_direct(bufs):
    n = len(bufs)

    def copies(ins, outs, send, recv):
        x, y, c, chips = _place()
        me = 2 * x + y
        sends, recvs = [], []
        for t in range(n):
            for j, (px, py) in enumerate(chips[:2]):
                k = 2 * t + j
                mine, theirs = outs[t].at[me, c], outs[t].at[2 * px + py, c]
                sends.append(_remote(mine, mine, send.at[k], recv.at[k], (px, py, c)))
                recvs.append(_remote(theirs, theirs, send.at[k], recv.at[k], (px, py, c)))
        return sends, recvs

    return Stage(bufs, [_sds(b) for b in bufs], {t: t for t in range(n)}, 2 * n, copies)


def stage_gather_relay(bufs):
    n = len(bufs)

    def copies(ins, outs, send, recv):
        x, y, c, chips = _place()
        (xx, xy), (yx, yy), (dx, dy) = chips
        sends, recvs = [], []
        for t in range(n):
            from_y, from_x = _relay_rows(outs[t], 2 * yx + yy, c, 0), _relay_rows(outs[t], 2 * xx + xy, c, 1)
            sends.append(_remote(from_y, from_y, send.at[2 * t], recv.at[2 * t], (xx, xy, c)))
            sends.append(_remote(from_x, from_x, send.at[2 * t + 1], recv.at[2 * t + 1], (yx, yy, c)))
            for q, (px, py) in enumerate(chips[:2]):
                got = _relay_rows(outs[t], 2 * dx + dy, c, q)
                recvs.append(_remote(got, got, send.at[2 * t + q], recv.at[2 * t + q], (px, py, c)))
        return sends, recvs

    return Stage(bufs, [_sds(b) for b in bufs], {t: t for t in range(n)}, 2 * n, copies)


def stage_gather_d2d(bufs):
    n = len(bufs)

    def copies(ins, outs, send, recv):
        x, y, c, chips = _place()
        sends, recvs = [], []
        for t in range(n):
            for j, (px, py) in enumerate(chips):
                k = 3 * t + j
                got, sib = outs[t].at[2 * px + py, c], outs[t].at[2 * px + py, 1 - c]
                sends.append(_remote(got, got, send.at[k], recv.at[k], (x, y, 1 - c)))
                recvs.append(_remote(sib, sib, send.at[k], recv.at[k], (x, y, 1 - c)))
        return sends, recvs

    return Stage(bufs, [_sds(b) for b in bufs], {t: t for t in range(n)}, 3 * n, copies)


def stage_to_sibling(grads):
    n = len(grads)

    def copies(ins, outs, send, recv):
        x, y, c, _ = _place()
        sends, recvs = [], []
        for t in range(n):
            for k4 in range(4):
                k = 4 * t + k4
                sends.append(_remote(ins[t].at[k4, 1 - c], outs[t].at[k4], send.at[k], recv.at[k], (x, y, 1 - c)))
                recvs.append(_remote(outs[t].at[k4], outs[t].at[k4], send.at[k], recv.at[k], (x, y, 1 - c)))
        return sends, recvs

    return Stage(grads, [jax.ShapeDtypeStruct((4, *g.shape[2:]), g.dtype) for g in grads], {}, 4 * n, copies)


def stage_to_chips(presums, peers=(0, 1, 2), into=None):
    n = len(presums)

    def copies(ins, outs, send, recv):
        x, y, c, chips = _place()
        sends, recvs = [], []
        for t in range(n):
            for slot, j in enumerate(peers):
                px, py = chips[j]
                k = len(peers) * t + slot
                sends.append(_remote(ins[t].at[2 * px + py], outs[t].at[j], send.at[k], recv.at[k], (px, py, c)))
                recvs.append(_remote(outs[t].at[j], outs[t].at[j], send.at[k], recv.at[k], (px, py, c)))
        return sends, recvs

    out_shapes = [jax.ShapeDtypeStruct((3, *p.shape[1:]), p.dtype) for p in presums]
    if into is None:
        return Stage(presums, out_shapes, {}, len(peers) * n, copies)
    return Stage([*presums, *into], out_shapes, {n + t: t for t in range(n)}, len(peers) * n, copies)


def stage_from_sibling(halves):
    n = len(halves)

    def copies(ins, outs, send, recv):
        x, y, c, _ = _place()
        sends = [_remote(ins[t], outs[t], send.at[t], recv.at[t], (x, y, 1 - c)) for t in range(n)]
        recvs = [_remote(outs[t], outs[t], send.at[t], recv.at[t], (x, y, 1 - c)) for t in range(n)]
        return sends, recvs

    return Stage(halves, [_sds(h) for h in halves], {}, n, copies)


def _any_specs(n):
    return [pl.BlockSpec(memory_space=pl.ANY)] * n


def _staged_call(body, *, grid, in_specs, out_specs, out_shape, scratch_shapes, name, semantics, inputs, stages=()):
    n_in, n_out, n_scr = len(in_specs), len(out_specs), len(scratch_shapes)
    st_arrays = [a for s in stages for a in s.arrays]
    st_outs = [o for s in stages for o in s.out_shapes]
    st_sems = [pltpu.SemaphoreType.DMA((s.n_copies,)) for s in stages for _ in range(2)]
    aliases = {}
    at_in, at_out = n_in, n_out
    for s in stages:
        for a, o in s.aliases.items():
            aliases[at_in + a] = at_out + o
        at_in += len(s.arrays)
        at_out += len(s.out_shapes)

    def full_body(*refs):
        pos = 0

        def take(count):
            nonlocal pos
            part = refs[pos:pos + count]
            pos += count
            return part

        ins, s_ins, outs, s_outs, scr = take(n_in), take(len(st_arrays)), take(n_out), take(len(st_outs)), take(n_scr)
        s_sems = refs[pos:]

        def each_stage(action):
            at_i = at_o = 0
            for idx, s in enumerate(stages):
                sends, recvs = s.copies(s_ins[at_i:at_i + len(s.arrays)], s_outs[at_o:at_o + len(s.out_shapes)],
                                        s_sems[2 * idx], s_sems[2 * idx + 1])
                action(sends, recvs)
                at_i += len(s.arrays)
                at_o += len(s.out_shapes)

        if stages:
            ids = [pl.program_id(a) for a in range(len(grid))]
            first = functools.reduce(jnp.logical_and, [i == 0 for i in ids])
            last = functools.reduce(jnp.logical_and, [i == g - 1 for i, g in zip(ids, grid)])

            def start(sends, recvs):
                for cp in sends:
                    cp.start()

            def finish(sends, recvs):
                for cp in recvs:
                    cp.wait_recv()
                for cp in sends:
                    cp.wait_send()

            @pl.when(first)
            def _():
                each_stage(start)

        body(*ins, *outs, *scr)

        if stages:
            @pl.when(last)
            def _():
                each_stage(finish)

    sem = tuple("arbitrary" for _ in grid) if stages else tuple(semantics)
    res = pl.pallas_call(
        full_body, grid=grid, in_specs=[*in_specs, *_any_specs(len(st_arrays))], out_specs=[*out_specs, *_any_specs(len(st_outs))],
        out_shape=[*out_shape, *st_outs], scratch_shapes=[*scratch_shapes, *st_sems], input_output_aliases=aliases, name=name,
        compiler_params=_params(*sem))(*inputs, *st_arrays)
    outs, rest = list(res[:n_out]), list(res[n_out:])
    per_stage = []
    for s in stages:
        per_stage.append(rest[:len(s.out_shapes)])
        rest = rest[len(s.out_shapes):]
    return outs, per_stage


def comm_call(stages, *, name):
    return _staged_call(lambda: None, grid=(1,), in_specs=[], out_specs=[], out_shape=[], scratch_shapes=[], name=name,
                        semantics=("arbitrary",), inputs=[], stages=stages)[1]


def _matmul(kind, a, b, *, grid, a_spec, b_spec, out_spec, out_shape, acc_shape, name,
            a_pro=None, epilogue=None, extras=(), extra_specs=(), stages=()):
    nk = grid[2]
    n_extra = len(extras)

    def body(a_ref, b_ref, *rest):
        extra_refs = rest[:n_extra]
        o_ref = rest[n_extra]
        av = a_ref[...]
        if a_pro is not None:
            av = a_pro(av)
        av = av.astype(BF16)
        bv = b_ref[...].astype(BF16)
        if kind == "nn":
            p = jnp.dot(av, bv, preferred_element_type=F32)
        elif kind == "nt":
            p = _dot_nt(av, bv)
        else:
            p = _dot_tn(av, bv)

        def finish(r):
            if epilogue is not None:
                r = epilogue(r, *[e[...] for e in extra_refs])
            o_ref[...] = r.astype(o_ref.dtype)

        if nk == 1:
            finish(p)
        else:
            acc_ref = rest[n_extra + 1]
            k = pl.program_id(2)

            @pl.when(k == 0)
            def _():
                acc_ref[...] = p

            @pl.when(k > 0)
            def _():
                acc_ref[...] += p

            @pl.when(k == nk - 1)
            def _():
                finish(acc_ref[...])

    scratch = [] if nk == 1 else [pltpu.VMEM(acc_shape, F32)]
    outs, staged = _staged_call(body, grid=grid, in_specs=[a_spec, b_spec, *extra_specs], out_specs=[out_spec], out_shape=[out_shape],
                                scratch_shapes=scratch, name=name, semantics=("parallel", "parallel", "arbitrary"),
                                inputs=[a, b, *extras], stages=stages)
    return (outs[0], staged) if stages else outs[0]


def mm_nn(a, b, *, out_dtype, name, a_pro=None, epilogue=None, extras=(), stages=(), out_shards=1, tiles=(None, None, None)):
    M, K = a.shape
    sharded = b.ndim == 3
    ns = b.shape[2] if sharded else b.shape[1]
    N = ns * b.shape[0] if sharded else ns
    if out_shards > 1:
        ns = N // out_shards
    tm = tiles[0] or _pick(M, (1408, 1024, 512, 256, 128))
    tn = tiles[1] or _pick(ns, (512, 256, 128))
    tk = tiles[2] or _pick(K, (2048, 1408, 1024, 512, 256, 128))
    per = ns // tn
    if sharded:
        b_spec = pl.BlockSpec((None, tk, tn), lambda i, j, k: (j // per, k, j % per))
    else:
        b_spec = pl.BlockSpec((tk, tn), lambda i, j, k: (k, j))
    mn = pl.BlockSpec((tm, tn), lambda i, j, k: (i, j))
    if out_shards > 1:
        out_spec = pl.BlockSpec((None, tm, tn), lambda i, j, k: (j // per, i, j % per))
        out_shape = jax.ShapeDtypeStruct((out_shards, M, ns), out_dtype)
    else:
        out_spec, out_shape = mn, jax.ShapeDtypeStruct((M, N), out_dtype)
    return _matmul("nn", a, b, grid=(M // tm, N // tn, K // tk), a_spec=pl.BlockSpec((tm, tk), lambda i, j, k: (i, k)),
                   b_spec=b_spec, out_spec=out_spec, out_shape=out_shape, acc_shape=(tm, tn),
                   name=name, a_pro=a_pro, epilogue=epilogue, extras=extras, extra_specs=[mn] * len(extras), stages=stages)


def mm_nt(a, w, *, out_dtype, name, epilogue=None, extras=(), stages=(), tiles=(None, None, None)):
    M, N = a.shape
    sharded = w.ndim == 3
    kw = w.shape[1] if sharded else w.shape[0]
    ns = w.shape[2] if sharded else w.shape[1]
    tm = tiles[0] or _pick(M, (1408, 512, 256, 128))
    tkw = tiles[1] or _pick(kw, (1024, 512, 256, 128))
    tk = tiles[2] or _pick(ns, (2048, 1280, 1024, 512, 256, 128))
    per = ns // tk
    if sharded:
        w_spec = pl.BlockSpec((None, tkw, tk), lambda i, j, k: (k // per, j, k % per))
    else:
        w_spec = pl.BlockSpec((tkw, tk), lambda i, j, k: (j, k))
    mo = pl.BlockSpec((tm, tkw), lambda i, j, k: (i, j))
    return _matmul("nt", a, w, grid=(M // tm, kw // tkw, N // tk), a_spec=pl.BlockSpec((tm, tk), lambda i, j, k: (i, k)),
                   b_spec=w_spec, out_spec=mo, out_shape=jax.ShapeDtypeStruct((M, kw), out_dtype), acc_shape=(tm, tkw),
                   name=name, epilogue=epilogue, extras=extras, extra_specs=[mo] * len(extras), stages=stages)


def mm_tn(a, g, *, shards, name, out_dtype, a_pro=None, stages=(), tiles=(None, None, None)):
    T, kw = a.shape
    N = g.shape[1]
    ns = N // shards
    tkw = tiles[0] or _pick(kw, (1024, 512, 256, 128))
    tn = tiles[1] or _pick(ns, (1280, 1024, 512, 256, 128))
    tt = tiles[2] or _pick(T, (1408, 512, 256, 128))
    per = ns // tn
    if shards > 1:
        out_spec = pl.BlockSpec((None, tkw, tn), lambda i, j, k: (j // per, i, j % per))
        out_shape = jax.ShapeDtypeStruct((shards, kw, ns), out_dtype)
    else:
        out_spec = pl.BlockSpec((tkw, tn), lambda i, j, k: (i, j))
        out_shape = jax.ShapeDtypeStruct((kw, N), out_dtype)
    return _matmul("tn", a, g, grid=(kw // tkw, N // tn, T // tt), a_spec=pl.BlockSpec((tt, tkw), lambda i, j, k: (k, i)),
                   b_spec=pl.BlockSpec((tt, tn), lambda i, j, k: (k, j)), out_spec=out_spec, out_shape=out_shape,
                   acc_shape=(tkw, tn), name=name, a_pro=a_pro, stages=stages)


def _relu_sq(a):
    r = jnp.maximum(a, 0.0)
    return r * r


def rmsnorm_fwd(h, g, *, name, stages=()):
    Tp, D = h.shape
    tr = _pick(Tp, (384, 256, 128))

    def body(h_ref, g_ref, u_ref, ut_ref):
        x = h_ref[...]
        r = lax.rsqrt(jnp.mean(x * x, axis=-1, keepdims=True) + NORM_EPS)
        u = (x * r) * g_ref[...]
        u_ref[...] = u.astype(BF16)
        ut_ref[...] = u.T.astype(BF16)

    row = pl.BlockSpec((tr, D), lambda i: (i, 0))
    outs, staged = _staged_call(body, grid=(Tp // tr,), in_specs=[row, pl.BlockSpec((1, D), lambda i: (0, 0))],
                                out_specs=[row, pl.BlockSpec((D, tr), lambda i: (0, i))],
                                out_shape=[jax.ShapeDtypeStruct((Tp, D), BF16), jax.ShapeDtypeStruct((D, Tp), BF16)],
                                scratch_shapes=[], name=name, semantics=("parallel",), inputs=[h, g], stages=stages)
    return (outs, staged) if stages else outs


def rmsnorm_fwd_input(x, meta_tile, g, *, name):
    S, D = x.shape
    nt = S // ROW_TILE + 1

    def body(x_ref, mt_ref, g_ref, h_ref, u_ref, ut_ref):
        h = jnp.where(pl.program_id(0) == nt - 1, mt_ref[...], x_ref[...])
        r = lax.rsqrt(jnp.mean(h * h, axis=-1, keepdims=True) + NORM_EPS)
        u = (h * r) * g_ref[...]
        h_ref[...] = h
        u_ref[...] = u.astype(BF16)
        ut_ref[...] = u.T.astype(BF16)

    row = pl.BlockSpec((ROW_TILE, D), lambda i: (i, 0))
    return pl.pallas_call(
        body, grid=(nt,),
        in_specs=[pl.BlockSpec((ROW_TILE, D), lambda i: (jnp.minimum(i, nt - 2), 0)), pl.BlockSpec((ROW_TILE, D), lambda i: (0, 0)),
                  pl.BlockSpec((1, D), lambda i: (0, 0))],
        out_specs=[row, row, pl.BlockSpec((D, ROW_TILE), lambda i: (0, i))],
        out_shape=[jax.ShapeDtypeStruct((S + ROW_TILE, D), F32), jax.ShapeDtypeStruct((S + ROW_TILE, D), BF16),
                   jax.ShapeDtypeStruct((D, S + ROW_TILE), BF16)],
        name=name, compiler_params=_params("parallel"))(x, meta_tile, g)


def _rms_bwd_math(x, g, dy):
    r = lax.rsqrt(jnp.mean(x * x, axis=-1, keepdims=True) + NORM_EPS)
    xh = x * r
    dyg = dy * g
    dx = r * (dyg - xh * jnp.mean(dyg * xh, axis=-1, keepdims=True))
    return dx, xh


def final_loss(h2, target, gf, *, name):
    Tp, D = h2.shape
    nt = Tp // ROW_TILE

    def body(h_ref, t_ref, g_ref, dh_ref, dhb_ref, st_ref):
        i = pl.program_id(0)

        @pl.when(i == 0)
        def _():
            st_ref[...] = jnp.zeros_like(st_ref)

        x = h_ref[...]
        g = g_ref[...]
        r = lax.rsqrt(jnp.mean(x * x, axis=-1, keepdims=True) + NORM_EPS)
        xh = x * r
        err = jnp.where(i == nt - 1, 0.0, xh * g - t_ref[...])
        dout = err * (1.0 / D)
        dyg = dout * g
        dx = r * (dyg - xh * jnp.mean(dyg * xh, axis=-1, keepdims=True))
        dh_ref[...] = dx
        dhb_ref[...] = dx.astype(BF16)
        st_ref[0:1, :] += jnp.sum(dout * xh, axis=0, keepdims=True)
        st_ref[1:2, :] += jnp.sum(err * err, axis=0, keepdims=True) * (0.5 / D)

    row = pl.BlockSpec((ROW_TILE, D), lambda i: (i, 0))
    return pl.pallas_call(
        body, grid=(nt,),
        in_specs=[row, pl.BlockSpec((ROW_TILE, D), lambda i: (jnp.minimum(i, nt - 2), 0)), pl.BlockSpec((1, D), lambda i: (0, 0))],
        out_specs=[row, row, pl.BlockSpec((8, D), lambda i: (0, 0))],
        out_shape=[jax.ShapeDtypeStruct((Tp, D), F32), jax.ShapeDtypeStruct((Tp, D), BF16), jax.ShapeDtypeStruct((8, D), F32)],
        name=name, compiler_params=_params("arbitrary"))(h2, target, gf)


def rms_bwd(h, g, du, dres, *, name):
    Tp, D = h.shape
    tr = _pick(Tp, (384, 256, 128))

    def body(h_ref, g_ref, du_ref, dr_ref, dh_ref, dhb_ref, st_ref):
        @pl.when(pl.program_id(0) == 0)
        def _():
            st_ref[...] = jnp.zeros_like(st_ref)

        du_v = du_ref[...].astype(F32)
        dx, xh = _rms_bwd_math(h_ref[...], g_ref[...], du_v)
        dh = dr_ref[...] + dx
        dh_ref[...] = dh
        dhb_ref[...] = dh.astype(BF16)
        st_ref[0:1, :] += jnp.sum(du_v * xh, axis=0, keepdims=True)

    row = pl.BlockSpec((tr, D), lambda i: (i, 0))
    return pl.pallas_call(
        body, grid=(Tp // tr,), in_specs=[row, pl.BlockSpec((1, D), lambda i: (0, 0)), row, row],
        out_specs=[row, row, pl.BlockSpec((8, D), lambda i: (0, 0))],
        out_shape=[jax.ShapeDtypeStruct((Tp, D), F32), jax.ShapeDtypeStruct((Tp, D), BF16), jax.ShapeDtypeStruct((8, D), F32)],
        name=name, compiler_params=_params("arbitrary"))(h, g, du, dres)


def rms_bwd_input(h, g, du, dres, *, name, stages=()):
    Tp, D = h.shape
    nt = Tp // ROW_TILE

    def body(h_ref, g_ref, du_ref, dr_ref, gx_ref, gm_ref, st_ref):
        i = pl.program_id(0)

        @pl.when(i == 0)
        def _():
            st_ref[...] = jnp.zeros_like(st_ref)

        du_v = du_ref[...].astype(F32)
        dx, xh = _rms_bwd_math(h_ref[...], g_ref[...], du_v)
        dh = dr_ref[...] + dx
        st_ref[0:1, :] += jnp.sum(du_v * xh, axis=0, keepdims=True)

        @pl.when(i < nt - 1)
        def _():
            gx_ref[...] = dh

        @pl.when(i == nt - 1)
        def _():
            gm_ref[...] = dh[ROW_TILE - N_META:, :]

    row = pl.BlockSpec((ROW_TILE, D), lambda i: (i, 0))
    outs, staged = _staged_call(
        body, grid=(nt,), in_specs=[row, pl.BlockSpec((1, D), lambda i: (0, 0)), row, row],
        out_specs=[pl.BlockSpec((ROW_TILE, D), lambda i: (jnp.minimum(i, nt - 2), 0)), pl.BlockSpec((N_META, D), lambda i: (0, 0)),
                   pl.BlockSpec((8, D), lambda i: (0, 0))],
        out_shape=[jax.ShapeDtypeStruct((Tp - ROW_TILE, D), F32), jax.ShapeDtypeStruct((N_META, D), F32),
                   jax.ShapeDtypeStruct((8, D), F32)],
        scratch_shapes=[], name=name, semantics=("arbitrary",), inputs=[h, g, du, dres], stages=stages)
    return (outs, staged) if stages else outs


def _conv_shifts(ext):
    return tuple(pltpu.roll(ext, k, 0)[HIST:, :] for k in (1, 2, 3))


def _conv_taps(ext, cur, vec_ref, cs, rows):
    del rows
    x1, x2, x3 = _conv_shifts(ext)
    xc = (vec_ref[1:2, cs] + vec_ref[8:9, cs] * cur + vec_ref[7:8, cs] * x1 + vec_ref[6:7, cs] * x2 + vec_ref[5:6, cs] * x3)
    return xc, x1, x2, x3


def _window_sum_back(ext, w):
    s, sh = ext, 1
    while sh < w:
        s = s + pltpu.roll(s, sh, 0)
        sh *= 2
    return s[HIST:, :]


def _scan_rows(a, b, carry, *, reverse):
    rows = a.shape[0]
    rin = jnp.bitwise_and(lax.broadcasted_iota(jnp.int32, (rows, 1), 0), 7)
    sh = 1
    while sh < 8:
        keep = (rin < 8 - sh) if reverse else (rin >= sh)
        amount = rows - sh if reverse else sh
        a_sh = jnp.where(keep, pltpu.roll(a, amount, 0), 1.0)
        b_sh = jnp.where(keep, pltpu.roll(b, amount, 0), 0.0)
        b = b + a * b_sh
        a = a * a_sh
        sh *= 2
    out = [None] * (rows // 8)
    for g in (reversed(range(rows // 8)) if reverse else range(rows // 8)):
        hg = b[8 * g:8 * g + 8, :] + a[8 * g:8 * g + 8, :] * carry
        carry = hg[0:1, :] if reverse else hg[7:8, :]
        out[g] = hg
    return jnp.concatenate(out, axis=0)


def _lru_gates(xc, wa, wx, vec_ref, cs, sp):
    xcb = xc.astype(BF16)
    r = _sigmoid(jnp.dot(xcb, wa, preferred_element_type=F32) + vec_ref[2:3, cs])
    ig = _sigmoid(jnp.dot(xcb, wx, preferred_element_type=F32) + vec_ref[3:4, cs])
    log_a = (-LRU_C) * r * sp
    a = jnp.exp(log_a)
    a2 = a * a
    return xcb, r, ig, a, a2, _neg_expm1(2.0 * log_a, a2)


def mix_fwd(proj, pool_w, gate_a, gate_x, vecs, *, name, stages=()):
    Tp = proj.shape[0]
    D = proj.shape[1] // 5
    R = ROW_TILE
    nt = Tp // R
    H = D // HEAD_DIM
    PG = D // len(POOL_WINDOWS)

    def body(p_ref, pw_ref, wa_ref, wx_ref, vec_ref, m_ref, hs_ref, xc_ref, r_ref, ig_ref, a_ref, mu_ref, ge_ref, dge_ref,
             hist_p, hist_l, hcar, mtmp):
        i = pl.program_id(0)
        is_meta = i == 0

        @pl.when(is_meta)
        def _():
            hist_p[...] = jnp.zeros_like(hist_p)
            hist_l[...] = jnp.zeros_like(hist_l)
            hcar[...] = jnp.zeros_like(hcar)

        row, valid, t_log = _tile_masks(is_meta, R)

        for g, w in enumerate(POOL_WINDOWS):
            cs = slice(g * PG, (g + 1) * PG)
            v = p_ref[:, g * PG:(g + 1) * PG].astype(F32)
            ws = _window_sum_back(jnp.concatenate([hist_p[:, cs], v], axis=0), w)
            d = ws * _window_count_inv(t_log, w) - v
            y = jnp.dot(d.astype(BF16), pw_ref[g], preferred_element_type=F32)
            gp = p_ref[:, 3 * D + g * PG:3 * D + (g + 1) * PG].astype(F32)
            mtmp[:, cs] = _sigmoid(gp) * (y * vec_ref[0:1, cs])
            hist_p[:, cs] = v[R - HIST:, :]

        for h in range(H):
            cs = slice(h * HEAD_DIM, (h + 1) * HEAD_DIM)
            vl = p_ref[:, D + h * HEAD_DIM:D + (h + 1) * HEAD_DIM].astype(F32)
            xc, _, _, _ = _conv_taps(jnp.concatenate([hist_l[:, cs], vl], axis=0), vl, vec_ref, cs, R)
            sp = _softplus_neg(vec_ref[4:5, cs])
            xcb, r, ig, a, _, em = _lru_gates(xc, wa_ref[h], wx_ref[h], vec_ref, cs, sp)
            mult = jnp.sqrt(em)
            b = jnp.where(valid, mult * (ig * xc), 0.0)
            hs = _scan_rows(a, b, hcar[7:8, cs], reverse=False)
            hs_ref[:, cs] = hs
            xc_ref[:, cs], r_ref[:, cs], ig_ref[:, cs] = xcb, r.astype(BF16), ig.astype(BF16)
            a_ref[:, cs], mu_ref[:, cs] = a, mult
            hcar[:, cs] = hs[R - 8:, :]
            hist_l[:, cs] = vl[R - HIST:, :]
            vg = p_ref[:, 2 * D + h * HEAD_DIM:2 * D + (h + 1) * HEAD_DIM].astype(F32)
            ge, th = _gelu_tanh(vg)
            ge_ref[:, cs], dge_ref[:, cs] = ge.astype(BF16), _gelu_tanh_grad(vg, th).astype(BF16)
            gl = p_ref[:, 4 * D + h * HEAD_DIM:4 * D + (h + 1) * HEAD_DIM].astype(F32)
            m_ref[:, cs] = (mtmp[:, cs] + _sigmoid(gl) * (hs * ge)).astype(BF16)

    def tile(i):
        return (i + nt - 1) % nt

    full = lambda shape: pl.BlockSpec(shape, lambda i: (0,) * len(shape))
    outs, staged = _staged_call(
        body, grid=(nt,),
        in_specs=[pl.BlockSpec((R, 5 * D), lambda i: (tile(i), 0)), full(pool_w.shape), full(gate_a.shape), full(gate_x.shape),
                  full(vecs.shape)],
        out_specs=[pl.BlockSpec((R, D), lambda i: (tile(i), 0))] * 9,
        out_shape=[jax.ShapeDtypeStruct((Tp, D), dt) for dt in (BF16, F32, BF16, BF16, BF16, F32, F32, BF16, BF16)],
        scratch_shapes=[pltpu.VMEM((HIST, D), F32), pltpu.VMEM((HIST, D), F32), pltpu.VMEM((8, D), F32), pltpu.VMEM((R, D), F32)],
        name=name, semantics=("arbitrary",), inputs=[proj, pool_w, gate_a, gate_x, vecs], stages=stages)
    outs = [outs[0], outs[1], outs[2:]]
    return (outs, staged) if stages else outs


def mix_bwd(proj, hs, saved, dmerged, pool_w, gate_a, gate_x, vecs, *, name, stages=()):
    Tp = proj.shape[0]
    D = proj.shape[1] // 5
    R = ROW_TILE
    nt = Tp // R
    H = D // HEAD_DIM
    PG = D // len(POOL_WINDOWS)

    def body(p_ref, pprev_ref, hs_ref, hprev_ref, dm_ref, xc_ref, r_ref, ig_ref, a_ref, mu_ref, ge_ref, dge_ref, pw_ref, wa_ref, wx_ref, vec_ref,
             dp_ref, dpw_ref, dwa_ref, dwx_ref, dvec_ref, car_g, fut_dxc, fut_q):
        i = pl.program_id(0)
        is_meta = i == nt - 1

        @pl.when(i == 0)
        def _():
            dpw_ref[...] = jnp.zeros_like(dpw_ref)
            dwa_ref[...] = jnp.zeros_like(dwa_ref)
            dwx_ref[...] = jnp.zeros_like(dwx_ref)
            dvec_ref[...] = jnp.zeros_like(dvec_ref)
            car_g[...] = jnp.zeros_like(car_g)
            fut_dxc[...] = jnp.zeros_like(fut_dxc)
            fut_q[...] = jnp.zeros_like(fut_q)

        row, valid, t_log = _tile_masks(is_meta, R)
        keep_prev = jnp.logical_not(is_meta)

        def colsum(x):
            return jnp.sum(x, axis=0, keepdims=True)

        for g, w in enumerate(POOL_WINDOWS):
            cs = slice(g * PG, (g + 1) * PG)
            v = p_ref[:, g * PG:(g + 1) * PG].astype(F32)
            vprev = jnp.where(keep_prev, pprev_ref[:, g * PG:(g + 1) * PG].astype(F32), 0.0)
            inv_cnt = _window_count_inv(t_log, w)
            d = _window_sum_back(jnp.concatenate([vprev, v], axis=0), w) * inv_cnt - v
            d_bf = d.astype(BF16)
            y = jnp.dot(d_bf, pw_ref[g], preferred_element_type=F32)
            scale = vec_ref[0:1, cs]
            sg = _sigmoid(p_ref[:, 3 * D + g * PG:3 * D + (g + 1) * PG].astype(F32))
            dm = dm_ref[:, cs].astype(F32)
            dpo = dm * sg
            dp_ref[:, 3 * D + g * PG:3 * D + (g + 1) * PG] = (dm * (y * scale) * sg * (1.0 - sg)).astype(BF16)
            dvec_ref[0:1, cs] += colsum(dpo * y)
            dy = (dpo * scale).astype(BF16)
            dd = _dot_nt(dy, pw_ref[g])
            dpw_ref[g] += _dot_tn(d_bf, dy)
            q = dd * inv_cnt
            s, sh = jnp.concatenate([q, fut_q[:, cs]], axis=0), 1
            while sh < w:
                s = s + pltpu.roll(s, R + HIST - sh, 0)
                sh *= 2
            dp_ref[:, g * PG:(g + 1) * PG] = (s[:R, :] - dd).astype(BF16)
            fut_q[:, cs] = q[:HIST, :]

        for h in range(H):
            cs = slice(h * HEAD_DIM, (h + 1) * HEAD_DIM)
            pc = lambda blk: slice(blk * D + h * HEAD_DIM, blk * D + (h + 1) * HEAD_DIM)
            vl = p_ref[:, pc(1)].astype(F32)
            vlprev = jnp.where(keep_prev, pprev_ref[:, pc(1)].astype(F32), 0.0)
            x1, x2, x3 = _conv_shifts(jnp.concatenate([vlprev, vl], axis=0))
            lam = vec_ref[4:5, cs]
            sp = _softplus_neg(lam)
            xcb = xc_ref[:, cs]
            xc, r, ig = xcb.astype(F32), r_ref[:, cs].astype(F32), ig_ref[:, cs].astype(F32)
            a, mult = a_ref[:, cs], mu_ref[:, cs]
            a2 = a * a
            inv_mult = 1.0 / mult
            hsv = hs_ref[:, cs]
            hprev = jnp.where(row >= 1, pltpu.roll(hsv, 1, 0), hprev_ref[HIST - 1:HIST, cs])
            ge = ge_ref[:, cs].astype(F32)
            sgl = _sigmoid(p_ref[:, pc(4)].astype(F32))
            dm = dm_ref[:, cs].astype(F32)
            dlo = dm * sgl
            dp_ref[:, pc(4)] = (dm * (hsv * ge) * sgl * (1.0 - sgl)).astype(BF16)
            dp_ref[:, pc(2)] = (dlo * hsv * dge_ref[:, cs].astype(F32)).astype(BF16)
            a_next = jnp.where(row < R - 1, pltpu.roll(a, R - 1, 0), 1.0)
            G = _scan_rows(a_next, dlo * ge, car_g[0:1, cs], reverse=True)
            car_g[:, cs] = (a * G)[0:8, :]
            da = jnp.where(valid, G * hprev, 0.0)
            db = jnp.where(valid, G, 0.0)
            dmult = db * (ig * xc)
            dig = db * (mult * xc)
            dxc = db * (mult * ig)
            dlog_a = da * a - dmult * (a2 * inv_mult)
            dvec_ref[4:5, cs] += colsum(dlog_a * r) * (-LRU_C)
            dr = dlog_a * ((-LRU_C) * sp)
            dpa = dr * r * (1.0 - r)
            dpx = dig * ig * (1.0 - ig)
            dpa_bf = dpa.astype(BF16)
            dpx_bf = dpx.astype(BF16)
            dwa_ref[h] += _dot_tn(xcb, dpa_bf)
            dwx_ref[h] += _dot_tn(xcb, dpx_bf)
            dvec_ref[2:3, cs] += colsum(dpa)
            dvec_ref[3:4, cs] += colsum(dpx)
            dxc = dxc + _dot_nt(dpa_bf, wa_ref[h]) + _dot_nt(dpx_bf, wx_ref[h])
            ext = jnp.concatenate([dxc, fut_dxc[:, cs]], axis=0)
            n = R + HIST
            dvl = (vec_ref[8:9, cs] * dxc + vec_ref[7:8, cs] * pltpu.roll(ext, n - 1, 0)[:R, :]
                   + vec_ref[6:7, cs] * pltpu.roll(ext, n - 2, 0)[:R, :] + vec_ref[5:6, cs] * pltpu.roll(ext, n - 3, 0)[:R, :])
            dp_ref[:, pc(1)] = dvl.astype(BF16)
            dvec_ref[1:2, cs] += colsum(dxc)
            dvec_ref[8:9, cs] += colsum(dxc * vl)
            dvec_ref[7:8, cs] += colsum(dxc * x1)
            dvec_ref[6:7, cs] += colsum(dxc * x2)
            dvec_ref[5:6, cs] += colsum(dxc * x3)
            fut_dxc[:, cs] = dxc[:HIST, :]

            @pl.when(is_meta)
            def _():
                dvec_ref[4:5, cs] = dvec_ref[4:5, cs] * (-_sigmoid(-lam))

    def tile(i):
        return (2 * nt - 2 - i) % nt

    def prev_blk(i):
        per = R // HIST
        return jnp.where(i == nt - 1, 0, jnp.where(i == nt - 2, Tp // HIST - 1, (nt - 2 - i) * per - 1))

    full = lambda shape: pl.BlockSpec(shape, lambda i: (0,) * len(shape))
    G_ = len(POOL_WINDOWS)
    outs, staged = _staged_call(
        body, grid=(nt,),
        in_specs=[pl.BlockSpec((R, 5 * D), lambda i: (tile(i), 0)), pl.BlockSpec((HIST, 5 * D), lambda i: (prev_blk(i), 0)),
                  pl.BlockSpec((R, D), lambda i: (tile(i), 0)), pl.BlockSpec((HIST, D), lambda i: (prev_blk(i), 0)),
                  *[pl.BlockSpec((R, D), lambda i: (tile(i), 0))] * 8,
                  full(pool_w.shape), full(gate_a.shape), full(gate_x.shape), full(vecs.shape)],
        out_specs=[pl.BlockSpec((R, 5 * D), lambda i: (tile(i), 0)), full((G_, PG, PG)), full((H, HEAD_DIM, HEAD_DIM)),
                   full((H, HEAD_DIM, HEAD_DIM)), full((16, D))],
        out_shape=[jax.ShapeDtypeStruct((Tp, 5 * D), BF16), jax.ShapeDtypeStruct((G_, PG, PG), F32),
                   jax.ShapeDtypeStruct((H, HEAD_DIM, HEAD_DIM), F32), jax.ShapeDtypeStruct((H, HEAD_DIM, HEAD_DIM), F32),
                   jax.ShapeDtypeStruct((16, D), F32)],
        scratch_shapes=[pltpu.VMEM((8, D), F32), pltpu.VMEM((HIST, D), F32), pltpu.VMEM((HIST, D), F32)],
        name=name, semantics=("arbitrary",), inputs=[proj, proj, hs, hs, dmerged, *saved, pool_w, gate_a, gate_x, vecs], stages=stages)
    return (outs, staged) if stages else outs


def _adamw_math(w, g, m, v):
    mn = ADAM_B1 * m + (1.0 - ADAM_B1) * g
    vn = ADAM_B2 * v + (1.0 - ADAM_B2) * (g * g)
    m_hat = mn / (1.0 - ADAM_B1 ** ADAM_STEP)
    v_hat = vn / (1.0 - ADAM_B2 ** ADAM_STEP)
    return -ADAM_LR * (m_hat / (jnp.sqrt(v_hat) + ADAM_EPS) + ADAM_WD * w), mn, vn


def adamw(w, g, m, v, *, name):
    rows, cols = w.shape
    tr = _pick(rows, (256, 128, 64, 32, 16, 8))

    def body(w_ref, g_ref, m_ref, v_ref, go_ref, d_ref, mo_ref, vo_ref):
        gv = g_ref[...]
        go_ref[...] = gv
        d_ref[...], mo_ref[...], vo_ref[...] = _adamw_math(w_ref[...], gv, m_ref[...], v_ref[...])

    blk = pl.BlockSpec((tr, cols), lambda i: (i, 0))
    sds = jax.ShapeDtypeStruct((rows, cols), F32)
    return pl.pallas_call(body, grid=(rows // tr,), in_specs=[blk] * 4, out_specs=[blk] * 4, out_shape=[sds] * 4, name=name,
                          compiler_params=_params("parallel"))(w, g, m, v)


def allgather8(block, *, name, reduce_sum=False, stages=()):
    rows, cols = block.shape

    def body(x_ref, out_ref, *scratch):
        if reduce_sum:
            buf, send_sems, recv_sems, local_sem = scratch
        else:
            buf = out_ref
            send_sems, recv_sems, local_sem = scratch
        x, y, c, chips = _place()
        me, sibling = (x, y, c), (x, y, 1 - c)

        def slot(px, py, pc):
            return buf.at[4 * px + 2 * py + pc]

        def copy(k, blk, to, src=None):
            return pltpu.make_async_remote_copy(src_ref=slot(*blk) if src is None else src, dst_ref=slot(*blk),
                                                send_sem=send_sems.at[k], recv_sem=recv_sems.at[k], device_id=to, device_id_type=MESH)

        mine = pltpu.make_async_copy(x_ref, slot(*me), local_sem)
        mine.start()
        first = [copy(0, me, sibling, src=x_ref)]
        first += [copy(1 + j, me, (*chip, c), src=x_ref) for j, chip in enumerate(chips)]
        for cp in first:
            cp.start()
        passed = [copy(4 + j, (*chip, c), sibling) for j, chip in enumerate(chips)]
        for j, chip in enumerate(chips):
            copy(1 + j, (*chip, c), me).wait_recv()
            passed[j].start()
        copy(0, sibling, me).wait_recv()
        for j, chip in enumerate(chips):
            copy(4 + j, (*chip, 1 - c), me).wait_recv()
        for cp in first + passed:
            cp.wait_send()
        mine.wait()
        if reduce_sum:
            acc = buf[0]
            for d in range(1, 8):
                acc = acc + buf[d]
            out_ref[...] = acc

    sems = [pltpu.SemaphoreType.DMA((7,)), pltpu.SemaphoreType.DMA((7,)), pltpu.SemaphoreType.DMA]
    if reduce_sum:
        out_shape = jax.ShapeDtypeStruct((rows, cols), block.dtype)
        scratch = [pltpu.VMEM((8, rows, cols), block.dtype)] + sems
    else:
        out_shape = jax.ShapeDtypeStruct((8, rows, cols), block.dtype)
        scratch = sems
    vmem = pl.BlockSpec(memory_space=pltpu.VMEM)
    outs, staged = _staged_call(body, grid=(1,), in_specs=[vmem], out_specs=[vmem], out_shape=[out_shape], scratch_shapes=scratch,
                                name=name, semantics=("arbitrary",), inputs=[block], stages=stages)
    return (outs[0], staged) if stages else outs[0]


def cast_into_slot(w, chip_arr, *, name):
    _, r, cols = w.shape
    tr = _pick(r, (256, 128, 64, 32, 16))

    def body(chip_ref, w_ref, o_ref):
        del chip_ref
        o_ref[...] = w_ref[...].astype(BF16)

    grid_spec = pltpu.PrefetchScalarGridSpec(
        num_scalar_prefetch=1, grid=(2, r // tr),
        in_specs=[pl.BlockSpec((None, tr, cols), lambda h, i, chip: (h, i, 0))],
        out_specs=pl.BlockSpec((None, None, tr, cols), lambda h, i, chip: (chip[0], h, i, 0)))
    return pl.pallas_call(body, grid_spec=grid_spec, out_shape=jax.ShapeDtypeStruct((4, 2, r, cols), BF16), name=name,
                          compiler_params=_params("parallel", "parallel"))(chip_arr, w)


def proj_with_gather(u, bufs, order_arr, *, name):
    Tp, K = u.shape
    n = len(bufs)
    Ns = bufs[0].shape[3]
    tm = _pick(Tp, (1408, 512, 256, 128))
    tc = _pick(Ns, (512, 256, 128))
    n_rows = Tp // tm

    def body(order_ref, u_ref, *refs):
        del order_ref
        o_ref, outs = refs[n], refs[n + 1:2 * n + 1]
        wbuf, dir_send, dir_recv, rel_send, rel_recv, d2d_send, d2d_recv, load_sems = refs[2 * n + 1:]
        s, i = pl.program_id(0), pl.program_id(1)
        x, y, c, chips = _place()
        me = 2 * x + y
        (dx, dy) = chips[2]

        def direct(t, j, landing):
            px, py = chips[j]
            slot = outs[t].at[2 * px + py, c] if landing else outs[t].at[me, c]
            return _remote(slot, slot, dir_send.at[t, j], dir_recv.at[t, j], (px, py, c))

        def relay(t, j, landing):
            px, py = chips[j]
            ox, oy = chips[1 - j]
            rows = _relay_rows(outs[t], 2 * dx + dy, c, j) if landing else _relay_rows(outs[t], 2 * ox + oy, c, j)
            return _remote(rows, rows, rel_send.at[t, j], rel_recv.at[t, j], (px, py, c))

        def d2d(t, j, landing):
            px, py = chips[j]
            slot = outs[t].at[2 * px + py, 1 - c] if landing else outs[t].at[2 * px + py, c]
            return _remote(slot, slot, d2d_send.at[t, j], d2d_recv.at[t, j], (x, y, 1 - c))

        def load(chip_idx):
            parts = [pltpu.make_async_copy(outs[0].at[chip_idx, hh], wbuf.at[pl.ds(hh * (K // 2), K // 2), :], load_sems.at[hh])
                     for hh in range(2)]
            for cp in parts:
                cp.start()
            for cp in parts:
                cp.wait()

        def chip_of(j):
            px, py = chips[j]
            return 2 * px + py

        @pl.when(jnp.logical_and(s == 0, i == 0))
        def _():
            for t in range(n):
                for j in range(2):
                    direct(t, j, False).start()
            load(me)

        @pl.when(jnp.logical_and(s == 1, i == 0))
        def _():
            for j in range(2):
                direct(0, j, True).wait_recv()
            for j in range(2):
                relay(0, j, False).start()
                d2d(0, j, False).start()
            d2d(0, 0, True).wait_recv()
            load(chip_of(0))

        @pl.when(jnp.logical_and(s == 2, i == 0))
        def _():
            d2d(0, 1, True).wait_recv()
            for t in range(1, n):
                for j in range(2):
                    direct(t, j, True).wait_recv()
                for j in range(2):
                    relay(t, j, False).start()
                    d2d(t, j, False).start()
            load(chip_of(1))

        @pl.when(jnp.logical_and(s == 3, i == 0))
        def _():
            for j in range(2):
                relay(0, j, True).wait_recv()
            d2d(0, 2, False).start()
            d2d(0, 2, True).wait_recv()
            load(chip_of(2))

        uv = u_ref[...]
        for cc in range(Ns // tc):
            o_ref[:, cc * tc:(cc + 1) * tc] = jnp.dot(uv, wbuf[:, cc * tc:(cc + 1) * tc], preferred_element_type=F32).astype(BF16)

        @pl.when(jnp.logical_and(s == 3, i == n_rows - 1))
        def _():
            for t in range(1, n):
                for j in range(2):
                    relay(t, j, True).wait_recv()
                d2d(t, 2, False).start()
            for t in range(1, n):
                for j in range(3):
                    d2d(t, j, True).wait_recv()
            for t in range(n):
                for j in range(2):
                    direct(t, j, False).wait_send()
                    relay(t, j, False).wait_send()
                for j in range(3):
                    d2d(t, j, False).wait_send()

    grid_spec = pltpu.PrefetchScalarGridSpec(
        num_scalar_prefetch=1, grid=(4, n_rows),
        in_specs=[pl.BlockSpec((tm, K), lambda s, i, order: (i, 0)), *_any_specs(n)],
        out_specs=[pl.BlockSpec((tm, Ns), lambda s, i, order: (i, order[s])), *_any_specs(n)],
        scratch_shapes=[pltpu.VMEM((K, Ns), BF16), pltpu.SemaphoreType.DMA((n, 2)), pltpu.SemaphoreType.DMA((n, 2)),
                        pltpu.SemaphoreType.DMA((n, 2)), pltpu.SemaphoreType.DMA((n, 2)),
                        pltpu.SemaphoreType.DMA((n, 3)), pltpu.SemaphoreType.DMA((n, 3)), pltpu.SemaphoreType.DMA((2,))])
    res = pl.pallas_call(body, grid_spec=grid_spec, out_shape=[jax.ShapeDtypeStruct((Tp, 4 * Ns), BF16), *[_sds(b) for b in bufs]],
                         input_output_aliases={2 + t: 1 + t for t in range(n)}, name=name,
                         compiler_params=_params("arbitrary", "arbitrary"))(order_arr, u, *bufs)
    return res[0], list(res[1:])


def chip_presum(grad, recv, c_arr, *, name):
    _, _, r, cols = grad.shape
    tr = _pick(r, (256, 128, 64, 32, 16))

    def body(c_ref, g_ref, r_ref, o_ref):
        del c_ref
        o_ref[...] = (g_ref[...].astype(F32) + r_ref[...].astype(F32)).astype(BF16)

    grid_spec = pltpu.PrefetchScalarGridSpec(
        num_scalar_prefetch=1, grid=(4, r // tr),
        in_specs=[pl.BlockSpec((None, None, tr, cols), lambda k, i, c_ref: (k, c_ref[0], i, 0)),
                  pl.BlockSpec((None, tr, cols), lambda k, i, c_ref: (k, i, 0))],
        out_specs=pl.BlockSpec((None, tr, cols), lambda k, i, c_ref: (k, i, 0)))
    return pl.pallas_call(body, grid_spec=grid_spec, out_shape=jax.ShapeDtypeStruct((4, r, cols), BF16), name=name,
                          compiler_params=_params("parallel", "parallel"))(c_arr, grad, recv)


def final_half(grad, recv, got, mc_arr, *, name):
    _, _, r, cols = grad.shape
    tr = _pick(r, (256, 128, 64, 32, 16))

    def body(mc_ref, g_ref, r_ref, q_ref, o_ref):
        del mc_ref
        acc = g_ref[...].astype(F32) + r_ref[...].astype(F32)
        for j in range(3):
            acc = acc + q_ref[j].astype(F32)
        o_ref[...] = acc

    grid_spec = pltpu.PrefetchScalarGridSpec(
        num_scalar_prefetch=1, grid=(r // tr,),
        in_specs=[pl.BlockSpec((None, None, tr, cols), lambda i, mc: (mc[0], mc[1], i, 0)),
                  pl.BlockSpec((None, tr, cols), lambda i, mc: (mc[0], i, 0)),
                  pl.BlockSpec((3, tr, cols), lambda i, mc: (0, i, 0))],
        out_specs=pl.BlockSpec((tr, cols), lambda i, mc: (i, 0)))
    return pl.pallas_call(body, grid_spec=grid_spec, out_shape=jax.ShapeDtypeStruct((r, cols), F32), name=name,
                          compiler_params=_params("parallel"))(mc_arr, grad, recv, got)


def adamw_halves(w, mine, theirs, m, v, c_arr, *, name):
    _, r, cols = w.shape
    tr = _pick(r, (256, 128, 64, 32, 16, 8))

    def body(c_ref, w_ref, mine_ref, theirs_ref, m_ref, v_ref, go_ref, d_ref, mo_ref, vo_ref):
        gv = jnp.where(pl.program_id(0) == c_ref[0], mine_ref[...], theirs_ref[...])
        go_ref[...] = gv
        d_ref[...], mo_ref[...], vo_ref[...] = _adamw_math(w_ref[...], gv, m_ref[...], v_ref[...])

    blk = pl.BlockSpec((None, tr, cols), lambda h, i, c_ref: (h, i, 0))
    grid_spec = pltpu.PrefetchScalarGridSpec(
        num_scalar_prefetch=1, grid=(2, r // tr),
        in_specs=[blk, pl.BlockSpec((tr, cols), lambda h, i, c_ref: (jnp.where(h == c_ref[0], i, 0), 0)),
                  pl.BlockSpec((tr, cols), lambda h, i, c_ref: (jnp.where(h == c_ref[0], 0, i), 0)), blk, blk],
        out_specs=[blk] * 4)
    sds = jax.ShapeDtypeStruct((2, r, cols), F32)
    return pl.pallas_call(body, grid_spec=grid_spec, out_shape=[sds] * 4, name=name,
                          compiler_params=_params("parallel", "parallel"))(c_arr, w, mine, theirs, m, v)


def _pad_rows(a, rows):
    return jnp.pad(a, ((0, rows - a.shape[0]), (0, 0)))


def kernel(x, meta_tokens, norm1_g, w_in, pool_w, pool_scale, conv_w, conv_b, gate_a_w, gate_a_b, gate_x_w, gate_x_b, lru_lambda, w_out, norm2_g, mlp_w1, mlp_w2, final_g, loss_target, m_meta_tokens, m_norm1_g, m_w_in, m_pool_w, m_pool_scale, m_conv_w, m_conv_b, m_gate_a_w, m_gate_a_b, m_gate_x_w, m_gate_x_b, m_lru_lambda, m_w_out, m_norm2_g, m_mlp_w1, m_mlp_w2, m_final_g, v_meta_tokens, v_norm1_g, v_w_in, v_pool_w, v_pool_scale, v_conv_w, v_conv_b, v_gate_a_w, v_gate_a_b, v_gate_x_w, v_gate_x_b, v_lru_lambda, v_w_out, v_norm2_g, v_mlp_w1, v_mlp_w2, v_final_g):
    D = x.shape[-1]
    H = D // HEAD_DIM
    G = len(POOL_WINDOWS)
    PG = D // G
    ax, ay, ac = lax.axis_index("x"), lax.axis_index("y"), lax.axis_index("c")
    chip = 2 * ax + ay
    dshard = D // 4

    c_arr = jnp.reshape(ac, (1,)).astype(jnp.int32)
    chip_arr = jnp.reshape(chip, (1,)).astype(jnp.int32)
    mc_arr = jnp.stack([chip, ac]).astype(jnp.int32)
    names = ["w_in", "pool_w", "gate_a_w", "gate_x_w", "w_out", "mlp_w1", "mlp_w2"]
    big = [w_in, pool_w, gate_a_w, gate_x_w, w_out, mlp_w1, mlp_w2]

    def halves(w):
        w2d = w.reshape(-1, w.shape[-1])
        return w2d.reshape(2, w2d.shape[0] // 2, w2d.shape[1])

    bufs = [cast_into_slot(halves(w), chip_arr, name="cast_" + nm) for w, nm in zip(big, names)]
    order_arr = jnp.stack([chip, 2 * (1 - ax) + ay, 2 * ax + (1 - ay), 2 * (1 - ax) + (1 - ay)]).astype(jnp.int32)

    small_in = jnp.concatenate([meta_tokens, _pad_rows(conv_w[0], 8), _pad_rows(gate_a_b.reshape(1, dshard), 8),
                                _pad_rows(gate_x_b.reshape(1, dshard), 8)], axis=0)
    sm = allgather8(small_in, name="gather_small")[0::2]
    meta_f = sm[:, 0:16].transpose(1, 0, 2).reshape(N_META, D)
    conv_w_f = sm[:, 16:20].transpose(1, 0, 2).reshape(4, D)
    hd4 = HEAD_DIM // 4
    ba_f = sm[:, 24].reshape(4, H, hd4).transpose(1, 0, 2).reshape(1, D)
    bx_f = sm[:, 32].reshape(4, H, hd4).transpose(1, 0, 2).reshape(1, D)
    vecs = jnp.zeros((16, D), F32)
    for r0, part in ((0, pool_scale), (1, conv_b), (2, ba_f), (3, bx_f), (4, lru_lambda), (5, conv_w_f)):
        vecs = lax.dynamic_update_slice(vecs, part, (r0, 0))

    def chipwise(g, n_blocks, rows):
        return g.reshape(n_blocks, 4, rows, g.shape[-1]).transpose(1, 0, 2, 3).reshape(4, n_blocks * rows, g.shape[-1])

    def split2(g):
        return g.reshape(4, 2, g.shape[1] // 2, g.shape[2])

    def presum(t, g, r):
        return chip_presum(g, r, c_arr, name="presum_" + names[t])

    def total(t, g, r, q):
        return final_half(g, r, q, mc_arr, name="sum_" + names[t])

    xs, target, gfin = x[0], loss_target[0], final_g.reshape(1, D)
    meta_tile = jnp.concatenate([jnp.zeros((ROW_TILE - N_META, D), F32), meta_f], axis=0)
    h0, u, u_t = rmsnorm_fwd_input(xs, meta_tile, norm1_g, name="norm1")
    proj, (b_win, b_pool, b_ga, b_gx, b_wout) = proj_with_gather(u, bufs[0:5], order_arr, name="proj")
    w_in_f = b_win.reshape(4, w_in.shape[1], w_in.shape[2])
    pool_f = b_pool.reshape(4, G, PG // 4, PG).transpose(1, 0, 2, 3).reshape(G, PG, PG)
    ga_f = b_ga.reshape(4, H, HEAD_DIM // 4, HEAD_DIM).transpose(1, 0, 2, 3).reshape(H, HEAD_DIM, HEAD_DIM)
    gx_f = b_gx.reshape(4, H, HEAD_DIM // 4, HEAD_DIM).transpose(1, 0, 2, 3).reshape(H, HEAD_DIM, HEAD_DIM)
    w_out_f = b_wout.reshape(4 * w_out.shape[1], w_out.shape[2])
    (merged, hs, saved), [[b_w1, b_w2]] = mix_fwd(proj, pool_f, ga_f, gx_f, vecs, name="mix_fwd",
                                                  stages=[stage_gather_direct([bufs[5], bufs[6]])])
    h1, [[b_w1]] = mm_nn(merged, w_out_f, out_dtype=F32, name="out_proj", epilogue=lambda r, res: r + res, extras=(h0,),
                         stages=[stage_gather_relay([b_w1])])
    (u2, u2_t), [[b_w1]] = rmsnorm_fwd(h1, norm2_g, name="norm2", stages=[stage_gather_d2d([b_w1])])
    w1_f = b_w1.reshape(4, mlp_w1.shape[1], mlp_w1.shape[2])
    a1, [[b_w2]] = mm_nn(u2, w1_f, out_dtype=BF16, name="mlp_up", stages=[stage_gather_relay([b_w2])])
    [[b_w2]] = comm_call([stage_gather_d2d([b_w2])], name="w2_to_sibling")
    w2_f = b_w2.reshape(4 * mlp_w2.shape[1], mlp_w2.shape[2])
    Tp = h0.shape[0]
    h2 = mm_nn(a1, w2_f, out_dtype=F32, name="mlp_down", a_pro=_relu_sq, epilogue=lambda r, res: r + res, extras=(h1,),
               tiles=(_pick(Tp, (704, 512, 256, 128)), min(256, D), a1.shape[1]))
    dh2, dh2_bf, st_f = final_loss(h2, target, gfin, name="final_loss")

    da1 = mm_nt(dh2_bf, w2_f, out_dtype=BF16, name="mlp_down_dx",
                epilogue=lambda r, a: r * (2.0 * jnp.maximum(a.astype(F32), 0.0)), extras=(a1,))
    d_w2 = mm_tn(a1, dh2_bf, shards=1, out_dtype=BF16, name="mlp_down_dw", a_pro=_relu_sq, tiles=(512, min(1024, D), Tp))
    g6 = split2(d_w2.reshape(4, -1, d_w2.shape[-1]))
    d_w1, [[r6]] = mm_nn(u2_t, da1, out_dtype=BF16, name="mlp_up_dw", out_shards=4, tiles=(min(1024, D), None, Tp),
                         stages=[stage_to_sibling([g6])])
    g5 = split2(d_w1)
    p6 = presum(6, g6, r6)
    du2, [[q6_near], [r5]] = mm_nt(da1, w1_f, out_dtype=F32, name="mlp_up_dx",
                                  stages=[stage_to_chips([p6], peers=(0, 1)), stage_to_sibling([g5])])
    p5 = presum(5, g5, r5)
    dh1, dh1_bf, st_2 = rms_bwd(h1, norm2_g, du2, dh2, name="norm2_bwd")
    dmerged = mm_nt(dh1_bf, w_out_f, out_dtype=BF16, name="out_proj_dx")
    d_wout = mm_tn(merged, dh1_bf, shards=1, out_dtype=BF16, name="out_proj_dw", tiles=(512, min(1024, D), Tp))
    g4 = split2(d_wout.reshape(4, -1, d_wout.shape[-1]))
    (dproj, d_pool, d_ga, d_gx, d_vecs), [[q6], [q5], [r4]] = mix_bwd(
        proj, hs, saved, dmerged, pool_f, ga_f, gx_f, vecs, name="mix_bwd",
        stages=[stage_to_chips([p6], peers=(2,), into=[q6_near]), stage_to_chips([p5]), stage_to_sibling([g4])])
    p4 = presum(4, g4, r4)
    f5, f6 = total(5, g5, r5, q5), total(6, g6, r6, q6)
    g1, g2, g3 = split2(chipwise(d_pool, G, PG // 4)), split2(chipwise(d_ga, H, hd4)), split2(chipwise(d_gx, H, hd4))
    half_k = u_t.shape[0] // 2
    whole_k = (half_k, None, Tp)
    u_theirs = lax.dynamic_slice_in_dim(u_t, (1 - ac) * half_k, half_k, axis=0)
    u_mine = lax.dynamic_slice_in_dim(u_t, ac * half_k, half_k, axis=0)
    g_theirs, [[r1, r2, r3], [o5, o6], [q4]] = mm_nn(u_theirs, dproj, out_dtype=BF16, out_shards=4, tiles=whole_k, name="proj_dw_sibling_rows",
                                                    stages=[stage_to_sibling([g1, g2, g3]), stage_from_sibling([f5, f6]), stage_to_chips([p4])])
    p1, p2, p3 = presum(1, g1, r1), presum(2, g2, r2), presum(3, g3, r3)
    g_mine, [[r0], [q1, q2, q3]] = mm_nn(u_mine, dproj, out_dtype=BF16, out_shards=4, tiles=whole_k, name="proj_dw_own_rows",
                                         stages=[stage_from_sibling([g_theirs]), stage_to_chips([p1, p2, p3])])
    g0 = g_mine[:, None]
    p0 = chip_presum(g0, r0, jnp.zeros((1,), jnp.int32), name="presum_w_in")
    f4 = total(4, g4, r4, q4)
    du, [[q0], [o4]] = mm_nt(dproj, w_in_f, out_dtype=F32, name="proj_dx", tiles=(None, None, w_in_f.shape[2]),
                             stages=[stage_to_chips([p0]), stage_from_sibling([f4])])
    f0 = final_half(g0, r0, q0, jnp.stack([chip, 0]).astype(jnp.int32), name="sum_w_in")
    f1, f2, f3 = total(1, g1, r1, q1), total(2, g2, r2, q2), total(3, g3, r3, q3)
    grad_x, d_meta, st_1 = rms_bwd_input(h0, norm1_g, du, dh1, name="norm1_bwd")

    small = jnp.concatenate([d_meta, d_vecs, st_1, st_2, st_f], axis=0)
    tot, [[o0, o1, o2, o3]] = allgather8(small, name="sum_small", reduce_sum=True, stages=[stage_from_sibling([f0, f1, f2, f3])])
    mine = [f0, f1, f2, f3, f4, f5, f6]
    theirs = [o0, o1, o2, o3, o4, o5, o6]
    loss = jnp.sum(tot[49])
    g_meta = lax.dynamic_slice_in_dim(tot[0:16], chip * dshard, dshard, axis=1)
    g_pool_scale, g_conv_b, g_lam = tot[16:17], tot[17:18], tot[20:21]
    g_ba = lax.dynamic_slice_in_dim(tot[18].reshape(H, HEAD_DIM), chip * hd4, hd4, axis=1)[None]
    g_bx = lax.dynamic_slice_in_dim(tot[19].reshape(H, HEAD_DIM), chip * hd4, hd4, axis=1)[None]
    g_conv_w = lax.dynamic_slice_in_dim(tot[21:25], chip * dshard, dshard, axis=1)[None]
    g_n1, g_n2, g_fin = tot[32:33], tot[40:41], tot[48]

    def step(w, g, m, v, nm):
        cols = w.shape[-1]
        outs = adamw(w.reshape(-1, cols), g.reshape(-1, cols), m.reshape(-1, cols), v.reshape(-1, cols), name="adamw_" + nm)
        return [o.reshape(w.shape) for o in outs]

    def step_big(t, w, m, v):
        outs = adamw_halves(halves(w), mine[t], theirs[t], halves(m), halves(v), c_arr, name="adamw_" + names[t])
        return [o.reshape(w.shape) for o in outs]

    res = dict(meta_tokens=step(meta_tokens, g_meta, m_meta_tokens, v_meta_tokens, "meta_tokens"),
               norm1_g=step(norm1_g, g_n1, m_norm1_g, v_norm1_g, "norm1_g"),
               w_in=step_big(0, w_in, m_w_in, v_w_in), pool_w=step_big(1, pool_w, m_pool_w, v_pool_w),
               pool_scale=step(pool_scale, g_pool_scale, m_pool_scale, v_pool_scale, "pool_scale"),
               conv_w=step(conv_w, g_conv_w, m_conv_w, v_conv_w, "conv_w"), conv_b=step(conv_b, g_conv_b, m_conv_b, v_conv_b, "conv_b"),
               gate_a_w=step_big(2, gate_a_w, m_gate_a_w, v_gate_a_w), gate_a_b=step(gate_a_b, g_ba, m_gate_a_b, v_gate_a_b, "gate_a_b"),
               gate_x_w=step_big(3, gate_x_w, m_gate_x_w, v_gate_x_w), gate_x_b=step(gate_x_b, g_bx, m_gate_x_b, v_gate_x_b, "gate_x_b"),
               lru_lambda=step(lru_lambda, g_lam, m_lru_lambda, v_lru_lambda, "lru_lambda"), w_out=step_big(4, w_out, m_w_out, v_w_out),
               norm2_g=step(norm2_g, g_n2, m_norm2_g, v_norm2_g, "norm2_g"), mlp_w1=step_big(5, mlp_w1, m_mlp_w1, v_mlp_w1),
               mlp_w2=step_big(6, mlp_w2, m_mlp_w2, v_mlp_w2), final_g=step(final_g, g_fin, m_final_g, v_final_g, "final_g"))
    order = list(res)
    return (loss, grad_x[None], *[res[n][0] for n in order], *[res[n][1] for n in order], *[res[n][2] for n in order],
            *[res[n][3] for n in order])
```

```python
import functools

import jax
import jax.numpy as jnp
from jax import lax
from jax.experimental import pallas as pl
from jax.experimental.pallas import tpu as pltpu

F32 = jnp.float32
BF16 = jnp.bfloat16
MESH = pl.DeviceIdType.MESH

NORM_EPS = 1e-6
N_META = 16
HEAD_DIM = 256
POOL_WINDOWS = (2, 4, 8, 16)
LRU_C = 8.0
ROW_TILE = 128
HIST = 16
VMEM_LIMIT_BYTES = 56 * 1024 * 1024
ADAM_LR, ADAM_B1, ADAM_B2, ADAM_EPS, ADAM_WD, ADAM_STEP = 0.001, 0.9, 0.999, 1e-08, 0.01, 10


def _pick(n, prefs):
    for p in prefs:
        if n % p == 0:
            return p
    return n


def _params(*sem):
    return pltpu.CompilerParams(dimension_semantics=sem, vmem_limit_bytes=VMEM_LIMIT_BYTES)


def _sigmoid(x):
    return 1.0 / (1.0 + jnp.exp(-x))


def _gelu_tanh(x):
    t = jnp.tanh(0.7978845608028654 * (x + 0.044715 * (x * x * x)))
    return 0.5 * x * (1.0 + t), t


def _gelu_tanh_grad(x, t):
    return 0.5 * (1.0 + t) + 0.5 * x * (1.0 - t * t) * (0.7978845608028654 * (1.0 + 3.0 * 0.044715 * x * x))


def _neg_expm1(x, exp_x):
    series = x * (-1.0 + x * (-0.5 + x * ((-1.0 / 6.0) + x * ((-1.0 / 24.0) + x * (-1.0 / 120.0)))))
    return jnp.where(x > -0.125, series, 1.0 - exp_x)


def _softplus_neg(lam):
    z = jnp.exp(-jnp.abs(lam))
    log1p_z = jnp.where(z < 0.01, z * (1.0 - z * (0.5 - z * (1.0 / 3.0))), jnp.log(1.0 + z))
    return jnp.maximum(-lam, 0.0) + log1p_z


def _tile_masks(is_meta, rows):
    row = lax.broadcasted_iota(jnp.int32, (rows, 1), 0)
    valid = jnp.logical_or(jnp.logical_not(is_meta), row >= rows - N_META)
    t_log = jnp.where(is_meta, row - (rows - N_META), 1 << 20)
    return row, valid, t_log


def _window_count_inv(t_log, w):
    return 1.0 / jnp.clip(t_log + 1, 1, w).astype(F32)


def _dot_nt(a, b):
    return lax.dot_general(a, b, (((1,), (1,)), ((), ())), preferred_element_type=F32)


def _dot_tn(a, b):
    return lax.dot_general(a, b, (((0,), (0,)), ((), ())), preferred_element_type=F32)


def _place():
    x, y, c = lax.axis_index("x"), lax.axis_index("y"), lax.axis_index("c")
    chips = [(1 - x, y), (x, 1 - y), (1 - x, 1 - y)]
    return x, y, c, chips


def _remote(src, dst, send_sem, recv_sem, to):
    return pltpu.make_async_remote_copy(src_ref=src, dst_ref=dst, send_sem=send_sem, recv_sem=recv_sem, device_id=to,
                                        device_id_type=MESH)


class Stage:
    def __init__(self, arrays, out_shapes, aliases, n_copies, copies):
        self.arrays, self.out_shapes, self.aliases, self.n_copies, self.copies = list(arrays), list(out_shapes), aliases, n_copies, copies


def _sds(a):
    return jax.ShapeDtypeStruct(a.shape, a.dtype)


def _relay_rows(buf_ref, chip_idx, c, quarter):
    rows = buf_ref.shape[2] // 2
    return buf_ref.at[chip_idx, c, pl.ds(quarter * rows, rows)]


def stage_gather_direct(bufs):
    n = len(bufs)

    def copies(ins, outs, send, recv):
        x, y, c, chips = _place()
        me = 2 * x + y
        sends, recvs = [], []
        for t in range(n):
            for j, (px, py) in enumerate(chips[:2]):
                k = 2 * t + j
                mine, theirs = outs[t].at[me, c], outs[t].at[2 * px + py, c]
                sends.append(_remote(mine, mine, send.at[k], recv.at[k], (px, py, c)))
                recvs.append(_remote(theirs, theirs, send.at[k], recv.at[k], (px, py, c)))
        return sends, recvs

    return Stage(bufs, [_sds(b) for b in bufs], {t: t for t in range(n)}, 2 * n, copies)


def stage_gather_relay(bufs):
    n = len(bufs)

    def copies(ins, outs, send, recv):
        x, y, c, chips = _place()
        (xx, xy), (yx, yy), (dx, dy) = chips
        sends, recvs = [], []
        for t in range(n):
            from_y, from_x = _relay_rows(outs[t], 2 * yx + yy, c, 0), _relay_rows(outs[t], 2 * xx + xy, c, 1)
            sends.append(_remote(from_y, from_y, send.at[2 * t], recv.at[2 * t], (xx, xy, c)))
            sends.append(_remote(from_x, from_x, send.at[2 * t + 1], recv.at[2 * t + 1], (yx, yy, c)))
            for q, (px, py) in enumerate(chips[:2]):
                got = _relay_rows(outs[t], 2 * dx + dy, c, q)
                recvs.append(_remote(got, got, send.at[2 * t + q], recv.at[2 * t + q], (px, py, c)))
        return sends, recvs

    return Stage(bufs, [_sds(b) for b in bufs], {t: t for t in range(n)}, 2 * n, copies)


def stage_gather_d2d(bufs):
    n = len(bufs)

    def copies(ins, outs, send, recv):
        x, y, c, chips = _place()
        sends, recvs = [], []
        for t in range(n):
            for j, (px, py) in enumerate(chips):
                k = 3 * t + j
                got, sib = outs[t].at[2 * px + py, c], outs[t].at[2 * px + py, 1 - c]
                sends.append(_remote(got, got, send.at[k], recv.at[k], (x, y, 1 - c)))
                recvs.append(_remote(sib, sib, send.at[k], recv.at[k], (x, y, 1 - c)))
        return sends, recvs

    return Stage(bufs, [_sds(b) for b in bufs], {t: t for t in range(n)}, 3 * n, copies)


def stage_to_sibling(grads):
    n = len(grads)

    def copies(ins, outs, send, recv):
        x, y, c, _ = _place()
        sends, recvs = [], []
        for t in range(n):
            for k4 in range(4):
                k = 4 * t + k4
                sends.append(_remote(ins[t].at[k4, 1 - c], outs[t].at[k4], send.at[k], recv.at[k], (x, y, 1 - c)))
                recvs.append(_remote(outs[t].at[k4], outs[t].at[k4], send.at[k], recv.at[k], (x, y, 1 - c)))
        return sends, recvs

    return Stage(grads, [jax.ShapeDtypeStruct((4, *g.shape[2:]), g.dtype) for g in grads], {}, 4 * n, copies)


def stage_to_chips(presums, peers=(0, 1, 2), into=None):
    n = len(presums)

    def copies(ins, outs, send, recv):
        x, y, c, chips = _place()
        sends, recvs = [], []
        for t in range(n):
            for slot, j in enumerate(peers):
                px, py = chips[j]
                k = len(peers) * t + slot
                sends.append(_remote(ins[t].at[2 * px + py], outs[t].at[j], send.at[k], recv.at[k], (px, py, c)))
                recvs.append(_remote(outs[t].at[j], outs[t].at[j], send.at[k], recv.at[k], (px, py, c)))
        return sends, recvs

    out_shapes = [jax.ShapeDtypeStruct((3, *p.shape[1:]), p.dtype) for p in presums]
    if into is None:
        return Stage(presums, out_shapes, {}, len(peers) * n, copies)
    return Stage([*presums, *into], out_shapes, {n + t: t for t in range(n)}, len(peers) * n, copies)


def stage_from_sibling(halves):
    n = len(halves)

    def copies(ins, outs, send, recv):
        x, y, c, _ = _place()
        sends = [_remote(ins[t], outs[t], send.at[t], recv.at[t], (x, y, 1 - c)) for t in range(n)]
        recvs = [_remote(outs[t], outs[t], send.at[t], recv.at[t], (x, y, 1 - c)) for t in range(n)]
        return sends, recvs

    return Stage(halves, [_sds(h) for h in halves], {}, n, copies)


def _any_specs(n):
    return [pl.BlockSpec(memory_space=pl.ANY)] * n


def _staged_call(body, *, grid, in_specs, out_specs, out_shape, scratch_shapes, name, semantics, inputs, stages=()):
    n_in, n_out, n_scr = len(in_specs), len(out_specs), len(scratch_shapes)
    st_arrays = [a for s in stages for a in s.arrays]
    st_outs = [o for s in stages for o in s.out_shapes]
    st_sems = [pltpu.SemaphoreType.DMA((s.n_copies,)) for s in stages for _ in range(2)]
    aliases = {}
    at_in, at_out = n_in, n_out
    for s in stages:
        for a, o in s.aliases.items():
            aliases[at_in + a] = at_out + o
        at_in += len(s.arrays)
        at_out += len(s.out_shapes)

    def full_body(*refs):
        pos = 0

        def take(count):
            nonlocal pos
            part = refs[pos:pos + count]
            pos += count
            return part

        ins, s_ins, outs, s_outs, scr = take(n_in), take(len(st_arrays)), take(n_out), take(len(st_outs)), take(n_scr)
        s_sems = refs[pos:]

        def each_stage(action):
            at_i = at_o = 0
            for idx, s in enumerate(stages):
                sends, recvs = s.copies(s_ins[at_i:at_i + len(s.arrays)], s_outs[at_o:at_o + len(s.out_shapes)],
                                        s_sems[2 * idx], s_sems[2 * idx + 1])
                action(sends, recvs)
                at_i += len(s.arrays)
                at_o += len(s.out_shapes)

        if stages:
            ids = [pl.program_id(a) for a in range(len(grid))]
            first = functools.reduce(jnp.logical_and, [i == 0 for i in ids])
            last = functools.reduce(jnp.logical_and, [i == g - 1 for i, g in zip(ids, grid)])

            def start(sends, recvs):
                for cp in sends:
                    cp.start()

            def finish(sends, recvs):
                for cp in recvs:
                    cp.wait_recv()
                for cp in sends:
                    cp.wait_send()

            @pl.when(first)
            def _():
                each_stage(start)

        body(*ins, *outs, *scr)

        if stages:
            @pl.when(last)
            def _():
                each_stage(finish)

    sem = tuple("arbitrary" for _ in grid) if stages else tuple(semantics)
    res = pl.pallas_call(
        full_body, grid=grid, in_specs=[*in_specs, *_any_specs(len(st_arrays))], out_specs=[*out_specs, *_any_specs(len(st_outs))],
        out_shape=[*out_shape, *st_outs], scratch_shapes=[*scratch_shapes, *st_sems], input_output_aliases=aliases, name=name,
        compiler_params=_params(*sem))(*inputs, *st_arrays)
    outs, rest = list(res[:n_out]), list(res[n_out:])
    per_stage = []
    for s in stages:
        per_stage.append(rest[:len(s.out_shapes)])
        rest = rest[len(s.out_shapes):]
    return outs, per_stage


def comm_call(stages, *, name):
    return _staged_call(lambda: None, grid=(1,), in_specs=[], out_specs=[], out_shape=[], scratch_shapes=[], name=name,
                        semantics=("arbitrary",), inputs=[], stages=stages)[1]


def _matmul(kind, a, b, *, grid, a_spec, b_spec, out_spec, out_shape, acc_shape, name,
            a_pro=None, epilogue=None, extras=(), extra_specs=(), stages=()):
    nk = grid[2]
    n_extra = len(extras)

    def body(a_ref, b_ref, *rest):
        extra_refs = rest[:n_extra]
        o_ref = rest[n_extra]
        av = a_ref[...]
        if a_pro is not None:
            av = a_pro(av)
        av = av.astype(BF16)
        bv = b_ref[...].astype(BF16)
        if kind == "nn":
            p = jnp.dot(av, bv, preferred_element_type=F32)
        elif kind == "nt":
            p = _dot_nt(av, bv)
        else:
            p = _dot_tn(av, bv)

        def finish(r):
            if epilogue is not None:
                r = epilogue(r, *[e[...] for e in extra_refs])
            o_ref[...] = r.astype(o_ref.dtype)

        if nk == 1:
            finish(p)
        else:
            acc_ref = rest[n_extra + 1]
            k = pl.program_id(2)

            @pl.when(k == 0)
            def _():
                acc_ref[...] = p

            @pl.when(k > 0)
            def _():
                acc_ref[...] += p

            @pl.when(k == nk - 1)
            def _():
                finish(acc_ref[...])

    scratch = [] if nk == 1 else [pltpu.VMEM(acc_shape, F32)]
    outs, staged = _staged_call(body, grid=grid, in_specs=[a_spec, b_spec, *extra_specs], out_specs=[out_spec], out_shape=[out_shape],
                                scratch_shapes=scratch, name=name, semantics=("parallel", "parallel", "arbitrary"),
                                inputs=[a, b, *extras], stages=stages)
    return (outs[0], staged) if stages else outs[0]


def mm_nn(a, b, *, out_dtype, name, a_pro=None, epilogue=None, extras=(), stages=(), out_shards=1, tiles=(None, None, None)):
    M, K = a.shape
    sharded = b.ndim == 3
    ns = b.shape[2] if sharded else b.shape[1]
    N = ns * b.shape[0] if sharded else ns
    if out_shards > 1:
        ns = N // out_shards
    tm = tiles[0] or _pick(M, (1408, 1024, 512, 256, 128))
    tn = tiles[1] or _pick(ns, (512, 256, 128))
    tk = tiles[2] or _pick(K, (2048, 1408, 1024, 512, 256, 128))
    per = ns // tn
    if sharded:
        b_spec = pl.BlockSpec((None, tk, tn), lambda i, j, k: (j // per, k, j % per))
    else:
        b_spec = pl.BlockSpec((tk, tn), lambda i, j, k: (k, j))
    mn = pl.BlockSpec((tm, tn), lambda i, j, k: (i, j))
    if out_shards > 1:
        out_spec = pl.BlockSpec((None, tm, tn), lambda i, j, k: (j // per, i, j % per))
        out_shape = jax.ShapeDtypeStruct((out_shards, M, ns), out_dtype)
    else:
        out_spec, out_shape = mn, jax.ShapeDtypeStruct((M, N), out_dtype)
    return _matmul("nn", a, b, grid=(M // tm, N // tn, K // tk), a_spec=pl.BlockSpec((tm, tk), lambda i, j, k: (i, k)),
                   b_spec=b_spec, out_spec=out_spec, out_shape=out_shape, acc_shape=(tm, tn),
                   name=name, a_pro=a_pro, epilogue=epilogue, extras=extras, extra_specs=[mn] * len(extras), stages=stages)


def mm_nt(a, w, *, out_dtype, name, epilogue=None, extras=(), stages=(), tiles=(None, None, None)):
    M, N = a.shape
    sharded = w.ndim == 3
    kw = w.shape[1] if sharded else w.shape[0]
    ns = w.shape[2] if sharded else w.shape[1]
    tm = tiles[0] or _pick(M, (1408, 512, 256, 128))
    tkw = tiles[1] or _pick(kw, (1024, 512, 256, 128))
    tk = tiles[2] or _pick(ns, (2048, 1280, 1024, 512, 256, 128))
    per = ns // tk
    if sharded:
        w_spec = pl.BlockSpec((None, tkw, tk), lambda i, j, k: (k // per, j, k % per))
    else:
        w_spec = pl.BlockSpec((tkw, tk), lambda i, j, k: (j, k))
    mo = pl.BlockSpec((tm, tkw), lambda i, j, k: (i, j))
    return _matmul("nt", a, w, grid=(M // tm, kw // tkw, N // tk), a_spec=pl.BlockSpec((tm, tk), lambda i, j, k: (i, k)),
                   b_spec=w_spec, out_spec=mo, out_shape=jax.ShapeDtypeStruct((M, kw), out_dtype), acc_shape=(tm, tkw),
                   name=name, epilogue=epilogue, extras=extras, extra_specs=[mo] * len(extras), stages=stages)


def mm_tn(a, g, *, shards, name, out_dtype, a_pro=None, stages=(), tiles=(None, None, None)):
    T, kw = a.shape
    N = g.shape[1]
    ns = N // shards
    tkw = tiles[0] or _pick(kw, (1024, 512, 256, 128))
    tn = tiles[1] or _pick(ns, (1280, 1024, 512, 256, 128))
    tt = tiles[2] or _pick(T, (1408, 512, 256, 128))
    per = ns // tn
    if shards > 1:
        out_spec = pl.BlockSpec((None, tkw, tn), lambda i, j, k: (j // per, i, j % per))
        out_shape = jax.ShapeDtypeStruct((shards, kw, ns), out_dtype)
    else:
        out_spec = pl.BlockSpec((tkw, tn), lambda i, j, k: (i, j))
        out_shape = jax.ShapeDtypeStruct((kw, N), out_dtype)
    return _matmul("tn", a, g, grid=(kw // tkw, N // tn, T // tt), a_spec=pl.BlockSpec((tt, tkw), lambda i, j, k: (k, i)),
                   b_spec=pl.BlockSpec((tt, tn), lambda i, j, k: (k, j)), out_spec=out_spec, out_shape=out_shape,
                   acc_shape=(tkw, tn), name=name, a_pro=a_pro, stages=stages)


def _relu_sq(a):
    r = jnp.maximum(a, 0.0)
    return r * r


def rmsnorm_fwd(h, g, *, name, stages=()):
    Tp, D = h.shape
    tr = _pick(Tp, (384, 256, 128))

    def body(h_ref, g_ref, u_ref, ut_ref):
        x = h_ref[...]
        r = lax.rsqrt(jnp.mean(x * x, axis=-1, keepdims=True) + NORM_EPS)
        u = (x * r) * g_ref[...]
        u_ref[...] = u.astype(BF16)
        ut_ref[...] = u.T.astype(BF16)

    row = pl.BlockSpec((tr, D), lambda i: (i, 0))
    outs, staged = _staged_call(body, grid=(Tp // tr,), in_specs=[row, pl.BlockSpec((1, D), lambda i: (0, 0))],
                                out_specs=[row, pl.BlockSpec((D, tr), lambda i: (0, i))],
                                out_shape=[jax.ShapeDtypeStruct((Tp, D), BF16), jax.ShapeDtypeStruct((D, Tp), BF16)],
                                scratch_shapes=[], name=name, semantics=("parallel",), inputs=[h, g], stages=stages)
    return (outs, staged) if stages else outs


def rmsnorm_fwd_input(x, meta_tile, g, *, name):
    S, D = x.shape
    nt = S // ROW_TILE + 1

    def body(x_ref, mt_ref, g_ref, h_ref, u_ref, ut_ref):
        h = jnp.where(pl.program_id(0) == nt - 1, mt_ref[...], x_ref[...])
        r = lax.rsqrt(jnp.mean(h * h, axis=-1, keepdims=True) + NORM_EPS)
        u = (h * r) * g_ref[...]
        h_ref[...] = h
        u_ref[...] = u.astype(BF16)
        ut_ref[...] = u.T.astype(BF16)

    row = pl.BlockSpec((ROW_TILE, D), lambda i: (i, 0))
    return pl.pallas_call(
        body, grid=(nt,),
        in_specs=[pl.BlockSpec((ROW_TILE, D), lambda i: (jnp.minimum(i, nt - 2), 0)), pl.BlockSpec((ROW_TILE, D), lambda i: (0, 0)),
                  pl.BlockSpec((1, D), lambda i: (0, 0))],
        out_specs=[row, row, pl.BlockSpec((D, ROW_TILE), lambda i: (0, i))],
        out_shape=[jax.ShapeDtypeStruct((S + ROW_TILE, D), F32), jax.ShapeDtypeStruct((S + ROW_TILE, D), BF16),
                   jax.ShapeDtypeStruct((D, S + ROW_TILE), BF16)],
        name=name, compiler_params=_params("parallel"))(x, meta_tile, g)


def _rms_bwd_math(x, g, dy):
    r = lax.rsqrt(jnp.mean(x * x, axis=-1, keepdims=True) + NORM_EPS)
    xh = x * r
    dyg = dy * g
    dx = r * (dyg - xh * jnp.mean(dyg * xh, axis=-1, keepdims=True))
    return dx, xh


def final_loss(h2, target, gf, *, name):
    Tp, D = h2.shape
    nt = Tp // ROW_TILE

    def body(h_ref, t_ref, g_ref, dhb_ref, st_ref):
        i = pl.program_id(0)

        @pl.when(i == 0)
        def _():
            st_ref[...] = jnp.zeros_like(st_ref)

        x = h_ref[...]
        g = g_ref[...]
        r = lax.rsqrt(jnp.mean(x * x, axis=-1, keepdims=True) + NORM_EPS)
        xh = x * r
        err = jnp.where(i == nt - 1, 0.0, xh * g - t_ref[...])
        dout = err * (1.0 / D)
        dyg = dout * g
        dx = r * (dyg - xh * jnp.mean(dyg * xh, axis=-1, keepdims=True))
        dhb_ref[...] = dx.astype(BF16)
        st_ref[0:1, :] += jnp.sum(dout * xh, axis=0, keepdims=True)
        st_ref[1:2, :] += jnp.sum(err * err, axis=0, keepdims=True) * (0.5 / D)

    row = pl.BlockSpec((ROW_TILE, D), lambda i: (i, 0))
    return pl.pallas_call(
        body, grid=(nt,),
        in_specs=[row, pl.BlockSpec((ROW_TILE, D), lambda i: (jnp.minimum(i, nt - 2), 0)), pl.BlockSpec((1, D), lambda i: (0, 0))],
        out_specs=[row, pl.BlockSpec((8, D), lambda i: (0, 0))],
        out_shape=[jax.ShapeDtypeStruct((Tp, D), BF16), jax.ShapeDtypeStruct((8, D), F32)],
        name=name, compiler_params=_params("arbitrary"))(h2, target, gf)


def rms_bwd(h, g, du, dres, *, name):
    Tp, D = h.shape
    tr = _pick(Tp, (384, 256, 128))

    def body(h_ref, g_ref, du_ref, dr_ref, dhb_ref, st_ref):
        @pl.when(pl.program_id(0) == 0)
        def _():
            st_ref[...] = jnp.zeros_like(st_ref)

        du_v = du_ref[...].astype(F32)
        dx, xh = _rms_bwd_math(h_ref[...], g_ref[...], du_v)
        dhb_ref[...] = (dr_ref[...].astype(F32) + dx).astype(BF16)
        st_ref[0:1, :] += jnp.sum(du_v * xh, axis=0, keepdims=True)

    row = pl.BlockSpec((tr, D), lambda i: (i, 0))
    return pl.pallas_call(
        body, grid=(Tp // tr,), in_specs=[row, pl.BlockSpec((1, D), lambda i: (0, 0)), row, row],
        out_specs=[row, pl.BlockSpec((8, D), lambda i: (0, 0))],
        out_shape=[jax.ShapeDtypeStruct((Tp, D), BF16), jax.ShapeDtypeStruct((8, D), F32)],
        name=name, compiler_params=_params("arbitrary"))(h, g, du, dres)


def rms_bwd_input(h, g, du, dres, *, name, stages=()):
    Tp, D = h.shape
    nt = Tp // ROW_TILE

    def body(h_ref, g_ref, du_ref, dr_ref, gx_ref, gm_ref, st_ref):
        i = pl.program_id(0)

        @pl.when(i == 0)
        def _():
            st_ref[...] = jnp.zeros_like(st_ref)

        du_v = du_ref[...].astype(F32)
        dx, xh = _rms_bwd_math(h_ref[...], g_ref[...], du_v)
        dh = dr_ref[...].astype(F32) + dx
        st_ref[0:1, :] += jnp.sum(du_v * xh, axis=0, keepdims=True)

        @pl.when(i < nt - 1)
        def _():
            gx_ref[...] = dh

        @pl.when(i == nt - 1)
        def _():
            gm_ref[...] = dh[ROW_TILE - N_META:, :]

    row = pl.BlockSpec((ROW_TILE, D), lambda i: (i, 0))
    outs, staged = _staged_call(
        body, grid=(nt,), in_specs=[row, pl.BlockSpec((1, D), lambda i: (0, 0)), row, row],
        out_specs=[pl.BlockSpec((ROW_TILE, D), lambda i: (jnp.minimum(i, nt - 2), 0)), pl.BlockSpec((N_META, D), lambda i: (0, 0)),
                   pl.BlockSpec((8, D), lambda i: (0, 0))],
        out_shape=[jax.ShapeDtypeStruct((Tp - ROW_TILE, D), F32), jax.ShapeDtypeStruct((N_META, D), F32),
                   jax.ShapeDtypeStruct((8, D), F32)],
        scratch_shapes=[], name=name, semantics=("arbitrary",), inputs=[h, g, du, dres], stages=stages)
    return (outs, staged) if stages else outs


def _conv_shifts(ext):
    return tuple(pltpu.roll(ext, k, 0)[HIST:, :] for k in (1, 2, 3))


def _conv_taps(ext, cur, vec_ref, cs, rows):
    del rows
    x1, x2, x3 = _conv_shifts(ext)
    xc = (vec_ref[1:2, cs] + vec_ref[8:9, cs] * cur + vec_ref[7:8, cs] * x1 + vec_ref[6:7, cs] * x2 + vec_ref[5:6, cs] * x3)
    return xc, x1, x2, x3


def _window_sum_back(ext, w):
    s, sh = ext, 1
    while sh < w:
        s = s + pltpu.roll(s, sh, 0)
        sh *= 2
    return s[HIST:, :]


def _scan_rows(a, b, carry, *, reverse):
    rows = a.shape[0]
    rin = jnp.bitwise_and(lax.broadcasted_iota(jnp.int32, (rows, 1), 0), 7)
    sh = 1
    while sh < 8:
        keep = (rin < 8 - sh) if reverse else (rin >= sh)
        amount = rows - sh if reverse else sh
        a_sh = jnp.where(keep, pltpu.roll(a, amount, 0), 1.0)
        b_sh = jnp.where(keep, pltpu.roll(b, amount, 0), 0.0)
        b = b + a * b_sh
        a = a * a_sh
        sh *= 2
    out = [None] * (rows // 8)
    for g in (reversed(range(rows // 8)) if reverse else range(rows // 8)):
        hg = b[8 * g:8 * g + 8, :] + a[8 * g:8 * g + 8, :] * carry
        carry = hg[0:1, :] if reverse else hg[7:8, :]
        out[g] = hg
    return jnp.concatenate(out, axis=0)


def _lru_gates(xc, wa, wx, vec_ref, cs, sp):
    xcb = xc.astype(BF16)
    r = _sigmoid(jnp.dot(xcb, wa, preferred_element_type=F32) + vec_ref[2:3, cs])
    ig = _sigmoid(jnp.dot(xcb, wx, preferred_element_type=F32) + vec_ref[3:4, cs])
    log_a = (-LRU_C) * r * sp
    a = jnp.exp(log_a)
    a2 = a * a
    return xcb, r, ig, a, a2, _neg_expm1(2.0 * log_a, a2)


def mix_fwd(proj, pool_w, gate_a, gate_x, vecs, *, name, stages=()):
    Tp = proj.shape[0]
    D = proj.shape[1] // 5
    R = ROW_TILE
    nt = Tp // R
    H = D // HEAD_DIM
    PG = D // len(POOL_WINDOWS)

    def body(p_ref, pw_ref, wa_ref, wx_ref, vec_ref, m_ref, hs_ref, xc_ref, r_ref, ig_ref, a_ref, mu_ref, ge_ref, dge_ref,
             hist_p, hist_l, hcar, mtmp):
        i = pl.program_id(0)
        is_meta = i == 0

        @pl.when(is_meta)
        def _():
            hist_p[...] = jnp.zeros_like(hist_p)
            hist_l[...] = jnp.zeros_like(hist_l)
            hcar[...] = jnp.zeros_like(hcar)

        row, valid, t_log = _tile_masks(is_meta, R)

        for g, w in enumerate(POOL_WINDOWS):
            cs = slice(g * PG, (g + 1) * PG)
            v = p_ref[:, g * PG:(g + 1) * PG].astype(F32)
            ws = _window_sum_back(jnp.concatenate([hist_p[:, cs], v], axis=0), w)
            d = ws * _window_count_inv(t_log, w) - v
            y = jnp.dot(d.astype(BF16), pw_ref[g], preferred_element_type=F32)
            gp = p_ref[:, 3 * D + g * PG:3 * D + (g + 1) * PG].astype(F32)
            mtmp[:, cs] = _sigmoid(gp) * (y * vec_ref[0:1, cs])
            hist_p[:, cs] = v[R - HIST:, :]

        for h in range(H):
            cs = slice(h * HEAD_DIM, (h + 1) * HEAD_DIM)
            vl = p_ref[:, D + h * HEAD_DIM:D + (h + 1) * HEAD_DIM].astype(F32)
            xc, _, _, _ = _conv_taps(jnp.concatenate([hist_l[:, cs], vl], axis=0), vl, vec_ref, cs, R)
            sp = _softplus_neg(vec_ref[4:5, cs])
            xcb, r, ig, a, _, em = _lru_gates(xc, wa_ref[h], wx_ref[h], vec_ref, cs, sp)
            mult = jnp.sqrt(em)
            b = jnp.where(valid, mult * (ig * xc), 0.0)
            hs = _scan_rows(a, b, hcar[7:8, cs], reverse=False)
            hs_ref[:, cs] = hs
            xc_ref[:, cs], r_ref[:, cs], ig_ref[:, cs] = xcb, r.astype(BF16), ig.astype(BF16)
            a_ref[:, cs], mu_ref[:, cs] = a, mult
            hcar[:, cs] = hs[R - 8:, :]
            hist_l[:, cs] = vl[R - HIST:, :]
            vg = p_ref[:, 2 * D + h * HEAD_DIM:2 * D + (h + 1) * HEAD_DIM].astype(F32)
            ge, th = _gelu_tanh(vg)
            ge_ref[:, cs], dge_ref[:, cs] = ge.astype(BF16), _gelu_tanh_grad(vg, th).astype(BF16)
            gl = p_ref[:, 4 * D + h * HEAD_DIM:4 * D + (h + 1) * HEAD_DIM].astype(F32)
            m_ref[:, cs] = (mtmp[:, cs] + _sigmoid(gl) * (hs * ge)).astype(BF16)

    def tile(i):
        return (i + nt - 1) % nt

    full = lambda shape: pl.BlockSpec(shape, lambda i: (0,) * len(shape))
    outs, staged = _staged_call(
        body, grid=(nt,),
        in_specs=[pl.BlockSpec((R, 5 * D), lambda i: (tile(i), 0)), full(pool_w.shape), full(gate_a.shape), full(gate_x.shape),
                  full(vecs.shape)],
        out_specs=[pl.BlockSpec((R, D), lambda i: (tile(i), 0))] * 9,
        out_shape=[jax.ShapeDtypeStruct((Tp, D), dt) for dt in (BF16, F32, BF16, BF16, BF16, F32, F32, BF16, BF16)],
        scratch_shapes=[pltpu.VMEM((HIST, D), F32), pltpu.VMEM((HIST, D), F32), pltpu.VMEM((8, D), F32), pltpu.VMEM((R, D), F32)],
        name=name, semantics=("arbitrary",), inputs=[proj, pool_w, gate_a, gate_x, vecs], stages=stages)
    outs = [outs[0], outs[1], outs[2:]]
    return (outs, staged) if stages else outs


def mix_bwd(proj, hs, saved, dmerged, pool_w, gate_a, gate_x, vecs, *, name, stages=()):
    Tp = proj.shape[0]
    D = proj.shape[1] // 5
    R = ROW_TILE
    nt = Tp // R
    H = D // HEAD_DIM
    PG = D // len(POOL_WINDOWS)

    def body(p_ref, pprev_ref, hs_ref, hprev_ref, dm_ref, xc_ref, r_ref, ig_ref, a_ref, mu_ref, ge_ref, dge_ref, pw_ref, wa_ref, wx_ref, vec_ref,
             dp_ref, dpw_ref, dwa_ref, dwx_ref, dvec_ref, car_g, fut_dxc, fut_q):
        i = pl.program_id(0)
        is_meta = i == nt - 1

        @pl.when(i == 0)
        def _():
            dpw_ref[...] = jnp.zeros_like(dpw_ref)
            dwa_ref[...] = jnp.zeros_like(dwa_ref)
            dwx_ref[...] = jnp.zeros_like(dwx_ref)
            dvec_ref[...] = jnp.zeros_like(dvec_ref)
            car_g[...] = jnp.zeros_like(car_g)
            fut_dxc[...] = jnp.zeros_like(fut_dxc)
            fut_q[...] = jnp.zeros_like(fut_q)

        row, valid, t_log = _tile_masks(is_meta, R)
        keep_prev = jnp.logical_not(is_meta)

        def colsum(x):
            return jnp.sum(x, axis=0, keepdims=True)

        for g, w in enumerate(POOL_WINDOWS):
            cs = slice(g * PG, (g + 1) * PG)
            v = p_ref[:, g * PG:(g + 1) * PG].astype(F32)
            vprev = jnp.where(keep_prev, pprev_ref[:, g * PG:(g + 1) * PG].astype(F32), 0.0)
            inv_cnt = _window_count_inv(t_log, w)
            d = _window_sum_back(jnp.concatenate([vprev, v], axis=0), w) * inv_cnt - v
            d_bf = d.astype(BF16)
            y = jnp.dot(d_bf, pw_ref[g], preferred_element_type=F32)
            scale = vec_ref[0:1, cs]
            sg = _sigmoid(p_ref[:, 3 * D + g * PG:3 * D + (g + 1) * PG].astype(F32))
            dm = dm_ref[:, cs].astype(F32)
            dpo = dm * sg
            dp_ref[:, 3 * D + g * PG:3 * D + (g + 1) * PG] = (dm * (y * scale) * sg * (1.0 - sg)).astype(BF16)
            dvec_ref[0:1, cs] += colsum(dpo * y)
            dy = (dpo * scale).astype(BF16)
            dd = _dot_nt(dy, pw_ref[g])
            dpw_ref[g] += _dot_tn(d_bf, dy)
            q = dd * inv_cnt
            s, sh = jnp.concatenate([q, fut_q[:, cs]], axis=0), 1
            while sh < w:
                s = s + pltpu.roll(s, R + HIST - sh, 0)
                sh *= 2
            dp_ref[:, g * PG:(g + 1) * PG] = (s[:R, :] - dd).astype(BF16)
            fut_q[:, cs] = q[:HIST, :]

        for h in range(H):
            cs = slice(h * HEAD_DIM, (h + 1) * HEAD_DIM)
            pc = lambda blk: slice(blk * D + h * HEAD_DIM, blk * D + (h + 1) * HEAD_DIM)
            vl = p_ref[:, pc(1)].astype(F32)
            vlprev = jnp.where(keep_prev, pprev_ref[:, pc(1)].astype(F32), 0.0)
            x1, x2, x3 = _conv_shifts(jnp.concatenate([vlprev, vl], axis=0))
            lam = vec_ref[4:5, cs]
            sp = _softplus_neg(lam)
            xcb = xc_ref[:, cs]
            xc, r, ig = xcb.astype(F32), r_ref[:, cs].astype(F32), ig_ref[:, cs].astype(F32)
            a, mult = a_ref[:, cs], mu_ref[:, cs]
            a2 = a * a
            inv_mult = 1.0 / mult
            hsv = hs_ref[:, cs]
            hprev = jnp.where(row >= 1, pltpu.roll(hsv, 1, 0), hprev_ref[HIST - 1:HIST, cs])
            ge = ge_ref[:, cs].astype(F32)
            sgl = _sigmoid(p_ref[:, pc(4)].astype(F32))
            dm = dm_ref[:, cs].astype(F32)
            dlo = dm * sgl
            dp_ref[:, pc(4)] = (dm * (hsv * ge) * sgl * (1.0 - sgl)).astype(BF16)
            dp_ref[:, pc(2)] = (dlo * hsv * dge_ref[:, cs].astype(F32)).astype(BF16)
            a_next = jnp.where(row < R - 1, pltpu.roll(a, R - 1, 0), 1.0)
            G = _scan_rows(a_next, dlo * ge, car_g[0:1, cs], reverse=True)
            car_g[:, cs] = (a * G)[0:8, :]
            da = jnp.where(valid, G * hprev, 0.0)
            db = jnp.where(valid, G, 0.0)
            dmult = db * (ig * xc)
            dig = db * (mult * xc)
            dxc = db * (mult * ig)
            dlog_a = da * a - dmult * (a2 * inv_mult)
            dvec_ref[4:5, cs] += colsum(dlog_a * r) * (-LRU_C)
            dr = dlog_a * ((-LRU_C) * sp)
            dpa = dr * r * (1.0 - r)
            dpx = dig * ig * (1.0 - ig)
            dpa_bf = dpa.astype(BF16)
            dpx_bf = dpx.astype(BF16)
            dwa_ref[h] += _dot_tn(xcb, dpa_bf)
            dwx_ref[h] += _dot_tn(xcb, dpx_bf)
            dvec_ref[2:3, cs] += colsum(dpa)
            dvec_ref[3:4, cs] += colsum(dpx)
            dxc = dxc + _dot_nt(dpa_bf, wa_ref[h]) + _dot_nt(dpx_bf, wx_ref[h])
            ext = jnp.concatenate([dxc, fut_dxc[:, cs]], axis=0)
            n = R + HIST
            dvl = (vec_ref[8:9, cs] * dxc + vec_ref[7:8, cs] * pltpu.roll(ext, n - 1, 0)[:R, :]
                   + vec_ref[6:7, cs] * pltpu.roll(ext, n - 2, 0)[:R, :] + vec_ref[5:6, cs] * pltpu.roll(ext, n - 3, 0)[:R, :])
            dp_ref[:, pc(1)] = dvl.astype(BF16)
            dvec_ref[1:2, cs] += colsum(dxc)
            dvec_ref[8:9, cs] += colsum(dxc * vl)
            dvec_ref[7:8, cs] += colsum(dxc * x1)
            dvec_ref[6:7, cs] += colsum(dxc * x2)
            dvec_ref[5:6, cs] += colsum(dxc * x3)
            fut_dxc[:, cs] = dxc[:HIST, :]

            @pl.when(is_meta)
            def _():
                dvec_ref[4:5, cs] = dvec_ref[4:5, cs] * (-_sigmoid(-lam))

    def tile(i):
        return (2 * nt - 2 - i) % nt

    def prev_blk(i):
        per = R // HIST
        return jnp.where(i == nt - 1, 0, jnp.where(i == nt - 2, Tp // HIST - 1, (nt - 2 - i) * per - 1))

    full = lambda shape: pl.BlockSpec(shape, lambda i: (0,) * len(shape))
    G_ = len(POOL_WINDOWS)
    outs, staged = _staged_call(
        body, grid=(nt,),
        in_specs=[pl.BlockSpec((R, 5 * D), lambda i: (tile(i), 0)), pl.BlockSpec((HIST, 5 * D), lambda i: (prev_blk(i), 0)),
                  pl.BlockSpec((R, D), lambda i: (tile(i), 0)), pl.BlockSpec((HIST, D), lambda i: (prev_blk(i), 0)),
                  *[pl.BlockSpec((R, D), lambda i: (tile(i), 0))] * 8,
                  full(pool_w.shape), full(gate_a.shape), full(gate_x.shape), full(vecs.shape)],
        out_specs=[pl.BlockSpec((R, 5 * D), lambda i: (tile(i), 0)), full((G_, PG, PG)), full((H, HEAD_DIM, HEAD_DIM)),
                   full((H, HEAD_DIM, HEAD_DIM)), full((16, D))],
        out_shape=[jax.ShapeDtypeStruct((Tp, 5 * D), BF16), jax.ShapeDtypeStruct((G_, PG, PG), F32),
                   jax.ShapeDtypeStruct((H, HEAD_DIM, HEAD_DIM), F32), jax.ShapeDtypeStruct((H, HEAD_DIM, HEAD_DIM), F32),
                   jax.ShapeDtypeStruct((16, D), F32)],
        scratch_shapes=[pltpu.VMEM((8, D), F32), pltpu.VMEM((HIST, D), F32), pltpu.VMEM((HIST, D), F32)],
        name=name, semantics=("arbitrary",), inputs=[proj, proj, hs, hs, dmerged, *saved, pool_w, gate_a, gate_x, vecs], stages=stages)
    return (outs, staged) if stages else outs


def _adamw_math(w, g, m, v):
    mn = ADAM_B1 * m + (1.0 - ADAM_B1) * g
    vn = ADAM_B2 * v + (1.0 - ADAM_B2) * (g * g)
    m_hat = mn / (1.0 - ADAM_B1 ** ADAM_STEP)
    v_hat = vn / (1.0 - ADAM_B2 ** ADAM_STEP)
    return -ADAM_LR * (m_hat / (jnp.sqrt(v_hat) + ADAM_EPS) + ADAM_WD * w), mn, vn


def adamw(w, g, m, v, *, name):
    rows, cols = w.shape
    tr = _pick(rows, (256, 128, 64, 32, 16, 8))

    def body(w_ref, g_ref, m_ref, v_ref, go_ref, d_ref, mo_ref, vo_ref):
        gv = g_ref[...]
        go_ref[...] = gv
        d_ref[...], mo_ref[...], vo_ref[...] = _adamw_math(w_ref[...], gv, m_ref[...], v_ref[...])

    blk = pl.BlockSpec((tr, cols), lambda i: (i, 0))
    sds = jax.ShapeDtypeStruct((rows, cols), F32)
    return pl.pallas_call(body, grid=(rows // tr,), in_specs=[blk] * 4, out_specs=[blk] * 4, out_shape=[sds] * 4, name=name,
                          compiler_params=_params("parallel"))(w, g, m, v)


def allgather8(block, *, name, reduce_sum=False, stages=()):
    rows, cols = block.shape

    def body(x_ref, out_ref, *scratch):
        if reduce_sum:
            buf, send_sems, recv_sems, local_sem = scratch
        else:
            buf = out_ref
            send_sems, recv_sems, local_sem = scratch
        x, y, c, chips = _place()
        me, sibling = (x, y, c), (x, y, 1 - c)

        def slot(px, py, pc):
            return buf.at[4 * px + 2 * py + pc]

        def copy(k, blk, to, src=None):
            return pltpu.make_async_remote_copy(src_ref=slot(*blk) if src is None else src, dst_ref=slot(*blk),
                                                send_sem=send_sems.at[k], recv_sem=recv_sems.at[k], device_id=to, device_id_type=MESH)

        mine = pltpu.make_async_copy(x_ref, slot(*me), local_sem)
        mine.start()
        first = [copy(0, me, sibling, src=x_ref)]
        first += [copy(1 + j, me, (*chip, c), src=x_ref) for j, chip in enumerate(chips)]
        for cp in first:
            cp.start()
        passed = [copy(4 + j, (*chip, c), sibling) for j, chip in enumerate(chips)]
        for j, chip in enumerate(chips):
            copy(1 + j, (*chip, c), me).wait_recv()
            passed[j].start()
        copy(0, sibling, me).wait_recv()
        for j, chip in enumerate(chips):
            copy(4 + j, (*chip, 1 - c), me).wait_recv()
        for cp in first + passed:
            cp.wait_send()
        mine.wait()
        if reduce_sum:
            acc = buf[0]
            for d in range(1, 8):
                acc = acc + buf[d]
            out_ref[...] = acc

    sems = [pltpu.SemaphoreType.DMA((7,)), pltpu.SemaphoreType.DMA((7,)), pltpu.SemaphoreType.DMA]
    if reduce_sum:
        out_shape = jax.ShapeDtypeStruct((rows, cols), block.dtype)
        scratch = [pltpu.VMEM((8, rows, cols), block.dtype)] + sems
    else:
        out_shape = jax.ShapeDtypeStruct((8, rows, cols), block.dtype)
        scratch = sems
    vmem = pl.BlockSpec(memory_space=pltpu.VMEM)
    outs, staged = _staged_call(body, grid=(1,), in_specs=[vmem], out_specs=[vmem], out_shape=[out_shape], scratch_shapes=scratch,
                                name=name, semantics=("arbitrary",), inputs=[block], stages=stages)
    return (outs[0], staged) if stages else outs[0]


def cast_into_slot(w, chip_arr, *, name):
    _, r, cols = w.shape
    tr = _pick(r, (256, 128, 64, 32, 16))

    def body(chip_ref, w_ref, o_ref):
        del chip_ref
        o_ref[...] = w_ref[...].astype(BF16)

    grid_spec = pltpu.PrefetchScalarGridSpec(
        num_scalar_prefetch=1, grid=(2, r // tr),
        in_specs=[pl.BlockSpec((None, tr, cols), lambda h, i, chip: (h, i, 0))],
        out_specs=pl.BlockSpec((None, None, tr, cols), lambda h, i, chip: (chip[0], h, i, 0)))
    return pl.pallas_call(body, grid_spec=grid_spec, out_shape=jax.ShapeDtypeStruct((4, 2, r, cols), BF16), name=name,
                          compiler_params=_params("parallel", "parallel"))(chip_arr, w)


def proj_with_gather(u, bufs, order_arr, *, name):
    Tp, K = u.shape
    n = len(bufs)
    Ns = bufs[0].shape[3]
    tm = _pick(Tp, (1408, 512, 256, 128))
    tc = _pick(Ns, (512, 256, 128))
    n_rows = Tp // tm

    def body(order_ref, u_ref, *refs):
        del order_ref
        o_ref, outs = refs[n], refs[n + 1:2 * n + 1]
        wbuf, dir_send, dir_recv, rel_send, rel_recv, d2d_send, d2d_recv, load_sems = refs[2 * n + 1:]
        s, i = pl.program_id(0), pl.program_id(1)
        x, y, c, chips = _place()
        me = 2 * x + y
        (dx, dy) = chips[2]

        def direct(t, j, landing):
            px, py = chips[j]
            slot = outs[t].at[2 * px + py, c] if landing else outs[t].at[me, c]
            return _remote(slot, slot, dir_send.at[t, j], dir_recv.at[t, j], (px, py, c))

        def relay(t, j, landing):
            px, py = chips[j]
            ox, oy = chips[1 - j]
            rows = _relay_rows(outs[t], 2 * dx + dy, c, j) if landing else _relay_rows(outs[t], 2 * ox + oy, c, j)
            return _remote(rows, rows, rel_send.at[t, j], rel_recv.at[t, j], (px, py, c))

        def d2d(t, j, landing):
            px, py = chips[j]
            slot = outs[t].at[2 * px + py, 1 - c] if landing else outs[t].at[2 * px + py, c]
            return _remote(slot, slot, d2d_send.at[t, j], d2d_recv.at[t, j], (x, y, 1 - c))

        def load(chip_idx):
            parts = [pltpu.make_async_copy(outs[0].at[chip_idx, hh], wbuf.at[pl.ds(hh * (K // 2), K // 2), :], load_sems.at[hh])
                     for hh in range(2)]
            for cp in parts:
                cp.start()
            for cp in parts:
                cp.wait()

        def chip_of(j):
            px, py = chips[j]
            return 2 * px + py

        @pl.when(jnp.logical_and(s == 0, i == 0))
        def _():
            for t in range(n):
                for j in range(2):
                    direct(t, j, False).start()
            load(me)

        @pl.when(jnp.logical_and(s == 1, i == 0))
        def _():
            for j in range(2):
                direct(0, j, True).wait_recv()
            for j in range(2):
                relay(0, j, False).start()
                d2d(0, j, False).start()
            d2d(0, 0, True).wait_recv()
            load(chip_of(0))

        @pl.when(jnp.logical_and(s == 2, i == 0))
        def _():
            d2d(0, 1, True).wait_recv()
            for t in range(1, n):
                for j in range(2):
                    direct(t, j, True).wait_recv()
                for j in range(2):
                    relay(t, j, False).start()
                    d2d(t, j, False).start()
            load(chip_of(1))

        @pl.when(jnp.logical_and(s == 3, i == 0))
        def _():
            for j in range(2):
                relay(0, j, True).wait_recv()
            d2d(0, 2, False).start()
            d2d(0, 2, True).wait_recv()
            load(chip_of(2))

        uv = u_ref[...]
        for cc in range(Ns // tc):
            o_ref[:, cc * tc:(cc + 1) * tc] = jnp.dot(uv, wbuf[:, cc * tc:(cc + 1) * tc], preferred_element_type=F32).astype(BF16)

        @pl.when(jnp.logical_and(s == 3, i == n_rows - 1))
        def _():
            for t in range(1, n):
                for j in range(2):
                    relay(t, j, True).wait_recv()
                d2d(t, 2, False).start()
            for t in range(1, n):
                for j in range(3):
                    d2d(t, j, True).wait_recv()
            for t in range(n):
                for j in range(2):
                    direct(t, j, False).wait_send()
                    relay(t, j, False).wait_send()
                for j in range(3):
                    d2d(t, j, False).wait_send()

    grid_spec = pltpu.PrefetchScalarGridSpec(
        num_scalar_prefetch=1, grid=(4, n_rows),
        in_specs=[pl.BlockSpec((tm, K), lambda s, i, order: (i, 0)), *_any_specs(n)],
        out_specs=[pl.BlockSpec((tm, Ns), lambda s, i, order: (i, order[s])), *_any_specs(n)],
        scratch_shapes=[pltpu.VMEM((K, Ns), BF16), pltpu.SemaphoreType.DMA((n, 2)), pltpu.SemaphoreType.DMA((n, 2)),
                        pltpu.SemaphoreType.DMA((n, 2)), pltpu.SemaphoreType.DMA((n, 2)),
                        pltpu.SemaphoreType.DMA((n, 3)), pltpu.SemaphoreType.DMA((n, 3)), pltpu.SemaphoreType.DMA((2,))])
    res = pl.pallas_call(body, grid_spec=grid_spec, out_shape=[jax.ShapeDtypeStruct((Tp, 4 * Ns), BF16), *[_sds(b) for b in bufs]],
                         input_output_aliases={2 + t: 1 + t for t in range(n)}, name=name,
                         compiler_params=_params("arbitrary", "arbitrary"))(order_arr, u, *bufs)
    return res[0], list(res[1:])


def chip_presum(grad, recv, c_arr, *, name):
    _, _, r, cols = grad.shape
    tr = _pick(r, (256, 128, 64, 32, 16))

    def body(c_ref, g_ref, r_ref, o_ref):
        del c_ref
        o_ref[...] = (g_ref[...].astype(F32) + r_ref[...].astype(F32)).astype(BF16)

    grid_spec = pltpu.PrefetchScalarGridSpec(
        num_scalar_prefetch=1, grid=(4, r // tr),
        in_specs=[pl.BlockSpec((None, None, tr, cols), lambda k, i, c_ref: (k, c_ref[0], i, 0)),
                  pl.BlockSpec((None, tr, cols), lambda k, i, c_ref: (k, i, 0))],
        out_specs=pl.BlockSpec((None, tr, cols), lambda k, i, c_ref: (k, i, 0)))
    return pl.pallas_call(body, grid_spec=grid_spec, out_shape=jax.ShapeDtypeStruct((4, r, cols), BF16), name=name,
                          compiler_params=_params("parallel", "parallel"))(c_arr, grad, recv)


def final_half(grad, recv, got, mc_arr, *, name):
    _, _, r, cols = grad.shape
    tr = _pick(r, (256, 128, 64, 32, 16))

    def body(mc_ref, g_ref, r_ref, q_ref, o_ref):
        del mc_ref
        acc = g_ref[...].astype(F32) + r_ref[...].astype(F32)
        for j in range(3):
            acc = acc + q_ref[j].astype(F32)
        o_ref[...] = acc

    grid_spec = pltpu.PrefetchScalarGridSpec(
        num_scalar_prefetch=1, grid=(r // tr,),
        in_specs=[pl.BlockSpec((None, None, tr, cols), lambda i, mc: (mc[0], mc[1], i, 0)),
                  pl.BlockSpec((None, tr, cols), lambda i, mc: (mc[0], i, 0)),
                  pl.BlockSpec((3, tr, cols), lambda i, mc: (0, i, 0))],
        out_specs=pl.BlockSpec((tr, cols), lambda i, mc: (i, 0)))
    return pl.pallas_call(body, grid_spec=grid_spec, out_shape=jax.ShapeDtypeStruct((r, cols), F32), name=name,
                          compiler_params=_params("parallel"))(mc_arr, grad, recv, got)


def adamw_halves(w, mine, theirs, m, v, c_arr, *, name):
    _, r, cols = w.shape
    tr = _pick(r, (256, 128, 64, 32, 16, 8))

    def body(c_ref, w_ref, mine_ref, theirs_ref, m_ref, v_ref, go_ref, d_ref, mo_ref, vo_ref):
        gv = jnp.where(pl.program_id(0) == c_ref[0], mine_ref[...], theirs_ref[...])
        go_ref[...] = gv
        d_ref[...], mo_ref[...], vo_ref[...] = _adamw_math(w_ref[...], gv, m_ref[...], v_ref[...])

    blk = pl.BlockSpec((None, tr, cols), lambda h, i, c_ref: (h, i, 0))
    grid_spec = pltpu.PrefetchScalarGridSpec(
        num_scalar_prefetch=1, grid=(2, r // tr),
        in_specs=[blk, pl.BlockSpec((tr, cols), lambda h, i, c_ref: (jnp.where(h == c_ref[0], i, 0), 0)),
                  pl.BlockSpec((tr, cols), lambda h, i, c_ref: (jnp.where(h == c_ref[0], 0, i), 0)), blk, blk],
        out_specs=[blk] * 4)
    sds = jax.ShapeDtypeStruct((2, r, cols), F32)
    return pl.pallas_call(body, grid_spec=grid_spec, out_shape=[sds] * 4, name=name,
                          compiler_params=_params("parallel", "parallel"))(c_arr, w, mine, theirs, m, v)


def _pad_rows(a, rows):
    return jnp.pad(a, ((0, rows - a.shape[0]), (0, 0)))


def kernel(x, meta_tokens, norm1_g, w_in, pool_w, pool_scale, conv_w, conv_b, gate_a_w, gate_a_b, gate_x_w, gate_x_b, lru_lambda, w_out, norm2_g, mlp_w1, mlp_w2, final_g, loss_target, m_meta_tokens, m_norm1_g, m_w_in, m_pool_w, m_pool_scale, m_conv_w, m_conv_b, m_gate_a_w, m_gate_a_b, m_gate_x_w, m_gate_x_b, m_lru_lambda, m_w_out, m_norm2_g, m_mlp_w1, m_mlp_w2, m_final_g, v_meta_tokens, v_norm1_g, v_w_in, v_pool_w, v_pool_scale, v_conv_w, v_conv_b, v_gate_a_w, v_gate_a_b, v_gate_x_w, v_gate_x_b, v_lru_lambda, v_w_out, v_norm2_g, v_mlp_w1, v_mlp_w2, v_final_g):
    D = x.shape[-1]
    H = D // HEAD_DIM
    G = len(POOL_WINDOWS)
    PG = D // G
    ax, ay, ac = lax.axis_index("x"), lax.axis_index("y"), lax.axis_index("c")
    chip = 2 * ax + ay
    dshard = D // 4

    c_arr = jnp.reshape(ac, (1,)).astype(jnp.int32)
    chip_arr = jnp.reshape(chip, (1,)).astype(jnp.int32)
    mc_arr = jnp.stack([chip, ac]).astype(jnp.int32)
    names = ["w_in", "pool_w", "gate_a_w", "gate_x_w", "w_out", "mlp_w1", "mlp_w2"]
    big = [w_in, pool_w, gate_a_w, gate_x_w, w_out, mlp_w1, mlp_w2]

    def halves(w):
        w2d = w.reshape(-1, w.shape[-1])
        return w2d.reshape(2, w2d.shape[0] // 2, w2d.shape[1])

    bufs = [cast_into_slot(halves(w), chip_arr, name="cast_" + nm) for w, nm in zip(big, names)]
    order_arr = jnp.stack([chip, 2 * (1 - ax) + ay, 2 * ax + (1 - ay), 2 * (1 - ax) + (1 - ay)]).astype(jnp.int32)

    small_in = jnp.concatenate([meta_tokens, _pad_rows(conv_w[0], 8), _pad_rows(gate_a_b.reshape(1, dshard), 8),
                                _pad_rows(gate_x_b.reshape(1, dshard), 8)], axis=0)
    sm = allgather8(small_in, name="gather_small")[0::2]
    meta_f = sm[:, 0:16].transpose(1, 0, 2).reshape(N_META, D)
    conv_w_f = sm[:, 16:20].transpose(1, 0, 2).reshape(4, D)
    hd4 = HEAD_DIM // 4
    ba_f = sm[:, 24].reshape(4, H, hd4).transpose(1, 0, 2).reshape(1, D)
    bx_f = sm[:, 32].reshape(4, H, hd4).transpose(1, 0, 2).reshape(1, D)
    vecs = jnp.zeros((16, D), F32)
    for r0, part in ((0, pool_scale), (1, conv_b), (2, ba_f), (3, bx_f), (4, lru_lambda), (5, conv_w_f)):
        vecs = lax.dynamic_update_slice(vecs, part, (r0, 0))

    def chipwise(g, n_blocks, rows):
        return g.reshape(n_blocks, 4, rows, g.shape[-1]).transpose(1, 0, 2, 3).reshape(4, n_blocks * rows, g.shape[-1])

    def split2(g):
        return g.reshape(4, 2, g.shape[1] // 2, g.shape[2])

    def presum(t, g, r):
        return chip_presum(g, r, c_arr, name="presum_" + names[t])

    def total(t, g, r, q):
        return final_half(g, r, q, mc_arr, name="sum_" + names[t])

    xs, target, gfin = x[0], loss_target[0], final_g.reshape(1, D)
    meta_tile = jnp.concatenate([jnp.zeros((ROW_TILE - N_META, D), F32), meta_f], axis=0)
    h0, u, u_t = rmsnorm_fwd_input(xs, meta_tile, norm1_g, name="norm1")
    proj, (b_win, b_pool, b_ga, b_gx, b_wout) = proj_with_gather(u, bufs[0:5], order_arr, name="proj")
    w_in_f = b_win.reshape(4, w_in.shape[1], w_in.shape[2])
    pool_f = b_pool.reshape(4, G, PG // 4, PG).transpose(1, 0, 2, 3).reshape(G, PG, PG)
    ga_f = b_ga.reshape(4, H, HEAD_DIM // 4, HEAD_DIM).transpose(1, 0, 2, 3).reshape(H, HEAD_DIM, HEAD_DIM)
    gx_f = b_gx.reshape(4, H, HEAD_DIM // 4, HEAD_DIM).transpose(1, 0, 2, 3).reshape(H, HEAD_DIM, HEAD_DIM)
    w_out_f = b_wout.reshape(4 * w_out.shape[1], w_out.shape[2])
    (merged, hs, saved), [[b_w1, b_w2]] = mix_fwd(proj, pool_f, ga_f, gx_f, vecs, name="mix_fwd",
                                                  stages=[stage_gather_direct([bufs[5], bufs[6]])])
    h1, [[b_w1]] = mm_nn(merged, w_out_f, out_dtype=F32, name="out_proj", epilogue=lambda r, res: r + res, extras=(h0,),
                         stages=[stage_gather_relay([b_w1])])
    (u2, u2_t), [[b_w1]] = rmsnorm_fwd(h1, norm2_g, name="norm2", stages=[stage_gather_d2d([b_w1])])
    w1_f = b_w1.reshape(4, mlp_w1.shape[1], mlp_w1.shape[2])
    a1, [[b_w2]] = mm_nn(u2, w1_f, out_dtype=BF16, name="mlp_up", stages=[stage_gather_relay([b_w2])])
    [[b_w2]] = comm_call([stage_gather_d2d([b_w2])], name="w2_to_sibling")
    w2_f = b_w2.reshape(4 * mlp_w2.shape[1], mlp_w2.shape[2])
    Tp = h0.shape[0]
    h2 = mm_nn(a1, w2_f, out_dtype=F32, name="mlp_down", a_pro=_relu_sq, epilogue=lambda r, res: r + res, extras=(h1,),
               tiles=(_pick(Tp, (704, 512, 256, 128)), min(256, D), a1.shape[1]))
    dh2_bf, st_f = final_loss(h2, target, gfin, name="final_loss")

    da1 = mm_nt(dh2_bf, w2_f, out_dtype=BF16, name="mlp_down_dx",
                epilogue=lambda r, a: r * (2.0 * jnp.maximum(a.astype(F32), 0.0)), extras=(a1,))
    d_w2 = mm_tn(a1, dh2_bf, shards=1, out_dtype=BF16, name="mlp_down_dw", a_pro=_relu_sq, tiles=(512, min(1024, D), Tp))
    g6 = split2(d_w2.reshape(4, -1, d_w2.shape[-1]))
    d_w1, [[r6]] = mm_nn(u2_t, da1, out_dtype=BF16, name="mlp_up_dw", out_shards=4, tiles=(min(1024, D), None, Tp),
                         stages=[stage_to_sibling([g6])])
    g5 = split2(d_w1)
    p6 = presum(6, g6, r6)
    du2, [[q6_near], [r5]] = mm_nt(da1, w1_f, out_dtype=BF16, name="mlp_up_dx",
                                  stages=[stage_to_chips([p6], peers=(0, 1)), stage_to_sibling([g5])])
    p5 = presum(5, g5, r5)
    dh1_bf, st_2 = rms_bwd(h1, norm2_g, du2, dh2_bf, name="norm2_bwd")
    dmerged = mm_nt(dh1_bf, w_out_f, out_dtype=BF16, name="out_proj_dx")
    d_wout = mm_tn(merged, dh1_bf, shards=1, out_dtype=BF16, name="out_proj_dw", tiles=(512, min(1024, D), Tp))
    g4 = split2(d_wout.reshape(4, -1, d_wout.shape[-1]))
    (dproj, d_pool, d_ga, d_gx, d_vecs), [[q6], [q5], [r4]] = mix_bwd(
        proj, hs, saved, dmerged, pool_f, ga_f, gx_f, vecs, name="mix_bwd",
        stages=[stage_to_chips([p6], peers=(2,), into=[q6_near]), stage_to_chips([p5]), stage_to_sibling([g4])])
    p4 = presum(4, g4, r4)
    f5, f6 = total(5, g5, r5, q5), total(6, g6, r6, q6)
    g1, g2, g3 = split2(chipwise(d_pool, G, PG // 4)), split2(chipwise(d_ga, H, hd4)), split2(chipwise(d_gx, H, hd4))
    half_k = u_t.shape[0] // 2
    whole_k = (half_k, None, Tp)
    u_theirs = lax.dynamic_slice_in_dim(u_t, (1 - ac) * half_k, half_k, axis=0)
    u_mine = lax.dynamic_slice_in_dim(u_t, ac * half_k, half_k, axis=0)
    g_theirs, [[r1, r2, r3], [o5, o6], [q4]] = mm_nn(u_theirs, dproj, out_dtype=BF16, out_shards=4, tiles=whole_k, name="proj_dw_sibling_rows",
                                                    stages=[stage_to_sibling([g1, g2, g3]), stage_from_sibling([f5, f6]), stage_to_chips([p4])])
    p1, p2, p3 = presum(1, g1, r1), presum(2, g2, r2), presum(3, g3, r3)
    g_mine, [[r0], [q1, q2, q3]] = mm_nn(u_mine, dproj, out_dtype=BF16, out_shards=4, tiles=whole_k, name="proj_dw_own_rows",
                                         stages=[stage_from_sibling([g_theirs]), stage_to_chips([p1, p2, p3])])
    g0 = g_mine[:, None]
    p0 = chip_presum(g0, r0, jnp.zeros((1,), jnp.int32), name="presum_w_in")
    f4 = total(4, g4, r4, q4)
    du, [[q0], [o4]] = mm_nt(dproj, w_in_f, out_dtype=BF16, name="proj_dx", tiles=(None, None, w_in_f.shape[2]),
                             stages=[stage_to_chips([p0]), stage_from_sibling([f4])])
    f0 = final_half(g0, r0, q0, jnp.stack([chip, 0]).astype(jnp.int32), name="sum_w_in")
    f1, f2, f3 = total(1, g1, r1, q1), total(2, g2, r2, q2), total(3, g3, r3, q3)
    grad_x, d_meta, st_1 = rms_bwd_input(h0, norm1_g, du, dh1_bf, name="norm1_bwd")

    small = jnp.concatenate([d_meta, d_vecs, st_1, st_2, st_f], axis=0)
    tot, [[o0, o1, o2, o3]] = allgather8(small, name="sum_small", reduce_sum=True, stages=[stage_from_sibling([f0, f1, f2, f3])])
    mine = [f0, f1, f2, f3, f4, f5, f6]
    theirs = [o0, o1, o2, o3, o4, o5, o6]
    loss = jnp.sum(tot[49])
    g_meta = lax.dynamic_slice_in_dim(tot[0:16], chip * dshard, dshard, axis=1)
    g_pool_scale, g_conv_b, g_lam = tot[16:17], tot[17:18], tot[20:21]
    g_ba = lax.dynamic_slice_in_dim(tot[18].reshape(H, HEAD_DIM), chip * hd4, hd4, axis=1)[None]
    g_bx = lax.dynamic_slice_in_dim(tot[19].reshape(H, HEAD_DIM), chip * hd4, hd4, axis=1)[None]
    g_conv_w = lax.dynamic_slice_in_dim(tot[21:25], chip * dshard, dshard, axis=1)[None]
    g_n1, g_n2, g_fin = tot[32:33], tot[40:41], tot[48]

    def step(w, g, m, v, nm):
        cols = w.shape[-1]
        outs = adamw(w.reshape(-1, cols), g.reshape(-1, cols), m.reshape(-1, cols), v.reshape(-1, cols), name="adamw_" + nm)
        return [o.reshape(w.shape) for o in outs]

    def step_big(t, w, m, v):
        outs = adamw_halves(halves(w), mine[t], theirs[t], halves(m), halves(v), c_arr, name="adamw_" + names[t])
        return [o.reshape(w.shape) for o in outs]

    res = dict(meta_tokens=step(meta_tokens, g_meta, m_meta_tokens, v_meta_tokens, "meta_tokens"),
               norm1_g=step(norm1_g, g_n1, m_norm1_g, v_norm1_g, "norm1_g"),
               w_in=step_big(0, w_in, m_w_in, v_w_in), pool_w=step_big(1, pool_w, m_pool_w, v_pool_w),
               pool_scale=step(pool_scale, g_pool_scale, m_pool_scale, v_pool_scale, "pool_scale"),
               conv_w=step(conv_w, g_conv_w, m_conv_w, v_conv_w, "conv_w"), conv_b=step(conv_b, g_conv_b, m_conv_b, v_conv_b, "conv_b"),
               gate_a_w=step_big(2, gate_a_w, m_gate_a_w, v_gate_a_w), gate_a_b=step(gate_a_b, g_ba, m_gate_a_b, v_gate_a_b, "gate_a_b"),
               gate_x_w=step_big(3, gate_x_w, m_gate_x_w, v_gate_x_w), gate_x_b=step(gate_x_b, g_bx, m_gate_x_b, v_gate_x_b, "gate_x_b"),
               lru_lambda=step(lru_lambda, g_lam, m_lru_lambda, v_lru_lambda, "lru_lambda"), w_out=step_big(4, w_out, m_w_out, v_w_out),
               norm2_g=step(norm2_g, g_n2, m_norm2_g, v_norm2_g, "norm2_g"), mlp_w1=step_big(5, mlp_w1, m_mlp_w1, v_mlp_w1),
               mlp_w2=step_big(6, mlp_w2, m_mlp_w2, v_mlp_w2), final_g=step(final_g, g_fin, m_final_g, v_final_g, "final_g"))
    order = list(res)
    return (loss, grad_x[None], *[res[n][0] for n in order], *[res[n][1] for n in order], *[res[n][2] for n in order],
            *[res[n][3] for n in order])
```

```python
import functools

import jax
import jax.numpy as jnp
from jax import lax
from jax.experimental import pallas as pl
from jax.experimental.pallas import tpu as pltpu

F32 = jnp.float32
BF16 = jnp.bfloat16
MESH = pl.DeviceIdType.MESH

NORM_EPS = 1e-6
N_META = 16
HEAD_DIM = 256
POOL_WINDOWS = (2, 4, 8, 16)
LRU_C = 8.0
ROW_TILE = 128
HIST = 16
VMEM_LIMIT_BYTES = 56 * 1024 * 1024
ADAM_LR, ADAM_B1, ADAM_B2, ADAM_EPS, ADAM_WD, ADAM_STEP = 0.001, 0.9, 0.999, 1e-08, 0.01, 10


def _pick(n, prefs):
    for p in prefs:
        if n % p == 0:
            return p
    return n


def _params(*sem):
    return pltpu.CompilerParams(dimension_semantics=sem, vmem_limit_bytes=VMEM_LIMIT_BYTES)


def _sigmoid(x):
    return 1.0 / (1.0 + jnp.exp(-x))


def _gelu_tanh(x):
    t = jnp.tanh(0.7978845608028654 * (x + 0.044715 * (x * x * x)))
    return 0.5 * x * (1.0 + t), t


def _gelu_tanh_grad(x, t):
    return 0.5 * (1.0 + t) + 0.5 * x * (1.0 - t * t) * (0.7978845608028654 * (1.0 + 3.0 * 0.044715 * x * x))


def _neg_expm1(x, exp_x):
    series = x * (-1.0 + x * (-0.5 + x * ((-1.0 / 6.0) + x * ((-1.0 / 24.0) + x * (-1.0 / 120.0)))))
    return jnp.where(x > -0.125, series, 1.0 - exp_x)


def _softplus_neg(lam):
    z = jnp.exp(-jnp.abs(lam))
    log1p_z = jnp.where(z < 0.01, z * (1.0 - z * (0.5 - z * (1.0 / 3.0))), jnp.log(1.0 + z))
    return jnp.maximum(-lam, 0.0) + log1p_z


def _tile_masks(is_meta, rows):
    row = lax.broadcasted_iota(jnp.int32, (rows, 1), 0)
    valid = jnp.logical_or(jnp.logical_not(is_meta), row >= rows - N_META)
    t_log = jnp.where(is_meta, row - (rows - N_META), 1 << 20)
    return row, valid, t_log


def _window_count_inv(t_log, w):
    return 1.0 / jnp.clip(t_log + 1, 1, w).astype(F32)


def _dot_nt(a, b):
    return lax.dot_general(a, b, (((1,), (1,)), ((), ())), preferred_element_type=F32)


def _dot_tn(a, b):
    return lax.dot_general(a, b, (((0,), (0,)), ((), ())), preferred_element_type=F32)


def _place():
    x, y, c = lax.axis_index("x"), lax.axis_index("y"), lax.axis_index("c")
    chips = [(1 - x, y), (x, 1 - y), (1 - x, 1 - y)]
    return x, y, c, chips


def _remote(src, dst, send_sem, recv_sem, to):
    return pltpu.make_async_remote_copy(src_ref=src, dst_ref=dst, send_sem=send_sem, recv_sem=recv_sem, device_id=to,
                                        device_id_type=MESH)


class Stage:
    def __init__(self, arrays, out_shapes, aliases, n_copies, copies):
        self.arrays, self.out_shapes, self.aliases, self.n_copies, self.copies = list(arrays), list(out_shapes), aliases, n_copies, copies


def _sds(a):
    return jax.ShapeDtypeStruct(a.shape, a.dtype)


def _relay_rows(buf_ref, chip_idx, c, quarter):
    rows = buf_ref.shape[2] // 2
    return buf_ref.at[chip_idx, c, pl.ds(quarter * rows, rows)]


def stage_gather_direct(bufs):
    n = len(bufs)

    def copies(ins, outs, send, recv):
        x, y, c, chips = _place()
        me = 2 * x + y
        sends, recvs = [], []
        for t in range(n):
            for j, (px, py) in enumerate(chips[:2]):
                k = 2 * t + j
                mine, theirs = outs[t].at[me, c], outs[t].at[2 * px + py, c]
                sends.append(_remote(mine, mine, send.at[k], recv.at[k], (px, py, c)))
                recvs.append(_remote(theirs, theirs, send.at[k], recv.at[k], (px, py, c)))
        return sends, recvs

    return Stage(bufs, [_sds(b) for b in bufs], {t: t for t in range(n)}, 2 * n, copies)


def stage_gather_relay(bufs):
    n = len(bufs)

    def copies(ins, outs, send, recv):
        x, y, c, chips = _place()
        (xx, xy), (yx, yy), (dx, dy) = chips
        sends, recvs = [], []
        for t in range(n):
            from_y, from_x = _relay_rows(outs[t], 2 * yx + yy, c, 0), _relay_rows(outs[t], 2 * xx + xy, c, 1)
            sends.append(_remote(from_y, from_y, send.at[2 * t], recv.at[2 * t], (xx, xy, c)))
            sends.append(_remote(from_x, from_x, send.at[2 * t + 1], recv.at[2 * t + 1], (yx, yy, c)))
            for q, (px, py) in enumerate(chips[:2]):
                got = _relay_rows(outs[t], 2 * dx + dy, c, q)
                recvs.append(_remote(got, got, send.at[2 * t + q], recv.at[2 * t + q], (px, py, c)))
        return sends, recvs

    return Stage(bufs, [_sds(b) for b in bufs], {t: t for t in range(n)}, 2 * n, copies)


def stage_gather_d2d(bufs):
    n = len(bufs)

    def copies(ins, outs, send, recv):
        x, y, c, chips = _place()
        sends, recvs = [], []
        for t in range(n):
            for j, (px, py) in enumerate(chips):
                k = 3 * t + j
                got, sib = outs[t].at[2 * px + py, c], outs[t].at[2 * px + py, 1 - c]
                sends.append(_remote(got, got, send.at[k], recv.at[k], (x, y, 1 - c)))
                recvs.append(_remote(sib, sib, send.at[k], recv.at[k], (x, y, 1 - c)))
        return sends, recvs

    return Stage(bufs, [_sds(b) for b in bufs], {t: t for t in range(n)}, 3 * n, copies)


def stage_to_sibling(grads):
    n = len(grads)

    def copies(ins, outs, send, recv):
        x, y, c, _ = _place()
        sends, recvs = [], []
        for t in range(n):
            for k4 in range(4):
                k = 4 * t + k4
                sends.append(_remote(ins[t].at[k4, 1 - c], outs[t].at[k4], send.at[k], recv.at[k], (x, y, 1 - c)))
                recvs.append(_remote(outs[t].at[k4], outs[t].at[k4], send.at[k], recv.at[k], (x, y, 1 - c)))
        return sends, recvs

    return Stage(grads, [jax.ShapeDtypeStruct((4, *g.shape[2:]), g.dtype) for g in grads], {}, 4 * n, copies)


def stage_to_chips(presums, peers=(0, 1, 2), into=None):
    n = len(presums)

    def copies(ins, outs, send, recv):
        x, y, c, chips = _place()
        sends, recvs = [], []
        for t in range(n):
            for slot, j in enumerate(peers):
                px, py = chips[j]
                k = len(peers) * t + slot
                sends.append(_remote(ins[t].at[2 * px + py], outs[t].at[j], send.at[k], recv.at[k], (px, py, c)))
                recvs.append(_remote(outs[t].at[j], outs[t].at[j], send.at[k], recv.at[k], (px, py, c)))
        return sends, recvs

    out_shapes = [jax.ShapeDtypeStruct((3, *p.shape[1:]), p.dtype) for p in presums]
    if into is None:
        return Stage(presums, out_shapes, {}, len(peers) * n, copies)
    return Stage([*presums, *into], out_shapes, {n + t: t for t in range(n)}, len(peers) * n, copies)


def stage_from_sibling(halves):
    n = len(halves)

    def copies(ins, outs, send, recv):
        x, y, c, _ = _place()
        sends = [_remote(ins[t], outs[t], send.at[t], recv.at[t], (x, y, 1 - c)) for t in range(n)]
        recvs = [_remote(outs[t], outs[t], send.at[t], recv.at[t], (x, y, 1 - c)) for t in range(n)]
        return sends, recvs

    return Stage(halves, [_sds(h) for h in halves], {}, n, copies)


def _any_specs(n):
    return [pl.BlockSpec(memory_space=pl.ANY)] * n


def _staged_call(body, *, grid, in_specs, out_specs, out_shape, scratch_shapes, name, semantics, inputs, stages=()):
    n_in, n_out, n_scr = len(in_specs), len(out_specs), len(scratch_shapes)
    st_arrays = [a for s in stages for a in s.arrays]
    st_outs = [o for s in stages for o in s.out_shapes]
    st_sems = [pltpu.SemaphoreType.DMA((s.n_copies,)) for s in stages for _ in range(2)]
    aliases = {}
    at_in, at_out = n_in, n_out
    for s in stages:
        for a, o in s.aliases.items():
            aliases[at_in + a] = at_out + o
        at_in += len(s.arrays)
        at_out += len(s.out_shapes)

    def full_body(*refs):
        pos = 0

        def take(count):
            nonlocal pos
            part = refs[pos:pos + count]
            pos += count
            return part

        ins, s_ins, outs, s_outs, scr = take(n_in), take(len(st_arrays)), take(n_out), take(len(st_outs)), take(n_scr)
        s_sems = refs[pos:]

        def each_stage(action):
            at_i = at_o = 0
            for idx, s in enumerate(stages):
                sends, recvs = s.copies(s_ins[at_i:at_i + len(s.arrays)], s_outs[at_o:at_o + len(s.out_shapes)],
                                        s_sems[2 * idx], s_sems[2 * idx + 1])
                action(sends, recvs)
                at_i += len(s.arrays)
                at_o += len(s.out_shapes)

        if stages:
            ids = [pl.program_id(a) for a in range(len(grid))]
            first = functools.reduce(jnp.logical_and, [i == 0 for i in ids])
            last = functools.reduce(jnp.logical_and, [i == g - 1 for i, g in zip(ids, grid)])

            def start(sends, recvs):
                for cp in sends:
                    cp.start()

            def finish(sends, recvs):
                for cp in recvs:
                    cp.wait_recv()
                for cp in sends:
                    cp.wait_send()

            @pl.when(first)
            def _():
                each_stage(start)

        body(*ins, *outs, *scr)

        if stages:
            @pl.when(last)
            def _():
                each_stage(finish)

    sem = tuple("arbitrary" for _ in grid) if stages else tuple(semantics)
    res = pl.pallas_call(
        full_body, grid=grid, in_specs=[*in_specs, *_any_specs(len(st_arrays))], out_specs=[*out_specs, *_any_specs(len(st_outs))],
        out_shape=[*out_shape, *st_outs], scratch_shapes=[*scratch_shapes, *st_sems], input_output_aliases=aliases, name=name,
        compiler_params=_params(*sem))(*inputs, *st_arrays)
    outs, rest = list(res[:n_out]), list(res[n_out:])
    per_stage = []
    for s in stages:
        per_stage.append(rest[:len(s.out_shapes)])
        rest = rest[len(s.out_shapes):]
    return outs, per_stage


def comm_call(stages, *, name):
    return _staged_call(lambda: None, grid=(1,), in_specs=[], out_specs=[], out_shape=[], scratch_shapes=[], name=name,
                        semantics=("arbitrary",), inputs=[], stages=stages)[1]


def _matmul(kind, a, b, *, grid, a_spec, b_spec, out_spec, out_shape, acc_shape, name,
            a_pro=None, epilogue=None, extras=(), extra_specs=(), stages=()):
    nk = grid[2]
    n_extra = len(extras)

    def body(a_ref, b_ref, *rest):
        extra_refs = rest[:n_extra]
        o_ref = rest[n_extra]
        av = a_ref[...]
        if a_pro is not None:
            av = a_pro(av)
        av = av.astype(BF16)
        bv = b_ref[...].astype(BF16)
        if kind == "nn":
            p = jnp.dot(av, bv, preferred_element_type=F32)
        elif kind == "nt":
            p = _dot_nt(av, bv)
        else:
            p = _dot_tn(av, bv)

        def finish(r):
            if epilogue is not None:
                r = epilogue(r, *[e[...] for e in extra_refs])
            o_ref[...] = r.astype(o_ref.dtype)

        if nk == 1:
            finish(p)
        else:
            acc_ref = rest[n_extra + 1]
            k = pl.program_id(2)

            @pl.when(k == 0)
            def _():
                acc_ref[...] = p

            @pl.when(k > 0)
            def _():
                acc_ref[...] += p

            @pl.when(k == nk - 1)
            def _():
                finish(acc_ref[...])

    scratch = [] if nk == 1 else [pltpu.VMEM(acc_shape, F32)]
    outs, staged = _staged_call(body, grid=grid, in_specs=[a_spec, b_spec, *extra_specs], out_specs=[out_spec], out_shape=[out_shape],
                                scratch_shapes=scratch, name=name, semantics=("parallel", "parallel", "arbitrary"),
                                inputs=[a, b, *extras], stages=stages)
    return (outs[0], staged) if stages else outs[0]


def mm_nn(a, b, *, out_dtype, name, a_pro=None, epilogue=None, extras=(), stages=(), out_shards=1, tiles=(None, None, None)):
    M, K = a.shape
    sharded = b.ndim == 3
    ns = b.shape[2] if sharded else b.shape[1]
    N = ns * b.shape[0] if sharded else ns
    if out_shards > 1:
        ns = N // out_shards
    tm = tiles[0] or _pick(M, (1408, 1024, 512, 256, 128))
    tn = tiles[1] or _pick(ns, (512, 256, 128))
    tk = tiles[2] or _pick(K, (2048, 1408, 1024, 512, 256, 128))
    per = ns // tn
    if sharded:
        b_spec = pl.BlockSpec((None, tk, tn), lambda i, j, k: (j // per, k, j % per))
    else:
        b_spec = pl.BlockSpec((tk, tn), lambda i, j, k: (k, j))
    mn = pl.BlockSpec((tm, tn), lambda i, j, k: (i, j))
    if out_shards > 1:
        out_spec = pl.BlockSpec((None, tm, tn), lambda i, j, k: (j // per, i, j % per))
        out_shape = jax.ShapeDtypeStruct((out_shards, M, ns), out_dtype)
    else:
        out_spec, out_shape = mn, jax.ShapeDtypeStruct((M, N), out_dtype)
    return _matmul("nn", a, b, grid=(M // tm, N // tn, K // tk), a_spec=pl.BlockSpec((tm, tk), lambda i, j, k: (i, k)),
                   b_spec=b_spec, out_spec=out_spec, out_shape=out_shape, acc_shape=(tm, tn),
                   name=name, a_pro=a_pro, epilogue=epilogue, extras=extras, extra_specs=[mn] * len(extras), stages=stages)


def mm_nt(a, w, *, out_dtype, name, epilogue=None, extras=(), stages=(), tiles=(None, None, None)):
    M, N = a.shape
    sharded = w.ndim == 3
    kw = w.shape[1] if sharded else w.shape[0]
    ns = w.shape[2] if sharded else w.shape[1]
    tm = tiles[0] or _pick(M, (1408, 512, 256, 128))
    tkw = tiles[1] or _pick(kw, (1024, 512, 256, 128))
    tk = tiles[2] or _pick(ns, (2048, 1280, 1024, 512, 256, 128))
    per = ns // tk
    if sharded:
        w_spec = pl.BlockSpec((None, tkw, tk), lambda i, j, k: (k // per, j, k % per))
    else:
        w_spec = pl.BlockSpec((tkw, tk), lambda i, j, k: (j, k))
    mo = pl.BlockSpec((tm, tkw), lambda i, j, k: (i, j))
    return _matmul("nt", a, w, grid=(M // tm, kw // tkw, N // tk), a_spec=pl.BlockSpec((tm, tk), lambda i, j, k: (i, k)),
                   b_spec=w_spec, out_spec=mo, out_shape=jax.ShapeDtypeStruct((M, kw), out_dtype), acc_shape=(tm, tkw),
                   name=name, epilogue=epilogue, extras=extras, extra_specs=[mo] * len(extras), stages=stages)


def mm_tn(a, g, *, shards, name, out_dtype, a_pro=None, stages=(), tiles=(None, None, None)):
    T, kw = a.shape
    N = g.shape[1]
    ns = N // shards
    tkw = tiles[0] or _pick(kw, (1024, 512, 256, 128))
    tn = tiles[1] or _pick(ns, (1280, 1024, 512, 256, 128))
    tt = tiles[2] or _pick(T, (1408, 512, 256, 128))
    per = ns // tn
    if shards > 1:
        out_spec = pl.BlockSpec((None, tkw, tn), lambda i, j, k: (j // per, i, j % per))
        out_shape = jax.ShapeDtypeStruct((shards, kw, ns), out_dtype)
    else:
        out_spec = pl.BlockSpec((tkw, tn), lambda i, j, k: (i, j))
        out_shape = jax.ShapeDtypeStruct((kw, N), out_dtype)
    return _matmul("tn", a, g, grid=(kw // tkw, N // tn, T // tt), a_spec=pl.BlockSpec((tt, tkw), lambda i, j, k: (k, i)),
                   b_spec=pl.BlockSpec((tt, tn), lambda i, j, k: (k, j)), out_spec=out_spec, out_shape=out_shape,
                   acc_shape=(tkw, tn), name=name, a_pro=a_pro, stages=stages)


def _relu_sq(a):
    r = jnp.maximum(a, 0.0)
    return r * r


def rmsnorm_fwd(h, g, *, name, stages=()):
    Tp, D = h.shape
    tr = _pick(Tp, (384, 256, 128))

    def body(h_ref, g_ref, u_ref, ut_ref):
        x = h_ref[...]
        r = lax.rsqrt(jnp.mean(x * x, axis=-1, keepdims=True) + NORM_EPS)
        u = (x * r) * g_ref[...]
        u_ref[...] = u.astype(BF16)
        ut_ref[...] = u.T.astype(BF16)

    row = pl.BlockSpec((tr, D), lambda i: (i, 0))
    outs, staged = _staged_call(body, grid=(Tp // tr,), in_specs=[row, pl.BlockSpec((1, D), lambda i: (0, 0))],
                                out_specs=[row, pl.BlockSpec((D, tr), lambda i: (0, i))],
                                out_shape=[jax.ShapeDtypeStruct((Tp, D), BF16), jax.ShapeDtypeStruct((D, Tp), BF16)],
                                scratch_shapes=[], name=name, semantics=("parallel",), inputs=[h, g], stages=stages)
    return (outs, staged) if stages else outs


def rmsnorm_fwd_input(x, meta_tile, g, *, name):
    S, D = x.shape
    nt = S // ROW_TILE + 1

    def body(x_ref, mt_ref, g_ref, h_ref, u_ref, ut_ref):
        h = jnp.where(pl.program_id(0) == nt - 1, mt_ref[...], x_ref[...])
        r = lax.rsqrt(jnp.mean(h * h, axis=-1, keepdims=True) + NORM_EPS)
        u = (h * r) * g_ref[...]
        h_ref[...] = h
        u_ref[...] = u.astype(BF16)
        ut_ref[...] = u.T.astype(BF16)

    row = pl.BlockSpec((ROW_TILE, D), lambda i: (i, 0))
    return pl.pallas_call(
        body, grid=(nt,),
        in_specs=[pl.BlockSpec((ROW_TILE, D), lambda i: (jnp.minimum(i, nt - 2), 0)), pl.BlockSpec((ROW_TILE, D), lambda i: (0, 0)),
                  pl.BlockSpec((1, D), lambda i: (0, 0))],
        out_specs=[row, row, pl.BlockSpec((D, ROW_TILE), lambda i: (0, i))],
        out_shape=[jax.ShapeDtypeStruct((S + ROW_TILE, D), F32), jax.ShapeDtypeStruct((S + ROW_TILE, D), BF16),
                   jax.ShapeDtypeStruct((D, S + ROW_TILE), BF16)],
        name=name, compiler_params=_params("parallel"))(x, meta_tile, g)


def _rms_bwd_math(x, g, dy):
    r = lax.rsqrt(jnp.mean(x * x, axis=-1, keepdims=True) + NORM_EPS)
    xh = x * r
    dyg = dy * g
    dx = r * (dyg - xh * jnp.mean(dyg * xh, axis=-1, keepdims=True))
    return dx, xh


def final_loss(h2, target, gf, *, name):
    Tp, D = h2.shape
    nt = Tp // ROW_TILE

    def body(h_ref, t_ref, g_ref, dhb_ref, st_ref):
        i = pl.program_id(0)

        @pl.when(i == 0)
        def _():
            st_ref[...] = jnp.zeros_like(st_ref)

        x = h_ref[...]
        g = g_ref[...]
        r = lax.rsqrt(jnp.mean(x * x, axis=-1, keepdims=True) + NORM_EPS)
        xh = x * r
        err = jnp.where(i == nt - 1, 0.0, xh * g - t_ref[...])
        dout = err * (1.0 / D)
        dyg = dout * g
        dx = r * (dyg - xh * jnp.mean(dyg * xh, axis=-1, keepdims=True))
        dhb_ref[...] = dx.astype(BF16)
        st_ref[0:1, :] += jnp.sum(dout * xh, axis=0, keepdims=True)
        st_ref[1:2, :] += jnp.sum(err * err, axis=0, keepdims=True) * (0.5 / D)

    row = pl.BlockSpec((ROW_TILE, D), lambda i: (i, 0))
    return pl.pallas_call(
        body, grid=(nt,),
        in_specs=[row, pl.BlockSpec((ROW_TILE, D), lambda i: (jnp.minimum(i, nt - 2), 0)), pl.BlockSpec((1, D), lambda i: (0, 0))],
        out_specs=[row, pl.BlockSpec((8, D), lambda i: (0, 0))],
        out_shape=[jax.ShapeDtypeStruct((Tp, D), BF16), jax.ShapeDtypeStruct((8, D), F32)],
        name=name, compiler_params=_params("arbitrary"))(h2, target, gf)


def rms_bwd(h, g, du, dres, *, name):
    Tp, D = h.shape
    tr = _pick(Tp, (384, 256, 128))

    def body(h_ref, g_ref, du_ref, dr_ref, dhb_ref, st_ref):
        @pl.when(pl.program_id(0) == 0)
        def _():
            st_ref[...] = jnp.zeros_like(st_ref)

        du_v = du_ref[...].astype(F32)
        dx, xh = _rms_bwd_math(h_ref[...], g_ref[...], du_v)
        dhb_ref[...] = (dr_ref[...].astype(F32) + dx).astype(BF16)
        st_ref[0:1, :] += jnp.sum(du_v * xh, axis=0, keepdims=True)

    row = pl.BlockSpec((tr, D), lambda i: (i, 0))
    return pl.pallas_call(
        body, grid=(Tp // tr,), in_specs=[row, pl.BlockSpec((1, D), lambda i: (0, 0)), row, row],
        out_specs=[row, pl.BlockSpec((8, D), lambda i: (0, 0))],
        out_shape=[jax.ShapeDtypeStruct((Tp, D), BF16), jax.ShapeDtypeStruct((8, D), F32)],
        name=name, compiler_params=_params("arbitrary"))(h, g, du, dres)


def rms_bwd_input(h, g, du, dres, *, name, stages=()):
    Tp, D = h.shape
    nt = Tp // ROW_TILE

    def body(h_ref, g_ref, du_ref, dr_ref, gx_ref, gm_ref, st_ref):
        i = pl.program_id(0)

        @pl.when(i == 0)
        def _():
            st_ref[...] = jnp.zeros_like(st_ref)

        du_v = du_ref[...].astype(F32)
        dx, xh = _rms_bwd_math(h_ref[...], g_ref[...], du_v)
        dh = dr_ref[...].astype(F32) + dx
        st_ref[0:1, :] += jnp.sum(du_v * xh, axis=0, keepdims=True)

        @pl.when(i < nt - 1)
        def _():
            gx_ref[...] = dh

        @pl.when(i == nt - 1)
        def _():
            gm_ref[...] = dh[ROW_TILE - N_META:, :]

    row = pl.BlockSpec((ROW_TILE, D), lambda i: (i, 0))
    outs, staged = _staged_call(
        body, grid=(nt,), in_specs=[row, pl.BlockSpec((1, D), lambda i: (0, 0)), row, row],
        out_specs=[pl.BlockSpec((ROW_TILE, D), lambda i: (jnp.minimum(i, nt - 2), 0)), pl.BlockSpec((N_META, D), lambda i: (0, 0)),
                   pl.BlockSpec((8, D), lambda i: (0, 0))],
        out_shape=[jax.ShapeDtypeStruct((Tp - ROW_TILE, D), F32), jax.ShapeDtypeStruct((N_META, D), F32),
                   jax.ShapeDtypeStruct((8, D), F32)],
        scratch_shapes=[], name=name, semantics=("arbitrary",), inputs=[h, g, du, dres], stages=stages)
    return (outs, staged) if stages else outs


def _conv_shifts(ext):
    return tuple(pltpu.roll(ext, k, 0)[HIST:, :] for k in (1, 2, 3))


def _conv_taps(ext, cur, vec_ref, cs, rows):
    del rows
    x1, x2, x3 = _conv_shifts(ext)
    xc = (vec_ref[1:2, cs] + vec_ref[8:9, cs] * cur + vec_ref[7:8, cs] * x1 + vec_ref[6:7, cs] * x2 + vec_ref[5:6, cs] * x3)
    return xc, x1, x2, x3


def _window_sum_back(ext, w):
    s, sh = ext, 1
    while sh < w:
        s = s + pltpu.roll(s, sh, 0)
        sh *= 2
    return s[HIST:, :]


def _scan_rows(a, b, carry, *, reverse):
    rows = a.shape[0]
    rin = jnp.bitwise_and(lax.broadcasted_iota(jnp.int32, (rows, 1), 0), 7)
    sh = 1
    while sh < 8:
        keep = (rin < 8 - sh) if reverse else (rin >= sh)
        amount = rows - sh if reverse else sh
        a_sh = jnp.where(keep, pltpu.roll(a, amount, 0), 1.0)
        b_sh = jnp.where(keep, pltpu.roll(b, amount, 0), 0.0)
        b = b + a * b_sh
        a = a * a_sh
        sh *= 2
    out = [None] * (rows // 8)
    for g in (reversed(range(rows // 8)) if reverse else range(rows // 8)):
        hg = b[8 * g:8 * g + 8, :] + a[8 * g:8 * g + 8, :] * carry
        carry = hg[0:1, :] if reverse else hg[7:8, :]
        out[g] = hg
    return jnp.concatenate(out, axis=0)


def _lru_gates(xc, wa, wx, vec_ref, cs, sp):
    xcb = xc.astype(BF16)
    r = _sigmoid(jnp.dot(xcb, wa, preferred_element_type=F32) + vec_ref[2:3, cs])
    ig = _sigmoid(jnp.dot(xcb, wx, preferred_element_type=F32) + vec_ref[3:4, cs])
    log_a = (-LRU_C) * r * sp
    a = jnp.exp(log_a)
    a2 = a * a
    return xcb, r, ig, a, a2, _neg_expm1(2.0 * log_a, a2)


def mix_fwd(proj, pool_w, gate_a, gate_x, vecs, *, name, stages=()):
    Tp = proj.shape[0]
    D = proj.shape[1] // 5
    R = ROW_TILE
    nt = Tp // R
    H = D // HEAD_DIM
    PG = D // len(POOL_WINDOWS)

    def body(p_ref, pw_ref, wa_ref, wx_ref, vec_ref, m_ref, hs_ref, xc_ref, r_ref, ig_ref, a_ref, mu_ref, ge_ref, dge_ref,
             hist_p, hist_l, hcar, mtmp):
        i = pl.program_id(0)
        is_meta = i == 0

        @pl.when(is_meta)
        def _():
            hist_p[...] = jnp.zeros_like(hist_p)
            hist_l[...] = jnp.zeros_like(hist_l)
            hcar[...] = jnp.zeros_like(hcar)

        row, valid, t_log = _tile_masks(is_meta, R)

        for g, w in enumerate(POOL_WINDOWS):
            cs = slice(g * PG, (g + 1) * PG)
            v = p_ref[:, g * PG:(g + 1) * PG].astype(F32)
            ws = _window_sum_back(jnp.concatenate([hist_p[:, cs], v], axis=0), w)
            d = ws * _window_count_inv(t_log, w) - v
            y = jnp.dot(d.astype(BF16), pw_ref[g], preferred_element_type=F32)
            gp = p_ref[:, 3 * D + g * PG:3 * D + (g + 1) * PG].astype(F32)
            mtmp[:, cs] = _sigmoid(gp) * (y * vec_ref[0:1, cs])
            hist_p[:, cs] = v[R - HIST:, :]

        for h in range(H):
            cs = slice(h * HEAD_DIM, (h + 1) * HEAD_DIM)
            vl = p_ref[:, D + h * HEAD_DIM:D + (h + 1) * HEAD_DIM].astype(F32)
            xc, _, _, _ = _conv_taps(jnp.concatenate([hist_l[:, cs], vl], axis=0), vl, vec_ref, cs, R)
            sp = _softplus_neg(vec_ref[4:5, cs])
            xcb, r, ig, a, _, em = _lru_gates(xc, wa_ref[h], wx_ref[h], vec_ref, cs, sp)
            mult = jnp.sqrt(em)
            b = jnp.where(valid, mult * (ig * xc), 0.0)
            hs = _scan_rows(a, b, hcar[7:8, cs], reverse=False)
            hs_ref[:, cs] = hs
            xc_ref[:, cs], r_ref[:, cs], ig_ref[:, cs] = xcb, r.astype(BF16), ig.astype(BF16)
            a_ref[:, cs], mu_ref[:, cs] = a, mult
            hcar[:, cs] = hs[R - 8:, :]
            hist_l[:, cs] = vl[R - HIST:, :]
            vg = p_ref[:, 2 * D + h * HEAD_DIM:2 * D + (h + 1) * HEAD_DIM].astype(F32)
            ge, th = _gelu_tanh(vg)
            ge_ref[:, cs], dge_ref[:, cs] = ge.astype(BF16), _gelu_tanh_grad(vg, th).astype(BF16)
            gl = p_ref[:, 4 * D + h * HEAD_DIM:4 * D + (h + 1) * HEAD_DIM].astype(F32)
            m_ref[:, cs] = (mtmp[:, cs] + _sigmoid(gl) * (hs * ge)).astype(BF16)

    def tile(i):
        return (i + nt - 1) % nt

    full = lambda shape: pl.BlockSpec(shape, lambda i: (0,) * len(shape))
    outs, staged = _staged_call(
        body, grid=(nt,),
        in_specs=[pl.BlockSpec((R, 5 * D), lambda i: (tile(i), 0)), full(pool_w.shape), full(gate_a.shape), full(gate_x.shape),
                  full(vecs.shape)],
        out_specs=[pl.BlockSpec((R, D), lambda i: (tile(i), 0))] * 9,
        out_shape=[jax.ShapeDtypeStruct((Tp, D), dt) for dt in (BF16, F32, BF16, BF16, BF16, F32, F32, BF16, BF16)],
        scratch_shapes=[pltpu.VMEM((HIST, D), F32), pltpu.VMEM((HIST, D), F32), pltpu.VMEM((8, D), F32), pltpu.VMEM((R, D), F32)],
        name=name, semantics=("arbitrary",), inputs=[proj, pool_w, gate_a, gate_x, vecs], stages=stages)
    outs = [outs[0], outs[1], outs[2:]]
    return (outs, staged) if stages else outs


def mix_bwd(proj, hs, saved, dmerged, pool_w, gate_a, gate_x, vecs, *, name, stages=()):
    Tp = proj.shape[0]
    D = proj.shape[1] // 5
    R = ROW_TILE
    nt = Tp // R
    H = D // HEAD_DIM
    PG = D // len(POOL_WINDOWS)

    def body(p_ref, pprev_ref, hs_ref, hprev_ref, dm_ref, xc_ref, r_ref, ig_ref, a_ref, mu_ref, ge_ref, dge_ref, pw_ref, wa_ref, wx_ref, vec_ref,
             dp_ref, dpw_ref, dwa_ref, dwx_ref, dvec_ref, car_g, fut_dxc, fut_q):
        i = pl.program_id(0)
        is_meta = i == nt - 1

        @pl.when(i == 0)
        def _():
            dpw_ref[...] = jnp.zeros_like(dpw_ref)
            dwa_ref[...] = jnp.zeros_like(dwa_ref)
            dwx_ref[...] = jnp.zeros_like(dwx_ref)
            dvec_ref[...] = jnp.zeros_like(dvec_ref)
            car_g[...] = jnp.zeros_like(car_g)
            fut_dxc[...] = jnp.zeros_like(fut_dxc)
            fut_q[...] = jnp.zeros_like(fut_q)

        row, valid, t_log = _tile_masks(is_meta, R)
        keep_prev = jnp.logical_not(is_meta)

        def colsum(x):
            return jnp.sum(x, axis=0, keepdims=True)

        for g, w in enumerate(POOL_WINDOWS):
            cs = slice(g * PG, (g + 1) * PG)
            v = p_ref[:, g * PG:(g + 1) * PG].astype(F32)
            vprev = jnp.where(keep_prev, pprev_ref[:, g * PG:(g + 1) * PG].astype(F32), 0.0)
            inv_cnt = _window_count_inv(t_log, w)
            d = _window_sum_back(jnp.concatenate([vprev, v], axis=0), w) * inv_cnt - v
            d_bf = d.astype(BF16)
            y = jnp.dot(d_bf, pw_ref[g], preferred_element_type=F32)
            scale = vec_ref[0:1, cs]
            sg = _sigmoid(p_ref[:, 3 * D + g * PG:3 * D + (g + 1) * PG].astype(F32))
            dm = dm_ref[:, cs].astype(F32)
            dpo = dm * sg
            dp_ref[:, 3 * D + g * PG:3 * D + (g + 1) * PG] = (dm * (y * scale) * sg * (1.0 - sg)).astype(BF16)
            dvec_ref[0:1, cs] += colsum(dpo * y)
            dy = (dpo * scale).astype(BF16)
            dd = _dot_nt(dy, pw_ref[g])
            dpw_ref[g] += _dot_tn(d_bf, dy)
            q = dd * inv_cnt
            s, sh = jnp.concatenate([q, fut_q[:, cs]], axis=0), 1
            while sh < w:
                s = s + pltpu.roll(s, R + HIST - sh, 0)
                sh *= 2
            dp_ref[:, g * PG:(g + 1) * PG] = (s[:R, :] - dd).astype(BF16)
            fut_q[:, cs] = q[:HIST, :]

        for h in range(H):
            cs = slice(h * HEAD_DIM, (h + 1) * HEAD_DIM)
            pc = lambda blk: slice(blk * D + h * HEAD_DIM, blk * D + (h + 1) * HEAD_DIM)
            vl = p_ref[:, pc(1)].astype(F32)
            vlprev = jnp.where(keep_prev, pprev_ref[:, pc(1)].astype(F32), 0.0)
            x1, x2, x3 = _conv_shifts(jnp.concatenate([vlprev, vl], axis=0))
            lam = vec_ref[4:5, cs]
            sp = _softplus_neg(lam)
            xcb = xc_ref[:, cs]
            xc, r, ig = xcb.astype(F32), r_ref[:, cs].astype(F32), ig_ref[:, cs].astype(F32)
            a, mult = a_ref[:, cs], mu_ref[:, cs]
            a2 = a * a
            inv_mult = 1.0 / mult
            hsv = hs_ref[:, cs]
            hprev = jnp.where(row >= 1, pltpu.roll(hsv, 1, 0), hprev_ref[HIST - 1:HIST, cs])
            ge = ge_ref[:, cs].astype(F32)
            sgl = _sigmoid(p_ref[:, pc(4)].astype(F32))
            dm = dm_ref[:, cs].astype(F32)
            dlo = dm * sgl
            dp_ref[:, pc(4)] = (dm * (hsv * ge) * sgl * (1.0 - sgl)).astype(BF16)
            dp_ref[:, pc(2)] = (dlo * hsv * dge_ref[:, cs].astype(F32)).astype(BF16)
            a_next = jnp.where(row < R - 1, pltpu.roll(a, R - 1, 0), 1.0)
            G = _scan_rows(a_next, dlo * ge, car_g[0:1, cs], reverse=True)
            car_g[:, cs] = (a * G)[0:8, :]
            da = jnp.where(valid, G * hprev, 0.0)
            db = jnp.where(valid, G, 0.0)
            dmult = db * (ig * xc)
            dig = db * (mult * xc)
            dxc = db * (mult * ig)
            dlog_a = da * a - dmult * (a2 * inv_mult)
            dvec_ref[4:5, cs] += colsum(dlog_a * r) * (-LRU_C)
            dr = dlog_a * ((-LRU_C) * sp)
            dpa = dr * r * (1.0 - r)
            dpx = dig * ig * (1.0 - ig)
            dpa_bf = dpa.astype(BF16)
            dpx_bf = dpx.astype(BF16)
            dwa_ref[h] += _dot_tn(xcb, dpa_bf)
            dwx_ref[h] += _dot_tn(xcb, dpx_bf)
            dvec_ref[2:3, cs] += colsum(dpa)
            dvec_ref[3:4, cs] += colsum(dpx)
            dxc = dxc + _dot_nt(dpa_bf, wa_ref[h]) + _dot_nt(dpx_bf, wx_ref[h])
            ext = jnp.concatenate([dxc, fut_dxc[:, cs]], axis=0)
            n = R + HIST
            dvl = (vec_ref[8:9, cs] * dxc + vec_ref[7:8, cs] * pltpu.roll(ext, n - 1, 0)[:R, :]
                   + vec_ref[6:7, cs] * pltpu.roll(ext, n - 2, 0)[:R, :] + vec_ref[5:6, cs] * pltpu.roll(ext, n - 3, 0)[:R, :])
            dp_ref[:, pc(1)] = dvl.astype(BF16)
            dvec_ref[1:2, cs] += colsum(dxc)
            dvec_ref[8:9, cs] += colsum(dxc * vl)
            dvec_ref[7:8, cs] += colsum(dxc * x1)
            dvec_ref[6:7, cs] += colsum(dxc * x2)
            dvec_ref[5:6, cs] += colsum(dxc * x3)
            fut_dxc[:, cs] = dxc[:HIST, :]

            @pl.when(is_meta)
            def _():
                dvec_ref[4:5, cs] = dvec_ref[4:5, cs] * (-_sigmoid(-lam))

    def tile(i):
        return (2 * nt - 2 - i) % nt

    def prev_blk(i):
        per = R // HIST
        return jnp.where(i == nt - 1, 0, jnp.where(i == nt - 2, Tp // HIST - 1, (nt - 2 - i) * per - 1))

    full = lambda shape: pl.BlockSpec(shape, lambda i: (0,) * len(shape))
    G_ = len(POOL_WINDOWS)
    outs, staged = _staged_call(
        body, grid=(nt,),
        in_specs=[pl.BlockSpec((R, 5 * D), lambda i: (tile(i), 0)), pl.BlockSpec((HIST, 5 * D), lambda i: (prev_blk(i), 0)),
                  pl.BlockSpec((R, D), lambda i: (tile(i), 0)), pl.BlockSpec((HIST, D), lambda i: (prev_blk(i), 0)),
                  *[pl.BlockSpec((R, D), lambda i: (tile(i), 0))] * 8,
                  full(pool_w.shape), full(gate_a.shape), full(gate_x.shape), full(vecs.shape)],
        out_specs=[pl.BlockSpec((R, 5 * D), lambda i: (tile(i), 0)), full((G_, PG, PG)), full((H, HEAD_DIM, HEAD_DIM)),
                   full((H, HEAD_DIM, HEAD_DIM)), full((16, D))],
        out_shape=[jax.ShapeDtypeStruct((Tp, 5 * D), BF16), jax.ShapeDtypeStruct((G_, PG, PG), F32),
                   jax.ShapeDtypeStruct((H, HEAD_DIM, HEAD_DIM), F32), jax.ShapeDtypeStruct((H, HEAD_DIM, HEAD_DIM), F32),
                   jax.ShapeDtypeStruct((16, D), F32)],
        scratch_shapes=[pltpu.VMEM((8, D), F32), pltpu.VMEM((HIST, D), F32), pltpu.VMEM((HIST, D), F32)],
        name=name, semantics=("arbitrary",), inputs=[proj, proj, hs, hs, dmerged, *saved, pool_w, gate_a, gate_x, vecs], stages=stages)
    return (outs, staged) if stages else outs


def _adamw_math(w, g, m, v):
    mn = ADAM_B1 * m + (1.0 - ADAM_B1) * g
    vn = ADAM_B2 * v + (1.0 - ADAM_B2) * (g * g)
    m_hat = mn / (1.0 - ADAM_B1 ** ADAM_STEP)
    v_hat = vn / (1.0 - ADAM_B2 ** ADAM_STEP)
    return -ADAM_LR * (m_hat / (jnp.sqrt(v_hat) + ADAM_EPS) + ADAM_WD * w), mn, vn


def adamw(w, g, m, v, *, name):
    rows, cols = w.shape
    tr = _pick(rows, (256, 128, 64, 32, 16, 8))

    def body(w_ref, g_ref, m_ref, v_ref, go_ref, d_ref, mo_ref, vo_ref):
        gv = g_ref[...]
        go_ref[...] = gv
        d_ref[...], mo_ref[...], vo_ref[...] = _adamw_math(w_ref[...], gv, m_ref[...], v_ref[...])

    blk = pl.BlockSpec((tr, cols), lambda i: (i, 0))
    sds = jax.ShapeDtypeStruct((rows, cols), F32)
    return pl.pallas_call(body, grid=(rows // tr,), in_specs=[blk] * 4, out_specs=[blk] * 4, out_shape=[sds] * 4, name=name,
                          compiler_params=_params("parallel"))(w, g, m, v)


def allgather8(block, *, name, reduce_sum=False, stages=()):
    rows, cols = block.shape

    def body(x_ref, out_ref, *scratch):
        if reduce_sum:
            buf, send_sems, recv_sems, local_sem = scratch
        else:
            buf = out_ref
            send_sems, recv_sems, local_sem = scratch
        x, y, c, chips = _place()
        me, sibling = (x, y, c), (x, y, 1 - c)

        def slot(px, py, pc):
            return buf.at[4 * px + 2 * py + pc]

        def copy(k, blk, to, src=None):
            return pltpu.make_async_remote_copy(src_ref=slot(*blk) if src is None else src, dst_ref=slot(*blk),
                                                send_sem=send_sems.at[k], recv_sem=recv_sems.at[k], device_id=to, device_id_type=MESH)

        mine = pltpu.make_async_copy(x_ref, slot(*me), local_sem)
        mine.start()
        first = [copy(0, me, sibling, src=x_ref)]
        first += [copy(1 + j, me, (*chip, c), src=x_ref) for j, chip in enumerate(chips)]
        for cp in first:
            cp.start()
        passed = [copy(4 + j, (*chip, c), sibling) for j, chip in enumerate(chips)]
        for j, chip in enumerate(chips):
            copy(1 + j, (*chip, c), me).wait_recv()
            passed[j].start()
        copy(0, sibling, me).wait_recv()
        for j, chip in enumerate(chips):
            copy(4 + j, (*chip, 1 - c), me).wait_recv()
        for cp in first + passed:
            cp.wait_send()
        mine.wait()
        if reduce_sum:
            acc = buf[0]
            for d in range(1, 8):
                acc = acc + buf[d]
            out_ref[...] = acc

    sems = [pltpu.SemaphoreType.DMA((7,)), pltpu.SemaphoreType.DMA((7,)), pltpu.SemaphoreType.DMA]
    if reduce_sum:
        out_shape = jax.ShapeDtypeStruct((rows, cols), block.dtype)
        scratch = [pltpu.VMEM((8, rows, cols), block.dtype)] + sems
    else:
        out_shape = jax.ShapeDtypeStruct((8, rows, cols), block.dtype)
        scratch = sems
    vmem = pl.BlockSpec(memory_space=pltpu.VMEM)
    outs, staged = _staged_call(body, grid=(1,), in_specs=[vmem], out_specs=[vmem], out_shape=[out_shape], scratch_shapes=scratch,
                                name=name, semantics=("arbitrary",), inputs=[block], stages=stages)
    return (outs[0], staged) if stages else outs[0]


def cast_into_slot(w, chip_arr, *, name):
    _, r, cols = w.shape
    tr = _pick(r, (256, 128, 64, 32, 16))

    def body(chip_ref, w_ref, o_ref):
        del chip_ref
        o_ref[...] = w_ref[...].astype(BF16)

    grid_spec = pltpu.PrefetchScalarGridSpec(
        num_scalar_prefetch=1, grid=(2, r // tr),
        in_specs=[pl.BlockSpec((None, tr, cols), lambda h, i, chip: (h, i, 0))],
        out_specs=pl.BlockSpec((None, None, tr, cols), lambda h, i, chip: (chip[0], h, i, 0)))
    return pl.pallas_call(body, grid_spec=grid_spec, out_shape=jax.ShapeDtypeStruct((4, 2, r, cols), BF16), name=name,
                          compiler_params=_params("parallel", "parallel"))(chip_arr, w)


def proj_with_gather(u, bufs, order_arr, *, name):
    Tp, K = u.shape
    n = len(bufs)
    Ns = bufs[0].shape[3]
    tm = _pick(Tp, (1408, 512, 256, 128))
    tc = _pick(Ns, (512, 256, 128))
    n_rows = Tp // tm

    def body(order_ref, u_ref, *refs):
        del order_ref
        o_ref, outs = refs[n], refs[n + 1:2 * n + 1]
        wbuf, dir_send, dir_recv, rel_send, rel_recv, d2d_send, d2d_recv, load_sems = refs[2 * n + 1:]
        s, i = pl.program_id(0), pl.program_id(1)
        x, y, c, chips = _place()
        me = 2 * x + y
        (dx, dy) = chips[2]

        def direct(t, j, landing):
            px, py = chips[j]
            slot = outs[t].at[2 * px + py, c] if landing else outs[t].at[me, c]
            return _remote(slot, slot, dir_send.at[t, j], dir_recv.at[t, j], (px, py, c))

        def relay(t, j, landing):
            px, py = chips[j]
            ox, oy = chips[1 - j]
            rows = _relay_rows(outs[t], 2 * dx + dy, c, j) if landing else _relay_rows(outs[t], 2 * ox + oy, c, j)
            return _remote(rows, rows, rel_send.at[t, j], rel_recv.at[t, j], (px, py, c))

        def d2d(t, j, landing):
            px, py = chips[j]
            slot = outs[t].at[2 * px + py, 1 - c] if landing else outs[t].at[2 * px + py, c]
            return _remote(slot, slot, d2d_send.at[t, j], d2d_recv.at[t, j], (x, y, 1 - c))

        def load(chip_idx):
            parts = [pltpu.make_async_copy(outs[0].at[chip_idx, hh], wbuf.at[pl.ds(hh * (K // 2), K // 2), :], load_sems.at[hh])
                     for hh in range(2)]
            for cp in parts:
                cp.start()
            for cp in parts:
                cp.wait()

        def chip_of(j):
            px, py = chips[j]
            return 2 * px + py

        @pl.when(jnp.logical_and(s == 0, i == 0))
        def _():
            for t in range(n):
                for j in range(2):
                    direct(t, j, False).start()
            load(me)

        @pl.when(jnp.logical_and(s == 1, i == 0))
        def _():
            for j in range(2):
                direct(0, j, True).wait_recv()
            for j in range(2):
                relay(0, j, False).start()
                d2d(0, j, False).start()
            d2d(0, 0, True).wait_recv()
            load(chip_of(0))

        @pl.when(jnp.logical_and(s == 2, i == 0))
        def _():
            d2d(0, 1, True).wait_recv()
            for t in range(1, n):
                for j in range(2):
                    direct(t, j, True).wait_recv()
                for j in range(2):
                    relay(t, j, False).start()
                    d2d(t, j, False).start()
            load(chip_of(1))

        @pl.when(jnp.logical_and(s == 3, i == 0))
        def _():
            for j in range(2):
                relay(0, j, True).wait_recv()
            d2d(0, 2, False).start()
            d2d(0, 2, True).wait_recv()
            load(chip_of(2))

        uv = u_ref[...]
        for cc in range(Ns // tc):
            o_ref[:, cc * tc:(cc + 1) * tc] = jnp.dot(uv, wbuf[:, cc * tc:(cc + 1) * tc], preferred_element_type=F32).astype(BF16)

        @pl.when(jnp.logical_and(s == 3, i == n_rows - 1))
        def _():
            for t in range(1, n):
                for j in range(2):
                    relay(t, j, True).wait_recv()
                d2d(t, 2, False).start()
            for t in range(1, n):
                for j in range(3):
                    d2d(t, j, True).wait_recv()
            for t in range(n):
                for j in range(2):
                    direct(t, j, False).wait_send()
                    relay(t, j, False).wait_send()
                for j in range(3):
                    d2d(t, j, False).wait_send()

    grid_spec = pltpu.PrefetchScalarGridSpec(
        num_scalar_prefetch=1, grid=(4, n_rows),
        in_specs=[pl.BlockSpec((tm, K), lambda s, i, order: (i, 0)), *_any_specs(n)],
        out_specs=[pl.BlockSpec((tm, Ns), lambda s, i, order: (i, order[s])), *_any_specs(n)],
        scratch_shapes=[pltpu.VMEM((K, Ns), BF16), pltpu.SemaphoreType.DMA((n, 2)), pltpu.SemaphoreType.DMA((n, 2)),
                        pltpu.SemaphoreType.DMA((n, 2)), pltpu.SemaphoreType.DMA((n, 2)),
                        pltpu.SemaphoreType.DMA((n, 3)), pltpu.SemaphoreType.DMA((n, 3)), pltpu.SemaphoreType.DMA((2,))])
    res = pl.pallas_call(body, grid_spec=grid_spec, out_shape=[jax.ShapeDtypeStruct((Tp, 4 * Ns), BF16), *[_sds(b) for b in bufs]],
                         input_output_aliases={2 + t: 1 + t for t in range(n)}, name=name,
                         compiler_params=_params("arbitrary", "arbitrary"))(order_arr, u, *bufs)
    return res[0], list(res[1:])


def chip_presum(grad, recv, c_arr, *, name):
    _, _, r, cols = grad.shape
    tr = _pick(r, (256, 128, 64, 32, 16))

    def body(c_ref, g_ref, r_ref, o_ref):
        del c_ref
        o_ref[...] = (g_ref[...].astype(F32) + r_ref[...].astype(F32)).astype(BF16)

    grid_spec = pltpu.PrefetchScalarGridSpec(
        num_scalar_prefetch=1, grid=(4, r // tr),
        in_specs=[pl.BlockSpec((None, None, tr, cols), lambda k, i, c_ref: (k, c_ref[0], i, 0)),
                  pl.BlockSpec((None, tr, cols), lambda k, i, c_ref: (k, i, 0))],
        out_specs=pl.BlockSpec((None, tr, cols), lambda k, i, c_ref: (k, i, 0)))
    return pl.pallas_call(body, grid_spec=grid_spec, out_shape=jax.ShapeDtypeStruct((4, r, cols), BF16), name=name,
                          compiler_params=_params("parallel", "parallel"))(c_arr, grad, recv)


def final_half(grad, recv, got, mc_arr, *, name):
    _, _, r, cols = grad.shape
    tr = _pick(r, (256, 128, 64, 32, 16))

    def body(mc_ref, g_ref, r_ref, q_ref, o_ref):
        del mc_ref
        acc = g_ref[...].astype(F32) + r_ref[...].astype(F32)
        for j in range(3):
            acc = acc + q_ref[j].astype(F32)
        o_ref[...] = acc

    grid_spec = pltpu.PrefetchScalarGridSpec(
        num_scalar_prefetch=1, grid=(r // tr,),
        in_specs=[pl.BlockSpec((None, None, tr, cols), lambda i, mc: (mc[0], mc[1], i, 0)),
                  pl.BlockSpec((None, tr, cols), lambda i, mc: (mc[0], i, 0)),
                  pl.BlockSpec((3, tr, cols), lambda i, mc: (0, i, 0))],
        out_specs=pl.BlockSpec((tr, cols), lambda i, mc: (i, 0)))
    return pl.pallas_call(body, grid_spec=grid_spec, out_shape=jax.ShapeDtypeStruct((r, cols), F32), name=name,
                          compiler_params=_params("parallel"))(mc_arr, grad, recv, got)


def adamw_halves(w, mine, theirs, m, v, c_arr, *, name):
    _, r, cols = w.shape
    tr = _pick(r, (256, 128, 64, 32, 16, 8))

    def body(c_ref, w_ref, mine_ref, theirs_ref, m_ref, v_ref, go_ref, d_ref, mo_ref, vo_ref):
        gv = jnp.where(pl.program_id(0) == c_ref[0], mine_ref[...], theirs_ref[...])
        go_ref[...] = gv
        d_ref[...], mo_ref[...], vo_ref[...] = _adamw_math(w_ref[...], gv, m_ref[...], v_ref[...])

    blk = pl.BlockSpec((None, tr, cols), lambda h, i, c_ref: (h, i, 0))
    grid_spec = pltpu.PrefetchScalarGridSpec(
        num_scalar_prefetch=1, grid=(2, r // tr),
        in_specs=[blk, pl.BlockSpec((tr, cols), lambda h, i, c_ref: (jnp.where(h == c_ref[0], i, 0), 0)),
                  pl.BlockSpec((tr, cols), lambda h, i, c_ref: (jnp.where(h == c_ref[0], 0, i), 0)), blk, blk],
        out_specs=[blk] * 4)
    sds = jax.ShapeDtypeStruct((2, r, cols), F32)
    return pl.pallas_call(body, grid_spec=grid_spec, out_shape=[sds] * 4, name=name,
                          compiler_params=_params("parallel", "parallel"))(c_arr, w, mine, theirs, m, v)


def _pad_rows(a, rows):
    return jnp.pad(a, ((0, rows - a.shape[0]), (0, 0)))


def kernel(x, meta_tokens, norm1_g, w_in, pool_w, pool_scale, conv_w, conv_b, gate_a_w, gate_a_b, gate_x_w, gate_x_b, lru_lambda, w_out, norm2_g, mlp_w1, mlp_w2, final_g, loss_target, m_meta_tokens, m_norm1_g, m_w_in, m_pool_w, m_pool_scale, m_conv_w, m_conv_b, m_gate_a_w, m_gate_a_b, m_gate_x_w, m_gate_x_b, m_lru_lambda, m_w_out, m_norm2_g, m_mlp_w1, m_mlp_w2, m_final_g, v_meta_tokens, v_norm1_g, v_w_in, v_pool_w, v_pool_scale, v_conv_w, v_conv_b, v_gate_a_w, v_gate_a_b, v_gate_x_w, v_gate_x_b, v_lru_lambda, v_w_out, v_norm2_g, v_mlp_w1, v_mlp_w2, v_final_g):
    D = x.shape[-1]
    H = D // HEAD_DIM
    G = len(POOL_WINDOWS)
    PG = D // G
    ax, ay, ac = lax.axis_index("x"), lax.axis_index("y"), lax.axis_index("c")
    chip = 2 * ax + ay
    dshard = D // 4

    c_arr = jnp.reshape(ac, (1,)).astype(jnp.int32)
    chip_arr = jnp.reshape(chip, (1,)).astype(jnp.int32)
    mc_arr = jnp.stack([chip, ac]).astype(jnp.int32)
    names = ["w_in", "pool_w", "gate_a_w", "gate_x_w", "w_out", "mlp_w1", "mlp_w2"]
    big = [w_in, pool_w, gate_a_w, gate_x_w, w_out, mlp_w1, mlp_w2]

    def halves(w):
        w2d = w.reshape(-1, w.shape[-1])
        return w2d.reshape(2, w2d.shape[0] // 2, w2d.shape[1])

    bufs = [cast_into_slot(halves(w), chip_arr, name="cast_" + nm) for w, nm in zip(big, names)]
    order_arr = jnp.stack([chip, 2 * (1 - ax) + ay, 2 * ax + (1 - ay), 2 * (1 - ax) + (1 - ay)]).astype(jnp.int32)

    small_in = jnp.concatenate([meta_tokens, _pad_rows(conv_w[0], 8), _pad_rows(gate_a_b.reshape(1, dshard), 8),
                                _pad_rows(gate_x_b.reshape(1, dshard), 8)], axis=0)
    sm = allgather8(small_in, name="gather_small")[0::2]
    meta_f = sm[:, 0:16].transpose(1, 0, 2).reshape(N_META, D)
    conv_w_f = sm[:, 16:20].transpose(1, 0, 2).reshape(4, D)
    hd4 = HEAD_DIM // 4
    ba_f = sm[:, 24].reshape(4, H, hd4).transpose(1, 0, 2).reshape(1, D)
    bx_f = sm[:, 32].reshape(4, H, hd4).transpose(1, 0, 2).reshape(1, D)
    vecs = jnp.zeros((16, D), F32)
    for r0, part in ((0, pool_scale), (1, conv_b), (2, ba_f), (3, bx_f), (4, lru_lambda), (5, conv_w_f)):
        vecs = lax.dynamic_update_slice(vecs, part, (r0, 0))

    def chipwise(g, n_blocks, rows):
        return g.reshape(n_blocks, 4, rows, g.shape[-1]).transpose(1, 0, 2, 3).reshape(4, n_blocks * rows, g.shape[-1])

    def split2(g):
        return g.reshape(4, 2, g.shape[1] // 2, g.shape[2])

    def presum(t, g, r):
        return chip_presum(g, r, c_arr, name="presum_" + names[t])

    def total(t, g, r, q):
        return final_half(g, r, q, mc_arr, name="sum_" + names[t])

    xs, target, gfin = x[0], loss_target[0], final_g.reshape(1, D)
    meta_tile = jnp.concatenate([jnp.zeros((ROW_TILE - N_META, D), F32), meta_f], axis=0)
    h0, u, u_t = rmsnorm_fwd_input(xs, meta_tile, norm1_g, name="norm1")
    proj, (b_win, b_pool, b_ga, b_gx, b_wout) = proj_with_gather(u, bufs[0:5], order_arr, name="proj")
    w_in_f = b_win.reshape(4, w_in.shape[1], w_in.shape[2])
    pool_f = b_pool.reshape(4, G, PG // 4, PG).transpose(1, 0, 2, 3).reshape(G, PG, PG)
    ga_f = b_ga.reshape(4, H, HEAD_DIM // 4, HEAD_DIM).transpose(1, 0, 2, 3).reshape(H, HEAD_DIM, HEAD_DIM)
    gx_f = b_gx.reshape(4, H, HEAD_DIM // 4, HEAD_DIM).transpose(1, 0, 2, 3).reshape(H, HEAD_DIM, HEAD_DIM)
    w_out_f = b_wout.reshape(4 * w_out.shape[1], w_out.shape[2])
    (merged, hs, saved), [[b_w1, b_w2]] = mix_fwd(proj, pool_f, ga_f, gx_f, vecs, name="mix_fwd",
                                                  stages=[stage_gather_direct([bufs[5], bufs[6]])])
    h1, [[b_w1]] = mm_nn(merged, w_out_f, out_dtype=F32, name="out_proj", epilogue=lambda r, res: r + res, extras=(h0,),
                         stages=[stage_gather_relay([b_w1])])
    (u2, u2_t), [[b_w1]] = rmsnorm_fwd(h1, norm2_g, name="norm2", stages=[stage_gather_d2d([b_w1])])
    w1_f = b_w1.reshape(4, mlp_w1.shape[1], mlp_w1.shape[2])
    a1, [[b_w2]] = mm_nn(u2, w1_f, out_dtype=BF16, name="mlp_up", tiles=(None, min(1024, D), None),
                         stages=[stage_gather_relay([b_w2])])
    [[b_w2]] = comm_call([stage_gather_d2d([b_w2])], name="w2_to_sibling")
    w2_f = b_w2.reshape(4 * mlp_w2.shape[1], mlp_w2.shape[2])
    Tp = h0.shape[0]
    h2 = mm_nn(a1, w2_f, out_dtype=F32, name="mlp_down", a_pro=_relu_sq, epilogue=lambda r, res: r + res, extras=(h1,),
               tiles=(_pick(Tp, (704, 512, 256, 128)), min(256, D), a1.shape[1]))
    dh2_bf, st_f = final_loss(h2, target, gfin, name="final_loss")

    da1 = mm_nt(dh2_bf, w2_f, out_dtype=BF16, name="mlp_down_dx",
                epilogue=lambda r, a: r * (2.0 * jnp.maximum(a.astype(F32), 0.0)), extras=(a1,))
    d_w2 = mm_tn(a1, dh2_bf, shards=1, out_dtype=BF16, name="mlp_down_dw", a_pro=_relu_sq, tiles=(512, min(1024, D), Tp))
    g6 = split2(d_w2.reshape(4, -1, d_w2.shape[-1]))
    d_w1, [[r6]] = mm_nn(u2_t, da1, out_dtype=BF16, name="mlp_up_dw", out_shards=4, tiles=(min(1024, D), min(1024, D), Tp),
                         stages=[stage_to_sibling([g6])])
    g5 = split2(d_w1)
    p6 = presum(6, g6, r6)
    du2, [[q6_near], [r5]] = mm_nt(da1, w1_f, out_dtype=BF16, name="mlp_up_dx",
                                  stages=[stage_to_chips([p6], peers=(0, 1)), stage_to_sibling([g5])])
    p5 = presum(5, g5, r5)
    dh1_bf, st_2 = rms_bwd(h1, norm2_g, du2, dh2_bf, name="norm2_bwd")
    dmerged = mm_nt(dh1_bf, w_out_f, out_dtype=BF16, name="out_proj_dx")
    d_wout = mm_tn(merged, dh1_bf, shards=1, out_dtype=BF16, name="out_proj_dw", tiles=(512, min(1024, D), Tp))
    g4 = split2(d_wout.reshape(4, -1, d_wout.shape[-1]))
    (dproj, d_pool, d_ga, d_gx, d_vecs), [[q6], [q5], [r4]] = mix_bwd(
        proj, hs, saved, dmerged, pool_f, ga_f, gx_f, vecs, name="mix_bwd",
        stages=[stage_to_chips([p6], peers=(2,), into=[q6_near]), stage_to_chips([p5]), stage_to_sibling([g4])])
    p4 = presum(4, g4, r4)
    f5, f6 = total(5, g5, r5, q5), total(6, g6, r6, q6)
    g1, g2, g3 = split2(chipwise(d_pool, G, PG // 4)), split2(chipwise(d_ga, H, hd4)), split2(chipwise(d_gx, H, hd4))
    half_k = u_t.shape[0] // 2
    whole_k = (half_k, _pick(dproj.shape[1] // 4, (1280, 512, 256, 128)), Tp)
    u_theirs = lax.dynamic_slice_in_dim(u_t, (1 - ac) * half_k, half_k, axis=0)
    u_mine = lax.dynamic_slice_in_dim(u_t, ac * half_k, half_k, axis=0)
    g_theirs, [[r1, r2, r3], [o5, o6], [q4]] = mm_nn(u_theirs, dproj, out_dtype=BF16, out_shards=4, tiles=whole_k, name="proj_dw_sibling_rows",
                                                    stages=[stage_to_sibling([g1, g2, g3]), stage_from_sibling([f5, f6]), stage_to_chips([p4])])
    p1, p2, p3 = presum(1, g1, r1), presum(2, g2, r2), presum(3, g3, r3)
    g_mine, [[r0], [q1, q2, q3]] = mm_nn(u_mine, dproj, out_dtype=BF16, out_shards=4, tiles=whole_k, name="proj_dw_own_rows",
                                         stages=[stage_from_sibling([g_theirs]), stage_to_chips([p1, p2, p3])])
    g0 = g_mine[:, None]
    p0 = chip_presum(g0, r0, jnp.zeros((1,), jnp.int32), name="presum_w_in")
    f4 = total(4, g4, r4, q4)
    du, [[q0], [o4]] = mm_nt(dproj, w_in_f, out_dtype=BF16, name="proj_dx", tiles=(None, None, w_in_f.shape[2]),
                             stages=[stage_to_chips([p0]), stage_from_sibling([f4])])
    f0 = final_half(g0, r0, q0, jnp.stack([chip, 0]).astype(jnp.int32), name="sum_w_in")
    f1, f2, f3 = total(1, g1, r1, q1), total(2, g2, r2, q2), total(3, g3, r3, q3)
    grad_x, d_meta, st_1 = rms_bwd_input(h0, norm1_g, du, dh1_bf, name="norm1_bwd")

    small = jnp.concatenate([d_meta, d_vecs, st_1, st_2, st_f], axis=0)
    tot, [[o0, o1, o2, o3]] = allgather8(small, name="sum_small", reduce_sum=True, stages=[stage_from_sibling([f0, f1, f2, f3])])
    mine = [f0, f1, f2, f3, f4, f5, f6]
    theirs = [o0, o1, o2, o3, o4, o5, o6]
    loss = jnp.sum(tot[49])
    g_meta = lax.dynamic_slice_in_dim(tot[0:16], chip * dshard, dshard, axis=1)
    g_pool_scale, g_conv_b, g_lam = tot[16:17], tot[17:18], tot[20:21]
    g_ba = lax.dynamic_slice_in_dim(tot[18].reshape(H, HEAD_DIM), chip * hd4, hd4, axis=1)[None]
    g_bx = lax.dynamic_slice_in_dim(tot[19].reshape(H, HEAD_DIM), chip * hd4, hd4, axis=1)[None]
    g_conv_w = lax.dynamic_slice_in_dim(tot[21:25], chip * dshard, dshard, axis=1)[None]
    g_n1, g_n2, g_fin = tot[32:33], tot[40:41], tot[48]

    def step(w, g, m, v, nm):
        cols = w.shape[-1]
        outs = adamw(w.reshape(-1, cols), g.reshape(-1, cols), m.reshape(-1, cols), v.reshape(-1, cols), name="adamw_" + nm)
        return [o.reshape(w.shape) for o in outs]

    def step_big(t, w, m, v):
        outs = adamw_halves(halves(w), mine[t], theirs[t], halves(m), halves(v), c_arr, name="adamw_" + names[t])
        return [o.reshape(w.shape) for o in outs]

    res = dict(meta_tokens=step(meta_tokens, g_meta, m_meta_tokens, v_meta_tokens, "meta_tokens"),
               norm1_g=step(norm1_g, g_n1, m_norm1_g, v_norm1_g, "norm1_g"),
               w_in=step_big(0, w_in, m_w_in, v_w_in), pool_w=step_big(1, pool_w, m_pool_w, v_pool_w),
               pool_scale=step(pool_scale, g_pool_scale, m_pool_scale, v_pool_scale, "pool_scale"),
               conv_w=step(conv_w, g_conv_w, m_conv_w, v_conv_w, "conv_w"), conv_b=step(conv_b, g_conv_b, m_conv_b, v_conv_b, "conv_b"),
               gate_a_w=step_big(2, gate_a_w, m_gate_a_w, v_gate_a_w), gate_a_b=step(gate_a_b, g_ba, m_gate_a_b, v_gate_a_b, "gate_a_b"),
               gate_x_w=step_big(3, gate_x_w, m_gate_x_w, v_gate_x_w), gate_x_b=step(gate_x_b, g_bx, m_gate_x_b, v_gate_x_b, "gate_x_b"),
               lru_lambda=step(lru_lambda, g_lam, m_lru_lambda, v_lru_lambda, "lru_lambda"), w_out=step_big(4, w_out, m_w_out, v_w_out),
               norm2_g=step(norm2_g, g_n2, m_norm2_g, v_norm2_g, "norm2_g"), mlp_w1=step_big(5, mlp_w1, m_mlp_w1, v_mlp_w1),
               mlp_w2=step_big(6, mlp_w2, m_mlp_w2, v_mlp_w2), final_g=step(final_g, g_fin, m_final_g, v_final_g, "final_g"))
    order = list(res)
    return (loss, grad_x[None], *[res[n][0] for n in order], *[res[n][1] for n in order], *[res[n][2] for n in order],
            *[res[n][3] for n in order])
```

```python
import functools

import jax
import jax.numpy as jnp
from jax import lax
from jax.experimental import pallas as pl
from jax.experimental.pallas import tpu as pltpu

F32 = jnp.float32
BF16 = jnp.bfloat16
MESH = pl.DeviceIdType.MESH

NORM_EPS = 1e-6
N_META = 16
HEAD_DIM = 256
POOL_WINDOWS = (2, 4, 8, 16)
LRU_C = 8.0
ROW_TILE = 128
HIST = 16
VMEM_LIMIT_BYTES = 56 * 1024 * 1024
ADAM_LR, ADAM_B1, ADAM_B2, ADAM_EPS, ADAM_WD, ADAM_STEP = 0.001, 0.9, 0.999, 1e-08, 0.01, 10


def _pick(n, prefs):
    for p in prefs:
        if n % p == 0:
            return p
    return n


def _params(*sem):
    return pltpu.CompilerParams(dimension_semantics=sem, vmem_limit_bytes=VMEM_LIMIT_BYTES)


def _sigmoid(x):
    return 1.0 / (1.0 + jnp.exp(-x))


def _gelu_tanh(x):
    t = jnp.tanh(0.7978845608028654 * (x + 0.044715 * (x * x * x)))
    return 0.5 * x * (1.0 + t), t


def _gelu_tanh_grad(x, t):
    return 0.5 * (1.0 + t) + 0.5 * x * (1.0 - t * t) * (0.7978845608028654 * (1.0 + 3.0 * 0.044715 * x * x))


def _neg_expm1(x, exp_x):
    series = x * (-1.0 + x * (-0.5 + x * ((-1.0 / 6.0) + x * ((-1.0 / 24.0) + x * (-1.0 / 120.0)))))
    return jnp.where(x > -0.125, series, 1.0 - exp_x)


def _softplus_neg(lam):
    z = jnp.exp(-jnp.abs(lam))
    log1p_z = jnp.where(z < 0.01, z * (1.0 - z * (0.5 - z * (1.0 / 3.0))), jnp.log(1.0 + z))
    return jnp.maximum(-lam, 0.0) + log1p_z


def _tile_masks(is_meta, rows):
    row = lax.broadcasted_iota(jnp.int32, (rows, 1), 0)
    valid = jnp.logical_or(jnp.logical_not(is_meta), row >= rows - N_META)
    t_log = jnp.where(is_meta, row - (rows - N_META), 1 << 20)
    return row, valid, t_log


def _window_count_inv(t_log, w):
    return 1.0 / jnp.clip(t_log + 1, 1, w).astype(F32)


def _dot_nt(a, b):
    return lax.dot_general(a, b, (((1,), (1,)), ((), ())), preferred_element_type=F32)


def _dot_tn(a, b):
    return lax.dot_general(a, b, (((0,), (0,)), ((), ())), preferred_element_type=F32)


def _place():
    x, y, c = lax.axis_index("x"), lax.axis_index("y"), lax.axis_index("c")
    chips = [(1 - x, y), (x, 1 - y), (1 - x, 1 - y)]
    return x, y, c, chips


def _remote(src, dst, send_sem, recv_sem, to):
    return pltpu.make_async_remote_copy(src_ref=src, dst_ref=dst, send_sem=send_sem, recv_sem=recv_sem, device_id=to,
                                        device_id_type=MESH)


class Stage:
    def __init__(self, arrays, out_shapes, aliases, n_copies, copies):
        self.arrays, self.out_shapes, self.aliases, self.n_copies, self.copies = list(arrays), list(out_shapes), aliases, n_copies, copies


def _sds(a):
    return jax.ShapeDtypeStruct(a.shape, a.dtype)


def _relay_rows(buf_ref, chip_idx, c, quarter):
    rows = buf_ref.shape[2] // 2
    return buf_ref.at[chip_idx, c, pl.ds(quarter * rows, rows)]


def stage_gather_direct(bufs):
    n = len(bufs)

    def copies(ins, outs, send, recv):
        x, y, c, chips = _place()
        me = 2 * x + y
        sends, recvs = [], []
        for t in range(n):
            for j, (px, py) in enumerate(chips[:2]):
                k = 2 * t + j
                mine, theirs = outs[t].at[me, c], outs[t].at[2 * px + py, c]
                sends.append(_remote(mine, mine, send.at[k], recv.at[k], (px, py, c)))
                recvs.append(_remote(theirs, theirs, send.at[k], recv.at[k], (px, py, c)))
        return sends, recvs

    return Stage(bufs, [_sds(b) for b in bufs], {t: t for t in range(n)}, 2 * n, copies)


def stage_gather_relay(bufs):
    n = len(bufs)

    def copies(ins, outs, send, recv):
        x, y, c, chips = _place()
        (xx, xy), (yx, yy), (dx, dy) = chips
        sends, recvs = [], []
        for t in range(n):
            from_y, from_x = _relay_rows(outs[t], 2 * yx + yy, c, 0), _relay_rows(outs[t], 2 * xx + xy, c, 1)
            sends.append(_remote(from_y, from_y, send.at[2 * t], recv.at[2 * t], (xx, xy, c)))
            sends.append(_remote(from_x, from_x, send.at[2 * t + 1], recv.at[2 * t + 1], (yx, yy, c)))
            for q, (px, py) in enumerate(chips[:2]):
                got = _relay_rows(outs[t], 2 * dx + dy, c, q)
                recvs.append(_remote(got, got, send.at[2 * t + q], recv.at[2 * t + q], (px, py, c)))
        return sends, recvs

    return Stage(bufs, [_sds(b) for b in bufs], {t: t for t in range(n)}, 2 * n, copies)


def stage_gather_d2d(bufs, peers=(0, 1, 2)):
    n = len(bufs)

    def copies(ins, outs, send, recv):
        x, y, c, chips = _place()
        sends, recvs = [], []
        for t in range(n):
            for slot, j in enumerate(peers):
                px, py = chips[j]
                k = len(peers) * t + slot
                got, sib = outs[t].at[2 * px + py, c], outs[t].at[2 * px + py, 1 - c]
                sends.append(_remote(got, got, send.at[k], recv.at[k], (x, y, 1 - c)))
                recvs.append(_remote(sib, sib, send.at[k], recv.at[k], (x, y, 1 - c)))
        return sends, recvs

    return Stage(bufs, [_sds(b) for b in bufs], {t: t for t in range(n)}, len(peers) * n, copies)


class _SemsFrom:
    def __init__(self, sems, first):
        self.sems, self.first = sems, first

    @property
    def at(self):
        return self

    def __getitem__(self, k):
        return self.sems.at[self.first + k]


def stage_both(first, second):
    def copies(ins, outs, send, recv):
        s1, r1 = first.copies(ins, outs, send, recv)
        s2, r2 = second.copies(ins, outs, _SemsFrom(send, first.n_copies), _SemsFrom(recv, first.n_copies))
        return s1 + s2, r1 + r2

    return Stage(first.arrays, first.out_shapes, first.aliases, first.n_copies + second.n_copies, copies)


def stage_to_sibling(grads):
    n = len(grads)

    def copies(ins, outs, send, recv):
        x, y, c, _ = _place()
        sends, recvs = [], []
        for t in range(n):
            for k4 in range(4):
                k = 4 * t + k4
                sends.append(_remote(ins[t].at[k4, 1 - c], outs[t].at[k4], send.at[k], recv.at[k], (x, y, 1 - c)))
                recvs.append(_remote(outs[t].at[k4], outs[t].at[k4], send.at[k], recv.at[k], (x, y, 1 - c)))
        return sends, recvs

    return Stage(grads, [jax.ShapeDtypeStruct((4, *g.shape[2:]), g.dtype) for g in grads], {}, 4 * n, copies)


def stage_to_chips(presums, peers=(0, 1, 2), into=None):
    n = len(presums)

    def copies(ins, outs, send, recv):
        x, y, c, chips = _place()
        sends, recvs = [], []
        for t in range(n):
            for slot, j in enumerate(peers):
                px, py = chips[j]
                k = len(peers) * t + slot
                sends.append(_remote(ins[t].at[2 * px + py], outs[t].at[j], send.at[k], recv.at[k], (px, py, c)))
                recvs.append(_remote(outs[t].at[j], outs[t].at[j], send.at[k], recv.at[k], (px, py, c)))
        return sends, recvs

    out_shapes = [jax.ShapeDtypeStruct((3, *p.shape[1:]), p.dtype) for p in presums]
    if into is None:
        return Stage(presums, out_shapes, {}, len(peers) * n, copies)
    return Stage([*presums, *into], out_shapes, {n + t: t for t in range(n)}, len(peers) * n, copies)


def stage_from_sibling(halves):
    n = len(halves)

    def copies(ins, outs, send, recv):
        x, y, c, _ = _place()
        sends = [_remote(ins[t], outs[t], send.at[t], recv.at[t], (x, y, 1 - c)) for t in range(n)]
        recvs = [_remote(outs[t], outs[t], send.at[t], recv.at[t], (x, y, 1 - c)) for t in range(n)]
        return sends, recvs

    return Stage(halves, [_sds(h) for h in halves], {}, n, copies)


def _any_specs(n):
    return [pl.BlockSpec(memory_space=pl.ANY)] * n


def _staged_call(body, *, grid, in_specs, out_specs, out_shape, scratch_shapes, name, semantics, inputs, stages=()):
    n_in, n_out, n_scr = len(in_specs), len(out_specs), len(scratch_shapes)
    st_arrays = [a for s in stages for a in s.arrays]
    st_outs = [o for s in stages for o in s.out_shapes]
    st_sems = [pltpu.SemaphoreType.DMA((s.n_copies,)) for s in stages for _ in range(2)]
    aliases = {}
    at_in, at_out = n_in, n_out
    for s in stages:
        for a, o in s.aliases.items():
            aliases[at_in + a] = at_out + o
        at_in += len(s.arrays)
        at_out += len(s.out_shapes)

    def full_body(*refs):
        pos = 0

        def take(count):
            nonlocal pos
            part = refs[pos:pos + count]
            pos += count
            return part

        ins, s_ins, outs, s_outs, scr = take(n_in), take(len(st_arrays)), take(n_out), take(len(st_outs)), take(n_scr)
        s_sems = refs[pos:]

        def each_stage(action):
            at_i = at_o = 0
            for idx, s in enumerate(stages):
                sends, recvs = s.copies(s_ins[at_i:at_i + len(s.arrays)], s_outs[at_o:at_o + len(s.out_shapes)],
                                        s_sems[2 * idx], s_sems[2 * idx + 1])
                action(sends, recvs)
                at_i += len(s.arrays)
                at_o += len(s.out_shapes)

        if stages:
            ids = [pl.program_id(a) for a in range(len(grid))]
            first = functools.reduce(jnp.logical_and, [i == 0 for i in ids])
            last = functools.reduce(jnp.logical_and, [i == g - 1 for i, g in zip(ids, grid)])

            def start(sends, recvs):
                for cp in sends:
                    cp.start()

            def finish(sends, recvs):
                for cp in recvs:
                    cp.wait_recv()
                for cp in sends:
                    cp.wait_send()

            @pl.when(first)
            def _():
                each_stage(start)

        body(*ins, *outs, *scr)

        if stages:
            @pl.when(last)
            def _():
                each_stage(finish)

    sem = tuple("arbitrary" for _ in grid) if stages else tuple(semantics)
    res = pl.pallas_call(
        full_body, grid=grid, in_specs=[*in_specs, *_any_specs(len(st_arrays))], out_specs=[*out_specs, *_any_specs(len(st_outs))],
        out_shape=[*out_shape, *st_outs], scratch_shapes=[*scratch_shapes, *st_sems], input_output_aliases=aliases, name=name,
        compiler_params=_params(*sem))(*inputs, *st_arrays)
    outs, rest = list(res[:n_out]), list(res[n_out:])
    per_stage = []
    for s in stages:
        per_stage.append(rest[:len(s.out_shapes)])
        rest = rest[len(s.out_shapes):]
    return outs, per_stage


def comm_call(stages, *, name):
    return _staged_call(lambda: None, grid=(1,), in_specs=[], out_specs=[], out_shape=[], scratch_shapes=[], name=name,
                        semantics=("arbitrary",), inputs=[], stages=stages)[1]


def _matmul(kind, a, b, *, grid, a_spec, b_spec, out_spec, out_shape, acc_shape, name,
            a_pro=None, epilogue=None, extras=(), extra_specs=(), stages=()):
    nk = grid[2]
    n_extra = len(extras)

    def body(a_ref, b_ref, *rest):
        extra_refs = rest[:n_extra]
        o_ref = rest[n_extra]
        av = a_ref[...]
        if a_pro is not None:
            av = a_pro(av)
        av = av.astype(BF16)
        bv = b_ref[...].astype(BF16)
        if kind == "nn":
            p = jnp.dot(av, bv, preferred_element_type=F32)
        elif kind == "nt":
            p = _dot_nt(av, bv)
        else:
            p = _dot_tn(av, bv)

        def finish(r):
            if epilogue is not None:
                r = epilogue(r, *[e[...] for e in extra_refs])
            o_ref[...] = r.astype(o_ref.dtype)

        if nk == 1:
            finish(p)
        else:
            acc_ref = rest[n_extra + 1]
            k = pl.program_id(2)

            @pl.when(k == 0)
            def _():
                acc_ref[...] = p

            @pl.when(k > 0)
            def _():
                acc_ref[...] += p

            @pl.when(k == nk - 1)
            def _():
                finish(acc_ref[...])

    scratch = [] if nk == 1 else [pltpu.VMEM(acc_shape, F32)]
    outs, staged = _staged_call(body, grid=grid, in_specs=[a_spec, b_spec, *extra_specs], out_specs=[out_spec], out_shape=[out_shape],
                                scratch_shapes=scratch, name=name, semantics=("parallel", "parallel", "arbitrary"),
                                inputs=[a, b, *extras], stages=stages)
    return (outs[0], staged) if stages else outs[0]


def mm_nn(a, b, *, out_dtype, name, a_pro=None, epilogue=None, extras=(), stages=(), out_shards=1, tiles=(None, None, None)):
    M, K = a.shape
    sharded = b.ndim == 3
    ns = b.shape[2] if sharded else b.shape[1]
    N = ns * b.shape[0] if sharded else ns
    if out_shards > 1:
        ns = N // out_shards
    tm = tiles[0] or _pick(M, (1408, 1024, 512, 256, 128))
    tn = tiles[1] or _pick(ns, (512, 256, 128))
    tk = tiles[2] or _pick(K, (2048, 1408, 1024, 512, 256, 128))
    per = ns // tn
    if sharded:
        b_spec = pl.BlockSpec((None, tk, tn), lambda i, j, k: (j // per, k, j % per))
    else:
        b_spec = pl.BlockSpec((tk, tn), lambda i, j, k: (k, j))
    mn = pl.BlockSpec((tm, tn), lambda i, j, k: (i, j))
    if out_shards > 1:
        out_spec = pl.BlockSpec((None, tm, tn), lambda i, j, k: (j // per, i, j % per))
        out_shape = jax.ShapeDtypeStruct((out_shards, M, ns), out_dtype)
    else:
        out_spec, out_shape = mn, jax.ShapeDtypeStruct((M, N), out_dtype)
    return _matmul("nn", a, b, grid=(M // tm, N // tn, K // tk), a_spec=pl.BlockSpec((tm, tk), lambda i, j, k: (i, k)),
                   b_spec=b_spec, out_spec=out_spec, out_shape=out_shape, acc_shape=(tm, tn),
                   name=name, a_pro=a_pro, epilogue=epilogue, extras=extras, extra_specs=[mn] * len(extras), stages=stages)


def mm_nt(a, w, *, out_dtype, name, epilogue=None, extras=(), stages=(), tiles=(None, None, None)):
    M, N = a.shape
    sharded = w.ndim == 3
    kw = w.shape[1] if sharded else w.shape[0]
    ns = w.shape[2] if sharded else w.shape[1]
    tm = tiles[0] or _pick(M, (1408, 512, 256, 128))
    tkw = tiles[1] or _pick(kw, (1024, 512, 256, 128))
    tk = tiles[2] or _pick(ns, (2048, 1280, 1024, 512, 256, 128))
    per = ns // tk
    if sharded:
        w_spec = pl.BlockSpec((None, tkw, tk), lambda i, j, k: (k // per, j, k % per))
    else:
        w_spec = pl.BlockSpec((tkw, tk), lambda i, j, k: (j, k))
    mo = pl.BlockSpec((tm, tkw), lambda i, j, k: (i, j))
    return _matmul("nt", a, w, grid=(M // tm, kw // tkw, N // tk), a_spec=pl.BlockSpec((tm, tk), lambda i, j, k: (i, k)),
                   b_spec=w_spec, out_spec=mo, out_shape=jax.ShapeDtypeStruct((M, kw), out_dtype), acc_shape=(tm, tkw),
                   name=name, epilogue=epilogue, extras=extras, extra_specs=[mo] * len(extras), stages=stages)


def mm_tn(a, g, *, shards, name, out_dtype, a_pro=None, stages=(), tiles=(None, None, None)):
    T, kw = a.shape
    N = g.shape[1]
    ns = N // shards
    tkw = tiles[0] or _pick(kw, (1024, 512, 256, 128))
    tn = tiles[1] or _pick(ns, (1280, 1024, 512, 256, 128))
    tt = tiles[2] or _pick(T, (1408, 512, 256, 128))
    per = ns // tn
    if shards > 1:
        out_spec = pl.BlockSpec((None, tkw, tn), lambda i, j, k: (j // per, i, j % per))
        out_shape = jax.ShapeDtypeStruct((shards, kw, ns), out_dtype)
    else:
        out_spec = pl.BlockSpec((tkw, tn), lambda i, j, k: (i, j))
        out_shape = jax.ShapeDtypeStruct((kw, N), out_dtype)
    return _matmul("tn", a, g, grid=(kw // tkw, N // tn, T // tt), a_spec=pl.BlockSpec((tt, tkw), lambda i, j, k: (k, i)),
                   b_spec=pl.BlockSpec((tt, tn), lambda i, j, k: (k, j)), out_spec=out_spec, out_shape=out_shape,
                   acc_shape=(tkw, tn), name=name, a_pro=a_pro, stages=stages)


def _relu_sq(a):
    r = jnp.maximum(a, 0.0)
    return r * r


def rmsnorm_fwd(h, g, *, name, stages=()):
    Tp, D = h.shape
    tr = _pick(Tp, (384, 256, 128))

    def body(h_ref, g_ref, u_ref, ut_ref):
        x = h_ref[...]
        r = lax.rsqrt(jnp.mean(x * x, axis=-1, keepdims=True) + NORM_EPS)
        u = (x * r) * g_ref[...]
        u_ref[...] = u.astype(BF16)
        ut_ref[...] = u.T.astype(BF16)

    row = pl.BlockSpec((tr, D), lambda i: (i, 0))
    outs, staged = _staged_call(body, grid=(Tp // tr,), in_specs=[row, pl.BlockSpec((1, D), lambda i: (0, 0))],
                                out_specs=[row, pl.BlockSpec((D, tr), lambda i: (0, i))],
                                out_shape=[jax.ShapeDtypeStruct((Tp, D), BF16), jax.ShapeDtypeStruct((D, Tp), BF16)],
                                scratch_shapes=[], name=name, semantics=("parallel",), inputs=[h, g], stages=stages)
    return (outs, staged) if stages else outs


def rmsnorm_fwd_input(x, meta_tile, g, *, name):
    S, D = x.shape
    nt = S // ROW_TILE + 1

    def body(x_ref, mt_ref, g_ref, h_ref, u_ref, ut_ref):
        h = jnp.where(pl.program_id(0) == nt - 1, mt_ref[...], x_ref[...])
        r = lax.rsqrt(jnp.mean(h * h, axis=-1, keepdims=True) + NORM_EPS)
        u = (h * r) * g_ref[...]
        h_ref[...] = h
        u_ref[...] = u.astype(BF16)
        ut_ref[...] = u.T.astype(BF16)

    row = pl.BlockSpec((ROW_TILE, D), lambda i: (i, 0))
    return pl.pallas_call(
        body, grid=(nt,),
        in_specs=[pl.BlockSpec((ROW_TILE, D), lambda i: (jnp.minimum(i, nt - 2), 0)), pl.BlockSpec((ROW_TILE, D), lambda i: (0, 0)),
                  pl.BlockSpec((1, D), lambda i: (0, 0))],
        out_specs=[row, row, pl.BlockSpec((D, ROW_TILE), lambda i: (0, i))],
        out_shape=[jax.ShapeDtypeStruct((S + ROW_TILE, D), F32), jax.ShapeDtypeStruct((S + ROW_TILE, D), BF16),
                   jax.ShapeDtypeStruct((D, S + ROW_TILE), BF16)],
        name=name, compiler_params=_params("parallel"))(x, meta_tile, g)


def _rms_bwd_math(x, g, dy):
    r = lax.rsqrt(jnp.mean(x * x, axis=-1, keepdims=True) + NORM_EPS)
    xh = x * r
    dyg = dy * g
    dx = r * (dyg - xh * jnp.mean(dyg * xh, axis=-1, keepdims=True))
    return dx, xh


def final_loss(h2, target, gf, *, name):
    Tp, D = h2.shape
    nt = Tp // ROW_TILE

    def body(h_ref, t_ref, g_ref, dhb_ref, st_ref):
        i = pl.program_id(0)

        @pl.when(i == 0)
        def _():
            st_ref[...] = jnp.zeros_like(st_ref)

        x = h_ref[...]
        g = g_ref[...]
        r = lax.rsqrt(jnp.mean(x * x, axis=-1, keepdims=True) + NORM_EPS)
        xh = x * r
        err = jnp.where(i == nt - 1, 0.0, xh * g - t_ref[...])
        dout = err * (1.0 / D)
        dyg = dout * g
        dx = r * (dyg - xh * jnp.mean(dyg * xh, axis=-1, keepdims=True))
        dhb_ref[...] = dx.astype(BF16)
        st_ref[0:1, :] += jnp.sum(dout * xh, axis=0, keepdims=True)
        st_ref[1:2, :] += jnp.sum(err * err, axis=0, keepdims=True) * (0.5 / D)

    row = pl.BlockSpec((ROW_TILE, D), lambda i: (i, 0))
    return pl.pallas_call(
        body, grid=(nt,),
        in_specs=[row, pl.BlockSpec((ROW_TILE, D), lambda i: (jnp.minimum(i, nt - 2), 0)), pl.BlockSpec((1, D), lambda i: (0, 0))],
        out_specs=[row, pl.BlockSpec((8, D), lambda i: (0, 0))],
        out_shape=[jax.ShapeDtypeStruct((Tp, D), BF16), jax.ShapeDtypeStruct((8, D), F32)],
        name=name, compiler_params=_params("arbitrary"))(h2, target, gf)


def rms_bwd(h, g, du, dres, *, name):
    Tp, D = h.shape
    tr = _pick(Tp, (384, 256, 128))

    def body(h_ref, g_ref, du_ref, dr_ref, dhb_ref, st_ref):
        @pl.when(pl.program_id(0) == 0)
        def _():
            st_ref[...] = jnp.zeros_like(st_ref)

        du_v = du_ref[...].astype(F32)
        dx, xh = _rms_bwd_math(h_ref[...], g_ref[...], du_v)
        dhb_ref[...] = (dr_ref[...].astype(F32) + dx).astype(BF16)
        st_ref[0:1, :] += jnp.sum(du_v * xh, axis=0, keepdims=True)

    row = pl.BlockSpec((tr, D), lambda i: (i, 0))
    return pl.pallas_call(
        body, grid=(Tp // tr,), in_specs=[row, pl.BlockSpec((1, D), lambda i: (0, 0)), row, row],
        out_specs=[row, pl.BlockSpec((8, D), lambda i: (0, 0))],
        out_shape=[jax.ShapeDtypeStruct((Tp, D), BF16), jax.ShapeDtypeStruct((8, D), F32)],
        name=name, compiler_params=_params("arbitrary"))(h, g, du, dres)


def rms_bwd_input(h, g, du, dres, *, name, stages=()):
    Tp, D = h.shape
    nt = Tp // ROW_TILE

    def body(h_ref, g_ref, du_ref, dr_ref, gx_ref, gm_ref, st_ref):
        i = pl.program_id(0)

        @pl.when(i == 0)
        def _():
            st_ref[...] = jnp.zeros_like(st_ref)

        du_v = du_ref[...].astype(F32)
        dx, xh = _rms_bwd_math(h_ref[...], g_ref[...], du_v)
        dh = dr_ref[...].astype(F32) + dx
        st_ref[0:1, :] += jnp.sum(du_v * xh, axis=0, keepdims=True)

        @pl.when(i < nt - 1)
        def _():
            gx_ref[...] = dh

        @pl.when(i == nt - 1)
        def _():
            gm_ref[...] = dh[ROW_TILE - N_META:, :]

    row = pl.BlockSpec((ROW_TILE, D), lambda i: (i, 0))
    outs, staged = _staged_call(
        body, grid=(nt,), in_specs=[row, pl.BlockSpec((1, D), lambda i: (0, 0)), row, row],
        out_specs=[pl.BlockSpec((ROW_TILE, D), lambda i: (jnp.minimum(i, nt - 2), 0)), pl.BlockSpec((N_META, D), lambda i: (0, 0)),
                   pl.BlockSpec((8, D), lambda i: (0, 0))],
        out_shape=[jax.ShapeDtypeStruct((Tp - ROW_TILE, D), F32), jax.ShapeDtypeStruct((N_META, D), F32),
                   jax.ShapeDtypeStruct((8, D), F32)],
        scratch_shapes=[], name=name, semantics=("arbitrary",), inputs=[h, g, du, dres], stages=stages)
    return (outs, staged) if stages else outs


def _conv_shifts(ext):
    return tuple(pltpu.roll(ext, k, 0)[HIST:, :] for k in (1, 2, 3))


def _conv_taps(ext, cur, vec_ref, cs):
    x1, x2, x3 = _conv_shifts(ext)
    return vec_ref[1:2, cs] + vec_ref[8:9, cs] * cur + vec_ref[7:8, cs] * x1 + vec_ref[6:7, cs] * x2 + vec_ref[5:6, cs] * x3


def _window_sum_back(ext, w):
    s, sh = ext, 1
    while sh < w:
        s = s + pltpu.roll(s, sh, 0)
        sh *= 2
    return s[HIST:, :]


def _scan_rows(a, b, carry, *, reverse):
    rows = a.shape[0]
    rin = jnp.bitwise_and(lax.broadcasted_iota(jnp.int32, (rows, 1), 0), 7)
    sh = 1
    while sh < 8:
        keep = (rin < 8 - sh) if reverse else (rin >= sh)
        amount = rows - sh if reverse else sh
        a_sh = jnp.where(keep, pltpu.roll(a, amount, 0), 1.0)
        b_sh = jnp.where(keep, pltpu.roll(b, amount, 0), 0.0)
        b = b + a * b_sh
        a = a * a_sh
        sh *= 2
    out = [None] * (rows // 8)
    for g in (reversed(range(rows // 8)) if reverse else range(rows // 8)):
        hg = b[8 * g:8 * g + 8, :] + a[8 * g:8 * g + 8, :] * carry
        carry = hg[0:1, :] if reverse else hg[7:8, :]
        out[g] = hg
    return jnp.concatenate(out, axis=0)


def _lru_gates(xc, wa, wx, vec_ref, cs, sp):
    xcb = xc.astype(BF16)
    r = _sigmoid(jnp.dot(xcb, wa, preferred_element_type=F32) + vec_ref[2:3, cs])
    ig = _sigmoid(jnp.dot(xcb, wx, preferred_element_type=F32) + vec_ref[3:4, cs])
    log_a = (-LRU_C) * r * sp
    a = jnp.exp(log_a)
    a2 = a * a
    return xcb, r, ig, a, a2, _neg_expm1(2.0 * log_a, a2)


def mix_fwd(proj, pool_w, gate_a, gate_x, vecs, *, name, stages=()):
    Tp = proj.shape[0]
    D = proj.shape[1] // 5
    R = ROW_TILE
    nt = Tp // R
    H = D // HEAD_DIM
    PG = D // len(POOL_WINDOWS)

    def body(p_ref, pw_ref, wa_ref, wx_ref, vec_ref, m_ref, hs_ref, xc_ref, r_ref, ig_ref, a_ref, mu_ref, ge_ref, dge_ref,
             hist_p, hist_l, hcar, mtmp):
        i = pl.program_id(0)
        is_meta = i == 0

        @pl.when(is_meta)
        def _():
            hist_p[...] = jnp.zeros_like(hist_p)
            hist_l[...] = jnp.zeros_like(hist_l)
            hcar[...] = jnp.zeros_like(hcar)

        row, valid, t_log = _tile_masks(is_meta, R)

        for g, w in enumerate(POOL_WINDOWS):
            cs = slice(g * PG, (g + 1) * PG)
            v = p_ref[:, g * PG:(g + 1) * PG].astype(F32)
            ws = _window_sum_back(jnp.concatenate([hist_p[:, cs], v], axis=0), w)
            d = ws * _window_count_inv(t_log, w) - v
            y = jnp.dot(d.astype(BF16), pw_ref[g], preferred_element_type=F32)
            gp = p_ref[:, 3 * D + g * PG:3 * D + (g + 1) * PG].astype(F32)
            mtmp[:, cs] = _sigmoid(gp) * (y * vec_ref[0:1, cs])
            hist_p[:, cs] = v[R - HIST:, :]

        for h in range(H):
            cs = slice(h * HEAD_DIM, (h + 1) * HEAD_DIM)
            vl = p_ref[:, D + h * HEAD_DIM:D + (h + 1) * HEAD_DIM].astype(F32)
            xc = _conv_taps(jnp.concatenate([hist_l[:, cs], vl], axis=0), vl, vec_ref, cs)
            sp = _softplus_neg(vec_ref[4:5, cs])
            xcb, r, ig, a, _, em = _lru_gates(xc, wa_ref[h], wx_ref[h], vec_ref, cs, sp)
            mult = jnp.sqrt(em)
            b = jnp.where(valid, mult * (ig * xc), 0.0)
            hs = _scan_rows(a, b, hcar[7:8, cs], reverse=False)
            hs_ref[:, cs] = hs
            xc_ref[:, cs], r_ref[:, cs], ig_ref[:, cs] = xcb, r.astype(BF16), ig.astype(BF16)
            a_ref[:, cs], mu_ref[:, cs] = a, mult
            hcar[:, cs] = hs[R - 8:, :]
            hist_l[:, cs] = vl[R - HIST:, :]
            vg = p_ref[:, 2 * D + h * HEAD_DIM:2 * D + (h + 1) * HEAD_DIM].astype(F32)
            ge, th = _gelu_tanh(vg)
            ge_ref[:, cs], dge_ref[:, cs] = ge.astype(BF16), _gelu_tanh_grad(vg, th).astype(BF16)
            gl = p_ref[:, 4 * D + h * HEAD_DIM:4 * D + (h + 1) * HEAD_DIM].astype(F32)
            m_ref[:, cs] = (mtmp[:, cs] + _sigmoid(gl) * (hs * ge)).astype(BF16)

    def tile(i):
        return (i + nt - 1) % nt

    full = lambda shape: pl.BlockSpec(shape, lambda i: (0,) * len(shape))
    outs, staged = _staged_call(
        body, grid=(nt,),
        in_specs=[pl.BlockSpec((R, 5 * D), lambda i: (tile(i), 0)), full(pool_w.shape), full(gate_a.shape), full(gate_x.shape),
                  full(vecs.shape)],
        out_specs=[pl.BlockSpec((R, D), lambda i: (tile(i), 0))] * 9,
        out_shape=[jax.ShapeDtypeStruct((Tp, D), dt) for dt in (BF16, F32, BF16, BF16, BF16, F32, F32, BF16, BF16)],
        scratch_shapes=[pltpu.VMEM((HIST, D), F32), pltpu.VMEM((HIST, D), F32), pltpu.VMEM((8, D), F32), pltpu.VMEM((R, D), F32)],
        name=name, semantics=("arbitrary",), inputs=[proj, pool_w, gate_a, gate_x, vecs], stages=stages)
    outs = [outs[0], outs[1], outs[2:]]
    return (outs, staged) if stages else outs


def mix_bwd(proj, hs, saved, dmerged, pool_w, gate_a, gate_x, vecs, *, name, stages=()):
    Tp = proj.shape[0]
    D = proj.shape[1] // 5
    R = ROW_TILE
    nt = Tp // R
    H = D // HEAD_DIM
    PG = D // len(POOL_WINDOWS)

    def body(p_ref, pprev_ref, hs_ref, hprev_ref, dm_ref, xc_ref, r_ref, ig_ref, a_ref, mu_ref, ge_ref, dge_ref, pw_ref, wa_ref, wx_ref, vec_ref,
             dp_ref, dpw_ref, dwa_ref, dwx_ref, dvec_ref, car_g, fut_dxc, fut_q):
        i = pl.program_id(0)
        is_meta = i == nt - 1

        @pl.when(i == 0)
        def _():
            dpw_ref[...] = jnp.zeros_like(dpw_ref)
            dwa_ref[...] = jnp.zeros_like(dwa_ref)
            dwx_ref[...] = jnp.zeros_like(dwx_ref)
            dvec_ref[...] = jnp.zeros_like(dvec_ref)
            car_g[...] = jnp.zeros_like(car_g)
            fut_dxc[...] = jnp.zeros_like(fut_dxc)
            fut_q[...] = jnp.zeros_like(fut_q)

        row, valid, t_log = _tile_masks(is_meta, R)
        keep_prev = jnp.logical_not(is_meta)

        def colsum(x):
            return jnp.sum(x, axis=0, keepdims=True)

        for g, w in enumerate(POOL_WINDOWS):
            cs = slice(g * PG, (g + 1) * PG)
            v = p_ref[:, g * PG:(g + 1) * PG].astype(F32)
            vprev = jnp.where(keep_prev, pprev_ref[:, g * PG:(g + 1) * PG].astype(F32), 0.0)
            inv_cnt = _window_count_inv(t_log, w)
            d = _window_sum_back(jnp.concatenate([vprev, v], axis=0), w) * inv_cnt - v
            d_bf = d.astype(BF16)
            y = jnp.dot(d_bf, pw_ref[g], preferred_element_type=F32)
            scale = vec_ref[0:1, cs]
            sg = _sigmoid(p_ref[:, 3 * D + g * PG:3 * D + (g + 1) * PG].astype(F32))
            dm = dm_ref[:, cs].astype(F32)
            dpo = dm * sg
            dp_ref[:, 3 * D + g * PG:3 * D + (g + 1) * PG] = (dm * (y * scale) * sg * (1.0 - sg)).astype(BF16)
            dvec_ref[0:1, cs] += colsum(dpo * y)
            dy = (dpo * scale).astype(BF16)
            dd = _dot_nt(dy, pw_ref[g])
            dpw_ref[g] += _dot_tn(d_bf, dy)
            q = dd * inv_cnt
            s, sh = jnp.concatenate([q, fut_q[:, cs]], axis=0), 1
            while sh < w:
                s = s + pltpu.roll(s, R + HIST - sh, 0)
                sh *= 2
            dp_ref[:, g * PG:(g + 1) * PG] = (s[:R, :] - dd).astype(BF16)
            fut_q[:, cs] = q[:HIST, :]

        for h in range(H):
            cs = slice(h * HEAD_DIM, (h + 1) * HEAD_DIM)
            pc = lambda blk: slice(blk * D + h * HEAD_DIM, blk * D + (h + 1) * HEAD_DIM)
            vl = p_ref[:, pc(1)].astype(F32)
            vlprev = jnp.where(keep_prev, pprev_ref[:, pc(1)].astype(F32), 0.0)
            x1, x2, x3 = _conv_shifts(jnp.concatenate([vlprev, vl], axis=0))
            lam = vec_ref[4:5, cs]
            sp = _softplus_neg(lam)
            xcb = xc_ref[:, cs]
            xc, r, ig = xcb.astype(F32), r_ref[:, cs].astype(F32), ig_ref[:, cs].astype(F32)
            a, mult = a_ref[:, cs], mu_ref[:, cs]
            a2 = a * a
            inv_mult = 1.0 / mult
            hsv = hs_ref[:, cs]
            hprev = jnp.where(row >= 1, pltpu.roll(hsv, 1, 0), hprev_ref[HIST - 1:HIST, cs])
            ge = ge_ref[:, cs].astype(F32)
            sgl = _sigmoid(p_ref[:, pc(4)].astype(F32))
            dm = dm_ref[:, cs].astype(F32)
            dlo = dm * sgl
            dp_ref[:, pc(4)] = (dm * (hsv * ge) * sgl * (1.0 - sgl)).astype(BF16)
            dp_ref[:, pc(2)] = (dlo * hsv * dge_ref[:, cs].astype(F32)).astype(BF16)
            a_next = jnp.where(row < R - 1, pltpu.roll(a, R - 1, 0), 1.0)
            G = _scan_rows(a_next, dlo * ge, car_g[0:1, cs], reverse=True)
            car_g[:, cs] = (a * G)[0:8, :]
            da = jnp.where(valid, G * hprev, 0.0)
            db = jnp.where(valid, G, 0.0)
            dmult = db * (ig * xc)
            dig = db * (mult * xc)
            dxc = db * (mult * ig)
            dlog_a = da * a - dmult * (a2 * inv_mult)
            dvec_ref[4:5, cs] += colsum(dlog_a * r) * (-LRU_C)
            dr = dlog_a * ((-LRU_C) * sp)
            dpa = dr * r * (1.0 - r)
            dpx = dig * ig * (1.0 - ig)
            dpa_bf = dpa.astype(BF16)
            dpx_bf = dpx.astype(BF16)
            dwa_ref[h] += _dot_tn(xcb, dpa_bf)
            dwx_ref[h] += _dot_tn(xcb, dpx_bf)
            dvec_ref[2:3, cs] += colsum(dpa)
            dvec_ref[3:4, cs] += colsum(dpx)
            dxc = dxc + _dot_nt(dpa_bf, wa_ref[h]) + _dot_nt(dpx_bf, wx_ref[h])
            ext = jnp.concatenate([dxc, fut_dxc[:, cs]], axis=0)
            n = R + HIST
            dvl = (vec_ref[8:9, cs] * dxc + vec_ref[7:8, cs] * pltpu.roll(ext, n - 1, 0)[:R, :]
                   + vec_ref[6:7, cs] * pltpu.roll(ext, n - 2, 0)[:R, :] + vec_ref[5:6, cs] * pltpu.roll(ext, n - 3, 0)[:R, :])
            dp_ref[:, pc(1)] = dvl.astype(BF16)
            dvec_ref[1:2, cs] += colsum(dxc)
            dvec_ref[8:9, cs] += colsum(dxc * vl)
            dvec_ref[7:8, cs] += colsum(dxc * x1)
            dvec_ref[6:7, cs] += colsum(dxc * x2)
            dvec_ref[5:6, cs] += colsum(dxc * x3)
            fut_dxc[:, cs] = dxc[:HIST, :]

            @pl.when(is_meta)
            def _():
                dvec_ref[4:5, cs] = dvec_ref[4:5, cs] * (-_sigmoid(-lam))

    def tile(i):
        return (2 * nt - 2 - i) % nt

    def prev_blk(i):
        per = R // HIST
        return jnp.where(i == nt - 1, 0, jnp.where(i == nt - 2, Tp // HIST - 1, (nt - 2 - i) * per - 1))

    full = lambda shape: pl.BlockSpec(shape, lambda i: (0,) * len(shape))
    G_ = len(POOL_WINDOWS)
    outs, staged = _staged_call(
        body, grid=(nt,),
        in_specs=[pl.BlockSpec((R, 5 * D), lambda i: (tile(i), 0)), pl.BlockSpec((HIST, 5 * D), lambda i: (prev_blk(i), 0)),
                  pl.BlockSpec((R, D), lambda i: (tile(i), 0)), pl.BlockSpec((HIST, D), lambda i: (prev_blk(i), 0)),
                  *[pl.BlockSpec((R, D), lambda i: (tile(i), 0))] * 8,
                  full(pool_w.shape), full(gate_a.shape), full(gate_x.shape), full(vecs.shape)],
        out_specs=[pl.BlockSpec((R, 5 * D), lambda i: (tile(i), 0)), full((G_, PG, PG)), full((H, HEAD_DIM, HEAD_DIM)),
                   full((H, HEAD_DIM, HEAD_DIM)), full((16, D))],
        out_shape=[jax.ShapeDtypeStruct((Tp, 5 * D), BF16), jax.ShapeDtypeStruct((G_, PG, PG), F32),
                   jax.ShapeDtypeStruct((H, HEAD_DIM, HEAD_DIM), F32), jax.ShapeDtypeStruct((H, HEAD_DIM, HEAD_DIM), F32),
                   jax.ShapeDtypeStruct((16, D), F32)],
        scratch_shapes=[pltpu.VMEM((8, D), F32), pltpu.VMEM((HIST, D), F32), pltpu.VMEM((HIST, D), F32)],
        name=name, semantics=("arbitrary",), inputs=[proj, proj, hs, hs, dmerged, *saved, pool_w, gate_a, gate_x, vecs], stages=stages)
    return (outs, staged) if stages else outs


def _adamw_math(w, g, m, v):
    mn = ADAM_B1 * m + (1.0 - ADAM_B1) * g
    vn = ADAM_B2 * v + (1.0 - ADAM_B2) * (g * g)
    m_hat = mn / (1.0 - ADAM_B1 ** ADAM_STEP)
    v_hat = vn / (1.0 - ADAM_B2 ** ADAM_STEP)
    return -ADAM_LR * (m_hat / (jnp.sqrt(v_hat) + ADAM_EPS) + ADAM_WD * w), mn, vn


def adamw(w, g, m, v, *, name):
    rows, cols = w.shape
    tr = _pick(rows, (256, 128, 64, 32, 16, 8))

    def body(w_ref, g_ref, m_ref, v_ref, go_ref, d_ref, mo_ref, vo_ref):
        gv = g_ref[...]
        go_ref[...] = gv
        d_ref[...], mo_ref[...], vo_ref[...] = _adamw_math(w_ref[...], gv, m_ref[...], v_ref[...])

    blk = pl.BlockSpec((tr, cols), lambda i: (i, 0))
    sds = jax.ShapeDtypeStruct((rows, cols), F32)
    return pl.pallas_call(body, grid=(rows // tr,), in_specs=[blk] * 4, out_specs=[blk] * 4, out_shape=[sds] * 4, name=name,
                          compiler_params=_params("parallel"))(w, g, m, v)


def allgather8(block, *, name, reduce_sum=False, stages=()):
    rows, cols = block.shape

    def body(x_ref, out_ref, *scratch):
        if reduce_sum:
            buf, send_sems, recv_sems, local_sem = scratch
        else:
            buf = out_ref
            send_sems, recv_sems, local_sem = scratch
        x, y, c, chips = _place()
        me, sibling = (x, y, c), (x, y, 1 - c)

        def slot(px, py, pc):
            return buf.at[4 * px + 2 * py + pc]

        def copy(k, blk, to, src=None):
            return pltpu.make_async_remote_copy(src_ref=slot(*blk) if src is None else src, dst_ref=slot(*blk),
                                                send_sem=send_sems.at[k], recv_sem=recv_sems.at[k], device_id=to, device_id_type=MESH)

        mine = pltpu.make_async_copy(x_ref, slot(*me), local_sem)
        mine.start()
        first = [copy(0, me, sibling, src=x_ref)]
        first += [copy(1 + j, me, (*chip, c), src=x_ref) for j, chip in enumerate(chips)]
        for cp in first:
            cp.start()
        passed = [copy(4 + j, (*chip, c), sibling) for j, chip in enumerate(chips)]
        for j, chip in enumerate(chips):
            copy(1 + j, (*chip, c), me).wait_recv()
            passed[j].start()
        copy(0, sibling, me).wait_recv()
        for j, chip in enumerate(chips):
            copy(4 + j, (*chip, 1 - c), me).wait_recv()
        for cp in first + passed:
            cp.wait_send()
        mine.wait()
        if reduce_sum:
            acc = buf[0]
            for d in range(1, 8):
                acc = acc + buf[d]
            out_ref[...] = acc

    sems = [pltpu.SemaphoreType.DMA((7,)), pltpu.SemaphoreType.DMA((7,)), pltpu.SemaphoreType.DMA]
    if reduce_sum:
        out_shape = jax.ShapeDtypeStruct((rows, cols), block.dtype)
        scratch = [pltpu.VMEM((8, rows, cols), block.dtype)] + sems
    else:
        out_shape = jax.ShapeDtypeStruct((8, rows, cols), block.dtype)
        scratch = sems
    vmem = pl.BlockSpec(memory_space=pltpu.VMEM)
    outs, staged = _staged_call(body, grid=(1,), in_specs=[vmem], out_specs=[vmem], out_shape=[out_shape], scratch_shapes=scratch,
                                name=name, semantics=("arbitrary",), inputs=[block], stages=stages)
    return (outs[0], staged) if stages else outs[0]


def cast_into_slot(w, chip_arr, *, name):
    _, r, cols = w.shape
    tr = _pick(r, (256, 128, 64, 32, 16))

    def body(chip_ref, w_ref, o_ref):
        del chip_ref
        o_ref[...] = w_ref[...].astype(BF16)

    grid_spec = pltpu.PrefetchScalarGridSpec(
        num_scalar_prefetch=1, grid=(2, r // tr),
        in_specs=[pl.BlockSpec((None, tr, cols), lambda h, i, chip: (h, i, 0))],
        out_specs=pl.BlockSpec((None, None, tr, cols), lambda h, i, chip: (chip[0], h, i, 0)))
    return pl.pallas_call(body, grid_spec=grid_spec, out_shape=jax.ShapeDtypeStruct((4, 2, r, cols), BF16), name=name,
                          compiler_params=_params("parallel", "parallel"))(chip_arr, w)


def proj_with_gather(u, bufs, order_arr, *, name):
    Tp, K = u.shape
    n = len(bufs)
    Ns = bufs[0].shape[3]
    tm = _pick(Tp, (1408, 512, 256, 128))
    tc = _pick(Ns, (512, 256, 128))
    n_rows = Tp // tm

    def body(order_ref, u_ref, *refs):
        del order_ref
        o_ref, outs = refs[n], refs[n + 1:2 * n + 1]
        wbuf, dir_send, dir_recv, rel_send, rel_recv, d2d_send, d2d_recv, load_sems = refs[2 * n + 1:]
        s, i = pl.program_id(0), pl.program_id(1)
        x, y, c, chips = _place()
        me = 2 * x + y
        (dx, dy) = chips[2]

        def direct(t, j, landing):
            px, py = chips[j]
            slot = outs[t].at[2 * px + py, c] if landing else outs[t].at[me, c]
            return _remote(slot, slot, dir_send.at[t, j], dir_recv.at[t, j], (px, py, c))

        def relay(t, j, landing):
            px, py = chips[j]
            ox, oy = chips[1 - j]
            rows = _relay_rows(outs[t], 2 * dx + dy, c, j) if landing else _relay_rows(outs[t], 2 * ox + oy, c, j)
            return _remote(rows, rows, rel_send.at[t, j], rel_recv.at[t, j], (px, py, c))

        def d2d(t, j, landing):
            px, py = chips[j]
            slot = outs[t].at[2 * px + py, 1 - c] if landing else outs[t].at[2 * px + py, c]
            return _remote(slot, slot, d2d_send.at[t, j], d2d_recv.at[t, j], (x, y, 1 - c))

        def load(chip_idx):
            parts = [pltpu.make_async_copy(outs[0].at[chip_idx, hh], wbuf.at[pl.ds(hh * (K // 2), K // 2), :], load_sems.at[hh])
                     for hh in range(2)]
            for cp in parts:
                cp.start()
            for cp in parts:
                cp.wait()

        def chip_of(j):
            px, py = chips[j]
            return 2 * px + py

        @pl.when(jnp.logical_and(s == 0, i == 0))
        def _():
            for t in range(n):
                for j in range(2):
                    direct(t, j, False).start()
            load(me)

        @pl.when(jnp.logical_and(s == 1, i == 0))
        def _():
            for j in range(2):
                direct(0, j, True).wait_recv()
            for j in range(2):
                relay(0, j, False).start()
                d2d(0, j, False).start()
            d2d(0, 0, True).wait_recv()
            load(chip_of(0))

        @pl.when(jnp.logical_and(s == 2, i == 0))
        def _():
            d2d(0, 1, True).wait_recv()
            for t in range(1, n):
                for j in range(2):
                    direct(t, j, True).wait_recv()
                for j in range(2):
                    relay(t, j, False).start()
                    d2d(t, j, False).start()
            load(chip_of(1))

        @pl.when(jnp.logical_and(s == 3, i == 0))
        def _():
            for j in range(2):
                relay(0, j, True).wait_recv()
            d2d(0, 2, False).start()
            d2d(0, 2, True).wait_recv()
            load(chip_of(2))

        uv = u_ref[...]
        for cc in range(Ns // tc):
            o_ref[:, cc * tc:(cc + 1) * tc] = jnp.dot(uv, wbuf[:, cc * tc:(cc + 1) * tc], preferred_element_type=F32).astype(BF16)

        @pl.when(jnp.logical_and(s == 3, i == n_rows - 1))
        def _():
            for t in range(1, n):
                for j in range(2):
                    relay(t, j, True).wait_recv()
                d2d(t, 2, False).start()
            for t in range(1, n):
                for j in range(3):
                    d2d(t, j, True).wait_recv()
            for t in range(n):
                for j in range(2):
                    direct(t, j, False).wait_send()
                    relay(t, j, False).wait_send()
                for j in range(3):
                    d2d(t, j, False).wait_send()

    grid_spec = pltpu.PrefetchScalarGridSpec(
        num_scalar_prefetch=1, grid=(4, n_rows),
        in_specs=[pl.BlockSpec((tm, K), lambda s, i, order: (i, 0)), *_any_specs(n)],
        out_specs=[pl.BlockSpec((tm, Ns), lambda s, i, order: (i, order[s])), *_any_specs(n)],
        scratch_shapes=[pltpu.VMEM((K, Ns), BF16), pltpu.SemaphoreType.DMA((n, 2)), pltpu.SemaphoreType.DMA((n, 2)),
                        pltpu.SemaphoreType.DMA((n, 2)), pltpu.SemaphoreType.DMA((n, 2)),
                        pltpu.SemaphoreType.DMA((n, 3)), pltpu.SemaphoreType.DMA((n, 3)), pltpu.SemaphoreType.DMA((2,))])
    res = pl.pallas_call(body, grid_spec=grid_spec, out_shape=[jax.ShapeDtypeStruct((Tp, 4 * Ns), BF16), *[_sds(b) for b in bufs]],
                         input_output_aliases={2 + t: 1 + t for t in range(n)}, name=name,
                         compiler_params=_params("arbitrary", "arbitrary"))(order_arr, u, *bufs)
    return res[0], list(res[1:])


def chip_presum(grad, recv, c_arr, *, name):
    _, _, r, cols = grad.shape
    tr = _pick(r, (256, 128, 64, 32, 16))

    def body(c_ref, g_ref, r_ref, o_ref):
        del c_ref
        o_ref[...] = (g_ref[...].astype(F32) + r_ref[...].astype(F32)).astype(BF16)

    grid_spec = pltpu.PrefetchScalarGridSpec(
        num_scalar_prefetch=1, grid=(4, r // tr),
        in_specs=[pl.BlockSpec((None, None, tr, cols), lambda k, i, c_ref: (k, c_ref[0], i, 0)),
                  pl.BlockSpec((None, tr, cols), lambda k, i, c_ref: (k, i, 0))],
        out_specs=pl.BlockSpec((None, tr, cols), lambda k, i, c_ref: (k, i, 0)))
    return pl.pallas_call(body, grid_spec=grid_spec, out_shape=jax.ShapeDtypeStruct((4, r, cols), BF16), name=name,
                          compiler_params=_params("parallel", "parallel"))(c_arr, grad, recv)


def final_half(grad, recv, got, mc_arr, *, name):
    _, _, r, cols = grad.shape
    tr = _pick(r, (256, 128, 64, 32, 16))

    def body(mc_ref, g_ref, r_ref, q_ref, o_ref):
        del mc_ref
        acc = g_ref[...].astype(F32) + r_ref[...].astype(F32)
        for j in range(3):
            acc = acc + q_ref[j].astype(F32)
        o_ref[...] = acc

    grid_spec = pltpu.PrefetchScalarGridSpec(
        num_scalar_prefetch=1, grid=(r // tr,),
        in_specs=[pl.BlockSpec((None, None, tr, cols), lambda i, mc: (mc[0], mc[1], i, 0)),
                  pl.BlockSpec((None, tr, cols), lambda i, mc: (mc[0], i, 0)),
                  pl.BlockSpec((3, tr, cols), lambda i, mc: (0, i, 0))],
        out_specs=pl.BlockSpec((tr, cols), lambda i, mc: (i, 0)))
    return pl.pallas_call(body, grid_spec=grid_spec, out_shape=jax.ShapeDtypeStruct((r, cols), F32), name=name,
                          compiler_params=_params("parallel"))(mc_arr, grad, recv, got)


def adamw_halves(w, mine, theirs, m, v, c_arr, *, name):
    _, r, cols = w.shape
    tr = _pick(r, (256, 128, 64, 32, 16, 8))

    def body(c_ref, w_ref, mine_ref, theirs_ref, m_ref, v_ref, go_ref, d_ref, mo_ref, vo_ref):
        gv = jnp.where(pl.program_id(0) == c_ref[0], mine_ref[...], theirs_ref[...])
        go_ref[...] = gv
        d_ref[...], mo_ref[...], vo_ref[...] = _adamw_math(w_ref[...], gv, m_ref[...], v_ref[...])

    blk = pl.BlockSpec((None, tr, cols), lambda h, i, c_ref: (h, i, 0))
    grid_spec = pltpu.PrefetchScalarGridSpec(
        num_scalar_prefetch=1, grid=(2, r // tr),
        in_specs=[blk, pl.BlockSpec((tr, cols), lambda h, i, c_ref: (jnp.where(h == c_ref[0], i, 0), 0)),
                  pl.BlockSpec((tr, cols), lambda h, i, c_ref: (jnp.where(h == c_ref[0], 0, i), 0)), blk, blk],
        out_specs=[blk] * 4)
    sds = jax.ShapeDtypeStruct((2, r, cols), F32)
    return pl.pallas_call(body, grid_spec=grid_spec, out_shape=[sds] * 4, name=name,
                          compiler_params=_params("parallel", "parallel"))(c_arr, w, mine, theirs, m, v)


def _pad_rows(a, rows):
    return jnp.pad(a, ((0, rows - a.shape[0]), (0, 0)))


def kernel(x, meta_tokens, norm1_g, w_in, pool_w, pool_scale, conv_w, conv_b, gate_a_w, gate_a_b, gate_x_w, gate_x_b, lru_lambda, w_out, norm2_g, mlp_w1, mlp_w2, final_g, loss_target, m_meta_tokens, m_norm1_g, m_w_in, m_pool_w, m_pool_scale, m_conv_w, m_conv_b, m_gate_a_w, m_gate_a_b, m_gate_x_w, m_gate_x_b, m_lru_lambda, m_w_out, m_norm2_g, m_mlp_w1, m_mlp_w2, m_final_g, v_meta_tokens, v_norm1_g, v_w_in, v_pool_w, v_pool_scale, v_conv_w, v_conv_b, v_gate_a_w, v_gate_a_b, v_gate_x_w, v_gate_x_b, v_lru_lambda, v_w_out, v_norm2_g, v_mlp_w1, v_mlp_w2, v_final_g):
    D = x.shape[-1]
    H = D // HEAD_DIM
    G = len(POOL_WINDOWS)
    PG = D // G
    ax, ay, ac = lax.axis_index("x"), lax.axis_index("y"), lax.axis_index("c")
    chip = 2 * ax + ay
    dshard = D // 4

    c_arr = jnp.reshape(ac, (1,)).astype(jnp.int32)
    chip_arr = jnp.reshape(chip, (1,)).astype(jnp.int32)
    mc_arr = jnp.stack([chip, ac]).astype(jnp.int32)
    names = ["w_in", "pool_w", "gate_a_w", "gate_x_w", "w_out", "mlp_w1", "mlp_w2"]
    big = [w_in, pool_w, gate_a_w, gate_x_w, w_out, mlp_w1, mlp_w2]

    def halves(w):
        w2d = w.reshape(-1, w.shape[-1])
        return w2d.reshape(2, w2d.shape[0] // 2, w2d.shape[1])

    bufs = [cast_into_slot(halves(w), chip_arr, name="cast_" + nm) for w, nm in zip(big, names)]
    order_arr = jnp.stack([chip, 2 * (1 - ax) + ay, 2 * ax + (1 - ay), 2 * (1 - ax) + (1 - ay)]).astype(jnp.int32)

    small_in = jnp.concatenate([meta_tokens, _pad_rows(conv_w[0], 8), _pad_rows(gate_a_b.reshape(1, dshard), 8),
                                _pad_rows(gate_x_b.reshape(1, dshard), 8)], axis=0)
    sm = allgather8(small_in, name="gather_small")[0::2]
    meta_f = sm[:, 0:16].transpose(1, 0, 2).reshape(N_META, D)
    conv_w_f = sm[:, 16:20].transpose(1, 0, 2).reshape(4, D)
    hd4 = HEAD_DIM // 4
    ba_f = sm[:, 24].reshape(4, H, hd4).transpose(1, 0, 2).reshape(1, D)
    bx_f = sm[:, 32].reshape(4, H, hd4).transpose(1, 0, 2).reshape(1, D)
    vecs = jnp.zeros((16, D), F32)
    for r0, part in ((0, pool_scale), (1, conv_b), (2, ba_f), (3, bx_f), (4, lru_lambda), (5, conv_w_f)):
        vecs = lax.dynamic_update_slice(vecs, part, (r0, 0))

    def chipwise(g, n_blocks, rows):
        return g.reshape(n_blocks, 4, rows, g.shape[-1]).transpose(1, 0, 2, 3).reshape(4, n_blocks * rows, g.shape[-1])

    def split2(g):
        return g.reshape(4, 2, g.shape[1] // 2, g.shape[2])

    def presum(t, g, r):
        return chip_presum(g, r, c_arr, name="presum_" + names[t])

    def total(t, g, r, q):
        return final_half(g, r, q, mc_arr, name="sum_" + names[t])

    xs, target, gfin = x[0], loss_target[0], final_g.reshape(1, D)
    meta_tile = jnp.concatenate([jnp.zeros((ROW_TILE - N_META, D), F32), meta_f], axis=0)
    h0, u, u_t = rmsnorm_fwd_input(xs, meta_tile, norm1_g, name="norm1")
    proj, (b_win, b_pool, b_ga, b_gx, b_wout) = proj_with_gather(u, bufs[0:5], order_arr, name="proj")
    w_in_f = b_win.reshape(4, w_in.shape[1], w_in.shape[2])
    pool_f = b_pool.reshape(4, G, PG // 4, PG).transpose(1, 0, 2, 3).reshape(G, PG, PG)
    ga_f = b_ga.reshape(4, H, HEAD_DIM // 4, HEAD_DIM).transpose(1, 0, 2, 3).reshape(H, HEAD_DIM, HEAD_DIM)
    gx_f = b_gx.reshape(4, H, HEAD_DIM // 4, HEAD_DIM).transpose(1, 0, 2, 3).reshape(H, HEAD_DIM, HEAD_DIM)
    w_out_f = b_wout.reshape(4 * w_out.shape[1], w_out.shape[2])
    (merged, hs, saved), [[b_w1, b_w2]] = mix_fwd(proj, pool_f, ga_f, gx_f, vecs, name="mix_fwd",
                                                  stages=[stage_gather_direct([bufs[5], bufs[6]])])
    h1, [[b_w1]] = mm_nn(merged, w_out_f, out_dtype=F32, name="out_proj", epilogue=lambda r, res: r + res, extras=(h0,),
                         stages=[stage_both(stage_gather_relay([b_w1]), stage_gather_d2d([b_w1], peers=(0, 1)))])
    (u2, u2_t), [[b_w1]] = rmsnorm_fwd(h1, norm2_g, name="norm2", stages=[stage_gather_d2d([b_w1], peers=(2,))])
    w1_f = b_w1.reshape(4, mlp_w1.shape[1], mlp_w1.shape[2])
    a1, [[b_w2]] = mm_nn(u2, w1_f, out_dtype=BF16, name="mlp_up", tiles=(None, min(1024, D), None),
                         stages=[stage_both(stage_gather_relay([b_w2]), stage_gather_d2d([b_w2], peers=(0, 1)))])
    [[b_w2]] = comm_call([stage_gather_d2d([b_w2], peers=(2,))], name="w2_to_sibling")
    w2_f = b_w2.reshape(4 * mlp_w2.shape[1], mlp_w2.shape[2])
    Tp = h0.shape[0]
    h2 = mm_nn(a1, w2_f, out_dtype=F32, name="mlp_down", a_pro=_relu_sq, epilogue=lambda r, res: r + res, extras=(h1,),
               tiles=(_pick(Tp, (704, 512, 256, 128)), min(256, D), a1.shape[1]))
    dh2_bf, st_f = final_loss(h2, target, gfin, name="final_loss")

    da1 = mm_nt(dh2_bf, w2_f, out_dtype=BF16, name="mlp_down_dx",
                epilogue=lambda r, a: r * (2.0 * jnp.maximum(a.astype(F32), 0.0)), extras=(a1,))
    d_w2 = mm_tn(a1, dh2_bf, shards=1, out_dtype=BF16, name="mlp_down_dw", a_pro=_relu_sq, tiles=(512, min(1024, D), Tp))
    g6 = split2(d_w2.reshape(4, -1, d_w2.shape[-1]))
    d_w1, [[r6]] = mm_nn(u2_t, da1, out_dtype=BF16, name="mlp_up_dw", out_shards=4, tiles=(min(1024, D), min(1024, D), Tp),
                         stages=[stage_to_sibling([g6])])
    g5 = split2(d_w1)
    p6 = presum(6, g6, r6)
    du2, [[q6_near], [r5]] = mm_nt(da1, w1_f, out_dtype=BF16, name="mlp_up_dx",
                                  stages=[stage_to_chips([p6], peers=(0, 1)), stage_to_sibling([g5])])
    p5 = presum(5, g5, r5)
    dh1_bf, st_2 = rms_bwd(h1, norm2_g, du2, dh2_bf, name="norm2_bwd")
    dmerged = mm_nt(dh1_bf, w_out_f, out_dtype=BF16, name="out_proj_dx")
    d_wout = mm_tn(merged, dh1_bf, shards=1, out_dtype=BF16, name="out_proj_dw", tiles=(512, min(1024, D), Tp))
    g4 = split2(d_wout.reshape(4, -1, d_wout.shape[-1]))
    (dproj, d_pool, d_ga, d_gx, d_vecs), [[q6], [q5], [r4]] = mix_bwd(
        proj, hs, saved, dmerged, pool_f, ga_f, gx_f, vecs, name="mix_bwd",
        stages=[stage_to_chips([p6], peers=(2,), into=[q6_near]), stage_to_chips([p5]), stage_to_sibling([g4])])
    p4 = presum(4, g4, r4)
    f5, f6 = total(5, g5, r5, q5), total(6, g6, r6, q6)
    g1, g2, g3 = split2(chipwise(d_pool, G, PG // 4)), split2(chipwise(d_ga, H, hd4)), split2(chipwise(d_gx, H, hd4))
    half_k = u_t.shape[0] // 2
    whole_k = (half_k, _pick(dproj.shape[1] // 4, (1280, 512, 256, 128)), Tp)
    u_theirs = lax.dynamic_slice_in_dim(u_t, (1 - ac) * half_k, half_k, axis=0)
    u_mine = lax.dynamic_slice_in_dim(u_t, ac * half_k, half_k, axis=0)
    g_theirs, [[r1, r2, r3], [o5, o6], [q4]] = mm_nn(u_theirs, dproj, out_dtype=BF16, out_shards=4, tiles=whole_k, name="proj_dw_sibling_rows",
                                                    stages=[stage_to_sibling([g1, g2, g3]), stage_from_sibling([f5, f6]), stage_to_chips([p4])])
    p1, p2, p3 = presum(1, g1, r1), presum(2, g2, r2), presum(3, g3, r3)
    g_mine, [[r0], [q1, q2, q3]] = mm_nn(u_mine, dproj, out_dtype=BF16, out_shards=4, tiles=whole_k, name="proj_dw_own_rows",
                                         stages=[stage_from_sibling([g_theirs]), stage_to_chips([p1, p2, p3])])
    g0 = g_mine[:, None]
    p0 = chip_presum(g0, r0, jnp.zeros((1,), jnp.int32), name="presum_w_in")
    f4 = total(4, g4, r4, q4)
    du, [[q0], [o4]] = mm_nt(dproj, w_in_f, out_dtype=BF16, name="proj_dx", tiles=(None, None, w_in_f.shape[2]),
                             stages=[stage_to_chips([p0]), stage_from_sibling([f4])])
    f0 = final_half(g0, r0, q0, jnp.stack([chip, 0]).astype(jnp.int32), name="sum_w_in")
    f1, f2, f3 = total(1, g1, r1, q1), total(2, g2, r2, q2), total(3, g3, r3, q3)
    grad_x, d_meta, st_1 = rms_bwd_input(h0, norm1_g, du, dh1_bf, name="norm1_bwd")

    small = jnp.concatenate([d_meta, d_vecs, st_1, st_2, st_f], axis=0)
    tot, [[o0, o1, o2, o3]] = allgather8(small, name="sum_small", reduce_sum=True, stages=[stage_from_sibling([f0, f1, f2, f3])])
    mine = [f0, f1, f2, f3, f4, f5, f6]
    theirs = [o0, o1, o2, o3, o4, o5, o6]
    loss = jnp.sum(tot[49])
    g_meta = lax.dynamic_slice_in_dim(tot[0:16], chip * dshard, dshard, axis=1)
    g_pool_scale, g_conv_b, g_lam = tot[16:17], tot[17:18], tot[20:21]
    g_ba = lax.dynamic_slice_in_dim(tot[18].reshape(H, HEAD_DIM), chip * hd4, hd4, axis=1)[None]
    g_bx = lax.dynamic_slice_in_dim(tot[19].reshape(H, HEAD_DIM), chip * hd4, hd4, axis=1)[None]
    g_conv_w = lax.dynamic_slice_in_dim(tot[21:25], chip * dshard, dshard, axis=1)[None]
    g_n1, g_n2, g_fin = tot[32:33], tot[40:41], tot[48]

    def step(w, g, m, v, nm):
        cols = w.shape[-1]
        outs = adamw(w.reshape(-1, cols), g.reshape(-1, cols), m.reshape(-1, cols), v.reshape(-1, cols), name="adamw_" + nm)
        return [o.reshape(w.shape) for o in outs]

    def step_big(t, w, m, v):
        outs = adamw_halves(halves(w), mine[t], theirs[t], halves(m), halves(v), c_arr, name="adamw_" + names[t])
        return [o.reshape(w.shape) for o in outs]

    res = dict(meta_tokens=step(meta_tokens, g_meta, m_meta_tokens, v_meta_tokens, "meta_tokens"),
               norm1_g=step(norm1_g, g_n1, m_norm1_g, v_norm1_g, "norm1_g"),
               w_in=step_big(0, w_in, m_w_in, v_w_in), pool_w=step_big(1, pool_w, m_pool_w, v_pool_w),
               pool_scale=step(pool_scale, g_pool_scale, m_pool_scale, v_pool_scale, "pool_scale"),
               conv_w=step(conv_w, g_conv_w, m_conv_w, v_conv_w, "conv_w"), conv_b=step(conv_b, g_conv_b, m_conv_b, v_conv_b, "conv_b"),
               gate_a_w=step_big(2, gate_a_w, m_gate_a_w, v_gate_a_w), gate_a_b=step(gate_a_b, g_ba, m_gate_a_b, v_gate_a_b, "gate_a_b"),
               gate_x_w=step_big(3, gate_x_w, m_gate_x_w, v_gate_x_w), gate_x_b=step(gate_x_b, g_bx, m_gate_x_b, v_gate_x_b, "gate_x_b"),
               lru_lambda=step(lru_lambda, g_lam, m_lru_lambda, v_lru_lambda, "lru_lambda"), w_out=step_big(4, w_out, m_w_out, v_w_out),
               norm2_g=step(norm2_g, g_n2, m_norm2_g, v_norm2_g, "norm2_g"), mlp_w1=step_big(5, mlp_w1, m_mlp_w1, v_mlp_w1),
               mlp_w2=step_big(6, mlp_w2, m_mlp_w2, v_mlp_w2), final_g=step(final_g, g_fin, m_final_g, v_final_g, "final_g"))
    order = list(res)
    return (loss, grad_x[None], *[res[n][0] for n in order], *[res[n][1] for n in order], *[res[n][2] for n in order],
            *[res[n][3] for n in order])
```

```python
import functools

import jax
import jax.numpy as jnp
from jax import lax
from jax.experimental import pallas as pl
from jax.experimental.pallas import tpu as pltpu

F32 = jnp.float32
BF16 = jnp.bfloat16
MESH = pl.DeviceIdType.MESH

NORM_EPS = 1e-6
N_META = 16
HEAD_DIM = 256
POOL_WINDOWS = (2, 4, 8, 16)
LRU_C = 8.0
ROW_TILE = 128
HIST = 16
VMEM_LIMIT_BYTES = 56 * 1024 * 1024
ADAM_LR, ADAM_B1, ADAM_B2, ADAM_EPS, ADAM_WD, ADAM_STEP = 0.001, 0.9, 0.999, 1e-08, 0.01, 10


def _pick(n, prefs):
    for p in prefs:
        if n % p == 0:
            return p
    return n


def _params(*sem):
    return pltpu.CompilerParams(dimension_semantics=sem, vmem_limit_bytes=VMEM_LIMIT_BYTES)


def _sigmoid(x):
    return 1.0 / (1.0 + jnp.exp(-x))


def _gelu_tanh(x):
    t = jnp.tanh(0.7978845608028654 * (x + 0.044715 * (x * x * x)))
    return 0.5 * x * (1.0 + t), t


def _gelu_tanh_grad(x, t):
    return 0.5 * (1.0 + t) + 0.5 * x * (1.0 - t * t) * (0.7978845608028654 * (1.0 + 3.0 * 0.044715 * x * x))


def _neg_expm1(x, exp_x):
    series = x * (-1.0 + x * (-0.5 + x * ((-1.0 / 6.0) + x * ((-1.0 / 24.0) + x * (-1.0 / 120.0)))))
    return jnp.where(x > -0.125, series, 1.0 - exp_x)


def _softplus_neg(lam):
    z = jnp.exp(-jnp.abs(lam))
    log1p_z = jnp.where(z < 0.01, z * (1.0 - z * (0.5 - z * (1.0 / 3.0))), jnp.log(1.0 + z))
    return jnp.maximum(-lam, 0.0) + log1p_z


def _tile_masks(is_meta, rows):
    row = lax.broadcasted_iota(jnp.int32, (rows, 1), 0)
    valid = jnp.logical_or(jnp.logical_not(is_meta), row >= rows - N_META)
    t_log = jnp.where(is_meta, row - (rows - N_META), 1 << 20)
    return row, valid, t_log


def _window_count_inv(t_log, w):
    return 1.0 / jnp.clip(t_log + 1, 1, w).astype(F32)


def _dot_nt(a, b):
    return lax.dot_general(a, b, (((1,), (1,)), ((), ())), preferred_element_type=F32)


def _dot_tn(a, b):
    return lax.dot_general(a, b, (((0,), (0,)), ((), ())), preferred_element_type=F32)


def _place():
    x, y, c = lax.axis_index("x"), lax.axis_index("y"), lax.axis_index("c")
    chips = [(1 - x, y), (x, 1 - y), (1 - x, 1 - y)]
    return x, y, c, chips


def _remote(src, dst, send_sem, recv_sem, to):
    return pltpu.make_async_remote_copy(src_ref=src, dst_ref=dst, send_sem=send_sem, recv_sem=recv_sem, device_id=to,
                                        device_id_type=MESH)


class Stage:
    def __init__(self, arrays, out_shapes, aliases, n_copies, copies):
        self.arrays, self.out_shapes, self.aliases, self.n_copies, self.copies = list(arrays), list(out_shapes), aliases, n_copies, copies


def _sds(a):
    return jax.ShapeDtypeStruct(a.shape, a.dtype)


def _relay_rows(buf_ref, chip_idx, c, quarter):
    rows = buf_ref.shape[2] // 2
    return buf_ref.at[chip_idx, c, pl.ds(quarter * rows, rows)]


def stage_gather_direct(bufs, quarter=None):
    n = len(bufs)

    def copies(ins, outs, send, recv):
        x, y, c, chips = _place()
        me = 2 * x + y
        sends, recvs = [], []
        for t in range(n):
            for j, (px, py) in enumerate(chips[:2]):
                k = 2 * t + j
                if quarter is None:
                    mine, theirs = outs[t].at[me, c], outs[t].at[2 * px + py, c]
                else:
                    mine, theirs = _relay_rows(outs[t], me, c, quarter), _relay_rows(outs[t], 2 * px + py, c, quarter)
                sends.append(_remote(mine, mine, send.at[k], recv.at[k], (px, py, c)))
                recvs.append(_remote(theirs, theirs, send.at[k], recv.at[k], (px, py, c)))
        return sends, recvs

    return Stage(bufs, [_sds(b) for b in bufs], {t: t for t in range(n)}, 2 * n, copies)


def stage_gather_relay(bufs):
    n = len(bufs)

    def copies(ins, outs, send, recv):
        x, y, c, chips = _place()
        (xx, xy), (yx, yy), (dx, dy) = chips
        sends, recvs = [], []
        for t in range(n):
            from_y, from_x = _relay_rows(outs[t], 2 * yx + yy, c, 0), _relay_rows(outs[t], 2 * xx + xy, c, 1)
            sends.append(_remote(from_y, from_y, send.at[2 * t], recv.at[2 * t], (xx, xy, c)))
            sends.append(_remote(from_x, from_x, send.at[2 * t + 1], recv.at[2 * t + 1], (yx, yy, c)))
            for q, (px, py) in enumerate(chips[:2]):
                got = _relay_rows(outs[t], 2 * dx + dy, c, q)
                recvs.append(_remote(got, got, send.at[2 * t + q], recv.at[2 * t + q], (px, py, c)))
        return sends, recvs

    return Stage(bufs, [_sds(b) for b in bufs], {t: t for t in range(n)}, 2 * n, copies)


def stage_gather_d2d(bufs, peers=(0, 1, 2)):
    n = len(bufs)

    def copies(ins, outs, send, recv):
        x, y, c, chips = _place()
        sends, recvs = [], []
        for t in range(n):
            for slot, j in enumerate(peers):
                px, py = chips[j]
                k = len(peers) * t + slot
                got, sib = outs[t].at[2 * px + py, c], outs[t].at[2 * px + py, 1 - c]
                sends.append(_remote(got, got, send.at[k], recv.at[k], (x, y, 1 - c)))
                recvs.append(_remote(sib, sib, send.at[k], recv.at[k], (x, y, 1 - c)))
        return sends, recvs

    return Stage(bufs, [_sds(b) for b in bufs], {t: t for t in range(n)}, len(peers) * n, copies)


class _SemsFrom:
    def __init__(self, sems, first):
        self.sems, self.first = sems, first

    @property
    def at(self):
        return self

    def __getitem__(self, k):
        return self.sems.at[self.first + k]


def stage_both(first, second):
    def copies(ins, outs, send, recv):
        s1, r1 = first.copies(ins, outs, send, recv)
        s2, r2 = second.copies(ins, outs, _SemsFrom(send, first.n_copies), _SemsFrom(recv, first.n_copies))
        return s1 + s2, r1 + r2

    return Stage(first.arrays, first.out_shapes, first.aliases, first.n_copies + second.n_copies, copies)


def stage_to_sibling(grads):
    n = len(grads)

    def copies(ins, outs, send, recv):
        x, y, c, _ = _place()
        sends, recvs = [], []
        for t in range(n):
            for k4 in range(4):
                k = 4 * t + k4
                sends.append(_remote(ins[t].at[k4, 1 - c], outs[t].at[k4], send.at[k], recv.at[k], (x, y, 1 - c)))
                recvs.append(_remote(outs[t].at[k4], outs[t].at[k4], send.at[k], recv.at[k], (x, y, 1 - c)))
        return sends, recvs

    return Stage(grads, [jax.ShapeDtypeStruct((4, *g.shape[2:]), g.dtype) for g in grads], {}, 4 * n, copies)


def stage_to_chips(presums, peers=(0, 1, 2), into=None):
    n = len(presums)

    def copies(ins, outs, send, recv):
        x, y, c, chips = _place()
        sends, recvs = [], []
        for t in range(n):
            for slot, j in enumerate(peers):
                px, py = chips[j]
                k = len(peers) * t + slot
                sends.append(_remote(ins[t].at[2 * px + py], outs[t].at[j], send.at[k], recv.at[k], (px, py, c)))
                recvs.append(_remote(outs[t].at[j], outs[t].at[j], send.at[k], recv.at[k], (px, py, c)))
        return sends, recvs

    out_shapes = [jax.ShapeDtypeStruct((3, *p.shape[1:]), p.dtype) for p in presums]
    if into is None:
        return Stage(presums, out_shapes, {}, len(peers) * n, copies)
    return Stage([*presums, *into], out_shapes, {n + t: t for t in range(n)}, len(peers) * n, copies)


def stage_from_sibling(halves):
    n = len(halves)

    def copies(ins, outs, send, recv):
        x, y, c, _ = _place()
        sends = [_remote(ins[t], outs[t], send.at[t], recv.at[t], (x, y, 1 - c)) for t in range(n)]
        recvs = [_remote(outs[t], outs[t], send.at[t], recv.at[t], (x, y, 1 - c)) for t in range(n)]
        return sends, recvs

    return Stage(halves, [_sds(h) for h in halves], {}, n, copies)


def _any_specs(n):
    return [pl.BlockSpec(memory_space=pl.ANY)] * n


def _staged_call(body, *, grid, in_specs, out_specs, out_shape, scratch_shapes, name, semantics, inputs, stages=()):
    n_in, n_out, n_scr = len(in_specs), len(out_specs), len(scratch_shapes)
    st_arrays = [a for s in stages for a in s.arrays]
    st_outs = [o for s in stages for o in s.out_shapes]
    st_sems = [pltpu.SemaphoreType.DMA((s.n_copies,)) for s in stages for _ in range(2)]
    aliases = {}
    at_in, at_out = n_in, n_out
    for s in stages:
        for a, o in s.aliases.items():
            aliases[at_in + a] = at_out + o
        at_in += len(s.arrays)
        at_out += len(s.out_shapes)

    def full_body(*refs):
        pos = 0

        def take(count):
            nonlocal pos
            part = refs[pos:pos + count]
            pos += count
            return part

        ins, s_ins, outs, s_outs, scr = take(n_in), take(len(st_arrays)), take(n_out), take(len(st_outs)), take(n_scr)
        s_sems = refs[pos:]

        def each_stage(action):
            at_i = at_o = 0
            for idx, s in enumerate(stages):
                sends, recvs = s.copies(s_ins[at_i:at_i + len(s.arrays)], s_outs[at_o:at_o + len(s.out_shapes)],
                                        s_sems[2 * idx], s_sems[2 * idx + 1])
                action(sends, recvs)
                at_i += len(s.arrays)
                at_o += len(s.out_shapes)

        if stages:
            ids = [pl.program_id(a) for a in range(len(grid))]
            first = functools.reduce(jnp.logical_and, [i == 0 for i in ids])
            last = functools.reduce(jnp.logical_and, [i == g - 1 for i, g in zip(ids, grid)])

            def start(sends, recvs):
                for cp in sends:
                    cp.start()

            def finish(sends, recvs):
                for cp in recvs:
                    cp.wait_recv()
                for cp in sends:
                    cp.wait_send()

            @pl.when(first)
            def _():
                each_stage(start)

        body(*ins, *outs, *scr)

        if stages:
            @pl.when(last)
            def _():
                each_stage(finish)

    sem = tuple("arbitrary" for _ in grid) if stages else tuple(semantics)
    res = pl.pallas_call(
        full_body, grid=grid, in_specs=[*in_specs, *_any_specs(len(st_arrays))], out_specs=[*out_specs, *_any_specs(len(st_outs))],
        out_shape=[*out_shape, *st_outs], scratch_shapes=[*scratch_shapes, *st_sems], input_output_aliases=aliases, name=name,
        compiler_params=_params(*sem))(*inputs, *st_arrays)
    outs, rest = list(res[:n_out]), list(res[n_out:])
    per_stage = []
    for s in stages:
        per_stage.append(rest[:len(s.out_shapes)])
        rest = rest[len(s.out_shapes):]
    return outs, per_stage


def comm_call(stages, *, name):
    return _staged_call(lambda: None, grid=(1,), in_specs=[], out_specs=[], out_shape=[], scratch_shapes=[], name=name,
                        semantics=("arbitrary",), inputs=[], stages=stages)[1]


def _matmul(kind, a, b, *, grid, a_spec, b_spec, out_spec, out_shape, acc_shape, name,
            a_pro=None, epilogue=None, extras=(), extra_specs=(), stages=()):
    nk = grid[2]
    n_extra = len(extras)

    def body(a_ref, b_ref, *rest):
        extra_refs = rest[:n_extra]
        o_ref = rest[n_extra]
        av = a_ref[...]
        if a_pro is not None:
            av = a_pro(av)
        av = av.astype(BF16)
        bv = b_ref[...].astype(BF16)
        if kind == "nn":
            p = jnp.dot(av, bv, preferred_element_type=F32)
        elif kind == "nt":
            p = _dot_nt(av, bv)
        else:
            p = _dot_tn(av, bv)

        def finish(r):
            if epilogue is not None:
                r = epilogue(r, *[e[...] for e in extra_refs])
            o_ref[...] = r.astype(o_ref.dtype)

        if nk == 1:
            finish(p)
        else:
            acc_ref = rest[n_extra + 1]
            k = pl.program_id(2)

            @pl.when(k == 0)
            def _():
                acc_ref[...] = p

            @pl.when(k > 0)
            def _():
                acc_ref[...] += p

            @pl.when(k == nk - 1)
            def _():
                finish(acc_ref[...])

    scratch = [] if nk == 1 else [pltpu.VMEM(acc_shape, F32)]
    outs, staged = _staged_call(body, grid=grid, in_specs=[a_spec, b_spec, *extra_specs], out_specs=[out_spec], out_shape=[out_shape],
                                scratch_shapes=scratch, name=name, semantics=("parallel", "parallel", "arbitrary"),
                                inputs=[a, b, *extras], stages=stages)
    return (outs[0], staged) if stages else outs[0]


def mm_nn(a, b, *, out_dtype, name, a_pro=None, epilogue=None, extras=(), stages=(), out_shards=1, tiles=(None, None, None)):
    M, K = a.shape
    sharded = b.ndim == 3
    ns = b.shape[2] if sharded else b.shape[1]
    N = ns * b.shape[0] if sharded else ns
    if out_shards > 1:
        ns = N // out_shards
    tm = tiles[0] or _pick(M, (1408, 1024, 512, 256, 128))
    tn = tiles[1] or _pick(ns, (512, 256, 128))
    tk = tiles[2] or _pick(K, (2048, 1408, 1024, 512, 256, 128))
    per = ns // tn
    if sharded:
        b_spec = pl.BlockSpec((None, tk, tn), lambda i, j, k: (j // per, k, j % per))
    else:
        b_spec = pl.BlockSpec((tk, tn), lambda i, j, k: (k, j))
    mn = pl.BlockSpec((tm, tn), lambda i, j, k: (i, j))
    if out_shards > 1:
        out_spec = pl.BlockSpec((None, tm, tn), lambda i, j, k: (j // per, i, j % per))
        out_shape = jax.ShapeDtypeStruct((out_shards, M, ns), out_dtype)
    else:
        out_spec, out_shape = mn, jax.ShapeDtypeStruct((M, N), out_dtype)
    return _matmul("nn", a, b, grid=(M // tm, N // tn, K // tk), a_spec=pl.BlockSpec((tm, tk), lambda i, j, k: (i, k)),
                   b_spec=b_spec, out_spec=out_spec, out_shape=out_shape, acc_shape=(tm, tn),
                   name=name, a_pro=a_pro, epilogue=epilogue, extras=extras, extra_specs=[mn] * len(extras), stages=stages)


def mm_nt(a, w, *, out_dtype, name, epilogue=None, extras=(), stages=(), tiles=(None, None, None)):
    M, N = a.shape
    sharded = w.ndim == 3
    kw = w.shape[1] if sharded else w.shape[0]
    ns = w.shape[2] if sharded else w.shape[1]
    tm = tiles[0] or _pick(M, (1408, 512, 256, 128))
    tkw = tiles[1] or _pick(kw, (1024, 512, 256, 128))
    tk = tiles[2] or _pick(ns, (2048, 1280, 1024, 512, 256, 128))
    per = ns // tk
    if sharded:
        w_spec = pl.BlockSpec((None, tkw, tk), lambda i, j, k: (k // per, j, k % per))
    else:
        w_spec = pl.BlockSpec((tkw, tk), lambda i, j, k: (j, k))
    mo = pl.BlockSpec((tm, tkw), lambda i, j, k: (i, j))
    return _matmul("nt", a, w, grid=(M // tm, kw // tkw, N // tk), a_spec=pl.BlockSpec((tm, tk), lambda i, j, k: (i, k)),
                   b_spec=w_spec, out_spec=mo, out_shape=jax.ShapeDtypeStruct((M, kw), out_dtype), acc_shape=(tm, tkw),
                   name=name, epilogue=epilogue, extras=extras, extra_specs=[mo] * len(extras), stages=stages)


def mm_tn(a, g, *, shards, name, out_dtype, a_pro=None, stages=(), tiles=(None, None, None)):
    T, kw = a.shape
    N = g.shape[1]
    ns = N // shards
    tkw = tiles[0] or _pick(kw, (1024, 512, 256, 128))
    tn = tiles[1] or _pick(ns, (1280, 1024, 512, 256, 128))
    tt = tiles[2] or _pick(T, (1408, 512, 256, 128))
    per = ns // tn
    if shards > 1:
        out_spec = pl.BlockSpec((None, tkw, tn), lambda i, j, k: (j // per, i, j % per))
        out_shape = jax.ShapeDtypeStruct((shards, kw, ns), out_dtype)
    else:
        out_spec = pl.BlockSpec((tkw, tn), lambda i, j, k: (i, j))
        out_shape = jax.ShapeDtypeStruct((kw, N), out_dtype)
    return _matmul("tn", a, g, grid=(kw // tkw, N // tn, T // tt), a_spec=pl.BlockSpec((tt, tkw), lambda i, j, k: (k, i)),
                   b_spec=pl.BlockSpec((tt, tn), lambda i, j, k: (k, j)), out_spec=out_spec, out_shape=out_shape,
                   acc_shape=(tkw, tn), name=name, a_pro=a_pro, stages=stages)


def _relu_sq(a):
    r = jnp.maximum(a, 0.0)
    return r * r


def rmsnorm_fwd(h, g, *, name, stages=()):
    Tp, D = h.shape
    tr = _pick(Tp, (384, 256, 128))

    def body(h_ref, g_ref, u_ref, ut_ref):
        x = h_ref[...]
        r = lax.rsqrt(jnp.mean(x * x, axis=-1, keepdims=True) + NORM_EPS)
        u = (x * r) * g_ref[...]
        u_ref[...] = u.astype(BF16)
        ut_ref[...] = u.T.astype(BF16)

    row = pl.BlockSpec((tr, D), lambda i: (i, 0))
    outs, staged = _staged_call(body, grid=(Tp // tr,), in_specs=[row, pl.BlockSpec((1, D), lambda i: (0, 0))],
                                out_specs=[row, pl.BlockSpec((D, tr), lambda i: (0, i))],
                                out_shape=[jax.ShapeDtypeStruct((Tp, D), BF16), jax.ShapeDtypeStruct((D, Tp), BF16)],
                                scratch_shapes=[], name=name, semantics=("parallel",), inputs=[h, g], stages=stages)
    return (outs, staged) if stages else outs


def rmsnorm_fwd_input(x, meta_tile, g, *, name, stages=()):
    S, D = x.shape
    nt = S // ROW_TILE + 1

    def body(x_ref, mt_ref, g_ref, h_ref, u_ref, ut_ref):
        h = jnp.where(pl.program_id(0) == nt - 1, mt_ref[...], x_ref[...])
        r = lax.rsqrt(jnp.mean(h * h, axis=-1, keepdims=True) + NORM_EPS)
        u = (h * r) * g_ref[...]
        h_ref[...] = h
        u_ref[...] = u.astype(BF16)
        ut_ref[...] = u.T.astype(BF16)

    row = pl.BlockSpec((ROW_TILE, D), lambda i: (i, 0))
    outs, staged = _staged_call(
        body, grid=(nt,),
        in_specs=[pl.BlockSpec((ROW_TILE, D), lambda i: (jnp.minimum(i, nt - 2), 0)), pl.BlockSpec((ROW_TILE, D), lambda i: (0, 0)),
                  pl.BlockSpec((1, D), lambda i: (0, 0))],
        out_specs=[row, row, pl.BlockSpec((D, ROW_TILE), lambda i: (0, i))],
        out_shape=[jax.ShapeDtypeStruct((S + ROW_TILE, D), F32), jax.ShapeDtypeStruct((S + ROW_TILE, D), BF16),
                   jax.ShapeDtypeStruct((D, S + ROW_TILE), BF16)],
        scratch_shapes=[], name=name, semantics=("parallel",), inputs=[x, meta_tile, g], stages=stages)
    return (outs, staged) if stages else outs


def _rms_bwd_math(x, g, dy):
    r = lax.rsqrt(jnp.mean(x * x, axis=-1, keepdims=True) + NORM_EPS)
    xh = x * r
    dyg = dy * g
    dx = r * (dyg - xh * jnp.mean(dyg * xh, axis=-1, keepdims=True))
    return dx, xh


def final_loss(h2, target, gf, *, name):
    Tp, D = h2.shape
    nt = Tp // ROW_TILE

    def body(h_ref, t_ref, g_ref, dhb_ref, st_ref):
        i = pl.program_id(0)

        @pl.when(i == 0)
        def _():
            st_ref[...] = jnp.zeros_like(st_ref)

        x = h_ref[...]
        g = g_ref[...]
        r = lax.rsqrt(jnp.mean(x * x, axis=-1, keepdims=True) + NORM_EPS)
        xh = x * r
        err = jnp.where(i == nt - 1, 0.0, xh * g - t_ref[...])
        dout = err * (1.0 / D)
        dyg = dout * g
        dx = r * (dyg - xh * jnp.mean(dyg * xh, axis=-1, keepdims=True))
        dhb_ref[...] = dx.astype(BF16)
        st_ref[0:1, :] += jnp.sum(dout * xh, axis=0, keepdims=True)
        st_ref[1:2, :] += jnp.sum(err * err, axis=0, keepdims=True) * (0.5 / D)

    row = pl.BlockSpec((ROW_TILE, D), lambda i: (i, 0))
    return pl.pallas_call(
        body, grid=(nt,),
        in_specs=[row, pl.BlockSpec((ROW_TILE, D), lambda i: (jnp.minimum(i, nt - 2), 0)), pl.BlockSpec((1, D), lambda i: (0, 0))],
        out_specs=[row, pl.BlockSpec((8, D), lambda i: (0, 0))],
        out_shape=[jax.ShapeDtypeStruct((Tp, D), BF16), jax.ShapeDtypeStruct((8, D), F32)],
        name=name, compiler_params=_params("arbitrary"))(h2, target, gf)


def rms_bwd(h, g, du, dres, *, name):
    Tp, D = h.shape
    tr = _pick(Tp, (384, 256, 128))

    def body(h_ref, g_ref, du_ref, dr_ref, dhb_ref, st_ref):
        @pl.when(pl.program_id(0) == 0)
        def _():
            st_ref[...] = jnp.zeros_like(st_ref)

        du_v = du_ref[...].astype(F32)
        dx, xh = _rms_bwd_math(h_ref[...], g_ref[...], du_v)
        dhb_ref[...] = (dr_ref[...].astype(F32) + dx).astype(BF16)
        st_ref[0:1, :] += jnp.sum(du_v * xh, axis=0, keepdims=True)

    row = pl.BlockSpec((tr, D), lambda i: (i, 0))
    return pl.pallas_call(
        body, grid=(Tp // tr,), in_specs=[row, pl.BlockSpec((1, D), lambda i: (0, 0)), row, row],
        out_specs=[row, pl.BlockSpec((8, D), lambda i: (0, 0))],
        out_shape=[jax.ShapeDtypeStruct((Tp, D), BF16), jax.ShapeDtypeStruct((8, D), F32)],
        name=name, compiler_params=_params("arbitrary"))(h, g, du, dres)


def rms_bwd_input(h, g, du, dres, *, name, stages=()):
    Tp, D = h.shape
    nt = Tp // ROW_TILE

    def body(h_ref, g_ref, du_ref, dr_ref, gx_ref, gm_ref, st_ref):
        i = pl.program_id(0)

        @pl.when(i == 0)
        def _():
            st_ref[...] = jnp.zeros_like(st_ref)

        du_v = du_ref[...].astype(F32)
        dx, xh = _rms_bwd_math(h_ref[...], g_ref[...], du_v)
        dh = dr_ref[...].astype(F32) + dx
        st_ref[0:1, :] += jnp.sum(du_v * xh, axis=0, keepdims=True)

        @pl.when(i < nt - 1)
        def _():
            gx_ref[...] = dh

        @pl.when(i == nt - 1)
        def _():
            gm_ref[...] = dh[ROW_TILE - N_META:, :]

    row = pl.BlockSpec((ROW_TILE, D), lambda i: (i, 0))
    outs, staged = _staged_call(
        body, grid=(nt,), in_specs=[row, pl.BlockSpec((1, D), lambda i: (0, 0)), row, row],
        out_specs=[pl.BlockSpec((ROW_TILE, D), lambda i: (jnp.minimum(i, nt - 2), 0)), pl.BlockSpec((N_META, D), lambda i: (0, 0)),
                   pl.BlockSpec((8, D), lambda i: (0, 0))],
        out_shape=[jax.ShapeDtypeStruct((Tp - ROW_TILE, D), F32), jax.ShapeDtypeStruct((N_META, D), F32),
                   jax.ShapeDtypeStruct((8, D), F32)],
        scratch_shapes=[], name=name, semantics=("arbitrary",), inputs=[h, g, du, dres], stages=stages)
    return (outs, staged) if stages else outs


def _conv_shifts(ext):
    return tuple(pltpu.roll(ext, k, 0)[HIST:, :] for k in (1, 2, 3))


def _conv_taps(ext, cur, vec_ref, cs):
    x1, x2, x3 = _conv_shifts(ext)
    return vec_ref[1:2, cs] + vec_ref[8:9, cs] * cur + vec_ref[7:8, cs] * x1 + vec_ref[6:7, cs] * x2 + vec_ref[5:6, cs] * x3


def _window_sum_back(ext, w):
    s, sh = ext, 1
    while sh < w:
        s = s + pltpu.roll(s, sh, 0)
        sh *= 2
    return s[HIST:, :]


def _scan_rows(a, b, carry, *, reverse):
    rows = a.shape[0]
    rin = jnp.bitwise_and(lax.broadcasted_iota(jnp.int32, (rows, 1), 0), 7)
    sh = 1
    while sh < 8:
        keep = (rin < 8 - sh) if reverse else (rin >= sh)
        amount = rows - sh if reverse else sh
        a_sh = jnp.where(keep, pltpu.roll(a, amount, 0), 1.0)
        b_sh = jnp.where(keep, pltpu.roll(b, amount, 0), 0.0)
        b = b + a * b_sh
        a = a * a_sh
        sh *= 2
    out = [None] * (rows // 8)
    for g in (reversed(range(rows // 8)) if reverse else range(rows // 8)):
        hg = b[8 * g:8 * g + 8, :] + a[8 * g:8 * g + 8, :] * carry
        carry = hg[0:1, :] if reverse else hg[7:8, :]
        out[g] = hg
    return jnp.concatenate(out, axis=0)


def _lru_gates(xc, wa, wx, vec_ref, cs, sp):
    xcb = xc.astype(BF16)
    r = _sigmoid(jnp.dot(xcb, wa, preferred_element_type=F32) + vec_ref[2:3, cs])
    ig = _sigmoid(jnp.dot(xcb, wx, preferred_element_type=F32) + vec_ref[3:4, cs])
    log_a = (-LRU_C) * r * sp
    a = jnp.exp(log_a)
    a2 = a * a
    return xcb, r, ig, a, a2, _neg_expm1(2.0 * log_a, a2)


def mix_fwd(proj, pool_w, gate_a, gate_x, vecs, *, name, stages=()):
    Tp = proj.shape[0]
    D = proj.shape[1] // 5
    R = ROW_TILE
    nt = Tp // R
    H = D // HEAD_DIM
    PG = D // len(POOL_WINDOWS)

    def body(p_ref, pw_ref, wa_ref, wx_ref, vec_ref, m_ref, hs_ref, xc_ref, r_ref, ig_ref, a_ref, mu_ref, ge_ref, dge_ref,
             hist_p, hist_l, hcar, mtmp):
        i = pl.program_id(0)
        is_meta = i == 0

        @pl.when(is_meta)
        def _():
            hist_p[...] = jnp.zeros_like(hist_p)
            hist_l[...] = jnp.zeros_like(hist_l)
            hcar[...] = jnp.zeros_like(hcar)

        row, valid, t_log = _tile_masks(is_meta, R)

        for g, w in enumerate(POOL_WINDOWS):
            cs = slice(g * PG, (g + 1) * PG)
            v = p_ref[:, g * PG:(g + 1) * PG].astype(F32)
            ws = _window_sum_back(jnp.concatenate([hist_p[:, cs], v], axis=0), w)
            d = ws * _window_count_inv(t_log, w) - v
            y = jnp.dot(d.astype(BF16), pw_ref[g], preferred_element_type=F32)
            gp = p_ref[:, 3 * D + g * PG:3 * D + (g + 1) * PG].astype(F32)
            mtmp[:, cs] = _sigmoid(gp) * (y * vec_ref[0:1, cs])
            hist_p[:, cs] = v[R - HIST:, :]

        for h in range(H):
            cs = slice(h * HEAD_DIM, (h + 1) * HEAD_DIM)
            vl = p_ref[:, D + h * HEAD_DIM:D + (h + 1) * HEAD_DIM].astype(F32)
            xc = _conv_taps(jnp.concatenate([hist_l[:, cs], vl], axis=0), vl, vec_ref, cs)
            sp = _softplus_neg(vec_ref[4:5, cs])
            xcb, r, ig, a, _, em = _lru_gates(xc, wa_ref[h], wx_ref[h], vec_ref, cs, sp)
            mult = jnp.sqrt(em)
            b = jnp.where(valid, mult * (ig * xc), 0.0)
            hs = _scan_rows(a, b, hcar[7:8, cs], reverse=False)
            hs_ref[:, cs] = hs
            xc_ref[:, cs], r_ref[:, cs], ig_ref[:, cs] = xcb, r.astype(BF16), ig.astype(BF16)
            a_ref[:, cs], mu_ref[:, cs] = a, mult
            hcar[:, cs] = hs[R - 8:, :]
            hist_l[:, cs] = vl[R - HIST:, :]
            vg = p_ref[:, 2 * D + h * HEAD_DIM:2 * D + (h + 1) * HEAD_DIM].astype(F32)
            ge, th = _gelu_tanh(vg)
            ge_ref[:, cs], dge_ref[:, cs] = ge.astype(BF16), _gelu_tanh_grad(vg, th).astype(BF16)
            gl = p_ref[:, 4 * D + h * HEAD_DIM:4 * D + (h + 1) * HEAD_DIM].astype(F32)
            m_ref[:, cs] = (mtmp[:, cs] + _sigmoid(gl) * (hs * ge)).astype(BF16)

    def tile(i):
        return (i + nt - 1) % nt

    full = lambda shape: pl.BlockSpec(shape, lambda i: (0,) * len(shape))
    outs, staged = _staged_call(
        body, grid=(nt,),
        in_specs=[pl.BlockSpec((R, 5 * D), lambda i: (tile(i), 0)), full(pool_w.shape), full(gate_a.shape), full(gate_x.shape),
                  full(vecs.shape)],
        out_specs=[pl.BlockSpec((R, D), lambda i: (tile(i), 0))] * 9,
        out_shape=[jax.ShapeDtypeStruct((Tp, D), dt) for dt in (BF16, F32, BF16, BF16, BF16, F32, F32, BF16, BF16)],
        scratch_shapes=[pltpu.VMEM((HIST, D), F32), pltpu.VMEM((HIST, D), F32), pltpu.VMEM((8, D), F32), pltpu.VMEM((R, D), F32)],
        name=name, semantics=("arbitrary",), inputs=[proj, pool_w, gate_a, gate_x, vecs], stages=stages)
    outs = [outs[0], outs[1], outs[2:]]
    return (outs, staged) if stages else outs


def mix_bwd(proj, hs, saved, dmerged, pool_w, gate_a, gate_x, vecs, *, name, stages=()):
    Tp = proj.shape[0]
    D = proj.shape[1] // 5
    R = ROW_TILE
    nt = Tp // R
    H = D // HEAD_DIM
    PG = D // len(POOL_WINDOWS)

    def body(p_ref, pprev_ref, hs_ref, hprev_ref, dm_ref, xc_ref, r_ref, ig_ref, a_ref, mu_ref, ge_ref, dge_ref, pw_ref, wa_ref, wx_ref, vec_ref,
             dp_ref, dpw_ref, dwa_ref, dwx_ref, dvec_ref, car_g, fut_dxc, fut_q):
        i = pl.program_id(0)
        is_meta = i == nt - 1

        @pl.when(i == 0)
        def _():
            dpw_ref[...] = jnp.zeros_like(dpw_ref)
            dwa_ref[...] = jnp.zeros_like(dwa_ref)
            dwx_ref[...] = jnp.zeros_like(dwx_ref)
            dvec_ref[...] = jnp.zeros_like(dvec_ref)
            car_g[...] = jnp.zeros_like(car_g)
            fut_dxc[...] = jnp.zeros_like(fut_dxc)
            fut_q[...] = jnp.zeros_like(fut_q)

        row, valid, t_log = _tile_masks(is_meta, R)
        keep_prev = jnp.logical_not(is_meta)

        def colsum(x):
            return jnp.sum(x, axis=0, keepdims=True)

        for g, w in enumerate(POOL_WINDOWS):
            cs = slice(g * PG, (g + 1) * PG)
            v = p_ref[:, g * PG:(g + 1) * PG].astype(F32)
            vprev = jnp.where(keep_prev, pprev_ref[:, g * PG:(g + 1) * PG].astype(F32), 0.0)
            inv_cnt = _window_count_inv(t_log, w)
            d = _window_sum_back(jnp.concatenate([vprev, v], axis=0), w) * inv_cnt - v
            d_bf = d.astype(BF16)
            y = jnp.dot(d_bf, pw_ref[g], preferred_element_type=F32)
            scale = vec_ref[0:1, cs]
            sg = _sigmoid(p_ref[:, 3 * D + g * PG:3 * D + (g + 1) * PG].astype(F32))
            dm = dm_ref[:, cs].astype(F32)
            dpo = dm * sg
            dp_ref[:, 3 * D + g * PG:3 * D + (g + 1) * PG] = (dm * (y * scale) * sg * (1.0 - sg)).astype(BF16)
            dvec_ref[0:1, cs] += colsum(dpo * y)
            dy = (dpo * scale).astype(BF16)
            dd = _dot_nt(dy, pw_ref[g])
            dpw_ref[g] += _dot_tn(d_bf, dy)
            q = dd * inv_cnt
            s, sh = jnp.concatenate([q, fut_q[:, cs]], axis=0), 1
            while sh < w:
                s = s + pltpu.roll(s, R + HIST - sh, 0)
                sh *= 2
            dp_ref[:, g * PG:(g + 1) * PG] = (s[:R, :] - dd).astype(BF16)
            fut_q[:, cs] = q[:HIST, :]

        for h in range(H):
            cs = slice(h * HEAD_DIM, (h + 1) * HEAD_DIM)
            pc = lambda blk: slice(blk * D + h * HEAD_DIM, blk * D + (h + 1) * HEAD_DIM)
            vl = p_ref[:, pc(1)].astype(F32)
            vlprev = jnp.where(keep_prev, pprev_ref[:, pc(1)].astype(F32), 0.0)
            x1, x2, x3 = _conv_shifts(jnp.concatenate([vlprev, vl], axis=0))
            lam = vec_ref[4:5, cs]
            sp = _softplus_neg(lam)
            xcb = xc_ref[:, cs]
            xc, r, ig = xcb.astype(F32), r_ref[:, cs].astype(F32), ig_ref[:, cs].astype(F32)
            a, mult = a_ref[:, cs], mu_ref[:, cs]
            a2 = a * a
            inv_mult = 1.0 / mult
            hsv = hs_ref[:, cs]
            hprev = jnp.where(row >= 1, pltpu.roll(hsv, 1, 0), hprev_ref[HIST - 1:HIST, cs])
            ge = ge_ref[:, cs].astype(F32)
            sgl = _sigmoid(p_ref[:, pc(4)].astype(F32))
            dm = dm_ref[:, cs].astype(F32)
            dlo = dm * sgl
            dp_ref[:, pc(4)] = (dm * (hsv * ge) * sgl * (1.0 - sgl)).astype(BF16)
            dp_ref[:, pc(2)] = (dlo * hsv * dge_ref[:, cs].astype(F32)).astype(BF16)
            a_next = jnp.where(row < R - 1, pltpu.roll(a, R - 1, 0), 1.0)
            G = _scan_rows(a_next, dlo * ge, car_g[0:1, cs], reverse=True)
            car_g[:, cs] = (a * G)[0:8, :]
            da = jnp.where(valid, G * hprev, 0.0)
            db = jnp.where(valid, G, 0.0)
            dmult = db * (ig * xc)
            dig = db * (mult * xc)
            dxc = db * (mult * ig)
            dlog_a = da * a - dmult * (a2 * inv_mult)
            dvec_ref[4:5, cs] += colsum(dlog_a * r) * (-LRU_C)
            dr = dlog_a * ((-LRU_C) * sp)
            dpa = dr * r * (1.0 - r)
            dpx = dig * ig * (1.0 - ig)
            dpa_bf = dpa.astype(BF16)
            dpx_bf = dpx.astype(BF16)
            dwa_ref[h] += _dot_tn(xcb, dpa_bf)
            dwx_ref[h] += _dot_tn(xcb, dpx_bf)
            dvec_ref[2:3, cs] += colsum(dpa)
            dvec_ref[3:4, cs] += colsum(dpx)
            dxc = dxc + _dot_nt(dpa_bf, wa_ref[h]) + _dot_nt(dpx_bf, wx_ref[h])
            ext = jnp.concatenate([dxc, fut_dxc[:, cs]], axis=0)
            n = R + HIST
            dvl = (vec_ref[8:9, cs] * dxc + vec_ref[7:8, cs] * pltpu.roll(ext, n - 1, 0)[:R, :]
                   + vec_ref[6:7, cs] * pltpu.roll(ext, n - 2, 0)[:R, :] + vec_ref[5:6, cs] * pltpu.roll(ext, n - 3, 0)[:R, :])
            dp_ref[:, pc(1)] = dvl.astype(BF16)
            dvec_ref[1:2, cs] += colsum(dxc)
            dvec_ref[8:9, cs] += colsum(dxc * vl)
            dvec_ref[7:8, cs] += colsum(dxc * x1)
            dvec_ref[6:7, cs] += colsum(dxc * x2)
            dvec_ref[5:6, cs] += colsum(dxc * x3)
            fut_dxc[:, cs] = dxc[:HIST, :]

            @pl.when(is_meta)
            def _():
                dvec_ref[4:5, cs] = dvec_ref[4:5, cs] * (-_sigmoid(-lam))

    def tile(i):
        return (2 * nt - 2 - i) % nt

    def prev_blk(i):
        per = R // HIST
        return jnp.where(i == nt - 1, 0, jnp.where(i == nt - 2, Tp // HIST - 1, (nt - 2 - i) * per - 1))

    full = lambda shape: pl.BlockSpec(shape, lambda i: (0,) * len(shape))
    G_ = len(POOL_WINDOWS)
    outs, staged = _staged_call(
        body, grid=(nt,),
        in_specs=[pl.BlockSpec((R, 5 * D), lambda i: (tile(i), 0)), pl.BlockSpec((HIST, 5 * D), lambda i: (prev_blk(i), 0)),
                  pl.BlockSpec((R, D), lambda i: (tile(i), 0)), pl.BlockSpec((HIST, D), lambda i: (prev_blk(i), 0)),
                  *[pl.BlockSpec((R, D), lambda i: (tile(i), 0))] * 8,
                  full(pool_w.shape), full(gate_a.shape), full(gate_x.shape), full(vecs.shape)],
        out_specs=[pl.BlockSpec((R, 5 * D), lambda i: (tile(i), 0)), full((G_, PG, PG)), full((H, HEAD_DIM, HEAD_DIM)),
                   full((H, HEAD_DIM, HEAD_DIM)), full((16, D))],
        out_shape=[jax.ShapeDtypeStruct((Tp, 5 * D), BF16), jax.ShapeDtypeStruct((G_, PG, PG), F32),
                   jax.ShapeDtypeStruct((H, HEAD_DIM, HEAD_DIM), F32), jax.ShapeDtypeStruct((H, HEAD_DIM, HEAD_DIM), F32),
                   jax.ShapeDtypeStruct((16, D), F32)],
        scratch_shapes=[pltpu.VMEM((8, D), F32), pltpu.VMEM((HIST, D), F32), pltpu.VMEM((HIST, D), F32)],
        name=name, semantics=("arbitrary",), inputs=[proj, proj, hs, hs, dmerged, *saved, pool_w, gate_a, gate_x, vecs], stages=stages)
    return (outs, staged) if stages else outs


def _adamw_math(w, g, m, v):
    mn = ADAM_B1 * m + (1.0 - ADAM_B1) * g
    vn = ADAM_B2 * v + (1.0 - ADAM_B2) * (g * g)
    m_hat = mn / (1.0 - ADAM_B1 ** ADAM_STEP)
    v_hat = vn / (1.0 - ADAM_B2 ** ADAM_STEP)
    return -ADAM_LR * (m_hat / (jnp.sqrt(v_hat) + ADAM_EPS) + ADAM_WD * w), mn, vn


def adamw(w, g, m, v, *, name):
    rows, cols = w.shape
    tr = _pick(rows, (256, 128, 64, 32, 16, 8))

    def body(w_ref, g_ref, m_ref, v_ref, go_ref, d_ref, mo_ref, vo_ref):
        gv = g_ref[...]
        go_ref[...] = gv
        d_ref[...], mo_ref[...], vo_ref[...] = _adamw_math(w_ref[...], gv, m_ref[...], v_ref[...])

    blk = pl.BlockSpec((tr, cols), lambda i: (i, 0))
    sds = jax.ShapeDtypeStruct((rows, cols), F32)
    return pl.pallas_call(body, grid=(rows // tr,), in_specs=[blk] * 4, out_specs=[blk] * 4, out_shape=[sds] * 4, name=name,
                          compiler_params=_params("parallel"))(w, g, m, v)


def allgather8(block, *, name, reduce_sum=False, stages=()):
    rows, cols = block.shape

    def body(x_ref, out_ref, *scratch):
        if reduce_sum:
            buf, send_sems, recv_sems, local_sem = scratch
        else:
            buf = out_ref
            send_sems, recv_sems, local_sem = scratch
        x, y, c, chips = _place()
        me, sibling = (x, y, c), (x, y, 1 - c)

        def slot(px, py, pc):
            return buf.at[4 * px + 2 * py + pc]

        def copy(k, blk, to, src=None):
            return pltpu.make_async_remote_copy(src_ref=slot(*blk) if src is None else src, dst_ref=slot(*blk),
                                                send_sem=send_sems.at[k], recv_sem=recv_sems.at[k], device_id=to, device_id_type=MESH)

        mine = pltpu.make_async_copy(x_ref, slot(*me), local_sem)
        mine.start()
        first = [copy(0, me, sibling, src=x_ref)]
        first += [copy(1 + j, me, (*chip, c), src=x_ref) for j, chip in enumerate(chips)]
        for cp in first:
            cp.start()
        passed = [copy(4 + j, (*chip, c), sibling) for j, chip in enumerate(chips)]
        for j, chip in enumerate(chips):
            copy(1 + j, (*chip, c), me).wait_recv()
            passed[j].start()
        copy(0, sibling, me).wait_recv()
        for j, chip in enumerate(chips):
            copy(4 + j, (*chip, 1 - c), me).wait_recv()
        for cp in first + passed:
            cp.wait_send()
        mine.wait()
        if reduce_sum:
            acc = buf[0]
            for d in range(1, 8):
                acc = acc + buf[d]
            out_ref[...] = acc

    sems = [pltpu.SemaphoreType.DMA((7,)), pltpu.SemaphoreType.DMA((7,)), pltpu.SemaphoreType.DMA]
    if reduce_sum:
        out_shape = jax.ShapeDtypeStruct((rows, cols), block.dtype)
        scratch = [pltpu.VMEM((8, rows, cols), block.dtype)] + sems
    else:
        out_shape = jax.ShapeDtypeStruct((8, rows, cols), block.dtype)
        scratch = sems
    vmem = pl.BlockSpec(memory_space=pltpu.VMEM)
    outs, staged = _staged_call(body, grid=(1,), in_specs=[vmem], out_specs=[vmem], out_shape=[out_shape], scratch_shapes=scratch,
                                name=name, semantics=("arbitrary",), inputs=[block], stages=stages)
    return (outs[0], staged) if stages else outs[0]


def cast_into_slot(w, chip_arr, *, name):
    _, r, cols = w.shape
    tr = _pick(r, (256, 128, 64, 32, 16))

    def body(chip_ref, w_ref, o_ref):
        del chip_ref
        o_ref[...] = w_ref[...].astype(BF16)

    grid_spec = pltpu.PrefetchScalarGridSpec(
        num_scalar_prefetch=1, grid=(2, r // tr),
        in_specs=[pl.BlockSpec((None, tr, cols), lambda h, i, chip: (h, i, 0))],
        out_specs=pl.BlockSpec((None, None, tr, cols), lambda h, i, chip: (chip[0], h, i, 0)))
    return pl.pallas_call(body, grid_spec=grid_spec, out_shape=jax.ShapeDtypeStruct((4, 2, r, cols), BF16), name=name,
                          compiler_params=_params("parallel", "parallel"))(chip_arr, w)


def proj_with_gather(u, bufs, order_arr, *, head_start, name):
    Tp, K = u.shape
    n = len(bufs)
    Ns = bufs[0].shape[3]
    tm = _pick(Tp, (1408, 512, 256, 128))
    tc = _pick(Ns, (512, 256, 128))
    n_rows = Tp // tm

    def body(order_ref, u_ref, *refs):
        del order_ref
        o_ref, outs = refs[n], refs[n + 1:2 * n + 1]
        wbuf, dir_send, dir_recv, rel_send, rel_recv, d2d_send, d2d_recv, load_sems = refs[2 * n + 1:]
        s, i = pl.program_id(0), pl.program_id(1)
        x, y, c, chips = _place()
        me = 2 * x + y
        (dx, dy) = chips[2]

        def direct(t, j, landing):
            px, py = chips[j]
            chip_idx = 2 * px + py if landing else me
            slot = _relay_rows(outs[t], chip_idx, c, 1) if (t == 0 and head_start) else outs[t].at[chip_idx, c]
            return _remote(slot, slot, dir_send.at[t, j], dir_recv.at[t, j], (px, py, c))

        def relay(t, j, landing):
            px, py = chips[j]
            ox, oy = chips[1 - j]
            rows = _relay_rows(outs[t], 2 * dx + dy, c, j) if landing else _relay_rows(outs[t], 2 * ox + oy, c, j)
            return _remote(rows, rows, rel_send.at[t, j], rel_recv.at[t, j], (px, py, c))

        def d2d(t, j, landing):
            px, py = chips[j]
            slot = outs[t].at[2 * px + py, 1 - c] if landing else outs[t].at[2 * px + py, c]
            return _remote(slot, slot, d2d_send.at[t, j], d2d_recv.at[t, j], (x, y, 1 - c))

        def load(chip_idx):
            parts = [pltpu.make_async_copy(outs[0].at[chip_idx, hh], wbuf.at[pl.ds(hh * (K // 2), K // 2), :], load_sems.at[hh])
                     for hh in range(2)]
            for cp in parts:
                cp.start()
            for cp in parts:
                cp.wait()

        def chip_of(j):
            px, py = chips[j]
            return 2 * px + py

        @pl.when(jnp.logical_and(s == 0, i == 0))
        def _():
            for t in range(n):
                for j in range(2):
                    direct(t, j, False).start()
            if head_start:
                relay(0, 0, False).start()
            load(me)

        @pl.when(jnp.logical_and(s == 1, i == 0))
        def _():
            for j in range(2):
                direct(0, j, True).wait_recv()
            for j in range(2):
                if not (head_start and j == 0):
                    relay(0, j, False).start()
                d2d(0, j, False).start()
            d2d(0, 0, True).wait_recv()
            load(chip_of(0))

        @pl.when(jnp.logical_and(s == 2, i == 0))
        def _():
            d2d(0, 1, True).wait_recv()
            for t in range(1, n):
                for j in range(2):
                    direct(t, j, True).wait_recv()
                for j in range(2):
                    relay(t, j, False).start()
                    d2d(t, j, False).start()
            load(chip_of(1))

        @pl.when(jnp.logical_and(s == 3, i == 0))
        def _():
            for j in range(2):
                relay(0, j, True).wait_recv()
            d2d(0, 2, False).start()
            d2d(0, 2, True).wait_recv()
            load(chip_of(2))

        @pl.when(jnp.logical_and(s == 3, i == min(1, n_rows - 1)))
        def _():
            for t in range(1, n):
                for j in range(2):
                    relay(t, j, True).wait_recv()
                d2d(t, 2, False).start()

        uv = u_ref[...]
        for cc in range(Ns // tc):
            o_ref[:, cc * tc:(cc + 1) * tc] = jnp.dot(uv, wbuf[:, cc * tc:(cc + 1) * tc], preferred_element_type=F32).astype(BF16)

        @pl.when(jnp.logical_and(s == 3, i == n_rows - 1))
        def _():
            for t in range(1, n):
                for j in range(3):
                    d2d(t, j, True).wait_recv()
            for t in range(n):
                for j in range(2):
                    direct(t, j, False).wait_send()
                    relay(t, j, False).wait_send()
                for j in range(3):
                    d2d(t, j, False).wait_send()

    grid_spec = pltpu.PrefetchScalarGridSpec(
        num_scalar_prefetch=1, grid=(4, n_rows),
        in_specs=[pl.BlockSpec((tm, K), lambda s, i, order: (i, 0)), *_any_specs(n)],
        out_specs=[pl.BlockSpec((tm, Ns), lambda s, i, order: (i, order[s])), *_any_specs(n)],
        scratch_shapes=[pltpu.VMEM((K, Ns), BF16), pltpu.SemaphoreType.DMA((n, 2)), pltpu.SemaphoreType.DMA((n, 2)),
                        pltpu.SemaphoreType.DMA((n, 2)), pltpu.SemaphoreType.DMA((n, 2)),
                        pltpu.SemaphoreType.DMA((n, 3)), pltpu.SemaphoreType.DMA((n, 3)), pltpu.SemaphoreType.DMA((2,))])
    res = pl.pallas_call(body, grid_spec=grid_spec, out_shape=[jax.ShapeDtypeStruct((Tp, 4 * Ns), BF16), *[_sds(b) for b in bufs]],
                         input_output_aliases={2 + t: 1 + t for t in range(n)}, name=name,
                         compiler_params=_params("arbitrary", "arbitrary"))(order_arr, u, *bufs)
    return res[0], list(res[1:])


def chip_presum(grad, recv, c_arr, *, name):
    _, _, r, cols = grad.shape
    tr = _pick(r, (256, 128, 64, 32, 16))

    def body(c_ref, g_ref, r_ref, o_ref):
        del c_ref
        o_ref[...] = (g_ref[...].astype(F32) + r_ref[...].astype(F32)).astype(BF16)

    grid_spec = pltpu.PrefetchScalarGridSpec(
        num_scalar_prefetch=1, grid=(4, r // tr),
        in_specs=[pl.BlockSpec((None, None, tr, cols), lambda k, i, c_ref: (k, c_ref[0], i, 0)),
                  pl.BlockSpec((None, tr, cols), lambda k, i, c_ref: (k, i, 0))],
        out_specs=pl.BlockSpec((None, tr, cols), lambda k, i, c_ref: (k, i, 0)))
    return pl.pallas_call(body, grid_spec=grid_spec, out_shape=jax.ShapeDtypeStruct((4, r, cols), BF16), name=name,
                          compiler_params=_params("parallel", "parallel"))(c_arr, grad, recv)


def final_half(grad, recv, got, mc_arr, *, name):
    _, _, r, cols = grad.shape
    tr = _pick(r, (256, 128, 64, 32, 16))

    def body(mc_ref, g_ref, r_ref, q_ref, o_ref):
        del mc_ref
        acc = g_ref[...].astype(F32) + r_ref[...].astype(F32)
        for j in range(3):
            acc = acc + q_ref[j].astype(F32)
        o_ref[...] = acc

    grid_spec = pltpu.PrefetchScalarGridSpec(
        num_scalar_prefetch=1, grid=(r // tr,),
        in_specs=[pl.BlockSpec((None, None, tr, cols), lambda i, mc: (mc[0], mc[1], i, 0)),
                  pl.BlockSpec((None, tr, cols), lambda i, mc: (mc[0], i, 0)),
                  pl.BlockSpec((3, tr, cols), lambda i, mc: (0, i, 0))],
        out_specs=pl.BlockSpec((tr, cols), lambda i, mc: (i, 0)))
    return pl.pallas_call(body, grid_spec=grid_spec, out_shape=jax.ShapeDtypeStruct((r, cols), F32), name=name,
                          compiler_params=_params("parallel"))(mc_arr, grad, recv, got)


def adamw_halves(w, mine, theirs, m, v, c_arr, *, name):
    _, r, cols = w.shape
    tr = _pick(r, (256, 128, 64, 32, 16, 8))

    def body(c_ref, w_ref, mine_ref, theirs_ref, m_ref, v_ref, go_ref, d_ref, mo_ref, vo_ref):
        gv = jnp.where(pl.program_id(0) == c_ref[0], mine_ref[...], theirs_ref[...])
        go_ref[...] = gv
        d_ref[...], mo_ref[...], vo_ref[...] = _adamw_math(w_ref[...], gv, m_ref[...], v_ref[...])

    blk = pl.BlockSpec((None, tr, cols), lambda h, i, c_ref: (h, i, 0))
    grid_spec = pltpu.PrefetchScalarGridSpec(
        num_scalar_prefetch=1, grid=(2, r // tr),
        in_specs=[blk, pl.BlockSpec((tr, cols), lambda h, i, c_ref: (jnp.where(h == c_ref[0], i, 0), 0)),
                  pl.BlockSpec((tr, cols), lambda h, i, c_ref: (jnp.where(h == c_ref[0], 0, i), 0)), blk, blk],
        out_specs=[blk] * 4)
    sds = jax.ShapeDtypeStruct((2, r, cols), F32)
    return pl.pallas_call(body, grid_spec=grid_spec, out_shape=[sds] * 4, name=name,
                          compiler_params=_params("parallel", "parallel"))(c_arr, w, mine, theirs, m, v)


def _pad_rows(a, rows):
    return jnp.pad(a, ((0, rows - a.shape[0]), (0, 0)))


def kernel(x, meta_tokens, norm1_g, w_in, pool_w, pool_scale, conv_w, conv_b, gate_a_w, gate_a_b, gate_x_w, gate_x_b, lru_lambda, w_out, norm2_g, mlp_w1, mlp_w2, final_g, loss_target, m_meta_tokens, m_norm1_g, m_w_in, m_pool_w, m_pool_scale, m_conv_w, m_conv_b, m_gate_a_w, m_gate_a_b, m_gate_x_w, m_gate_x_b, m_lru_lambda, m_w_out, m_norm2_g, m_mlp_w1, m_mlp_w2, m_final_g, v_meta_tokens, v_norm1_g, v_w_in, v_pool_w, v_pool_scale, v_conv_w, v_conv_b, v_gate_a_w, v_gate_a_b, v_gate_x_w, v_gate_x_b, v_lru_lambda, v_w_out, v_norm2_g, v_mlp_w1, v_mlp_w2, v_final_g):
    D = x.shape[-1]
    H = D // HEAD_DIM
    G = len(POOL_WINDOWS)
    PG = D // G
    ax, ay, ac = lax.axis_index("x"), lax.axis_index("y"), lax.axis_index("c")
    chip = 2 * ax + ay
    dshard = D // 4

    c_arr = jnp.reshape(ac, (1,)).astype(jnp.int32)
    chip_arr = jnp.reshape(chip, (1,)).astype(jnp.int32)
    mc_arr = jnp.stack([chip, ac]).astype(jnp.int32)
    names = ["w_in", "pool_w", "gate_a_w", "gate_x_w", "w_out", "mlp_w1", "mlp_w2"]
    big = [w_in, pool_w, gate_a_w, gate_x_w, w_out, mlp_w1, mlp_w2]

    def halves(w):
        w2d = w.reshape(-1, w.shape[-1])
        return w2d.reshape(2, w2d.shape[0] // 2, w2d.shape[1])

    bufs = [cast_into_slot(halves(w), chip_arr, name="cast_" + nm) for w, nm in zip(big, names)]
    order_arr = jnp.stack([chip, 2 * (1 - ax) + ay, 2 * ax + (1 - ay), 2 * (1 - ax) + (1 - ay)]).astype(jnp.int32)

    small_in = jnp.concatenate([meta_tokens, _pad_rows(conv_w[0], 8), _pad_rows(gate_a_b.reshape(1, dshard), 8),
                                _pad_rows(gate_x_b.reshape(1, dshard), 8)], axis=0)
    sm = allgather8(small_in, name="gather_small")[0::2]
    meta_f = sm[:, 0:16].transpose(1, 0, 2).reshape(N_META, D)
    conv_w_f = sm[:, 16:20].transpose(1, 0, 2).reshape(4, D)
    hd4 = HEAD_DIM // 4
    ba_f = sm[:, 24].reshape(4, H, hd4).transpose(1, 0, 2).reshape(1, D)
    bx_f = sm[:, 32].reshape(4, H, hd4).transpose(1, 0, 2).reshape(1, D)
    vecs = jnp.zeros((16, D), F32)
    for r0, part in ((0, pool_scale), (1, conv_b), (2, ba_f), (3, bx_f), (4, lru_lambda), (5, conv_w_f)):
        vecs = lax.dynamic_update_slice(vecs, part, (r0, 0))

    def chipwise(g, n_blocks, rows):
        return g.reshape(n_blocks, 4, rows, g.shape[-1]).transpose(1, 0, 2, 3).reshape(4, n_blocks * rows, g.shape[-1])

    def split2(g):
        return g.reshape(4, 2, g.shape[1] // 2, g.shape[2])

    def presum(t, g, r):
        return chip_presum(g, r, c_arr, name="presum_" + names[t])

    def total(t, g, r, q):
        return final_half(g, r, q, mc_arr, name="sum_" + names[t])

    xs, target, gfin = x[0], loss_target[0], final_g.reshape(1, D)
    meta_tile = jnp.concatenate([jnp.zeros((ROW_TILE - N_META, D), F32), meta_f], axis=0)
    (h0, u, u_t), [[b_win]] = rmsnorm_fwd_input(xs, meta_tile, norm1_g, name="norm1",
                                                stages=[stage_gather_direct([bufs[0]], quarter=0)])
    proj, (b_win, b_pool, b_ga, b_gx, b_wout) = proj_with_gather(u, [b_win, *bufs[1:5]], order_arr, head_start=True, name="proj")
    w_in_f = b_win.reshape(4, w_in.shape[1], w_in.shape[2])
    pool_f = b_pool.reshape(4, G, PG // 4, PG).transpose(1, 0, 2, 3).reshape(G, PG, PG)
    ga_f = b_ga.reshape(4, H, HEAD_DIM // 4, HEAD_DIM).transpose(1, 0, 2, 3).reshape(H, HEAD_DIM, HEAD_DIM)
    gx_f = b_gx.reshape(4, H, HEAD_DIM // 4, HEAD_DIM).transpose(1, 0, 2, 3).reshape(H, HEAD_DIM, HEAD_DIM)
    w_out_f = b_wout.reshape(4 * w_out.shape[1], w_out.shape[2])
    (merged, hs, saved), [[b_w1, b_w2]] = mix_fwd(proj, pool_f, ga_f, gx_f, vecs, name="mix_fwd",
                                                  stages=[stage_gather_direct([bufs[5], bufs[6]])])
    h1, [[b_w1]] = mm_nn(merged, w_out_f, out_dtype=F32, name="out_proj", epilogue=lambda r, res: r + res, extras=(h0,),
                         stages=[stage_both(stage_gather_relay([b_w1]), stage_gather_d2d([b_w1], peers=(0, 1)))])
    (u2, u2_t), [[b_w1]] = rmsnorm_fwd(h1, norm2_g, name="norm2", stages=[stage_gather_d2d([b_w1], peers=(2,))])
    w1_f = b_w1.reshape(4, mlp_w1.shape[1], mlp_w1.shape[2])
    a1, [[b_w2]] = mm_nn(u2, w1_f, out_dtype=BF16, name="mlp_up", tiles=(None, min(1024, D), None),
                         stages=[stage_both(stage_gather_relay([b_w2]), stage_gather_d2d([b_w2], peers=(0, 1)))])
    [[b_w2]] = comm_call([stage_gather_d2d([b_w2], peers=(2,))], name="w2_to_sibling")
    w2_f = b_w2.reshape(4 * mlp_w2.shape[1], mlp_w2.shape[2])
    Tp = h0.shape[0]
    h2 = mm_nn(a1, w2_f, out_dtype=F32, name="mlp_down", a_pro=_relu_sq, epilogue=lambda r, res: r + res, extras=(h1,),
               tiles=(_pick(Tp, (704, 512, 256, 128)), min(256, D), a1.shape[1]))
    dh2_bf, st_f = final_loss(h2, target, gfin, name="final_loss")

    da1 = mm_nt(dh2_bf, w2_f, out_dtype=BF16, name="mlp_down_dx",
                epilogue=lambda r, a: r * (2.0 * jnp.maximum(a.astype(F32), 0.0)), extras=(a1,))
    d_w2 = mm_tn(a1, dh2_bf, shards=1, out_dtype=BF16, name="mlp_down_dw", a_pro=_relu_sq, tiles=(512, min(1024, D), Tp))
    g6 = split2(d_w2.reshape(4, -1, d_w2.shape[-1]))
    d_w1, [[r6]] = mm_nn(u2_t, da1, out_dtype=BF16, name="mlp_up_dw", out_shards=4, tiles=(min(1024, D), min(1024, D), Tp),
                         stages=[stage_to_sibling([g6])])
    g5 = split2(d_w1)
    p6 = presum(6, g6, r6)
    du2, [[q6_near], [r5]] = mm_nt(da1, w1_f, out_dtype=BF16, name="mlp_up_dx",
                                  stages=[stage_to_chips([p6], peers=(0, 1)), stage_to_sibling([g5])])
    p5 = presum(5, g5, r5)
    dh1_bf, st_2 = rms_bwd(h1, norm2_g, du2, dh2_bf, name="norm2_bwd")
    dmerged = mm_nt(dh1_bf, w_out_f, out_dtype=BF16, name="out_proj_dx")
    d_wout = mm_tn(merged, dh1_bf, shards=1, out_dtype=BF16, name="out_proj_dw", tiles=(512, min(1024, D), Tp))
    g4 = split2(d_wout.reshape(4, -1, d_wout.shape[-1]))
    (dproj, d_pool, d_ga, d_gx, d_vecs), [[q6], [q5], [r4]] = mix_bwd(
        proj, hs, saved, dmerged, pool_f, ga_f, gx_f, vecs, name="mix_bwd",
        stages=[stage_to_chips([p6], peers=(2,), into=[q6_near]), stage_to_chips([p5]), stage_to_sibling([g4])])
    p4 = presum(4, g4, r4)
    f5, f6 = total(5, g5, r5, q5), total(6, g6, r6, q6)
    g1, g2, g3 = split2(chipwise(d_pool, G, PG // 4)), split2(chipwise(d_ga, H, hd4)), split2(chipwise(d_gx, H, hd4))
    half_k = u_t.shape[0] // 2
    whole_k = (half_k, _pick(dproj.shape[1] // 4, (1280, 512, 256, 128)), Tp)
    u_theirs = lax.dynamic_slice_in_dim(u_t, (1 - ac) * half_k, half_k, axis=0)
    u_mine = lax.dynamic_slice_in_dim(u_t, ac * half_k, half_k, axis=0)
    g_theirs, [[r1, r2, r3], [o5, o6], [q4]] = mm_nn(u_theirs, dproj, out_dtype=BF16, out_shards=4, tiles=whole_k, name="proj_dw_sibling_rows",
                                                    stages=[stage_to_sibling([g1, g2, g3]), stage_from_sibling([f5, f6]), stage_to_chips([p4])])
    p1, p2, p3 = presum(1, g1, r1), presum(2, g2, r2), presum(3, g3, r3)
    g_mine, [[r0], [q1, q2, q3]] = mm_nn(u_mine, dproj, out_dtype=BF16, out_shards=4, tiles=whole_k, name="proj_dw_own_rows",
                                         stages=[stage_from_sibling([g_theirs]), stage_to_chips([p1, p2, p3])])
    g0 = g_mine[:, None]
    p0 = chip_presum(g0, r0, jnp.zeros((1,), jnp.int32), name="presum_w_in")
    f4 = total(4, g4, r4, q4)
    du, [[q0], [o4]] = mm_nt(dproj, w_in_f, out_dtype=BF16, name="proj_dx", tiles=(None, None, w_in_f.shape[2]),
                             stages=[stage_to_chips([p0]), stage_from_sibling([f4])])
    f0 = final_half(g0, r0, q0, jnp.stack([chip, 0]).astype(jnp.int32), name="sum_w_in")
    f1, f2, f3 = total(1, g1, r1, q1), total(2, g2, r2, q2), total(3, g3, r3, q3)
    grad_x, d_meta, st_1 = rms_bwd_input(h0, norm1_g, du, dh1_bf, name="norm1_bwd")

    small = jnp.concatenate([d_meta, d_vecs, st_1, st_2, st_f], axis=0)
    tot, [[o0, o1, o2, o3]] = allgather8(small, name="sum_small", reduce_sum=True, stages=[stage_from_sibling([f0, f1, f2, f3])])
    mine = [f0, f1, f2, f3, f4, f5, f6]
    theirs = [o0, o1, o2, o3, o4, o5, o6]
    loss = jnp.sum(tot[49])
    g_meta = lax.dynamic_slice_in_dim(tot[0:16], chip * dshard, dshard, axis=1)
    g_pool_scale, g_conv_b, g_lam = tot[16:17], tot[17:18], tot[20:21]
    g_ba = lax.dynamic_slice_in_dim(tot[18].reshape(H, HEAD_DIM), chip * hd4, hd4, axis=1)[None]
    g_bx = lax.dynamic_slice_in_dim(tot[19].reshape(H, HEAD_DIM), chip * hd4, hd4, axis=1)[None]
    g_conv_w = lax.dynamic_slice_in_dim(tot[21:25], chip * dshard, dshard, axis=1)[None]
    g_n1, g_n2, g_fin = tot[32:33], tot[40:41], tot[48]

    def step(w, g, m, v, nm):
        cols = w.shape[-1]
        outs = adamw(w.reshape(-1, cols), g.reshape(-1, cols), m.reshape(-1, cols), v.reshape(-1, cols), name="adamw_" + nm)
        return [o.reshape(w.shape) for o in outs]

    def step_big(t, w, m, v):
        outs = adamw_halves(halves(w), mine[t], theirs[t], halves(m), halves(v), c_arr, name="adamw_" + names[t])
        return [o.reshape(w.shape) for o in outs]

    res = dict(meta_tokens=step(meta_tokens, g_meta, m_meta_tokens, v_meta_tokens, "meta_tokens"),
               norm1_g=step(norm1_g, g_n1, m_norm1_g, v_norm1_g, "norm1_g"),
               w_in=step_big(0, w_in, m_w_in, v_w_in), pool_w=step_big(1, pool_w, m_pool_w, v_pool_w),
               pool_scale=step(pool_scale, g_pool_scale, m_pool_scale, v_pool_scale, "pool_scale"),
               conv_w=step(conv_w, g_conv_w, m_conv_w, v_conv_w, "conv_w"), conv_b=step(conv_b, g_conv_b, m_conv_b, v_conv_b, "conv_b"),
               gate_a_w=step_big(2, gate_a_w, m_gate_a_w, v_gate_a_w), gate_a_b=step(gate_a_b, g_ba, m_gate_a_b, v_gate_a_b, "gate_a_b"),
               gate_x_w=step_big(3, gate_x_w, m_gate_x_w, v_gate_x_w), gate_x_b=step(gate_x_b, g_bx, m_gate_x_b, v_gate_x_b, "gate_x_b"),
               lru_lambda=step(lru_lambda, g_lam, m_lru_lambda, v_lru_lambda, "lru_lambda"), w_out=step_big(4, w_out, m_w_out, v_w_out),
               norm2_g=step(norm2_g, g_n2, m_norm2_g, v_norm2_g, "norm2_g"), mlp_w1=step_big(5, mlp_w1, m_mlp_w1, v_mlp_w1),
               mlp_w2=step_big(6, mlp_w2, m_mlp_w2, v_mlp_w2), final_g=step(final_g, g_fin, m_final_g, v_final_g, "final_g"))
    order = list(res)
    return (loss, grad_x[None], *[res[n][0] for n in order], *[res[n][1] for n in order], *[res[n][2] for n in order],
            *[res[n][3] for n in order])
```

```python
import functools

import jax
import jax.numpy as jnp
from jax import lax
from jax.experimental import pallas as pl
from jax.experimental.pallas import tpu as pltpu

F32 = jnp.float32
BF16 = jnp.bfloat16
MESH = pl.DeviceIdType.MESH

NORM_EPS = 1e-6
N_META = 16
HEAD_DIM = 256
POOL_WINDOWS = (2, 4, 8, 16)
LRU_C = 8.0
ROW_TILE = 128
HIST = 16
VMEM_LIMIT_BYTES = 56 * 1024 * 1024
ADAM_LR, ADAM_B1, ADAM_B2, ADAM_EPS, ADAM_WD, ADAM_STEP = 0.001, 0.9, 0.999, 1e-08, 0.01, 10


def _pick(n, prefs):
    for p in prefs:
        if n % p == 0:
            return p
    return n


def _params(*sem):
    return pltpu.CompilerParams(dimension_semantics=sem, vmem_limit_bytes=VMEM_LIMIT_BYTES)


def _sigmoid(x):
    return 1.0 / (1.0 + jnp.exp(-x))


def _gelu_tanh(x):
    t = jnp.tanh(0.7978845608028654 * (x + 0.044715 * (x * x * x)))
    return 0.5 * x * (1.0 + t), t


def _gelu_tanh_grad(x, t):
    return 0.5 * (1.0 + t) + 0.5 * x * (1.0 - t * t) * (0.7978845608028654 * (1.0 + 3.0 * 0.044715 * x * x))


def _neg_expm1(x, exp_x):
    series = x * (-1.0 + x * (-0.5 + x * ((-1.0 / 6.0) + x * ((-1.0 / 24.0) + x * (-1.0 / 120.0)))))
    return jnp.where(x > -0.125, series, 1.0 - exp_x)


def _softplus_neg(lam):
    z = jnp.exp(-jnp.abs(lam))
    log1p_z = jnp.where(z < 0.01, z * (1.0 - z * (0.5 - z * (1.0 / 3.0))), jnp.log(1.0 + z))
    return jnp.maximum(-lam, 0.0) + log1p_z


def _tile_masks(is_meta, rows):
    row = lax.broadcasted_iota(jnp.int32, (rows, 1), 0)
    valid = jnp.logical_or(jnp.logical_not(is_meta), row >= rows - N_META)
    t_log = jnp.where(is_meta, row - (rows - N_META), 1 << 20)
    return row, valid, t_log


def _window_count_inv(t_log, w):
    return 1.0 / jnp.clip(t_log + 1, 1, w).astype(F32)


def _dot_nt(a, b):
    return lax.dot_general(a, b, (((1,), (1,)), ((), ())), preferred_element_type=F32)


def _dot_tn(a, b):
    return lax.dot_general(a, b, (((0,), (0,)), ((), ())), preferred_element_type=F32)


def _place():
    x, y, c = lax.axis_index("x"), lax.axis_index("y"), lax.axis_index("c")
    chips = [(1 - x, y), (x, 1 - y), (1 - x, 1 - y)]
    return x, y, c, chips


def _remote(src, dst, send_sem, recv_sem, to):
    return pltpu.make_async_remote_copy(src_ref=src, dst_ref=dst, send_sem=send_sem, recv_sem=recv_sem, device_id=to,
                                        device_id_type=MESH)


class Stage:
    def __init__(self, arrays, out_shapes, aliases, n_copies, copies):
        self.arrays, self.out_shapes, self.aliases, self.n_copies, self.copies = list(arrays), list(out_shapes), aliases, n_copies, copies


def _sds(a):
    return jax.ShapeDtypeStruct(a.shape, a.dtype)


def _relay_rows(buf_ref, chip_idx, c, quarter):
    rows = buf_ref.shape[2] // 2
    return buf_ref.at[chip_idx, c, pl.ds(quarter * rows, rows)]


def stage_gather_direct(bufs, quarter=None):
    n = len(bufs)

    def copies(ins, outs, send, recv):
        x, y, c, chips = _place()
        me = 2 * x + y
        sends, recvs = [], []
        for t in range(n):
            for j, (px, py) in enumerate(chips[:2]):
                k = 2 * t + j
                if quarter is None:
                    mine, theirs = outs[t].at[me, c], outs[t].at[2 * px + py, c]
                else:
                    mine, theirs = _relay_rows(outs[t], me, c, quarter), _relay_rows(outs[t], 2 * px + py, c, quarter)
                sends.append(_remote(mine, mine, send.at[k], recv.at[k], (px, py, c)))
                recvs.append(_remote(theirs, theirs, send.at[k], recv.at[k], (px, py, c)))
        return sends, recvs

    return Stage(bufs, [_sds(b) for b in bufs], {t: t for t in range(n)}, 2 * n, copies)


def stage_gather_relay(bufs):
    n = len(bufs)

    def copies(ins, outs, send, recv):
        x, y, c, chips = _place()
        (xx, xy), (yx, yy), (dx, dy) = chips
        sends, recvs = [], []
        for t in range(n):
            from_y, from_x = _relay_rows(outs[t], 2 * yx + yy, c, 0), _relay_rows(outs[t], 2 * xx + xy, c, 1)
            sends.append(_remote(from_y, from_y, send.at[2 * t], recv.at[2 * t], (xx, xy, c)))
            sends.append(_remote(from_x, from_x, send.at[2 * t + 1], recv.at[2 * t + 1], (yx, yy, c)))
            for q, (px, py) in enumerate(chips[:2]):
                got = _relay_rows(outs[t], 2 * dx + dy, c, q)
                recvs.append(_remote(got, got, send.at[2 * t + q], recv.at[2 * t + q], (px, py, c)))
        return sends, recvs

    return Stage(bufs, [_sds(b) for b in bufs], {t: t for t in range(n)}, 2 * n, copies)


def stage_gather_d2d(bufs, peers=(0, 1, 2)):
    n = len(bufs)

    def copies(ins, outs, send, recv):
        x, y, c, chips = _place()
        sends, recvs = [], []
        for t in range(n):
            for slot, j in enumerate(peers):
                px, py = chips[j]
                k = len(peers) * t + slot
                got, sib = outs[t].at[2 * px + py, c], outs[t].at[2 * px + py, 1 - c]
                sends.append(_remote(got, got, send.at[k], recv.at[k], (x, y, 1 - c)))
                recvs.append(_remote(sib, sib, send.at[k], recv.at[k], (x, y, 1 - c)))
        return sends, recvs

    return Stage(bufs, [_sds(b) for b in bufs], {t: t for t in range(n)}, len(peers) * n, copies)


class _SemsFrom:
    def __init__(self, sems, first):
        self.sems, self.first = sems, first

    @property
    def at(self):
        return self

    def __getitem__(self, k):
        return self.sems.at[self.first + k]


def stage_both(first, second):
    def copies(ins, outs, send, recv):
        s1, r1 = first.copies(ins, outs, send, recv)
        s2, r2 = second.copies(ins, outs, _SemsFrom(send, first.n_copies), _SemsFrom(recv, first.n_copies))
        return s1 + s2, r1 + r2

    return Stage(first.arrays, first.out_shapes, first.aliases, first.n_copies + second.n_copies, copies)


def stage_to_sibling(grads):
    n = len(grads)

    def copies(ins, outs, send, recv):
        x, y, c, _ = _place()
        sends, recvs = [], []
        for t in range(n):
            for k4 in range(4):
                k = 4 * t + k4
                sends.append(_remote(ins[t].at[k4, 1 - c], outs[t].at[k4], send.at[k], recv.at[k], (x, y, 1 - c)))
                recvs.append(_remote(outs[t].at[k4], outs[t].at[k4], send.at[k], recv.at[k], (x, y, 1 - c)))
        return sends, recvs

    return Stage(grads, [jax.ShapeDtypeStruct((4, *g.shape[2:]), g.dtype) for g in grads], {}, 4 * n, copies)


def stage_to_chips(presums, peers=(0, 1, 2), into=None):
    n = len(presums)

    def copies(ins, outs, send, recv):
        x, y, c, chips = _place()
        sends, recvs = [], []
        for t in range(n):
            for slot, j in enumerate(peers):
                px, py = chips[j]
                k = len(peers) * t + slot
                sends.append(_remote(ins[t].at[2 * px + py], outs[t].at[j], send.at[k], recv.at[k], (px, py, c)))
                recvs.append(_remote(outs[t].at[j], outs[t].at[j], send.at[k], recv.at[k], (px, py, c)))
        return sends, recvs

    out_shapes = [jax.ShapeDtypeStruct((3, *p.shape[1:]), p.dtype) for p in presums]
    if into is None:
        return Stage(presums, out_shapes, {}, len(peers) * n, copies)
    return Stage([*presums, *into], out_shapes, {n + t: t for t in range(n)}, len(peers) * n, copies)


def stage_from_sibling(halves):
    n = len(halves)

    def copies(ins, outs, send, recv):
        x, y, c, _ = _place()
        sends = [_remote(ins[t], outs[t], send.at[t], recv.at[t], (x, y, 1 - c)) for t in range(n)]
        recvs = [_remote(outs[t], outs[t], send.at[t], recv.at[t], (x, y, 1 - c)) for t in range(n)]
        return sends, recvs

    return Stage(halves, [_sds(h) for h in halves], {}, n, copies)


def _any_specs(n):
    return [pl.BlockSpec(memory_space=pl.ANY)] * n


def _staged_call(body, *, grid, in_specs, out_specs, out_shape, scratch_shapes, name, semantics, inputs, stages=()):
    n_in, n_out, n_scr = len(in_specs), len(out_specs), len(scratch_shapes)
    st_arrays = [a for s in stages for a in s.arrays]
    st_outs = [o for s in stages for o in s.out_shapes]
    st_sems = [pltpu.SemaphoreType.DMA((s.n_copies,)) for s in stages for _ in range(2)]
    aliases = {}
    at_in, at_out = n_in, n_out
    for s in stages:
        for a, o in s.aliases.items():
            aliases[at_in + a] = at_out + o
        at_in += len(s.arrays)
        at_out += len(s.out_shapes)

    def full_body(*refs):
        pos = 0

        def take(count):
            nonlocal pos
            part = refs[pos:pos + count]
            pos += count
            return part

        ins, s_ins, outs, s_outs, scr = take(n_in), take(len(st_arrays)), take(n_out), take(len(st_outs)), take(n_scr)
        s_sems = refs[pos:]

        def each_stage(action):
            at_i = at_o = 0
            for idx, s in enumerate(stages):
                sends, recvs = s.copies(s_ins[at_i:at_i + len(s.arrays)], s_outs[at_o:at_o + len(s.out_shapes)],
                                        s_sems[2 * idx], s_sems[2 * idx + 1])
                action(sends, recvs)
                at_i += len(s.arrays)
                at_o += len(s.out_shapes)

        if stages:
            ids = [pl.program_id(a) for a in range(len(grid))]
            first = functools.reduce(jnp.logical_and, [i == 0 for i in ids])
            last = functools.reduce(jnp.logical_and, [i == g - 1 for i, g in zip(ids, grid)])

            def start(sends, recvs):
                for cp in sends:
                    cp.start()

            def finish(sends, recvs):
                for cp in recvs:
                    cp.wait_recv()
                for cp in sends:
                    cp.wait_send()

            @pl.when(first)
            def _():
                each_stage(start)

        body(*ins, *outs, *scr)

        if stages:
            @pl.when(last)
            def _():
                each_stage(finish)

    sem = tuple("arbitrary" for _ in grid) if stages else tuple(semantics)
    res = pl.pallas_call(
        full_body, grid=grid, in_specs=[*in_specs, *_any_specs(len(st_arrays))], out_specs=[*out_specs, *_any_specs(len(st_outs))],
        out_shape=[*out_shape, *st_outs], scratch_shapes=[*scratch_shapes, *st_sems], input_output_aliases=aliases, name=name,
        compiler_params=_params(*sem))(*inputs, *st_arrays)
    outs, rest = list(res[:n_out]), list(res[n_out:])
    per_stage = []
    for s in stages:
        per_stage.append(rest[:len(s.out_shapes)])
        rest = rest[len(s.out_shapes):]
    return outs, per_stage


def comm_call(stages, *, name):
    return _staged_call(lambda: None, grid=(1,), in_specs=[], out_specs=[], out_shape=[], scratch_shapes=[], name=name,
                        semantics=("arbitrary",), inputs=[], stages=stages)[1]


def _matmul(kind, a, b, *, grid, a_spec, b_spec, out_spec, out_shape, acc_shape, name,
            a_pro=None, epilogue=None, extras=(), extra_specs=(), stages=()):
    nk = grid[2]
    n_extra = len(extras)

    def body(a_ref, b_ref, *rest):
        extra_refs = rest[:n_extra]
        o_ref = rest[n_extra]
        av = a_ref[...]
        if a_pro is not None:
            av = a_pro(av)
        av = av.astype(BF16)
        bv = b_ref[...].astype(BF16)
        if kind == "nn":
            p = jnp.dot(av, bv, preferred_element_type=F32)
        elif kind == "nt":
            p = _dot_nt(av, bv)
        else:
            p = _dot_tn(av, bv)

        def finish(r):
            if epilogue is not None:
                r = epilogue(r, *[e[...] for e in extra_refs])
            o_ref[...] = r.astype(o_ref.dtype)

        if nk == 1:
            finish(p)
        else:
            acc_ref = rest[n_extra + 1]
            k = pl.program_id(2)

            @pl.when(k == 0)
            def _():
                acc_ref[...] = p

            @pl.when(k > 0)
            def _():
                acc_ref[...] += p

            @pl.when(k == nk - 1)
            def _():
                finish(acc_ref[...])

    scratch = [] if nk == 1 else [pltpu.VMEM(acc_shape, F32)]
    outs, staged = _staged_call(body, grid=grid, in_specs=[a_spec, b_spec, *extra_specs], out_specs=[out_spec], out_shape=[out_shape],
                                scratch_shapes=scratch, name=name, semantics=("parallel", "parallel", "arbitrary"),
                                inputs=[a, b, *extras], stages=stages)
    return (outs[0], staged) if stages else outs[0]


def mm_nn(a, b, *, out_dtype, name, a_pro=None, epilogue=None, extras=(), stages=(), out_shards=1, tiles=(None, None, None)):
    M, K = a.shape
    sharded = b.ndim == 3
    ns = b.shape[2] if sharded else b.shape[1]
    N = ns * b.shape[0] if sharded else ns
    if out_shards > 1:
        ns = N // out_shards
    tm = tiles[0] or _pick(M, (1408, 1024, 512, 256, 128))
    tn = tiles[1] or _pick(ns, (512, 256, 128))
    tk = tiles[2] or _pick(K, (2048, 1408, 1024, 512, 256, 128))
    per = ns // tn
    if sharded:
        b_spec = pl.BlockSpec((None, tk, tn), lambda i, j, k: (j // per, k, j % per))
    else:
        b_spec = pl.BlockSpec((tk, tn), lambda i, j, k: (k, j))
    mn = pl.BlockSpec((tm, tn), lambda i, j, k: (i, j))
    if out_shards > 1:
        out_spec = pl.BlockSpec((None, tm, tn), lambda i, j, k: (j // per, i, j % per))
        out_shape = jax.ShapeDtypeStruct((out_shards, M, ns), out_dtype)
    else:
        out_spec, out_shape = mn, jax.ShapeDtypeStruct((M, N), out_dtype)
    return _matmul("nn", a, b, grid=(M // tm, N // tn, K // tk), a_spec=pl.BlockSpec((tm, tk), lambda i, j, k: (i, k)),
                   b_spec=b_spec, out_spec=out_spec, out_shape=out_shape, acc_shape=(tm, tn),
                   name=name, a_pro=a_pro, epilogue=epilogue, extras=extras, extra_specs=[mn] * len(extras), stages=stages)


def mm_nt(a, w, *, out_dtype, name, epilogue=None, extras=(), stages=(), tiles=(None, None, None)):
    M, N = a.shape
    sharded = w.ndim == 3
    kw = w.shape[1] if sharded else w.shape[0]
    ns = w.shape[2] if sharded else w.shape[1]
    tm = tiles[0] or _pick(M, (1408, 512, 256, 128))
    tkw = tiles[1] or _pick(kw, (1024, 512, 256, 128))
    tk = tiles[2] or _pick(ns, (2048, 1280, 1024, 512, 256, 128))
    per = ns // tk
    if sharded:
        w_spec = pl.BlockSpec((None, tkw, tk), lambda i, j, k: (k // per, j, k % per))
    else:
        w_spec = pl.BlockSpec((tkw, tk), lambda i, j, k: (j, k))
    mo = pl.BlockSpec((tm, tkw), lambda i, j, k: (i, j))
    return _matmul("nt", a, w, grid=(M // tm, kw // tkw, N // tk), a_spec=pl.BlockSpec((tm, tk), lambda i, j, k: (i, k)),
                   b_spec=w_spec, out_spec=mo, out_shape=jax.ShapeDtypeStruct((M, kw), out_dtype), acc_shape=(tm, tkw),
                   name=name, epilogue=epilogue, extras=extras, extra_specs=[mo] * len(extras), stages=stages)


def mm_tn(a, g, *, shards, name, out_dtype, a_pro=None, stages=(), tiles=(None, None, None)):
    T, kw = a.shape
    N = g.shape[1]
    ns = N // shards
    tkw = tiles[0] or _pick(kw, (1024, 512, 256, 128))
    tn = tiles[1] or _pick(ns, (1280, 1024, 512, 256, 128))
    tt = tiles[2] or _pick(T, (1408, 512, 256, 128))
    per = ns // tn
    if shards > 1:
        out_spec = pl.BlockSpec((None, tkw, tn), lambda i, j, k: (j // per, i, j % per))
        out_shape = jax.ShapeDtypeStruct((shards, kw, ns), out_dtype)
    else:
        out_spec = pl.BlockSpec((tkw, tn), lambda i, j, k: (i, j))
        out_shape = jax.ShapeDtypeStruct((kw, N), out_dtype)
    return _matmul("tn", a, g, grid=(kw // tkw, N // tn, T // tt), a_spec=pl.BlockSpec((tt, tkw), lambda i, j, k: (k, i)),
                   b_spec=pl.BlockSpec((tt, tn), lambda i, j, k: (k, j)), out_spec=out_spec, out_shape=out_shape,
                   acc_shape=(tkw, tn), name=name, a_pro=a_pro, stages=stages)


def _relu_sq(a):
    r = jnp.maximum(a, 0.0)
    return r * r


def rmsnorm_fwd(h, g, *, name, stages=()):
    Tp, D = h.shape
    tr = _pick(Tp, (384, 256, 128))

    def body(h_ref, g_ref, u_ref, ut_ref):
        x = h_ref[...]
        r = lax.rsqrt(jnp.mean(x * x, axis=-1, keepdims=True) + NORM_EPS)
        u = (x * r) * g_ref[...]
        u_ref[...] = u.astype(BF16)
        ut_ref[...] = u.T.astype(BF16)

    row = pl.BlockSpec((tr, D), lambda i: (i, 0))
    outs, staged = _staged_call(body, grid=(Tp // tr,), in_specs=[row, pl.BlockSpec((1, D), lambda i: (0, 0))],
                                out_specs=[row, pl.BlockSpec((D, tr), lambda i: (0, i))],
                                out_shape=[jax.ShapeDtypeStruct((Tp, D), BF16), jax.ShapeDtypeStruct((D, Tp), BF16)],
                                scratch_shapes=[], name=name, semantics=("parallel",), inputs=[h, g], stages=stages)
    return (outs, staged) if stages else outs


def rmsnorm_fwd_input(x, meta_tile, g, *, name, stages=()):
    S, D = x.shape
    nt = S // ROW_TILE + 1

    def body(x_ref, mt_ref, g_ref, h_ref, u_ref, ut_mine_ref, ut_theirs_ref):
        h = jnp.where(pl.program_id(0) == nt - 1, mt_ref[...], x_ref[...])
        r = lax.rsqrt(jnp.mean(h * h, axis=-1, keepdims=True) + NORM_EPS)
        u = (h * r) * g_ref[...]
        h_ref[...] = h
        u_ref[...] = u.astype(BF16)
        ut = u.T.astype(BF16)
        low, high = ut[:D // 2, :], ut[D // 2:, :]
        mine_is_high = lax.axis_index("c") == 1
        ut_mine_ref[...] = jnp.where(mine_is_high, high, low)
        ut_theirs_ref[...] = jnp.where(mine_is_high, low, high)

    row = pl.BlockSpec((ROW_TILE, D), lambda i: (i, 0))
    outs, staged = _staged_call(
        body, grid=(nt,),
        in_specs=[pl.BlockSpec((ROW_TILE, D), lambda i: (jnp.minimum(i, nt - 2), 0)), pl.BlockSpec((ROW_TILE, D), lambda i: (0, 0)),
                  pl.BlockSpec((1, D), lambda i: (0, 0))],
        out_specs=[row, row, pl.BlockSpec((D // 2, ROW_TILE), lambda i: (0, i)), pl.BlockSpec((D // 2, ROW_TILE), lambda i: (0, i))],
        out_shape=[jax.ShapeDtypeStruct((S + ROW_TILE, D), F32), jax.ShapeDtypeStruct((S + ROW_TILE, D), BF16),
                   jax.ShapeDtypeStruct((D // 2, S + ROW_TILE), BF16), jax.ShapeDtypeStruct((D // 2, S + ROW_TILE), BF16)],
        scratch_shapes=[], name=name, semantics=("parallel",), inputs=[x, meta_tile, g], stages=stages)
    return (outs, staged) if stages else outs


def _rms_bwd_math(x, g, dy):
    r = lax.rsqrt(jnp.mean(x * x, axis=-1, keepdims=True) + NORM_EPS)
    xh = x * r
    dyg = dy * g
    dx = r * (dyg - xh * jnp.mean(dyg * xh, axis=-1, keepdims=True))
    return dx, xh


def final_loss(h2, target, gf, *, name):
    Tp, D = h2.shape
    nt = Tp // ROW_TILE

    def body(h_ref, t_ref, g_ref, dhb_ref, st_ref):
        i = pl.program_id(0)

        @pl.when(i == 0)
        def _():
            st_ref[...] = jnp.zeros_like(st_ref)

        x = h_ref[...]
        g = g_ref[...]
        r = lax.rsqrt(jnp.mean(x * x, axis=-1, keepdims=True) + NORM_EPS)
        xh = x * r
        err = jnp.where(i == nt - 1, 0.0, xh * g - t_ref[...])
        dout = err * (1.0 / D)
        dyg = dout * g
        dx = r * (dyg - xh * jnp.mean(dyg * xh, axis=-1, keepdims=True))
        dhb_ref[...] = dx.astype(BF16)
        st_ref[0:1, :] += jnp.sum(dout * xh, axis=0, keepdims=True)
        st_ref[1:2, :] += jnp.sum(err * err, axis=0, keepdims=True) * (0.5 / D)

    row = pl.BlockSpec((ROW_TILE, D), lambda i: (i, 0))
    return pl.pallas_call(
        body, grid=(nt,),
        in_specs=[row, pl.BlockSpec((ROW_TILE, D), lambda i: (jnp.minimum(i, nt - 2), 0)), pl.BlockSpec((1, D), lambda i: (0, 0))],
        out_specs=[row, pl.BlockSpec((8, D), lambda i: (0, 0))],
        out_shape=[jax.ShapeDtypeStruct((Tp, D), BF16), jax.ShapeDtypeStruct((8, D), F32)],
        name=name, compiler_params=_params("arbitrary"))(h2, target, gf)


def rms_bwd(h, g, du, dres, *, name):
    Tp, D = h.shape
    tr = _pick(Tp, (384, 256, 128))

    def body(h_ref, g_ref, du_ref, dr_ref, dhb_ref, st_ref):
        @pl.when(pl.program_id(0) == 0)
        def _():
            st_ref[...] = jnp.zeros_like(st_ref)

        du_v = du_ref[...].astype(F32)
        dx, xh = _rms_bwd_math(h_ref[...], g_ref[...], du_v)
        dhb_ref[...] = (dr_ref[...].astype(F32) + dx).astype(BF16)
        st_ref[0:1, :] += jnp.sum(du_v * xh, axis=0, keepdims=True)

    row = pl.BlockSpec((tr, D), lambda i: (i, 0))
    return pl.pallas_call(
        body, grid=(Tp // tr,), in_specs=[row, pl.BlockSpec((1, D), lambda i: (0, 0)), row, row],
        out_specs=[row, pl.BlockSpec((8, D), lambda i: (0, 0))],
        out_shape=[jax.ShapeDtypeStruct((Tp, D), BF16), jax.ShapeDtypeStruct((8, D), F32)],
        name=name, compiler_params=_params("arbitrary"))(h, g, du, dres)


def rms_bwd_input(h, g, du, dres, *, name, stages=()):
    Tp, D = h.shape
    nt = Tp // ROW_TILE

    def body(h_ref, g_ref, du_ref, dr_ref, gx_ref, gm_ref, st_ref):
        i = pl.program_id(0)

        @pl.when(i == 0)
        def _():
            st_ref[...] = jnp.zeros_like(st_ref)

        du_v = du_ref[...].astype(F32)
        dx, xh = _rms_bwd_math(h_ref[...], g_ref[...], du_v)
        dh = dr_ref[...].astype(F32) + dx
        st_ref[0:1, :] += jnp.sum(du_v * xh, axis=0, keepdims=True)

        @pl.when(i < nt - 1)
        def _():
            gx_ref[...] = dh

        @pl.when(i == nt - 1)
        def _():
            gm_ref[...] = dh[ROW_TILE - N_META:, :]

    row = pl.BlockSpec((ROW_TILE, D), lambda i: (i, 0))
    outs, staged = _staged_call(
        body, grid=(nt,), in_specs=[row, pl.BlockSpec((1, D), lambda i: (0, 0)), row, row],
        out_specs=[pl.BlockSpec((ROW_TILE, D), lambda i: (jnp.minimum(i, nt - 2), 0)), pl.BlockSpec((N_META, D), lambda i: (0, 0)),
                   pl.BlockSpec((8, D), lambda i: (0, 0))],
        out_shape=[jax.ShapeDtypeStruct((Tp - ROW_TILE, D), F32), jax.ShapeDtypeStruct((N_META, D), F32),
                   jax.ShapeDtypeStruct((8, D), F32)],
        scratch_shapes=[], name=name, semantics=("arbitrary",), inputs=[h, g, du, dres], stages=stages)
    return (outs, staged) if stages else outs


def _conv_shifts(ext):
    return tuple(pltpu.roll(ext, k, 0)[HIST:, :] for k in (1, 2, 3))


def _conv_taps(ext, cur, vec_ref, cs):
    x1, x2, x3 = _conv_shifts(ext)
    return vec_ref[1:2, cs] + vec_ref[8:9, cs] * cur + vec_ref[7:8, cs] * x1 + vec_ref[6:7, cs] * x2 + vec_ref[5:6, cs] * x3


def _window_sum_back(ext, w):
    s, sh = ext, 1
    while sh < w:
        s = s + pltpu.roll(s, sh, 0)
        sh *= 2
    return s[HIST:, :]


def _scan_rows(a, b, carry, *, reverse):
    rows = a.shape[0]
    rin = jnp.bitwise_and(lax.broadcasted_iota(jnp.int32, (rows, 1), 0), 7)
    sh = 1
    while sh < 8:
        keep = (rin < 8 - sh) if reverse else (rin >= sh)
        amount = rows - sh if reverse else sh
        a_sh = jnp.where(keep, pltpu.roll(a, amount, 0), 1.0)
        b_sh = jnp.where(keep, pltpu.roll(b, amount, 0), 0.0)
        b = b + a * b_sh
        a = a * a_sh
        sh *= 2
    out = [None] * (rows // 8)
    for g in (reversed(range(rows // 8)) if reverse else range(rows // 8)):
        hg = b[8 * g:8 * g + 8, :] + a[8 * g:8 * g + 8, :] * carry
        carry = hg[0:1, :] if reverse else hg[7:8, :]
        out[g] = hg
    return jnp.concatenate(out, axis=0)


def _lru_gates(xc, wa, wx, vec_ref, cs, sp):
    xcb = xc.astype(BF16)
    r = _sigmoid(jnp.dot(xcb, wa, preferred_element_type=F32) + vec_ref[2:3, cs])
    ig = _sigmoid(jnp.dot(xcb, wx, preferred_element_type=F32) + vec_ref[3:4, cs])
    log_a = (-LRU_C) * r * sp
    a = jnp.exp(log_a)
    a2 = a * a
    return xcb, r, ig, a, a2, _neg_expm1(2.0 * log_a, a2)


def mix_fwd(proj, pool_w, gate_a, gate_x, vecs, *, name, stages=()):
    Tp = proj.shape[0]
    D = proj.shape[1] // 5
    R = ROW_TILE
    nt = Tp // R
    H = D // HEAD_DIM
    PG = D // len(POOL_WINDOWS)

    def body(p_ref, pw_ref, wa_ref, wx_ref, vec_ref, m_ref, hs_ref, xc_ref, r_ref, ig_ref, a_ref, mu_ref, ge_ref, dge_ref,
             hist_p, hist_l, hcar, mtmp):
        i = pl.program_id(0)
        is_meta = i == 0

        @pl.when(is_meta)
        def _():
            hist_p[...] = jnp.zeros_like(hist_p)
            hist_l[...] = jnp.zeros_like(hist_l)
            hcar[...] = jnp.zeros_like(hcar)

        row, valid, t_log = _tile_masks(is_meta, R)

        for g, w in enumerate(POOL_WINDOWS):
            cs = slice(g * PG, (g + 1) * PG)
            v = p_ref[:, g * PG:(g + 1) * PG].astype(F32)
            ws = _window_sum_back(jnp.concatenate([hist_p[:, cs], v], axis=0), w)
            d = ws * _window_count_inv(t_log, w) - v
            y = jnp.dot(d.astype(BF16), pw_ref[g], preferred_element_type=F32)
            gp = p_ref[:, 3 * D + g * PG:3 * D + (g + 1) * PG].astype(F32)
            mtmp[:, cs] = _sigmoid(gp) * (y * vec_ref[0:1, cs])
            hist_p[:, cs] = v[R - HIST:, :]

        for h in range(H):
            cs = slice(h * HEAD_DIM, (h + 1) * HEAD_DIM)
            vl = p_ref[:, D + h * HEAD_DIM:D + (h + 1) * HEAD_DIM].astype(F32)
            xc = _conv_taps(jnp.concatenate([hist_l[:, cs], vl], axis=0), vl, vec_ref, cs)
            sp = _softplus_neg(vec_ref[4:5, cs])
            xcb, r, ig, a, _, em = _lru_gates(xc, wa_ref[h], wx_ref[h], vec_ref, cs, sp)
            mult = jnp.sqrt(em)
            b = jnp.where(valid, mult * (ig * xc), 0.0)
            hs = _scan_rows(a, b, hcar[7:8, cs], reverse=False)
            hs_ref[:, cs] = hs
            xc_ref[:, cs], r_ref[:, cs], ig_ref[:, cs] = xcb, r.astype(BF16), ig.astype(BF16)
            a_ref[:, cs], mu_ref[:, cs] = a, mult
            hcar[:, cs] = hs[R - 8:, :]
            hist_l[:, cs] = vl[R - HIST:, :]
            vg = p_ref[:, 2 * D + h * HEAD_DIM:2 * D + (h + 1) * HEAD_DIM].astype(F32)
            ge, th = _gelu_tanh(vg)
            ge_ref[:, cs], dge_ref[:, cs] = ge.astype(BF16), _gelu_tanh_grad(vg, th).astype(BF16)
            gl = p_ref[:, 4 * D + h * HEAD_DIM:4 * D + (h + 1) * HEAD_DIM].astype(F32)
            m_ref[:, cs] = (mtmp[:, cs] + _sigmoid(gl) * (hs * ge)).astype(BF16)

    def tile(i):
        return (i + nt - 1) % nt

    full = lambda shape: pl.BlockSpec(shape, lambda i: (0,) * len(shape))
    outs, staged = _staged_call(
        body, grid=(nt,),
        in_specs=[pl.BlockSpec((R, 5 * D), lambda i: (tile(i), 0)), full(pool_w.shape), full(gate_a.shape), full(gate_x.shape),
                  full(vecs.shape)],
        out_specs=[pl.BlockSpec((R, D), lambda i: (tile(i), 0))] * 9,
        out_shape=[jax.ShapeDtypeStruct((Tp, D), dt) for dt in (BF16, F32, BF16, BF16, BF16, F32, F32, BF16, BF16)],
        scratch_shapes=[pltpu.VMEM((HIST, D), F32), pltpu.VMEM((HIST, D), F32), pltpu.VMEM((8, D), F32), pltpu.VMEM((R, D), F32)],
        name=name, semantics=("arbitrary",), inputs=[proj, pool_w, gate_a, gate_x, vecs], stages=stages)
    outs = [outs[0], outs[1], outs[2:]]
    return (outs, staged) if stages else outs


def mix_bwd(proj, hs, saved, dmerged, pool_w, gate_a, gate_x, vecs, *, name, stages=()):
    Tp = proj.shape[0]
    D = proj.shape[1] // 5
    R = ROW_TILE
    nt = Tp // R
    H = D // HEAD_DIM
    PG = D // len(POOL_WINDOWS)

    def body(p_ref, pprev_ref, hs_ref, hprev_ref, dm_ref, xc_ref, r_ref, ig_ref, a_ref, mu_ref, ge_ref, dge_ref, pw_ref, wa_ref, wx_ref, vec_ref,
             dp_ref, dpw_ref, dwa_ref, dwx_ref, dvec_ref, car_g, fut_dxc, fut_q):
        i = pl.program_id(0)
        is_meta = i == nt - 1

        @pl.when(i == 0)
        def _():
            dpw_ref[...] = jnp.zeros_like(dpw_ref)
            dwa_ref[...] = jnp.zeros_like(dwa_ref)
            dwx_ref[...] = jnp.zeros_like(dwx_ref)
            dvec_ref[...] = jnp.zeros_like(dvec_ref)
            car_g[...] = jnp.zeros_like(car_g)
            fut_dxc[...] = jnp.zeros_like(fut_dxc)
            fut_q[...] = jnp.zeros_like(fut_q)

        row, valid, t_log = _tile_masks(is_meta, R)
        keep_prev = jnp.logical_not(is_meta)

        def colsum(x):
            return jnp.sum(x, axis=0, keepdims=True)

        for g, w in enumerate(POOL_WINDOWS):
            cs = slice(g * PG, (g + 1) * PG)
            v = p_ref[:, g * PG:(g + 1) * PG].astype(F32)
            vprev = jnp.where(keep_prev, pprev_ref[:, g * PG:(g + 1) * PG].astype(F32), 0.0)
            inv_cnt = _window_count_inv(t_log, w)
            d = _window_sum_back(jnp.concatenate([vprev, v], axis=0), w) * inv_cnt - v
            d_bf = d.astype(BF16)
            y = jnp.dot(d_bf, pw_ref[g], preferred_element_type=F32)
            scale = vec_ref[0:1, cs]
            sg = _sigmoid(p_ref[:, 3 * D + g * PG:3 * D + (g + 1) * PG].astype(F32))
            dm = dm_ref[:, cs].astype(F32)
            dpo = dm * sg
            dp_ref[:, 3 * D + g * PG:3 * D + (g + 1) * PG] = (dm * (y * scale) * sg * (1.0 - sg)).astype(BF16)
            dvec_ref[0:1, cs] += colsum(dpo * y)
            dy = (dpo * scale).astype(BF16)
            dd = _dot_nt(dy, pw_ref[g])
            dpw_ref[g] += _dot_tn(d_bf, dy)
            q = dd * inv_cnt
            s, sh = jnp.concatenate([q, fut_q[:, cs]], axis=0), 1
            while sh < w:
                s = s + pltpu.roll(s, R + HIST - sh, 0)
                sh *= 2
            dp_ref[:, g * PG:(g + 1) * PG] = (s[:R, :] - dd).astype(BF16)
            fut_q[:, cs] = q[:HIST, :]

        for h in range(H):
            cs = slice(h * HEAD_DIM, (h + 1) * HEAD_DIM)
            pc = lambda blk: slice(blk * D + h * HEAD_DIM, blk * D + (h + 1) * HEAD_DIM)
            vl = p_ref[:, pc(1)].astype(F32)
            vlprev = jnp.where(keep_prev, pprev_ref[:, pc(1)].astype(F32), 0.0)
            x1, x2, x3 = _conv_shifts(jnp.concatenate([vlprev, vl], axis=0))
            lam = vec_ref[4:5, cs]
            sp = _softplus_neg(lam)
            xcb = xc_ref[:, cs]
            xc, r, ig = xcb.astype(F32), r_ref[:, cs].astype(F32), ig_ref[:, cs].astype(F32)
            a, mult = a_ref[:, cs], mu_ref[:, cs]
            a2 = a * a
            inv_mult = 1.0 / mult
            hsv = hs_ref[:, cs]
            hprev = jnp.where(row >= 1, pltpu.roll(hsv, 1, 0), hprev_ref[HIST - 1:HIST, cs])
            ge = ge_ref[:, cs].astype(F32)
            sgl = _sigmoid(p_ref[:, pc(4)].astype(F32))
            dm = dm_ref[:, cs].astype(F32)
            dlo = dm * sgl
            dp_ref[:, pc(4)] = (dm * (hsv * ge) * sgl * (1.0 - sgl)).astype(BF16)
            dp_ref[:, pc(2)] = (dlo * hsv * dge_ref[:, cs].astype(F32)).astype(BF16)
            a_next = jnp.where(row < R - 1, pltpu.roll(a, R - 1, 0), 1.0)
            G = _scan_rows(a_next, dlo * ge, car_g[0:1, cs], reverse=True)
            car_g[:, cs] = (a * G)[0:8, :]
            da = jnp.where(valid, G * hprev, 0.0)
            db = jnp.where(valid, G, 0.0)
            dmult = db * (ig * xc)
            dig = db * (mult * xc)
            dxc = db * (mult * ig)
            dlog_a = da * a - dmult * (a2 * inv_mult)
            dvec_ref[4:5, cs] += colsum(dlog_a * r) * (-LRU_C)
            dr = dlog_a * ((-LRU_C) * sp)
            dpa = dr * r * (1.0 - r)
            dpx = dig * ig * (1.0 - ig)
            dpa_bf = dpa.astype(BF16)
            dpx_bf = dpx.astype(BF16)
            dwa_ref[h] += _dot_tn(xcb, dpa_bf)
            dwx_ref[h] += _dot_tn(xcb, dpx_bf)
            dvec_ref[2:3, cs] += colsum(dpa)
            dvec_ref[3:4, cs] += colsum(dpx)
            dxc = dxc + _dot_nt(dpa_bf, wa_ref[h]) + _dot_nt(dpx_bf, wx_ref[h])
            ext = jnp.concatenate([dxc, fut_dxc[:, cs]], axis=0)
            n = R + HIST
            dvl = (vec_ref[8:9, cs] * dxc + vec_ref[7:8, cs] * pltpu.roll(ext, n - 1, 0)[:R, :]
                   + vec_ref[6:7, cs] * pltpu.roll(ext, n - 2, 0)[:R, :] + vec_ref[5:6, cs] * pltpu.roll(ext, n - 3, 0)[:R, :])
            dp_ref[:, pc(1)] = dvl.astype(BF16)
            dvec_ref[1:2, cs] += colsum(dxc)
            dvec_ref[8:9, cs] += colsum(dxc * vl)
            dvec_ref[7:8, cs] += colsum(dxc * x1)
            dvec_ref[6:7, cs] += colsum(dxc * x2)
            dvec_ref[5:6, cs] += colsum(dxc * x3)
            fut_dxc[:, cs] = dxc[:HIST, :]

            @pl.when(is_meta)
            def _():
                dvec_ref[4:5, cs] = dvec_ref[4:5, cs] * (-_sigmoid(-lam))

    def tile(i):
        return (2 * nt - 2 - i) % nt

    def prev_blk(i):
        per = R // HIST
        return jnp.where(i == nt - 1, 0, jnp.where(i == nt - 2, Tp // HIST - 1, (nt - 2 - i) * per - 1))

    full = lambda shape: pl.BlockSpec(shape, lambda i: (0,) * len(shape))
    G_ = len(POOL_WINDOWS)
    outs, staged = _staged_call(
        body, grid=(nt,),
        in_specs=[pl.BlockSpec((R, 5 * D), lambda i: (tile(i), 0)), pl.BlockSpec((HIST, 5 * D), lambda i: (prev_blk(i), 0)),
                  pl.BlockSpec((R, D), lambda i: (tile(i), 0)), pl.BlockSpec((HIST, D), lambda i: (prev_blk(i), 0)),
                  *[pl.BlockSpec((R, D), lambda i: (tile(i), 0))] * 8,
                  full(pool_w.shape), full(gate_a.shape), full(gate_x.shape), full(vecs.shape)],
        out_specs=[pl.BlockSpec((R, 5 * D), lambda i: (tile(i), 0)), full((G_, PG, PG)), full((H, HEAD_DIM, HEAD_DIM)),
                   full((H, HEAD_DIM, HEAD_DIM)), full((16, D))],
        out_shape=[jax.ShapeDtypeStruct((Tp, 5 * D), BF16), jax.ShapeDtypeStruct((G_, PG, PG), F32),
                   jax.ShapeDtypeStruct((H, HEAD_DIM, HEAD_DIM), F32), jax.ShapeDtypeStruct((H, HEAD_DIM, HEAD_DIM), F32),
                   jax.ShapeDtypeStruct((16, D), F32)],
        scratch_shapes=[pltpu.VMEM((8, D), F32), pltpu.VMEM((HIST, D), F32), pltpu.VMEM((HIST, D), F32)],
        name=name, semantics=("arbitrary",), inputs=[proj, proj, hs, hs, dmerged, *saved, pool_w, gate_a, gate_x, vecs], stages=stages)
    return (outs, staged) if stages else outs


def _adamw_math(w, g, m, v):
    mn = ADAM_B1 * m + (1.0 - ADAM_B1) * g
    vn = ADAM_B2 * v + (1.0 - ADAM_B2) * (g * g)
    m_hat = mn / (1.0 - ADAM_B1 ** ADAM_STEP)
    v_hat = vn / (1.0 - ADAM_B2 ** ADAM_STEP)
    return -ADAM_LR * (m_hat / (jnp.sqrt(v_hat) + ADAM_EPS) + ADAM_WD * w), mn, vn


def adamw(w, g, m, v, *, name):
    rows, cols = w.shape
    tr = _pick(rows, (256, 128, 64, 32, 16, 8))

    def body(w_ref, g_ref, m_ref, v_ref, go_ref, d_ref, mo_ref, vo_ref):
        gv = g_ref[...]
        go_ref[...] = gv
        d_ref[...], mo_ref[...], vo_ref[...] = _adamw_math(w_ref[...], gv, m_ref[...], v_ref[...])

    blk = pl.BlockSpec((tr, cols), lambda i: (i, 0))
    sds = jax.ShapeDtypeStruct((rows, cols), F32)
    return pl.pallas_call(body, grid=(rows // tr,), in_specs=[blk] * 4, out_specs=[blk] * 4, out_shape=[sds] * 4, name=name,
                          compiler_params=_params("parallel"))(w, g, m, v)


def allgather8(block, *, name, reduce_sum=False, stages=()):
    rows, cols = block.shape

    def body(x_ref, out_ref, *scratch):
        if reduce_sum:
            buf, send_sems, recv_sems, local_sem = scratch
        else:
            buf = out_ref
            send_sems, recv_sems, local_sem = scratch
        x, y, c, chips = _place()
        me, sibling = (x, y, c), (x, y, 1 - c)

        def slot(px, py, pc):
            return buf.at[4 * px + 2 * py + pc]

        def copy(k, blk, to, src=None):
            return pltpu.make_async_remote_copy(src_ref=slot(*blk) if src is None else src, dst_ref=slot(*blk),
                                                send_sem=send_sems.at[k], recv_sem=recv_sems.at[k], device_id=to, device_id_type=MESH)

        mine = pltpu.make_async_copy(x_ref, slot(*me), local_sem)
        mine.start()
        first = [copy(0, me, sibling, src=x_ref)]
        first += [copy(1 + j, me, (*chip, c), src=x_ref) for j, chip in enumerate(chips)]
        for cp in first:
            cp.start()
        passed = [copy(4 + j, (*chip, c), sibling) for j, chip in enumerate(chips)]
        for j, chip in enumerate(chips):
            copy(1 + j, (*chip, c), me).wait_recv()
            passed[j].start()
        copy(0, sibling, me).wait_recv()
        for j, chip in enumerate(chips):
            copy(4 + j, (*chip, 1 - c), me).wait_recv()
        for cp in first + passed:
            cp.wait_send()
        mine.wait()
        if reduce_sum:
            acc = buf[0]
            for d in range(1, 8):
                acc = acc + buf[d]
            out_ref[...] = acc

    sems = [pltpu.SemaphoreType.DMA((7,)), pltpu.SemaphoreType.DMA((7,)), pltpu.SemaphoreType.DMA]
    if reduce_sum:
        out_shape = jax.ShapeDtypeStruct((rows, cols), block.dtype)
        scratch = [pltpu.VMEM((8, rows, cols), block.dtype)] + sems
    else:
        out_shape = jax.ShapeDtypeStruct((8, rows, cols), block.dtype)
        scratch = sems
    vmem = pl.BlockSpec(memory_space=pltpu.VMEM)
    outs, staged = _staged_call(body, grid=(1,), in_specs=[vmem], out_specs=[vmem], out_shape=[out_shape], scratch_shapes=scratch,
                                name=name, semantics=("arbitrary",), inputs=[block], stages=stages)
    return (outs[0], staged) if stages else outs[0]


def cast_into_slot(w, chip_arr, *, name):
    _, r, cols = w.shape
    tr = _pick(r, (256, 128, 64, 32, 16))

    def body(chip_ref, w_ref, o_ref):
        del chip_ref
        o_ref[...] = w_ref[...].astype(BF16)

    grid_spec = pltpu.PrefetchScalarGridSpec(
        num_scalar_prefetch=1, grid=(2, r // tr),
        in_specs=[pl.BlockSpec((None, tr, cols), lambda h, i, chip: (h, i, 0))],
        out_specs=pl.BlockSpec((None, None, tr, cols), lambda h, i, chip: (chip[0], h, i, 0)))
    return pl.pallas_call(body, grid_spec=grid_spec, out_shape=jax.ShapeDtypeStruct((4, 2, r, cols), BF16), name=name,
                          compiler_params=_params("parallel", "parallel"))(chip_arr, w)


def proj_with_gather(u, bufs, order_arr, *, head_start, name):
    Tp, K = u.shape
    n = len(bufs)
    Ns = bufs[0].shape[3]
    tm = _pick(Tp, (1408, 512, 256, 128))
    tc = _pick(Ns, (512, 256, 128))
    n_rows = Tp // tm

    def body(order_ref, u_ref, *refs):
        del order_ref
        o_ref, outs = refs[n], refs[n + 1:2 * n + 1]
        wbuf, dir_send, dir_recv, rel_send, rel_recv, d2d_send, d2d_recv, load_sems = refs[2 * n + 1:]
        s, i = pl.program_id(0), pl.program_id(1)
        x, y, c, chips = _place()
        me = 2 * x + y
        (dx, dy) = chips[2]

        def direct(t, j, landing):
            px, py = chips[j]
            chip_idx = 2 * px + py if landing else me
            slot = _relay_rows(outs[t], chip_idx, c, 1) if (t == 0 and head_start) else outs[t].at[chip_idx, c]
            return _remote(slot, slot, dir_send.at[t, j], dir_recv.at[t, j], (px, py, c))

        def relay(t, j, landing):
            px, py = chips[j]
            ox, oy = chips[1 - j]
            rows = _relay_rows(outs[t], 2 * dx + dy, c, j) if landing else _relay_rows(outs[t], 2 * ox + oy, c, j)
            return _remote(rows, rows, rel_send.at[t, j], rel_recv.at[t, j], (px, py, c))

        def d2d(t, j, landing):
            px, py = chips[j]
            slot = outs[t].at[2 * px + py, 1 - c] if landing else outs[t].at[2 * px + py, c]
            return _remote(slot, slot, d2d_send.at[t, j], d2d_recv.at[t, j], (x, y, 1 - c))

        def load(chip_idx):
            parts = [pltpu.make_async_copy(outs[0].at[chip_idx, hh], wbuf.at[pl.ds(hh * (K // 2), K // 2), :], load_sems.at[hh])
                     for hh in range(2)]
            for cp in parts:
                cp.start()
            for cp in parts:
                cp.wait()

        def chip_of(j):
            px, py = chips[j]
            return 2 * px + py

        @pl.when(jnp.logical_and(s == 0, i == 0))
        def _():
            for t in range(n):
                for j in range(2):
                    direct(t, j, False).start()
            if head_start:
                relay(0, 0, False).start()
            load(me)

        @pl.when(jnp.logical_and(s == 1, i == 0))
        def _():
            for j in range(2):
                direct(0, j, True).wait_recv()
            for j in range(2):
                if not (head_start and j == 0):
                    relay(0, j, False).start()
                d2d(0, j, False).start()
            d2d(0, 0, True).wait_recv()
            load(chip_of(0))

        @pl.when(jnp.logical_and(s == 2, i == 0))
        def _():
            d2d(0, 1, True).wait_recv()
            for t in range(1, n):
                for j in range(2):
                    direct(t, j, True).wait_recv()
                for j in range(2):
                    relay(t, j, False).start()
                    d2d(t, j, False).start()
            load(chip_of(1))

        @pl.when(jnp.logical_and(s == 3, i == 0))
        def _():
            for j in range(2):
                relay(0, j, True).wait_recv()
            d2d(0, 2, False).start()
            d2d(0, 2, True).wait_recv()
            load(chip_of(2))

        @pl.when(jnp.logical_and(s == 3, i == min(1, n_rows - 1)))
        def _():
            for t in range(1, n):
                for j in range(2):
                    relay(t, j, True).wait_recv()
                d2d(t, 2, False).start()

        uv = u_ref[...]
        for cc in range(Ns // tc):
            o_ref[:, cc * tc:(cc + 1) * tc] = jnp.dot(uv, wbuf[:, cc * tc:(cc + 1) * tc], preferred_element_type=F32).astype(BF16)

        @pl.when(jnp.logical_and(s == 3, i == n_rows - 1))
        def _():
            for t in range(1, n):
                for j in range(3):
                    d2d(t, j, True).wait_recv()
            for t in range(n):
                for j in range(2):
                    direct(t, j, False).wait_send()
                    relay(t, j, False).wait_send()
                for j in range(3):
                    d2d(t, j, False).wait_send()

    grid_spec = pltpu.PrefetchScalarGridSpec(
        num_scalar_prefetch=1, grid=(4, n_rows),
        in_specs=[pl.BlockSpec((tm, K), lambda s, i, order: (i, 0)), *_any_specs(n)],
        out_specs=[pl.BlockSpec((tm, Ns), lambda s, i, order: (i, order[s])), *_any_specs(n)],
        scratch_shapes=[pltpu.VMEM((K, Ns), BF16), pltpu.SemaphoreType.DMA((n, 2)), pltpu.SemaphoreType.DMA((n, 2)),
                        pltpu.SemaphoreType.DMA((n, 2)), pltpu.SemaphoreType.DMA((n, 2)),
                        pltpu.SemaphoreType.DMA((n, 3)), pltpu.SemaphoreType.DMA((n, 3)), pltpu.SemaphoreType.DMA((2,))])
    res = pl.pallas_call(body, grid_spec=grid_spec, out_shape=[jax.ShapeDtypeStruct((Tp, 4 * Ns), BF16), *[_sds(b) for b in bufs]],
                         input_output_aliases={2 + t: 1 + t for t in range(n)}, name=name,
                         compiler_params=_params("arbitrary", "arbitrary"))(order_arr, u, *bufs)
    return res[0], list(res[1:])


def chip_presum(grad, recv, c_arr, *, name):
    _, _, r, cols = grad.shape
    tr = _pick(r, (256, 128, 64, 32, 16))

    def body(c_ref, g_ref, r_ref, o_ref):
        del c_ref
        o_ref[...] = (g_ref[...].astype(F32) + r_ref[...].astype(F32)).astype(BF16)

    grid_spec = pltpu.PrefetchScalarGridSpec(
        num_scalar_prefetch=1, grid=(4, r // tr),
        in_specs=[pl.BlockSpec((None, None, tr, cols), lambda k, i, c_ref: (k, c_ref[0], i, 0)),
                  pl.BlockSpec((None, tr, cols), lambda k, i, c_ref: (k, i, 0))],
        out_specs=pl.BlockSpec((None, tr, cols), lambda k, i, c_ref: (k, i, 0)))
    return pl.pallas_call(body, grid_spec=grid_spec, out_shape=jax.ShapeDtypeStruct((4, r, cols), BF16), name=name,
                          compiler_params=_params("parallel", "parallel"))(c_arr, grad, recv)


def final_half(grad, recv, got, mc_arr, *, name):
    _, _, r, cols = grad.shape
    tr = _pick(r, (256, 128, 64, 32, 16))

    def body(mc_ref, g_ref, r_ref, q_ref, o_ref):
        del mc_ref
        acc = g_ref[...].astype(F32) + r_ref[...].astype(F32)
        for j in range(3):
            acc = acc + q_ref[j].astype(F32)
        o_ref[...] = acc

    grid_spec = pltpu.PrefetchScalarGridSpec(
        num_scalar_prefetch=1, grid=(r // tr,),
        in_specs=[pl.BlockSpec((None, None, tr, cols), lambda i, mc: (mc[0], mc[1], i, 0)),
                  pl.BlockSpec((None, tr, cols), lambda i, mc: (mc[0], i, 0)),
                  pl.BlockSpec((3, tr, cols), lambda i, mc: (0, i, 0))],
        out_specs=pl.BlockSpec((tr, cols), lambda i, mc: (i, 0)))
    return pl.pallas_call(body, grid_spec=grid_spec, out_shape=jax.ShapeDtypeStruct((r, cols), F32), name=name,
                          compiler_params=_params("parallel"))(mc_arr, grad, recv, got)


def adamw_halves(w, mine, theirs, m, v, c_arr, *, name):
    _, r, cols = w.shape
    tr = _pick(r, (256, 128, 64, 32, 16, 8))

    def body(c_ref, w_ref, mine_ref, theirs_ref, m_ref, v_ref, go_ref, d_ref, mo_ref, vo_ref):
        gv = jnp.where(pl.program_id(0) == c_ref[0], mine_ref[...], theirs_ref[...])
        go_ref[...] = gv
        d_ref[...], mo_ref[...], vo_ref[...] = _adamw_math(w_ref[...], gv, m_ref[...], v_ref[...])

    blk = pl.BlockSpec((None, tr, cols), lambda h, i, c_ref: (h, i, 0))
    grid_spec = pltpu.PrefetchScalarGridSpec(
        num_scalar_prefetch=1, grid=(2, r // tr),
        in_specs=[blk, pl.BlockSpec((tr, cols), lambda h, i, c_ref: (jnp.where(h == c_ref[0], i, 0), 0)),
                  pl.BlockSpec((tr, cols), lambda h, i, c_ref: (jnp.where(h == c_ref[0], 0, i), 0)), blk, blk],
        out_specs=[blk] * 4)
    sds = jax.ShapeDtypeStruct((2, r, cols), F32)
    return pl.pallas_call(body, grid_spec=grid_spec, out_shape=[sds] * 4, name=name,
                          compiler_params=_params("parallel", "parallel"))(c_arr, w, mine, theirs, m, v)


def _pad_rows(a, rows):
    return jnp.pad(a, ((0, rows - a.shape[0]), (0, 0)))


def kernel(x, meta_tokens, norm1_g, w_in, pool_w, pool_scale, conv_w, conv_b, gate_a_w, gate_a_b, gate_x_w, gate_x_b, lru_lambda, w_out, norm2_g, mlp_w1, mlp_w2, final_g, loss_target, m_meta_tokens, m_norm1_g, m_w_in, m_pool_w, m_pool_scale, m_conv_w, m_conv_b, m_gate_a_w, m_gate_a_b, m_gate_x_w, m_gate_x_b, m_lru_lambda, m_w_out, m_norm2_g, m_mlp_w1, m_mlp_w2, m_final_g, v_meta_tokens, v_norm1_g, v_w_in, v_pool_w, v_pool_scale, v_conv_w, v_conv_b, v_gate_a_w, v_gate_a_b, v_gate_x_w, v_gate_x_b, v_lru_lambda, v_w_out, v_norm2_g, v_mlp_w1, v_mlp_w2, v_final_g):
    D = x.shape[-1]
    H = D // HEAD_DIM
    G = len(POOL_WINDOWS)
    PG = D // G
    ax, ay, ac = lax.axis_index("x"), lax.axis_index("y"), lax.axis_index("c")
    chip = 2 * ax + ay
    dshard = D // 4

    c_arr = jnp.reshape(ac, (1,)).astype(jnp.int32)
    chip_arr = jnp.reshape(chip, (1,)).astype(jnp.int32)
    mc_arr = jnp.stack([chip, ac]).astype(jnp.int32)
    names = ["w_in", "pool_w", "gate_a_w", "gate_x_w", "w_out", "mlp_w1", "mlp_w2"]
    big = [w_in, pool_w, gate_a_w, gate_x_w, w_out, mlp_w1, mlp_w2]

    def halves(w):
        w2d = w.reshape(-1, w.shape[-1])
        return w2d.reshape(2, w2d.shape[0] // 2, w2d.shape[1])

    bufs = [cast_into_slot(halves(w), chip_arr, name="cast_" + nm) for w, nm in zip(big, names)]
    order_arr = jnp.stack([chip, 2 * (1 - ax) + ay, 2 * ax + (1 - ay), 2 * (1 - ax) + (1 - ay)]).astype(jnp.int32)

    small_in = jnp.concatenate([meta_tokens, _pad_rows(conv_w[0], 8), _pad_rows(gate_a_b.reshape(1, dshard), 8),
                                _pad_rows(gate_x_b.reshape(1, dshard), 8)], axis=0)
    sm = allgather8(small_in, name="gather_small")[0::2]
    meta_f = sm[:, 0:16].transpose(1, 0, 2).reshape(N_META, D)
    conv_w_f = sm[:, 16:20].transpose(1, 0, 2).reshape(4, D)
    hd4 = HEAD_DIM // 4
    ba_f = sm[:, 24].reshape(4, H, hd4).transpose(1, 0, 2).reshape(1, D)
    bx_f = sm[:, 32].reshape(4, H, hd4).transpose(1, 0, 2).reshape(1, D)
    vecs = jnp.zeros((16, D), F32)
    for r0, part in ((0, pool_scale), (1, conv_b), (2, ba_f), (3, bx_f), (4, lru_lambda), (5, conv_w_f)):
        vecs = lax.dynamic_update_slice(vecs, part, (r0, 0))

    def chipwise(g, n_blocks, rows):
        return g.reshape(n_blocks, 4, rows, g.shape[-1]).transpose(1, 0, 2, 3).reshape(4, n_blocks * rows, g.shape[-1])

    def split2(g):
        return g.reshape(4, 2, g.shape[1] // 2, g.shape[2])

    def presum(t, g, r):
        return chip_presum(g, r, c_arr, name="presum_" + names[t])

    def total(t, g, r, q):
        return final_half(g, r, q, mc_arr, name="sum_" + names[t])

    xs, target, gfin = x[0], loss_target[0], final_g.reshape(1, D)
    meta_tile = jnp.concatenate([jnp.zeros((ROW_TILE - N_META, D), F32), meta_f], axis=0)
    (h0, u, u_mine, u_theirs), [[b_win]] = rmsnorm_fwd_input(xs, meta_tile, norm1_g, name="norm1",
                                                stages=[stage_gather_direct([bufs[0]], quarter=0)])
    proj, (b_win, b_pool, b_ga, b_gx, b_wout) = proj_with_gather(u, [b_win, *bufs[1:5]], order_arr, head_start=True, name="proj")
    w_in_f = b_win.reshape(4, w_in.shape[1], w_in.shape[2])
    pool_f = b_pool.reshape(4, G, PG // 4, PG).transpose(1, 0, 2, 3).reshape(G, PG, PG)
    ga_f = b_ga.reshape(4, H, HEAD_DIM // 4, HEAD_DIM).transpose(1, 0, 2, 3).reshape(H, HEAD_DIM, HEAD_DIM)
    gx_f = b_gx.reshape(4, H, HEAD_DIM // 4, HEAD_DIM).transpose(1, 0, 2, 3).reshape(H, HEAD_DIM, HEAD_DIM)
    w_out_f = b_wout.reshape(4 * w_out.shape[1], w_out.shape[2])
    (merged, hs, saved), [[b_w1, b_w2]] = mix_fwd(proj, pool_f, ga_f, gx_f, vecs, name="mix_fwd",
                                                  stages=[stage_gather_direct([bufs[5], bufs[6]])])
    h1, [[b_w1]] = mm_nn(merged, w_out_f, out_dtype=F32, name="out_proj", epilogue=lambda r, res: r + res, extras=(h0,),
                         stages=[stage_both(stage_gather_relay([b_w1]), stage_gather_d2d([b_w1], peers=(0, 1)))])
    (u2, u2_t), [[b_w1]] = rmsnorm_fwd(h1, norm2_g, name="norm2", stages=[stage_gather_d2d([b_w1], peers=(2,))])
    w1_f = b_w1.reshape(4, mlp_w1.shape[1], mlp_w1.shape[2])
    a1, [[b_w2]] = mm_nn(u2, w1_f, out_dtype=BF16, name="mlp_up", tiles=(None, min(1024, D), None),
                         stages=[stage_both(stage_gather_relay([b_w2]), stage_gather_d2d([b_w2], peers=(0, 1)))])
    [[b_w2]] = comm_call([stage_gather_d2d([b_w2], peers=(2,))], name="w2_to_sibling")
    w2_f = b_w2.reshape(4 * mlp_w2.shape[1], mlp_w2.shape[2])
    Tp = h0.shape[0]
    h2 = mm_nn(a1, w2_f, out_dtype=F32, name="mlp_down", a_pro=_relu_sq, epilogue=lambda r, res: r + res, extras=(h1,),
               tiles=(_pick(Tp, (704, 512, 256, 128)), min(256, D), a1.shape[1]))
    dh2_bf, st_f = final_loss(h2, target, gfin, name="final_loss")

    da1 = mm_nt(dh2_bf, w2_f, out_dtype=BF16, name="mlp_down_dx",
                epilogue=lambda r, a: r * (2.0 * jnp.maximum(a.astype(F32), 0.0)), extras=(a1,))
    d_w2 = mm_tn(a1, dh2_bf, shards=1, out_dtype=BF16, name="mlp_down_dw", a_pro=_relu_sq, tiles=(512, min(1024, D), Tp))
    g6 = split2(d_w2.reshape(4, -1, d_w2.shape[-1]))
    d_w1, [[r6]] = mm_nn(u2_t, da1, out_dtype=BF16, name="mlp_up_dw", out_shards=4, tiles=(min(1024, D), min(1024, D), Tp),
                         stages=[stage_to_sibling([g6])])
    g5 = split2(d_w1)
    p6 = presum(6, g6, r6)
    du2, [[q6_near], [r5]] = mm_nt(da1, w1_f, out_dtype=BF16, name="mlp_up_dx",
                                  stages=[stage_to_chips([p6], peers=(0, 1)), stage_to_sibling([g5])])
    p5 = presum(5, g5, r5)
    dh1_bf, st_2 = rms_bwd(h1, norm2_g, du2, dh2_bf, name="norm2_bwd")
    dmerged = mm_nt(dh1_bf, w_out_f, out_dtype=BF16, name="out_proj_dx")
    d_wout = mm_tn(merged, dh1_bf, shards=1, out_dtype=BF16, name="out_proj_dw", tiles=(512, min(1024, D), Tp))
    g4 = split2(d_wout.reshape(4, -1, d_wout.shape[-1]))
    (dproj, d_pool, d_ga, d_gx, d_vecs), [[q6], [q5], [r4]] = mix_bwd(
        proj, hs, saved, dmerged, pool_f, ga_f, gx_f, vecs, name="mix_bwd",
        stages=[stage_to_chips([p6], peers=(2,), into=[q6_near]), stage_to_chips([p5]), stage_to_sibling([g4])])
    p4 = presum(4, g4, r4)
    f5, f6 = total(5, g5, r5, q5), total(6, g6, r6, q6)
    g1, g2, g3 = split2(chipwise(d_pool, G, PG // 4)), split2(chipwise(d_ga, H, hd4)), split2(chipwise(d_gx, H, hd4))
    whole_k = (u_mine.shape[0], _pick(dproj.shape[1] // 4, (1280, 512, 256, 128)), Tp)
    g_theirs, [[r1, r2, r3], [o5, o6], [q4]] = mm_nn(u_theirs, dproj, out_dtype=BF16, out_shards=4, tiles=whole_k, name="proj_dw_sibling_rows",
                                                    stages=[stage_to_sibling([g1, g2, g3]), stage_from_sibling([f5, f6]), stage_to_chips([p4])])
    p1, p2, p3 = presum(1, g1, r1), presum(2, g2, r2), presum(3, g3, r3)
    g_mine, [[r0], [q1, q2, q3]] = mm_nn(u_mine, dproj, out_dtype=BF16, out_shards=4, tiles=whole_k, name="proj_dw_own_rows",
                                         stages=[stage_from_sibling([g_theirs]), stage_to_chips([p1, p2, p3])])
    g0 = g_mine[:, None]
    p0 = chip_presum(g0, r0, jnp.zeros((1,), jnp.int32), name="presum_w_in")
    f4 = total(4, g4, r4, q4)
    du, [[q0], [o4]] = mm_nt(dproj, w_in_f, out_dtype=BF16, name="proj_dx", tiles=(None, None, w_in_f.shape[2]),
                             stages=[stage_to_chips([p0]), stage_from_sibling([f4])])
    f0 = final_half(g0, r0, q0, jnp.stack([chip, 0]).astype(jnp.int32), name="sum_w_in")
    f1, f2, f3 = total(1, g1, r1, q1), total(2, g2, r2, q2), total(3, g3, r3, q3)
    grad_x, d_meta, st_1 = rms_bwd_input(h0, norm1_g, du, dh1_bf, name="norm1_bwd")

    small = jnp.concatenate([d_meta, d_vecs, st_1, st_2, st_f], axis=0)
    tot, [[o0, o1, o2, o3]] = allgather8(small, name="sum_small", reduce_sum=True, stages=[stage_from_sibling([f0, f1, f2, f3])])
    mine = [f0, f1, f2, f3, f4, f5, f6]
    theirs = [o0, o1, o2, o3, o4, o5, o6]
    loss = jnp.sum(tot[49])
    g_meta = lax.dynamic_slice_in_dim(tot[0:16], chip * dshard, dshard, axis=1)
    g_pool_scale, g_conv_b, g_lam = tot[16:17], tot[17:18], tot[20:21]
    g_ba = lax.dynamic_slice_in_dim(tot[18].reshape(H, HEAD_DIM), chip * hd4, hd4, axis=1)[None]
    g_bx = lax.dynamic_slice_in_dim(tot[19].reshape(H, HEAD_DIM), chip * hd4, hd4, axis=1)[None]
    g_conv_w = lax.dynamic_slice_in_dim(tot[21:25], chip * dshard, dshard, axis=1)[None]
    g_n1, g_n2, g_fin = tot[32:33], tot[40:41], tot[48]

    def step(w, g, m, v, nm):
        cols = w.shape[-1]
        outs = adamw(w.reshape(-1, cols), g.reshape(-1, cols), m.reshape(-1, cols), v.reshape(-1, cols), name="adamw_" + nm)
        return [o.reshape(w.shape) for o in outs]

    def step_big(t, w, m, v):
        outs = adamw_halves(halves(w), mine[t], theirs[t], halves(m), halves(v), c_arr, name="adamw_" + names[t])
        return [o.reshape(w.shape) for o in outs]

    res = dict(meta_tokens=step(meta_tokens, g_meta, m_meta_tokens, v_meta_tokens, "meta_tokens"),
               norm1_g=step(norm1_g, g_n1, m_norm1_g, v_norm1_g, "norm1_g"),
               w_in=step_big(0, w_in, m_w_in, v_w_in), pool_w=step_big(1, pool_w, m_pool_w, v_pool_w),
               pool_scale=step(pool_scale, g_pool_scale, m_pool_scale, v_pool_scale, "pool_scale"),
               conv_w=step(conv_w, g_conv_w, m_conv_w, v_conv_w, "conv_w"), conv_b=step(conv_b, g_conv_b, m_conv_b, v_conv_b, "conv_b"),
               gate_a_w=step_big(2, gate_a_w, m_gate_a_w, v_gate_a_w), gate_a_b=step(gate_a_b, g_ba, m_gate_a_b, v_gate_a_b, "gate_a_b"),
               gate_x_w=step_big(3, gate_x_w, m_gate_x_w, v_gate_x_w), gate_x_b=step(gate_x_b, g_bx, m_gate_x_b, v_gate_x_b, "gate_x_b"),
               lru_lambda=step(lru_lambda, g_lam, m_lru_lambda, v_lru_lambda, "lru_lambda"), w_out=step_big(4, w_out, m_w_out, v_w_out),
               norm2_g=step(norm2_g, g_n2, m_norm2_g, v_norm2_g, "norm2_g"), mlp_w1=step_big(5, mlp_w1, m_mlp_w1, v_mlp_w1),
               mlp_w2=step_big(6, mlp_w2, m_mlp_w2, v_mlp_w2), final_g=step(final_g, g_fin, m_final_g, v_final_g, "final_g"))
    order = list(res)
    return (loss, grad_x[None], *[res[n][0] for n in order], *[res[n][1] for n in order], *[res[n][2] for n in order],
            *[res[n][3] for n in order])
```

```python
import functools

import jax
import jax.numpy as jnp
from jax import lax
from jax.experimental import pallas as pl
from jax.experimental.pallas import tpu as pltpu

F32 = jnp.float32
BF16 = jnp.bfloat16
MESH = pl.DeviceIdType.MESH

NORM_EPS = 1e-6
N_META = 16
HEAD_DIM = 256
POOL_WINDOWS = (2, 4, 8, 16)
LRU_C = 8.0
ROW_TILE = 128
HIST = 16
VMEM_LIMIT_BYTES = 56 * 1024 * 1024
ADAM_LR, ADAM_B1, ADAM_B2, ADAM_EPS, ADAM_WD, ADAM_STEP = 0.001, 0.9, 0.999, 1e-08, 0.01, 10


def _pick(n, prefs):
    for p in prefs:
        if n % p == 0:
            return p
    return n


def _params(*sem):
    return pltpu.CompilerParams(dimension_semantics=sem, vmem_limit_bytes=VMEM_LIMIT_BYTES)


def _sigmoid(x):
    return 1.0 / (1.0 + jnp.exp(-x))


def _gelu_tanh(x):
    t = jnp.tanh(0.7978845608028654 * (x + 0.044715 * (x * x * x)))
    return 0.5 * x * (1.0 + t), t


def _gelu_tanh_grad(x, t):
    return 0.5 * (1.0 + t) + 0.5 * x * (1.0 - t * t) * (0.7978845608028654 * (1.0 + 3.0 * 0.044715 * x * x))


def _neg_expm1(x, exp_x):
    series = x * (-1.0 + x * (-0.5 + x * ((-1.0 / 6.0) + x * ((-1.0 / 24.0) + x * (-1.0 / 120.0)))))
    return jnp.where(x > -0.125, series, 1.0 - exp_x)


def _softplus_neg(lam):
    z = jnp.exp(-jnp.abs(lam))
    log1p_z = jnp.where(z < 0.01, z * (1.0 - z * (0.5 - z * (1.0 / 3.0))), jnp.log(1.0 + z))
    return jnp.maximum(-lam, 0.0) + log1p_z


def _tile_masks(is_meta, rows):
    row = lax.broadcasted_iota(jnp.int32, (rows, 1), 0)
    valid = jnp.logical_or(jnp.logical_not(is_meta), row >= rows - N_META)
    t_log = jnp.where(is_meta, row - (rows - N_META), 1 << 20)
    return row, valid, t_log


def _window_count_inv(t_log, w):
    return 1.0 / jnp.clip(t_log + 1, 1, w).astype(F32)


def _dot_nt(a, b):
    return lax.dot_general(a, b, (((1,), (1,)), ((), ())), preferred_element_type=F32)


def _dot_tn(a, b):
    return lax.dot_general(a, b, (((0,), (0,)), ((), ())), preferred_element_type=F32)


def _place():
    x, y, c = lax.axis_index("x"), lax.axis_index("y"), lax.axis_index("c")
    chips = [(1 - x, y), (x, 1 - y), (1 - x, 1 - y)]
    return x, y, c, chips


def _remote(src, dst, send_sem, recv_sem, to):
    return pltpu.make_async_remote_copy(src_ref=src, dst_ref=dst, send_sem=send_sem, recv_sem=recv_sem, device_id=to,
                                        device_id_type=MESH)


class Stage:
    def __init__(self, arrays, out_shapes, aliases, n_copies, copies):
        self.arrays, self.out_shapes, self.aliases, self.n_copies, self.copies = list(arrays), list(out_shapes), aliases, n_copies, copies


def _sds(a):
    return jax.ShapeDtypeStruct(a.shape, a.dtype)


def _relay_rows(buf_ref, chip_idx, c, quarter):
    rows = buf_ref.shape[2] // 2
    return buf_ref.at[chip_idx, c, pl.ds(quarter * rows, rows)]


def stage_gather_direct(bufs, quarter=None):
    n = len(bufs)

    def copies(ins, outs, send, recv):
        x, y, c, chips = _place()
        me = 2 * x + y
        sends, recvs = [], []
        for t in range(n):
            for j, (px, py) in enumerate(chips[:2]):
                k = 2 * t + j
                if quarter is None:
                    mine, theirs = outs[t].at[me, c], outs[t].at[2 * px + py, c]
                else:
                    mine, theirs = _relay_rows(outs[t], me, c, quarter), _relay_rows(outs[t], 2 * px + py, c, quarter)
                sends.append(_remote(mine, mine, send.at[k], recv.at[k], (px, py, c)))
                recvs.append(_remote(theirs, theirs, send.at[k], recv.at[k], (px, py, c)))
        return sends, recvs

    return Stage(bufs, [_sds(b) for b in bufs], {t: t for t in range(n)}, 2 * n, copies)


def stage_gather_relay(bufs):
    n = len(bufs)

    def copies(ins, outs, send, recv):
        x, y, c, chips = _place()
        (xx, xy), (yx, yy), (dx, dy) = chips
        sends, recvs = [], []
        for t in range(n):
            from_y, from_x = _relay_rows(outs[t], 2 * yx + yy, c, 0), _relay_rows(outs[t], 2 * xx + xy, c, 1)
            sends.append(_remote(from_y, from_y, send.at[2 * t], recv.at[2 * t], (xx, xy, c)))
            sends.append(_remote(from_x, from_x, send.at[2 * t + 1], recv.at[2 * t + 1], (yx, yy, c)))
            for q, (px, py) in enumerate(chips[:2]):
                got = _relay_rows(outs[t], 2 * dx + dy, c, q)
                recvs.append(_remote(got, got, send.at[2 * t + q], recv.at[2 * t + q], (px, py, c)))
        return sends, recvs

    return Stage(bufs, [_sds(b) for b in bufs], {t: t for t in range(n)}, 2 * n, copies)


def stage_gather_d2d(bufs, peers=(0, 1, 2)):
    n = len(bufs)

    def copies(ins, outs, send, recv):
        x, y, c, chips = _place()
        sends, recvs = [], []
        for t in range(n):
            for slot, j in enumerate(peers):
                px, py = chips[j]
                k = len(peers) * t + slot
                got, sib = outs[t].at[2 * px + py, c], outs[t].at[2 * px + py, 1 - c]
                sends.append(_remote(got, got, send.at[k], recv.at[k], (x, y, 1 - c)))
                recvs.append(_remote(sib, sib, send.at[k], recv.at[k], (x, y, 1 - c)))
        return sends, recvs

    return Stage(bufs, [_sds(b) for b in bufs], {t: t for t in range(n)}, len(peers) * n, copies)


class _SemsFrom:
    def __init__(self, sems, first):
        self.sems, self.first = sems, first

    @property
    def at(self):
        return self

    def __getitem__(self, k):
        return self.sems.at[self.first + k]


def stage_both(first, second):
    def copies(ins, outs, send, recv):
        s1, r1 = first.copies(ins, outs, send, recv)
        s2, r2 = second.copies(ins, outs, _SemsFrom(send, first.n_copies), _SemsFrom(recv, first.n_copies))
        return s1 + s2, r1 + r2

    return Stage(first.arrays, first.out_shapes, first.aliases, first.n_copies + second.n_copies, copies)


def stage_to_sibling(grads):
    n = len(grads)

    def copies(ins, outs, send, recv):
        x, y, c, _ = _place()
        sends, recvs = [], []
        for t in range(n):
            for k4 in range(4):
                k = 4 * t + k4
                sends.append(_remote(ins[t].at[k4, 1 - c], outs[t].at[k4], send.at[k], recv.at[k], (x, y, 1 - c)))
                recvs.append(_remote(outs[t].at[k4], outs[t].at[k4], send.at[k], recv.at[k], (x, y, 1 - c)))
        return sends, recvs

    return Stage(grads, [jax.ShapeDtypeStruct((4, *g.shape[2:]), g.dtype) for g in grads], {}, 4 * n, copies)


def stage_to_chips(presums, peers=(0, 1, 2), into=None):
    n = len(presums)

    def copies(ins, outs, send, recv):
        x, y, c, chips = _place()
        sends, recvs = [], []
        for t in range(n):
            for slot, j in enumerate(peers):
                px, py = chips[j]
                k = len(peers) * t + slot
                sends.append(_remote(ins[t].at[2 * px + py], outs[t].at[j], send.at[k], recv.at[k], (px, py, c)))
                recvs.append(_remote(outs[t].at[j], outs[t].at[j], send.at[k], recv.at[k], (px, py, c)))
        return sends, recvs

    out_shapes = [jax.ShapeDtypeStruct((3, *p.shape[1:]), p.dtype) for p in presums]
    if into is None:
        return Stage(presums, out_shapes, {}, len(peers) * n, copies)
    return Stage([*presums, *into], out_shapes, {n + t: t for t in range(n)}, len(peers) * n, copies)


def stage_from_sibling(halves):
    n = len(halves)

    def copies(ins, outs, send, recv):
        x, y, c, _ = _place()
        sends = [_remote(ins[t], outs[t], send.at[t], recv.at[t], (x, y, 1 - c)) for t in range(n)]
        recvs = [_remote(outs[t], outs[t], send.at[t], recv.at[t], (x, y, 1 - c)) for t in range(n)]
        return sends, recvs

    return Stage(halves, [_sds(h) for h in halves], {}, n, copies)


def _any_specs(n):
    return [pl.BlockSpec(memory_space=pl.ANY)] * n


def _staged_call(body, *, grid, in_specs, out_specs, out_shape, scratch_shapes, name, semantics, inputs, stages=()):
    n_in, n_out, n_scr = len(in_specs), len(out_specs), len(scratch_shapes)
    st_arrays = [a for s in stages for a in s.arrays]
    st_outs = [o for s in stages for o in s.out_shapes]
    st_sems = [pltpu.SemaphoreType.DMA((s.n_copies,)) for s in stages for _ in range(2)]
    aliases = {}
    at_in, at_out = n_in, n_out
    for s in stages:
        for a, o in s.aliases.items():
            aliases[at_in + a] = at_out + o
        at_in += len(s.arrays)
        at_out += len(s.out_shapes)

    def full_body(*refs):
        pos = 0

        def take(count):
            nonlocal pos
            part = refs[pos:pos + count]
            pos += count
            return part

        ins, s_ins, outs, s_outs, scr = take(n_in), take(len(st_arrays)), take(n_out), take(len(st_outs)), take(n_scr)
        s_sems = refs[pos:]

        def each_stage(action):
            at_i = at_o = 0
            for idx, s in enumerate(stages):
                sends, recvs = s.copies(s_ins[at_i:at_i + len(s.arrays)], s_outs[at_o:at_o + len(s.out_shapes)],
                                        s_sems[2 * idx], s_sems[2 * idx + 1])
                action(sends, recvs)
                at_i += len(s.arrays)
                at_o += len(s.out_shapes)

        if stages:
            ids = [pl.program_id(a) for a in range(len(grid))]
            first = functools.reduce(jnp.logical_and, [i == 0 for i in ids])
            last = functools.reduce(jnp.logical_and, [i == g - 1 for i, g in zip(ids, grid)])

            def start(sends, recvs):
                for cp in sends:
                    cp.start()

            def finish(sends, recvs):
                for cp in recvs:
                    cp.wait_recv()
                for cp in sends:
                    cp.wait_send()

            @pl.when(first)
            def _():
                each_stage(start)

        body(*ins, *outs, *scr)

        if stages:
            @pl.when(last)
            def _():
                each_stage(finish)

    sem = tuple("arbitrary" for _ in grid) if stages else tuple(semantics)
    res = pl.pallas_call(
        full_body, grid=grid, in_specs=[*in_specs, *_any_specs(len(st_arrays))], out_specs=[*out_specs, *_any_specs(len(st_outs))],
        out_shape=[*out_shape, *st_outs], scratch_shapes=[*scratch_shapes, *st_sems], input_output_aliases=aliases, name=name,
        compiler_params=_params(*sem))(*inputs, *st_arrays)
    outs, rest = list(res[:n_out]), list(res[n_out:])
    per_stage = []
    for s in stages:
        per_stage.append(rest[:len(s.out_shapes)])
        rest = rest[len(s.out_shapes):]
    return outs, per_stage


def comm_call(stages, *, name):
    return _staged_call(lambda: None, grid=(1,), in_specs=[], out_specs=[], out_shape=[], scratch_shapes=[], name=name,
                        semantics=("arbitrary",), inputs=[], stages=stages)[1]


def _matmul(kind, a, b, *, grid, a_spec, b_spec, out_spec, out_shape, acc_shape, name,
            a_pro=None, epilogue=None, extras=(), extra_specs=(), stages=()):
    nk = grid[2]
    n_extra = len(extras)

    def body(a_ref, b_ref, *rest):
        extra_refs = rest[:n_extra]
        o_ref = rest[n_extra]
        av = a_ref[...]
        if a_pro is not None:
            av = a_pro(av)
        av = av.astype(BF16)
        bv = b_ref[...].astype(BF16)
        if kind == "nn":
            p = jnp.dot(av, bv, preferred_element_type=F32)
        elif kind == "nt":
            p = _dot_nt(av, bv)
        else:
            p = _dot_tn(av, bv)

        def finish(r):
            if epilogue is not None:
                r = epilogue(r, *[e[...] for e in extra_refs])
            o_ref[...] = r.astype(o_ref.dtype)

        if nk == 1:
            finish(p)
        else:
            acc_ref = rest[n_extra + 1]
            k = pl.program_id(2)

            @pl.when(k == 0)
            def _():
                acc_ref[...] = p

            @pl.when(k > 0)
            def _():
                acc_ref[...] += p

            @pl.when(k == nk - 1)
            def _():
                finish(acc_ref[...])

    scratch = [] if nk == 1 else [pltpu.VMEM(acc_shape, F32)]
    outs, staged = _staged_call(body, grid=grid, in_specs=[a_spec, b_spec, *extra_specs], out_specs=[out_spec], out_shape=[out_shape],
                                scratch_shapes=scratch, name=name, semantics=("parallel", "parallel", "arbitrary"),
                                inputs=[a, b, *extras], stages=stages)
    return (outs[0], staged) if stages else outs[0]


def mm_nn(a, b, *, out_dtype, name, a_pro=None, epilogue=None, extras=(), stages=(), out_shards=1, tiles=(None, None, None)):
    M, K = a.shape
    sharded = b.ndim == 3
    ns = b.shape[2] if sharded else b.shape[1]
    N = ns * b.shape[0] if sharded else ns
    if out_shards > 1:
        ns = N // out_shards
    tm = tiles[0] or _pick(M, (1408, 1024, 512, 256, 128))
    tn = tiles[1] or _pick(ns, (512, 256, 128))
    tk = tiles[2] or _pick(K, (2048, 1408, 1024, 512, 256, 128))
    per = ns // tn
    if sharded:
        b_spec = pl.BlockSpec((None, tk, tn), lambda i, j, k: (j // per, k, j % per))
    else:
        b_spec = pl.BlockSpec((tk, tn), lambda i, j, k: (k, j))
    mn = pl.BlockSpec((tm, tn), lambda i, j, k: (i, j))
    if out_shards > 1:
        out_spec = pl.BlockSpec((None, tm, tn), lambda i, j, k: (j // per, i, j % per))
        out_shape = jax.ShapeDtypeStruct((out_shards, M, ns), out_dtype)
    else:
        out_spec, out_shape = mn, jax.ShapeDtypeStruct((M, N), out_dtype)
    return _matmul("nn", a, b, grid=(M // tm, N // tn, K // tk), a_spec=pl.BlockSpec((tm, tk), lambda i, j, k: (i, k)),
                   b_spec=b_spec, out_spec=out_spec, out_shape=out_shape, acc_shape=(tm, tn),
                   name=name, a_pro=a_pro, epilogue=epilogue, extras=extras, extra_specs=[mn] * len(extras), stages=stages)


def mm_nt(a, w, *, out_dtype, name, epilogue=None, extras=(), stages=(), tiles=(None, None, None)):
    M, N = a.shape
    sharded = w.ndim == 3
    kw = w.shape[1] if sharded else w.shape[0]
    ns = w.shape[2] if sharded else w.shape[1]
    tm = tiles[0] or _pick(M, (1408, 512, 256, 128))
    tkw = tiles[1] or _pick(kw, (1024, 512, 256, 128))
    tk = tiles[2] or _pick(ns, (2048, 1280, 1024, 512, 256, 128))
    per = ns // tk
    if sharded:
        w_spec = pl.BlockSpec((None, tkw, tk), lambda i, j, k: (k // per, j, k % per))
    else:
        w_spec = pl.BlockSpec((tkw, tk), lambda i, j, k: (j, k))
    mo = pl.BlockSpec((tm, tkw), lambda i, j, k: (i, j))
    return _matmul("nt", a, w, grid=(M // tm, kw // tkw, N // tk), a_spec=pl.BlockSpec((tm, tk), lambda i, j, k: (i, k)),
                   b_spec=w_spec, out_spec=mo, out_shape=jax.ShapeDtypeStruct((M, kw), out_dtype), acc_shape=(tm, tkw),
                   name=name, epilogue=epilogue, extras=extras, extra_specs=[mo] * len(extras), stages=stages)


def mm_tn(a, g, *, shards, name, out_dtype, a_pro=None, stages=(), tiles=(None, None, None)):
    T, kw = a.shape
    N = g.shape[1]
    ns = N // shards
    tkw = tiles[0] or _pick(kw, (1024, 512, 256, 128))
    tn = tiles[1] or _pick(ns, (1280, 1024, 512, 256, 128))
    tt = tiles[2] or _pick(T, (1408, 512, 256, 128))
    per = ns // tn
    if shards > 1:
        out_spec = pl.BlockSpec((None, tkw, tn), lambda i, j, k: (j // per, i, j % per))
        out_shape = jax.ShapeDtypeStruct((shards, kw, ns), out_dtype)
    else:
        out_spec = pl.BlockSpec((tkw, tn), lambda i, j, k: (i, j))
        out_shape = jax.ShapeDtypeStruct((kw, N), out_dtype)
    return _matmul("tn", a, g, grid=(kw // tkw, N // tn, T // tt), a_spec=pl.BlockSpec((tt, tkw), lambda i, j, k: (k, i)),
                   b_spec=pl.BlockSpec((tt, tn), lambda i, j, k: (k, j)), out_spec=out_spec, out_shape=out_shape,
                   acc_shape=(tkw, tn), name=name, a_pro=a_pro, stages=stages)


def _relu_sq(a):
    r = jnp.maximum(a, 0.0)
    return r * r


def rmsnorm_fwd(h, g, *, name, stages=()):
    Tp, D = h.shape
    tr = _pick(Tp, (384, 256, 128))

    def body(h_ref, g_ref, u_ref, ut_ref):
        x = h_ref[...]
        r = lax.rsqrt(jnp.mean(x * x, axis=-1, keepdims=True) + NORM_EPS)
        u = (x * r) * g_ref[...]
        u_ref[...] = u.astype(BF16)
        ut_ref[...] = u.T.astype(BF16)

    row = pl.BlockSpec((tr, D), lambda i: (i, 0))
    outs, staged = _staged_call(body, grid=(Tp // tr,), in_specs=[row, pl.BlockSpec((1, D), lambda i: (0, 0))],
                                out_specs=[row, pl.BlockSpec((D, tr), lambda i: (0, i))],
                                out_shape=[jax.ShapeDtypeStruct((Tp, D), BF16), jax.ShapeDtypeStruct((D, Tp), BF16)],
                                scratch_shapes=[], name=name, semantics=("parallel",), inputs=[h, g], stages=stages)
    return (outs, staged) if stages else outs


def rmsnorm_fwd_input(x, meta_tile, g, *, name, stages=()):
    S, D = x.shape
    nt = S // ROW_TILE + 1

    def body(x_ref, mt_ref, g_ref, h_ref, u_ref, ut_mine_ref, ut_theirs_ref):
        h = jnp.where(pl.program_id(0) == nt - 1, mt_ref[...], x_ref[...])
        r = lax.rsqrt(jnp.mean(h * h, axis=-1, keepdims=True) + NORM_EPS)
        u = (h * r) * g_ref[...]
        h_ref[...] = h
        u_ref[...] = u.astype(BF16)
        ut = u.T.astype(BF16)
        low, high = ut[:D // 2, :], ut[D // 2:, :]
        mine_is_high = lax.axis_index("c") == 1
        ut_mine_ref[...] = jnp.where(mine_is_high, high, low)
        ut_theirs_ref[...] = jnp.where(mine_is_high, low, high)

    row = pl.BlockSpec((ROW_TILE, D), lambda i: (i, 0))
    outs, staged = _staged_call(
        body, grid=(nt,),
        in_specs=[pl.BlockSpec((ROW_TILE, D), lambda i: (jnp.minimum(i, nt - 2), 0)), pl.BlockSpec((ROW_TILE, D), lambda i: (0, 0)),
                  pl.BlockSpec((1, D), lambda i: (0, 0))],
        out_specs=[row, row, pl.BlockSpec((D // 2, ROW_TILE), lambda i: (0, i)), pl.BlockSpec((D // 2, ROW_TILE), lambda i: (0, i))],
        out_shape=[jax.ShapeDtypeStruct((S + ROW_TILE, D), F32), jax.ShapeDtypeStruct((S + ROW_TILE, D), BF16),
                   jax.ShapeDtypeStruct((D // 2, S + ROW_TILE), BF16), jax.ShapeDtypeStruct((D // 2, S + ROW_TILE), BF16)],
        scratch_shapes=[], name=name, semantics=("parallel",), inputs=[x, meta_tile, g], stages=stages)
    return (outs, staged) if stages else outs


def _rms_bwd_math(x, g, dy):
    r = lax.rsqrt(jnp.mean(x * x, axis=-1, keepdims=True) + NORM_EPS)
    xh = x * r
    dyg = dy * g
    dx = r * (dyg - xh * jnp.mean(dyg * xh, axis=-1, keepdims=True))
    return dx, xh


def final_loss(h2, target, gf, *, name):
    Tp, D = h2.shape
    nt = Tp // ROW_TILE

    def body(h_ref, t_ref, g_ref, dhb_ref, st_ref):
        i = pl.program_id(0)

        @pl.when(i == 0)
        def _():
            st_ref[...] = jnp.zeros_like(st_ref)

        x = h_ref[...]
        g = g_ref[...]
        r = lax.rsqrt(jnp.mean(x * x, axis=-1, keepdims=True) + NORM_EPS)
        xh = x * r
        err = jnp.where(i == nt - 1, 0.0, xh * g - t_ref[...])
        dout = err * (1.0 / D)
        dyg = dout * g
        dx = r * (dyg - xh * jnp.mean(dyg * xh, axis=-1, keepdims=True))
        dhb_ref[...] = dx.astype(BF16)
        st_ref[0:1, :] += jnp.sum(dout * xh, axis=0, keepdims=True)
        st_ref[1:2, :] += jnp.sum(err * err, axis=0, keepdims=True) * (0.5 / D)

    row = pl.BlockSpec((ROW_TILE, D), lambda i: (i, 0))
    return pl.pallas_call(
        body, grid=(nt,),
        in_specs=[row, pl.BlockSpec((ROW_TILE, D), lambda i: (jnp.minimum(i, nt - 2), 0)), pl.BlockSpec((1, D), lambda i: (0, 0))],
        out_specs=[row, pl.BlockSpec((8, D), lambda i: (0, 0))],
        out_shape=[jax.ShapeDtypeStruct((Tp, D), BF16), jax.ShapeDtypeStruct((8, D), F32)],
        name=name, compiler_params=_params("arbitrary"))(h2, target, gf)


def rms_bwd(h, g, du, dres, *, name):
    Tp, D = h.shape
    tr = _pick(Tp, (384, 256, 128))

    def body(h_ref, g_ref, du_ref, dr_ref, dhb_ref, st_ref):
        @pl.when(pl.program_id(0) == 0)
        def _():
            st_ref[...] = jnp.zeros_like(st_ref)

        du_v = du_ref[...].astype(F32)
        dx, xh = _rms_bwd_math(h_ref[...], g_ref[...], du_v)
        dhb_ref[...] = (dr_ref[...].astype(F32) + dx).astype(BF16)
        st_ref[0:1, :] += jnp.sum(du_v * xh, axis=0, keepdims=True)

    row = pl.BlockSpec((tr, D), lambda i: (i, 0))
    return pl.pallas_call(
        body, grid=(Tp // tr,), in_specs=[row, pl.BlockSpec((1, D), lambda i: (0, 0)), row, row],
        out_specs=[row, pl.BlockSpec((8, D), lambda i: (0, 0))],
        out_shape=[jax.ShapeDtypeStruct((Tp, D), BF16), jax.ShapeDtypeStruct((8, D), F32)],
        name=name, compiler_params=_params("arbitrary"))(h, g, du, dres)


def rms_bwd_input(h, g, du, dres, *, name, stages=()):
    Tp, D = h.shape
    nt = Tp // ROW_TILE

    def body(h_ref, g_ref, du_ref, dr_ref, gx_ref, gm_ref, st_ref):
        i = pl.program_id(0)

        @pl.when(i == 0)
        def _():
            st_ref[...] = jnp.zeros_like(st_ref)

        du_v = du_ref[...].astype(F32)
        dx, xh = _rms_bwd_math(h_ref[...], g_ref[...], du_v)
        dh = dr_ref[...].astype(F32) + dx
        st_ref[0:1, :] += jnp.sum(du_v * xh, axis=0, keepdims=True)

        @pl.when(i < nt - 1)
        def _():
            gx_ref[...] = dh

        @pl.when(i == nt - 1)
        def _():
            gm_ref[...] = dh[ROW_TILE - N_META:, :]

    row = pl.BlockSpec((ROW_TILE, D), lambda i: (i, 0))
    outs, staged = _staged_call(
        body, grid=(nt,), in_specs=[row, pl.BlockSpec((1, D), lambda i: (0, 0)), row, row],
        out_specs=[pl.BlockSpec((ROW_TILE, D), lambda i: (jnp.minimum(i, nt - 2), 0)), pl.BlockSpec((N_META, D), lambda i: (0, 0)),
                   pl.BlockSpec((8, D), lambda i: (0, 0))],
        out_shape=[jax.ShapeDtypeStruct((Tp - ROW_TILE, D), F32), jax.ShapeDtypeStruct((N_META, D), F32),
                   jax.ShapeDtypeStruct((8, D), F32)],
        scratch_shapes=[], name=name, semantics=("arbitrary",), inputs=[h, g, du, dres], stages=stages)
    return (outs, staged) if stages else outs


def _conv_shifts(ext):
    return tuple(pltpu.roll(ext, k, 0)[HIST:, :] for k in (1, 2, 3))


def _conv_taps(ext, cur, vec_ref, cs):
    x1, x2, x3 = _conv_shifts(ext)
    return vec_ref[1:2, cs] + vec_ref[8:9, cs] * cur + vec_ref[7:8, cs] * x1 + vec_ref[6:7, cs] * x2 + vec_ref[5:6, cs] * x3


def _window_sum_back(ext, w):
    s, sh = ext, 1
    while sh < w:
        s = s + pltpu.roll(s, sh, 0)
        sh *= 2
    return s[HIST:, :]


def _scan_rows(a, b, carry, *, reverse):
    rows = a.shape[0]
    rin = jnp.bitwise_and(lax.broadcasted_iota(jnp.int32, (rows, 1), 0), 7)
    sh = 1
    while sh < 8:
        keep = (rin < 8 - sh) if reverse else (rin >= sh)
        amount = rows - sh if reverse else sh
        a_sh = jnp.where(keep, pltpu.roll(a, amount, 0), 1.0)
        b_sh = jnp.where(keep, pltpu.roll(b, amount, 0), 0.0)
        b = b + a * b_sh
        a = a * a_sh
        sh *= 2
    out = [None] * (rows // 8)
    for g in (reversed(range(rows // 8)) if reverse else range(rows // 8)):
        hg = b[8 * g:8 * g + 8, :] + a[8 * g:8 * g + 8, :] * carry
        carry = hg[0:1, :] if reverse else hg[7:8, :]
        out[g] = hg
    return jnp.concatenate(out, axis=0)


def _lru_gates(xc, wa, wx, vec_ref, cs, sp):
    xcb = xc.astype(BF16)
    r = _sigmoid(jnp.dot(xcb, wa, preferred_element_type=F32) + vec_ref[2:3, cs])
    ig = _sigmoid(jnp.dot(xcb, wx, preferred_element_type=F32) + vec_ref[3:4, cs])
    log_a = (-LRU_C) * r * sp
    a = jnp.exp(log_a)
    a2 = a * a
    return xcb, r, ig, a, a2, _neg_expm1(2.0 * log_a, a2)


def mix_fwd(proj, pool_w, gate_a, gate_x, vecs, *, name, stages=()):
    Tp = proj.shape[0]
    D = proj.shape[1] // 5
    R = ROW_TILE
    nt = Tp // R
    H = D // HEAD_DIM
    PG = D // len(POOL_WINDOWS)

    def body(p_ref, pw_ref, wa_ref, wx_ref, vec_ref, m_ref, hs_ref, xc_ref, r_ref, ig_ref, a_ref, mu_ref, ge_ref, dge_ref,
             hist_p, hist_l, hcar, mtmp):
        i = pl.program_id(0)
        is_meta = i == 0

        @pl.when(is_meta)
        def _():
            hist_p[...] = jnp.zeros_like(hist_p)
            hist_l[...] = jnp.zeros_like(hist_l)
            hcar[...] = jnp.zeros_like(hcar)

        row, valid, t_log = _tile_masks(is_meta, R)

        for g, w in enumerate(POOL_WINDOWS):
            cs = slice(g * PG, (g + 1) * PG)
            v = p_ref[:, g * PG:(g + 1) * PG].astype(F32)
            ws = _window_sum_back(jnp.concatenate([hist_p[:, cs], v], axis=0), w)
            d = ws * _window_count_inv(t_log, w) - v
            y = jnp.dot(d.astype(BF16), pw_ref[g], preferred_element_type=F32)
            gp = p_ref[:, 3 * D + g * PG:3 * D + (g + 1) * PG].astype(F32)
            mtmp[:, cs] = _sigmoid(gp) * (y * vec_ref[0:1, cs])
            hist_p[:, cs] = v[R - HIST:, :]

        for h in range(H):
            cs = slice(h * HEAD_DIM, (h + 1) * HEAD_DIM)
            vl = p_ref[:, D + h * HEAD_DIM:D + (h + 1) * HEAD_DIM].astype(F32)
            xc = _conv_taps(jnp.concatenate([hist_l[:, cs], vl], axis=0), vl, vec_ref, cs)
            sp = _softplus_neg(vec_ref[4:5, cs])
            xcb, r, ig, a, _, em = _lru_gates(xc, wa_ref[h], wx_ref[h], vec_ref, cs, sp)
            mult = jnp.sqrt(em)
            b = jnp.where(valid, mult * (ig * xc), 0.0)
            hs = _scan_rows(a, b, hcar[7:8, cs], reverse=False)
            hs_ref[:, cs] = hs
            xc_ref[:, cs], r_ref[:, cs], ig_ref[:, cs] = xcb, r.astype(BF16), ig.astype(BF16)
            a_ref[:, cs], mu_ref[:, cs] = a, mult
            hcar[:, cs] = hs[R - 8:, :]
            hist_l[:, cs] = vl[R - HIST:, :]
            vg = p_ref[:, 2 * D + h * HEAD_DIM:2 * D + (h + 1) * HEAD_DIM].astype(F32)
            ge, th = _gelu_tanh(vg)
            ge_ref[:, cs], dge_ref[:, cs] = ge.astype(BF16), _gelu_tanh_grad(vg, th).astype(BF16)
            gl = p_ref[:, 4 * D + h * HEAD_DIM:4 * D + (h + 1) * HEAD_DIM].astype(F32)
            m_ref[:, cs] = (mtmp[:, cs] + _sigmoid(gl) * (hs * ge)).astype(BF16)

    def tile(i):
        return (i + nt - 1) % nt

    full = lambda shape: pl.BlockSpec(shape, lambda i: (0,) * len(shape))
    outs, staged = _staged_call(
        body, grid=(nt,),
        in_specs=[pl.BlockSpec((R, 5 * D), lambda i: (tile(i), 0)), full(pool_w.shape), full(gate_a.shape), full(gate_x.shape),
                  full(vecs.shape)],
        out_specs=[pl.BlockSpec((R, D), lambda i: (tile(i), 0))] * 9,
        out_shape=[jax.ShapeDtypeStruct((Tp, D), dt) for dt in (BF16, F32, BF16, BF16, BF16, F32, F32, BF16, BF16)],
        scratch_shapes=[pltpu.VMEM((HIST, D), F32), pltpu.VMEM((HIST, D), F32), pltpu.VMEM((8, D), F32), pltpu.VMEM((R, D), F32)],
        name=name, semantics=("arbitrary",), inputs=[proj, pool_w, gate_a, gate_x, vecs], stages=stages)
    outs = [outs[0], outs[1], outs[2:]]
    return (outs, staged) if stages else outs


def mix_bwd(proj, hs, saved, dmerged, pool_w, gate_a, gate_x, vecs, *, name, stages=()):
    Tp = proj.shape[0]
    D = proj.shape[1] // 5
    R = ROW_TILE
    nt = Tp // R
    H = D // HEAD_DIM
    PG = D // len(POOL_WINDOWS)

    def body(p_ref, pprev_ref, hs_ref, hprev_ref, dm_ref, xc_ref, r_ref, ig_ref, a_ref, mu_ref, ge_ref, dge_ref, pw_ref, wa_ref, wx_ref, vec_ref,
             dp_ref, dpw_ref, dwa_ref, dwx_ref, dvec_ref, car_g, fut_dxc, fut_q):
        i = pl.program_id(0)
        is_meta = i == nt - 1

        @pl.when(i == 0)
        def _():
            dpw_ref[...] = jnp.zeros_like(dpw_ref)
            dwa_ref[...] = jnp.zeros_like(dwa_ref)
            dwx_ref[...] = jnp.zeros_like(dwx_ref)
            dvec_ref[...] = jnp.zeros_like(dvec_ref)
            car_g[...] = jnp.zeros_like(car_g)
            fut_dxc[...] = jnp.zeros_like(fut_dxc)
            fut_q[...] = jnp.zeros_like(fut_q)

        row, valid, t_log = _tile_masks(is_meta, R)
        keep_prev = jnp.logical_not(is_meta)

        def colsum(x):
            return jnp.sum(x, axis=0, keepdims=True)

        for g, w in enumerate(POOL_WINDOWS):
            cs = slice(g * PG, (g + 1) * PG)
            v = p_ref[:, g * PG:(g + 1) * PG].astype(F32)
            vprev = jnp.where(keep_prev, pprev_ref[:, g * PG:(g + 1) * PG].astype(F32), 0.0)
            inv_cnt = _window_count_inv(t_log, w)
            d = _window_sum_back(jnp.concatenate([vprev, v], axis=0), w) * inv_cnt - v
            d_bf = d.astype(BF16)
            y = jnp.dot(d_bf, pw_ref[g], preferred_element_type=F32)
            scale = vec_ref[0:1, cs]
            sg = _sigmoid(p_ref[:, 3 * D + g * PG:3 * D + (g + 1) * PG].astype(F32))
            dm = dm_ref[:, cs].astype(F32)
            dpo = dm * sg
            dp_ref[:, 3 * D + g * PG:3 * D + (g + 1) * PG] = (dm * (y * scale) * sg * (1.0 - sg)).astype(BF16)
            dvec_ref[0:1, cs] += colsum(dpo * y)
            dy = (dpo * scale).astype(BF16)
            dd = _dot_nt(dy, pw_ref[g])
            dpw_ref[g] += _dot_tn(d_bf, dy)
            q = dd * inv_cnt
            s, sh = jnp.concatenate([q, fut_q[:, cs]], axis=0), 1
            while sh < w:
                s = s + pltpu.roll(s, R + HIST - sh, 0)
                sh *= 2
            dp_ref[:, g * PG:(g + 1) * PG] = (s[:R, :] - dd).astype(BF16)
            fut_q[:, cs] = q[:HIST, :]

        for h in range(H):
            cs = slice(h * HEAD_DIM, (h + 1) * HEAD_DIM)
            pc = lambda blk: slice(blk * D + h * HEAD_DIM, blk * D + (h + 1) * HEAD_DIM)
            vl = p_ref[:, pc(1)].astype(F32)
            vlprev = jnp.where(keep_prev, pprev_ref[:, pc(1)].astype(F32), 0.0)
            x1, x2, x3 = _conv_shifts(jnp.concatenate([vlprev, vl], axis=0))
            lam = vec_ref[4:5, cs]
            sp = _softplus_neg(lam)
            xcb = xc_ref[:, cs]
            xc, r, ig = xcb.astype(F32), r_ref[:, cs].astype(F32), ig_ref[:, cs].astype(F32)
            a, mult = a_ref[:, cs], mu_ref[:, cs]
            a2 = a * a
            inv_mult = 1.0 / mult
            hsv = hs_ref[:, cs]
            hprev = jnp.where(row >= 1, pltpu.roll(hsv, 1, 0), hprev_ref[HIST - 1:HIST, cs])
            ge = ge_ref[:, cs].astype(F32)
            sgl = _sigmoid(p_ref[:, pc(4)].astype(F32))
            dm = dm_ref[:, cs].astype(F32)
            dlo = dm * sgl
            dp_ref[:, pc(4)] = (dm * (hsv * ge) * sgl * (1.0 - sgl)).astype(BF16)
            dp_ref[:, pc(2)] = (dlo * hsv * dge_ref[:, cs].astype(F32)).astype(BF16)
            a_next = jnp.where(row < R - 1, pltpu.roll(a, R - 1, 0), 1.0)
            G = _scan_rows(a_next, dlo * ge, car_g[0:1, cs], reverse=True)
            car_g[:, cs] = (a * G)[0:8, :]
            da = jnp.where(valid, G * hprev, 0.0)
            db = jnp.where(valid, G, 0.0)
            dmult = db * (ig * xc)
            dig = db * (mult * xc)
            dxc = db * (mult * ig)
            dlog_a = da * a - dmult * (a2 * inv_mult)
            dvec_ref[4:5, cs] += colsum(dlog_a * r) * (-LRU_C)
            dr = dlog_a * ((-LRU_C) * sp)
            dpa = dr * r * (1.0 - r)
            dpx = dig * ig * (1.0 - ig)
            dpa_bf = dpa.astype(BF16)
            dpx_bf = dpx.astype(BF16)
            dwa_ref[h] += _dot_tn(xcb, dpa_bf)
            dwx_ref[h] += _dot_tn(xcb, dpx_bf)
            dvec_ref[2:3, cs] += colsum(dpa)
            dvec_ref[3:4, cs] += colsum(dpx)
            dxc = dxc + _dot_nt(dpa_bf, wa_ref[h]) + _dot_nt(dpx_bf, wx_ref[h])
            ext = jnp.concatenate([dxc, fut_dxc[:, cs]], axis=0)
            n = R + HIST
            dvl = (vec_ref[8:9, cs] * dxc + vec_ref[7:8, cs] * pltpu.roll(ext, n - 1, 0)[:R, :]
                   + vec_ref[6:7, cs] * pltpu.roll(ext, n - 2, 0)[:R, :] + vec_ref[5:6, cs] * pltpu.roll(ext, n - 3, 0)[:R, :])
            dp_ref[:, pc(1)] = dvl.astype(BF16)
            dvec_ref[1:2, cs] += colsum(dxc)
            dvec_ref[8:9, cs] += colsum(dxc * vl)
            dvec_ref[7:8, cs] += colsum(dxc * x1)
            dvec_ref[6:7, cs] += colsum(dxc * x2)
            dvec_ref[5:6, cs] += colsum(dxc * x3)
            fut_dxc[:, cs] = dxc[:HIST, :]

            @pl.when(is_meta)
            def _():
                dvec_ref[4:5, cs] = dvec_ref[4:5, cs] * (-_sigmoid(-lam))

    def tile(i):
        return (2 * nt - 2 - i) % nt

    def prev_blk(i):
        per = R // HIST
        return jnp.where(i == nt - 1, 0, jnp.where(i == nt - 2, Tp // HIST - 1, (nt - 2 - i) * per - 1))

    full = lambda shape: pl.BlockSpec(shape, lambda i: (0,) * len(shape))
    G_ = len(POOL_WINDOWS)
    outs, staged = _staged_call(
        body, grid=(nt,),
        in_specs=[pl.BlockSpec((R, 5 * D), lambda i: (tile(i), 0)), pl.BlockSpec((HIST, 5 * D), lambda i: (prev_blk(i), 0)),
                  pl.BlockSpec((R, D), lambda i: (tile(i), 0)), pl.BlockSpec((HIST, D), lambda i: (prev_blk(i), 0)),
                  *[pl.BlockSpec((R, D), lambda i: (tile(i), 0))] * 8,
                  full(pool_w.shape), full(gate_a.shape), full(gate_x.shape), full(vecs.shape)],
        out_specs=[pl.BlockSpec((R, 5 * D), lambda i: (tile(i), 0)), full((G_, PG, PG)), full((H, HEAD_DIM, HEAD_DIM)),
                   full((H, HEAD_DIM, HEAD_DIM)), full((16, D))],
        out_shape=[jax.ShapeDtypeStruct((Tp, 5 * D), BF16), jax.ShapeDtypeStruct((G_, PG, PG), F32),
                   jax.ShapeDtypeStruct((H, HEAD_DIM, HEAD_DIM), F32), jax.ShapeDtypeStruct((H, HEAD_DIM, HEAD_DIM), F32),
                   jax.ShapeDtypeStruct((16, D), F32)],
        scratch_shapes=[pltpu.VMEM((8, D), F32), pltpu.VMEM((HIST, D), F32), pltpu.VMEM((HIST, D), F32)],
        name=name, semantics=("arbitrary",), inputs=[proj, proj, hs, hs, dmerged, *saved, pool_w, gate_a, gate_x, vecs], stages=stages)
    return (outs, staged) if stages else outs


def _adamw_math(w, g, m, v):
    mn = ADAM_B1 * m + (1.0 - ADAM_B1) * g
    vn = ADAM_B2 * v + (1.0 - ADAM_B2) * (g * g)
    m_hat = mn / (1.0 - ADAM_B1 ** ADAM_STEP)
    v_hat = vn / (1.0 - ADAM_B2 ** ADAM_STEP)
    return -ADAM_LR * (m_hat / (jnp.sqrt(v_hat) + ADAM_EPS) + ADAM_WD * w), mn, vn


def adamw(w, g, m, v, *, name):
    rows, cols = w.shape
    tr = _pick(rows, (256, 128, 64, 32, 16, 8))

    def body(w_ref, g_ref, m_ref, v_ref, go_ref, d_ref, mo_ref, vo_ref):
        gv = g_ref[...]
        go_ref[...] = gv
        d_ref[...], mo_ref[...], vo_ref[...] = _adamw_math(w_ref[...], gv, m_ref[...], v_ref[...])

    blk = pl.BlockSpec((tr, cols), lambda i: (i, 0))
    sds = jax.ShapeDtypeStruct((rows, cols), F32)
    return pl.pallas_call(body, grid=(rows // tr,), in_specs=[blk] * 4, out_specs=[blk] * 4, out_shape=[sds] * 4, name=name,
                          compiler_params=_params("parallel"))(w, g, m, v)


def allgather8(block, *, name, reduce_sum=False, stages=()):
    rows, cols = block.shape

    def body(x_ref, out_ref, *scratch):
        if reduce_sum:
            buf, send_sems, recv_sems, local_sem = scratch
        else:
            buf = out_ref
            send_sems, recv_sems, local_sem = scratch
        x, y, c, chips = _place()
        me, sibling = (x, y, c), (x, y, 1 - c)

        def slot(px, py, pc):
            return buf.at[4 * px + 2 * py + pc]

        def copy(k, blk, to, src=None):
            return pltpu.make_async_remote_copy(src_ref=slot(*blk) if src is None else src, dst_ref=slot(*blk),
                                                send_sem=send_sems.at[k], recv_sem=recv_sems.at[k], device_id=to, device_id_type=MESH)

        mine = pltpu.make_async_copy(x_ref, slot(*me), local_sem)
        mine.start()
        first = [copy(0, me, sibling, src=x_ref)]
        first += [copy(1 + j, me, (*chip, c), src=x_ref) for j, chip in enumerate(chips)]
        for cp in first:
            cp.start()
        passed = [copy(4 + j, (*chip, c), sibling) for j, chip in enumerate(chips)]
        for j, chip in enumerate(chips):
            copy(1 + j, (*chip, c), me).wait_recv()
            passed[j].start()
        copy(0, sibling, me).wait_recv()
        for j, chip in enumerate(chips):
            copy(4 + j, (*chip, 1 - c), me).wait_recv()
        for cp in first + passed:
            cp.wait_send()
        mine.wait()
        if reduce_sum:
            acc = buf[0]
            for d in range(1, 8):
                acc = acc + buf[d]
            out_ref[...] = acc

    sems = [pltpu.SemaphoreType.DMA((7,)), pltpu.SemaphoreType.DMA((7,)), pltpu.SemaphoreType.DMA]
    if reduce_sum:
        out_shape = jax.ShapeDtypeStruct((rows, cols), block.dtype)
        scratch = [pltpu.VMEM((8, rows, cols), block.dtype)] + sems
    else:
        out_shape = jax.ShapeDtypeStruct((8, rows, cols), block.dtype)
        scratch = sems
    vmem = pl.BlockSpec(memory_space=pltpu.VMEM)
    outs, staged = _staged_call(body, grid=(1,), in_specs=[vmem], out_specs=[vmem], out_shape=[out_shape], scratch_shapes=scratch,
                                name=name, semantics=("arbitrary",), inputs=[block], stages=stages)
    return (outs[0], staged) if stages else outs[0]


def cast_into_slot(w, chip_arr, *, name):
    _, r, cols = w.shape
    tr = _pick(r, (256, 128, 64, 32, 16))

    def body(chip_ref, w_ref, o_ref):
        del chip_ref
        o_ref[...] = w_ref[...].astype(BF16)

    grid_spec = pltpu.PrefetchScalarGridSpec(
        num_scalar_prefetch=1, grid=(2, r // tr),
        in_specs=[pl.BlockSpec((None, tr, cols), lambda h, i, chip: (h, i, 0))],
        out_specs=pl.BlockSpec((None, None, tr, cols), lambda h, i, chip: (chip[0], h, i, 0)))
    return pl.pallas_call(body, grid_spec=grid_spec, out_shape=jax.ShapeDtypeStruct((4, 2, r, cols), BF16), name=name,
                          compiler_params=_params("parallel", "parallel"))(chip_arr, w)


def proj_with_gather(u, bufs, order_arr, *, head_start, name):
    Tp, K = u.shape
    n = len(bufs)
    Ns = bufs[0].shape[3]
    tm = _pick(Tp, (1408, 512, 256, 128))
    tc = _pick(Ns, (512, 256, 128))
    n_rows = Tp // tm

    def body(order_ref, u_ref, *refs):
        del order_ref
        o_ref, outs = refs[n], refs[n + 1:2 * n + 1]
        wbuf, dir_send, dir_recv, rel_send, rel_recv, d2d_send, d2d_recv, load_sems = refs[2 * n + 1:]
        s, i = pl.program_id(0), pl.program_id(1)
        x, y, c, chips = _place()
        me = 2 * x + y
        (dx, dy) = chips[2]

        def direct(t, j, landing):
            px, py = chips[j]
            chip_idx = 2 * px + py if landing else me
            slot = _relay_rows(outs[t], chip_idx, c, 1) if (t == 0 and head_start) else outs[t].at[chip_idx, c]
            return _remote(slot, slot, dir_send.at[t, j], dir_recv.at[t, j], (px, py, c))

        def relay(t, j, landing):
            px, py = chips[j]
            ox, oy = chips[1 - j]
            rows = _relay_rows(outs[t], 2 * dx + dy, c, j) if landing else _relay_rows(outs[t], 2 * ox + oy, c, j)
            return _remote(rows, rows, rel_send.at[t, j], rel_recv.at[t, j], (px, py, c))

        def d2d(t, j, landing):
            px, py = chips[j]
            slot = outs[t].at[2 * px + py, 1 - c] if landing else outs[t].at[2 * px + py, c]
            return _remote(slot, slot, d2d_send.at[t, j], d2d_recv.at[t, j], (x, y, 1 - c))

        def load(chip_idx, slot, start=True, wait=True):
            parts = [pltpu.make_async_copy(outs[0].at[chip_idx, hh], wbuf.at[slot, pl.ds(hh * (K // 2), K // 2), :],
                                           load_sems.at[slot, hh]) for hh in range(2)]
            for cp in parts:
                if start:
                    cp.start()
            for cp in parts:
                if wait:
                    cp.wait()

        mid, last = min(1, n_rows - 1), n_rows - 1

        def chip_of(j):
            px, py = chips[j]
            return 2 * px + py

        @pl.when(jnp.logical_and(s == 0, i == 0))
        def _():
            for t in range(n):
                for j in range(2):
                    direct(t, j, False).start()
            if head_start:
                relay(0, 0, False).start()
            load(me, 0)

        @pl.when(jnp.logical_and(s == 1, i == 0))
        def _():
            for j in range(2):
                direct(0, j, True).wait_recv()
            for j in range(2):
                if not (head_start and j == 0):
                    relay(0, j, False).start()
                d2d(0, j, False).start()
            d2d(0, 0, True).wait_recv()
            load(chip_of(0), 1)

        @pl.when(jnp.logical_and(s == 1, i == mid))
        def _():
            d2d(0, 1, True).wait_recv()
            load(chip_of(1), 0, wait=False)

        @pl.when(jnp.logical_and(s == 2, i == 0))
        def _():
            load(chip_of(1), 0, start=False)
            for t in range(1, n):
                for j in range(2):
                    direct(t, j, True).wait_recv()
                for j in range(2):
                    relay(t, j, False).start()
                    d2d(t, j, False).start()

        @pl.when(jnp.logical_and(s == 2, i == mid))
        def _():
            for j in range(2):
                relay(0, j, True).wait_recv()
            d2d(0, 2, False).start()

        @pl.when(jnp.logical_and(s == 2, i == last))
        def _():
            d2d(0, 2, True).wait_recv()
            load(chip_of(2), 1, wait=False)

        @pl.when(jnp.logical_and(s == 3, i == 0))
        def _():
            load(chip_of(2), 1, start=False)

        @pl.when(jnp.logical_and(s == 3, i == min(1, n_rows - 1)))
        def _():
            for t in range(1, n):
                for j in range(2):
                    relay(t, j, True).wait_recv()
                d2d(t, 2, False).start()

        uv = u_ref[...]
        for cc in range(Ns // tc):
            o_ref[:, cc * tc:(cc + 1) * tc] = jnp.dot(uv, wbuf[s % 2, :, cc * tc:(cc + 1) * tc], preferred_element_type=F32).astype(BF16)

        @pl.when(jnp.logical_and(s == 3, i == n_rows - 1))
        def _():
            for t in range(1, n):
                for j in range(3):
                    d2d(t, j, True).wait_recv()
            for t in range(n):
                for j in range(2):
                    direct(t, j, False).wait_send()
                    relay(t, j, False).wait_send()
                for j in range(3):
                    d2d(t, j, False).wait_send()

    grid_spec = pltpu.PrefetchScalarGridSpec(
        num_scalar_prefetch=1, grid=(4, n_rows),
        in_specs=[pl.BlockSpec((tm, K), lambda s, i, order: (i, 0)), *_any_specs(n)],
        out_specs=[pl.BlockSpec((tm, Ns), lambda s, i, order: (i, order[s])), *_any_specs(n)],
        scratch_shapes=[pltpu.VMEM((2, K, Ns), BF16), pltpu.SemaphoreType.DMA((n, 2)), pltpu.SemaphoreType.DMA((n, 2)),
                        pltpu.SemaphoreType.DMA((n, 2)), pltpu.SemaphoreType.DMA((n, 2)),
                        pltpu.SemaphoreType.DMA((n, 3)), pltpu.SemaphoreType.DMA((n, 3)), pltpu.SemaphoreType.DMA((2, 2))])
    res = pl.pallas_call(body, grid_spec=grid_spec, out_shape=[jax.ShapeDtypeStruct((Tp, 4 * Ns), BF16), *[_sds(b) for b in bufs]],
                         input_output_aliases={2 + t: 1 + t for t in range(n)}, name=name,
                         compiler_params=_params("arbitrary", "arbitrary"))(order_arr, u, *bufs)
    return res[0], list(res[1:])


def chip_presum(grad, recv, c_arr, *, name):
    _, _, r, cols = grad.shape
    tr = _pick(r, (256, 128, 64, 32, 16))

    def body(c_ref, g_ref, r_ref, o_ref):
        del c_ref
        o_ref[...] = (g_ref[...].astype(F32) + r_ref[...].astype(F32)).astype(BF16)

    grid_spec = pltpu.PrefetchScalarGridSpec(
        num_scalar_prefetch=1, grid=(4, r // tr),
        in_specs=[pl.BlockSpec((None, None, tr, cols), lambda k, i, c_ref: (k, c_ref[0], i, 0)),
                  pl.BlockSpec((None, tr, cols), lambda k, i, c_ref: (k, i, 0))],
        out_specs=pl.BlockSpec((None, tr, cols), lambda k, i, c_ref: (k, i, 0)))
    return pl.pallas_call(body, grid_spec=grid_spec, out_shape=jax.ShapeDtypeStruct((4, r, cols), BF16), name=name,
                          compiler_params=_params("parallel", "parallel"))(c_arr, grad, recv)


def final_half(grad, recv, got, mc_arr, *, name):
    _, _, r, cols = grad.shape
    tr = _pick(r, (256, 128, 64, 32, 16))

    def body(mc_ref, g_ref, r_ref, q_ref, o_ref):
        del mc_ref
        acc = g_ref[...].astype(F32) + r_ref[...].astype(F32)
        for j in range(3):
            acc = acc + q_ref[j].astype(F32)
        o_ref[...] = acc

    grid_spec = pltpu.PrefetchScalarGridSpec(
        num_scalar_prefetch=1, grid=(r // tr,),
        in_specs=[pl.BlockSpec((None, None, tr, cols), lambda i, mc: (mc[0], mc[1], i, 0)),
                  pl.BlockSpec((None, tr, cols), lambda i, mc: (mc[0], i, 0)),
                  pl.BlockSpec((3, tr, cols), lambda i, mc: (0, i, 0))],
        out_specs=pl.BlockSpec((tr, cols), lambda i, mc: (i, 0)))
    return pl.pallas_call(body, grid_spec=grid_spec, out_shape=jax.ShapeDtypeStruct((r, cols), F32), name=name,
                          compiler_params=_params("parallel"))(mc_arr, grad, recv, got)


def adamw_halves(w, mine, theirs, m, v, c_arr, *, name):
    _, r, cols = w.shape
    tr = _pick(r, (256, 128, 64, 32, 16, 8))

    def body(c_ref, w_ref, mine_ref, theirs_ref, m_ref, v_ref, go_ref, d_ref, mo_ref, vo_ref):
        gv = jnp.where(pl.program_id(0) == c_ref[0], mine_ref[...], theirs_ref[...])
        go_ref[...] = gv
        d_ref[...], mo_ref[...], vo_ref[...] = _adamw_math(w_ref[...], gv, m_ref[...], v_ref[...])

    blk = pl.BlockSpec((None, tr, cols), lambda h, i, c_ref: (h, i, 0))
    grid_spec = pltpu.PrefetchScalarGridSpec(
        num_scalar_prefetch=1, grid=(2, r // tr),
        in_specs=[blk, pl.BlockSpec((tr, cols), lambda h, i, c_ref: (jnp.where(h == c_ref[0], i, 0), 0)),
                  pl.BlockSpec((tr, cols), lambda h, i, c_ref: (jnp.where(h == c_ref[0], 0, i), 0)), blk, blk],
        out_specs=[blk] * 4)
    sds = jax.ShapeDtypeStruct((2, r, cols), F32)
    return pl.pallas_call(body, grid_spec=grid_spec, out_shape=[sds] * 4, name=name,
                          compiler_params=_params("parallel", "parallel"))(c_arr, w, mine, theirs, m, v)


def _pad_rows(a, rows):
    return jnp.pad(a, ((0, rows - a.shape[0]), (0, 0)))


def kernel(x, meta_tokens, norm1_g, w_in, pool_w, pool_scale, conv_w, conv_b, gate_a_w, gate_a_b, gate_x_w, gate_x_b, lru_lambda, w_out, norm2_g, mlp_w1, mlp_w2, final_g, loss_target, m_meta_tokens, m_norm1_g, m_w_in, m_pool_w, m_pool_scale, m_conv_w, m_conv_b, m_gate_a_w, m_gate_a_b, m_gate_x_w, m_gate_x_b, m_lru_lambda, m_w_out, m_norm2_g, m_mlp_w1, m_mlp_w2, m_final_g, v_meta_tokens, v_norm1_g, v_w_in, v_pool_w, v_pool_scale, v_conv_w, v_conv_b, v_gate_a_w, v_gate_a_b, v_gate_x_w, v_gate_x_b, v_lru_lambda, v_w_out, v_norm2_g, v_mlp_w1, v_mlp_w2, v_final_g):
    D = x.shape[-1]
    H = D // HEAD_DIM
    G = len(POOL_WINDOWS)
    PG = D // G
    ax, ay, ac = lax.axis_index("x"), lax.axis_index("y"), lax.axis_index("c")
    chip = 2 * ax + ay
    dshard = D // 4

    c_arr = jnp.reshape(ac, (1,)).astype(jnp.int32)
    chip_arr = jnp.reshape(chip, (1,)).astype(jnp.int32)
    mc_arr = jnp.stack([chip, ac]).astype(jnp.int32)
    names = ["w_in", "pool_w", "gate_a_w", "gate_x_w", "w_out", "mlp_w1", "mlp_w2"]
    big = [w_in, pool_w, gate_a_w, gate_x_w, w_out, mlp_w1, mlp_w2]

    def halves(w):
        w2d = w.reshape(-1, w.shape[-1])
        return w2d.reshape(2, w2d.shape[0] // 2, w2d.shape[1])

    bufs = [cast_into_slot(halves(w), chip_arr, name="cast_" + nm) for w, nm in zip(big, names)]
    order_arr = jnp.stack([chip, 2 * (1 - ax) + ay, 2 * ax + (1 - ay), 2 * (1 - ax) + (1 - ay)]).astype(jnp.int32)

    small_in = jnp.concatenate([meta_tokens, _pad_rows(conv_w[0], 8), _pad_rows(gate_a_b.reshape(1, dshard), 8),
                                _pad_rows(gate_x_b.reshape(1, dshard), 8)], axis=0)
    sm = allgather8(small_in, name="gather_small")[0::2]
    meta_f = sm[:, 0:16].transpose(1, 0, 2).reshape(N_META, D)
    conv_w_f = sm[:, 16:20].transpose(1, 0, 2).reshape(4, D)
    hd4 = HEAD_DIM // 4
    ba_f = sm[:, 24].reshape(4, H, hd4).transpose(1, 0, 2).reshape(1, D)
    bx_f = sm[:, 32].reshape(4, H, hd4).transpose(1, 0, 2).reshape(1, D)
    vecs = jnp.zeros((16, D), F32)
    for r0, part in ((0, pool_scale), (1, conv_b), (2, ba_f), (3, bx_f), (4, lru_lambda), (5, conv_w_f)):
        vecs = lax.dynamic_update_slice(vecs, part, (r0, 0))

    def chipwise(g, n_blocks, rows):
        return g.reshape(n_blocks, 4, rows, g.shape[-1]).transpose(1, 0, 2, 3).reshape(4, n_blocks * rows, g.shape[-1])

    def split2(g):
        return g.reshape(4, 2, g.shape[1] // 2, g.shape[2])

    def presum(t, g, r):
        return chip_presum(g, r, c_arr, name="presum_" + names[t])

    def total(t, g, r, q):
        return final_half(g, r, q, mc_arr, name="sum_" + names[t])

    xs, target, gfin = x[0], loss_target[0], final_g.reshape(1, D)
    meta_tile = jnp.concatenate([jnp.zeros((ROW_TILE - N_META, D), F32), meta_f], axis=0)
    (h0, u, u_mine, u_theirs), [[b_win]] = rmsnorm_fwd_input(xs, meta_tile, norm1_g, name="norm1",
                                                stages=[stage_gather_direct([bufs[0]], quarter=0)])
    proj, (b_win, b_pool, b_ga, b_gx, b_wout) = proj_with_gather(u, [b_win, *bufs[1:5]], order_arr, head_start=True, name="proj")
    w_in_f = b_win.reshape(4, w_in.shape[1], w_in.shape[2])
    pool_f = b_pool.reshape(4, G, PG // 4, PG).transpose(1, 0, 2, 3).reshape(G, PG, PG)
    ga_f = b_ga.reshape(4, H, HEAD_DIM // 4, HEAD_DIM).transpose(1, 0, 2, 3).reshape(H, HEAD_DIM, HEAD_DIM)
    gx_f = b_gx.reshape(4, H, HEAD_DIM // 4, HEAD_DIM).transpose(1, 0, 2, 3).reshape(H, HEAD_DIM, HEAD_DIM)
    w_out_f = b_wout.reshape(4 * w_out.shape[1], w_out.shape[2])
    (merged, hs, saved), [[b_w1, b_w2]] = mix_fwd(proj, pool_f, ga_f, gx_f, vecs, name="mix_fwd",
                                                  stages=[stage_gather_direct([bufs[5], bufs[6]])])
    h1, [[b_w1]] = mm_nn(merged, w_out_f, out_dtype=F32, name="out_proj", epilogue=lambda r, res: r + res, extras=(h0,),
                         stages=[stage_both(stage_gather_relay([b_w1]), stage_gather_d2d([b_w1], peers=(0, 1)))])
    (u2, u2_t), [[b_w1]] = rmsnorm_fwd(h1, norm2_g, name="norm2", stages=[stage_gather_d2d([b_w1], peers=(2,))])
    w1_f = b_w1.reshape(4, mlp_w1.shape[1], mlp_w1.shape[2])
    a1, [[b_w2]] = mm_nn(u2, w1_f, out_dtype=BF16, name="mlp_up", tiles=(None, min(1024, D), None),
                         stages=[stage_both(stage_gather_relay([b_w2]), stage_gather_d2d([b_w2], peers=(0, 1)))])
    [[b_w2]] = comm_call([stage_gather_d2d([b_w2], peers=(2,))], name="w2_to_sibling")
    w2_f = b_w2.reshape(4 * mlp_w2.shape[1], mlp_w2.shape[2])
    Tp = h0.shape[0]
    h2 = mm_nn(a1, w2_f, out_dtype=F32, name="mlp_down", a_pro=_relu_sq, epilogue=lambda r, res: r + res, extras=(h1,),
               tiles=(_pick(Tp, (704, 512, 256, 128)), min(256, D), a1.shape[1]))
    dh2_bf, st_f = final_loss(h2, target, gfin, name="final_loss")

    da1 = mm_nt(dh2_bf, w2_f, out_dtype=BF16, name="mlp_down_dx",
                epilogue=lambda r, a: r * (2.0 * jnp.maximum(a.astype(F32), 0.0)), extras=(a1,))
    d_w2 = mm_tn(a1, dh2_bf, shards=1, out_dtype=BF16, name="mlp_down_dw", a_pro=_relu_sq, tiles=(512, min(1024, D), Tp))
    g6 = split2(d_w2.reshape(4, -1, d_w2.shape[-1]))
    d_w1, [[r6]] = mm_nn(u2_t, da1, out_dtype=BF16, name="mlp_up_dw", out_shards=4, tiles=(min(1024, D), min(1024, D), Tp),
                         stages=[stage_to_sibling([g6])])
    g5 = split2(d_w1)
    p6 = presum(6, g6, r6)
    du2, [[q6_near], [r5]] = mm_nt(da1, w1_f, out_dtype=BF16, name="mlp_up_dx",
                                  stages=[stage_to_chips([p6], peers=(0, 1)), stage_to_sibling([g5])])
    p5 = presum(5, g5, r5)
    dh1_bf, st_2 = rms_bwd(h1, norm2_g, du2, dh2_bf, name="norm2_bwd")
    dmerged = mm_nt(dh1_bf, w_out_f, out_dtype=BF16, name="out_proj_dx")
    d_wout = mm_tn(merged, dh1_bf, shards=1, out_dtype=BF16, name="out_proj_dw", tiles=(512, min(1024, D), Tp))
    g4 = split2(d_wout.reshape(4, -1, d_wout.shape[-1]))
    (dproj, d_pool, d_ga, d_gx, d_vecs), [[q6], [q5], [r4]] = mix_bwd(
        proj, hs, saved, dmerged, pool_f, ga_f, gx_f, vecs, name="mix_bwd",
        stages=[stage_to_chips([p6], peers=(2,), into=[q6_near]), stage_to_chips([p5]), stage_to_sibling([g4])])
    p4 = presum(4, g4, r4)
    f5, f6 = total(5, g5, r5, q5), total(6, g6, r6, q6)
    g1, g2, g3 = split2(chipwise(d_pool, G, PG // 4)), split2(chipwise(d_ga, H, hd4)), split2(chipwise(d_gx, H, hd4))
    whole_k = (u_mine.shape[0], _pick(dproj.shape[1] // 4, (1280, 512, 256, 128)), Tp)
    g_theirs, [[r1, r2, r3], [o5, o6], [q4]] = mm_nn(u_theirs, dproj, out_dtype=BF16, out_shards=4, tiles=whole_k, name="proj_dw_sibling_rows",
                                                    stages=[stage_to_sibling([g1, g2, g3]), stage_from_sibling([f5, f6]), stage_to_chips([p4])])
    p1, p2, p3 = presum(1, g1, r1), presum(2, g2, r2), presum(3, g3, r3)
    g_mine, [[r0], [q1, q2, q3]] = mm_nn(u_mine, dproj, out_dtype=BF16, out_shards=4, tiles=whole_k, name="proj_dw_own_rows",
                                         stages=[stage_from_sibling([g_theirs]), stage_to_chips([p1, p2, p3])])
    g0 = g_mine[:, None]
    p0 = chip_presum(g0, r0, jnp.zeros((1,), jnp.int32), name="presum_w_in")
    f4 = total(4, g4, r4, q4)
    du, [[q0], [o4]] = mm_nt(dproj, w_in_f, out_dtype=BF16, name="proj_dx", tiles=(None, None, w_in_f.shape[2]),
                             stages=[stage_to_chips([p0]), stage_from_sibling([f4])])
    f0 = final_half(g0, r0, q0, jnp.stack([chip, 0]).astype(jnp.int32), name="sum_w_in")
    f1, f2, f3 = total(1, g1, r1, q1), total(2, g2, r2, q2), total(3, g3, r3, q3)
    grad_x, d_meta, st_1 = rms_bwd_input(h0, norm1_g, du, dh1_bf, name="norm1_bwd")

    small = jnp.concatenate([d_meta, d_vecs, st_1, st_2, st_f], axis=0)
    tot, [[o0, o1, o2, o3]] = allgather8(small, name="sum_small", reduce_sum=True, stages=[stage_from_sibling([f0, f1, f2, f3])])
    mine = [f0, f1, f2, f3, f4, f5, f6]
    theirs = [o0, o1, o2, o3, o4, o5, o6]
    loss = jnp.sum(tot[49])
    g_meta = lax.dynamic_slice_in_dim(tot[0:16], chip * dshard, dshard, axis=1)
    g_pool_scale, g_conv_b, g_lam = tot[16:17], tot[17:18], tot[20:21]
    g_ba = lax.dynamic_slice_in_dim(tot[18].reshape(H, HEAD_DIM), chip * hd4, hd4, axis=1)[None]
    g_bx = lax.dynamic_slice_in_dim(tot[19].reshape(H, HEAD_DIM), chip * hd4, hd4, axis=1)[None]
    g_conv_w = lax.dynamic_slice_in_dim(tot[21:25], chip * dshard, dshard, axis=1)[None]
    g_n1, g_n2, g_fin = tot[32:33], tot[40:41], tot[48]

    def step(w, g, m, v, nm):
        cols = w.shape[-1]
        outs = adamw(w.reshape(-1, cols), g.reshape(-1, cols), m.reshape(-1, cols), v.reshape(-1, cols), name="adamw_" + nm)
        return [o.reshape(w.shape) for o in outs]

    def step_big(t, w, m, v):
        outs = adamw_halves(halves(w), mine[t], theirs[t], halves(m), halves(v), c_arr, name="adamw_" + names[t])
        return [o.reshape(w.shape) for o in outs]

    res = dict(meta_tokens=step(meta_tokens, g_meta, m_meta_tokens, v_meta_tokens, "meta_tokens"),
               norm1_g=step(norm1_g, g_n1, m_norm1_g, v_norm1_g, "norm1_g"),
               w_in=step_big(0, w_in, m_w_in, v_w_in), pool_w=step_big(1, pool_w, m_pool_w, v_pool_w),
               pool_scale=step(pool_scale, g_pool_scale, m_pool_scale, v_pool_scale, "pool_scale"),
               conv_w=step(conv_w, g_conv_w, m_conv_w, v_conv_w, "conv_w"), conv_b=step(conv_b, g_conv_b, m_conv_b, v_conv_b, "conv_b"),
               gate_a_w=step_big(2, gate_a_w, m_gate_a_w, v_gate_a_w), gate_a_b=step(gate_a_b, g_ba, m_gate_a_b, v_gate_a_b, "gate_a_b"),
               gate_x_w=step_big(3, gate_x_w, m_gate_x_w, v_gate_x_w), gate_x_b=step(gate_x_b, g_bx, m_gate_x_b, v_gate_x_b, "gate_x_b"),
               lru_lambda=step(lru_lambda, g_lam, m_lru_lambda, v_lru_lambda, "lru_lambda"), w_out=step_big(4, w_out, m_w_out, v_w_out),
               norm2_g=step(norm2_g, g_n2, m_norm2_g, v_norm2_g, "norm2_g"), mlp_w1=step_big(5, mlp_w1, m_mlp_w1, v_mlp_w1),
               mlp_w2=step_big(6, mlp_w2, m_mlp_w2, v_mlp_w2), final_g=step(final_g, g_fin, m_final_g, v_final_g, "final_g"))
    order = list(res)
    return (loss, grad_x[None], *[res[n][0] for n in order], *[res[n][1] for n in order], *[res[n][2] for n in order],
            *[res[n][3] for n in order])
```

```python
import functools

import jax
import jax.numpy as jnp
from jax import lax
from jax.experimental import pallas as pl
from jax.experimental.pallas import tpu as pltpu

F32 = jnp.float32
BF16 = jnp.bfloat16
MESH = pl.DeviceIdType.MESH

NORM_EPS = 1e-6
N_META = 16
HEAD_DIM = 256
POOL_WINDOWS = (2, 4, 8, 16)
LRU_C = 8.0
ROW_TILE = 128
HIST = 16
VMEM_LIMIT_BYTES = 56 * 1024 * 1024
ADAM_LR, ADAM_B1, ADAM_B2, ADAM_EPS, ADAM_WD, ADAM_STEP = 0.001, 0.9, 0.999, 1e-08, 0.01, 10


def _pick(n, prefs):
    for p in prefs:
        if n % p == 0:
            return p
    return n


def _params(*sem):
    return pltpu.CompilerParams(dimension_semantics=sem, vmem_limit_bytes=VMEM_LIMIT_BYTES)


def _sigmoid(x):
    return 1.0 / (1.0 + jnp.exp(-x))


def _gelu_tanh(x):
    t = jnp.tanh(0.7978845608028654 * (x + 0.044715 * (x * x * x)))
    return 0.5 * x * (1.0 + t), t


def _gelu_tanh_grad(x, t):
    return 0.5 * (1.0 + t) + 0.5 * x * (1.0 - t * t) * (0.7978845608028654 * (1.0 + 3.0 * 0.044715 * x * x))


def _neg_expm1(x, exp_x):
    series = x * (-1.0 + x * (-0.5 + x * ((-1.0 / 6.0) + x * ((-1.0 / 24.0) + x * (-1.0 / 120.0)))))
    return jnp.where(x > -0.125, series, 1.0 - exp_x)


def _softplus_neg(lam):
    z = jnp.exp(-jnp.abs(lam))
    log1p_z = jnp.where(z < 0.01, z * (1.0 - z * (0.5 - z * (1.0 / 3.0))), jnp.log(1.0 + z))
    return jnp.maximum(-lam, 0.0) + log1p_z


def _tile_masks(is_meta, rows):
    row = lax.broadcasted_iota(jnp.int32, (rows, 1), 0)
    valid = jnp.logical_or(jnp.logical_not(is_meta), row >= rows - N_META)
    t_log = jnp.where(is_meta, row - (rows - N_META), 1 << 20)
    return row, valid, t_log


def _window_count_inv(t_log, w):
    return 1.0 / jnp.clip(t_log + 1, 1, w).astype(F32)


def _dot_nt(a, b):
    return lax.dot_general(a, b, (((1,), (1,)), ((), ())), preferred_element_type=F32)


def _dot_tn(a, b):
    return lax.dot_general(a, b, (((0,), (0,)), ((), ())), preferred_element_type=F32)


def _place():
    x, y, c = lax.axis_index("x"), lax.axis_index("y"), lax.axis_index("c")
    chips = [(1 - x, y), (x, 1 - y), (1 - x, 1 - y)]
    return x, y, c, chips


def _remote(src, dst, send_sem, recv_sem, to):
    return pltpu.make_async_remote_copy(src_ref=src, dst_ref=dst, send_sem=send_sem, recv_sem=recv_sem, device_id=to,
                                        device_id_type=MESH)


class Stage:
    def __init__(self, arrays, out_shapes, aliases, n_copies, copies):
        self.arrays, self.out_shapes, self.aliases, self.n_copies, self.copies = list(arrays), list(out_shapes), aliases, n_copies, copies


def _sds(a):
    return jax.ShapeDtypeStruct(a.shape, a.dtype)


def _relay_rows(buf_ref, chip_idx, c, quarter):
    rows = buf_ref.shape[2] // 2
    return buf_ref.at[chip_idx, c, pl.ds(quarter * rows, rows)]


def stage_gather_direct(bufs, quarter=None):
    n = len(bufs)

    def copies(ins, outs, send, recv):
        x, y, c, chips = _place()
        me = 2 * x + y
        sends, recvs = [], []
        for t in range(n):
            for j, (px, py) in enumerate(chips[:2]):
                k = 2 * t + j
                if quarter is None:
                    mine, theirs = outs[t].at[me, c], outs[t].at[2 * px + py, c]
                else:
                    mine, theirs = _relay_rows(outs[t], me, c, quarter), _relay_rows(outs[t], 2 * px + py, c, quarter)
                sends.append(_remote(mine, mine, send.at[k], recv.at[k], (px, py, c)))
                recvs.append(_remote(theirs, theirs, send.at[k], recv.at[k], (px, py, c)))
        return sends, recvs

    return Stage(bufs, [_sds(b) for b in bufs], {t: t for t in range(n)}, 2 * n, copies)


def stage_gather_relay(bufs):
    n = len(bufs)

    def copies(ins, outs, send, recv):
        x, y, c, chips = _place()
        (xx, xy), (yx, yy), (dx, dy) = chips
        sends, recvs = [], []
        for t in range(n):
            from_y, from_x = _relay_rows(outs[t], 2 * yx + yy, c, 0), _relay_rows(outs[t], 2 * xx + xy, c, 1)
            sends.append(_remote(from_y, from_y, send.at[2 * t], recv.at[2 * t], (xx, xy, c)))
            sends.append(_remote(from_x, from_x, send.at[2 * t + 1], recv.at[2 * t + 1], (yx, yy, c)))
            for q, (px, py) in enumerate(chips[:2]):
                got = _relay_rows(outs[t], 2 * dx + dy, c, q)
                recvs.append(_remote(got, got, send.at[2 * t + q], recv.at[2 * t + q], (px, py, c)))
        return sends, recvs

    return Stage(bufs, [_sds(b) for b in bufs], {t: t for t in range(n)}, 2 * n, copies)


def stage_gather_d2d(bufs, peers=(0, 1, 2)):
    n = len(bufs)

    def copies(ins, outs, send, recv):
        x, y, c, chips = _place()
        sends, recvs = [], []
        for t in range(n):
            for slot, j in enumerate(peers):
                px, py = chips[j]
                k = len(peers) * t + slot
                got, sib = outs[t].at[2 * px + py, c], outs[t].at[2 * px + py, 1 - c]
                sends.append(_remote(got, got, send.at[k], recv.at[k], (x, y, 1 - c)))
                recvs.append(_remote(sib, sib, send.at[k], recv.at[k], (x, y, 1 - c)))
        return sends, recvs

    return Stage(bufs, [_sds(b) for b in bufs], {t: t for t in range(n)}, len(peers) * n, copies)


class _SemsFrom:
    def __init__(self, sems, first):
        self.sems, self.first = sems, first

    @property
    def at(self):
        return self

    def __getitem__(self, k):
        return self.sems.at[self.first + k]


def stage_both(first, second):
    def copies(ins, outs, send, recv):
        s1, r1 = first.copies(ins, outs, send, recv)
        s2, r2 = second.copies(ins, outs, _SemsFrom(send, first.n_copies), _SemsFrom(recv, first.n_copies))
        return s1 + s2, r1 + r2

    return Stage(first.arrays, first.out_shapes, first.aliases, first.n_copies + second.n_copies, copies)


def stage_to_sibling(grads):
    n = len(grads)

    def copies(ins, outs, send, recv):
        x, y, c, _ = _place()
        sends, recvs = [], []
        for t in range(n):
            for k4 in range(4):
                k = 4 * t + k4
                sends.append(_remote(ins[t].at[k4, 1 - c], outs[t].at[k4], send.at[k], recv.at[k], (x, y, 1 - c)))
                recvs.append(_remote(outs[t].at[k4], outs[t].at[k4], send.at[k], recv.at[k], (x, y, 1 - c)))
        return sends, recvs

    return Stage(grads, [jax.ShapeDtypeStruct((4, *g.shape[2:]), g.dtype) for g in grads], {}, 4 * n, copies)


def stage_to_chips(presums, peers=(0, 1, 2), into=None):
    n = len(presums)

    def copies(ins, outs, send, recv):
        x, y, c, chips = _place()
        sends, recvs = [], []
        for t in range(n):
            for slot, j in enumerate(peers):
                px, py = chips[j]
                k = len(peers) * t + slot
                sends.append(_remote(ins[t].at[2 * px + py], outs[t].at[j], send.at[k], recv.at[k], (px, py, c)))
                recvs.append(_remote(outs[t].at[j], outs[t].at[j], send.at[k], recv.at[k], (px, py, c)))
        return sends, recvs

    out_shapes = [jax.ShapeDtypeStruct((3, *p.shape[1:]), p.dtype) for p in presums]
    if into is None:
        return Stage(presums, out_shapes, {}, len(peers) * n, copies)
    return Stage([*presums, *into], out_shapes, {n + t: t for t in range(n)}, len(peers) * n, copies)


def stage_from_sibling(halves):
    n = len(halves)

    def copies(ins, outs, send, recv):
        x, y, c, _ = _place()
        sends = [_remote(ins[t], outs[t], send.at[t], recv.at[t], (x, y, 1 - c)) for t in range(n)]
        recvs = [_remote(outs[t], outs[t], send.at[t], recv.at[t], (x, y, 1 - c)) for t in range(n)]
        return sends, recvs

    return Stage(halves, [_sds(h) for h in halves], {}, n, copies)


def _any_specs(n):
    return [pl.BlockSpec(memory_space=pl.ANY)] * n


def _staged_call(body, *, grid, in_specs, out_specs, out_shape, scratch_shapes, name, semantics, inputs, stages=()):
    n_in, n_out, n_scr = len(in_specs), len(out_specs), len(scratch_shapes)
    st_arrays = [a for s in stages for a in s.arrays]
    st_outs = [o for s in stages for o in s.out_shapes]
    st_sems = [pltpu.SemaphoreType.DMA((s.n_copies,)) for s in stages for _ in range(2)]
    aliases = {}
    at_in, at_out = n_in, n_out
    for s in stages:
        for a, o in s.aliases.items():
            aliases[at_in + a] = at_out + o
        at_in += len(s.arrays)
        at_out += len(s.out_shapes)

    def full_body(*refs):
        pos = 0

        def take(count):
            nonlocal pos
            part = refs[pos:pos + count]
            pos += count
            return part

        ins, s_ins, outs, s_outs, scr = take(n_in), take(len(st_arrays)), take(n_out), take(len(st_outs)), take(n_scr)
        s_sems = refs[pos:]

        def each_stage(action):
            at_i = at_o = 0
            for idx, s in enumerate(stages):
                sends, recvs = s.copies(s_ins[at_i:at_i + len(s.arrays)], s_outs[at_o:at_o + len(s.out_shapes)],
                                        s_sems[2 * idx], s_sems[2 * idx + 1])
                action(sends, recvs)
                at_i += len(s.arrays)
                at_o += len(s.out_shapes)

        if stages:
            ids = [pl.program_id(a) for a in range(len(grid))]
            first = functools.reduce(jnp.logical_and, [i == 0 for i in ids])
            last = functools.reduce(jnp.logical_and, [i == g - 1 for i, g in zip(ids, grid)])

            def start(sends, recvs):
                for cp in sends:
                    cp.start()

            def finish(sends, recvs):
                for cp in recvs:
                    cp.wait_recv()
                for cp in sends:
                    cp.wait_send()

            @pl.when(first)
            def _():
                each_stage(start)

        body(*ins, *outs, *scr)

        if stages:
            @pl.when(last)
            def _():
                each_stage(finish)

    sem = tuple("arbitrary" for _ in grid) if stages else tuple(semantics)
    res = pl.pallas_call(
        full_body, grid=grid, in_specs=[*in_specs, *_any_specs(len(st_arrays))], out_specs=[*out_specs, *_any_specs(len(st_outs))],
        out_shape=[*out_shape, *st_outs], scratch_shapes=[*scratch_shapes, *st_sems], input_output_aliases=aliases, name=name,
        compiler_params=_params(*sem))(*inputs, *st_arrays)
    outs, rest = list(res[:n_out]), list(res[n_out:])
    per_stage = []
    for s in stages:
        per_stage.append(rest[:len(s.out_shapes)])
        rest = rest[len(s.out_shapes):]
    return outs, per_stage


def comm_call(stages, *, name):
    return _staged_call(lambda: None, grid=(1,), in_specs=[], out_specs=[], out_shape=[], scratch_shapes=[], name=name,
                        semantics=("arbitrary",), inputs=[], stages=stages)[1]


def _matmul(kind, a, b, *, grid, a_spec, b_spec, out_spec, out_shape, acc_shape, name,
            a_pro=None, epilogue=None, extras=(), extra_specs=(), stages=()):
    nk = grid[2]
    n_extra = len(extras)

    def body(a_ref, b_ref, *rest):
        extra_refs = rest[:n_extra]
        o_ref = rest[n_extra]
        av = a_ref[...]
        if a_pro is not None:
            av = a_pro(av)
        av = av.astype(BF16)
        bv = b_ref[...].astype(BF16)
        if kind == "nn":
            p = jnp.dot(av, bv, preferred_element_type=F32)
        elif kind == "nt":
            p = _dot_nt(av, bv)
        else:
            p = _dot_tn(av, bv)

        def finish(r):
            if epilogue is not None:
                r = epilogue(r, *[e[...] for e in extra_refs])
            o_ref[...] = r.astype(o_ref.dtype)

        if nk == 1:
            finish(p)
        else:
            acc_ref = rest[n_extra + 1]
            k = pl.program_id(2)

            @pl.when(k == 0)
            def _():
                acc_ref[...] = p

            @pl.when(k > 0)
            def _():
                acc_ref[...] += p

            @pl.when(k == nk - 1)
            def _():
                finish(acc_ref[...])

    scratch = [] if nk == 1 else [pltpu.VMEM(acc_shape, F32)]
    outs, staged = _staged_call(body, grid=grid, in_specs=[a_spec, b_spec, *extra_specs], out_specs=[out_spec], out_shape=[out_shape],
                                scratch_shapes=scratch, name=name, semantics=("parallel", "parallel", "arbitrary"),
                                inputs=[a, b, *extras], stages=stages)
    return (outs[0], staged) if stages else outs[0]


def mm_nn(a, b, *, out_dtype, name, a_pro=None, epilogue=None, extras=(), stages=(), out_shards=1, tiles=(None, None, None)):
    M, K = a.shape
    sharded = b.ndim == 3
    ns = b.shape[2] if sharded else b.shape[1]
    N = ns * b.shape[0] if sharded else ns
    if out_shards > 1:
        ns = N // out_shards
    tm = tiles[0] or _pick(M, (1408, 1024, 512, 256, 128))
    tn = tiles[1] or _pick(ns, (512, 256, 128))
    tk = tiles[2] or _pick(K, (2048, 1408, 1024, 512, 256, 128))
    per = ns // tn
    if sharded:
        b_spec = pl.BlockSpec((None, tk, tn), lambda i, j, k: (j // per, k, j % per))
    else:
        b_spec = pl.BlockSpec((tk, tn), lambda i, j, k: (k, j))
    mn = pl.BlockSpec((tm, tn), lambda i, j, k: (i, j))
    if out_shards > 1:
        out_spec = pl.BlockSpec((None, tm, tn), lambda i, j, k: (j // per, i, j % per))
        out_shape = jax.ShapeDtypeStruct((out_shards, M, ns), out_dtype)
    else:
        out_spec, out_shape = mn, jax.ShapeDtypeStruct((M, N), out_dtype)
    return _matmul("nn", a, b, grid=(M // tm, N // tn, K // tk), a_spec=pl.BlockSpec((tm, tk), lambda i, j, k: (i, k)),
                   b_spec=b_spec, out_spec=out_spec, out_shape=out_shape, acc_shape=(tm, tn),
                   name=name, a_pro=a_pro, epilogue=epilogue, extras=extras, extra_specs=[mn] * len(extras), stages=stages)


def mm_nt(a, w, *, out_dtype, name, epilogue=None, extras=(), stages=(), tiles=(None, None, None)):
    M, N = a.shape
    sharded = w.ndim == 3
    kw = w.shape[1] if sharded else w.shape[0]
    ns = w.shape[2] if sharded else w.shape[1]
    tm = tiles[0] or _pick(M, (1408, 512, 256, 128))
    tkw = tiles[1] or _pick(kw, (1024, 512, 256, 128))
    tk = tiles[2] or _pick(ns, (2048, 1280, 1024, 512, 256, 128))
    per = ns // tk
    if sharded:
        w_spec = pl.BlockSpec((None, tkw, tk), lambda i, j, k: (k // per, j, k % per))
    else:
        w_spec = pl.BlockSpec((tkw, tk), lambda i, j, k: (j, k))
    mo = pl.BlockSpec((tm, tkw), lambda i, j, k: (i, j))
    return _matmul("nt", a, w, grid=(M // tm, kw // tkw, N // tk), a_spec=pl.BlockSpec((tm, tk), lambda i, j, k: (i, k)),
                   b_spec=w_spec, out_spec=mo, out_shape=jax.ShapeDtypeStruct((M, kw), out_dtype), acc_shape=(tm, tkw),
                   name=name, epilogue=epilogue, extras=extras, extra_specs=[mo] * len(extras), stages=stages)


def mm_tn(a, g, *, shards, name, out_dtype, a_pro=None, stages=(), tiles=(None, None, None)):
    T, kw = a.shape
    N = g.shape[1]
    ns = N // shards
    tkw = tiles[0] or _pick(kw, (1024, 512, 256, 128))
    tn = tiles[1] or _pick(ns, (1280, 1024, 512, 256, 128))
    tt = tiles[2] or _pick(T, (1408, 512, 256, 128))
    per = ns // tn
    if shards > 1:
        out_spec = pl.BlockSpec((None, tkw, tn), lambda i, j, k: (j // per, i, j % per))
        out_shape = jax.ShapeDtypeStruct((shards, kw, ns), out_dtype)
    else:
        out_spec = pl.BlockSpec((tkw, tn), lambda i, j, k: (i, j))
        out_shape = jax.ShapeDtypeStruct((kw, N), out_dtype)
    return _matmul("tn", a, g, grid=(kw // tkw, N // tn, T // tt), a_spec=pl.BlockSpec((tt, tkw), lambda i, j, k: (k, i)),
                   b_spec=pl.BlockSpec((tt, tn), lambda i, j, k: (k, j)), out_spec=out_spec, out_shape=out_shape,
                   acc_shape=(tkw, tn), name=name, a_pro=a_pro, stages=stages)


def _relu_sq(a):
    r = jnp.maximum(a, 0.0)
    return r * r


def rmsnorm_fwd(h, g, *, name, stages=()):
    Tp, D = h.shape
    tr = _pick(Tp, (384, 256, 128))

    def body(h_ref, g_ref, u_ref, ut_ref):
        x = h_ref[...]
        r = lax.rsqrt(jnp.mean(x * x, axis=-1, keepdims=True) + NORM_EPS)
        u = (x * r) * g_ref[...]
        u_ref[...] = u.astype(BF16)
        ut_ref[...] = u.T.astype(BF16)

    row = pl.BlockSpec((tr, D), lambda i: (i, 0))
    outs, staged = _staged_call(body, grid=(Tp // tr,), in_specs=[row, pl.BlockSpec((1, D), lambda i: (0, 0))],
                                out_specs=[row, pl.BlockSpec((D, tr), lambda i: (0, i))],
                                out_shape=[jax.ShapeDtypeStruct((Tp, D), BF16), jax.ShapeDtypeStruct((D, Tp), BF16)],
                                scratch_shapes=[], name=name, semantics=("parallel",), inputs=[h, g], stages=stages)
    return (outs, staged) if stages else outs


def rmsnorm_fwd_input(x, meta_tile, g, *, name, stages=()):
    S, D = x.shape
    nt = S // ROW_TILE + 1

    def body(x_ref, mt_ref, g_ref, h_ref, u_ref, ut_mine_ref, ut_theirs_ref):
        h = jnp.where(pl.program_id(0) == nt - 1, mt_ref[...], x_ref[...])
        r = lax.rsqrt(jnp.mean(h * h, axis=-1, keepdims=True) + NORM_EPS)
        u = (h * r) * g_ref[...]
        h_ref[...] = h
        u_ref[...] = u.astype(BF16)
        ut = u.T.astype(BF16)
        low, high = ut[:D // 2, :], ut[D // 2:, :]
        mine_is_high = lax.axis_index("c") == 1
        ut_mine_ref[...] = jnp.where(mine_is_high, high, low)
        ut_theirs_ref[...] = jnp.where(mine_is_high, low, high)

    row = pl.BlockSpec((ROW_TILE, D), lambda i: (i, 0))
    outs, staged = _staged_call(
        body, grid=(nt,),
        in_specs=[pl.BlockSpec((ROW_TILE, D), lambda i: (jnp.minimum(i, nt - 2), 0)), pl.BlockSpec((ROW_TILE, D), lambda i: (0, 0)),
                  pl.BlockSpec((1, D), lambda i: (0, 0))],
        out_specs=[row, row, pl.BlockSpec((D // 2, ROW_TILE), lambda i: (0, i)), pl.BlockSpec((D // 2, ROW_TILE), lambda i: (0, i))],
        out_shape=[jax.ShapeDtypeStruct((S + ROW_TILE, D), F32), jax.ShapeDtypeStruct((S + ROW_TILE, D), BF16),
                   jax.ShapeDtypeStruct((D // 2, S + ROW_TILE), BF16), jax.ShapeDtypeStruct((D // 2, S + ROW_TILE), BF16)],
        scratch_shapes=[], name=name, semantics=("parallel",), inputs=[x, meta_tile, g], stages=stages)
    return (outs, staged) if stages else outs


def _rms_bwd_math(x, g, dy):
    r = lax.rsqrt(jnp.mean(x * x, axis=-1, keepdims=True) + NORM_EPS)
    xh = x * r
    dyg = dy * g
    dx = r * (dyg - xh * jnp.mean(dyg * xh, axis=-1, keepdims=True))
    return dx, xh


def final_loss(h2, target, gf, *, name):
    Tp, D = h2.shape
    nt = Tp // ROW_TILE

    def body(h_ref, t_ref, g_ref, dhb_ref, st_ref):
        i = pl.program_id(0)

        @pl.when(i == 0)
        def _():
            st_ref[...] = jnp.zeros_like(st_ref)

        x = h_ref[...]
        g = g_ref[...]
        r = lax.rsqrt(jnp.mean(x * x, axis=-1, keepdims=True) + NORM_EPS)
        xh = x * r
        err = jnp.where(i == nt - 1, 0.0, xh * g - t_ref[...])
        dout = err * (1.0 / D)
        dyg = dout * g
        dx = r * (dyg - xh * jnp.mean(dyg * xh, axis=-1, keepdims=True))
        dhb_ref[...] = dx.astype(BF16)
        st_ref[0:1, :] += jnp.sum(dout * xh, axis=0, keepdims=True)
        st_ref[1:2, :] += jnp.sum(err * err, axis=0, keepdims=True) * (0.5 / D)

    row = pl.BlockSpec((ROW_TILE, D), lambda i: (i, 0))
    return pl.pallas_call(
        body, grid=(nt,),
        in_specs=[row, pl.BlockSpec((ROW_TILE, D), lambda i: (jnp.minimum(i, nt - 2), 0)), pl.BlockSpec((1, D), lambda i: (0, 0))],
        out_specs=[row, pl.BlockSpec((8, D), lambda i: (0, 0))],
        out_shape=[jax.ShapeDtypeStruct((Tp, D), BF16), jax.ShapeDtypeStruct((8, D), F32)],
        name=name, compiler_params=_params("arbitrary"))(h2, target, gf)


def rms_bwd(h, g, du, dres, *, name):
    Tp, D = h.shape
    tr = _pick(Tp, (384, 256, 128))

    def body(h_ref, g_ref, du_ref, dr_ref, dhb_ref, st_ref):
        @pl.when(pl.program_id(0) == 0)
        def _():
            st_ref[...] = jnp.zeros_like(st_ref)

        du_v = du_ref[...].astype(F32)
        dx, xh = _rms_bwd_math(h_ref[...], g_ref[...], du_v)
        dhb_ref[...] = (dr_ref[...].astype(F32) + dx).astype(BF16)
        st_ref[0:1, :] += jnp.sum(du_v * xh, axis=0, keepdims=True)

    row = pl.BlockSpec((tr, D), lambda i: (i, 0))
    return pl.pallas_call(
        body, grid=(Tp // tr,), in_specs=[row, pl.BlockSpec((1, D), lambda i: (0, 0)), row, row],
        out_specs=[row, pl.BlockSpec((8, D), lambda i: (0, 0))],
        out_shape=[jax.ShapeDtypeStruct((Tp, D), BF16), jax.ShapeDtypeStruct((8, D), F32)],
        name=name, compiler_params=_params("arbitrary"))(h, g, du, dres)


def rms_bwd_input(h, g, du, dres, *, name, stages=()):
    Tp, D = h.shape
    nt = Tp // ROW_TILE

    def body(h_ref, g_ref, du_ref, dr_ref, gx_ref, gm_ref, st_ref):
        i = pl.program_id(0)

        @pl.when(i == 0)
        def _():
            st_ref[...] = jnp.zeros_like(st_ref)

        du_v = du_ref[...].astype(F32)
        dx, xh = _rms_bwd_math(h_ref[...], g_ref[...], du_v)
        dh = dr_ref[...].astype(F32) + dx
        st_ref[0:1, :] += jnp.sum(du_v * xh, axis=0, keepdims=True)

        @pl.when(i < nt - 1)
        def _():
            gx_ref[...] = dh

        @pl.when(i == nt - 1)
        def _():
            gm_ref[...] = dh[ROW_TILE - N_META:, :]

    row = pl.BlockSpec((ROW_TILE, D), lambda i: (i, 0))
    outs, staged = _staged_call(
        body, grid=(nt,), in_specs=[row, pl.BlockSpec((1, D), lambda i: (0, 0)), row, row],
        out_specs=[pl.BlockSpec((ROW_TILE, D), lambda i: (jnp.minimum(i, nt - 2), 0)), pl.BlockSpec((N_META, D), lambda i: (0, 0)),
                   pl.BlockSpec((8, D), lambda i: (0, 0))],
        out_shape=[jax.ShapeDtypeStruct((Tp - ROW_TILE, D), F32), jax.ShapeDtypeStruct((N_META, D), F32),
                   jax.ShapeDtypeStruct((8, D), F32)],
        scratch_shapes=[], name=name, semantics=("arbitrary",), inputs=[h, g, du, dres], stages=stages)
    return (outs, staged) if stages else outs


def _conv_shifts(ext):
    return tuple(pltpu.roll(ext, k, 0)[HIST:, :] for k in (1, 2, 3))


def _conv_taps(ext, cur, vec_ref, cs):
    x1, x2, x3 = _conv_shifts(ext)
    return vec_ref[1:2, cs] + vec_ref[8:9, cs] * cur + vec_ref[7:8, cs] * x1 + vec_ref[6:7, cs] * x2 + vec_ref[5:6, cs] * x3


def _window_sum_back(ext, w):
    s, sh = ext, 1
    while sh < w:
        s = s + pltpu.roll(s, sh, 0)
        sh *= 2
    return s[HIST:, :]


def _scan_rows(a, b, carry, *, reverse):
    rows = a.shape[0]
    rin = jnp.bitwise_and(lax.broadcasted_iota(jnp.int32, (rows, 1), 0), 7)
    sh = 1
    while sh < 8:
        keep = (rin < 8 - sh) if reverse else (rin >= sh)
        amount = rows - sh if reverse else sh
        a_sh = jnp.where(keep, pltpu.roll(a, amount, 0), 1.0)
        b_sh = jnp.where(keep, pltpu.roll(b, amount, 0), 0.0)
        b = b + a * b_sh
        a = a * a_sh
        sh *= 2
    out = [None] * (rows // 8)
    for g in (reversed(range(rows // 8)) if reverse else range(rows // 8)):
        hg = b[8 * g:8 * g + 8, :] + a[8 * g:8 * g + 8, :] * carry
        carry = hg[0:1, :] if reverse else hg[7:8, :]
        out[g] = hg
    return jnp.concatenate(out, axis=0)


def _lru_gates(xc, wa, wx, vec_ref, cs, sp):
    xcb = xc.astype(BF16)
    r = _sigmoid(jnp.dot(xcb, wa, preferred_element_type=F32) + vec_ref[2:3, cs])
    ig = _sigmoid(jnp.dot(xcb, wx, preferred_element_type=F32) + vec_ref[3:4, cs])
    log_a = (-LRU_C) * r * sp
    a = jnp.exp(log_a)
    a2 = a * a
    return xcb, r, ig, a, a2, _neg_expm1(2.0 * log_a, a2)


def mix_fwd(proj, pool_w, gate_a, gate_x, vecs, *, name, stages=()):
    Tp = proj.shape[0]
    D = proj.shape[1] // 5
    R = ROW_TILE
    nt = Tp // R
    H = D // HEAD_DIM
    PG = D // len(POOL_WINDOWS)

    def body(p_ref, pw_ref, wa_ref, wx_ref, vec_ref, m_ref, hs_ref, xc_ref, r_ref, ig_ref, a_ref, mu_ref, ge_ref, dge_ref,
             hist_p, hist_l, hcar, mtmp):
        i = pl.program_id(0)
        is_meta = i == 0

        @pl.when(is_meta)
        def _():
            hist_p[...] = jnp.zeros_like(hist_p)
            hist_l[...] = jnp.zeros_like(hist_l)
            hcar[...] = jnp.zeros_like(hcar)

        row, valid, t_log = _tile_masks(is_meta, R)

        for g, w in enumerate(POOL_WINDOWS):
            cs = slice(g * PG, (g + 1) * PG)
            v = p_ref[:, g * PG:(g + 1) * PG].astype(F32)
            ws = _window_sum_back(jnp.concatenate([hist_p[:, cs], v], axis=0), w)
            d = ws * _window_count_inv(t_log, w) - v
            y = jnp.dot(d.astype(BF16), pw_ref[g], preferred_element_type=F32)
            gp = p_ref[:, 3 * D + g * PG:3 * D + (g + 1) * PG].astype(F32)
            mtmp[:, cs] = _sigmoid(gp) * (y * vec_ref[0:1, cs])
            hist_p[:, cs] = v[R - HIST:, :]

        for h in range(H):
            cs = slice(h * HEAD_DIM, (h + 1) * HEAD_DIM)
            vl = p_ref[:, D + h * HEAD_DIM:D + (h + 1) * HEAD_DIM].astype(F32)
            xc = _conv_taps(jnp.concatenate([hist_l[:, cs], vl], axis=0), vl, vec_ref, cs)
            sp = _softplus_neg(vec_ref[4:5, cs])
            xcb, r, ig, a, _, em = _lru_gates(xc, wa_ref[h], wx_ref[h], vec_ref, cs, sp)
            mult = jnp.sqrt(em)
            b = jnp.where(valid, mult * (ig * xc), 0.0)
            hs = _scan_rows(a, b, hcar[7:8, cs], reverse=False)
            hs_ref[:, cs] = hs
            xc_ref[:, cs], r_ref[:, cs], ig_ref[:, cs] = xcb, r.astype(BF16), ig.astype(BF16)
            a_ref[:, cs], mu_ref[:, cs] = a, mult
            hcar[:, cs] = hs[R - 8:, :]
            hist_l[:, cs] = vl[R - HIST:, :]
            vg = p_ref[:, 2 * D + h * HEAD_DIM:2 * D + (h + 1) * HEAD_DIM].astype(F32)
            ge, th = _gelu_tanh(vg)
            ge_ref[:, cs], dge_ref[:, cs] = ge.astype(BF16), _gelu_tanh_grad(vg, th).astype(BF16)
            gl = p_ref[:, 4 * D + h * HEAD_DIM:4 * D + (h + 1) * HEAD_DIM].astype(F32)
            m_ref[:, cs] = (mtmp[:, cs] + _sigmoid(gl) * (hs * ge)).astype(BF16)

    def tile(i):
        return (i + nt - 1) % nt

    full = lambda shape: pl.BlockSpec(shape, lambda i: (0,) * len(shape))
    outs, staged = _staged_call(
        body, grid=(nt,),
        in_specs=[pl.BlockSpec((R, 5 * D), lambda i: (tile(i), 0)), full(pool_w.shape), full(gate_a.shape), full(gate_x.shape),
                  full(vecs.shape)],
        out_specs=[pl.BlockSpec((R, D), lambda i: (tile(i), 0))] * 9,
        out_shape=[jax.ShapeDtypeStruct((Tp, D), dt) for dt in (BF16, F32, BF16, BF16, BF16, F32, F32, BF16, BF16)],
        scratch_shapes=[pltpu.VMEM((HIST, D), F32), pltpu.VMEM((HIST, D), F32), pltpu.VMEM((8, D), F32), pltpu.VMEM((R, D), F32)],
        name=name, semantics=("arbitrary",), inputs=[proj, pool_w, gate_a, gate_x, vecs], stages=stages)
    outs = [outs[0], outs[1], outs[2:]]
    return (outs, staged) if stages else outs


def mix_bwd(proj, hs, saved, dmerged, pool_w, gate_a, gate_x, vecs, *, name, stages=()):
    Tp = proj.shape[0]
    D = proj.shape[1] // 5
    R = ROW_TILE
    nt = Tp // R
    H = D // HEAD_DIM
    PG = D // len(POOL_WINDOWS)

    def body(p_ref, pprev_ref, hs_ref, hprev_ref, dm_ref, xc_ref, r_ref, ig_ref, a_ref, mu_ref, ge_ref, dge_ref, pw_ref, wa_ref, wx_ref, vec_ref,
             dp_ref, dpw_ref, dwa_ref, dwx_ref, dvec_ref, car_g, fut_dxc, fut_q):
        i = pl.program_id(0)
        is_meta = i == nt - 1

        @pl.when(i == 0)
        def _():
            dpw_ref[...] = jnp.zeros_like(dpw_ref)
            dwa_ref[...] = jnp.zeros_like(dwa_ref)
            dwx_ref[...] = jnp.zeros_like(dwx_ref)
            dvec_ref[...] = jnp.zeros_like(dvec_ref)
            car_g[...] = jnp.zeros_like(car_g)
            fut_dxc[...] = jnp.zeros_like(fut_dxc)
            fut_q[...] = jnp.zeros_like(fut_q)

        row, valid, t_log = _tile_masks(is_meta, R)
        keep_prev = jnp.logical_not(is_meta)

        def colsum(x):
            return jnp.sum(x, axis=0, keepdims=True)

        for g, w in enumerate(POOL_WINDOWS):
            cs = slice(g * PG, (g + 1) * PG)
            v = p_ref[:, g * PG:(g + 1) * PG].astype(F32)
            vprev = jnp.where(keep_prev, pprev_ref[:, g * PG:(g + 1) * PG].astype(F32), 0.0)
            inv_cnt = _window_count_inv(t_log, w)
            d = _window_sum_back(jnp.concatenate([vprev, v], axis=0), w) * inv_cnt - v
            d_bf = d.astype(BF16)
            y = jnp.dot(d_bf, pw_ref[g], preferred_element_type=F32)
            scale = vec_ref[0:1, cs]
            sg = _sigmoid(p_ref[:, 3 * D + g * PG:3 * D + (g + 1) * PG].astype(F32))
            dm = dm_ref[:, cs].astype(F32)
            dpo = dm * sg
            dp_ref[:, 3 * D + g * PG:3 * D + (g + 1) * PG] = (dm * (y * scale) * sg * (1.0 - sg)).astype(BF16)
            dvec_ref[0:1, cs] += colsum(dpo * y)
            dy = (dpo * scale).astype(BF16)
            dd = _dot_nt(dy, pw_ref[g])
            dpw_ref[g] += _dot_tn(d_bf, dy)
            q = dd * inv_cnt
            s, sh = jnp.concatenate([q, fut_q[:, cs]], axis=0), 1
            while sh < w:
                s = s + pltpu.roll(s, R + HIST - sh, 0)
                sh *= 2
            dp_ref[:, g * PG:(g + 1) * PG] = (s[:R, :] - dd).astype(BF16)
            fut_q[:, cs] = q[:HIST, :]

        for h in range(H):
            cs = slice(h * HEAD_DIM, (h + 1) * HEAD_DIM)
            pc = lambda blk: slice(blk * D + h * HEAD_DIM, blk * D + (h + 1) * HEAD_DIM)
            vl = p_ref[:, pc(1)].astype(F32)
            vlprev = jnp.where(keep_prev, pprev_ref[:, pc(1)].astype(F32), 0.0)
            x1, x2, x3 = _conv_shifts(jnp.concatenate([vlprev, vl], axis=0))
            lam = vec_ref[4:5, cs]
            sp = _softplus_neg(lam)
            xcb = xc_ref[:, cs]
            xc, r, ig = xcb.astype(F32), r_ref[:, cs].astype(F32), ig_ref[:, cs].astype(F32)
            a, mult = a_ref[:, cs], mu_ref[:, cs]
            a2 = a * a
            inv_mult = 1.0 / mult
            hsv = hs_ref[:, cs]
            hprev = jnp.where(row >= 1, pltpu.roll(hsv, 1, 0), hprev_ref[HIST - 1:HIST, cs])
            ge = ge_ref[:, cs].astype(F32)
            sgl = _sigmoid(p_ref[:, pc(4)].astype(F32))
            dm = dm_ref[:, cs].astype(F32)
            dlo = dm * sgl
            dp_ref[:, pc(4)] = (dm * (hsv * ge) * sgl * (1.0 - sgl)).astype(BF16)
            dp_ref[:, pc(2)] = (dlo * hsv * dge_ref[:, cs].astype(F32)).astype(BF16)
            a_next = jnp.where(row < R - 1, pltpu.roll(a, R - 1, 0), 1.0)
            G = _scan_rows(a_next, dlo * ge, car_g[0:1, cs], reverse=True)
            car_g[:, cs] = (a * G)[0:8, :]
            da = jnp.where(valid, G * hprev, 0.0)
            db = jnp.where(valid, G, 0.0)
            dmult = db * (ig * xc)
            dig = db * (mult * xc)
            dxc = db * (mult * ig)
            dlog_a = da * a - dmult * (a2 * inv_mult)
            dvec_ref[4:5, cs] += colsum(dlog_a * r) * (-LRU_C)
            dr = dlog_a * ((-LRU_C) * sp)
            dpa = dr * r * (1.0 - r)
            dpx = dig * ig * (1.0 - ig)
            dpa_bf = dpa.astype(BF16)
            dpx_bf = dpx.astype(BF16)
            dwa_ref[h] += _dot_tn(xcb, dpa_bf)
            dwx_ref[h] += _dot_tn(xcb, dpx_bf)
            dvec_ref[2:3, cs] += colsum(dpa)
            dvec_ref[3:4, cs] += colsum(dpx)
            dxc = dxc + _dot_nt(dpa_bf, wa_ref[h]) + _dot_nt(dpx_bf, wx_ref[h])
            ext = jnp.concatenate([dxc, fut_dxc[:, cs]], axis=0)
            n = R + HIST
            dvl = (vec_ref[8:9, cs] * dxc + vec_ref[7:8, cs] * pltpu.roll(ext, n - 1, 0)[:R, :]
                   + vec_ref[6:7, cs] * pltpu.roll(ext, n - 2, 0)[:R, :] + vec_ref[5:6, cs] * pltpu.roll(ext, n - 3, 0)[:R, :])
            dp_ref[:, pc(1)] = dvl.astype(BF16)
            dvec_ref[1:2, cs] += colsum(dxc)
            dvec_ref[8:9, cs] += colsum(dxc * vl)
            dvec_ref[7:8, cs] += colsum(dxc * x1)
            dvec_ref[6:7, cs] += colsum(dxc * x2)
            dvec_ref[5:6, cs] += colsum(dxc * x3)
            fut_dxc[:, cs] = dxc[:HIST, :]

            @pl.when(is_meta)
            def _():
                dvec_ref[4:5, cs] = dvec_ref[4:5, cs] * (-_sigmoid(-lam))

    def tile(i):
        return (2 * nt - 2 - i) % nt

    def prev_blk(i):
        per = R // HIST
        return jnp.where(i == nt - 1, 0, jnp.where(i == nt - 2, Tp // HIST - 1, (nt - 2 - i) * per - 1))

    full = lambda shape: pl.BlockSpec(shape, lambda i: (0,) * len(shape))
    G_ = len(POOL_WINDOWS)
    outs, staged = _staged_call(
        body, grid=(nt,),
        in_specs=[pl.BlockSpec((R, 5 * D), lambda i: (tile(i), 0)), pl.BlockSpec((HIST, 5 * D), lambda i: (prev_blk(i), 0)),
                  pl.BlockSpec((R, D), lambda i: (tile(i), 0)), pl.BlockSpec((HIST, D), lambda i: (prev_blk(i), 0)),
                  *[pl.BlockSpec((R, D), lambda i: (tile(i), 0))] * 8,
                  full(pool_w.shape), full(gate_a.shape), full(gate_x.shape), full(vecs.shape)],
        out_specs=[pl.BlockSpec((R, 5 * D), lambda i: (tile(i), 0)), full((G_, PG, PG)), full((H, HEAD_DIM, HEAD_DIM)),
                   full((H, HEAD_DIM, HEAD_DIM)), full((16, D))],
        out_shape=[jax.ShapeDtypeStruct((Tp, 5 * D), BF16), jax.ShapeDtypeStruct((G_, PG, PG), F32),
                   jax.ShapeDtypeStruct((H, HEAD_DIM, HEAD_DIM), F32), jax.ShapeDtypeStruct((H, HEAD_DIM, HEAD_DIM), F32),
                   jax.ShapeDtypeStruct((16, D), F32)],
        scratch_shapes=[pltpu.VMEM((8, D), F32), pltpu.VMEM((HIST, D), F32), pltpu.VMEM((HIST, D), F32)],
        name=name, semantics=("arbitrary",), inputs=[proj, proj, hs, hs, dmerged, *saved, pool_w, gate_a, gate_x, vecs], stages=stages)
    return (outs, staged) if stages else outs


def _adamw_math(w, g, m, v):
    mn = ADAM_B1 * m + (1.0 - ADAM_B1) * g
    vn = ADAM_B2 * v + (1.0 - ADAM_B2) * (g * g)
    m_hat = mn / (1.0 - ADAM_B1 ** ADAM_STEP)
    v_hat = vn / (1.0 - ADAM_B2 ** ADAM_STEP)
    return -ADAM_LR * (m_hat / (jnp.sqrt(v_hat) + ADAM_EPS) + ADAM_WD * w), mn, vn


def adamw(w, g, m, v, *, name):
    rows, cols = w.shape
    tr = _pick(rows, (256, 128, 64, 32, 16, 8))

    def body(w_ref, g_ref, m_ref, v_ref, go_ref, d_ref, mo_ref, vo_ref):
        gv = g_ref[...]
        go_ref[...] = gv
        d_ref[...], mo_ref[...], vo_ref[...] = _adamw_math(w_ref[...], gv, m_ref[...], v_ref[...])

    blk = pl.BlockSpec((tr, cols), lambda i: (i, 0))
    sds = jax.ShapeDtypeStruct((rows, cols), F32)
    return pl.pallas_call(body, grid=(rows // tr,), in_specs=[blk] * 4, out_specs=[blk] * 4, out_shape=[sds] * 4, name=name,
                          compiler_params=_params("parallel"))(w, g, m, v)


def allgather8(block, *, name, reduce_sum=False, stages=()):
    rows, cols = block.shape

    def body(x_ref, out_ref, *scratch):
        if reduce_sum:
            buf, send_sems, recv_sems, local_sem = scratch
        else:
            buf = out_ref
            send_sems, recv_sems, local_sem = scratch
        x, y, c, chips = _place()
        me, sibling = (x, y, c), (x, y, 1 - c)

        def slot(px, py, pc):
            return buf.at[4 * px + 2 * py + pc]

        def copy(k, blk, to, src=None):
            return pltpu.make_async_remote_copy(src_ref=slot(*blk) if src is None else src, dst_ref=slot(*blk),
                                                send_sem=send_sems.at[k], recv_sem=recv_sems.at[k], device_id=to, device_id_type=MESH)

        mine = pltpu.make_async_copy(x_ref, slot(*me), local_sem)
        mine.start()
        first = [copy(0, me, sibling, src=x_ref)]
        first += [copy(1 + j, me, (*chip, c), src=x_ref) for j, chip in enumerate(chips)]
        for cp in first:
            cp.start()
        passed = [copy(4 + j, (*chip, c), sibling) for j, chip in enumerate(chips)]
        for j, chip in enumerate(chips):
            copy(1 + j, (*chip, c), me).wait_recv()
            passed[j].start()
        copy(0, sibling, me).wait_recv()
        for j, chip in enumerate(chips):
            copy(4 + j, (*chip, 1 - c), me).wait_recv()
        for cp in first + passed:
            cp.wait_send()
        mine.wait()
        if reduce_sum:
            acc = buf[0]
            for d in range(1, 8):
                acc = acc + buf[d]
            out_ref[...] = acc

    sems = [pltpu.SemaphoreType.DMA((7,)), pltpu.SemaphoreType.DMA((7,)), pltpu.SemaphoreType.DMA]
    if reduce_sum:
        out_shape = jax.ShapeDtypeStruct((rows, cols), block.dtype)
        scratch = [pltpu.VMEM((8, rows, cols), block.dtype)] + sems
    else:
        out_shape = jax.ShapeDtypeStruct((8, rows, cols), block.dtype)
        scratch = sems
    vmem = pl.BlockSpec(memory_space=pltpu.VMEM)
    outs, staged = _staged_call(body, grid=(1,), in_specs=[vmem], out_specs=[vmem], out_shape=[out_shape], scratch_shapes=scratch,
                                name=name, semantics=("arbitrary",), inputs=[block], stages=stages)
    return (outs[0], staged) if stages else outs[0]


def allgather_chips(block, *, name):
    rows, cols = block.shape

    def body(x_ref, out_ref, send_sems, recv_sems, local_sem):
        x, y, c, chips = _place()
        mine = pltpu.make_async_copy(x_ref, out_ref.at[2 * x + y], local_sem)
        mine.start()
        sends = [_remote(x_ref, out_ref.at[2 * x + y], send_sems.at[j], recv_sems.at[j], (px, py, c)) for j, (px, py) in enumerate(chips)]
        for cp in sends:
            cp.start()
        for j, (px, py) in enumerate(chips):
            _remote(x_ref, out_ref.at[2 * px + py], send_sems.at[j], recv_sems.at[j], (px, py, c)).wait_recv()
        for cp in sends:
            cp.wait_send()
        mine.wait()

    vmem = pl.BlockSpec(memory_space=pltpu.VMEM)
    return pl.pallas_call(body, out_shape=jax.ShapeDtypeStruct((4, rows, cols), block.dtype), in_specs=[vmem], out_specs=vmem,
                          scratch_shapes=[pltpu.SemaphoreType.DMA((3,)), pltpu.SemaphoreType.DMA((3,)), pltpu.SemaphoreType.DMA],
                          name=name)(block)


def cast_into_slot(w, chip_arr, *, name):
    _, r, cols = w.shape
    tr = _pick(r, (256, 128, 64, 32, 16))

    def body(chip_ref, w_ref, o_ref):
        del chip_ref
        o_ref[...] = w_ref[...].astype(BF16)

    grid_spec = pltpu.PrefetchScalarGridSpec(
        num_scalar_prefetch=1, grid=(2, r // tr),
        in_specs=[pl.BlockSpec((None, tr, cols), lambda h, i, chip: (h, i, 0))],
        out_specs=pl.BlockSpec((None, None, tr, cols), lambda h, i, chip: (chip[0], h, i, 0)))
    return pl.pallas_call(body, grid_spec=grid_spec, out_shape=jax.ShapeDtypeStruct((4, 2, r, cols), BF16), name=name,
                          compiler_params=_params("parallel", "parallel"))(chip_arr, w)


def proj_with_gather(u, bufs, order_arr, *, head_start, name):
    Tp, K = u.shape
    n = len(bufs)
    Ns = bufs[0].shape[3]
    tm = _pick(Tp, (1408, 512, 256, 128))
    tc = _pick(Ns, (512, 256, 128))
    n_rows = Tp // tm

    def body(order_ref, u_ref, *refs):
        del order_ref
        o_ref, outs = refs[n], refs[n + 1:2 * n + 1]
        wbuf, dir_send, dir_recv, rel_send, rel_recv, d2d_send, d2d_recv, load_sems = refs[2 * n + 1:]
        s, i = pl.program_id(0), pl.program_id(1)
        x, y, c, chips = _place()
        me = 2 * x + y
        (dx, dy) = chips[2]

        def direct(t, j, landing):
            px, py = chips[j]
            chip_idx = 2 * px + py if landing else me
            slot = _relay_rows(outs[t], chip_idx, c, 1) if (t == 0 and head_start) else outs[t].at[chip_idx, c]
            return _remote(slot, slot, dir_send.at[t, j], dir_recv.at[t, j], (px, py, c))

        def relay(t, j, landing):
            px, py = chips[j]
            ox, oy = chips[1 - j]
            rows = _relay_rows(outs[t], 2 * dx + dy, c, j) if landing else _relay_rows(outs[t], 2 * ox + oy, c, j)
            return _remote(rows, rows, rel_send.at[t, j], rel_recv.at[t, j], (px, py, c))

        def d2d(t, j, landing):
            px, py = chips[j]
            slot = outs[t].at[2 * px + py, 1 - c] if landing else outs[t].at[2 * px + py, c]
            return _remote(slot, slot, d2d_send.at[t, j], d2d_recv.at[t, j], (x, y, 1 - c))

        def load(chip_idx, slot, start=True, wait=True):
            parts = [pltpu.make_async_copy(outs[0].at[chip_idx, hh], wbuf.at[slot, pl.ds(hh * (K // 2), K // 2), :],
                                           load_sems.at[slot, hh]) for hh in range(2)]
            for cp in parts:
                if start:
                    cp.start()
            for cp in parts:
                if wait:
                    cp.wait()

        mid, last = min(1, n_rows - 1), n_rows - 1

        def chip_of(j):
            px, py = chips[j]
            return 2 * px + py

        @pl.when(jnp.logical_and(s == 0, i == 0))
        def _():
            for t in range(n):
                for j in range(2):
                    direct(t, j, False).start()
            if head_start:
                relay(0, 0, False).start()
            load(me, 0)

        @pl.when(jnp.logical_and(s == 1, i == 0))
        def _():
            for j in range(2):
                direct(0, j, True).wait_recv()
            for j in range(2):
                if not (head_start and j == 0):
                    relay(0, j, False).start()
                d2d(0, j, False).start()
            d2d(0, 0, True).wait_recv()
            load(chip_of(0), 1)

        @pl.when(jnp.logical_and(s == 1, i == mid))
        def _():
            d2d(0, 1, True).wait_recv()
            load(chip_of(1), 0, wait=False)

        @pl.when(jnp.logical_and(s == 2, i == 0))
        def _():
            load(chip_of(1), 0, start=False)
            for t in range(1, n):
                for j in range(2):
                    direct(t, j, True).wait_recv()
                for j in range(2):
                    relay(t, j, False).start()
                    d2d(t, j, False).start()

        @pl.when(jnp.logical_and(s == 2, i == mid))
        def _():
            for j in range(2):
                relay(0, j, True).wait_recv()
            d2d(0, 2, False).start()

        @pl.when(jnp.logical_and(s == 2, i == last))
        def _():
            d2d(0, 2, True).wait_recv()
            load(chip_of(2), 1, wait=False)

        @pl.when(jnp.logical_and(s == 3, i == 0))
        def _():
            load(chip_of(2), 1, start=False)

        @pl.when(jnp.logical_and(s == 3, i == min(1, n_rows - 1)))
        def _():
            for t in range(1, n):
                for j in range(2):
                    relay(t, j, True).wait_recv()
                d2d(t, 2, False).start()

        uv = u_ref[...]
        for cc in range(Ns // tc):
            o_ref[:, cc * tc:(cc + 1) * tc] = jnp.dot(uv, wbuf[s % 2, :, cc * tc:(cc + 1) * tc], preferred_element_type=F32).astype(BF16)

        @pl.when(jnp.logical_and(s == 3, i == n_rows - 1))
        def _():
            for t in range(1, n):
                for j in range(3):
                    d2d(t, j, True).wait_recv()
            for t in range(n):
                for j in range(2):
                    direct(t, j, False).wait_send()
                    relay(t, j, False).wait_send()
                for j in range(3):
                    d2d(t, j, False).wait_send()

    grid_spec = pltpu.PrefetchScalarGridSpec(
        num_scalar_prefetch=1, grid=(4, n_rows),
        in_specs=[pl.BlockSpec((tm, K), lambda s, i, order: (i, 0)), *_any_specs(n)],
        out_specs=[pl.BlockSpec((tm, Ns), lambda s, i, order: (i, order[s])), *_any_specs(n)],
        scratch_shapes=[pltpu.VMEM((2, K, Ns), BF16), pltpu.SemaphoreType.DMA((n, 2)), pltpu.SemaphoreType.DMA((n, 2)),
                        pltpu.SemaphoreType.DMA((n, 2)), pltpu.SemaphoreType.DMA((n, 2)),
                        pltpu.SemaphoreType.DMA((n, 3)), pltpu.SemaphoreType.DMA((n, 3)), pltpu.SemaphoreType.DMA((2, 2))])
    res = pl.pallas_call(body, grid_spec=grid_spec, out_shape=[jax.ShapeDtypeStruct((Tp, 4 * Ns), BF16), *[_sds(b) for b in bufs]],
                         input_output_aliases={2 + t: 1 + t for t in range(n)}, name=name,
                         compiler_params=_params("arbitrary", "arbitrary"))(order_arr, u, *bufs)
    return res[0], list(res[1:])


def chip_presum(grad, recv, c_arr, *, name):
    _, _, r, cols = grad.shape
    tr = _pick(r, (256, 128, 64, 32, 16))

    def body(c_ref, g_ref, r_ref, o_ref):
        del c_ref
        o_ref[...] = (g_ref[...].astype(F32) + r_ref[...].astype(F32)).astype(BF16)

    grid_spec = pltpu.PrefetchScalarGridSpec(
        num_scalar_prefetch=1, grid=(4, r // tr),
        in_specs=[pl.BlockSpec((None, None, tr, cols), lambda k, i, c_ref: (k, c_ref[0], i, 0)),
                  pl.BlockSpec((None, tr, cols), lambda k, i, c_ref: (k, i, 0))],
        out_specs=pl.BlockSpec((None, tr, cols), lambda k, i, c_ref: (k, i, 0)))
    return pl.pallas_call(body, grid_spec=grid_spec, out_shape=jax.ShapeDtypeStruct((4, r, cols), BF16), name=name,
                          compiler_params=_params("parallel", "parallel"))(c_arr, grad, recv)


def final_half(grad, recv, got, mc_arr, *, name):
    _, _, r, cols = grad.shape
    tr = _pick(r, (256, 128, 64, 32, 16))

    def body(mc_ref, g_ref, r_ref, q_ref, o_ref):
        del mc_ref
        acc = g_ref[...].astype(F32) + r_ref[...].astype(F32)
        for j in range(3):
            acc = acc + q_ref[j].astype(F32)
        o_ref[...] = acc

    grid_spec = pltpu.PrefetchScalarGridSpec(
        num_scalar_prefetch=1, grid=(r // tr,),
        in_specs=[pl.BlockSpec((None, None, tr, cols), lambda i, mc: (mc[0], mc[1], i, 0)),
                  pl.BlockSpec((None, tr, cols), lambda i, mc: (mc[0], i, 0)),
                  pl.BlockSpec((3, tr, cols), lambda i, mc: (0, i, 0))],
        out_specs=pl.BlockSpec((tr, cols), lambda i, mc: (i, 0)))
    return pl.pallas_call(body, grid_spec=grid_spec, out_shape=jax.ShapeDtypeStruct((r, cols), F32), name=name,
                          compiler_params=_params("parallel"))(mc_arr, grad, recv, got)


def adamw_halves(w, mine, theirs, m, v, c_arr, *, name):
    _, r, cols = w.shape
    tr = _pick(r, (256, 128, 64, 32, 16, 8))

    def body(c_ref, w_ref, mine_ref, theirs_ref, m_ref, v_ref, go_ref, d_ref, mo_ref, vo_ref):
        gv = jnp.where(pl.program_id(0) == c_ref[0], mine_ref[...], theirs_ref[...])
        go_ref[...] = gv
        d_ref[...], mo_ref[...], vo_ref[...] = _adamw_math(w_ref[...], gv, m_ref[...], v_ref[...])

    blk = pl.BlockSpec((None, tr, cols), lambda h, i, c_ref: (h, i, 0))
    grid_spec = pltpu.PrefetchScalarGridSpec(
        num_scalar_prefetch=1, grid=(2, r // tr),
        in_specs=[blk, pl.BlockSpec((tr, cols), lambda h, i, c_ref: (jnp.where(h == c_ref[0], i, 0), 0)),
                  pl.BlockSpec((tr, cols), lambda h, i, c_ref: (jnp.where(h == c_ref[0], 0, i), 0)), blk, blk],
        out_specs=[blk] * 4)
    sds = jax.ShapeDtypeStruct((2, r, cols), F32)
    return pl.pallas_call(body, grid_spec=grid_spec, out_shape=[sds] * 4, name=name,
                          compiler_params=_params("parallel", "parallel"))(c_arr, w, mine, theirs, m, v)


def _pad_rows(a, rows):
    return jnp.pad(a, ((0, rows - a.shape[0]), (0, 0)))


def kernel(x, meta_tokens, norm1_g, w_in, pool_w, pool_scale, conv_w, conv_b, gate_a_w, gate_a_b, gate_x_w, gate_x_b, lru_lambda, w_out, norm2_g, mlp_w1, mlp_w2, final_g, loss_target, m_meta_tokens, m_norm1_g, m_w_in, m_pool_w, m_pool_scale, m_conv_w, m_conv_b, m_gate_a_w, m_gate_a_b, m_gate_x_w, m_gate_x_b, m_lru_lambda, m_w_out, m_norm2_g, m_mlp_w1, m_mlp_w2, m_final_g, v_meta_tokens, v_norm1_g, v_w_in, v_pool_w, v_pool_scale, v_conv_w, v_conv_b, v_gate_a_w, v_gate_a_b, v_gate_x_w, v_gate_x_b, v_lru_lambda, v_w_out, v_norm2_g, v_mlp_w1, v_mlp_w2, v_final_g):
    D = x.shape[-1]
    H = D // HEAD_DIM
    G = len(POOL_WINDOWS)
    PG = D // G
    ax, ay, ac = lax.axis_index("x"), lax.axis_index("y"), lax.axis_index("c")
    chip = 2 * ax + ay
    dshard = D // 4

    c_arr = jnp.reshape(ac, (1,)).astype(jnp.int32)
    chip_arr = jnp.reshape(chip, (1,)).astype(jnp.int32)
    mc_arr = jnp.stack([chip, ac]).astype(jnp.int32)
    names = ["w_in", "pool_w", "gate_a_w", "gate_x_w", "w_out", "mlp_w1", "mlp_w2"]
    big = [w_in, pool_w, gate_a_w, gate_x_w, w_out, mlp_w1, mlp_w2]

    def halves(w):
        w2d = w.reshape(-1, w.shape[-1])
        return w2d.reshape(2, w2d.shape[0] // 2, w2d.shape[1])

    bufs = [cast_into_slot(halves(w), chip_arr, name="cast_" + nm) for w, nm in zip(big, names)]
    order_arr = jnp.stack([chip, 2 * (1 - ax) + ay, 2 * ax + (1 - ay), 2 * (1 - ax) + (1 - ay)]).astype(jnp.int32)

    small_in = jnp.concatenate([meta_tokens, _pad_rows(conv_w[0], 8), _pad_rows(gate_a_b.reshape(1, dshard), 8),
                                _pad_rows(gate_x_b.reshape(1, dshard), 8)], axis=0)
    sm = allgather_chips(small_in, name="gather_small")
    meta_f = sm[:, 0:16].transpose(1, 0, 2).reshape(N_META, D)
    conv_w_f = sm[:, 16:20].transpose(1, 0, 2).reshape(4, D)
    hd4 = HEAD_DIM // 4
    ba_f = sm[:, 24].reshape(4, H, hd4).transpose(1, 0, 2).reshape(1, D)
    bx_f = sm[:, 32].reshape(4, H, hd4).transpose(1, 0, 2).reshape(1, D)
    vecs = jnp.zeros((16, D), F32)
    for r0, part in ((0, pool_scale), (1, conv_b), (2, ba_f), (3, bx_f), (4, lru_lambda), (5, conv_w_f)):
        vecs = lax.dynamic_update_slice(vecs, part, (r0, 0))

    def chipwise(g, n_blocks, rows):
        return g.reshape(n_blocks, 4, rows, g.shape[-1]).transpose(1, 0, 2, 3).reshape(4, n_blocks * rows, g.shape[-1])

    def split2(g):
        return g.reshape(4, 2, g.shape[1] // 2, g.shape[2])

    def presum(t, g, r):
        return chip_presum(g, r, c_arr, name="presum_" + names[t])

    def total(t, g, r, q):
        return final_half(g, r, q, mc_arr, name="sum_" + names[t])

    xs, target, gfin = x[0], loss_target[0], final_g.reshape(1, D)
    meta_tile = jnp.concatenate([jnp.zeros((ROW_TILE - N_META, D), F32), meta_f], axis=0)
    (h0, u, u_mine, u_theirs), [[b_win]] = rmsnorm_fwd_input(xs, meta_tile, norm1_g, name="norm1",
                                                stages=[stage_gather_direct([bufs[0]], quarter=0)])
    proj, (b_win, b_pool, b_ga, b_gx, b_wout) = proj_with_gather(u, [b_win, *bufs[1:5]], order_arr, head_start=True, name="proj")
    w_in_f = b_win.reshape(4, w_in.shape[1], w_in.shape[2])
    pool_f = b_pool.reshape(4, G, PG // 4, PG).transpose(1, 0, 2, 3).reshape(G, PG, PG)
    ga_f = b_ga.reshape(4, H, HEAD_DIM // 4, HEAD_DIM).transpose(1, 0, 2, 3).reshape(H, HEAD_DIM, HEAD_DIM)
    gx_f = b_gx.reshape(4, H, HEAD_DIM // 4, HEAD_DIM).transpose(1, 0, 2, 3).reshape(H, HEAD_DIM, HEAD_DIM)
    w_out_f = b_wout.reshape(4 * w_out.shape[1], w_out.shape[2])
    (merged, hs, saved), [[b_w1, b_w2]] = mix_fwd(proj, pool_f, ga_f, gx_f, vecs, name="mix_fwd",
                                                  stages=[stage_gather_direct([bufs[5], bufs[6]])])
    h1, [[b_w1]] = mm_nn(merged, w_out_f, out_dtype=F32, name="out_proj", epilogue=lambda r, res: r + res, extras=(h0,),
                         stages=[stage_both(stage_gather_relay([b_w1]), stage_gather_d2d([b_w1], peers=(0, 1)))])
    (u2, u2_t), [[b_w1]] = rmsnorm_fwd(h1, norm2_g, name="norm2", stages=[stage_gather_d2d([b_w1], peers=(2,))])
    w1_f = b_w1.reshape(4, mlp_w1.shape[1], mlp_w1.shape[2])
    a1, [[b_w2]] = mm_nn(u2, w1_f, out_dtype=BF16, name="mlp_up", tiles=(None, min(1024, D), None),
                         stages=[stage_both(stage_gather_relay([b_w2]), stage_gather_d2d([b_w2], peers=(0, 1)))])
    [[b_w2]] = comm_call([stage_gather_d2d([b_w2], peers=(2,))], name="w2_to_sibling")
    w2_f = b_w2.reshape(4 * mlp_w2.shape[1], mlp_w2.shape[2])
    Tp = h0.shape[0]
    h2 = mm_nn(a1, w2_f, out_dtype=F32, name="mlp_down", a_pro=_relu_sq, epilogue=lambda r, res: r + res, extras=(h1,),
               tiles=(_pick(Tp, (704, 512, 256, 128)), min(256, D), a1.shape[1]))
    dh2_bf, st_f = final_loss(h2, target, gfin, name="final_loss")

    da1 = mm_nt(dh2_bf, w2_f, out_dtype=BF16, name="mlp_down_dx",
                epilogue=lambda r, a: r * (2.0 * jnp.maximum(a.astype(F32), 0.0)), extras=(a1,))
    d_w2 = mm_tn(a1, dh2_bf, shards=1, out_dtype=BF16, name="mlp_down_dw", a_pro=_relu_sq, tiles=(512, min(1024, D), Tp))
    g6 = split2(d_w2.reshape(4, -1, d_w2.shape[-1]))
    d_w1, [[r6]] = mm_nn(u2_t, da1, out_dtype=BF16, name="mlp_up_dw", out_shards=4, tiles=(min(1024, D), min(1024, D), Tp),
                         stages=[stage_to_sibling([g6])])
    g5 = split2(d_w1)
    p6 = presum(6, g6, r6)
    du2, [[q6_near], [r5]] = mm_nt(da1, w1_f, out_dtype=BF16, name="mlp_up_dx",
                                  stages=[stage_to_chips([p6], peers=(0, 1)), stage_to_sibling([g5])])
    p5 = presum(5, g5, r5)
    dh1_bf, st_2 = rms_bwd(h1, norm2_g, du2, dh2_bf, name="norm2_bwd")
    dmerged = mm_nt(dh1_bf, w_out_f, out_dtype=BF16, name="out_proj_dx")
    d_wout = mm_tn(merged, dh1_bf, shards=1, out_dtype=BF16, name="out_proj_dw", tiles=(512, min(1024, D), Tp))
    g4 = split2(d_wout.reshape(4, -1, d_wout.shape[-1]))
    (dproj, d_pool, d_ga, d_gx, d_vecs), [[q6], [q5], [r4]] = mix_bwd(
        proj, hs, saved, dmerged, pool_f, ga_f, gx_f, vecs, name="mix_bwd",
        stages=[stage_to_chips([p6], peers=(2,), into=[q6_near]), stage_to_chips([p5]), stage_to_sibling([g4])])
    p4 = presum(4, g4, r4)
    f5, f6 = total(5, g5, r5, q5), total(6, g6, r6, q6)
    g1, g2, g3 = split2(chipwise(d_pool, G, PG // 4)), split2(chipwise(d_ga, H, hd4)), split2(chipwise(d_gx, H, hd4))
    whole_k = (u_mine.shape[0], _pick(dproj.shape[1] // 4, (1280, 512, 256, 128)), Tp)
    g_theirs, [[r1, r2, r3], [o5, o6], [q4]] = mm_nn(u_theirs, dproj, out_dtype=BF16, out_shards=4, tiles=whole_k, name="proj_dw_sibling_rows",
                                                    stages=[stage_to_sibling([g1, g2, g3]), stage_from_sibling([f5, f6]), stage_to_chips([p4])])
    p1, p2, p3 = presum(1, g1, r1), presum(2, g2, r2), presum(3, g3, r3)
    g_mine, [[r0], [q1, q2, q3]] = mm_nn(u_mine, dproj, out_dtype=BF16, out_shards=4, tiles=whole_k, name="proj_dw_own_rows",
                                         stages=[stage_from_sibling([g_theirs]), stage_to_chips([p1, p2, p3])])
    g0 = g_mine[:, None]
    p0 = chip_presum(g0, r0, jnp.zeros((1,), jnp.int32), name="presum_w_in")
    f4 = total(4, g4, r4, q4)
    du, [[q0], [o4]] = mm_nt(dproj, w_in_f, out_dtype=BF16, name="proj_dx", tiles=(None, None, w_in_f.shape[2]),
                             stages=[stage_to_chips([p0]), stage_from_sibling([f4])])
    f0 = final_half(g0, r0, q0, jnp.stack([chip, 0]).astype(jnp.int32), name="sum_w_in")
    f1, f2, f3 = total(1, g1, r1, q1), total(2, g2, r2, q2), total(3, g3, r3, q3)
    grad_x, d_meta, st_1 = rms_bwd_input(h0, norm1_g, du, dh1_bf, name="norm1_bwd")

    small = jnp.concatenate([d_meta, d_vecs, st_1, st_2, st_f], axis=0)
    tot, [[o0, o1, o2, o3]] = allgather8(small, name="sum_small", reduce_sum=True, stages=[stage_from_sibling([f0, f1, f2, f3])])
    mine = [f0, f1, f2, f3, f4, f5, f6]
    theirs = [o0, o1, o2, o3, o4, o5, o6]
    loss = jnp.sum(tot[49])
    g_meta = lax.dynamic_slice_in_dim(tot[0:16], chip * dshard, dshard, axis=1)
    g_pool_scale, g_conv_b, g_lam = tot[16:17], tot[17:18], tot[20:21]
    g_ba = lax.dynamic_slice_in_dim(tot[18].reshape(H, HEAD_DIM), chip * hd4, hd4, axis=1)[None]
    g_bx = lax.dynamic_slice_in_dim(tot[19].reshape(H, HEAD_DIM), chip * hd4, hd4, axis=1)[None]
    g_conv_w = lax.dynamic_slice_in_dim(tot[21:25], chip * dshard, dshard, axis=1)[None]
    g_n1, g_n2, g_fin = tot[32:33], tot[40:41], tot[48]

    def step(w, g, m, v, nm):
        cols = w.shape[-1]
        outs = adamw(w.reshape(-1, cols), g.reshape(-1, cols), m.reshape(-1, cols), v.reshape(-1, cols), name="adamw_" + nm)
        return [o.reshape(w.shape) for o in outs]

    def step_big(t, w, m, v):
        outs = adamw_halves(halves(w), mine[t], theirs[t], halves(m), halves(v), c_arr, name="adamw_" + names[t])
        return [o.reshape(w.shape) for o in outs]

    res = dict(meta_tokens=step(meta_tokens, g_meta, m_meta_tokens, v_meta_tokens, "meta_tokens"),
               norm1_g=step(norm1_g, g_n1, m_norm1_g, v_norm1_g, "norm1_g"),
               w_in=step_big(0, w_in, m_w_in, v_w_in), pool_w=step_big(1, pool_w, m_pool_w, v_pool_w),
               pool_scale=step(pool_scale, g_pool_scale, m_pool_scale, v_pool_scale, "pool_scale"),
               conv_w=step(conv_w, g_conv_w, m_conv_w, v_conv_w, "conv_w"), conv_b=step(conv_b, g_conv_b, m_conv_b, v_conv_b, "conv_b"),
               gate_a_w=step_big(2, gate_a_w, m_gate_a_w, v_gate_a_w), gate_a_b=step(gate_a_b, g_ba, m_gate_a_b, v_gate_a_b, "gate_a_b"),
               gate_x_w=step_big(3, gate_x_w, m_gate_x_w, v_gate_x_w), gate_x_b=step(gate_x_b, g_bx, m_gate_x_b, v_gate_x_b, "gate_x_b"),
               lru_lambda=step(lru_lambda, g_lam, m_lru_lambda, v_lru_lambda, "lru_lambda"), w_out=step_big(4, w_out, m_w_out, v_w_out),
               norm2_g=step(norm2_g, g_n2, m_norm2_g, v_norm2_g, "norm2_g"), mlp_w1=step_big(5, mlp_w1, m_mlp_w1, v_mlp_w1),
               mlp_w2=step_big(6, mlp_w2, m_mlp_w2, v_mlp_w2), final_g=step(final_g, g_fin, m_final_g, v_final_g, "final_g"))
    order = list(res)
    return (loss, grad_x[None], *[res[n][0] for n in order], *[res[n][1] for n in order], *[res[n][2] for n in order],
            *[res[n][3] for n in order])
```
